```python
import math
import jax
import jax.numpy as jnp
from jax import lax
import numpy as np

D_MODEL = 1024
BATCH = 16
SEQ = 256
DEPTH = 2
DEC_BATCH = 8
DEC_SEQ = 1024
PAST_LEN = 256

GRID_W = 64
N_EVEN = (DEPTH + 1) // 2
N_ODD = DEPTH // 2

HEAD_DIM = 64
NAT_HEADS = 8
NAT_WIN_H = 8
NAT_WIN_W = 16
DIFF_HEADS = 4
NAT_WIDTH = NAT_HEADS * HEAD_DIM
DIFF_WIDTH = DIFF_HEADS * 2 * HEAD_DIM
EVEN_IN = 3 * NAT_WIDTH + 3 * DIFF_WIDTH
EVEN_OUT = NAT_WIDTH + DIFF_WIDTH
ROPE_THETA = 10000.0
Q_BLOCK = 128

DN_HEADS = 8
DN_DK = 128
DN_DV = 128
DN_QK_WIDTH = DN_HEADS * DN_DK
DN_WIDTH = DN_HEADS * DN_DV
DN_CONV_CH = 2 * DN_QK_WIDTH + DN_WIDTH
DN_CONV = 3
DN_CHUNK = 64
ODD_IN = DN_CONV_CH + 4 * DN_HEADS + DN_WIDTH

N_EXPERTS = 16
N_GROUPS = 4
EXPERTS_PER_GROUP = N_EXPERTS // N_GROUPS
TOP_K = 2
EXPERT_FF = 512

ALPHA = (2 * DEPTH) ** 0.25
BETA = (8 * DEPTH) ** -0.25
EPS = 1e-5

kernel_name = 'nat_diffattn_deltanet_moe_diffusion_step'

F32 = jnp.float32


def layer_norm(x, g, b):
    xf = x.astype(F32)
    xc = xf - xf.mean(-1, keepdims=True)
    var = (xc * xc).mean(-1, keepdims=True)
    return (xc * lax.rsqrt(var + EPS) * g + b).astype(x.dtype)


def rms_norm(x, g):
    xf = x.astype(F32)
    return (xf * lax.rsqrt((xf * xf).mean(-1, keepdims=True) + EPS) * g).astype(x.dtype)


def l2_normalize(x):
    return x * lax.rsqrt((x * x).sum(-1, keepdims=True) + 1e-6)


def ada_modulation(cond, w, b):
    mod = jax.nn.silu(cond) @ w + b
    mod = mod.reshape(mod.shape[:-1] + (6, D_MODEL))
    return [mod[..., j, :][..., None, :] for j in range(6)]


def modulate(x, shift, scale):
    return x * (1 + scale) + shift


def axial_rope(x):
    length, dh = x.shape[1], x.shape[-1]
    nf = dh // 4
    t = jnp.arange(length)
    inv = ROPE_THETA ** (-jnp.arange(nf, dtype=F32) / nf)
    ang_r = (t // GRID_W).astype(F32)[:, None] * inv
    ang_c = (t % GRID_W).astype(F32)[:, None] * inv
    bshape = (length,) + (1,) * (x.ndim - 3) + (nf,)
    cr = jnp.cos(ang_r).reshape(bshape).astype(x.dtype)
    sr = jnp.sin(ang_r).reshape(bshape).astype(x.dtype)
    cc = jnp.cos(ang_c).reshape(bshape).astype(x.dtype)
    sc = jnp.sin(ang_c).reshape(bshape).astype(x.dtype)
    x1, x2, x3, x4 = jnp.split(x, 4, axis=-1)
    return jnp.concatenate([x1 * cr - x2 * sr, x2 * cr + x1 * sr,
                            x3 * cc - x4 * sc, x4 * cc + x3 * sc], axis=-1)


def sweep_query_blocks(fn, q):
    b, length = q.shape[:2]
    nb = length // Q_BLOCK
    qb = jnp.moveaxis(q.reshape((b, nb, Q_BLOCK) + q.shape[2:]), 1, 0)
    out = lax.map(fn, qb)
    return jnp.moveaxis(out, 0, 1).reshape((b, length) + out.shape[3:])


def softmax_attend(q, k, v):
    scale = q.shape[-1] ** -0.5

    def block(qb):
        s = jnp.einsum('bqhd,bkhd->bhqk', qb, k).astype(F32) * scale
        p = jax.nn.softmax(s, axis=-1).astype(v.dtype)
        return jnp.einsum('bhqk,bkhd->bqhd', p, v)

    return sweep_query_blocks(block, q)


def diff_attend(q, k, v, lam):
    scale = q.shape[-1] ** -0.5

    def block(qb):
        s = jnp.einsum('bqhjd,bkhjd->bhjqk', qb, k).astype(F32) * scale
        p = jax.nn.softmax(s, axis=-1)
        p = (p[:, :, 0] - lam * p[:, :, 1]).astype(v.dtype)
        return jnp.einsum('bhqk,bkhe->bqhe', p, v)

    return sweep_query_blocks(block, q)


def diff_lambda(lam_p, layer):
    lam_init = 0.8 - 0.6 * math.exp(-0.3 * layer)
    lp = lam_p.astype(F32)
    lam = jnp.exp(jnp.sum(lp[0] * lp[1])) - jnp.exp(jnp.sum(lp[2] * lp[3])) + lam_init
    return lam, lam_init


def nat_latent(q, k, v, ctx_k, ctx_v, rel_bias):
    b, length, heads, dh = q.shape
    rows = length // GRID_W
    kh = min(NAT_WIN_H, rows)
    kw = NAT_WIN_W
    scale = dh ** -0.5
    qg = q.reshape(b, rows, GRID_W, heads, dh)
    kg = k.reshape(b, rows, GRID_W, heads, dh)
    vg = v.reshape(b, rows, GRID_W, heads, dh)
    col = jnp.arange(GRID_W)
    col_start = jnp.clip(col - kw // 2, 0, GRID_W - kw)
    col_ok = (col[None, :] >= col_start[:, None]) & (col[None, :] < col_start[:, None] + kw)
    dc = jnp.clip(col[None, :] - col[:, None], 1 - kw, kw - 1) + (NAT_WIN_W - 1)

    def row_block(r):
        start = jnp.clip(r - kh // 2, 0, rows - kh)
        qr = lax.dynamic_index_in_dim(qg, r, axis=1, keepdims=False)
        kr = lax.dynamic_slice_in_dim(kg, start, kh, axis=1)
        vr = lax.dynamic_slice_in_dim(vg, start, kh, axis=1)
        dr = start + jnp.arange(kh) - r + (NAT_WIN_H - 1)
        bias = rel_bias[:, dr[:, None, None], dc[None, :, :]].astype(F32)
        s_loc = (jnp.einsum('bqhd,bawhd->bhqaw', qr, kr).astype(F32) * scale
                 + jnp.transpose(bias, (0, 2, 1, 3))[None])
        s_loc = jnp.where(col_ok[:, None, :], s_loc, -jnp.inf).reshape(b, heads, GRID_W, kh * GRID_W)
        s_ctx = jnp.einsum('bqhd,bkhd->bhqk', qr, ctx_k).astype(F32) * scale
        p = jax.nn.softmax(jnp.concatenate([s_loc, s_ctx], axis=-1), axis=-1).astype(v.dtype)
        p_loc = p[..., :kh * GRID_W].reshape(b, heads, GRID_W, kh, GRID_W)
        p_ctx = p[..., kh * GRID_W:]
        return (jnp.einsum('bhqaw,bawhd->bqhd', p_loc, vr)
                + jnp.einsum('bhqk,bkhd->bqhd', p_ctx, ctx_v))

    out = lax.map(row_block, jnp.arange(rows))
    return jnp.moveaxis(out, 0, 1).reshape(b, length, heads, dh)


def even_project(h, w_in):
    b, length, _ = h.shape
    splits = [NAT_WIDTH, 2 * NAT_WIDTH, 3 * NAT_WIDTH,
              3 * NAT_WIDTH + DIFF_WIDTH, 3 * NAT_WIDTH + 2 * DIFF_WIDTH]
    nq, nk, nv, dq, dk, dv = jnp.split(h @ w_in, splits, axis=-1)
    nat = tuple(t.reshape(b, length, NAT_HEADS, HEAD_DIM) for t in (nq, nk, nv))
    dq = dq.reshape(b, length, DIFF_HEADS, 2, HEAD_DIM)
    dk = dk.reshape(b, length, DIFF_HEADS, 2, HEAD_DIM)
    dv = dv.reshape(b, length, DIFF_HEADS, 2 * HEAD_DIM)
    return nat, (dq, dk, dv)


def even_merge(nat_o, diff_o, subln_g, lam_init, w_out):
    b, length = nat_o.shape[:2]
    diff_o = rms_norm(diff_o, subln_g) * (1.0 - lam_init)
    o = jnp.concatenate([nat_o.reshape(b, length, NAT_WIDTH),
                         diff_o.reshape(b, length, DIFF_WIDTH)], axis=-1)
    return o @ w_out


def even_mixer_context(h, w_in, w_out, lam_p, subln_g, layer):
    (nq, nk, nv), (dq, dk, dv) = even_project(h, w_in)
    nat_o = softmax_attend(nq, nk, nv)
    lam, lam_init = diff_lambda(lam_p, layer)
    diff_o = diff_attend(dq, dk, dv, lam)
    return even_merge(nat_o, diff_o, subln_g, lam_init, w_out), (nk, nv, dk, dv)


def even_mixer_latent(h, w_in, w_out, lam_p, subln_g, rel_bias, ck_nat, cv_nat, ck_diff, cv_diff, layer):
    (nq, nk, nv), (dq, dk, dv) = even_project(h, w_in)
    nat_o = nat_latent(nq, nk, nv, ck_nat, cv_nat, rel_bias)
    lam, lam_init = diff_lambda(lam_p, layer)
    dq, dk = axial_rope(dq), axial_rope(dk)
    diff_o = diff_attend(dq, jnp.concatenate([ck_diff, dk], axis=1),
                         jnp.concatenate([cv_diff, dv], axis=1), lam)
    return even_merge(nat_o, diff_o, subln_g, lam_init, w_out)


def short_conv(x, w):
    length = x.shape[1]
    left = DN_CONV // 2
    xp = jnp.pad(x, ((0, 0), (left, DN_CONV - 1 - left), (0, 0)))
    return sum(xp[:, j:j + length] * w[j] for j in range(DN_CONV))


def gated_delta_chunked(q, k, v, g, beta, s0):
    b, length, heads, _ = q.shape
    dv = v.shape[-1]
    n = length // DN_CHUNK

    def chunks(t):
        t = jnp.moveaxis(t, 2, 1)
        return t.reshape((b, heads, n, DN_CHUNK) + t.shape[3:])

    q, k, v, g, beta = (chunks(t) for t in (q, k, v, g, beta))
    gc = jnp.cumsum(g, axis=-1)
    kb = k * beta[..., None]
    vb = v * beta[..., None]
    tri = jnp.tril(jnp.ones((DN_CHUNK, DN_CHUNK), dtype=bool))
    strict = jnp.tril(jnp.ones((DN_CHUNK, DN_CHUNK), dtype=bool), -1)
    decay = jnp.exp(jnp.where(tri, gc[..., :, None] - gc[..., None, :], -jnp.inf))
    lower = jnp.where(strict, jnp.einsum('bhncd,bhnsd->bhncs', kb, k) * decay, 0.0)
    eye = jnp.eye(DN_CHUNK, dtype=F32)
    t_inv = lax.linalg.triangular_solve(lower + eye, jnp.broadcast_to(eye, lower.shape),
                                        left_side=True, lower=True)
    u = t_inv @ vb
    w = t_inv @ (kb * jnp.exp(gc)[..., None])
    a_qk = jnp.einsum('bhncd,bhnsd->bhncs', q, k) * decay

    def step(s, xs):
        qi, ki, ui, wi, gi, ai = xs
        v_new = ui - jnp.einsum('bhck,bhkv->bhcv', wi, s)
        o = (jnp.einsum('bhck,bhkv->bhcv', qi * jnp.exp(gi)[..., None], s)
             + jnp.einsum('bhcs,bhsv->bhcv', ai, v_new))
        g_last = gi[..., -1]
        s = (s * jnp.exp(g_last)[..., None, None]
             + jnp.einsum('bhck,bhcv->bhkv', ki * jnp.exp(g_last[..., None] - gi)[..., None], v_new))
        return s, o

    xs = tuple(jnp.moveaxis(t, 2, 0) for t in (q, k, u, w, gc, a_qk))
    s_final, o = lax.scan(step, s0, xs)
    o = jnp.moveaxis(o, 0, 2).reshape(b, heads, length, dv)
    return jnp.moveaxis(o, 1, 2), s_final


def deltanet_mixer(h, w_in, conv_w, a_log, dt_bias, onorm_g, w_out, s0):
    b, length, _ = h.shape
    z = h @ w_in
    qkv, ab, gate = jnp.split(z, [DN_CONV_CH, DN_CONV_CH + 4 * DN_HEADS], axis=-1)
    qkv = jax.nn.silu(short_conv(qkv, conv_w)).astype(F32)
    q, k, v = jnp.split(qkv, [DN_QK_WIDTH, 2 * DN_QK_WIDTH], axis=-1)
    q = l2_normalize(q.reshape(b, length, DN_HEADS, DN_DK)) * (DN_DK ** -0.5)
    k = l2_normalize(k.reshape(b, length, DN_HEADS, DN_DK))
    v = v.reshape(b, length, DN_HEADS, DN_DV)
    ab = ab.astype(F32).reshape(b, length, 2, 2, DN_HEADS)
    g = -jnp.exp(a_log.astype(F32)) * jax.nn.softplus(ab[:, :, 0] + dt_bias.astype(F32))
    beta = jax.nn.sigmoid(ab[:, :, 1])
    s0 = s0.astype(F32)
    o_f, s_f = gated_delta_chunked(q, k, v, g[:, :, 0], beta[:, :, 0], s0[:, 0])
    flip = lambda t: t[:, ::-1]
    o_b, s_b = gated_delta_chunked(flip(q), flip(k), flip(v), flip(g[:, :, 1]), flip(beta[:, :, 1]), s0[:, 1])
    o = o_f + flip(o_b)
    o = rms_norm(o, onorm_g) * jax.nn.silu(gate.astype(F32).reshape(b, length, DN_HEADS, DN_DV))
    out = o.reshape(b, length, DN_WIDTH).astype(h.dtype) @ w_out
    return out, jnp.stack([s_f, s_b], axis=1)


def moe_ffn(h, router_w, router_b, wg, wu, wd):
    b, length, d = h.shape
    x = h.reshape(b * length, d)
    scores = jax.nn.sigmoid((x @ router_w).astype(F32))
    biased = scores + router_b.astype(F32)
    group_score = lax.top_k(biased.reshape(-1, N_GROUPS, EXPERTS_PER_GROUP), TOP_K)[0].sum(-1)
    group_sel = jnp.argmax(group_score, axis=-1)
    in_group = (jnp.arange(N_EXPERTS) // EXPERTS_PER_GROUP)[None, :] == group_sel[:, None]
    _, idx = lax.top_k(jnp.where(in_group, biased, -jnp.inf), TOP_K)
    wts = jnp.take_along_axis(scores, idx, axis=-1)
    wts = wts / wts.sum(-1, keepdims=True)
    gates = jnp.einsum('tk,tke->te', wts, jax.nn.one_hot(idx, N_EXPERTS, dtype=F32)).astype(h.dtype)
    y = jnp.zeros_like(x)
    for e in range(N_EXPERTS):
        he = jax.nn.silu(x @ wg[e]) * (x @ wu[e])
        y = y + gates[:, e:e + 1] * (he @ wd[e])
    return y.reshape(b, length, d)


def setup_inputs(seed: int = 0) -> dict:
    key = jax.random.key(seed)
    ks = jax.random.split(key, 32)
    d = D_MODEL

    def nrm(k, shape, scale):
        return jax.random.normal(k, shape, F32) * scale

    dt = jnp.exp(jax.random.uniform(ks[17], (N_ODD, 2, DN_HEADS), F32, math.log(1e-3), math.log(1e-1)))
    return {
        'x_prompt': nrm(ks[0], (BATCH, SEQ, d), 1.0),
        'x_sample': nrm(ks[1], (DEC_BATCH, DEC_SEQ, d), 1.0),
        'cache_nat_k': nrm(ks[2], (DEC_BATCH, N_EVEN, PAST_LEN, NAT_HEADS, HEAD_DIM), 1.0),
        'cache_nat_v': nrm(ks[3], (DEC_BATCH, N_EVEN, PAST_LEN, NAT_HEADS, HEAD_DIM), 1.0),
        'cache_diff_k': nrm(ks[4], (DEC_BATCH, N_EVEN, PAST_LEN, DIFF_HEADS, 2, HEAD_DIM), 1.0),
        'cache_diff_v': nrm(ks[5], (DEC_BATCH, N_EVEN, PAST_LEN, DIFF_HEADS, 2 * HEAD_DIM), 1.0),
        'state_delta': nrm(ks[6], (DEC_BATCH, N_ODD, 2, DN_HEADS, DN_DK, DN_DV), 0.2),
        'c': nrm(ks[7], (DEC_BATCH, d), 1.0),
        'c_ctx': nrm(ks[8], (d,), 1.0),
        'w_in_even': nrm(ks[9], (N_EVEN, d, EVEN_IN), d ** -0.5),
        'w_out_even': nrm(ks[10], (N_EVEN, EVEN_OUT, d), EVEN_OUT ** -0.5 * BETA),
        'nat_bias': nrm(ks[11], (N_EVEN, NAT_HEADS, 2 * NAT_WIN_H - 1, 2 * NAT_WIN_W - 1), 0.1),
        'diff_lam': nrm(ks[12], (N_EVEN, 4, HEAD_DIM), 0.1),
        'diff_subln': 1.0 + nrm(ks[13], (N_EVEN, 2 * HEAD_DIM), 0.01),
        'w_in_odd': nrm(ks[14], (N_ODD, d, ODD_IN), d ** -0.5),
        'conv_odd': nrm(ks[15], (N_ODD, DN_CONV, DN_CONV_CH), DN_CONV ** -0.5),
        'a_log_odd': jnp.log(jax.random.uniform(ks[16], (N_ODD, 2, DN_HEADS), F32, 1.0, 16.0)),
        'dt_bias_odd': dt + jnp.log(-jnp.expm1(-dt)),
        'onorm_odd': 1.0 + nrm(ks[18], (N_ODD, DN_DV), 0.01),
        'w_out_odd': nrm(ks[19], (N_ODD, DN_WIDTH, d), DN_WIDTH ** -0.5 * BETA),
        'ada_w': nrm(ks[20], (DEPTH, d, 6 * d), 0.5 * d ** -0.5),
        'ada_b': nrm(ks[21], (DEPTH, 6 * d), 0.02),
        'ln_g': 1.0 + nrm(ks[22], (DEPTH, 2, d), 0.01),
        'ln_b': nrm(ks[23], (DEPTH, 2, d), 0.01),
        'router_w': nrm(ks[24], (d, N_EXPERTS), d ** -0.5),
        'router_b': nrm(ks[25], (N_EXPERTS,), 0.01),
        'moe_wg': nrm(ks[26], (DEPTH, N_EXPERTS, d, EXPERT_FF), d ** -0.5),
        'moe_wu': nrm(ks[27], (DEPTH, N_EXPERTS, d, EXPERT_FF), d ** -0.5),
        'moe_wd': nrm(ks[28], (DEPTH, N_EXPERTS, EXPERT_FF, d), EXPERT_FF ** -0.5 * BETA),
    }


def reference(x_prompt, x_sample, cache_nat_k, cache_nat_v, cache_diff_k, cache_diff_v, state_delta,
              c, c_ctx, w_in_even, w_out_even, nat_bias, diff_lam, diff_subln,
              w_in_odd, conv_odd, a_log_odd, dt_bias_odd, onorm_odd, w_out_odd,
              ada_w, ada_b, ln_g, ln_b, router_w, router_b, moe_wg, moe_wu, moe_wd):
    xp, xs = x_prompt, x_sample
    nat_k_list, nat_v_list, diff_k_list, diff_v_list, delta_list = [], [], [], [], []
    for layer in range(DEPTH):
        mp = ada_modulation(c_ctx, ada_w[layer], ada_b[layer])
        ms = ada_modulation(c, ada_w[layer], ada_b[layer])
        hp = modulate(xp, mp[0], mp[1])
        hs = modulate(xs, ms[0], ms[1])
        i = layer // 2
        if layer % 2 == 0:
            op, (nk, nv, dk, dv) = even_mixer_context(hp, w_in_even[i], w_out_even[i], diff_lam[i],
                                                      diff_subln[i], layer)
            os_ = even_mixer_latent(hs, w_in_even[i], w_out_even[i], diff_lam[i], diff_subln[i], nat_bias[i],
                                    cache_nat_k[:, i], cache_nat_v[:, i], cache_diff_k[:, i], cache_diff_v[:, i],
                                    layer)
            nat_k_list.append(nk)
            nat_v_list.append(nv)
            diff_k_list.append(dk)
            diff_v_list.append(dv)
        else:
            s_zero = jnp.zeros((xp.shape[0], 2, DN_HEADS, DN_DK, DN_DV), F32)
            op, st = deltanet_mixer(hp, w_in_odd[i], conv_odd[i], a_log_odd[i], dt_bias_odd[i],
                                    onorm_odd[i], w_out_odd[i], s_zero)
            os_, _ = deltanet_mixer(hs, w_in_odd[i], conv_odd[i], a_log_odd[i], dt_bias_odd[i],
                                    onorm_odd[i], w_out_odd[i], state_delta[:, i])
            delta_list.append(st)
        xp = layer_norm(ALPHA * xp + mp[2] * op, ln_g[layer, 0], ln_b[layer, 0])
        xs = layer_norm(ALPHA * xs + ms[2] * os_, ln_g[layer, 0], ln_b[layer, 0])
        hp = modulate(xp, mp[3], mp[4])
        hs = modulate(xs, ms[3], ms[4])
        fp = moe_ffn(hp, router_w, router_b, moe_wg[layer], moe_wu[layer], moe_wd[layer])
        fs = moe_ffn(hs, router_w, router_b, moe_wg[layer], moe_wu[layer], moe_wd[layer])
        xp = layer_norm(ALPHA * xp + mp[5] * fp, ln_g[layer, 1], ln_b[layer, 1])
        xs = layer_norm(ALPHA * xs + ms[5] * fs, ln_g[layer, 1], ln_b[layer, 1])
    y_prompt = xp
    y_sample = xs
    new_nat_k = jnp.stack(nat_k_list, axis=1)
    new_nat_v = jnp.stack(nat_v_list, axis=1)
    new_diff_k = jnp.stack(diff_k_list, axis=1)
    new_diff_v = jnp.stack(diff_v_list, axis=1)
    new_state_delta = jnp.stack(delta_list, axis=1)
    return (y_prompt, y_sample, new_nat_k, new_nat_v, new_diff_k, new_diff_v, new_state_delta)
```

```python
import functools
import math

import jax
import jax.numpy as jnp
from jax import lax
from jax.experimental import pallas as pl
from jax.experimental.pallas import tpu as pltpu

F32 = jnp.float32
BF16 = jnp.bfloat16

D_MODEL = 1024
BATCH = 16
SEQ = 256
DEC_BATCH = 8
DEC_SEQ = 1024
PAST_LEN = 256
GRID_W = 64
N_ROWS = DEC_SEQ // GRID_W

HEAD_DIM = 64
NAT_HEADS = 8
NAT_WIN_H = 8
NAT_WIN_W = 16
DIFF_HEADS = 4
NAT_WIDTH = NAT_HEADS * HEAD_DIM
DIFF_WIDTH = DIFF_HEADS * 2 * HEAD_DIM
EVEN_IN = 3 * NAT_WIDTH + 3 * DIFF_WIDTH
ROPE_THETA = 10000.0

DN_HEADS = 8
DN_DK = 128
DN_DV = 128
DN_QK_WIDTH = DN_HEADS * DN_DK
DN_WIDTH = DN_HEADS * DN_DV
DN_CONV_CH = 2 * DN_QK_WIDTH + DN_WIDTH
DN_CHUNK = 64

N_EXPERTS = 16
N_GROUPS = 4
EXPERTS_PER_GROUP = N_EXPERTS // N_GROUPS
EXPERT_FF = 512

DEPTH = 2
ALPHA = (2 * DEPTH) ** 0.25
EPS = 1e-5

N_PROMPT = BATCH * SEQ
N_SAMPLE = DEC_BATCH * DEC_SEQ
N_TOK = N_PROMPT + N_SAMPLE
N_COND = 16

VMEM_LIMIT = 56 * 1024 * 1024
LANES = 128


def _params(*sem):
    return pltpu.CompilerParams(dimension_semantics=sem, vmem_limit_bytes=VMEM_LIMIT)


def _cond_row(row0):
    return jnp.where(row0 < N_PROMPT, 0, 1 + (row0 - N_PROMPT) // DEC_SEQ)


def _silu(x):
    return x * jax.nn.sigmoid(x)


def _layer_norm(r, g, b):
    mu = jnp.mean(r, axis=-1, keepdims=True)
    xc = r - mu
    var = jnp.mean(xc * xc, axis=-1, keepdims=True)
    return xc * lax.rsqrt(var + EPS) * g + b


def _dot(a, b):
    return jnp.dot(a, b, preferred_element_type=F32)


def _dot_nt(a, b):
    return lax.dot_general(a, b, (((1,), (1,)), ((), ())), preferred_element_type=F32)


def _dot_tn(a, b):
    return lax.dot_general(a, b, (((0,), (0,)), ((), ())), preferred_element_type=F32)


ADA_TN = 1536


def _ada_kernel(c_ref, w_ref, b_ref, o_ref):
    s = _silu(c_ref[...])
    o_ref[0] = _dot(s.astype(BF16), w_ref[0].astype(BF16)) + b_ref[0]


def _ada_modulation(cond, ada_w, ada_b):
    n = 6 * D_MODEL
    return pl.pallas_call(
        _ada_kernel,
        out_shape=jax.ShapeDtypeStruct((DEPTH, N_COND, n), F32),
        grid=(DEPTH, n // ADA_TN),
        in_specs=[
            pl.BlockSpec((N_COND, D_MODEL), lambda l, j: (0, 0)),
            pl.BlockSpec((1, D_MODEL, ADA_TN), lambda l, j: (l, 0, j)),
            pl.BlockSpec((1, 1, ADA_TN), lambda l, j: (l, 0, j)),
        ],
        out_specs=pl.BlockSpec((1, N_COND, ADA_TN), lambda l, j: (l, 0, j)),
        compiler_params=_params("arbitrary", "arbitrary"),
        name="ada_modulation",
    )(cond, ada_w, ada_b.reshape(DEPTH, 1, n))


PROJ_TM = 512


def _inproj_kernel(x_ref, mod_ref, w_ref, o_ref):
    m = mod_ref[0]
    h = x_ref[...] * (1.0 + m[1:2]) + m[0:1]
    o_ref[...] = _dot(h.astype(BF16), w_ref[...])


def _input_projection(x, mod, w_bf16, tm=PROJ_TM):
    n = w_bf16.shape[1]
    return pl.pallas_call(
        _inproj_kernel,
        out_shape=jax.ShapeDtypeStruct((N_TOK, n), F32),
        grid=(N_TOK // tm,),
        in_specs=[
            pl.BlockSpec((tm, D_MODEL), lambda i: (i, 0)),
            pl.BlockSpec((1, 6, D_MODEL), lambda i: (_cond_row(i * tm), 0, 0)),
            pl.BlockSpec((D_MODEL, n), lambda i: (0, 0)),
        ],
        out_specs=pl.BlockSpec((tm, n), lambda i: (i, 0)),
        compiler_params=_params("arbitrary"),
        name="input_projection",
    )(x, mod, w_bf16)


_NAT_Q0, _NAT_K0, _NAT_V0 = 0, NAT_WIDTH, 2 * NAT_WIDTH
_DIFF_Q0 = 3 * NAT_WIDTH
_DIFF_K0 = _DIFF_Q0 + DIFF_WIDTH
_DIFF_V0 = _DIFF_K0 + DIFF_WIDTH
ATTN_SCALE = HEAD_DIM ** -0.5


def _diff_lambda(lam_ref, lam_init):
    lp = lam_ref[...]
    return (jnp.exp(jnp.sum(lp[0:1] * lp[1:2], axis=-1, keepdims=True))
            - jnp.exp(jnp.sum(lp[2:3] * lp[3:4], axis=-1, keepdims=True)) + lam_init)


def _softmax_parts(s):
    m = jnp.max(s, axis=-1, keepdims=True)
    e = jnp.exp(s - m)
    return e, jnp.sum(e, axis=-1, keepdims=True)


def _sub_norm(o, g, lam_init):
    ms = jnp.mean(o * o, axis=-1, keepdims=True)
    return o * lax.rsqrt(ms + EPS) * g * (1.0 - lam_init)


def _ctx_attn_kernel(z_ref, lam_ref, subg_ref, o_ref, *, lam_init):
    lam = _diff_lambda(lam_ref, lam_init)
    for h in range(NAT_HEADS):
        c = h * HEAD_DIM
        q = z_ref[:, _NAT_Q0 + c:_NAT_Q0 + c + HEAD_DIM].astype(BF16)
        k = z_ref[:, _NAT_K0 + c:_NAT_K0 + c + HEAD_DIM].astype(BF16)
        v = z_ref[:, _NAT_V0 + c:_NAT_V0 + c + HEAD_DIM].astype(BF16)
        e, l = _softmax_parts(_dot_nt(q, k) * ATTN_SCALE)
        o_ref[:, c:c + HEAD_DIM] = (_dot(e.astype(BF16), v) / l).astype(o_ref.dtype)
    for h in range(DIFF_HEADS):
        c = h * 2 * HEAD_DIM
        parts = []
        v = z_ref[:, _DIFF_V0 + c:_DIFF_V0 + c + 2 * HEAD_DIM].astype(BF16)
        for j in range(2):
            cj = c + j * HEAD_DIM
            q = z_ref[:, _DIFF_Q0 + cj:_DIFF_Q0 + cj + HEAD_DIM].astype(BF16)
            k = z_ref[:, _DIFF_K0 + cj:_DIFF_K0 + cj + HEAD_DIM].astype(BF16)
            e, l = _softmax_parts(_dot_nt(q, k) * ATTN_SCALE)
            parts.append(_dot(e.astype(BF16), v) / l)
        o = parts[0] - lam * parts[1]
        o_ref[:, NAT_WIDTH + c:NAT_WIDTH + c + 2 * HEAD_DIM] = (
            _sub_norm(o, subg_ref[...], lam_init).astype(o_ref.dtype))


def _context_attention(z, lam_p, subln_g, lam_init):
    return pl.pallas_call(
        functools.partial(_ctx_attn_kernel, lam_init=lam_init),
        out_shape=jax.ShapeDtypeStruct((N_PROMPT, D_MODEL), BF16),
        grid=(BATCH,),
        in_specs=[
            pl.BlockSpec((SEQ, EVEN_IN), lambda b: (b, 0)),
            pl.BlockSpec((4, HEAD_DIM), lambda b: (0, 0)),
            pl.BlockSpec((1, 2 * HEAD_DIM), lambda b: (0, 0)),
        ],
        out_specs=pl.BlockSpec((SEQ, D_MODEL), lambda b: (b, 0)),
        compiler_params=_params("arbitrary"),
        name="context_attention",
    )(z, lam_p, subln_g.reshape(1, 2 * HEAD_DIM))


NAT_TQ = 256
_SAMPLE_BLK0 = N_PROMPT // DEC_SEQ


def _nat_bias_table(rel_bias):
    rows = N_ROWS
    kh = min(NAT_WIN_H, rows)
    kw = NAT_WIN_W
    t = jnp.arange(DEC_SEQ)
    r, c = t // GRID_W, t % GRID_W
    r_start = jnp.clip(r - kh // 2, 0, rows - kh)
    c_start = jnp.clip(c - kw // 2, 0, GRID_W - kw)
    row_ok = (r[None, :] >= r_start[:, None]) & (r[None, :] < r_start[:, None] + kh)
    col_ok = (c[None, :] >= c_start[:, None]) & (c[None, :] < c_start[:, None] + kw)
    dr = jnp.clip(r[None, :] - r[:, None], 1 - NAT_WIN_H, NAT_WIN_H - 1) + (NAT_WIN_H - 1)
    dc = jnp.clip(c[None, :] - c[:, None], 1 - kw, kw - 1) + (NAT_WIN_W - 1)
    bias = rel_bias[:, dr, dc].astype(F32)
    return jnp.where((row_ok & col_ok)[None], bias, -jnp.inf)


def _nat_latent_kernel(q_ref, k_ref, v_ref, ck_ref, cv_ref, bias_ref, o_ref):
    for j in range(2):
        sl = slice(j * HEAD_DIM, (j + 1) * HEAD_DIM)
        q = q_ref[:, sl].astype(BF16)
        s_loc = _dot_nt(q, k_ref[:, sl].astype(BF16)) * ATTN_SCALE + bias_ref[j]
        s_ctx = _dot_nt(q, ck_ref[0, :, sl].astype(BF16)) * ATTN_SCALE
        m = jnp.maximum(jnp.max(s_loc, axis=-1, keepdims=True), jnp.max(s_ctx, axis=-1, keepdims=True))
        e_loc = jnp.exp(s_loc - m)
        e_ctx = jnp.exp(s_ctx - m)
        l = jnp.sum(e_loc, axis=-1, keepdims=True) + jnp.sum(e_ctx, axis=-1, keepdims=True)
        o = (_dot(e_loc.astype(BF16), v_ref[:, sl].astype(BF16))
             + _dot(e_ctx.astype(BF16), cv_ref[0, :, sl].astype(BF16)))
        o_ref[:, sl] = (o / l).astype(o_ref.dtype)


def _nat_latent_attention(z, ck, cv, bias):
    nq = DEC_SEQ // NAT_TQ
    qblk0 = N_PROMPT // NAT_TQ
    pair = 2 * HEAD_DIM
    return pl.pallas_call(
        _nat_latent_kernel,
        out_shape=jax.ShapeDtypeStruct((N_SAMPLE, NAT_WIDTH), BF16),
        grid=(NAT_HEADS // 2, nq, DEC_BATCH),
        in_specs=[
            pl.BlockSpec((NAT_TQ, pair), lambda hp, qt, b: (qblk0 + b * nq + qt, _NAT_Q0 // pair + hp)),
            pl.BlockSpec((DEC_SEQ, pair), lambda hp, qt, b: (_SAMPLE_BLK0 + b, _NAT_K0 // pair + hp)),
            pl.BlockSpec((DEC_SEQ, pair), lambda hp, qt, b: (_SAMPLE_BLK0 + b, _NAT_V0 // pair + hp)),
            pl.BlockSpec((1, PAST_LEN, pair), lambda hp, qt, b: (b, 0, hp)),
            pl.BlockSpec((1, PAST_LEN, pair), lambda hp, qt, b: (b, 0, hp)),
            pl.BlockSpec((2, NAT_TQ, DEC_SEQ), lambda hp, qt, b: (hp, qt, 0)),
        ],
        out_specs=pl.BlockSpec((NAT_TQ, pair), lambda hp, qt, b: (b * nq + qt, hp)),
        compiler_params=_params("arbitrary", "arbitrary", "arbitrary"),
        name="nat_latent_attention",
    )(z, z, z, ck, cv, bias)


DIFF_TQ = 256


def _rope_tables():
    nf = HEAD_DIM // 4
    t = jnp.arange(DEC_SEQ)
    inv = ROPE_THETA ** (-jnp.arange(nf, dtype=F32) / nf)
    ang_r = (t // GRID_W).astype(F32)[:, None] * inv
    ang_c = (t % GRID_W).astype(F32)[:, None] * inv
    cos = jnp.concatenate([jnp.cos(ang_r)] * 2 + [jnp.cos(ang_c)] * 2, axis=-1)
    sin = jnp.concatenate([-jnp.sin(ang_r), jnp.sin(ang_r), -jnp.sin(ang_c), jnp.sin(ang_c)], axis=-1)
    return jnp.tile(cos, (1, 2)), jnp.tile(sin, (1, 2))


def _rope(x, cos, sin):
    nf = HEAD_DIM // 4
    lane = lax.broadcasted_iota(jnp.int32, x.shape, 1)
    upper = (lane // nf) % 2 == 1
    partner = jnp.where(upper, pltpu.roll(x, nf, axis=1), pltpu.roll(x, x.shape[1] - nf, axis=1))
    return x * cos + partner * sin


def _diff_latent_kernel(q_ref, k_ref, v_ref, ck_ref, cv_ref, cos_ref, sin_ref, lam_ref, subg_ref,
                        o_ref, kr_ref, *, lam_init):
    qt = pl.program_id(2)

    @pl.when(qt == 0)
    def _():
        kr_ref[...] = _rope(k_ref[...], cos_ref[...], sin_ref[...]).astype(BF16)

    lam = _diff_lambda(lam_ref, lam_init)
    row0 = pl.multiple_of(qt * DIFF_TQ, DIFF_TQ)
    q = _rope(q_ref[...], cos_ref[pl.ds(row0, DIFF_TQ), :], sin_ref[pl.ds(row0, DIFF_TQ), :]).astype(BF16)
    v_loc = v_ref[...].astype(BF16)
    v_ctx = cv_ref[0].astype(BF16)
    parts = []
    for j in range(2):
        sl = slice(j * HEAD_DIM, (j + 1) * HEAD_DIM)
        s_ctx = _dot_nt(q[:, sl], ck_ref[0, :, sl].astype(BF16)) * ATTN_SCALE
        s_loc = _dot_nt(q[:, sl], kr_ref[:, sl]) * ATTN_SCALE
        m = jnp.maximum(jnp.max(s_loc, axis=-1, keepdims=True), jnp.max(s_ctx, axis=-1, keepdims=True))
        e_loc = jnp.exp(s_loc - m)
        e_ctx = jnp.exp(s_ctx - m)
        l = jnp.sum(e_loc, axis=-1, keepdims=True) + jnp.sum(e_ctx, axis=-1, keepdims=True)
        parts.append((_dot(e_loc.astype(BF16), v_loc) + _dot(e_ctx.astype(BF16), v_ctx)) / l)
    o = parts[0] - lam * parts[1]
    o_ref[...] = _sub_norm(o, subg_ref[...], lam_init).astype(o_ref.dtype)


def _diff_latent_attention(z, ck, cv, lam_p, subln_g, lam_init):
    nq = DEC_SEQ // DIFF_TQ
    qblk0 = N_PROMPT // DIFF_TQ
    w = 2 * HEAD_DIM
    cos, sin = _rope_tables()
    return pl.pallas_call(
        functools.partial(_diff_latent_kernel, lam_init=lam_init),
        out_shape=jax.ShapeDtypeStruct((N_SAMPLE, DIFF_WIDTH), BF16),
        grid=(DEC_BATCH, DIFF_HEADS, nq),
        in_specs=[
            pl.BlockSpec((DIFF_TQ, w), lambda b, h, qt: (qblk0 + b * nq + qt, _DIFF_Q0 // w + h)),
            pl.BlockSpec((DEC_SEQ, w), lambda b, h, qt: (_SAMPLE_BLK0 + b, _DIFF_K0 // w + h)),
            pl.BlockSpec((DEC_SEQ, w), lambda b, h, qt: (_SAMPLE_BLK0 + b, _DIFF_V0 // w + h)),
            pl.BlockSpec((1, PAST_LEN, w), lambda b, h, qt: (b, 0, h)),
            pl.BlockSpec((1, PAST_LEN, w), lambda b, h, qt: (b, 0, h)),
            pl.BlockSpec((DEC_SEQ, w), lambda b, h, qt: (0, 0)),
            pl.BlockSpec((DEC_SEQ, w), lambda b, h, qt: (0, 0)),
            pl.BlockSpec((4, HEAD_DIM), lambda b, h, qt: (0, 0)),
            pl.BlockSpec((1, w), lambda b, h, qt: (0, 0)),
        ],
        out_specs=pl.BlockSpec((DIFF_TQ, w), lambda b, h, qt: (b * nq + qt, h)),
        scratch_shapes=[pltpu.VMEM((DEC_SEQ, w), BF16)],
        compiler_params=_params("arbitrary", "arbitrary", "arbitrary"),
        name="diff_latent_attention",
    )(z, z, z, ck, cv, cos, sin, lam_p, subln_g.reshape(1, w))


OUT_TM = 256


def _route(logits_t, rb_col):
    sc = jax.nn.sigmoid(logits_t)
    bi = sc + rb_col
    srow = [sc[e:e + 1] for e in range(N_EXPERTS)]
    brow = [bi[e:e + 1] for e in range(N_EXPERTS)]
    n = EXPERTS_PER_GROUP
    gscore = []
    for g in range(N_GROUPS):
        v = brow[g * n:(g + 1) * n]
        best = None
        for i in range(n):
            for j in range(i + 1, n):
                s = v[i] + v[j]
                best = s if best is None else jnp.maximum(best, s)
        gscore.append(best)
    sel = jnp.zeros_like(gscore[0], dtype=jnp.int32)
    best = gscore[0]
    for g in range(1, N_GROUPS):
        better = gscore[g] > best
        sel = jnp.where(better, g, sel)
        best = jnp.where(better, gscore[g], best)

    def pick_group(rows, i):
        out = rows[(N_GROUPS - 1) * n + i]
        for g in range(N_GROUPS - 2, -1, -1):
            out = jnp.where(sel == g, rows[g * n + i], out)
        return out

    bv = [pick_group(brow, i) for i in range(n)]
    sv = [pick_group(srow, i) for i in range(n)]
    i1 = jnp.zeros_like(sel)
    m1, w1 = bv[0], sv[0]
    for i in range(1, n):
        better = bv[i] > m1
        i1 = jnp.where(better, i, i1)
        m1 = jnp.where(better, bv[i], m1)
        w1 = jnp.where(better, sv[i], w1)
    i2 = jnp.zeros_like(sel)
    m2 = jnp.full_like(m1, -jnp.inf)
    w2 = jnp.zeros_like(w1)
    for i in range(n):
        better = (i1 != i) & (bv[i] > m2)
        i2 = jnp.where(better, i, i2)
        m2 = jnp.where(better, bv[i], m2)
        w2 = jnp.where(better, sv[i], w2)
    tot = w1 + w2
    w1, w2 = w1 / tot, w2 / tot
    e1 = sel * n + i1
    e2 = sel * n + i2
    eid = lax.broadcasted_iota(jnp.int32, logits_t.shape, 0)
    return jnp.where(eid == e1, w1, 0.0) + jnp.where(eid == e2, w2, 0.0)


def _outproj_kernel(o_ref, x_ref, mod_ref, w_ref, lng_ref, lnb_ref, rw_ref, rb_ref,
                    x1_ref, h2_ref, gates_ref):
    m = mod_ref[0]
    y = _dot(o_ref[...], w_ref[...])
    x1 = _layer_norm(ALPHA * x_ref[...] + m[2:3] * y, lng_ref[...], lnb_ref[...])
    x1_ref[...] = x1
    h2 = x1 * (1.0 + m[4:5]) + m[3:4]
    h2_ref[...] = h2.astype(BF16)
    logits_t = lax.dot_general(rw_ref[...], h2, (((1,), (1,)), ((), ())),
                               precision=lax.Precision.HIGHEST, preferred_element_type=F32)
    gates_t = _route(logits_t, rb_ref[...])
    pad = jnp.zeros((LANES - N_EXPERTS, gates_t.shape[1]), F32)
    gates_ref[...] = jnp.concatenate([gates_t, pad], axis=0).T


def _output_projection(o, x, mod, w_bf16, ln_g, ln_b, router_w_t, router_b, tm=OUT_TM):
    return pl.pallas_call(
        _outproj_kernel,
        out_shape=(jax.ShapeDtypeStruct((N_TOK, D_MODEL), F32),
                   jax.ShapeDtypeStruct((N_TOK, D_MODEL), BF16),
                   jax.ShapeDtypeStruct((N_TOK, LANES), F32)),
        grid=(N_TOK // tm,),
        in_specs=[
            pl.BlockSpec((tm, D_MODEL), lambda i: (i, 0)),
            pl.BlockSpec((tm, D_MODEL), lambda i: (i, 0)),
            pl.BlockSpec((1, 6, D_MODEL), lambda i: (_cond_row(i * tm), 0, 0)),
            pl.BlockSpec((D_MODEL, D_MODEL), lambda i: (0, 0)),
            pl.BlockSpec((1, D_MODEL), lambda i: (0, 0)),
            pl.BlockSpec((1, D_MODEL), lambda i: (0, 0)),
            pl.BlockSpec((N_EXPERTS, D_MODEL), lambda i: (0, 0)),
            pl.BlockSpec((N_EXPERTS, 1), lambda i: (0, 0)),
        ],
        out_specs=(pl.BlockSpec((tm, D_MODEL), lambda i: (i, 0)),
                   pl.BlockSpec((tm, D_MODEL), lambda i: (i, 0)),
                   pl.BlockSpec((tm, LANES), lambda i: (i, 0))),
        compiler_params=_params("arbitrary"),
        name="output_projection",
    )(o, x, mod, w_bf16, ln_g.reshape(1, D_MODEL), ln_b.reshape(1, D_MODEL),
      router_w_t, router_b.reshape(N_EXPERTS, 1))


MOE_TM = 512


def _moe_kernel(h_ref, gates_ref, wg_ref, wu_ref, wd_ref, x1_ref, mod_ref, lng_ref, lnb_ref,
                o_ref, acc_ref):
    e = pl.program_id(1)

    @pl.when(e == 0)
    def _():
        acc_ref[...] = jnp.zeros_like(acc_ref)

    h = h_ref[...]
    he = _silu(_dot(h, wg_ref[0])) * _dot(h, wu_ref[0])
    lane = lax.broadcasted_iota(jnp.int32, gates_ref.shape, 1)
    g = jnp.sum(jnp.where(lane == e, gates_ref[...], 0.0), axis=-1, keepdims=True)
    acc_ref[...] += g * _dot(he.astype(BF16), wd_ref[0])

    @pl.when(e == N_EXPERTS - 1)
    def _():
        m = mod_ref[0]
        o_ref[...] = _layer_norm(ALPHA * x1_ref[...] + m[5:6] * acc_ref[...], lng_ref[...], lnb_ref[...])


def _moe_ffn(h2, gates, wg, wu, wd, x1, mod, ln_g, ln_b, tm=MOE_TM):
    return pl.pallas_call(
        _moe_kernel,
        out_shape=jax.ShapeDtypeStruct((N_TOK, D_MODEL), F32),
        grid=(N_TOK // tm, N_EXPERTS),
        in_specs=[
            pl.BlockSpec((tm, D_MODEL), lambda i, e: (i, 0)),
            pl.BlockSpec((tm, LANES), lambda i, e: (i, 0)),
            pl.BlockSpec((1, D_MODEL, EXPERT_FF), lambda i, e: (e, 0, 0)),
            pl.BlockSpec((1, D_MODEL, EXPERT_FF), lambda i, e: (e, 0, 0)),
            pl.BlockSpec((1, EXPERT_FF, D_MODEL), lambda i, e: (e, 0, 0)),
            pl.BlockSpec((tm, D_MODEL), lambda i, e: (i, 0)),
            pl.BlockSpec((1, 6, D_MODEL), lambda i, e: (_cond_row(i * tm), 0, 0)),
            pl.BlockSpec((1, D_MODEL), lambda i, e: (0, 0)),
            pl.BlockSpec((1, D_MODEL), lambda i, e: (0, 0)),
        ],
        out_specs=pl.BlockSpec((tm, D_MODEL), lambda i, e: (i, 0)),
        scratch_shapes=[pltpu.VMEM((tm, D_MODEL), F32)],
        compiler_params=_params("arbitrary", "arbitrary"),
        name="moe_ffn",
    )(h2, gates, wg, wu, wd, x1, mod, ln_g.reshape(1, D_MODEL), ln_b.reshape(1, D_MODEL))


DN_GATE_COL0 = DN_CONV_CH
DN_AB_COL0 = DN_CONV_CH + DN_WIDTH
DN_IN = DN_AB_COL0 + LANES
N_DIRS = 2


def _dn_prep_kernel(q_ref, k_ref, v_ref, wq_ref, wk_ref, wv_ref, qo_ref, ko_ref, vo_ref):
    length = q_ref.shape[0]
    row = lax.broadcasted_iota(jnp.int32, (length, DN_DK), 0)

    def conv(x, w):
        prev = jnp.where(row == 0, 0.0, pltpu.roll(x, 1, axis=0))
        nxt = jnp.where(row == length - 1, 0.0, pltpu.roll(x, length - 1, axis=0))
        return _silu(prev * w[0:1] + x * w[1:2] + nxt * w[2:3])

    def l2n(x):
        return x * lax.rsqrt(jnp.sum(x * x, axis=-1, keepdims=True) + 1e-6)

    qo_ref[0] = l2n(conv(q_ref[...], wq_ref[...])) * (DN_DK ** -0.5)
    ko_ref[0] = l2n(conv(k_ref[...], wk_ref[...]))
    vo_ref[0] = conv(v_ref[...], wv_ref[...])


def _dn_prep(z1, conv_w, nb, length, row_blk0):
    hq, hk, hv = 0, DN_HEADS, 2 * DN_HEADS
    shp = jax.ShapeDtypeStruct((DN_HEADS, nb * length, DN_DK), F32)
    blk = lambda off: pl.BlockSpec((length, DN_DK), lambda b, h: (row_blk0 + b, off + h))
    wblk = lambda off: pl.BlockSpec((3, DN_DK), lambda b, h: (0, off + h))
    oblk = pl.BlockSpec((1, length, DN_DK), lambda b, h: (h, b, 0))
    return pl.pallas_call(
        _dn_prep_kernel,
        out_shape=(shp, shp, shp),
        grid=(nb, DN_HEADS),
        in_specs=[blk(hq), blk(hk), blk(hv), wblk(hq), wblk(hk), wblk(hv)],
        out_specs=(oblk, oblk, oblk),
        compiler_params=_params("arbitrary", "arbitrary"),
        name="deltanet_prep",
    )(z1, z1, z1, conv_w, conv_w, conv_w)


def _dn_gates_kernel(ab_ref, alog_ref, dtb_ref, o_ref):
    length = ab_ref.shape[0]
    x = ab_ref[...]
    a = x + dtb_ref[...]
    softplus = jnp.maximum(a, 0.0) + jnp.log1p(jnp.exp(-jnp.abs(a)))
    g = -jnp.exp(alog_ref[...]) * softplus
    beta = jax.nn.sigmoid(x)
    ri = lax.broadcasted_iota(jnp.int32, (DN_CHUNK, DN_CHUNK), 0)
    ci = lax.broadcasted_iota(jnp.int32, (DN_CHUNK, DN_CHUNK), 1)
    tril = (ri >= ci).astype(F32)
    triu = (ri <= ci).astype(F32)
    lane = lax.broadcasted_iota(jnp.int32, (DN_CHUNK, LANES), 1)
    for c in range(length // DN_CHUNK):
        rows = slice(c * DN_CHUNK, (c + 1) * DN_CHUNK)
        gch = g[rows]
        pre = jnp.dot(tril, gch, precision=lax.Precision.HIGHEST, preferred_element_type=F32)
        suf = jnp.dot(triu, gch, precision=lax.Precision.HIGHEST, preferred_element_type=F32)
        gc = jnp.where(lane < DN_HEADS, pre, suf)
        o_ref[rows, :] = jnp.where(lane < N_DIRS * DN_HEADS, gc, beta[rows])


def _dn_gates(z1, a_log, dt_bias, nb, length, row_blk0):
    pad = LANES - N_DIRS * DN_HEADS
    alog = jnp.pad(a_log.reshape(1, -1).astype(F32), ((0, 0), (0, pad)))
    dtb = jnp.pad(dt_bias.reshape(1, -1).astype(F32), ((0, 0), (0, pad)))
    return pl.pallas_call(
        _dn_gates_kernel,
        out_shape=jax.ShapeDtypeStruct((nb * length, LANES), F32),
        grid=(nb,),
        in_specs=[
            pl.BlockSpec((length, LANES), lambda b: (row_blk0 + b, DN_AB_COL0 // LANES)),
            pl.BlockSpec((1, LANES), lambda b: (0, 0)),
            pl.BlockSpec((1, LANES), lambda b: (0, 0)),
        ],
        out_specs=pl.BlockSpec((length, LANES), lambda b: (b, 0)),
        compiler_params=_params("arbitrary"),
        name="deltanet_gates",
    )(z1, alog, dtb)


def _unit_triangular_inverse(n):
    ri = lax.broadcasted_iota(jnp.int32, n.shape, 0)
    ci = lax.broadcasted_iota(jnp.int32, n.shape, 1)
    p = jnp.where(ri == ci, 1.0, 0.0) + n
    m = n
    steps = int(math.log2(DN_CHUNK)) - 1
    for _ in range(steps):
        mb = m.astype(BF16)
        m = _dot(mb, mb)
        p = p + _dot(p.astype(BF16), m.astype(BF16))
    return p


def _delta_chunk(q, k, v, beta, gcol, grow, s, lower):
    ri = lax.broadcasted_iota(jnp.int32, (DN_CHUNK, DN_CHUNK), 0)
    ci = lax.broadcasted_iota(jnp.int32, (DN_CHUNK, DN_CHUNK), 1)
    incl = (ri >= ci) if lower else (ri <= ci)
    strict = (ri > ci) if lower else (ri < ci)
    decay = jnp.exp(jnp.where(incl, gcol - grow, -jnp.inf))
    kb = k * beta
    vb = v * beta
    k16 = k.astype(BF16)
    kk = _dot_nt(kb.astype(BF16), k16)
    t_inv = _unit_triangular_inverse(jnp.where(strict, -(kk * decay), 0.0))
    eg = jnp.exp(gcol)
    uw = _dot(t_inv.astype(BF16), jnp.concatenate([vb, kb * eg], axis=1).astype(BF16))
    u, w = uw[:, :DN_DV], uw[:, DN_DV:]
    a_qk = _dot_nt(q.astype(BF16), k16) * decay
    s16 = s.astype(BF16)
    v_new = u - _dot(w.astype(BF16), s16)
    v_new16 = v_new.astype(BF16)
    o = _dot((q * eg).astype(BF16), s16) + _dot(a_qk.astype(BF16), v_new16)
    g_last = gcol[DN_CHUNK - 1:DN_CHUNK] if lower else gcol[0:1]
    kd = k * jnp.exp(g_last - gcol)
    s_new = s * jnp.exp(g_last) + _dot_tn(kd.astype(BF16), v_new16)
    return o, s_new


def _dn_scan_kernel(*refs, has_init, want_final):
    (qf_ref, kf_ref, vf_ref, gf_ref, qb_ref, kb_ref, vb_ref, gb_ref), rest = refs[:8], refs[8:]
    if has_init:
        s0_ref, rest = rest[0], rest[1:]
    of_ref, ob_ref = rest[0], rest[1]
    rest = rest[2:]
    if want_final:
        sf_ref, rest = rest[0], rest[1:]
    s_ref = rest[0]
    c = pl.program_id(1)
    nc = pl.num_programs(1)

    @pl.when(c == 0)
    def _():
        for d in range(N_DIRS):
            for h in range(DN_HEADS):
                if has_init:
                    s_ref[d * DN_HEADS + h] = s0_ref[0, d, h]
                else:
                    s_ref[d * DN_HEADS + h] = jnp.zeros((DN_DK, DN_DV), F32)

    dirs = ((qf_ref, kf_ref, vf_ref, gf_ref, of_ref, True), (qb_ref, kb_ref, vb_ref, gb_ref, ob_ref, False))
    for d, (q_ref, k_ref, v_ref, g_ref, o_ref, lower) in enumerate(dirs):
        gates = g_ref[...]
        gates_t = gates.T
        for h in range(DN_HEADS):
            j = d * DN_HEADS + h
            jb = N_DIRS * DN_HEADS + j
            o, s_new = _delta_chunk(q_ref[h], k_ref[h], v_ref[h], gates[:, jb:jb + 1],
                                    gates[:, j:j + 1], gates_t[j:j + 1, :], s_ref[j], lower)
            o_ref[:, h * DN_DV:(h + 1) * DN_DV] = o
            s_ref[j] = s_new

    if want_final:
        @pl.when(c == nc - 1)
        def _():
            for d in range(N_DIRS):
                for h in range(DN_HEADS):
                    sf_ref[0, d, h] = s_ref[d * DN_HEADS + h]


def _dn_scan(qn, kn, vn, gates, s0, nb, length, want_final):
    nc = length // DN_CHUNK
    has_init = s0 is not None
    fwd = lambda b, c: (0, b * nc + c, 0)
    bwd = lambda b, c: (0, b * nc + (nc - 1 - c), 0)
    hm = lambda im: pl.BlockSpec((DN_HEADS, DN_CHUNK, DN_DK), im)
    in_specs = [hm(fwd), hm(fwd), hm(fwd), pl.BlockSpec((DN_CHUNK, LANES), lambda b, c: (b * nc + c, 0)),
                hm(bwd), hm(bwd), hm(bwd), pl.BlockSpec((DN_CHUNK, LANES), lambda b, c: (b * nc + (nc - 1 - c), 0))]
    args = [qn, kn, vn, gates, qn, kn, vn, gates]
    state_blk = pl.BlockSpec((1, N_DIRS, DN_HEADS, DN_DK, DN_DV), lambda b, c: (b, 0, 0, 0, 0))
    if has_init:
        in_specs.append(state_blk)
        args.append(s0)
    o_shape = jax.ShapeDtypeStruct((nb * length, DN_WIDTH), F32)
    out_shape = [o_shape, o_shape]
    out_specs = [pl.BlockSpec((DN_CHUNK, DN_WIDTH), lambda b, c: (b * nc + c, 0)),
                 pl.BlockSpec((DN_CHUNK, DN_WIDTH), lambda b, c: (b * nc + (nc - 1 - c), 0))]
    if want_final:
        out_shape.append(jax.ShapeDtypeStruct((nb, N_DIRS, DN_HEADS, DN_DK, DN_DV), F32))
        out_specs.append(state_blk)
    return pl.pallas_call(
        functools.partial(_dn_scan_kernel, has_init=has_init, want_final=want_final),
        out_shape=tuple(out_shape),
        grid=(nb, nc),
        in_specs=in_specs,
        out_specs=tuple(out_specs),
        scratch_shapes=[pltpu.VMEM((N_DIRS * DN_HEADS, DN_DK, DN_DV), F32)],
        compiler_params=_params("arbitrary", "arbitrary"),
        name="deltanet_scan",
    )(*args)


DN_MERGE_TM = 256


def _dn_merge_kernel(of_ref, ob_ref, gate_ref, g_ref, o_ref):
    for h in range(DN_HEADS):
        sl = slice(h * DN_DV, (h + 1) * DN_DV)
        o = of_ref[:, sl] + ob_ref[:, sl]
        o = o * lax.rsqrt(jnp.mean(o * o, axis=-1, keepdims=True) + EPS) * g_ref[...]
        o_ref[:, sl] = (o * _silu(gate_ref[:, sl])).astype(o_ref.dtype)


def _dn_merge(o_f, o_b, z1, onorm_g, n_rows, row0):
    tm = DN_MERGE_TM
    blk = pl.BlockSpec((tm, DN_WIDTH), lambda i: (i, 0))
    return pl.pallas_call(
        _dn_merge_kernel,
        out_shape=jax.ShapeDtypeStruct((n_rows, DN_WIDTH), BF16),
        grid=(n_rows // tm,),
        in_specs=[blk, blk,
                  pl.BlockSpec((tm, DN_WIDTH), lambda i: (row0 // tm + i, DN_GATE_COL0 // DN_WIDTH)),
                  pl.BlockSpec((1, DN_DV), lambda i: (0, 0))],
        out_specs=blk,
        compiler_params=_params("arbitrary"),
        name="deltanet_merge",
    )(o_f, o_b, z1, onorm_g.reshape(1, DN_DV))


def _deltanet_stream(z1, conv_w, a_log, dt_bias, onorm_g, s0, nb, length, row0, want_final):
    row_blk0 = row0 // length
    qn, kn, vn = _dn_prep(z1, conv_w, nb, length, row_blk0)
    gates = _dn_gates(z1, a_log, dt_bias, nb, length, row_blk0)
    outs = _dn_scan(qn, kn, vn, gates, s0, nb, length, want_final)
    merged = _dn_merge(outs[0], outs[1], z1, onorm_g, nb * length, row0)
    return merged, (outs[2] if want_final else None)


def kernel(x_prompt, x_sample, cache_nat_k, cache_nat_v, cache_diff_k, cache_diff_v, state_delta, c, c_ctx, w_in_even, w_out_even, nat_bias, diff_lam, diff_subln, w_in_odd, conv_odd, a_log_odd, dt_bias_odd, onorm_odd, w_out_odd, ada_w, ada_b, ln_g, ln_b, router_w, router_b, moe_wg, moe_wu, moe_wd):
    x = jnp.concatenate([x_prompt.reshape(N_PROMPT, D_MODEL), x_sample.reshape(N_SAMPLE, D_MODEL)], axis=0)
    cond = jnp.concatenate([c_ctx[None], c, jnp.zeros((N_COND - 1 - DEC_BATCH, D_MODEL), F32)], axis=0)
    mod = _ada_modulation(cond, ada_w, ada_b).reshape(DEPTH, N_COND, 6, D_MODEL)
    router_w_t = router_w.T

    lam_init = 0.8 - 0.6 * math.exp(-0.3 * 0)
    z = _input_projection(x, mod[0], w_in_even[0].astype(BF16))
    new_nat_k = z[:N_PROMPT, _NAT_K0:_NAT_V0].reshape(BATCH, 1, SEQ, NAT_HEADS, HEAD_DIM)
    new_nat_v = z[:N_PROMPT, _NAT_V0:_DIFF_Q0].reshape(BATCH, 1, SEQ, NAT_HEADS, HEAD_DIM)
    new_diff_k = z[:N_PROMPT, _DIFF_K0:_DIFF_V0].reshape(BATCH, 1, SEQ, DIFF_HEADS, 2, HEAD_DIM)
    new_diff_v = z[:N_PROMPT, _DIFF_V0:].reshape(BATCH, 1, SEQ, DIFF_HEADS, 2 * HEAD_DIM)
    o_ctx = _context_attention(z, diff_lam[0], diff_subln[0], lam_init)
    o_nat = _nat_latent_attention(z, cache_nat_k[:, 0].reshape(DEC_BATCH, PAST_LEN, NAT_WIDTH),
                                  cache_nat_v[:, 0].reshape(DEC_BATCH, PAST_LEN, NAT_WIDTH),
                                  _nat_bias_table(nat_bias[0]))
    o_diff = _diff_latent_attention(z, cache_diff_k[:, 0].reshape(DEC_BATCH, PAST_LEN, DIFF_WIDTH),
                                    cache_diff_v[:, 0].reshape(DEC_BATCH, PAST_LEN, DIFF_WIDTH),
                                    diff_lam[0], diff_subln[0], lam_init)
    o = jnp.concatenate([o_ctx, jnp.concatenate([o_nat, o_diff], axis=1)], axis=0)
    x1, h2, gates = _output_projection(o, x, mod[0], w_out_even[0].astype(BF16), ln_g[0, 0], ln_b[0, 0],
                                       router_w_t, router_b)
    x = _moe_ffn(h2, gates, moe_wg[0].astype(BF16), moe_wu[0].astype(BF16), moe_wd[0].astype(BF16),
                 x1, mod[0], ln_g[0, 1], ln_b[0, 1])

    w1 = w_in_odd[0]
    w1 = jnp.concatenate([w1[:, :DN_CONV_CH], w1[:, DN_CONV_CH + 4 * DN_HEADS:],
                          w1[:, DN_CONV_CH:DN_CONV_CH + 4 * DN_HEADS],
                          jnp.zeros((D_MODEL, LANES - 4 * DN_HEADS), F32)], axis=1).astype(BF16)
    z1 = _input_projection(x, mod[1], w1, tm=256)
    o_p, new_state = _deltanet_stream(z1, conv_odd[0], a_log_odd[0], dt_bias_odd[0], onorm_odd[0],
                                      None, BATCH, SEQ, 0, True)
    o_s, _ = _deltanet_stream(z1, conv_odd[0], a_log_odd[0], dt_bias_odd[0], onorm_odd[0],
                              state_delta[:, 0], DEC_BATCH, DEC_SEQ, N_PROMPT, False)
    o = jnp.concatenate([o_p, o_s], axis=0)
    x1, h2, gates = _output_projection(o, x, mod[1], w_out_odd[0].astype(BF16), ln_g[1, 0], ln_b[1, 0],
                                       router_w_t, router_b)
    x = _moe_ffn(h2, gates, moe_wg[1].astype(BF16), moe_wu[1].astype(BF16), moe_wd[1].astype(BF16),
                 x1, mod[1], ln_g[1, 1], ln_b[1, 1])

    y_prompt = x[:N_PROMPT].reshape(BATCH, SEQ, D_MODEL)
    y_sample = x[N_PROMPT:].reshape(DEC_BATCH, DEC_SEQ, D_MODEL)
    return (y_prompt, y_sample, new_nat_k, new_nat_v, new_diff_k, new_diff_v, new_state[:, None])
```

```python
import functools
import math

import jax
import jax.numpy as jnp
import numpy as np
from jax import lax
from jax.experimental import pallas as pl
from jax.experimental.pallas import tpu as pltpu

F32 = jnp.float32
BF16 = jnp.bfloat16

D_MODEL = 1024
BATCH = 16
SEQ = 256
DEC_BATCH = 8
DEC_SEQ = 1024
PAST_LEN = 256
GRID_W = 64
N_ROWS = DEC_SEQ // GRID_W

HEAD_DIM = 64
NAT_HEADS = 8
NAT_WIN_H = 8
NAT_WIN_W = 16
DIFF_HEADS = 4
NAT_WIDTH = NAT_HEADS * HEAD_DIM
DIFF_WIDTH = DIFF_HEADS * 2 * HEAD_DIM
EVEN_IN = 3 * NAT_WIDTH + 3 * DIFF_WIDTH
ROPE_THETA = 10000.0

DN_HEADS = 8
DN_DK = 128
DN_DV = 128
DN_QK_WIDTH = DN_HEADS * DN_DK
DN_WIDTH = DN_HEADS * DN_DV
DN_CONV_CH = 2 * DN_QK_WIDTH + DN_WIDTH
DN_CHUNK = 64

N_EXPERTS = 16
N_GROUPS = 4
EXPERTS_PER_GROUP = N_EXPERTS // N_GROUPS
EXPERT_FF = 512

DEPTH = 2
ALPHA = (2 * DEPTH) ** 0.25
EPS = 1e-5

N_PROMPT = BATCH * SEQ
N_SAMPLE = DEC_BATCH * DEC_SEQ
N_TOK = N_PROMPT + N_SAMPLE
N_COND = 16

VMEM_LIMIT = 56 * 1024 * 1024
LANES = 128


def _params(*sem):
    return pltpu.CompilerParams(dimension_semantics=sem, vmem_limit_bytes=VMEM_LIMIT)


def _cond_row(row0):
    return jnp.where(row0 < N_PROMPT, 0, 1 + (row0 - N_PROMPT) // DEC_SEQ)


def _silu(x):
    return x * jax.nn.sigmoid(x)


def _layer_norm(r, g, b):
    mu = jnp.mean(r, axis=-1, keepdims=True)
    xc = r - mu
    var = jnp.mean(xc * xc, axis=-1, keepdims=True)
    return xc * lax.rsqrt(var + EPS) * g + b


def _dot(a, b):
    return jnp.dot(a, b, preferred_element_type=F32)


def _dot_nt(a, b):
    return lax.dot_general(a, b, (((1,), (1,)), ((), ())), preferred_element_type=F32)


def _dot_tn(a, b):
    return lax.dot_general(a, b, (((0,), (0,)), ((), ())), preferred_element_type=F32)


ADA_TN = 1536


def _ada_kernel(c_ref, w_ref, b_ref, o_ref):
    s = _silu(c_ref[...])
    o_ref[0] = _dot(s.astype(BF16), w_ref[0].astype(BF16)) + b_ref[0]


def _ada_modulation(cond, ada_w, ada_b):
    n = 6 * D_MODEL
    return pl.pallas_call(
        _ada_kernel,
        out_shape=jax.ShapeDtypeStruct((DEPTH, N_COND, n), F32),
        grid=(DEPTH, n // ADA_TN),
        in_specs=[
            pl.BlockSpec((N_COND, D_MODEL), lambda l, j: (0, 0)),
            pl.BlockSpec((1, D_MODEL, ADA_TN), lambda l, j: (l, 0, j)),
            pl.BlockSpec((1, 1, ADA_TN), lambda l, j: (l, 0, j)),
        ],
        out_specs=pl.BlockSpec((1, N_COND, ADA_TN), lambda l, j: (l, 0, j)),
        compiler_params=_params("arbitrary", "arbitrary"),
        name="ada_modulation",
    )(cond, ada_w, ada_b.reshape(DEPTH, 1, n))


PROJ_TM = 512


def _inproj_kernel(x_ref, mod_ref, w_ref, o_ref):
    m = mod_ref[0]
    h = x_ref[...] * (1.0 + m[1:2]) + m[0:1]
    o_ref[...] = _dot(h.astype(BF16), w_ref[...])


def _input_projection(x, mod, w_bf16, tm=PROJ_TM):
    n = w_bf16.shape[1]
    return pl.pallas_call(
        _inproj_kernel,
        out_shape=jax.ShapeDtypeStruct((N_TOK, n), F32),
        grid=(N_TOK // tm,),
        in_specs=[
            pl.BlockSpec((tm, D_MODEL), lambda i: (i, 0)),
            pl.BlockSpec((1, 6, D_MODEL), lambda i: (_cond_row(i * tm), 0, 0)),
            pl.BlockSpec((D_MODEL, n), lambda i: (0, 0)),
        ],
        out_specs=pl.BlockSpec((tm, n), lambda i: (i, 0)),
        compiler_params=_params("arbitrary"),
        name="input_projection",
    )(x, mod, w_bf16)


_NAT_Q0, _NAT_K0, _NAT_V0 = 0, NAT_WIDTH, 2 * NAT_WIDTH
_DIFF_Q0 = 3 * NAT_WIDTH
_DIFF_K0 = _DIFF_Q0 + DIFF_WIDTH
_DIFF_V0 = _DIFF_K0 + DIFF_WIDTH
ATTN_SCALE = HEAD_DIM ** -0.5


def _diff_lambda(lam_ref, lam_init):
    lp = lam_ref[...]
    return (jnp.exp(jnp.sum(lp[0:1] * lp[1:2], axis=-1, keepdims=True))
            - jnp.exp(jnp.sum(lp[2:3] * lp[3:4], axis=-1, keepdims=True)) + lam_init)


def _softmax_parts(s):
    m = jnp.max(s, axis=-1, keepdims=True)
    e = jnp.exp(s - m)
    return e, jnp.sum(e, axis=-1, keepdims=True)


def _sub_norm(o, g, lam_init):
    ms = jnp.mean(o * o, axis=-1, keepdims=True)
    return o * lax.rsqrt(ms + EPS) * g * (1.0 - lam_init)


def _ctx_attn_kernel(z_ref, lam_ref, subg_ref, o_ref, *, lam_init):
    lam = _diff_lambda(lam_ref, lam_init)
    for h in range(NAT_HEADS):
        c = h * HEAD_DIM
        q = z_ref[:, _NAT_Q0 + c:_NAT_Q0 + c + HEAD_DIM].astype(BF16)
        k = z_ref[:, _NAT_K0 + c:_NAT_K0 + c + HEAD_DIM].astype(BF16)
        v = z_ref[:, _NAT_V0 + c:_NAT_V0 + c + HEAD_DIM].astype(BF16)
        e, l = _softmax_parts(_dot_nt(q, k) * ATTN_SCALE)
        o_ref[:, c:c + HEAD_DIM] = (_dot(e.astype(BF16), v) / l).astype(o_ref.dtype)
    for h in range(DIFF_HEADS):
        c = h * 2 * HEAD_DIM
        parts = []
        v = z_ref[:, _DIFF_V0 + c:_DIFF_V0 + c + 2 * HEAD_DIM].astype(BF16)
        for j in range(2):
            cj = c + j * HEAD_DIM
            q = z_ref[:, _DIFF_Q0 + cj:_DIFF_Q0 + cj + HEAD_DIM].astype(BF16)
            k = z_ref[:, _DIFF_K0 + cj:_DIFF_K0 + cj + HEAD_DIM].astype(BF16)
            e, l = _softmax_parts(_dot_nt(q, k) * ATTN_SCALE)
            parts.append(_dot(e.astype(BF16), v) / l)
        o = parts[0] - lam * parts[1]
        o_ref[:, NAT_WIDTH + c:NAT_WIDTH + c + 2 * HEAD_DIM] = (
            _sub_norm(o, subg_ref[...], lam_init).astype(o_ref.dtype))


def _context_attention(z, lam_p, subln_g, lam_init):
    return pl.pallas_call(
        functools.partial(_ctx_attn_kernel, lam_init=lam_init),
        out_shape=jax.ShapeDtypeStruct((N_PROMPT, D_MODEL), BF16),
        grid=(BATCH,),
        in_specs=[
            pl.BlockSpec((SEQ, EVEN_IN), lambda b: (b, 0)),
            pl.BlockSpec((4, HEAD_DIM), lambda b: (0, 0)),
            pl.BlockSpec((1, 2 * HEAD_DIM), lambda b: (0, 0)),
        ],
        out_specs=pl.BlockSpec((SEQ, D_MODEL), lambda b: (b, 0)),
        compiler_params=_params("arbitrary"),
        name="context_attention",
    )(z, lam_p, subln_g.reshape(1, 2 * HEAD_DIM))


NAT_TQ = 256
_SAMPLE_BLK0 = N_PROMPT // DEC_SEQ


def _nat_bias_table(rel_bias):
    rows = N_ROWS
    kh = min(NAT_WIN_H, rows)
    kw = NAT_WIN_W
    col = np.arange(GRID_W)
    c_start = np.clip(col - kw // 2, 0, GRID_W - kw)
    col_ok = (col[None, :] >= c_start[:, None]) & (col[None, :] < c_start[:, None] + kw)
    dc = np.clip(col[None, :] - col[:, None], 1 - kw, kw - 1) + (NAT_WIN_W - 1)
    onehot = ((dc[None] == np.arange(2 * kw - 1)[:, None, None]) & col_ok[None]).astype(np.float32)
    tiles = jnp.einsum('hrd,dqk->hrqk', rel_bias.astype(F32), onehot, precision=lax.Precision.HIGHEST)
    tiles = jnp.where(col_ok, tiles, -jnp.inf)
    outside = jnp.full((NAT_HEADS, GRID_W, GRID_W), -jnp.inf, F32)
    out_rows = []
    for r in range(rows):
        start = min(max(r - kh // 2, 0), rows - kh)
        blocks = [tiles[:, a - r + NAT_WIN_H - 1] if start <= a < start + kh else outside
                  for a in range(rows)]
        out_rows.append(jnp.concatenate(blocks, axis=-1))
    return jnp.stack(out_rows, axis=1).reshape(NAT_HEADS, DEC_SEQ, DEC_SEQ)


def _nat_latent_kernel(q_ref, k_ref, v_ref, ck_ref, cv_ref, bias_ref, o_ref):
    for j in range(2):
        sl = slice(j * HEAD_DIM, (j + 1) * HEAD_DIM)
        q = q_ref[:, sl].astype(BF16)
        s_loc = _dot_nt(q, k_ref[:, sl].astype(BF16)) * ATTN_SCALE + bias_ref[j]
        s_ctx = _dot_nt(q, ck_ref[0, :, sl].astype(BF16)) * ATTN_SCALE
        m = jnp.maximum(jnp.max(s_loc, axis=-1, keepdims=True), jnp.max(s_ctx, axis=-1, keepdims=True))
        e_loc = jnp.exp(s_loc - m)
        e_ctx = jnp.exp(s_ctx - m)
        l = jnp.sum(e_loc, axis=-1, keepdims=True) + jnp.sum(e_ctx, axis=-1, keepdims=True)
        o = (_dot(e_loc.astype(BF16), v_ref[:, sl].astype(BF16))
             + _dot(e_ctx.astype(BF16), cv_ref[0, :, sl].astype(BF16)))
        o_ref[:, sl] = (o / l).astype(o_ref.dtype)


def _nat_latent_attention(z, ck, cv, bias):
    nq = DEC_SEQ // NAT_TQ
    qblk0 = N_PROMPT // NAT_TQ
    pair = 2 * HEAD_DIM
    return pl.pallas_call(
        _nat_latent_kernel,
        out_shape=jax.ShapeDtypeStruct((N_SAMPLE, NAT_WIDTH), BF16),
        grid=(NAT_HEADS // 2, nq, DEC_BATCH),
        in_specs=[
            pl.BlockSpec((NAT_TQ, pair), lambda hp, qt, b: (qblk0 + b * nq + qt, _NAT_Q0 // pair + hp)),
            pl.BlockSpec((DEC_SEQ, pair), lambda hp, qt, b: (_SAMPLE_BLK0 + b, _NAT_K0 // pair + hp)),
            pl.BlockSpec((DEC_SEQ, pair), lambda hp, qt, b: (_SAMPLE_BLK0 + b, _NAT_V0 // pair + hp)),
            pl.BlockSpec((1, PAST_LEN, pair), lambda hp, qt, b: (b, 0, hp)),
            pl.BlockSpec((1, PAST_LEN, pair), lambda hp, qt, b: (b, 0, hp)),
            pl.BlockSpec((2, NAT_TQ, DEC_SEQ), lambda hp, qt, b: (hp, qt, 0)),
        ],
        out_specs=pl.BlockSpec((NAT_TQ, pair), lambda hp, qt, b: (b * nq + qt, hp)),
        compiler_params=_params("arbitrary", "arbitrary", "arbitrary"),
        name="nat_latent_attention",
    )(z, z, z, ck, cv, bias)


DIFF_TQ = 256


def _rope_tables():
    nf = HEAD_DIM // 4
    t = jnp.arange(DEC_SEQ)
    inv = ROPE_THETA ** (-jnp.arange(nf, dtype=F32) / nf)
    ang_r = (t // GRID_W).astype(F32)[:, None] * inv
    ang_c = (t % GRID_W).astype(F32)[:, None] * inv
    cos = jnp.concatenate([jnp.cos(ang_r)] * 2 + [jnp.cos(ang_c)] * 2, axis=-1)
    sin = jnp.concatenate([-jnp.sin(ang_r), jnp.sin(ang_r), -jnp.sin(ang_c), jnp.sin(ang_c)], axis=-1)
    return jnp.tile(cos, (1, 2)), jnp.tile(sin, (1, 2))


def _rope(x, cos, sin):
    nf = HEAD_DIM // 4
    lane = lax.broadcasted_iota(jnp.int32, x.shape, 1)
    upper = (lane // nf) % 2 == 1
    partner = jnp.where(upper, pltpu.roll(x, nf, axis=1), pltpu.roll(x, x.shape[1] - nf, axis=1))
    return x * cos + partner * sin


def _diff_latent_kernel(q_ref, k_ref, v_ref, ck_ref, cv_ref, cos_ref, sin_ref, lam_ref, subg_ref,
                        o_ref, kr_ref, *, lam_init):
    qt = pl.program_id(2)

    @pl.when(qt == 0)
    def _():
        kr_ref[...] = _rope(k_ref[...], cos_ref[...], sin_ref[...]).astype(BF16)

    lam = _diff_lambda(lam_ref, lam_init)
    row0 = pl.multiple_of(qt * DIFF_TQ, DIFF_TQ)
    q = _rope(q_ref[...], cos_ref[pl.ds(row0, DIFF_TQ), :], sin_ref[pl.ds(row0, DIFF_TQ), :]).astype(BF16)
    v_loc = v_ref[...].astype(BF16)
    v_ctx = cv_ref[0].astype(BF16)
    parts = []
    for j in range(2):
        sl = slice(j * HEAD_DIM, (j + 1) * HEAD_DIM)
        s_ctx = _dot_nt(q[:, sl], ck_ref[0, :, sl].astype(BF16)) * ATTN_SCALE
        s_loc = _dot_nt(q[:, sl], kr_ref[:, sl]) * ATTN_SCALE
        m = jnp.maximum(jnp.max(s_loc, axis=-1, keepdims=True), jnp.max(s_ctx, axis=-1, keepdims=True))
        e_loc = jnp.exp(s_loc - m)
        e_ctx = jnp.exp(s_ctx - m)
        l = jnp.sum(e_loc, axis=-1, keepdims=True) + jnp.sum(e_ctx, axis=-1, keepdims=True)
        parts.append((_dot(e_loc.astype(BF16), v_loc) + _dot(e_ctx.astype(BF16), v_ctx)) / l)
    o = parts[0] - lam * parts[1]
    o_ref[...] = _sub_norm(o, subg_ref[...], lam_init).astype(o_ref.dtype)


def _diff_latent_attention(z, ck, cv, lam_p, subln_g, lam_init):
    nq = DEC_SEQ // DIFF_TQ
    qblk0 = N_PROMPT // DIFF_TQ
    w = 2 * HEAD_DIM
    cos, sin = _rope_tables()
    return pl.pallas_call(
        functools.partial(_diff_latent_kernel, lam_init=lam_init),
        out_shape=jax.ShapeDtypeStruct((N_SAMPLE, DIFF_WIDTH), BF16),
        grid=(DEC_BATCH, DIFF_HEADS, nq),
        in_specs=[
            pl.BlockSpec((DIFF_TQ, w), lambda b, h, qt: (qblk0 + b * nq + qt, _DIFF_Q0 // w + h)),
            pl.BlockSpec((DEC_SEQ, w), lambda b, h, qt: (_SAMPLE_BLK0 + b, _DIFF_K0 // w + h)),
            pl.BlockSpec((DEC_SEQ, w), lambda b, h, qt: (_SAMPLE_BLK0 + b, _DIFF_V0 // w + h)),
            pl.BlockSpec((1, PAST_LEN, w), lambda b, h, qt: (b, 0, h)),
            pl.BlockSpec((1, PAST_LEN, w), lambda b, h, qt: (b, 0, h)),
            pl.BlockSpec((DEC_SEQ, w), lambda b, h, qt: (0, 0)),
            pl.BlockSpec((DEC_SEQ, w), lambda b, h, qt: (0, 0)),
            pl.BlockSpec((4, HEAD_DIM), lambda b, h, qt: (0, 0)),
            pl.BlockSpec((1, w), lambda b, h, qt: (0, 0)),
        ],
        out_specs=pl.BlockSpec((DIFF_TQ, w), lambda b, h, qt: (b * nq + qt, h)),
        scratch_shapes=[pltpu.VMEM((DEC_SEQ, w), BF16)],
        compiler_params=_params("arbitrary", "arbitrary", "arbitrary"),
        name="diff_latent_attention",
    )(z, z, z, ck, cv, cos, sin, lam_p, subln_g.reshape(1, w))


OUT_TM = 256


def _route(logits_t, rb_col):
    sc = jax.nn.sigmoid(logits_t)
    bi = sc + rb_col
    srow = [sc[e:e + 1] for e in range(N_EXPERTS)]
    brow = [bi[e:e + 1] for e in range(N_EXPERTS)]
    n = EXPERTS_PER_GROUP
    gscore = []
    for g in range(N_GROUPS):
        v = brow[g * n:(g + 1) * n]
        best = None
        for i in range(n):
            for j in range(i + 1, n):
                s = v[i] + v[j]
                best = s if best is None else jnp.maximum(best, s)
        gscore.append(best)
    sel = jnp.zeros_like(gscore[0], dtype=jnp.int32)
    best = gscore[0]
    for g in range(1, N_GROUPS):
        better = gscore[g] > best
        sel = jnp.where(better, g, sel)
        best = jnp.where(better, gscore[g], best)

    def pick_group(rows, i):
        out = rows[(N_GROUPS - 1) * n + i]
        for g in range(N_GROUPS - 2, -1, -1):
            out = jnp.where(sel == g, rows[g * n + i], out)
        return out

    bv = [pick_group(brow, i) for i in range(n)]
    sv = [pick_group(srow, i) for i in range(n)]
    i1 = jnp.zeros_like(sel)
    m1, w1 = bv[0], sv[0]
    for i in range(1, n):
        better = bv[i] > m1
        i1 = jnp.where(better, i, i1)
        m1 = jnp.where(better, bv[i], m1)
        w1 = jnp.where(better, sv[i], w1)
    i2 = jnp.zeros_like(sel)
    m2 = jnp.full_like(m1, -jnp.inf)
    w2 = jnp.zeros_like(w1)
    for i in range(n):
        better = (i1 != i) & (bv[i] > m2)
        i2 = jnp.where(better, i, i2)
        m2 = jnp.where(better, bv[i], m2)
        w2 = jnp.where(better, sv[i], w2)
    tot = w1 + w2
    w1, w2 = w1 / tot, w2 / tot
    e1 = sel * n + i1
    e2 = sel * n + i2
    eid = lax.broadcasted_iota(jnp.int32, logits_t.shape, 0)
    return jnp.where(eid == e1, w1, 0.0) + jnp.where(eid == e2, w2, 0.0)


def _outproj_kernel(o_ref, x_ref, mod_ref, w_ref, lng_ref, lnb_ref, rw_ref, rb_ref,
                    x1_ref, h2_ref, gates_ref):
    m = mod_ref[0]
    y = _dot(o_ref[...], w_ref[...])
    x1 = _layer_norm(ALPHA * x_ref[...] + m[2:3] * y, lng_ref[...], lnb_ref[...])
    x1_ref[...] = x1
    h2 = x1 * (1.0 + m[4:5]) + m[3:4]
    h2_ref[...] = h2.astype(BF16)
    logits_t = lax.dot_general(rw_ref[...], h2, (((1,), (1,)), ((), ())),
                               precision=lax.Precision.HIGHEST, preferred_element_type=F32)
    gates_t = _route(logits_t, rb_ref[...])
    pad = jnp.zeros((LANES - N_EXPERTS, gates_t.shape[1]), F32)
    gates_ref[...] = jnp.concatenate([gates_t, pad], axis=0).T


def _output_projection(o, x, mod, w_bf16, ln_g, ln_b, router_w_t, router_b, tm=OUT_TM):
    return pl.pallas_call(
        _outproj_kernel,
        out_shape=(jax.ShapeDtypeStruct((N_TOK, D_MODEL), F32),
                   jax.ShapeDtypeStruct((N_TOK, D_MODEL), BF16),
                   jax.ShapeDtypeStruct((N_TOK, LANES), F32)),
        grid=(N_TOK // tm,),
        in_specs=[
            pl.BlockSpec((tm, D_MODEL), lambda i: (i, 0)),
            pl.BlockSpec((tm, D_MODEL), lambda i: (i, 0)),
            pl.BlockSpec((1, 6, D_MODEL), lambda i: (_cond_row(i * tm), 0, 0)),
            pl.BlockSpec((D_MODEL, D_MODEL), lambda i: (0, 0)),
            pl.BlockSpec((1, D_MODEL), lambda i: (0, 0)),
            pl.BlockSpec((1, D_MODEL), lambda i: (0, 0)),
            pl.BlockSpec((N_EXPERTS, D_MODEL), lambda i: (0, 0)),
            pl.BlockSpec((N_EXPERTS, 1), lambda i: (0, 0)),
        ],
        out_specs=(pl.BlockSpec((tm, D_MODEL), lambda i: (i, 0)),
                   pl.BlockSpec((tm, D_MODEL), lambda i: (i, 0)),
                   pl.BlockSpec((tm, LANES), lambda i: (i, 0))),
        compiler_params=_params("arbitrary"),
        name="output_projection",
    )(o, x, mod, w_bf16, ln_g.reshape(1, D_MODEL), ln_b.reshape(1, D_MODEL),
      router_w_t, router_b.reshape(N_EXPERTS, 1))


MOE_TM = 512


def _moe_kernel(h_ref, gates_ref, wg_ref, wu_ref, wd_ref, x1_ref, mod_ref, lng_ref, lnb_ref,
                o_ref, acc_ref):
    e = pl.program_id(1)

    @pl.when(e == 0)
    def _():
        acc_ref[...] = jnp.zeros_like(acc_ref)

    h = h_ref[...]
    he = _silu(_dot(h, wg_ref[0])) * _dot(h, wu_ref[0])
    lane = lax.broadcasted_iota(jnp.int32, gates_ref.shape, 1)
    g = jnp.sum(jnp.where(lane == e, gates_ref[...], 0.0), axis=-1, keepdims=True)
    acc_ref[...] += g * _dot(he.astype(BF16), wd_ref[0])

    @pl.when(e == N_EXPERTS - 1)
    def _():
        m = mod_ref[0]
        o_ref[...] = _layer_norm(ALPHA * x1_ref[...] + m[5:6] * acc_ref[...], lng_ref[...], lnb_ref[...])


def _moe_ffn(h2, gates, wg, wu, wd, x1, mod, ln_g, ln_b, tm=MOE_TM):
    return pl.pallas_call(
        _moe_kernel,
        out_shape=jax.ShapeDtypeStruct((N_TOK, D_MODEL), F32),
        grid=(N_TOK // tm, N_EXPERTS),
        in_specs=[
            pl.BlockSpec((tm, D_MODEL), lambda i, e: (i, 0)),
            pl.BlockSpec((tm, LANES), lambda i, e: (i, 0)),
            pl.BlockSpec((1, D_MODEL, EXPERT_FF), lambda i, e: (e, 0, 0)),
            pl.BlockSpec((1, D_MODEL, EXPERT_FF), lambda i, e: (e, 0, 0)),
            pl.BlockSpec((1, EXPERT_FF, D_MODEL), lambda i, e: (e, 0, 0)),
            pl.BlockSpec((tm, D_MODEL), lambda i, e: (i, 0)),
            pl.BlockSpec((1, 6, D_MODEL), lambda i, e: (_cond_row(i * tm), 0, 0)),
            pl.BlockSpec((1, D_MODEL), lambda i, e: (0, 0)),
            pl.BlockSpec((1, D_MODEL), lambda i, e: (0, 0)),
        ],
        out_specs=pl.BlockSpec((tm, D_MODEL), lambda i, e: (i, 0)),
        scratch_shapes=[pltpu.VMEM((tm, D_MODEL), F32)],
        compiler_params=_params("arbitrary", "arbitrary"),
        name="moe_ffn",
    )(h2, gates, wg, wu, wd, x1, mod, ln_g.reshape(1, D_MODEL), ln_b.reshape(1, D_MODEL))


DN_GATE_COL0 = DN_CONV_CH
DN_AB_COL0 = DN_CONV_CH + DN_WIDTH
DN_IN = DN_AB_COL0 + LANES
N_DIRS = 2


def _dn_prep_kernel(q_ref, k_ref, v_ref, wq_ref, wk_ref, wv_ref, qo_ref, ko_ref, vo_ref):
    length = q_ref.shape[0]
    row = lax.broadcasted_iota(jnp.int32, (length, DN_DK), 0)

    def conv(x, w):
        prev = jnp.where(row == 0, 0.0, pltpu.roll(x, 1, axis=0))
        nxt = jnp.where(row == length - 1, 0.0, pltpu.roll(x, length - 1, axis=0))
        return _silu(prev * w[0:1] + x * w[1:2] + nxt * w[2:3])

    def l2n(x):
        return x * lax.rsqrt(jnp.sum(x * x, axis=-1, keepdims=True) + 1e-6)

    qo_ref[0] = l2n(conv(q_ref[...], wq_ref[...])) * (DN_DK ** -0.5)
    ko_ref[0] = l2n(conv(k_ref[...], wk_ref[...]))
    vo_ref[0] = conv(v_ref[...], wv_ref[...])


def _dn_prep(z1, conv_w, nb, length, row_blk0):
    hq, hk, hv = 0, DN_HEADS, 2 * DN_HEADS
    shp = jax.ShapeDtypeStruct((DN_HEADS, nb * length, DN_DK), F32)
    blk = lambda off: pl.BlockSpec((length, DN_DK), lambda b, h: (row_blk0 + b, off + h))
    wblk = lambda off: pl.BlockSpec((3, DN_DK), lambda b, h: (0, off + h))
    oblk = pl.BlockSpec((1, length, DN_DK), lambda b, h: (h, b, 0))
    return pl.pallas_call(
        _dn_prep_kernel,
        out_shape=(shp, shp, shp),
        grid=(nb, DN_HEADS),
        in_specs=[blk(hq), blk(hk), blk(hv), wblk(hq), wblk(hk), wblk(hv)],
        out_specs=(oblk, oblk, oblk),
        compiler_params=_params("arbitrary", "arbitrary"),
        name="deltanet_prep",
    )(z1, z1, z1, conv_w, conv_w, conv_w)


def _dn_gates_kernel(ab_ref, alog_ref, dtb_ref, o_ref):
    length = ab_ref.shape[0]
    x = ab_ref[...]
    a = x + dtb_ref[...]
    softplus = jnp.maximum(a, 0.0) + jnp.log1p(jnp.exp(-jnp.abs(a)))
    g = -jnp.exp(alog_ref[...]) * softplus
    beta = jax.nn.sigmoid(x)
    ri = lax.broadcasted_iota(jnp.int32, (DN_CHUNK, DN_CHUNK), 0)
    ci = lax.broadcasted_iota(jnp.int32, (DN_CHUNK, DN_CHUNK), 1)
    tril = (ri >= ci).astype(F32)
    triu = (ri <= ci).astype(F32)
    lane = lax.broadcasted_iota(jnp.int32, (DN_CHUNK, LANES), 1)
    for c in range(length // DN_CHUNK):
        rows = slice(c * DN_CHUNK, (c + 1) * DN_CHUNK)
        gch = g[rows]
        pre = jnp.dot(tril, gch, precision=lax.Precision.HIGHEST, preferred_element_type=F32)
        suf = jnp.dot(triu, gch, precision=lax.Precision.HIGHEST, preferred_element_type=F32)
        gc = jnp.where(lane < DN_HEADS, pre, suf)
        o_ref[rows, :] = jnp.where(lane < N_DIRS * DN_HEADS, gc, beta[rows])


def _dn_gates(z1, a_log, dt_bias, nb, length, row_blk0):
    pad = LANES - N_DIRS * DN_HEADS
    alog = jnp.pad(a_log.reshape(1, -1).astype(F32), ((0, 0), (0, pad)))
    dtb = jnp.pad(dt_bias.reshape(1, -1).astype(F32), ((0, 0), (0, pad)))
    return pl.pallas_call(
        _dn_gates_kernel,
        out_shape=jax.ShapeDtypeStruct((nb * length, LANES), F32),
        grid=(nb,),
        in_specs=[
            pl.BlockSpec((length, LANES), lambda b: (row_blk0 + b, DN_AB_COL0 // LANES)),
            pl.BlockSpec((1, LANES), lambda b: (0, 0)),
            pl.BlockSpec((1, LANES), lambda b: (0, 0)),
        ],
        out_specs=pl.BlockSpec((length, LANES), lambda b: (b, 0)),
        compiler_params=_params("arbitrary"),
        name="deltanet_gates",
    )(z1, alog, dtb)


def _delta_chunk(q, k, v, beta, gcol, grow, s, lower):
    ri = lax.broadcasted_iota(jnp.int32, (DN_CHUNK, DN_CHUNK), 0)
    ci = lax.broadcasted_iota(jnp.int32, (DN_CHUNK, DN_CHUNK), 1)
    incl = (ri >= ci) if lower else (ri <= ci)
    strict = (ri > ci) if lower else (ri < ci)
    decay = jnp.exp(jnp.where(incl, gcol - grow, -jnp.inf))
    kb = k * beta
    k16 = k.astype(BF16)
    kk = _dot_nt(kb.astype(BF16), k16)
    a_qk = _dot_nt(q.astype(BF16), k16) * decay
    yield
    tri_l = jnp.where(strict, kk * decay, 0.0)

    def off_block(size):
        same = (ri // (2 * size)) == (ci // (2 * size))
        rpar, cpar = (ri // size) % 2, (ci // size) % 2
        return same & ((rpar == 1) & (cpar == 0) if lower else (rpar == 0) & (cpar == 1))

    p = jnp.where(ri == ci, 1.0, 0.0) - jnp.where(off_block(1), tri_l, 0.0)
    size = 2
    while size < DN_CHUNK:
        p16 = p.astype(BF16)
        pc = _dot(p16, jnp.where(off_block(size), tri_l, 0.0).astype(BF16))
        yield
        p = p - _dot(pc.astype(BF16), p16)
        yield
        size *= 2
    eg = jnp.exp(gcol)
    uw = _dot(p.astype(BF16), jnp.concatenate([v * beta, kb * eg], axis=1).astype(BF16))
    yield
    u, w = uw[:, :DN_DV], uw[:, DN_DV:]
    s16 = s.astype(BF16)
    v_new = u - _dot(w.astype(BF16), s16)
    o_inter = _dot((q * eg).astype(BF16), s16)
    yield
    v_new16 = v_new.astype(BF16)
    o = o_inter + _dot(a_qk.astype(BF16), v_new16)
    g_last = gcol[DN_CHUNK - 1:DN_CHUNK] if lower else gcol[0:1]
    kd = k * jnp.exp(g_last - gcol)
    s_new = s * jnp.exp(g_last) + _dot_tn(kd.astype(BF16), v_new16)
    return o, s_new


def _run_interleaved(chains):
    results = [None] * len(chains)
    active = list(enumerate(chains))
    while active:
        still = []
        for idx, gen in active:
            try:
                next(gen)
                still.append((idx, gen))
            except StopIteration as stop:
                results[idx] = stop.value
        active = still
    return results


def _dn_scan_kernel(*refs, has_init, want_final):
    (qf_ref, kf_ref, vf_ref, gf_ref, qb_ref, kb_ref, vb_ref, gb_ref), rest = refs[:8], refs[8:]
    if has_init:
        s0_ref, rest = rest[0], rest[1:]
    of_ref, ob_ref = rest[0], rest[1]
    rest = rest[2:]
    if want_final:
        sf_ref, rest = rest[0], rest[1:]
    s_ref = rest[0]
    c = pl.program_id(1)
    nc = pl.num_programs(1)

    @pl.when(c == 0)
    def _():
        for d in range(N_DIRS):
            for h in range(DN_HEADS):
                if has_init:
                    s_ref[d * DN_HEADS + h] = s0_ref[0, d, h]
                else:
                    s_ref[d * DN_HEADS + h] = jnp.zeros((DN_DK, DN_DV), F32)

    dirs = ((qf_ref, kf_ref, vf_ref, gf_ref, of_ref, True), (qb_ref, kb_ref, vb_ref, gb_ref, ob_ref, False))
    chains = []
    for d, (q_ref, k_ref, v_ref, g_ref, o_ref, lower) in enumerate(dirs):
        gates = g_ref[...]
        gates_t = gates.T
        for h in range(DN_HEADS):
            j = d * DN_HEADS + h
            jb = N_DIRS * DN_HEADS + j
            chains.append(_delta_chunk(q_ref[h], k_ref[h], v_ref[h], gates[:, jb:jb + 1],
                                       gates[:, j:j + 1], gates_t[j:j + 1, :], s_ref[j], lower))
    results = _run_interleaved(chains)
    for d, (_, _, _, _, o_ref, _) in enumerate(dirs):
        for h in range(DN_HEADS):
            j = d * DN_HEADS + h
            o, s_new = results[j]
            o_ref[:, h * DN_DV:(h + 1) * DN_DV] = o
            s_ref[j] = s_new

    if want_final:
        @pl.when(c == nc - 1)
        def _():
            for d in range(N_DIRS):
                for h in range(DN_HEADS):
                    sf_ref[0, d, h] = s_ref[d * DN_HEADS + h]


def _dn_scan(qn, kn, vn, gates, s0, nb, length, want_final):
    nc = length // DN_CHUNK
    has_init = s0 is not None
    fwd = lambda b, c: (0, b * nc + c, 0)
    bwd = lambda b, c: (0, b * nc + (nc - 1 - c), 0)
    hm = lambda im: pl.BlockSpec((DN_HEADS, DN_CHUNK, DN_DK), im)
    in_specs = [hm(fwd), hm(fwd), hm(fwd), pl.BlockSpec((DN_CHUNK, LANES), lambda b, c: (b * nc + c, 0)),
                hm(bwd), hm(bwd), hm(bwd), pl.BlockSpec((DN_CHUNK, LANES), lambda b, c: (b * nc + (nc - 1 - c), 0))]
    args = [qn, kn, vn, gates, qn, kn, vn, gates]
    state_blk = pl.BlockSpec((1, N_DIRS, DN_HEADS, DN_DK, DN_DV), lambda b, c: (b, 0, 0, 0, 0))
    if has_init:
        in_specs.append(state_blk)
        args.append(s0)
    o_shape = jax.ShapeDtypeStruct((nb * length, DN_WIDTH), F32)
    out_shape = [o_shape, o_shape]
    out_specs = [pl.BlockSpec((DN_CHUNK, DN_WIDTH), lambda b, c: (b * nc + c, 0)),
                 pl.BlockSpec((DN_CHUNK, DN_WIDTH), lambda b, c: (b * nc + (nc - 1 - c), 0))]
    if want_final:
        out_shape.append(jax.ShapeDtypeStruct((nb, N_DIRS, DN_HEADS, DN_DK, DN_DV), F32))
        out_specs.append(state_blk)
    return pl.pallas_call(
        functools.partial(_dn_scan_kernel, has_init=has_init, want_final=want_final),
        out_shape=tuple(out_shape),
        grid=(nb, nc),
        in_specs=in_specs,
        out_specs=tuple(out_specs),
        scratch_shapes=[pltpu.VMEM((N_DIRS * DN_HEADS, DN_DK, DN_DV), F32)],
        compiler_params=_params("arbitrary", "arbitrary"),
        name="deltanet_scan",
    )(*args)


DN_MERGE_TM = 256


def _dn_merge_kernel(of_ref, ob_ref, gate_ref, g_ref, o_ref):
    for h in range(DN_HEADS):
        sl = slice(h * DN_DV, (h + 1) * DN_DV)
        o = of_ref[:, sl] + ob_ref[:, sl]
        o = o * lax.rsqrt(jnp.mean(o * o, axis=-1, keepdims=True) + EPS) * g_ref[...]
        o_ref[:, sl] = (o * _silu(gate_ref[:, sl])).astype(o_ref.dtype)


def _dn_merge(o_f, o_b, z1, onorm_g, n_rows, row0):
    tm = DN_MERGE_TM
    blk = pl.BlockSpec((tm, DN_WIDTH), lambda i: (i, 0))
    return pl.pallas_call(
        _dn_merge_kernel,
        out_shape=jax.ShapeDtypeStruct((n_rows, DN_WIDTH), BF16),
        grid=(n_rows // tm,),
        in_specs=[blk, blk,
                  pl.BlockSpec((tm, DN_WIDTH), lambda i: (row0 // tm + i, DN_GATE_COL0 // DN_WIDTH)),
                  pl.BlockSpec((1, DN_DV), lambda i: (0, 0))],
        out_specs=blk,
        compiler_params=_params("arbitrary"),
        name="deltanet_merge",
    )(o_f, o_b, z1, onorm_g.reshape(1, DN_DV))


def _deltanet_stream(z1, conv_w, a_log, dt_bias, onorm_g, s0, nb, length, row0, want_final):
    row_blk0 = row0 // length
    qn, kn, vn = _dn_prep(z1, conv_w, nb, length, row_blk0)
    gates = _dn_gates(z1, a_log, dt_bias, nb, length, row_blk0)
    outs = _dn_scan(qn, kn, vn, gates, s0, nb, length, want_final)
    merged = _dn_merge(outs[0], outs[1], z1, onorm_g, nb * length, row0)
    return merged, (outs[2] if want_final else None)


def kernel(x_prompt, x_sample, cache_nat_k, cache_nat_v, cache_diff_k, cache_diff_v, state_delta, c, c_ctx, w_in_even, w_out_even, nat_bias, diff_lam, diff_subln, w_in_odd, conv_odd, a_log_odd, dt_bias_odd, onorm_odd, w_out_odd, ada_w, ada_b, ln_g, ln_b, router_w, router_b, moe_wg, moe_wu, moe_wd):
    x = jnp.concatenate([x_prompt.reshape(N_PROMPT, D_MODEL), x_sample.reshape(N_SAMPLE, D_MODEL)], axis=0)
    cond = jnp.concatenate([c_ctx[None], c, jnp.zeros((N_COND - 1 - DEC_BATCH, D_MODEL), F32)], axis=0)
    mod = _ada_modulation(cond, ada_w, ada_b).reshape(DEPTH, N_COND, 6, D_MODEL)
    router_w_t = router_w.T

    lam_init = 0.8 - 0.6 * math.exp(-0.3 * 0)
    z = _input_projection(x, mod[0], w_in_even[0].astype(BF16))
    new_nat_k = z[:N_PROMPT, _NAT_K0:_NAT_V0].reshape(BATCH, 1, SEQ, NAT_HEADS, HEAD_DIM)
    new_nat_v = z[:N_PROMPT, _NAT_V0:_DIFF_Q0].reshape(BATCH, 1, SEQ, NAT_HEADS, HEAD_DIM)
    new_diff_k = z[:N_PROMPT, _DIFF_K0:_DIFF_V0].reshape(BATCH, 1, SEQ, DIFF_HEADS, 2, HEAD_DIM)
    new_diff_v = z[:N_PROMPT, _DIFF_V0:].reshape(BATCH, 1, SEQ, DIFF_HEADS, 2 * HEAD_DIM)
    o_ctx = _context_attention(z, diff_lam[0], diff_subln[0], lam_init)
    o_nat = _nat_latent_attention(z, cache_nat_k[:, 0].reshape(DEC_BATCH, PAST_LEN, NAT_WIDTH),
                                  cache_nat_v[:, 0].reshape(DEC_BATCH, PAST_LEN, NAT_WIDTH),
                                  _nat_bias_table(nat_bias[0]))
    o_diff = _diff_latent_attention(z, cache_diff_k[:, 0].reshape(DEC_BATCH, PAST_LEN, DIFF_WIDTH),
                                    cache_diff_v[:, 0].reshape(DEC_BATCH, PAST_LEN, DIFF_WIDTH),
                                    diff_lam[0], diff_subln[0], lam_init)
    o = jnp.concatenate([o_ctx, jnp.concatenate([o_nat, o_diff], axis=1)], axis=0)
    x1, h2, gates = _output_projection(o, x, mod[0], w_out_even[0].astype(BF16), ln_g[0, 0], ln_b[0, 0],
                                       router_w_t, router_b)
    x = _moe_ffn(h2, gates, moe_wg[0].astype(BF16), moe_wu[0].astype(BF16), moe_wd[0].astype(BF16),
                 x1, mod[0], ln_g[0, 1], ln_b[0, 1])

    w1 = w_in_odd[0]
    w1 = jnp.concatenate([w1[:, :DN_CONV_CH], w1[:, DN_CONV_CH + 4 * DN_HEADS:],
                          w1[:, DN_CONV_CH:DN_CONV_CH + 4 * DN_HEADS],
                          jnp.zeros((D_MODEL, LANES - 4 * DN_HEADS), F32)], axis=1).astype(BF16)
    z1 = _input_projection(x, mod[1], w1, tm=256)
    o_p, new_state = _deltanet_stream(z1, conv_odd[0], a_log_odd[0], dt_bias_odd[0], onorm_odd[0],
                                      None, BATCH, SEQ, 0, True)
    o_s, _ = _deltanet_stream(z1, conv_odd[0], a_log_odd[0], dt_bias_odd[0], onorm_odd[0],
                              state_delta[:, 0], DEC_BATCH, DEC_SEQ, N_PROMPT, False)
    o = jnp.concatenate([o_p, o_s], axis=0)
    x1, h2, gates = _output_projection(o, x, mod[1], w_out_odd[0].astype(BF16), ln_g[1, 0], ln_b[1, 0],
                                       router_w_t, router_b)
    x = _moe_ffn(h2, gates, moe_wg[1].astype(BF16), moe_wu[1].astype(BF16), moe_wd[1].astype(BF16),
                 x1, mod[1], ln_g[1, 1], ln_b[1, 1])

    y_prompt = x[:N_PROMPT].reshape(BATCH, SEQ, D_MODEL)
    y_sample = x[N_PROMPT:].reshape(DEC_BATCH, DEC_SEQ, D_MODEL)
    return (y_prompt, y_sample, new_nat_k, new_nat_v, new_diff_k, new_diff_v, new_state[:, None])
```

```python
import functools
import math

import jax
import jax.numpy as jnp
import numpy as np
from jax import lax
from jax.experimental import pallas as pl
from jax.experimental.pallas import tpu as pltpu

F32 = jnp.float32
BF16 = jnp.bfloat16

D_MODEL = 1024
BATCH = 16
SEQ = 256
DEC_BATCH = 8
DEC_SEQ = 1024
PAST_LEN = 256
GRID_W = 64
N_ROWS = DEC_SEQ // GRID_W

HEAD_DIM = 64
NAT_HEADS = 8
NAT_WIN_H = 8
NAT_WIN_W = 16
DIFF_HEADS = 4
NAT_WIDTH = NAT_HEADS * HEAD_DIM
DIFF_WIDTH = DIFF_HEADS * 2 * HEAD_DIM
EVEN_IN = 3 * NAT_WIDTH + 3 * DIFF_WIDTH
ROPE_THETA = 10000.0

DN_HEADS = 8
DN_DK = 128
DN_DV = 128
DN_QK_WIDTH = DN_HEADS * DN_DK
DN_WIDTH = DN_HEADS * DN_DV
DN_CONV_CH = 2 * DN_QK_WIDTH + DN_WIDTH
DN_CHUNK = 64

N_EXPERTS = 16
N_GROUPS = 4
EXPERTS_PER_GROUP = N_EXPERTS // N_GROUPS
EXPERT_FF = 512

DEPTH = 2
ALPHA = (2 * DEPTH) ** 0.25
EPS = 1e-5

N_PROMPT = BATCH * SEQ
N_SAMPLE = DEC_BATCH * DEC_SEQ
N_TOK = N_PROMPT + N_SAMPLE
N_COND = 16

VMEM_LIMIT = 56 * 1024 * 1024
LANES = 128
N_CHUNKS = D_MODEL // LANES


def _params(*sem):
    return pltpu.CompilerParams(dimension_semantics=sem, vmem_limit_bytes=VMEM_LIMIT)


def _cond_row(row0):
    return jnp.where(row0 < N_PROMPT, 0, 1 + (row0 - N_PROMPT) // DEC_SEQ)


def _silu(x):
    return x * jax.nn.sigmoid(x)


def _layer_norm(r, g, b):
    mu = jnp.mean(r, axis=-1, keepdims=True)
    xc = r - mu
    var = jnp.mean(xc * xc, axis=-1, keepdims=True)
    return xc * lax.rsqrt(var + EPS) * g + b


def _dot(a, b):
    return jnp.dot(a, b, preferred_element_type=F32)


def _dot_nt(a, b):
    return lax.dot_general(a, b, (((1,), (1,)), ((), ())), preferred_element_type=F32)


def _dot_tn(a, b):
    return lax.dot_general(a, b, (((0,), (0,)), ((), ())), preferred_element_type=F32)


ADA_TN = 1536


def _ada_kernel(c_ref, w_ref, b_ref, o_ref):
    s = _silu(c_ref[...])
    o_ref[0] = _dot(s.astype(BF16), w_ref[0].astype(BF16)) + b_ref[0]


def _ada_modulation(cond, ada_w, ada_b):
    n = 6 * D_MODEL
    return pl.pallas_call(
        _ada_kernel,
        out_shape=jax.ShapeDtypeStruct((DEPTH, N_COND, n), F32),
        grid=(DEPTH, n // ADA_TN),
        in_specs=[
            pl.BlockSpec((N_COND, D_MODEL), lambda l, j: (0, 0)),
            pl.BlockSpec((1, D_MODEL, ADA_TN), lambda l, j: (l, 0, j)),
            pl.BlockSpec((1, 1, ADA_TN), lambda l, j: (l, 0, j)),
        ],
        out_specs=pl.BlockSpec((1, N_COND, ADA_TN), lambda l, j: (l, 0, j)),
        compiler_params=_params("arbitrary", "arbitrary"),
        name="ada_modulation",
    )(cond, ada_w, ada_b.reshape(DEPTH, 1, n))


PROJ_TM = 512


def _inproj_kernel(x_ref, mod_ref, w_ref, o_ref):
    m = mod_ref[0]
    h = x_ref[...] * (1.0 + m[1:2]) + m[0:1]
    o_ref[...] = _dot(h.astype(BF16), w_ref[...])


def _input_projection(x, mod, w_bf16, tm=PROJ_TM):
    n = w_bf16.shape[1]
    return pl.pallas_call(
        _inproj_kernel,
        out_shape=jax.ShapeDtypeStruct((N_TOK, n), F32),
        grid=(N_TOK // tm,),
        in_specs=[
            pl.BlockSpec((tm, D_MODEL), lambda i: (i, 0)),
            pl.BlockSpec((1, 6, D_MODEL), lambda i: (_cond_row(i * tm), 0, 0)),
            pl.BlockSpec((D_MODEL, n), lambda i: (0, 0)),
        ],
        out_specs=pl.BlockSpec((tm, n), lambda i: (i, 0)),
        compiler_params=_params("arbitrary"),
        name="input_projection",
    )(x, mod, w_bf16)


_NAT_Q0, _NAT_K0, _NAT_V0 = 0, NAT_WIDTH, 2 * NAT_WIDTH
_DIFF_Q0 = 3 * NAT_WIDTH
_DIFF_K0 = _DIFF_Q0 + DIFF_WIDTH
_DIFF_V0 = _DIFF_K0 + DIFF_WIDTH
ATTN_SCALE = HEAD_DIM ** -0.5


def _diff_lambda(lam_ref, lam_init):
    lp = lam_ref[...]
    return (jnp.exp(jnp.sum(lp[0:1] * lp[1:2], axis=-1, keepdims=True))
            - jnp.exp(jnp.sum(lp[2:3] * lp[3:4], axis=-1, keepdims=True)) + lam_init)


def _softmax_parts(s):
    m = jnp.max(s, axis=-1, keepdims=True)
    e = jnp.exp(s - m)
    return e, jnp.sum(e, axis=-1, keepdims=True)


def _sub_norm(o, g, lam_init):
    ms = jnp.mean(o * o, axis=-1, keepdims=True)
    return o * lax.rsqrt(ms + EPS) * g * (1.0 - lam_init)


def _ctx_attn_kernel(z_ref, lam_ref, subg_ref, o_ref, *, lam_init):
    lam = _diff_lambda(lam_ref, lam_init)
    for h in range(NAT_HEADS):
        c = h * HEAD_DIM
        q = z_ref[:, _NAT_Q0 + c:_NAT_Q0 + c + HEAD_DIM].astype(BF16)
        k = z_ref[:, _NAT_K0 + c:_NAT_K0 + c + HEAD_DIM].astype(BF16)
        v = z_ref[:, _NAT_V0 + c:_NAT_V0 + c + HEAD_DIM].astype(BF16)
        e, l = _softmax_parts(_dot_nt(q, k) * ATTN_SCALE)
        o_ref[:, c:c + HEAD_DIM] = (_dot(e.astype(BF16), v) / l).astype(o_ref.dtype)
    for h in range(DIFF_HEADS):
        c = h * 2 * HEAD_DIM
        parts = []
        v = z_ref[:, _DIFF_V0 + c:_DIFF_V0 + c + 2 * HEAD_DIM].astype(BF16)
        for j in range(2):
            cj = c + j * HEAD_DIM
            q = z_ref[:, _DIFF_Q0 + cj:_DIFF_Q0 + cj + HEAD_DIM].astype(BF16)
            k = z_ref[:, _DIFF_K0 + cj:_DIFF_K0 + cj + HEAD_DIM].astype(BF16)
            e, l = _softmax_parts(_dot_nt(q, k) * ATTN_SCALE)
            parts.append(_dot(e.astype(BF16), v) / l)
        o = parts[0] - lam * parts[1]
        o_ref[:, NAT_WIDTH + c:NAT_WIDTH + c + 2 * HEAD_DIM] = (
            _sub_norm(o, subg_ref[...], lam_init).astype(o_ref.dtype))


def _context_attention(z, lam_p, subln_g, lam_init):
    return pl.pallas_call(
        functools.partial(_ctx_attn_kernel, lam_init=lam_init),
        out_shape=jax.ShapeDtypeStruct((N_PROMPT, D_MODEL), BF16),
        grid=(BATCH,),
        in_specs=[
            pl.BlockSpec((SEQ, EVEN_IN), lambda b: (b, 0)),
            pl.BlockSpec((4, HEAD_DIM), lambda b: (0, 0)),
            pl.BlockSpec((1, 2 * HEAD_DIM), lambda b: (0, 0)),
        ],
        out_specs=pl.BlockSpec((SEQ, D_MODEL), lambda b: (b, 0)),
        compiler_params=_params("arbitrary"),
        name="context_attention",
    )(z, lam_p, subln_g.reshape(1, 2 * HEAD_DIM))


NAT_TQ = 256
_SAMPLE_BLK0 = N_PROMPT // DEC_SEQ


def _nat_bias_table(rel_bias):
    rows = N_ROWS
    kh = min(NAT_WIN_H, rows)
    kw = NAT_WIN_W
    col = np.arange(GRID_W)
    c_start = np.clip(col - kw // 2, 0, GRID_W - kw)
    col_ok = (col[None, :] >= c_start[:, None]) & (col[None, :] < c_start[:, None] + kw)
    dc = np.clip(col[None, :] - col[:, None], 1 - kw, kw - 1) + (NAT_WIN_W - 1)
    onehot = ((dc[None] == np.arange(2 * kw - 1)[:, None, None]) & col_ok[None]).astype(np.float32)
    tiles = jnp.einsum('hrd,dqk->hrqk', rel_bias.astype(F32), onehot, precision=lax.Precision.HIGHEST)
    tiles = jnp.where(col_ok, tiles, -jnp.inf)
    outside = jnp.full((NAT_HEADS, GRID_W, GRID_W), -jnp.inf, F32)
    out_rows = []
    for r in range(rows):
        start = min(max(r - kh // 2, 0), rows - kh)
        blocks = [tiles[:, a - r + NAT_WIN_H - 1] if start <= a < start + kh else outside
                  for a in range(rows)]
        out_rows.append(jnp.concatenate(blocks, axis=-1))
    return jnp.stack(out_rows, axis=1).reshape(NAT_HEADS, DEC_SEQ, DEC_SEQ)


def _nat_latent_kernel(q_ref, k_ref, v_ref, ck_ref, cv_ref, bias_ref, o_ref):
    for j in range(2):
        sl = slice(j * HEAD_DIM, (j + 1) * HEAD_DIM)
        q = q_ref[:, sl].astype(BF16)
        s_loc = _dot_nt(q, k_ref[:, sl].astype(BF16)) * ATTN_SCALE + bias_ref[j]
        s_ctx = _dot_nt(q, ck_ref[0, :, sl].astype(BF16)) * ATTN_SCALE
        m = jnp.maximum(jnp.max(s_loc, axis=-1, keepdims=True), jnp.max(s_ctx, axis=-1, keepdims=True))
        e_loc = jnp.exp(s_loc - m)
        e_ctx = jnp.exp(s_ctx - m)
        l = jnp.sum(e_loc, axis=-1, keepdims=True) + jnp.sum(e_ctx, axis=-1, keepdims=True)
        o = (_dot(e_loc.astype(BF16), v_ref[:, sl].astype(BF16))
             + _dot(e_ctx.astype(BF16), cv_ref[0, :, sl].astype(BF16)))
        o_ref[:, sl] = (o / l).astype(o_ref.dtype)


def _nat_latent_attention(z, ck, cv, bias):
    nq = DEC_SEQ // NAT_TQ
    qblk0 = N_PROMPT // NAT_TQ
    pair = 2 * HEAD_DIM
    return pl.pallas_call(
        _nat_latent_kernel,
        out_shape=jax.ShapeDtypeStruct((N_SAMPLE, NAT_WIDTH), BF16),
        grid=(NAT_HEADS // 2, nq, DEC_BATCH),
        in_specs=[
            pl.BlockSpec((NAT_TQ, pair), lambda hp, qt, b: (qblk0 + b * nq + qt, _NAT_Q0 // pair + hp)),
            pl.BlockSpec((DEC_SEQ, pair), lambda hp, qt, b: (_SAMPLE_BLK0 + b, _NAT_K0 // pair + hp)),
            pl.BlockSpec((DEC_SEQ, pair), lambda hp, qt, b: (_SAMPLE_BLK0 + b, _NAT_V0 // pair + hp)),
            pl.BlockSpec((1, PAST_LEN, pair), lambda hp, qt, b: (b, 0, hp)),
            pl.BlockSpec((1, PAST_LEN, pair), lambda hp, qt, b: (b, 0, hp)),
            pl.BlockSpec((2, NAT_TQ, DEC_SEQ), lambda hp, qt, b: (hp, qt, 0)),
        ],
        out_specs=pl.BlockSpec((NAT_TQ, pair), lambda hp, qt, b: (b * nq + qt, hp)),
        compiler_params=_params("arbitrary", "arbitrary", "arbitrary"),
        name="nat_latent_attention",
    )(z, z, z, ck, cv, bias)


DIFF_TQ = 256


def _rope_tables():
    nf = HEAD_DIM // 4
    t = jnp.arange(DEC_SEQ)
    inv = ROPE_THETA ** (-jnp.arange(nf, dtype=F32) / nf)
    ang_r = (t // GRID_W).astype(F32)[:, None] * inv
    ang_c = (t % GRID_W).astype(F32)[:, None] * inv
    cos = jnp.concatenate([jnp.cos(ang_r)] * 2 + [jnp.cos(ang_c)] * 2, axis=-1)
    sin = jnp.concatenate([-jnp.sin(ang_r), jnp.sin(ang_r), -jnp.sin(ang_c), jnp.sin(ang_c)], axis=-1)
    return jnp.tile(cos, (1, 2)), jnp.tile(sin, (1, 2))


def _rope(x, cos, sin):
    nf = HEAD_DIM // 4
    lane = lax.broadcasted_iota(jnp.int32, x.shape, 1)
    upper = (lane // nf) % 2 == 1
    partner = jnp.where(upper, pltpu.roll(x, nf, axis=1), pltpu.roll(x, x.shape[1] - nf, axis=1))
    return x * cos + partner * sin


def _diff_latent_kernel(q_ref, k_ref, v_ref, ck_ref, cv_ref, cos_ref, sin_ref, lam_ref, subg_ref,
                        o_ref, kr_ref, *, lam_init):
    qt = pl.program_id(2)

    @pl.when(qt == 0)
    def _():
        kr_ref[...] = _rope(k_ref[...], cos_ref[...], sin_ref[...]).astype(BF16)

    lam = _diff_lambda(lam_ref, lam_init)
    row0 = pl.multiple_of(qt * DIFF_TQ, DIFF_TQ)
    q = _rope(q_ref[...], cos_ref[pl.ds(row0, DIFF_TQ), :], sin_ref[pl.ds(row0, DIFF_TQ), :]).astype(BF16)
    v_loc = v_ref[...].astype(BF16)
    v_ctx = cv_ref[0].astype(BF16)
    parts = []
    for j in range(2):
        sl = slice(j * HEAD_DIM, (j + 1) * HEAD_DIM)
        s_ctx = _dot_nt(q[:, sl], ck_ref[0, :, sl].astype(BF16)) * ATTN_SCALE
        s_loc = _dot_nt(q[:, sl], kr_ref[:, sl]) * ATTN_SCALE
        m = jnp.maximum(jnp.max(s_loc, axis=-1, keepdims=True), jnp.max(s_ctx, axis=-1, keepdims=True))
        e_loc = jnp.exp(s_loc - m)
        e_ctx = jnp.exp(s_ctx - m)
        l = jnp.sum(e_loc, axis=-1, keepdims=True) + jnp.sum(e_ctx, axis=-1, keepdims=True)
        parts.append((_dot(e_loc.astype(BF16), v_loc) + _dot(e_ctx.astype(BF16), v_ctx)) / l)
    o = parts[0] - lam * parts[1]
    o_ref[...] = _sub_norm(o, subg_ref[...], lam_init).astype(o_ref.dtype)


def _diff_latent_attention(z, ck, cv, lam_p, subln_g, lam_init):
    nq = DEC_SEQ // DIFF_TQ
    qblk0 = N_PROMPT // DIFF_TQ
    w = 2 * HEAD_DIM
    cos, sin = _rope_tables()
    return pl.pallas_call(
        functools.partial(_diff_latent_kernel, lam_init=lam_init),
        out_shape=jax.ShapeDtypeStruct((N_SAMPLE, DIFF_WIDTH), BF16),
        grid=(DEC_BATCH, DIFF_HEADS, nq),
        in_specs=[
            pl.BlockSpec((DIFF_TQ, w), lambda b, h, qt: (qblk0 + b * nq + qt, _DIFF_Q0 // w + h)),
            pl.BlockSpec((DEC_SEQ, w), lambda b, h, qt: (_SAMPLE_BLK0 + b, _DIFF_K0 // w + h)),
            pl.BlockSpec((DEC_SEQ, w), lambda b, h, qt: (_SAMPLE_BLK0 + b, _DIFF_V0 // w + h)),
            pl.BlockSpec((1, PAST_LEN, w), lambda b, h, qt: (b, 0, h)),
            pl.BlockSpec((1, PAST_LEN, w), lambda b, h, qt: (b, 0, h)),
            pl.BlockSpec((DEC_SEQ, w), lambda b, h, qt: (0, 0)),
            pl.BlockSpec((DEC_SEQ, w), lambda b, h, qt: (0, 0)),
            pl.BlockSpec((4, HEAD_DIM), lambda b, h, qt: (0, 0)),
            pl.BlockSpec((1, w), lambda b, h, qt: (0, 0)),
        ],
        out_specs=pl.BlockSpec((DIFF_TQ, w), lambda b, h, qt: (b * nq + qt, h)),
        scratch_shapes=[pltpu.VMEM((DEC_SEQ, w), BF16)],
        compiler_params=_params("arbitrary", "arbitrary", "arbitrary"),
        name="diff_latent_attention",
    )(z, z, z, ck, cv, cos, sin, lam_p, subln_g.reshape(1, w))


OUT_TM = 256


def _route(logits_t, rb_col):
    sc = jax.nn.sigmoid(logits_t)
    bi = sc + rb_col
    srow = [sc[e:e + 1] for e in range(N_EXPERTS)]
    brow = [bi[e:e + 1] for e in range(N_EXPERTS)]
    n = EXPERTS_PER_GROUP
    gscore = []
    for g in range(N_GROUPS):
        v = brow[g * n:(g + 1) * n]
        best = None
        for i in range(n):
            for j in range(i + 1, n):
                s = v[i] + v[j]
                best = s if best is None else jnp.maximum(best, s)
        gscore.append(best)
    sel = jnp.zeros_like(gscore[0], dtype=jnp.int32)
    best = gscore[0]
    for g in range(1, N_GROUPS):
        better = gscore[g] > best
        sel = jnp.where(better, g, sel)
        best = jnp.where(better, gscore[g], best)

    def pick_group(rows, i):
        out = rows[(N_GROUPS - 1) * n + i]
        for g in range(N_GROUPS - 2, -1, -1):
            out = jnp.where(sel == g, rows[g * n + i], out)
        return out

    bv = [pick_group(brow, i) for i in range(n)]
    sv = [pick_group(srow, i) for i in range(n)]
    i1 = jnp.zeros_like(sel)
    m1, w1 = bv[0], sv[0]
    for i in range(1, n):
        better = bv[i] > m1
        i1 = jnp.where(better, i, i1)
        m1 = jnp.where(better, bv[i], m1)
        w1 = jnp.where(better, sv[i], w1)
    i2 = jnp.zeros_like(sel)
    m2 = jnp.full_like(m1, -jnp.inf)
    w2 = jnp.zeros_like(w1)
    for i in range(n):
        better = (i1 != i) & (bv[i] > m2)
        i2 = jnp.where(better, i, i2)
        m2 = jnp.where(better, bv[i], m2)
        w2 = jnp.where(better, sv[i], w2)
    tot = w1 + w2
    return sel * n + i1, sel * n + i2, w1 / tot, w2 / tot


def _store_token_tiles(ref, value):
    for c in range(N_CHUNKS):
        ref[:, c, :] = value[:, c * LANES:(c + 1) * LANES]


def _load_token_tiles(ref):
    return jnp.concatenate([ref[:, c, :] for c in range(N_CHUNKS)], axis=1)


def _outproj_kernel(o_ref, x_ref, mod_ref, w_ref, lng_ref, lnb_ref, rw_ref, rb_ref,
                    x1_ref, h2_ref, ri_ref, rw_out_ref, cnt_ref):
    tm = o_ref.shape[0]
    m = mod_ref[0]
    y = _dot(o_ref[...], w_ref[...])
    x1 = _layer_norm(ALPHA * x_ref[...] + m[2:3] * y, lng_ref[...], lnb_ref[...])
    x1_ref[...] = x1
    h2 = x1 * (1.0 + m[4:5]) + m[3:4]
    _store_token_tiles(h2_ref, h2)
    logits = jnp.dot(h2, rw_ref[...], precision=lax.Precision.HIGHEST, preferred_element_type=F32)
    e1, e2, w1, w2 = _route(logits.T[:N_EXPERTS], rb_ref[...])
    eid = lax.broadcasted_iota(jnp.int32, (N_EXPERTS, tm), 0)
    onehot = (eid == e1) | (eid == e2)
    ti = lax.broadcasted_iota(jnp.int32, (tm, tm), 0)
    tj = lax.broadcasted_iota(jnp.int32, (tm, tm), 1)
    earlier = jnp.where(ti < tj, 1.0, 0.0).astype(BF16)
    rank = _dot(jnp.where(onehot, 1.0, 0.0).astype(BF16), earlier)
    r1 = jnp.sum(jnp.where(eid == e1, rank, 0.0), axis=0, keepdims=True).astype(jnp.int32)
    r2 = jnp.sum(jnp.where(eid == e2, rank, 0.0), axis=0, keepdims=True).astype(jnp.int32)
    ri_ref[...] = jnp.concatenate([e1, e2, r1, r2, jnp.zeros((4, tm), jnp.int32)], axis=0)
    wt = jnp.concatenate([w1, w2, jnp.zeros((LANES - 2, tm), F32)], axis=0)
    rw_out_ref[...] = wt.T
    cnt = jnp.sum(jnp.where(onehot, 1.0, 0.0), axis=1, keepdims=True)
    cnt_ref[0] = jnp.broadcast_to(cnt, (N_EXPERTS, LANES)).astype(jnp.int32)


def _output_projection(o, x, mod, w_bf16, ln_g, ln_b, router_w_pad, router_b, tm=OUT_TM):
    nt = N_TOK // tm
    return pl.pallas_call(
        _outproj_kernel,
        out_shape=(jax.ShapeDtypeStruct((N_TOK, D_MODEL), F32),
                   jax.ShapeDtypeStruct((N_TOK, N_CHUNKS, LANES), F32),
                   jax.ShapeDtypeStruct((8, N_TOK), jnp.int32),
                   jax.ShapeDtypeStruct((N_TOK, LANES), F32),
                   jax.ShapeDtypeStruct((nt, N_EXPERTS, LANES), jnp.int32)),
        grid=(nt,),
        in_specs=[
            pl.BlockSpec((tm, D_MODEL), lambda i: (i, 0)),
            pl.BlockSpec((tm, D_MODEL), lambda i: (i, 0)),
            pl.BlockSpec((1, 6, D_MODEL), lambda i: (_cond_row(i * tm), 0, 0)),
            pl.BlockSpec((D_MODEL, D_MODEL), lambda i: (0, 0)),
            pl.BlockSpec((1, D_MODEL), lambda i: (0, 0)),
            pl.BlockSpec((1, D_MODEL), lambda i: (0, 0)),
            pl.BlockSpec((D_MODEL, LANES), lambda i: (0, 0)),
            pl.BlockSpec((N_EXPERTS, 1), lambda i: (0, 0)),
        ],
        out_specs=(pl.BlockSpec((tm, D_MODEL), lambda i: (i, 0)),
                   pl.BlockSpec((tm, N_CHUNKS, LANES), lambda i: (i, 0, 0)),
                   pl.BlockSpec((8, tm), lambda i: (0, i)),
                   pl.BlockSpec((tm, LANES), lambda i: (i, 0)),
                   pl.BlockSpec((1, N_EXPERTS, LANES), lambda i: (i, 0, 0))),
        compiler_params=_params("arbitrary"),
        name="output_projection",
    )(o, x, mod, w_bf16, ln_g.reshape(1, D_MODEL), ln_b.reshape(1, D_MODEL),
      router_w_pad, router_b.reshape(N_EXPERTS, 1))


N_ASSIGN = 2 * N_TOK
MOE_TM = 256
MOE_ROWS = N_ASSIGN + N_EXPERTS * MOE_TM
MOE_TILES = MOE_ROWS // MOE_TM
DISPATCH_CHUNK = 256
COMBINE_TM = 256


def _moe_plan(route_i, counts):
    cnt = counts[:, :, 0]
    total = jnp.sum(cnt, axis=0)
    padded = (total + MOE_TM - 1) // MOE_TM * MOE_TM
    seg_end = jnp.cumsum(padded)
    seg_start = seg_end - padded
    tile_base = seg_start[None, :] + jnp.cumsum(cnt, axis=0) - cnt
    base_tok = jnp.repeat(tile_base, OUT_TM, axis=0)
    eids = jnp.arange(N_EXPERTS, dtype=jnp.int32)[None, :]
    pos = [jnp.sum(jnp.where(route_i[k][:, None] == eids, base_tok, 0), axis=1) + route_i[2 + k]
           for k in range(2)]
    pos = jnp.concatenate(pos).astype(jnp.int32)
    tile_row0 = jnp.arange(MOE_TILES, dtype=jnp.int32) * MOE_TM
    tile_expert = jnp.minimum(jnp.sum(seg_end[None, :] <= tile_row0[:, None], axis=1), N_EXPERTS - 1)
    n_tiles = (seg_end[-1] // MOE_TM).reshape(1)
    return pos, tile_expert.astype(jnp.int32), n_tiles.astype(jnp.int32)


def _row_copy_wait(src_hbm, dst, sem, n_rows):
    pltpu.make_async_copy(src_hbm.at[pl.ds(0, n_rows)], dst.at[pl.ds(0, n_rows)], sem).wait()


def _dispatch_kernel(pos_ref, h_ref, xs_in_ref, xs_ref, sem):
    del xs_in_ref
    n_chunks = N_TOK // DISPATCH_CHUNK

    def issue(chunk):
        def body(r, carry):
            t = chunk * DISPATCH_CHUNK + r
            for k in range(2):
                pltpu.make_async_copy(h_ref.at[t], xs_ref.at[pos_ref[k * N_TOK + t]], sem).start()
            return carry
        lax.fori_loop(0, DISPATCH_CHUNK, body, 0)

    def wait_chunk():
        for _ in range(2):
            _row_copy_wait(h_ref, xs_ref, sem, DISPATCH_CHUNK)

    issue(0)

    def step(chunk, carry):
        issue(chunk)
        wait_chunk()
        return carry

    lax.fori_loop(1, n_chunks, step, 0)
    wait_chunk()


def _moe_dispatch(pos, h2_tiles):
    zeros = jnp.zeros((MOE_ROWS, N_CHUNKS, LANES), F32)
    return pl.pallas_call(
        _dispatch_kernel,
        out_shape=jax.ShapeDtypeStruct((MOE_ROWS, N_CHUNKS, LANES), F32),
        grid_spec=pltpu.PrefetchScalarGridSpec(
            num_scalar_prefetch=1,
            grid=(1,),
            in_specs=[pl.BlockSpec(memory_space=pl.ANY), pl.BlockSpec(memory_space=pl.ANY)],
            out_specs=pl.BlockSpec(memory_space=pl.ANY),
            scratch_shapes=[pltpu.SemaphoreType.DMA],
        ),
        input_output_aliases={2: 0},
        compiler_params=_params("arbitrary"),
        name="moe_dispatch",
    )(pos, h2_tiles, zeros)


def _expert_kernel(te_ref, nt_ref, xs_ref, wg_ref, wu_ref, wd_ref, ys_ref, wg16_ref, wu16_ref, wd16_ref):
    i = pl.program_id(0)

    @pl.when(i < nt_ref[0])
    def _():
        @pl.when((i == 0) | (te_ref[i] != te_ref[jnp.maximum(i - 1, 0)]))
        def _():
            wg16_ref[...] = wg_ref[0].astype(BF16)
            wu16_ref[...] = wu_ref[0].astype(BF16)
            wd16_ref[...] = wd_ref[0].astype(BF16)

        x = _load_token_tiles(xs_ref).astype(BF16)
        he = _silu(_dot(x, wg16_ref[...])) * _dot(x, wu16_ref[...])
        _store_token_tiles(ys_ref, _dot(he.astype(BF16), wd16_ref[...]))

    @pl.when(i >= nt_ref[0])
    def _():
        ys_ref[...] = jnp.zeros_like(ys_ref)


def _moe_experts(tile_expert, n_tiles, xs, wg, wu, wd):
    row_blk = lambda i, te, nt: (jnp.minimum(i, nt[0] - 1), 0, 0)
    w_blk = lambda i, te, nt: (te[i], 0, 0)
    return pl.pallas_call(
        _expert_kernel,
        out_shape=jax.ShapeDtypeStruct((MOE_ROWS, N_CHUNKS, LANES), F32),
        grid_spec=pltpu.PrefetchScalarGridSpec(
            num_scalar_prefetch=2,
            grid=(MOE_TILES,),
            in_specs=[
                pl.BlockSpec((MOE_TM, N_CHUNKS, LANES), row_blk),
                pl.BlockSpec((1, D_MODEL, EXPERT_FF), w_blk),
                pl.BlockSpec((1, D_MODEL, EXPERT_FF), w_blk),
                pl.BlockSpec((1, EXPERT_FF, D_MODEL), w_blk),
            ],
            out_specs=pl.BlockSpec((MOE_TM, N_CHUNKS, LANES), lambda i, te, nt: (i, 0, 0)),
            scratch_shapes=[pltpu.VMEM((D_MODEL, EXPERT_FF), BF16), pltpu.VMEM((D_MODEL, EXPERT_FF), BF16),
                            pltpu.VMEM((EXPERT_FF, D_MODEL), BF16)],
        ),
        compiler_params=_params("arbitrary"),
        name="moe_experts",
    )(tile_expert, n_tiles, xs, wg, wu, wd)


def _combine_kernel(pos_ref, gw_ref, x1_ref, mod_ref, lng_ref, lnb_ref, ys_ref, o_ref, buf_ref, sem):
    i = pl.program_id(0)
    n = pl.num_programs(0)
    tm = COMBINE_TM

    def issue(tile, slot):
        def body(r, carry):
            for k in range(2):
                p = pos_ref[k * N_TOK + tile * tm + r]
                pltpu.make_async_copy(ys_ref.at[p], buf_ref.at[slot, k, r], sem.at[slot]).start()
            return carry
        lax.fori_loop(0, tm, body, 0)

    @pl.when(i == 0)
    def _():
        issue(0, 0)

    @pl.when(i + 1 < n)
    def _():
        issue(i + 1, (i + 1) % 2)

    slot = i % 2
    for k in range(2):
        _row_copy_wait(ys_ref, buf_ref.at[slot, k], sem.at[slot], tm)
    gw = gw_ref[...]
    f = gw[:, 0:1] * _load_token_tiles(buf_ref.at[slot, 0]) + gw[:, 1:2] * _load_token_tiles(buf_ref.at[slot, 1])
    m = mod_ref[0]
    o_ref[...] = _layer_norm(ALPHA * x1_ref[...] + m[5:6] * f, lng_ref[...], lnb_ref[...])


def _moe_combine(pos, gate_w, x1, mod, ln_g, ln_b, ys):
    tm = COMBINE_TM
    return pl.pallas_call(
        _combine_kernel,
        out_shape=jax.ShapeDtypeStruct((N_TOK, D_MODEL), F32),
        grid_spec=pltpu.PrefetchScalarGridSpec(
            num_scalar_prefetch=1,
            grid=(N_TOK // tm,),
            in_specs=[
                pl.BlockSpec((tm, LANES), lambda i, pos: (i, 0)),
                pl.BlockSpec((tm, D_MODEL), lambda i, pos: (i, 0)),
                pl.BlockSpec((1, 6, D_MODEL), lambda i, pos: (_cond_row(i * tm), 0, 0)),
                pl.BlockSpec((1, D_MODEL), lambda i, pos: (0, 0)),
                pl.BlockSpec((1, D_MODEL), lambda i, pos: (0, 0)),
                pl.BlockSpec(memory_space=pl.ANY),
            ],
            out_specs=pl.BlockSpec((tm, D_MODEL), lambda i, pos: (i, 0)),
            scratch_shapes=[pltpu.VMEM((2, 2, tm, N_CHUNKS, LANES), F32), pltpu.SemaphoreType.DMA((2,))],
        ),
        compiler_params=_params("arbitrary"),
        name="moe_combine",
    )(pos, gate_w, x1, mod, ln_g.reshape(1, D_MODEL), ln_b.reshape(1, D_MODEL), ys)


def _moe_ffn(h2_tiles, route_i, gate_w, counts, wg, wu, wd, x1, mod, ln_g, ln_b):
    pos, tile_expert, n_tiles = _moe_plan(route_i, counts)
    xs = _moe_dispatch(pos, h2_tiles)
    ys = _moe_experts(tile_expert, n_tiles, xs, wg, wu, wd)
    return _moe_combine(pos, gate_w, x1, mod, ln_g, ln_b, ys)


DN_GATE_COL0 = DN_CONV_CH
DN_AB_COL0 = DN_CONV_CH + DN_WIDTH
DN_IN = DN_AB_COL0 + LANES
N_DIRS = 2


def _dn_prep_kernel(q_ref, k_ref, v_ref, wq_ref, wk_ref, wv_ref, qo_ref, ko_ref, vo_ref):
    length = q_ref.shape[0]
    row = lax.broadcasted_iota(jnp.int32, (length, DN_DK), 0)

    def conv(x, w):
        prev = jnp.where(row == 0, 0.0, pltpu.roll(x, 1, axis=0))
        nxt = jnp.where(row == length - 1, 0.0, pltpu.roll(x, length - 1, axis=0))
        return _silu(prev * w[0:1] + x * w[1:2] + nxt * w[2:3])

    def l2n(x):
        return x * lax.rsqrt(jnp.sum(x * x, axis=-1, keepdims=True) + 1e-6)

    qo_ref[0] = l2n(conv(q_ref[...], wq_ref[...])) * (DN_DK ** -0.5)
    ko_ref[0] = l2n(conv(k_ref[...], wk_ref[...]))
    vo_ref[0] = conv(v_ref[...], wv_ref[...])


def _dn_prep(z1, conv_w, nb, length, row_blk0):
    hq, hk, hv = 0, DN_HEADS, 2 * DN_HEADS
    shp = jax.ShapeDtypeStruct((DN_HEADS, nb * length, DN_DK), F32)
    blk = lambda off: pl.BlockSpec((length, DN_DK), lambda b, h: (row_blk0 + b, off + h))
    wblk = lambda off: pl.BlockSpec((3, DN_DK), lambda b, h: (0, off + h))
    oblk = pl.BlockSpec((1, length, DN_DK), lambda b, h: (h, b, 0))
    return pl.pallas_call(
        _dn_prep_kernel,
        out_shape=(shp, shp, shp),
        grid=(nb, DN_HEADS),
        in_specs=[blk(hq), blk(hk), blk(hv), wblk(hq), wblk(hk), wblk(hv)],
        out_specs=(oblk, oblk, oblk),
        compiler_params=_params("arbitrary", "arbitrary"),
        name="deltanet_prep",
    )(z1, z1, z1, conv_w, conv_w, conv_w)


def _dn_gates_kernel(ab_ref, alog_ref, dtb_ref, o_ref):
    length = ab_ref.shape[0]
    x = ab_ref[...]
    a = x + dtb_ref[...]
    softplus = jnp.maximum(a, 0.0) + jnp.log1p(jnp.exp(-jnp.abs(a)))
    g = -jnp.exp(alog_ref[...]) * softplus
    beta = jax.nn.sigmoid(x)
    ri = lax.broadcasted_iota(jnp.int32, (DN_CHUNK, DN_CHUNK), 0)
    ci = lax.broadcasted_iota(jnp.int32, (DN_CHUNK, DN_CHUNK), 1)
    tril = (ri >= ci).astype(F32)
    triu = (ri <= ci).astype(F32)
    lane = lax.broadcasted_iota(jnp.int32, (DN_CHUNK, LANES), 1)
    for c in range(length // DN_CHUNK):
        rows = slice(c * DN_CHUNK, (c + 1) * DN_CHUNK)
        gch = g[rows]
        pre = jnp.dot(tril, gch, precision=lax.Precision.HIGHEST, preferred_element_type=F32)
        suf = jnp.dot(triu, gch, precision=lax.Precision.HIGHEST, preferred_element_type=F32)
        gc = jnp.where(lane < DN_HEADS, pre, suf)
        o_ref[rows, :] = jnp.where(lane < N_DIRS * DN_HEADS, gc, beta[rows])


def _dn_gates(z1, a_log, dt_bias, nb, length, row_blk0):
    pad = LANES - N_DIRS * DN_HEADS
    alog = jnp.pad(a_log.reshape(1, -1).astype(F32), ((0, 0), (0, pad)))
    dtb = jnp.pad(dt_bias.reshape(1, -1).astype(F32), ((0, 0), (0, pad)))
    return pl.pallas_call(
        _dn_gates_kernel,
        out_shape=jax.ShapeDtypeStruct((nb * length, LANES), F32),
        grid=(nb,),
        in_specs=[
            pl.BlockSpec((length, LANES), lambda b: (row_blk0 + b, DN_AB_COL0 // LANES)),
            pl.BlockSpec((1, LANES), lambda b: (0, 0)),
            pl.BlockSpec((1, LANES), lambda b: (0, 0)),
        ],
        out_specs=pl.BlockSpec((length, LANES), lambda b: (b, 0)),
        compiler_params=_params("arbitrary"),
        name="deltanet_gates",
    )(z1, alog, dtb)


def _delta_chunk(q, k, v, beta, gcol, grow, s, lower):
    ri = lax.broadcasted_iota(jnp.int32, (DN_CHUNK, DN_CHUNK), 0)
    ci = lax.broadcasted_iota(jnp.int32, (DN_CHUNK, DN_CHUNK), 1)
    incl = (ri >= ci) if lower else (ri <= ci)
    strict = (ri > ci) if lower else (ri < ci)
    decay = jnp.exp(jnp.where(incl, gcol - grow, -jnp.inf))
    kb = k * beta
    k16 = k.astype(BF16)
    kk = _dot_nt(kb.astype(BF16), k16)
    a_qk = _dot_nt(q.astype(BF16), k16) * decay
    yield
    tri_l = jnp.where(strict, kk * decay, 0.0)

    def off_block(size):
        same = (ri // (2 * size)) == (ci // (2 * size))
        rpar, cpar = (ri // size) % 2, (ci // size) % 2
        return same & ((rpar == 1) & (cpar == 0) if lower else (rpar == 0) & (cpar == 1))

    p = jnp.where(ri == ci, 1.0, 0.0) - jnp.where(off_block(1), tri_l, 0.0)
    size = 2
    while size < DN_CHUNK:
        p16 = p.astype(BF16)
        pc = _dot(p16, jnp.where(off_block(size), tri_l, 0.0).astype(BF16))
        yield
        p = p - _dot(pc.astype(BF16), p16)
        yield
        size *= 2
    eg = jnp.exp(gcol)
    uw = _dot(p.astype(BF16), jnp.concatenate([v * beta, kb * eg], axis=1).astype(BF16))
    yield
    u, w = uw[:, :DN_DV], uw[:, DN_DV:]
    s16 = s.astype(BF16)
    v_new = u - _dot(w.astype(BF16), s16)
    o_inter = _dot((q * eg).astype(BF16), s16)
    yield
    v_new16 = v_new.astype(BF16)
    o = o_inter + _dot(a_qk.astype(BF16), v_new16)
    g_last = gcol[DN_CHUNK - 1:DN_CHUNK] if lower else gcol[0:1]
    kd = k * jnp.exp(g_last - gcol)
    s_new = s * jnp.exp(g_last) + _dot_tn(kd.astype(BF16), v_new16)
    return o, s_new


def _run_interleaved(chains):
    results = [None] * len(chains)
    active = list(enumerate(chains))
    while active:
        still = []
        for idx, gen in active:
            try:
                next(gen)
                still.append((idx, gen))
            except StopIteration as stop:
                results[idx] = stop.value
        active = still
    return results


def _dn_scan_kernel(*refs, has_init, want_final):
    (qf_ref, kf_ref, vf_ref, gf_ref, qb_ref, kb_ref, vb_ref, gb_ref), rest = refs[:8], refs[8:]
    if has_init:
        s0_ref, rest = rest[0], rest[1:]
    of_ref, ob_ref = rest[0], rest[1]
    rest = rest[2:]
    if want_final:
        sf_ref, rest = rest[0], rest[1:]
    s_ref = rest[0]
    c = pl.program_id(1)
    nc = pl.num_programs(1)

    @pl.when(c == 0)
    def _():
        for d in range(N_DIRS):
            for h in range(DN_HEADS):
                if has_init:
                    s_ref[d * DN_HEADS + h] = s0_ref[0, d, h]
                else:
                    s_ref[d * DN_HEADS + h] = jnp.zeros((DN_DK, DN_DV), F32)

    dirs = ((qf_ref, kf_ref, vf_ref, gf_ref, of_ref, True), (qb_ref, kb_ref, vb_ref, gb_ref, ob_ref, False))
    chains = []
    for d, (q_ref, k_ref, v_ref, g_ref, o_ref, lower) in enumerate(dirs):
        gates = g_ref[...]
        gates_t = gates.T
        for h in range(DN_HEADS):
            j = d * DN_HEADS + h
            jb = N_DIRS * DN_HEADS + j
            chains.append(_delta_chunk(q_ref[h], k_ref[h], v_ref[h], gates[:, jb:jb + 1],
                                       gates[:, j:j + 1], gates_t[j:j + 1, :], s_ref[j], lower))
    results = _run_interleaved(chains)
    for d, (_, _, _, _, o_ref, _) in enumerate(dirs):
        for h in range(DN_HEADS):
            j = d * DN_HEADS + h
            o, s_new = results[j]
            o_ref[:, h * DN_DV:(h + 1) * DN_DV] = o
            s_ref[j] = s_new

    if want_final:
        @pl.when(c == nc - 1)
        def _():
            for d in range(N_DIRS):
                for h in range(DN_HEADS):
                    sf_ref[0, d, h] = s_ref[d * DN_HEADS + h]


def _dn_scan(qn, kn, vn, gates, s0, nb, length, want_final):
    nc = length // DN_CHUNK
    has_init = s0 is not None
    fwd = lambda b, c: (0, b * nc + c, 0)
    bwd = lambda b, c: (0, b * nc + (nc - 1 - c), 0)
    hm = lambda im: pl.BlockSpec((DN_HEADS, DN_CHUNK, DN_DK), im)
    in_specs = [hm(fwd), hm(fwd), hm(fwd), pl.BlockSpec((DN_CHUNK, LANES), lambda b, c: (b * nc + c, 0)),
                hm(bwd), hm(bwd), hm(bwd), pl.BlockSpec((DN_CHUNK, LANES), lambda b, c: (b * nc + (nc - 1 - c), 0))]
    args = [qn, kn, vn, gates, qn, kn, vn, gates]
    state_blk = pl.BlockSpec((1, N_DIRS, DN_HEADS, DN_DK, DN_DV), lambda b, c: (b, 0, 0, 0, 0))
    if has_init:
        in_specs.append(state_blk)
        args.append(s0)
    o_shape = jax.ShapeDtypeStruct((nb * length, DN_WIDTH), F32)
    out_shape = [o_shape, o_shape]
    out_specs = [pl.BlockSpec((DN_CHUNK, DN_WIDTH), lambda b, c: (b * nc + c, 0)),
                 pl.BlockSpec((DN_CHUNK, DN_WIDTH), lambda b, c: (b * nc + (nc - 1 - c), 0))]
    if want_final:
        out_shape.append(jax.ShapeDtypeStruct((nb, N_DIRS, DN_HEADS, DN_DK, DN_DV), F32))
        out_specs.append(state_blk)
    return pl.pallas_call(
        functools.partial(_dn_scan_kernel, has_init=has_init, want_final=want_final),
        out_shape=tuple(out_shape),
        grid=(nb, nc),
        in_specs=in_specs,
        out_specs=tuple(out_specs),
        scratch_shapes=[pltpu.VMEM((N_DIRS * DN_HEADS, DN_DK, DN_DV), F32)],
        compiler_params=_params("arbitrary", "arbitrary"),
        name="deltanet_scan",
    )(*args)


DN_MERGE_TM = 256


def _dn_merge_kernel(of_ref, ob_ref, gate_ref, g_ref, o_ref):
    for h in range(DN_HEADS):
        sl = slice(h * DN_DV, (h + 1) * DN_DV)
        o = of_ref[:, sl] + ob_ref[:, sl]
        o = o * lax.rsqrt(jnp.mean(o * o, axis=-1, keepdims=True) + EPS) * g_ref[...]
        o_ref[:, sl] = (o * _silu(gate_ref[:, sl])).astype(o_ref.dtype)


def _dn_merge(o_f, o_b, z1, onorm_g, n_rows, row0):
    tm = DN_MERGE_TM
    blk = pl.BlockSpec((tm, DN_WIDTH), lambda i: (i, 0))
    return pl.pallas_call(
        _dn_merge_kernel,
        out_shape=jax.ShapeDtypeStruct((n_rows, DN_WIDTH), BF16),
        grid=(n_rows // tm,),
        in_specs=[blk, blk,
                  pl.BlockSpec((tm, DN_WIDTH), lambda i: (row0 // tm + i, DN_GATE_COL0 // DN_WIDTH)),
                  pl.BlockSpec((1, DN_DV), lambda i: (0, 0))],
        out_specs=blk,
        compiler_params=_params("arbitrary"),
        name="deltanet_merge",
    )(o_f, o_b, z1, onorm_g.reshape(1, DN_DV))


def _deltanet_stream(z1, conv_w, a_log, dt_bias, onorm_g, s0, nb, length, row0, want_final):
    row_blk0 = row0 // length
    qn, kn, vn = _dn_prep(z1, conv_w, nb, length, row_blk0)
    gates = _dn_gates(z1, a_log, dt_bias, nb, length, row_blk0)
    outs = _dn_scan(qn, kn, vn, gates, s0, nb, length, want_final)
    merged = _dn_merge(outs[0], outs[1], z1, onorm_g, nb * length, row0)
    return merged, (outs[2] if want_final else None)


def kernel(x_prompt, x_sample, cache_nat_k, cache_nat_v, cache_diff_k, cache_diff_v, state_delta, c, c_ctx, w_in_even, w_out_even, nat_bias, diff_lam, diff_subln, w_in_odd, conv_odd, a_log_odd, dt_bias_odd, onorm_odd, w_out_odd, ada_w, ada_b, ln_g, ln_b, router_w, router_b, moe_wg, moe_wu, moe_wd):
    x = jnp.concatenate([x_prompt.reshape(N_PROMPT, D_MODEL), x_sample.reshape(N_SAMPLE, D_MODEL)], axis=0)
    cond = jnp.concatenate([c_ctx[None], c, jnp.zeros((N_COND - 1 - DEC_BATCH, D_MODEL), F32)], axis=0)
    mod = _ada_modulation(cond, ada_w, ada_b).reshape(DEPTH, N_COND, 6, D_MODEL)
    router_w_pad = jnp.pad(router_w, ((0, 0), (0, LANES - N_EXPERTS)))

    lam_init = 0.8 - 0.6 * math.exp(-0.3 * 0)
    z = _input_projection(x, mod[0], w_in_even[0].astype(BF16))
    new_nat_k = z[:N_PROMPT, _NAT_K0:_NAT_V0].reshape(BATCH, 1, SEQ, NAT_HEADS, HEAD_DIM)
    new_nat_v = z[:N_PROMPT, _NAT_V0:_DIFF_Q0].reshape(BATCH, 1, SEQ, NAT_HEADS, HEAD_DIM)
    new_diff_k = z[:N_PROMPT, _DIFF_K0:_DIFF_V0].reshape(BATCH, 1, SEQ, DIFF_HEADS, 2, HEAD_DIM)
    new_diff_v = z[:N_PROMPT, _DIFF_V0:].reshape(BATCH, 1, SEQ, DIFF_HEADS, 2 * HEAD_DIM)
    o_ctx = _context_attention(z, diff_lam[0], diff_subln[0], lam_init)
    o_nat = _nat_latent_attention(z, cache_nat_k[:, 0].reshape(DEC_BATCH, PAST_LEN, NAT_WIDTH),
                                  cache_nat_v[:, 0].reshape(DEC_BATCH, PAST_LEN, NAT_WIDTH),
                                  _nat_bias_table(nat_bias[0]))
    o_diff = _diff_latent_attention(z, cache_diff_k[:, 0].reshape(DEC_BATCH, PAST_LEN, DIFF_WIDTH),
                                    cache_diff_v[:, 0].reshape(DEC_BATCH, PAST_LEN, DIFF_WIDTH),
                                    diff_lam[0], diff_subln[0], lam_init)
    o = jnp.concatenate([o_ctx, jnp.concatenate([o_nat, o_diff], axis=1)], axis=0)
    x1, *routed = _output_projection(o, x, mod[0], w_out_even[0].astype(BF16), ln_g[0, 0], ln_b[0, 0],
                                     router_w_pad, router_b)
    x = _moe_ffn(*routed, moe_wg[0], moe_wu[0], moe_wd[0], x1, mod[0], ln_g[0, 1], ln_b[0, 1])

    w1 = w_in_odd[0]
    w1 = jnp.concatenate([w1[:, :DN_CONV_CH], w1[:, DN_CONV_CH + 4 * DN_HEADS:],
                          w1[:, DN_CONV_CH:DN_CONV_CH + 4 * DN_HEADS],
                          jnp.zeros((D_MODEL, LANES - 4 * DN_HEADS), F32)], axis=1).astype(BF16)
    z1 = _input_projection(x, mod[1], w1, tm=256)
    o_p, new_state = _deltanet_stream(z1, conv_odd[0], a_log_odd[0], dt_bias_odd[0], onorm_odd[0],
                                      None, BATCH, SEQ, 0, True)
    o_s, _ = _deltanet_stream(z1, conv_odd[0], a_log_odd[0], dt_bias_odd[0], onorm_odd[0],
                              state_delta[:, 0], DEC_BATCH, DEC_SEQ, N_PROMPT, False)
    o = jnp.concatenate([o_p, o_s], axis=0)
    x1, *routed = _output_projection(o, x, mod[1], w_out_odd[0].astype(BF16), ln_g[1, 0], ln_b[1, 0],
                                     router_w_pad, router_b)
    x = _moe_ffn(*routed, moe_wg[1], moe_wu[1], moe_wd[1], x1, mod[1], ln_g[1, 1], ln_b[1, 1])

    y_prompt = x[:N_PROMPT].reshape(BATCH, SEQ, D_MODEL)
    y_sample = x[N_PROMPT:].reshape(DEC_BATCH, DEC_SEQ, D_MODEL)
    return (y_prompt, y_sample, new_nat_k, new_nat_v, new_diff_k, new_diff_v, new_state[:, None])
```

```python
import functools
import math

import jax
import jax.numpy as jnp
import numpy as np
from jax import lax
from jax.experimental import pallas as pl
from jax.experimental.pallas import tpu as pltpu

F32 = jnp.float32
BF16 = jnp.bfloat16

D_MODEL = 1024
BATCH = 16
SEQ = 256
DEC_BATCH = 8
DEC_SEQ = 1024
PAST_LEN = 256
GRID_W = 64
N_ROWS = DEC_SEQ // GRID_W

HEAD_DIM = 64
NAT_HEADS = 8
NAT_WIN_H = 8
NAT_WIN_W = 16
DIFF_HEADS = 4
NAT_WIDTH = NAT_HEADS * HEAD_DIM
DIFF_WIDTH = DIFF_HEADS * 2 * HEAD_DIM
EVEN_IN = 3 * NAT_WIDTH + 3 * DIFF_WIDTH
ROPE_THETA = 10000.0

DN_HEADS = 8
DN_DK = 128
DN_DV = 128
DN_QK_WIDTH = DN_HEADS * DN_DK
DN_WIDTH = DN_HEADS * DN_DV
DN_CONV_CH = 2 * DN_QK_WIDTH + DN_WIDTH
DN_CHUNK = 64

N_EXPERTS = 16
N_GROUPS = 4
EXPERTS_PER_GROUP = N_EXPERTS // N_GROUPS
EXPERT_FF = 512

DEPTH = 2
ALPHA = (2 * DEPTH) ** 0.25
EPS = 1e-5

N_PROMPT = BATCH * SEQ
N_SAMPLE = DEC_BATCH * DEC_SEQ
N_TOK = N_PROMPT + N_SAMPLE
N_COND = 16

VMEM_LIMIT = 56 * 1024 * 1024
LANES = 128
N_CHUNKS = D_MODEL // LANES


def _params(*sem):
    return pltpu.CompilerParams(dimension_semantics=sem, vmem_limit_bytes=VMEM_LIMIT)


def _cond_row(row0):
    return jnp.where(row0 < N_PROMPT, 0, 1 + (row0 - N_PROMPT) // DEC_SEQ)


def _silu(x):
    return x * jax.nn.sigmoid(x)


def _layer_norm(r, g, b):
    mu = jnp.mean(r, axis=-1, keepdims=True)
    xc = r - mu
    var = jnp.mean(xc * xc, axis=-1, keepdims=True)
    return xc * lax.rsqrt(var + EPS) * g + b


def _dot(a, b):
    return jnp.dot(a, b, preferred_element_type=F32)


def _dot_nt(a, b):
    return lax.dot_general(a, b, (((1,), (1,)), ((), ())), preferred_element_type=F32)


def _dot_tn(a, b):
    return lax.dot_general(a, b, (((0,), (0,)), ((), ())), preferred_element_type=F32)


ADA_TN = 1536


def _ada_kernel(c_ref, w_ref, b_ref, o_ref):
    s = _silu(c_ref[...])
    o_ref[0] = _dot(s.astype(BF16), w_ref[0].astype(BF16)) + b_ref[0]


def _ada_modulation(cond, ada_w, ada_b):
    n = 6 * D_MODEL
    return pl.pallas_call(
        _ada_kernel,
        out_shape=jax.ShapeDtypeStruct((DEPTH, N_COND, n), F32),
        grid=(DEPTH, n // ADA_TN),
        in_specs=[
            pl.BlockSpec((N_COND, D_MODEL), lambda l, j: (0, 0)),
            pl.BlockSpec((1, D_MODEL, ADA_TN), lambda l, j: (l, 0, j)),
            pl.BlockSpec((1, 1, ADA_TN), lambda l, j: (l, 0, j)),
        ],
        out_specs=pl.BlockSpec((1, N_COND, ADA_TN), lambda l, j: (l, 0, j)),
        compiler_params=_params("arbitrary", "arbitrary"),
        name="ada_modulation",
    )(cond, ada_w, ada_b.reshape(DEPTH, 1, n))


PROJ_TM = 512


def _inproj_kernel(x_ref, mod_ref, w_ref, o_ref):
    m = mod_ref[0]
    h = x_ref[...] * (1.0 + m[1:2]) + m[0:1]
    o_ref[...] = _dot(h.astype(BF16), w_ref[...])


def _input_projection(x, mod, w_bf16, tm=PROJ_TM):
    n = w_bf16.shape[1]
    return pl.pallas_call(
        _inproj_kernel,
        out_shape=jax.ShapeDtypeStruct((N_TOK, n), F32),
        grid=(N_TOK // tm,),
        in_specs=[
            pl.BlockSpec((tm, D_MODEL), lambda i: (i, 0)),
            pl.BlockSpec((1, 6, D_MODEL), lambda i: (_cond_row(i * tm), 0, 0)),
            pl.BlockSpec((D_MODEL, n), lambda i: (0, 0)),
        ],
        out_specs=pl.BlockSpec((tm, n), lambda i: (i, 0)),
        compiler_params=_params("arbitrary"),
        name="input_projection",
    )(x, mod, w_bf16)


_NAT_Q0, _NAT_K0, _NAT_V0 = 0, NAT_WIDTH, 2 * NAT_WIDTH
_DIFF_Q0 = 3 * NAT_WIDTH
_DIFF_K0 = _DIFF_Q0 + DIFF_WIDTH
_DIFF_V0 = _DIFF_K0 + DIFF_WIDTH
ATTN_SCALE = HEAD_DIM ** -0.5


def _diff_lambda(lam_ref, lam_init):
    lp = lam_ref[...]
    return (jnp.exp(jnp.sum(lp[0:1] * lp[1:2], axis=-1, keepdims=True))
            - jnp.exp(jnp.sum(lp[2:3] * lp[3:4], axis=-1, keepdims=True)) + lam_init)


def _softmax_parts(s):
    m = jnp.max(s, axis=-1, keepdims=True)
    e = jnp.exp(s - m)
    return e, jnp.sum(e, axis=-1, keepdims=True)


def _sub_norm(o, g, lam_init):
    ms = jnp.mean(o * o, axis=-1, keepdims=True)
    return o * lax.rsqrt(ms + EPS) * g * (1.0 - lam_init)


def _ctx_attn_kernel(z_ref, lam_ref, subg_ref, o_ref, *, lam_init):
    lam = _diff_lambda(lam_ref, lam_init)
    for h in range(NAT_HEADS):
        c = h * HEAD_DIM
        q = z_ref[:, _NAT_Q0 + c:_NAT_Q0 + c + HEAD_DIM].astype(BF16)
        k = z_ref[:, _NAT_K0 + c:_NAT_K0 + c + HEAD_DIM].astype(BF16)
        v = z_ref[:, _NAT_V0 + c:_NAT_V0 + c + HEAD_DIM].astype(BF16)
        e, l = _softmax_parts(_dot_nt(q, k) * ATTN_SCALE)
        o_ref[:, c:c + HEAD_DIM] = (_dot(e.astype(BF16), v) / l).astype(o_ref.dtype)
    for h in range(DIFF_HEADS):
        c = h * 2 * HEAD_DIM
        parts = []
        v = z_ref[:, _DIFF_V0 + c:_DIFF_V0 + c + 2 * HEAD_DIM].astype(BF16)
        for j in range(2):
            cj = c + j * HEAD_DIM
            q = z_ref[:, _DIFF_Q0 + cj:_DIFF_Q0 + cj + HEAD_DIM].astype(BF16)
            k = z_ref[:, _DIFF_K0 + cj:_DIFF_K0 + cj + HEAD_DIM].astype(BF16)
            e, l = _softmax_parts(_dot_nt(q, k) * ATTN_SCALE)
            parts.append(_dot(e.astype(BF16), v) / l)
        o = parts[0] - lam * parts[1]
        o_ref[:, NAT_WIDTH + c:NAT_WIDTH + c + 2 * HEAD_DIM] = (
            _sub_norm(o, subg_ref[...], lam_init).astype(o_ref.dtype))


def _context_attention(z, lam_p, subln_g, lam_init):
    return pl.pallas_call(
        functools.partial(_ctx_attn_kernel, lam_init=lam_init),
        out_shape=jax.ShapeDtypeStruct((N_PROMPT, D_MODEL), BF16),
        grid=(BATCH,),
        in_specs=[
            pl.BlockSpec((SEQ, EVEN_IN), lambda b: (b, 0)),
            pl.BlockSpec((4, HEAD_DIM), lambda b: (0, 0)),
            pl.BlockSpec((1, 2 * HEAD_DIM), lambda b: (0, 0)),
        ],
        out_specs=pl.BlockSpec((SEQ, D_MODEL), lambda b: (b, 0)),
        compiler_params=_params("arbitrary"),
        name="context_attention",
    )(z, lam_p, subln_g.reshape(1, 2 * HEAD_DIM))


NAT_TQ = 256
_SAMPLE_BLK0 = N_PROMPT // DEC_SEQ


def _nat_bias_table(rel_bias):
    rows = N_ROWS
    kh = min(NAT_WIN_H, rows)
    kw = NAT_WIN_W
    col = np.arange(GRID_W)
    c_start = np.clip(col - kw // 2, 0, GRID_W - kw)
    col_ok = (col[None, :] >= c_start[:, None]) & (col[None, :] < c_start[:, None] + kw)
    dc = np.clip(col[None, :] - col[:, None], 1 - kw, kw - 1) + (NAT_WIN_W - 1)
    onehot = ((dc[None] == np.arange(2 * kw - 1)[:, None, None]) & col_ok[None]).astype(np.float32)
    tiles = jnp.einsum('hrd,dqk->hrqk', rel_bias.astype(F32), onehot, precision=lax.Precision.HIGHEST)
    tiles = jnp.where(col_ok, tiles, -jnp.inf)
    outside = jnp.full((NAT_HEADS, GRID_W, GRID_W), -jnp.inf, F32)
    out_rows = []
    for r in range(rows):
        start = min(max(r - kh // 2, 0), rows - kh)
        blocks = [tiles[:, a - r + NAT_WIN_H - 1] if start <= a < start + kh else outside
                  for a in range(rows)]
        out_rows.append(jnp.concatenate(blocks, axis=-1))
    return jnp.stack(out_rows, axis=1).reshape(NAT_HEADS, DEC_SEQ, DEC_SEQ)


def _nat_latent_kernel(q_ref, k_ref, v_ref, ck_ref, cv_ref, bias_ref, o_ref):
    for j in range(2):
        sl = slice(j * HEAD_DIM, (j + 1) * HEAD_DIM)
        q = q_ref[:, sl].astype(BF16)
        s_loc = _dot_nt(q, k_ref[:, sl].astype(BF16)) * ATTN_SCALE + bias_ref[j]
        s_ctx = _dot_nt(q, ck_ref[0, :, sl].astype(BF16)) * ATTN_SCALE
        m = jnp.maximum(jnp.max(s_loc, axis=-1, keepdims=True), jnp.max(s_ctx, axis=-1, keepdims=True))
        e_loc = jnp.exp(s_loc - m)
        e_ctx = jnp.exp(s_ctx - m)
        l = jnp.sum(e_loc, axis=-1, keepdims=True) + jnp.sum(e_ctx, axis=-1, keepdims=True)
        o = (_dot(e_loc.astype(BF16), v_ref[:, sl].astype(BF16))
             + _dot(e_ctx.astype(BF16), cv_ref[0, :, sl].astype(BF16)))
        o_ref[:, sl] = (o / l).astype(o_ref.dtype)


def _nat_latent_attention(z, ck, cv, bias):
    nq = DEC_SEQ // NAT_TQ
    qblk0 = N_PROMPT // NAT_TQ
    pair = 2 * HEAD_DIM
    return pl.pallas_call(
        _nat_latent_kernel,
        out_shape=jax.ShapeDtypeStruct((N_SAMPLE, NAT_WIDTH), BF16),
        grid=(NAT_HEADS // 2, nq, DEC_BATCH),
        in_specs=[
            pl.BlockSpec((NAT_TQ, pair), lambda hp, qt, b: (qblk0 + b * nq + qt, _NAT_Q0 // pair + hp)),
            pl.BlockSpec((DEC_SEQ, pair), lambda hp, qt, b: (_SAMPLE_BLK0 + b, _NAT_K0 // pair + hp)),
            pl.BlockSpec((DEC_SEQ, pair), lambda hp, qt, b: (_SAMPLE_BLK0 + b, _NAT_V0 // pair + hp)),
            pl.BlockSpec((1, PAST_LEN, pair), lambda hp, qt, b: (b, 0, hp)),
            pl.BlockSpec((1, PAST_LEN, pair), lambda hp, qt, b: (b, 0, hp)),
            pl.BlockSpec((2, NAT_TQ, DEC_SEQ), lambda hp, qt, b: (hp, qt, 0)),
        ],
        out_specs=pl.BlockSpec((NAT_TQ, pair), lambda hp, qt, b: (b * nq + qt, hp)),
        compiler_params=_params("arbitrary", "arbitrary", "arbitrary"),
        name="nat_latent_attention",
    )(z, z, z, ck, cv, bias)


DIFF_TQ = 256


def _rope_tables():
    nf = HEAD_DIM // 4
    t = jnp.arange(DEC_SEQ)
    inv = ROPE_THETA ** (-jnp.arange(nf, dtype=F32) / nf)
    ang_r = (t // GRID_W).astype(F32)[:, None] * inv
    ang_c = (t % GRID_W).astype(F32)[:, None] * inv
    cos = jnp.concatenate([jnp.cos(ang_r)] * 2 + [jnp.cos(ang_c)] * 2, axis=-1)
    sin = jnp.concatenate([-jnp.sin(ang_r), jnp.sin(ang_r), -jnp.sin(ang_c), jnp.sin(ang_c)], axis=-1)
    return jnp.tile(cos, (1, 2)), jnp.tile(sin, (1, 2))


def _rope(x, cos, sin):
    nf = HEAD_DIM // 4
    lane = lax.broadcasted_iota(jnp.int32, x.shape, 1)
    upper = (lane // nf) % 2 == 1
    partner = jnp.where(upper, pltpu.roll(x, nf, axis=1), pltpu.roll(x, x.shape[1] - nf, axis=1))
    return x * cos + partner * sin


def _diff_latent_kernel(q_ref, k_ref, v_ref, ck_ref, cv_ref, cos_ref, sin_ref, lam_ref, subg_ref,
                        o_ref, kr_ref, *, lam_init):
    qt = pl.program_id(2)

    @pl.when(qt == 0)
    def _():
        kr_ref[...] = _rope(k_ref[...], cos_ref[...], sin_ref[...]).astype(BF16)

    lam = _diff_lambda(lam_ref, lam_init)
    row0 = pl.multiple_of(qt * DIFF_TQ, DIFF_TQ)
    q = _rope(q_ref[...], cos_ref[pl.ds(row0, DIFF_TQ), :], sin_ref[pl.ds(row0, DIFF_TQ), :]).astype(BF16)
    v_loc = v_ref[...].astype(BF16)
    v_ctx = cv_ref[0].astype(BF16)
    parts = []
    for j in range(2):
        sl = slice(j * HEAD_DIM, (j + 1) * HEAD_DIM)
        s_ctx = _dot_nt(q[:, sl], ck_ref[0, :, sl].astype(BF16)) * ATTN_SCALE
        s_loc = _dot_nt(q[:, sl], kr_ref[:, sl]) * ATTN_SCALE
        m = jnp.maximum(jnp.max(s_loc, axis=-1, keepdims=True), jnp.max(s_ctx, axis=-1, keepdims=True))
        e_loc = jnp.exp(s_loc - m)
        e_ctx = jnp.exp(s_ctx - m)
        l = jnp.sum(e_loc, axis=-1, keepdims=True) + jnp.sum(e_ctx, axis=-1, keepdims=True)
        parts.append((_dot(e_loc.astype(BF16), v_loc) + _dot(e_ctx.astype(BF16), v_ctx)) / l)
    o = parts[0] - lam * parts[1]
    o_ref[...] = _sub_norm(o, subg_ref[...], lam_init).astype(o_ref.dtype)


def _diff_latent_attention(z, ck, cv, lam_p, subln_g, lam_init):
    nq = DEC_SEQ // DIFF_TQ
    qblk0 = N_PROMPT // DIFF_TQ
    w = 2 * HEAD_DIM
    cos, sin = _rope_tables()
    return pl.pallas_call(
        functools.partial(_diff_latent_kernel, lam_init=lam_init),
        out_shape=jax.ShapeDtypeStruct((N_SAMPLE, DIFF_WIDTH), BF16),
        grid=(DEC_BATCH, DIFF_HEADS, nq),
        in_specs=[
            pl.BlockSpec((DIFF_TQ, w), lambda b, h, qt: (qblk0 + b * nq + qt, _DIFF_Q0 // w + h)),
            pl.BlockSpec((DEC_SEQ, w), lambda b, h, qt: (_SAMPLE_BLK0 + b, _DIFF_K0 // w + h)),
            pl.BlockSpec((DEC_SEQ, w), lambda b, h, qt: (_SAMPLE_BLK0 + b, _DIFF_V0 // w + h)),
            pl.BlockSpec((1, PAST_LEN, w), lambda b, h, qt: (b, 0, h)),
            pl.BlockSpec((1, PAST_LEN, w), lambda b, h, qt: (b, 0, h)),
            pl.BlockSpec((DEC_SEQ, w), lambda b, h, qt: (0, 0)),
            pl.BlockSpec((DEC_SEQ, w), lambda b, h, qt: (0, 0)),
            pl.BlockSpec((4, HEAD_DIM), lambda b, h, qt: (0, 0)),
            pl.BlockSpec((1, w), lambda b, h, qt: (0, 0)),
        ],
        out_specs=pl.BlockSpec((DIFF_TQ, w), lambda b, h, qt: (b * nq + qt, h)),
        scratch_shapes=[pltpu.VMEM((DEC_SEQ, w), BF16)],
        compiler_params=_params("arbitrary", "arbitrary", "arbitrary"),
        name="diff_latent_attention",
    )(z, z, z, ck, cv, cos, sin, lam_p, subln_g.reshape(1, w))


OUT_TM = 256


def _route(logits_t, rb_col):
    sc = jax.nn.sigmoid(logits_t)
    bi = sc + rb_col
    srow = [sc[e:e + 1] for e in range(N_EXPERTS)]
    brow = [bi[e:e + 1] for e in range(N_EXPERTS)]
    n = EXPERTS_PER_GROUP
    gscore = []
    for g in range(N_GROUPS):
        v = brow[g * n:(g + 1) * n]
        best = None
        for i in range(n):
            for j in range(i + 1, n):
                s = v[i] + v[j]
                best = s if best is None else jnp.maximum(best, s)
        gscore.append(best)
    sel = jnp.zeros_like(gscore[0], dtype=jnp.int32)
    best = gscore[0]
    for g in range(1, N_GROUPS):
        better = gscore[g] > best
        sel = jnp.where(better, g, sel)
        best = jnp.where(better, gscore[g], best)

    def pick_group(rows, i):
        out = rows[(N_GROUPS - 1) * n + i]
        for g in range(N_GROUPS - 2, -1, -1):
            out = jnp.where(sel == g, rows[g * n + i], out)
        return out

    bv = [pick_group(brow, i) for i in range(n)]
    sv = [pick_group(srow, i) for i in range(n)]
    i1 = jnp.zeros_like(sel)
    m1, w1 = bv[0], sv[0]
    for i in range(1, n):
        better = bv[i] > m1
        i1 = jnp.where(better, i, i1)
        m1 = jnp.where(better, bv[i], m1)
        w1 = jnp.where(better, sv[i], w1)
    i2 = jnp.zeros_like(sel)
    m2 = jnp.full_like(m1, -jnp.inf)
    w2 = jnp.zeros_like(w1)
    for i in range(n):
        better = (i1 != i) & (bv[i] > m2)
        i2 = jnp.where(better, i, i2)
        m2 = jnp.where(better, bv[i], m2)
        w2 = jnp.where(better, sv[i], w2)
    tot = w1 + w2
    return sel * n + i1, sel * n + i2, w1 / tot, w2 / tot


def _store_token_tiles(ref, value):
    for c in range(N_CHUNKS):
        ref[:, c, :] = value[:, c * LANES:(c + 1) * LANES]


def _load_token_tiles(ref):
    return jnp.concatenate([ref[:, c, :] for c in range(N_CHUNKS)], axis=1)


def _outproj_kernel(o_ref, x_ref, mod_ref, w_ref, lng_ref, lnb_ref, rw_ref, rb_ref,
                    x1_ref, h2_ref, ri_ref, rw_out_ref, cnt_ref):
    tm = o_ref.shape[0]
    m = mod_ref[0]
    y = _dot(o_ref[...], w_ref[...])
    x1 = _layer_norm(ALPHA * x_ref[...] + m[2:3] * y, lng_ref[...], lnb_ref[...])
    x1_ref[...] = x1
    h2 = x1 * (1.0 + m[4:5]) + m[3:4]
    _store_token_tiles(h2_ref, h2)
    logits = jnp.dot(h2, rw_ref[...], precision=lax.Precision.HIGHEST, preferred_element_type=F32)
    e1, e2, w1, w2 = _route(logits.T[:N_EXPERTS], rb_ref[...])
    eid = lax.broadcasted_iota(jnp.int32, (N_EXPERTS, tm), 0)
    onehot = (eid == e1) | (eid == e2)
    ti = lax.broadcasted_iota(jnp.int32, (tm, tm), 0)
    tj = lax.broadcasted_iota(jnp.int32, (tm, tm), 1)
    earlier = jnp.where(ti < tj, 1.0, 0.0).astype(BF16)
    rank = _dot(jnp.where(onehot, 1.0, 0.0).astype(BF16), earlier)
    r1 = jnp.sum(jnp.where(eid == e1, rank, 0.0), axis=0, keepdims=True).astype(jnp.int32)
    r2 = jnp.sum(jnp.where(eid == e2, rank, 0.0), axis=0, keepdims=True).astype(jnp.int32)
    ri_ref[...] = jnp.concatenate([e1, e2, r1, r2, jnp.zeros((4, tm), jnp.int32)], axis=0)
    wt = jnp.concatenate([w1, w2, jnp.zeros((LANES - 2, tm), F32)], axis=0)
    rw_out_ref[...] = wt.T
    cnt = jnp.sum(jnp.where(onehot, 1.0, 0.0), axis=1, keepdims=True)
    cnt_ref[0] = jnp.broadcast_to(cnt, (N_EXPERTS, LANES)).astype(jnp.int32)


def _output_projection(o, x, mod, w_bf16, ln_g, ln_b, router_w_pad, router_b, tm=OUT_TM):
    nt = N_TOK // tm
    return pl.pallas_call(
        _outproj_kernel,
        out_shape=(jax.ShapeDtypeStruct((N_TOK, D_MODEL), F32),
                   jax.ShapeDtypeStruct((N_TOK, N_CHUNKS, LANES), F32),
                   jax.ShapeDtypeStruct((8, N_TOK), jnp.int32),
                   jax.ShapeDtypeStruct((N_TOK, LANES), F32),
                   jax.ShapeDtypeStruct((nt, N_EXPERTS, LANES), jnp.int32)),
        grid=(nt,),
        in_specs=[
            pl.BlockSpec((tm, D_MODEL), lambda i: (i, 0)),
            pl.BlockSpec((tm, D_MODEL), lambda i: (i, 0)),
            pl.BlockSpec((1, 6, D_MODEL), lambda i: (_cond_row(i * tm), 0, 0)),
            pl.BlockSpec((D_MODEL, D_MODEL), lambda i: (0, 0)),
            pl.BlockSpec((1, D_MODEL), lambda i: (0, 0)),
            pl.BlockSpec((1, D_MODEL), lambda i: (0, 0)),
            pl.BlockSpec((D_MODEL, LANES), lambda i: (0, 0)),
            pl.BlockSpec((N_EXPERTS, 1), lambda i: (0, 0)),
        ],
        out_specs=(pl.BlockSpec((tm, D_MODEL), lambda i: (i, 0)),
                   pl.BlockSpec((tm, N_CHUNKS, LANES), lambda i: (i, 0, 0)),
                   pl.BlockSpec((8, tm), lambda i: (0, i)),
                   pl.BlockSpec((tm, LANES), lambda i: (i, 0)),
                   pl.BlockSpec((1, N_EXPERTS, LANES), lambda i: (i, 0, 0))),
        compiler_params=_params("arbitrary"),
        name="output_projection",
    )(o, x, mod, w_bf16, ln_g.reshape(1, D_MODEL), ln_b.reshape(1, D_MODEL),
      router_w_pad, router_b.reshape(N_EXPERTS, 1))


N_ASSIGN = 2 * N_TOK
MOE_TM = 256
MOE_ROWS = N_ASSIGN + N_EXPERTS * MOE_TM
MOE_TILES = MOE_ROWS // MOE_TM
DISPATCH_TM = 512
COMBINE_TM = 256


def _moe_plan(route_i, counts):
    cnt = counts[:, :, 0]
    total = jnp.sum(cnt, axis=0)
    padded = (total + MOE_TM - 1) // MOE_TM * MOE_TM
    seg_end = jnp.cumsum(padded)
    seg_start = seg_end - padded
    tile_base = seg_start[None, :] + jnp.cumsum(cnt, axis=0) - cnt
    base_tok = jnp.repeat(tile_base, OUT_TM, axis=0)
    eids = jnp.arange(N_EXPERTS, dtype=jnp.int32)[None, :]
    pos = [jnp.sum(jnp.where(route_i[k][:, None] == eids, base_tok, 0), axis=1) + route_i[2 + k]
           for k in range(2)]
    pos = jnp.concatenate(pos).astype(jnp.int32)
    tile_row0 = jnp.arange(MOE_TILES, dtype=jnp.int32) * MOE_TM
    tile_expert = jnp.minimum(jnp.sum(seg_end[None, :] <= tile_row0[:, None], axis=1), N_EXPERTS - 1)
    n_tiles = (seg_end[-1] // MOE_TM).reshape(1)
    return pos, tile_expert.astype(jnp.int32), n_tiles.astype(jnp.int32)


def _row_copy_wait(src_hbm, dst, sem, n_rows):
    pltpu.make_async_copy(src_hbm.at[pl.ds(0, n_rows)], dst.at[pl.ds(0, n_rows)], sem).wait()


def _dispatch_kernel(pos_ref, h_ref, xs_in_ref, xs_ref, sem):
    del xs_in_ref
    tm = h_ref.shape[0]
    t0 = pl.program_id(0) * tm

    def body(r, carry):
        for k in range(2):
            pltpu.make_async_copy(h_ref.at[r], xs_ref.at[pos_ref[k * N_TOK + t0 + r]], sem).start()
        return carry

    lax.fori_loop(0, tm, body, 0, unroll=8)
    for _ in range(2):
        _row_copy_wait(h_ref, xs_ref, sem, tm)


def _moe_dispatch(pos, h2_tiles):
    tm = DISPATCH_TM
    zeros = jnp.zeros((MOE_ROWS, N_CHUNKS, LANES), F32)
    return pl.pallas_call(
        _dispatch_kernel,
        out_shape=jax.ShapeDtypeStruct((MOE_ROWS, N_CHUNKS, LANES), F32),
        grid_spec=pltpu.PrefetchScalarGridSpec(
            num_scalar_prefetch=1,
            grid=(N_TOK // tm,),
            in_specs=[pl.BlockSpec((tm, N_CHUNKS, LANES), lambda i, pos: (i, 0, 0)),
                      pl.BlockSpec(memory_space=pl.ANY)],
            out_specs=pl.BlockSpec(memory_space=pl.ANY),
            scratch_shapes=[pltpu.SemaphoreType.DMA],
        ),
        input_output_aliases={2: 0},
        compiler_params=_params("arbitrary"),
        name="moe_dispatch",
    )(pos, h2_tiles, zeros)


def _expert_kernel(te_ref, nt_ref, xs_ref, wg_ref, wu_ref, wd_ref, ys_ref, wg16_ref, wu16_ref, wd16_ref):
    i = pl.program_id(0)

    @pl.when(i < nt_ref[0])
    def _():
        @pl.when((i == 0) | (te_ref[i] != te_ref[jnp.maximum(i - 1, 0)]))
        def _():
            wg16_ref[...] = wg_ref[0].astype(BF16)
            wu16_ref[...] = wu_ref[0].astype(BF16)
            wd16_ref[...] = wd_ref[0].astype(BF16)

        x = _load_token_tiles(xs_ref).astype(BF16)
        he = _silu(_dot(x, wg16_ref[...])) * _dot(x, wu16_ref[...])
        _store_token_tiles(ys_ref, _dot(he.astype(BF16), wd16_ref[...]))

    @pl.when(i >= nt_ref[0])
    def _():
        ys_ref[...] = jnp.zeros_like(ys_ref)


def _moe_experts(tile_expert, n_tiles, xs, wg, wu, wd):
    row_blk = lambda i, te, nt: (jnp.minimum(i, nt[0] - 1), 0, 0)
    w_blk = lambda i, te, nt: (te[i], 0, 0)
    return pl.pallas_call(
        _expert_kernel,
        out_shape=jax.ShapeDtypeStruct((MOE_ROWS, N_CHUNKS, LANES), F32),
        grid_spec=pltpu.PrefetchScalarGridSpec(
            num_scalar_prefetch=2,
            grid=(MOE_TILES,),
            in_specs=[
                pl.BlockSpec((MOE_TM, N_CHUNKS, LANES), row_blk),
                pl.BlockSpec((1, D_MODEL, EXPERT_FF), w_blk),
                pl.BlockSpec((1, D_MODEL, EXPERT_FF), w_blk),
                pl.BlockSpec((1, EXPERT_FF, D_MODEL), w_blk),
            ],
            out_specs=pl.BlockSpec((MOE_TM, N_CHUNKS, LANES), lambda i, te, nt: (i, 0, 0)),
            scratch_shapes=[pltpu.VMEM((D_MODEL, EXPERT_FF), BF16), pltpu.VMEM((D_MODEL, EXPERT_FF), BF16),
                            pltpu.VMEM((EXPERT_FF, D_MODEL), BF16)],
        ),
        compiler_params=_params("arbitrary"),
        name="moe_experts",
    )(tile_expert, n_tiles, xs, wg, wu, wd)


def _combine_kernel(pos_ref, gw_ref, x1_ref, mod_ref, lng_ref, lnb_ref, ys_ref, o_ref, buf_ref, sem):
    i = pl.program_id(0)
    n = pl.num_programs(0)
    tm = COMBINE_TM

    def issue(tile, slot):
        def body(r, carry):
            for k in range(2):
                p = pos_ref[k * N_TOK + tile * tm + r]
                pltpu.make_async_copy(ys_ref.at[p], buf_ref.at[slot, k, r], sem.at[slot]).start()
            return carry
        lax.fori_loop(0, tm, body, 0, unroll=8)

    @pl.when(i == 0)
    def _():
        issue(0, 0)

    @pl.when(i + 1 < n)
    def _():
        issue(i + 1, (i + 1) % 2)

    slot = i % 2
    for k in range(2):
        _row_copy_wait(ys_ref, buf_ref.at[slot, k], sem.at[slot], tm)
    gw = gw_ref[...]
    f = gw[:, 0:1] * _load_token_tiles(buf_ref.at[slot, 0]) + gw[:, 1:2] * _load_token_tiles(buf_ref.at[slot, 1])
    m = mod_ref[0]
    o_ref[...] = _layer_norm(ALPHA * x1_ref[...] + m[5:6] * f, lng_ref[...], lnb_ref[...])


def _moe_combine(pos, gate_w, x1, mod, ln_g, ln_b, ys):
    tm = COMBINE_TM
    return pl.pallas_call(
        _combine_kernel,
        out_shape=jax.ShapeDtypeStruct((N_TOK, D_MODEL), F32),
        grid_spec=pltpu.PrefetchScalarGridSpec(
            num_scalar_prefetch=1,
            grid=(N_TOK // tm,),
            in_specs=[
                pl.BlockSpec((tm, LANES), lambda i, pos: (i, 0)),
                pl.BlockSpec((tm, D_MODEL), lambda i, pos: (i, 0)),
                pl.BlockSpec((1, 6, D_MODEL), lambda i, pos: (_cond_row(i * tm), 0, 0)),
                pl.BlockSpec((1, D_MODEL), lambda i, pos: (0, 0)),
                pl.BlockSpec((1, D_MODEL), lambda i, pos: (0, 0)),
                pl.BlockSpec(memory_space=pl.ANY),
            ],
            out_specs=pl.BlockSpec((tm, D_MODEL), lambda i, pos: (i, 0)),
            scratch_shapes=[pltpu.VMEM((2, 2, tm, N_CHUNKS, LANES), F32), pltpu.SemaphoreType.DMA((2,))],
        ),
        compiler_params=_params("arbitrary"),
        name="moe_combine",
    )(pos, gate_w, x1, mod, ln_g.reshape(1, D_MODEL), ln_b.reshape(1, D_MODEL), ys)


def _moe_ffn(h2_tiles, route_i, gate_w, counts, wg, wu, wd, x1, mod, ln_g, ln_b):
    pos, tile_expert, n_tiles = _moe_plan(route_i, counts)
    xs = _moe_dispatch(pos, h2_tiles)
    ys = _moe_experts(tile_expert, n_tiles, xs, wg, wu, wd)
    return _moe_combine(pos, gate_w, x1, mod, ln_g, ln_b, ys)


DN_GATE_COL0 = DN_CONV_CH
DN_AB_COL0 = DN_CONV_CH + DN_WIDTH
DN_IN = DN_AB_COL0 + LANES
N_DIRS = 2


def _dn_prep_kernel(q_ref, k_ref, v_ref, wq_ref, wk_ref, wv_ref, qo_ref, ko_ref, vo_ref):
    length = q_ref.shape[0]
    row = lax.broadcasted_iota(jnp.int32, (length, DN_DK), 0)

    def conv(x, w):
        prev = jnp.where(row == 0, 0.0, pltpu.roll(x, 1, axis=0))
        nxt = jnp.where(row == length - 1, 0.0, pltpu.roll(x, length - 1, axis=0))
        return _silu(prev * w[0:1] + x * w[1:2] + nxt * w[2:3])

    def l2n(x):
        return x * lax.rsqrt(jnp.sum(x * x, axis=-1, keepdims=True) + 1e-6)

    qo_ref[0] = l2n(conv(q_ref[...], wq_ref[...])) * (DN_DK ** -0.5)
    ko_ref[0] = l2n(conv(k_ref[...], wk_ref[...]))
    vo_ref[0] = conv(v_ref[...], wv_ref[...])


def _dn_prep(z1, conv_w, nb, length, row_blk0):
    hq, hk, hv = 0, DN_HEADS, 2 * DN_HEADS
    shp = jax.ShapeDtypeStruct((DN_HEADS, nb * length, DN_DK), F32)
    blk = lambda off: pl.BlockSpec((length, DN_DK), lambda b, h: (row_blk0 + b, off + h))
    wblk = lambda off: pl.BlockSpec((3, DN_DK), lambda b, h: (0, off + h))
    oblk = pl.BlockSpec((1, length, DN_DK), lambda b, h: (h, b, 0))
    return pl.pallas_call(
        _dn_prep_kernel,
        out_shape=(shp, shp, shp),
        grid=(nb, DN_HEADS),
        in_specs=[blk(hq), blk(hk), blk(hv), wblk(hq), wblk(hk), wblk(hv)],
        out_specs=(oblk, oblk, oblk),
        compiler_params=_params("arbitrary", "arbitrary"),
        name="deltanet_prep",
    )(z1, z1, z1, conv_w, conv_w, conv_w)


def _dn_gates_kernel(ab_ref, alog_ref, dtb_ref, o_ref):
    length = ab_ref.shape[0]
    x = ab_ref[...]
    a = x + dtb_ref[...]
    softplus = jnp.maximum(a, 0.0) + jnp.log1p(jnp.exp(-jnp.abs(a)))
    g = -jnp.exp(alog_ref[...]) * softplus
    beta = jax.nn.sigmoid(x)
    ri = lax.broadcasted_iota(jnp.int32, (DN_CHUNK, DN_CHUNK), 0)
    ci = lax.broadcasted_iota(jnp.int32, (DN_CHUNK, DN_CHUNK), 1)
    tril = (ri >= ci).astype(F32)
    triu = (ri <= ci).astype(F32)
    lane = lax.broadcasted_iota(jnp.int32, (DN_CHUNK, LANES), 1)
    for c in range(length // DN_CHUNK):
        rows = slice(c * DN_CHUNK, (c + 1) * DN_CHUNK)
        gch = g[rows]
        pre = jnp.dot(tril, gch, precision=lax.Precision.HIGHEST, preferred_element_type=F32)
        suf = jnp.dot(triu, gch, precision=lax.Precision.HIGHEST, preferred_element_type=F32)
        gc = jnp.where(lane < DN_HEADS, pre, suf)
        o_ref[rows, :] = jnp.where(lane < N_DIRS * DN_HEADS, gc, beta[rows])


def _dn_gates(z1, a_log, dt_bias, nb, length, row_blk0):
    pad = LANES - N_DIRS * DN_HEADS
    alog = jnp.pad(a_log.reshape(1, -1).astype(F32), ((0, 0), (0, pad)))
    dtb = jnp.pad(dt_bias.reshape(1, -1).astype(F32), ((0, 0), (0, pad)))
    return pl.pallas_call(
        _dn_gates_kernel,
        out_shape=jax.ShapeDtypeStruct((nb * length, LANES), F32),
        grid=(nb,),
        in_specs=[
            pl.BlockSpec((length, LANES), lambda b: (row_blk0 + b, DN_AB_COL0 // LANES)),
            pl.BlockSpec((1, LANES), lambda b: (0, 0)),
            pl.BlockSpec((1, LANES), lambda b: (0, 0)),
        ],
        out_specs=pl.BlockSpec((length, LANES), lambda b: (b, 0)),
        compiler_params=_params("arbitrary"),
        name="deltanet_gates",
    )(z1, alog, dtb)


def _delta_chunk(q, k, v, beta, gcol, grow, s, lower):
    ri = lax.broadcasted_iota(jnp.int32, (DN_CHUNK, DN_CHUNK), 0)
    ci = lax.broadcasted_iota(jnp.int32, (DN_CHUNK, DN_CHUNK), 1)
    incl = (ri >= ci) if lower else (ri <= ci)
    strict = (ri > ci) if lower else (ri < ci)
    decay = jnp.exp(jnp.where(incl, gcol - grow, -jnp.inf))
    kb = k * beta
    k16 = k.astype(BF16)
    kk = _dot_nt(kb.astype(BF16), k16)
    a_qk = _dot_nt(q.astype(BF16), k16) * decay
    yield
    tri_l = jnp.where(strict, kk * decay, 0.0)

    def off_block(size):
        same = (ri // (2 * size)) == (ci // (2 * size))
        rpar, cpar = (ri // size) % 2, (ci // size) % 2
        return same & ((rpar == 1) & (cpar == 0) if lower else (rpar == 0) & (cpar == 1))

    p = jnp.where(ri == ci, 1.0, 0.0) - jnp.where(off_block(1), tri_l, 0.0)
    size = 2
    while size < DN_CHUNK:
        p16 = p.astype(BF16)
        pc = _dot(p16, jnp.where(off_block(size), tri_l, 0.0).astype(BF16))
        yield
        p = p - _dot(pc.astype(BF16), p16)
        yield
        size *= 2
    eg = jnp.exp(gcol)
    uw = _dot(p.astype(BF16), jnp.concatenate([v * beta, kb * eg], axis=1).astype(BF16))
    yield
    u, w = uw[:, :DN_DV], uw[:, DN_DV:]
    s16 = s.astype(BF16)
    v_new = u - _dot(w.astype(BF16), s16)
    o_inter = _dot((q * eg).astype(BF16), s16)
    yield
    v_new16 = v_new.astype(BF16)
    o = o_inter + _dot(a_qk.astype(BF16), v_new16)
    g_last = gcol[DN_CHUNK - 1:DN_CHUNK] if lower else gcol[0:1]
    kd = k * jnp.exp(g_last - gcol)
    s_new = s * jnp.exp(g_last) + _dot_tn(kd.astype(BF16), v_new16)
    return o, s_new


def _run_interleaved(chains):
    results = [None] * len(chains)
    active = list(enumerate(chains))
    while active:
        still = []
        for idx, gen in active:
            try:
                next(gen)
                still.append((idx, gen))
            except StopIteration as stop:
                results[idx] = stop.value
        active = still
    return results


def _dn_scan_kernel(*refs, has_init, want_final):
    (qf_ref, kf_ref, vf_ref, gf_ref, qb_ref, kb_ref, vb_ref, gb_ref), rest = refs[:8], refs[8:]
    if has_init:
        s0_ref, rest = rest[0], rest[1:]
    of_ref, ob_ref = rest[0], rest[1]
    rest = rest[2:]
    if want_final:
        sf_ref, rest = rest[0], rest[1:]
    s_ref = rest[0]
    c = pl.program_id(1)
    nc = pl.num_programs(1)

    @pl.when(c == 0)
    def _():
        for d in range(N_DIRS):
            for h in range(DN_HEADS):
                if has_init:
                    s_ref[d * DN_HEADS + h] = s0_ref[0, d, h]
                else:
                    s_ref[d * DN_HEADS + h] = jnp.zeros((DN_DK, DN_DV), F32)

    dirs = ((qf_ref, kf_ref, vf_ref, gf_ref, of_ref, True), (qb_ref, kb_ref, vb_ref, gb_ref, ob_ref, False))
    chains = []
    for d, (q_ref, k_ref, v_ref, g_ref, o_ref, lower) in enumerate(dirs):
        gates = g_ref[...]
        gates_t = gates.T
        for h in range(DN_HEADS):
            j = d * DN_HEADS + h
            jb = N_DIRS * DN_HEADS + j
            chains.append(_delta_chunk(q_ref[h], k_ref[h], v_ref[h], gates[:, jb:jb + 1],
                                       gates[:, j:j + 1], gates_t[j:j + 1, :], s_ref[j], lower))
    results = _run_interleaved(chains)
    for d, (_, _, _, _, o_ref, _) in enumerate(dirs):
        for h in range(DN_HEADS):
            j = d * DN_HEADS + h
            o, s_new = results[j]
            o_ref[:, h * DN_DV:(h + 1) * DN_DV] = o
            s_ref[j] = s_new

    if want_final:
        @pl.when(c == nc - 1)
        def _():
            for d in range(N_DIRS):
                for h in range(DN_HEADS):
                    sf_ref[0, d, h] = s_ref[d * DN_HEADS + h]


def _dn_scan(qn, kn, vn, gates, s0, nb, length, want_final):
    nc = length // DN_CHUNK
    has_init = s0 is not None
    fwd = lambda b, c: (0, b * nc + c, 0)
    bwd = lambda b, c: (0, b * nc + (nc - 1 - c), 0)
    hm = lambda im: pl.BlockSpec((DN_HEADS, DN_CHUNK, DN_DK), im)
    in_specs = [hm(fwd), hm(fwd), hm(fwd), pl.BlockSpec((DN_CHUNK, LANES), lambda b, c: (b * nc + c, 0)),
                hm(bwd), hm(bwd), hm(bwd), pl.BlockSpec((DN_CHUNK, LANES), lambda b, c: (b * nc + (nc - 1 - c), 0))]
    args = [qn, kn, vn, gates, qn, kn, vn, gates]
    state_blk = pl.BlockSpec((1, N_DIRS, DN_HEADS, DN_DK, DN_DV), lambda b, c: (b, 0, 0, 0, 0))
    if has_init:
        in_specs.append(state_blk)
        args.append(s0)
    o_shape = jax.ShapeDtypeStruct((nb * length, DN_WIDTH), F32)
    out_shape = [o_shape, o_shape]
    out_specs = [pl.BlockSpec((DN_CHUNK, DN_WIDTH), lambda b, c: (b * nc + c, 0)),
                 pl.BlockSpec((DN_CHUNK, DN_WIDTH), lambda b, c: (b * nc + (nc - 1 - c), 0))]
    if want_final:
        out_shape.append(jax.ShapeDtypeStruct((nb, N_DIRS, DN_HEADS, DN_DK, DN_DV), F32))
        out_specs.append(state_blk)
    return pl.pallas_call(
        functools.partial(_dn_scan_kernel, has_init=has_init, want_final=want_final),
        out_shape=tuple(out_shape),
        grid=(nb, nc),
        in_specs=in_specs,
        out_specs=tuple(out_specs),
        scratch_shapes=[pltpu.VMEM((N_DIRS * DN_HEADS, DN_DK, DN_DV), F32)],
        compiler_params=_params("arbitrary", "arbitrary"),
        name="deltanet_scan",
    )(*args)


DN_MERGE_TM = 256


def _dn_merge_kernel(of_ref, ob_ref, gate_ref, g_ref, o_ref):
    for h in range(DN_HEADS):
        sl = slice(h * DN_DV, (h + 1) * DN_DV)
        o = of_ref[:, sl] + ob_ref[:, sl]
        o = o * lax.rsqrt(jnp.mean(o * o, axis=-1, keepdims=True) + EPS) * g_ref[...]
        o_ref[:, sl] = (o * _silu(gate_ref[:, sl])).astype(o_ref.dtype)


def _dn_merge(o_f, o_b, z1, onorm_g, n_rows, row0):
    tm = DN_MERGE_TM
    blk = pl.BlockSpec((tm, DN_WIDTH), lambda i: (i, 0))
    return pl.pallas_call(
        _dn_merge_kernel,
        out_shape=jax.ShapeDtypeStruct((n_rows, DN_WIDTH), BF16),
        grid=(n_rows // tm,),
        in_specs=[blk, blk,
                  pl.BlockSpec((tm, DN_WIDTH), lambda i: (row0 // tm + i, DN_GATE_COL0 // DN_WIDTH)),
                  pl.BlockSpec((1, DN_DV), lambda i: (0, 0))],
        out_specs=blk,
        compiler_params=_params("arbitrary"),
        name="deltanet_merge",
    )(o_f, o_b, z1, onorm_g.reshape(1, DN_DV))


def _deltanet_stream(z1, conv_w, a_log, dt_bias, onorm_g, s0, nb, length, row0, want_final):
    row_blk0 = row0 // length
    qn, kn, vn = _dn_prep(z1, conv_w, nb, length, row_blk0)
    gates = _dn_gates(z1, a_log, dt_bias, nb, length, row_blk0)
    outs = _dn_scan(qn, kn, vn, gates, s0, nb, length, want_final)
    merged = _dn_merge(outs[0], outs[1], z1, onorm_g, nb * length, row0)
    return merged, (outs[2] if want_final else None)


def kernel(x_prompt, x_sample, cache_nat_k, cache_nat_v, cache_diff_k, cache_diff_v, state_delta, c, c_ctx, w_in_even, w_out_even, nat_bias, diff_lam, diff_subln, w_in_odd, conv_odd, a_log_odd, dt_bias_odd, onorm_odd, w_out_odd, ada_w, ada_b, ln_g, ln_b, router_w, router_b, moe_wg, moe_wu, moe_wd):
    x = jnp.concatenate([x_prompt.reshape(N_PROMPT, D_MODEL), x_sample.reshape(N_SAMPLE, D_MODEL)], axis=0)
    cond = jnp.concatenate([c_ctx[None], c, jnp.zeros((N_COND - 1 - DEC_BATCH, D_MODEL), F32)], axis=0)
    mod = _ada_modulation(cond, ada_w, ada_b).reshape(DEPTH, N_COND, 6, D_MODEL)
    router_w_pad = jnp.pad(router_w, ((0, 0), (0, LANES - N_EXPERTS)))

    lam_init = 0.8 - 0.6 * math.exp(-0.3 * 0)
    z = _input_projection(x, mod[0], w_in_even[0].astype(BF16))
    new_nat_k = z[:N_PROMPT, _NAT_K0:_NAT_V0].reshape(BATCH, 1, SEQ, NAT_HEADS, HEAD_DIM)
    new_nat_v = z[:N_PROMPT, _NAT_V0:_DIFF_Q0].reshape(BATCH, 1, SEQ, NAT_HEADS, HEAD_DIM)
    new_diff_k = z[:N_PROMPT, _DIFF_K0:_DIFF_V0].reshape(BATCH, 1, SEQ, DIFF_HEADS, 2, HEAD_DIM)
    new_diff_v = z[:N_PROMPT, _DIFF_V0:].reshape(BATCH, 1, SEQ, DIFF_HEADS, 2 * HEAD_DIM)
    o_ctx = _context_attention(z, diff_lam[0], diff_subln[0], lam_init)
    o_nat = _nat_latent_attention(z, cache_nat_k[:, 0].reshape(DEC_BATCH, PAST_LEN, NAT_WIDTH),
                                  cache_nat_v[:, 0].reshape(DEC_BATCH, PAST_LEN, NAT_WIDTH),
                                  _nat_bias_table(nat_bias[0]))
    o_diff = _diff_latent_attention(z, cache_diff_k[:, 0].reshape(DEC_BATCH, PAST_LEN, DIFF_WIDTH),
                                    cache_diff_v[:, 0].reshape(DEC_BATCH, PAST_LEN, DIFF_WIDTH),
                                    diff_lam[0], diff_subln[0], lam_init)
    o = jnp.concatenate([o_ctx, jnp.concatenate([o_nat, o_diff], axis=1)], axis=0)
    x1, *routed = _output_projection(o, x, mod[0], w_out_even[0].astype(BF16), ln_g[0, 0], ln_b[0, 0],
                                     router_w_pad, router_b)
    x = _moe_ffn(*routed, moe_wg[0], moe_wu[0], moe_wd[0], x1, mod[0], ln_g[0, 1], ln_b[0, 1])

    w1 = w_in_odd[0]
    w1 = jnp.concatenate([w1[:, :DN_CONV_CH], w1[:, DN_CONV_CH + 4 * DN_HEADS:],
                          w1[:, DN_CONV_CH:DN_CONV_CH + 4 * DN_HEADS],
                          jnp.zeros((D_MODEL, LANES - 4 * DN_HEADS), F32)], axis=1).astype(BF16)
    z1 = _input_projection(x, mod[1], w1, tm=256)
    o_p, new_state = _deltanet_stream(z1, conv_odd[0], a_log_odd[0], dt_bias_odd[0], onorm_odd[0],
                                      None, BATCH, SEQ, 0, True)
    o_s, _ = _deltanet_stream(z1, conv_odd[0], a_log_odd[0], dt_bias_odd[0], onorm_odd[0],
                              state_delta[:, 0], DEC_BATCH, DEC_SEQ, N_PROMPT, False)
    o = jnp.concatenate([o_p, o_s], axis=0)
    x1, *routed = _output_projection(o, x, mod[1], w_out_odd[0].astype(BF16), ln_g[1, 0], ln_b[1, 0],
                                     router_w_pad, router_b)
    x = _moe_ffn(*routed, moe_wg[1], moe_wu[1], moe_wd[1], x1, mod[1], ln_g[1, 1], ln_b[1, 1])

    y_prompt = x[:N_PROMPT].reshape(BATCH, SEQ, D_MODEL)
    y_sample = x[N_PROMPT:].reshape(DEC_BATCH, DEC_SEQ, D_MODEL)
    return (y_prompt, y_sample, new_nat_k, new_nat_v, new_diff_k, new_diff_v, new_state[:, None])
```

```python
import functools
import math

import jax
import jax.numpy as jnp
import numpy as np
from jax import lax
from jax.experimental import pallas as pl
from jax.experimental.pallas import tpu as pltpu

F32 = jnp.float32
BF16 = jnp.bfloat16

D_MODEL = 1024
BATCH = 16
SEQ = 256
DEC_BATCH = 8
DEC_SEQ = 1024
PAST_LEN = 256
GRID_W = 64
N_ROWS = DEC_SEQ // GRID_W

HEAD_DIM = 64
NAT_HEADS = 8
NAT_WIN_H = 8
NAT_WIN_W = 16
DIFF_HEADS = 4
NAT_WIDTH = NAT_HEADS * HEAD_DIM
DIFF_WIDTH = DIFF_HEADS * 2 * HEAD_DIM
EVEN_IN = 3 * NAT_WIDTH + 3 * DIFF_WIDTH
ROPE_THETA = 10000.0

DN_HEADS = 8
DN_DK = 128
DN_DV = 128
DN_QK_WIDTH = DN_HEADS * DN_DK
DN_WIDTH = DN_HEADS * DN_DV
DN_CONV_CH = 2 * DN_QK_WIDTH + DN_WIDTH
DN_CHUNK = 64

N_EXPERTS = 16
N_GROUPS = 4
EXPERTS_PER_GROUP = N_EXPERTS // N_GROUPS
EXPERT_FF = 512

DEPTH = 2
ALPHA = (2 * DEPTH) ** 0.25
EPS = 1e-5

N_PROMPT = BATCH * SEQ
N_SAMPLE = DEC_BATCH * DEC_SEQ
N_TOK = N_PROMPT + N_SAMPLE
N_COND = 16

VMEM_LIMIT = 56 * 1024 * 1024
LANES = 128
N_CHUNKS = D_MODEL // LANES


def _params(*sem):
    return pltpu.CompilerParams(dimension_semantics=sem, vmem_limit_bytes=VMEM_LIMIT)


def _cond_row(row0):
    return jnp.where(row0 < N_PROMPT, 0, 1 + (row0 - N_PROMPT) // DEC_SEQ)


def _silu(x):
    return x * jax.nn.sigmoid(x)


def _layer_norm(r, g, b):
    mu = jnp.mean(r, axis=-1, keepdims=True)
    xc = r - mu
    var = jnp.mean(xc * xc, axis=-1, keepdims=True)
    return xc * lax.rsqrt(var + EPS) * g + b


def _dot(a, b):
    return jnp.dot(a, b, preferred_element_type=F32)


def _dot_nt(a, b):
    return lax.dot_general(a, b, (((1,), (1,)), ((), ())), preferred_element_type=F32)


def _dot_tn(a, b):
    return lax.dot_general(a, b, (((0,), (0,)), ((), ())), preferred_element_type=F32)


ADA_TN = 1536


def _ada_kernel(c_ref, w_ref, b_ref, o_ref):
    s = _silu(c_ref[...])
    o_ref[0] = _dot(s.astype(BF16), w_ref[0].astype(BF16)) + b_ref[0]


def _ada_modulation(cond, ada_w, ada_b):
    n = 6 * D_MODEL
    return pl.pallas_call(
        _ada_kernel,
        out_shape=jax.ShapeDtypeStruct((DEPTH, N_COND, n), F32),
        grid=(DEPTH, n // ADA_TN),
        in_specs=[
            pl.BlockSpec((N_COND, D_MODEL), lambda l, j: (0, 0)),
            pl.BlockSpec((1, D_MODEL, ADA_TN), lambda l, j: (l, 0, j)),
            pl.BlockSpec((1, 1, ADA_TN), lambda l, j: (l, 0, j)),
        ],
        out_specs=pl.BlockSpec((1, N_COND, ADA_TN), lambda l, j: (l, 0, j)),
        compiler_params=_params("arbitrary", "arbitrary"),
        name="ada_modulation",
    )(cond, ada_w, ada_b.reshape(DEPTH, 1, n))


PROJ_TM = 512


def _inproj_kernel(x_ref, mod_ref, w_ref, o_ref):
    m = mod_ref[0]
    h = x_ref[...] * (1.0 + m[1:2]) + m[0:1]
    o_ref[...] = _dot(h.astype(BF16), w_ref[...])


def _input_projection(x, mod, w_bf16, tm=PROJ_TM):
    n = w_bf16.shape[1]
    return pl.pallas_call(
        _inproj_kernel,
        out_shape=jax.ShapeDtypeStruct((N_TOK, n), F32),
        grid=(N_TOK // tm,),
        in_specs=[
            pl.BlockSpec((tm, D_MODEL), lambda i: (i, 0)),
            pl.BlockSpec((1, 6, D_MODEL), lambda i: (_cond_row(i * tm), 0, 0)),
            pl.BlockSpec((D_MODEL, n), lambda i: (0, 0)),
        ],
        out_specs=pl.BlockSpec((tm, n), lambda i: (i, 0)),
        compiler_params=_params("arbitrary"),
        name="input_projection",
    )(x, mod, w_bf16)


_NAT_Q0, _NAT_K0, _NAT_V0 = 0, NAT_WIDTH, 2 * NAT_WIDTH
_DIFF_Q0 = 3 * NAT_WIDTH
_DIFF_K0 = _DIFF_Q0 + DIFF_WIDTH
_DIFF_V0 = _DIFF_K0 + DIFF_WIDTH
ATTN_SCALE = HEAD_DIM ** -0.5


def _diff_lambda(lam_ref, lam_init):
    lp = lam_ref[...]
    return (jnp.exp(jnp.sum(lp[0:1] * lp[1:2], axis=-1, keepdims=True))
            - jnp.exp(jnp.sum(lp[2:3] * lp[3:4], axis=-1, keepdims=True)) + lam_init)


def _softmax_parts(s):
    m = jnp.max(s, axis=-1, keepdims=True)
    e = jnp.exp(s - m)
    return e, jnp.sum(e, axis=-1, keepdims=True)


def _sub_norm(o, g, lam_init):
    ms = jnp.mean(o * o, axis=-1, keepdims=True)
    return o * lax.rsqrt(ms + EPS) * g * (1.0 - lam_init)


def _ctx_attn_kernel(z_ref, lam_ref, subg_ref, o_ref, *, lam_init):
    lam = _diff_lambda(lam_ref, lam_init)
    for h in range(NAT_HEADS):
        c = h * HEAD_DIM
        q = z_ref[:, _NAT_Q0 + c:_NAT_Q0 + c + HEAD_DIM].astype(BF16)
        k = z_ref[:, _NAT_K0 + c:_NAT_K0 + c + HEAD_DIM].astype(BF16)
        v = z_ref[:, _NAT_V0 + c:_NAT_V0 + c + HEAD_DIM].astype(BF16)
        e, l = _softmax_parts(_dot_nt(q, k) * ATTN_SCALE)
        o_ref[:, c:c + HEAD_DIM] = (_dot(e.astype(BF16), v) / l).astype(o_ref.dtype)
    for h in range(DIFF_HEADS):
        c = h * 2 * HEAD_DIM
        parts = []
        v = z_ref[:, _DIFF_V0 + c:_DIFF_V0 + c + 2 * HEAD_DIM].astype(BF16)
        for j in range(2):
            cj = c + j * HEAD_DIM
            q = z_ref[:, _DIFF_Q0 + cj:_DIFF_Q0 + cj + HEAD_DIM].astype(BF16)
            k = z_ref[:, _DIFF_K0 + cj:_DIFF_K0 + cj + HEAD_DIM].astype(BF16)
            e, l = _softmax_parts(_dot_nt(q, k) * ATTN_SCALE)
            parts.append(_dot(e.astype(BF16), v) / l)
        o = parts[0] - lam * parts[1]
        o_ref[:, NAT_WIDTH + c:NAT_WIDTH + c + 2 * HEAD_DIM] = (
            _sub_norm(o, subg_ref[...], lam_init).astype(o_ref.dtype))


def _context_attention(z, lam_p, subln_g, lam_init):
    return pl.pallas_call(
        functools.partial(_ctx_attn_kernel, lam_init=lam_init),
        out_shape=jax.ShapeDtypeStruct((N_PROMPT, D_MODEL), BF16),
        grid=(BATCH,),
        in_specs=[
            pl.BlockSpec((SEQ, EVEN_IN), lambda b: (b, 0)),
            pl.BlockSpec((4, HEAD_DIM), lambda b: (0, 0)),
            pl.BlockSpec((1, 2 * HEAD_DIM), lambda b: (0, 0)),
        ],
        out_specs=pl.BlockSpec((SEQ, D_MODEL), lambda b: (b, 0)),
        compiler_params=_params("arbitrary"),
        name="context_attention",
    )(z, lam_p, subln_g.reshape(1, 2 * HEAD_DIM))


NAT_TQ = 256
NAT_KEYS = 12 * GRID_W
_SAMPLE_BLK0 = N_PROMPT // DEC_SEQ


def _nat_bias_table(rel_bias):
    rows = N_ROWS
    kh = min(NAT_WIN_H, rows)
    kw = NAT_WIN_W
    col = np.arange(GRID_W)
    c_start = np.clip(col - kw // 2, 0, GRID_W - kw)
    col_ok = (col[None, :] >= c_start[:, None]) & (col[None, :] < c_start[:, None] + kw)
    dc = np.clip(col[None, :] - col[:, None], 1 - kw, kw - 1) + (NAT_WIN_W - 1)
    onehot = ((dc[None] == np.arange(2 * kw - 1)[:, None, None]) & col_ok[None]).astype(np.float32)
    tiles = jnp.einsum('hrd,dqk->hrqk', rel_bias.astype(F32), onehot, precision=lax.Precision.HIGHEST)
    tiles = jnp.where(col_ok, tiles, -jnp.inf)
    outside = jnp.full((NAT_HEADS, GRID_W, GRID_W), -jnp.inf, F32)
    out_rows = []
    for r in range(rows):
        start = min(max(r - kh // 2, 0), rows - kh)
        blocks = [tiles[:, a - r + NAT_WIN_H - 1] if start <= a < start + kh else outside
                  for a in range(rows)]
        out_rows.append(jnp.concatenate(blocks, axis=-1))
    full = jnp.stack(out_rows, axis=1).reshape(NAT_HEADS, DEC_SEQ, DEC_SEQ)
    return jnp.concatenate(
        [full[:, qt * NAT_TQ:(qt + 1) * NAT_TQ, _nat_key0(qt):_nat_key0(qt) + NAT_KEYS]
         for qt in range(DEC_SEQ // NAT_TQ)], axis=1)


def _nat_key0(qt):
    half = DEC_SEQ // NAT_TQ // 2
    return (qt >= half) * (DEC_SEQ - NAT_KEYS)


def _nat_latent_kernel(q_ref, k_ref, v_ref, ck_ref, cv_ref, bias_ref, o_ref):
    key0 = pl.multiple_of(_nat_key0(pl.program_id(1)).astype(jnp.int32), DEC_SEQ - NAT_KEYS)
    keys = pl.ds(key0, NAT_KEYS)

    def head(j):
        sl = slice(j * HEAD_DIM, (j + 1) * HEAD_DIM)
        q = q_ref[:, sl].astype(BF16)
        s_loc = _dot_nt(q, k_ref[keys, sl].astype(BF16)) * ATTN_SCALE + bias_ref[j]
        s_ctx = _dot_nt(q, ck_ref[0, :, sl].astype(BF16)) * ATTN_SCALE
        yield
        m = jnp.maximum(jnp.max(s_loc, axis=-1, keepdims=True), jnp.max(s_ctx, axis=-1, keepdims=True))
        yield
        e_loc = jnp.exp(s_loc - m)
        e_ctx = jnp.exp(s_ctx - m)
        l = jnp.sum(e_loc, axis=-1, keepdims=True) + jnp.sum(e_ctx, axis=-1, keepdims=True)
        yield
        o = (_dot(e_loc.astype(BF16), v_ref[keys, sl].astype(BF16))
             + _dot(e_ctx.astype(BF16), cv_ref[0, :, sl].astype(BF16)))
        yield
        return (o / l).astype(o_ref.dtype)

    outs = _run_interleaved([head(j) for j in range(2)])
    for j in range(2):
        o_ref[:, j * HEAD_DIM:(j + 1) * HEAD_DIM] = outs[j]


def _nat_latent_attention(z, ck, cv, bias):
    nq = DEC_SEQ // NAT_TQ
    qblk0 = N_PROMPT // NAT_TQ
    pair = 2 * HEAD_DIM
    return pl.pallas_call(
        _nat_latent_kernel,
        out_shape=jax.ShapeDtypeStruct((N_SAMPLE, NAT_WIDTH), BF16),
        grid=(NAT_HEADS // 2, nq, DEC_BATCH),
        in_specs=[
            pl.BlockSpec((NAT_TQ, pair), lambda hp, qt, b: (qblk0 + b * nq + qt, _NAT_Q0 // pair + hp)),
            pl.BlockSpec((DEC_SEQ, pair), lambda hp, qt, b: (_SAMPLE_BLK0 + b, _NAT_K0 // pair + hp)),
            pl.BlockSpec((DEC_SEQ, pair), lambda hp, qt, b: (_SAMPLE_BLK0 + b, _NAT_V0 // pair + hp)),
            pl.BlockSpec((1, PAST_LEN, pair), lambda hp, qt, b: (b, 0, hp)),
            pl.BlockSpec((1, PAST_LEN, pair), lambda hp, qt, b: (b, 0, hp)),
            pl.BlockSpec((2, NAT_TQ, NAT_KEYS), lambda hp, qt, b: (hp, qt, 0)),
        ],
        out_specs=pl.BlockSpec((NAT_TQ, pair), lambda hp, qt, b: (b * nq + qt, hp)),
        compiler_params=_params("arbitrary", "arbitrary", "arbitrary"),
        name="nat_latent_attention",
    )(z, z, z, ck, cv, bias)


DIFF_TQ = 256


def _rope_tables():
    nf = HEAD_DIM // 4
    t = jnp.arange(DEC_SEQ)
    inv = ROPE_THETA ** (-jnp.arange(nf, dtype=F32) / nf)
    ang_r = (t // GRID_W).astype(F32)[:, None] * inv
    ang_c = (t % GRID_W).astype(F32)[:, None] * inv
    cos = jnp.concatenate([jnp.cos(ang_r)] * 2 + [jnp.cos(ang_c)] * 2, axis=-1)
    sin = jnp.concatenate([-jnp.sin(ang_r), jnp.sin(ang_r), -jnp.sin(ang_c), jnp.sin(ang_c)], axis=-1)
    return jnp.tile(cos, (1, 2)), jnp.tile(sin, (1, 2))


def _rope(x, cos, sin):
    nf = HEAD_DIM // 4
    lane = lax.broadcasted_iota(jnp.int32, x.shape, 1)
    upper = (lane // nf) % 2 == 1
    partner = jnp.where(upper, pltpu.roll(x, nf, axis=1), pltpu.roll(x, x.shape[1] - nf, axis=1))
    return x * cos + partner * sin


def _diff_latent_kernel(q_ref, k_ref, v_ref, ck_ref, cv_ref, cos_ref, sin_ref, lam_ref, subg_ref,
                        o_ref, kr_ref, *, lam_init):
    qt = pl.program_id(2)

    @pl.when(qt == 0)
    def _():
        kr_ref[...] = _rope(k_ref[...], cos_ref[...], sin_ref[...]).astype(BF16)

    lam = _diff_lambda(lam_ref, lam_init)
    row0 = pl.multiple_of(qt * DIFF_TQ, DIFF_TQ)
    q = _rope(q_ref[...], cos_ref[pl.ds(row0, DIFF_TQ), :], sin_ref[pl.ds(row0, DIFF_TQ), :]).astype(BF16)
    v_loc = v_ref[...].astype(BF16)
    v_ctx = cv_ref[0].astype(BF16)
    def softmax_av(j):
        sl = slice(j * HEAD_DIM, (j + 1) * HEAD_DIM)
        s_ctx = _dot_nt(q[:, sl], ck_ref[0, :, sl].astype(BF16)) * ATTN_SCALE
        s_loc = _dot_nt(q[:, sl], kr_ref[:, sl]) * ATTN_SCALE
        yield
        m = jnp.maximum(jnp.max(s_loc, axis=-1, keepdims=True), jnp.max(s_ctx, axis=-1, keepdims=True))
        yield
        e_loc = jnp.exp(s_loc - m)
        e_ctx = jnp.exp(s_ctx - m)
        l = jnp.sum(e_loc, axis=-1, keepdims=True) + jnp.sum(e_ctx, axis=-1, keepdims=True)
        yield
        av = _dot(e_loc.astype(BF16), v_loc) + _dot(e_ctx.astype(BF16), v_ctx)
        yield
        return av / l

    parts = _run_interleaved([softmax_av(j) for j in range(2)])
    o = parts[0] - lam * parts[1]
    o_ref[...] = _sub_norm(o, subg_ref[...], lam_init).astype(o_ref.dtype)


def _diff_latent_attention(z, ck, cv, lam_p, subln_g, lam_init):
    nq = DEC_SEQ // DIFF_TQ
    qblk0 = N_PROMPT // DIFF_TQ
    w = 2 * HEAD_DIM
    cos, sin = _rope_tables()
    return pl.pallas_call(
        functools.partial(_diff_latent_kernel, lam_init=lam_init),
        out_shape=jax.ShapeDtypeStruct((N_SAMPLE, DIFF_WIDTH), BF16),
        grid=(DEC_BATCH, DIFF_HEADS, nq),
        in_specs=[
            pl.BlockSpec((DIFF_TQ, w), lambda b, h, qt: (qblk0 + b * nq + qt, _DIFF_Q0 // w + h)),
            pl.BlockSpec((DEC_SEQ, w), lambda b, h, qt: (_SAMPLE_BLK0 + b, _DIFF_K0 // w + h)),
            pl.BlockSpec((DEC_SEQ, w), lambda b, h, qt: (_SAMPLE_BLK0 + b, _DIFF_V0 // w + h)),
            pl.BlockSpec((1, PAST_LEN, w), lambda b, h, qt: (b, 0, h)),
            pl.BlockSpec((1, PAST_LEN, w), lambda b, h, qt: (b, 0, h)),
            pl.BlockSpec((DEC_SEQ, w), lambda b, h, qt: (0, 0)),
            pl.BlockSpec((DEC_SEQ, w), lambda b, h, qt: (0, 0)),
            pl.BlockSpec((4, HEAD_DIM), lambda b, h, qt: (0, 0)),
            pl.BlockSpec((1, w), lambda b, h, qt: (0, 0)),
        ],
        out_specs=pl.BlockSpec((DIFF_TQ, w), lambda b, h, qt: (b * nq + qt, h)),
        scratch_shapes=[pltpu.VMEM((DEC_SEQ, w), BF16)],
        compiler_params=_params("arbitrary", "arbitrary", "arbitrary"),
        name="diff_latent_attention",
    )(z, z, z, ck, cv, cos, sin, lam_p, subln_g.reshape(1, w))


OUT_TM = 256


def _route(logits_t, rb_col):
    sc = jax.nn.sigmoid(logits_t)
    bi = sc + rb_col
    srow = [sc[e:e + 1] for e in range(N_EXPERTS)]
    brow = [bi[e:e + 1] for e in range(N_EXPERTS)]
    n = EXPERTS_PER_GROUP
    gscore = []
    for g in range(N_GROUPS):
        v = brow[g * n:(g + 1) * n]
        best = None
        for i in range(n):
            for j in range(i + 1, n):
                s = v[i] + v[j]
                best = s if best is None else jnp.maximum(best, s)
        gscore.append(best)
    sel = jnp.zeros_like(gscore[0], dtype=jnp.int32)
    best = gscore[0]
    for g in range(1, N_GROUPS):
        better = gscore[g] > best
        sel = jnp.where(better, g, sel)
        best = jnp.where(better, gscore[g], best)

    def pick_group(rows, i):
        out = rows[(N_GROUPS - 1) * n + i]
        for g in range(N_GROUPS - 2, -1, -1):
            out = jnp.where(sel == g, rows[g * n + i], out)
        return out

    bv = [pick_group(brow, i) for i in range(n)]
    sv = [pick_group(srow, i) for i in range(n)]
    i1 = jnp.zeros_like(sel)
    m1, w1 = bv[0], sv[0]
    for i in range(1, n):
        better = bv[i] > m1
        i1 = jnp.where(better, i, i1)
        m1 = jnp.where(better, bv[i], m1)
        w1 = jnp.where(better, sv[i], w1)
    i2 = jnp.zeros_like(sel)
    m2 = jnp.full_like(m1, -jnp.inf)
    w2 = jnp.zeros_like(w1)
    for i in range(n):
        better = (i1 != i) & (bv[i] > m2)
        i2 = jnp.where(better, i, i2)
        m2 = jnp.where(better, bv[i], m2)
        w2 = jnp.where(better, sv[i], w2)
    tot = w1 + w2
    return sel * n + i1, sel * n + i2, w1 / tot, w2 / tot


def _store_token_tiles(ref, value):
    for c in range(N_CHUNKS):
        ref[:, c, :] = value[:, c * LANES:(c + 1) * LANES]


def _load_token_tiles(ref):
    return jnp.concatenate([ref[:, c, :] for c in range(N_CHUNKS)], axis=1)


def _outproj_kernel(o_ref, x_ref, mod_ref, w_ref, lng_ref, lnb_ref, rw_ref, rb_ref,
                    x1_ref, h2_ref, ri_ref, rw_out_ref, cnt_ref):
    tm = o_ref.shape[0]
    m = mod_ref[0]
    y = _dot(o_ref[...], w_ref[...])
    x1 = _layer_norm(ALPHA * x_ref[...] + m[2:3] * y, lng_ref[...], lnb_ref[...])
    x1_ref[...] = x1
    h2 = x1 * (1.0 + m[4:5]) + m[3:4]
    _store_token_tiles(h2_ref, h2)
    rw = rw_ref[...]
    h_hi, rw_hi = h2.astype(BF16), rw.astype(BF16)
    h_lo = (h2 - h_hi.astype(F32)).astype(BF16)
    rw_lo = (rw - rw_hi.astype(F32)).astype(BF16)
    logits = _dot(h_hi, rw_hi) + (_dot(h_lo, rw_hi) + _dot(h_hi, rw_lo))
    e1, e2, w1, w2 = _route(logits.T[:N_EXPERTS], rb_ref[...])
    eid = lax.broadcasted_iota(jnp.int32, (N_EXPERTS, tm), 0)
    onehot = (eid == e1) | (eid == e2)
    ti = lax.broadcasted_iota(jnp.int32, (tm, tm), 0)
    tj = lax.broadcasted_iota(jnp.int32, (tm, tm), 1)
    earlier = jnp.where(ti < tj, 1.0, 0.0).astype(BF16)
    rank = _dot(jnp.where(onehot, 1.0, 0.0).astype(BF16), earlier)
    r1 = jnp.sum(jnp.where(eid == e1, rank, 0.0), axis=0, keepdims=True).astype(jnp.int32)
    r2 = jnp.sum(jnp.where(eid == e2, rank, 0.0), axis=0, keepdims=True).astype(jnp.int32)
    ri_ref[...] = jnp.concatenate([e1, e2, r1, r2, jnp.zeros((4, tm), jnp.int32)], axis=0)
    wt = jnp.concatenate([w1, w2, jnp.zeros((LANES - 2, tm), F32)], axis=0)
    rw_out_ref[...] = wt.T
    cnt = jnp.sum(jnp.where(onehot, 1.0, 0.0), axis=1, keepdims=True)
    cnt_ref[0] = jnp.broadcast_to(cnt, (N_EXPERTS, LANES)).astype(jnp.int32)


def _output_projection(o, x, mod, w_bf16, ln_g, ln_b, router_w_pad, router_b, tm=OUT_TM):
    nt = N_TOK // tm
    return pl.pallas_call(
        _outproj_kernel,
        out_shape=(jax.ShapeDtypeStruct((N_TOK, D_MODEL), F32),
                   jax.ShapeDtypeStruct((N_TOK, N_CHUNKS, LANES), F32),
                   jax.ShapeDtypeStruct((8, N_TOK), jnp.int32),
                   jax.ShapeDtypeStruct((N_TOK, LANES), F32),
                   jax.ShapeDtypeStruct((nt, N_EXPERTS, LANES), jnp.int32)),
        grid=(nt,),
        in_specs=[
            pl.BlockSpec((tm, D_MODEL), lambda i: (i, 0)),
            pl.BlockSpec((tm, D_MODEL), lambda i: (i, 0)),
            pl.BlockSpec((1, 6, D_MODEL), lambda i: (_cond_row(i * tm), 0, 0)),
            pl.BlockSpec((D_MODEL, D_MODEL), lambda i: (0, 0)),
            pl.BlockSpec((1, D_MODEL), lambda i: (0, 0)),
            pl.BlockSpec((1, D_MODEL), lambda i: (0, 0)),
            pl.BlockSpec((D_MODEL, LANES), lambda i: (0, 0)),
            pl.BlockSpec((N_EXPERTS, 1), lambda i: (0, 0)),
        ],
        out_specs=(pl.BlockSpec((tm, D_MODEL), lambda i: (i, 0)),
                   pl.BlockSpec((tm, N_CHUNKS, LANES), lambda i: (i, 0, 0)),
                   pl.BlockSpec((8, tm), lambda i: (0, i)),
                   pl.BlockSpec((tm, LANES), lambda i: (i, 0)),
                   pl.BlockSpec((1, N_EXPERTS, LANES), lambda i: (i, 0, 0))),
        compiler_params=_params("arbitrary"),
        name="output_projection",
    )(o, x, mod, w_bf16, ln_g.reshape(1, D_MODEL), ln_b.reshape(1, D_MODEL),
      router_w_pad, router_b.reshape(N_EXPERTS, 1))


N_ASSIGN = 2 * N_TOK
MOE_TM = 256
MOE_ROWS = N_ASSIGN + N_EXPERTS * MOE_TM
MOE_TILES = MOE_ROWS // MOE_TM
DISPATCH_TM = 512
COMBINE_TM = 256


def _moe_plan(route_i, counts):
    cnt = counts[:, :, 0]
    total = jnp.sum(cnt, axis=0)
    padded = (total + MOE_TM - 1) // MOE_TM * MOE_TM
    seg_end = jnp.cumsum(padded)
    seg_start = seg_end - padded
    tile_base = seg_start[None, :] + jnp.cumsum(cnt, axis=0) - cnt
    base_tok = jnp.repeat(tile_base, OUT_TM, axis=0)
    eids = jnp.arange(N_EXPERTS, dtype=jnp.int32)[None, :]
    pos = [jnp.sum(jnp.where(route_i[k][:, None] == eids, base_tok, 0), axis=1) + route_i[2 + k]
           for k in range(2)]
    pos = jnp.concatenate(pos).astype(jnp.int32)
    tile_row0 = jnp.arange(MOE_TILES, dtype=jnp.int32) * MOE_TM
    tile_expert = jnp.minimum(jnp.sum(seg_end[None, :] <= tile_row0[:, None], axis=1), N_EXPERTS - 1)
    n_tiles = (seg_end[-1] // MOE_TM).reshape(1)
    last = MOE_ROWS - MOE_TM
    fill_rows = jnp.concatenate([jnp.minimum(seg_start + total, last),
                                 last - jnp.arange(N_EXPERTS, dtype=jnp.int32) * MOE_TM])
    return pos, tile_expert.astype(jnp.int32), n_tiles.astype(jnp.int32), fill_rows.astype(jnp.int32)


def _row_copy_wait(src_hbm, dst, sem, n_rows):
    pltpu.make_async_copy(src_hbm.at[pl.ds(0, n_rows)], dst.at[pl.ds(0, n_rows)], sem).wait()


def _dispatch_kernel(pos_ref, fill_ref, h_ref, xs_ref, zero_ref, sem, fill_sem):
    tm = h_ref.shape[0]
    t0 = pl.program_id(0) * tm

    @pl.when(pl.program_id(0) == 0)
    def _():
        zero_ref[...] = jnp.zeros_like(zero_ref)
        fills = [pltpu.make_async_copy(zero_ref, xs_ref.at[pl.ds(fill_ref[j], MOE_TM)], fill_sem)
                 for j in range(2 * N_EXPERTS)]
        for cp in fills[N_EXPERTS:]:
            cp.start()
        for cp in fills[N_EXPERTS:]:
            cp.wait()
        for cp in fills[:N_EXPERTS]:
            cp.start()
            cp.wait()

    def body(r, carry):
        for k in range(2):
            pltpu.make_async_copy(h_ref.at[r], xs_ref.at[pos_ref[k * N_TOK + t0 + r]], sem).start()
        return carry

    lax.fori_loop(0, tm, body, 0, unroll=8)
    for _ in range(2):
        _row_copy_wait(h_ref, xs_ref, sem, tm)


def _moe_dispatch(pos, fill_rows, h2_tiles):
    tm = DISPATCH_TM
    return pl.pallas_call(
        _dispatch_kernel,
        out_shape=jax.ShapeDtypeStruct((MOE_ROWS, N_CHUNKS, LANES), F32),
        grid_spec=pltpu.PrefetchScalarGridSpec(
            num_scalar_prefetch=2,
            grid=(N_TOK // tm,),
            in_specs=[pl.BlockSpec((tm, N_CHUNKS, LANES), lambda i, pos, fill: (i, 0, 0))],
            out_specs=pl.BlockSpec(memory_space=pl.ANY),
            scratch_shapes=[pltpu.VMEM((MOE_TM, N_CHUNKS, LANES), F32), pltpu.SemaphoreType.DMA,
                            pltpu.SemaphoreType.DMA],
        ),
        compiler_params=_params("arbitrary"),
        name="moe_dispatch",
    )(pos, fill_rows, h2_tiles)


def _expert_kernel(te_ref, nt_ref, xs_ref, wg_ref, wu_ref, wd_ref, ys_ref, wg16_ref, wu16_ref, wd16_ref):
    i = pl.program_id(0)

    @pl.when(i < nt_ref[0])
    def _():
        @pl.when((i == 0) | (te_ref[i] != te_ref[jnp.maximum(i - 1, 0)]))
        def _():
            wg16_ref[...] = wg_ref[0].astype(BF16)
            wu16_ref[...] = wu_ref[0].astype(BF16)
            wd16_ref[...] = wd_ref[0].astype(BF16)

        x = _load_token_tiles(xs_ref).astype(BF16)
        he = _silu(_dot(x, wg16_ref[...])) * _dot(x, wu16_ref[...])
        _store_token_tiles(ys_ref, _dot(he.astype(BF16), wd16_ref[...]))

    @pl.when(i >= nt_ref[0])
    def _():
        ys_ref[...] = jnp.zeros_like(ys_ref)


def _moe_experts(tile_expert, n_tiles, xs, wg, wu, wd):
    row_blk = lambda i, te, nt: (jnp.minimum(i, nt[0] - 1), 0, 0)
    w_blk = lambda i, te, nt: (te[i], 0, 0)
    return pl.pallas_call(
        _expert_kernel,
        out_shape=jax.ShapeDtypeStruct((MOE_ROWS, N_CHUNKS, LANES), F32),
        grid_spec=pltpu.PrefetchScalarGridSpec(
            num_scalar_prefetch=2,
            grid=(MOE_TILES,),
            in_specs=[
                pl.BlockSpec((MOE_TM, N_CHUNKS, LANES), row_blk),
                pl.BlockSpec((1, D_MODEL, EXPERT_FF), w_blk),
                pl.BlockSpec((1, D_MODEL, EXPERT_FF), w_blk),
                pl.BlockSpec((1, EXPERT_FF, D_MODEL), w_blk),
            ],
            out_specs=pl.BlockSpec((MOE_TM, N_CHUNKS, LANES), lambda i, te, nt: (i, 0, 0)),
            scratch_shapes=[pltpu.VMEM((D_MODEL, EXPERT_FF), BF16), pltpu.VMEM((D_MODEL, EXPERT_FF), BF16),
                            pltpu.VMEM((EXPERT_FF, D_MODEL), BF16)],
        ),
        compiler_params=_params("arbitrary"),
        name="moe_experts",
    )(tile_expert, n_tiles, xs, wg, wu, wd)


def _combine_kernel(pos_ref, gw_ref, x1_ref, mod_ref, lng_ref, lnb_ref, ys_ref, *rest, split):
    out_refs, (buf_ref, sem) = rest[:-2], rest[-2:]
    i = pl.program_id(0)
    n = pl.num_programs(0)
    tm = COMBINE_TM

    def issue(tile, slot):
        def body(r, carry):
            for k in range(2):
                p = pos_ref[k * N_TOK + tile * tm + r]
                pltpu.make_async_copy(ys_ref.at[p], buf_ref.at[slot, k, r], sem.at[slot]).start()
            return carry
        lax.fori_loop(0, tm, body, 0, unroll=8)

    @pl.when(i == 0)
    def _():
        issue(0, 0)

    @pl.when(i + 1 < n)
    def _():
        issue(i + 1, (i + 1) % 2)

    slot = i % 2
    for k in range(2):
        _row_copy_wait(ys_ref, buf_ref.at[slot, k], sem.at[slot], tm)
    gw = gw_ref[...]
    f = gw[:, 0:1] * _load_token_tiles(buf_ref.at[slot, 0]) + gw[:, 1:2] * _load_token_tiles(buf_ref.at[slot, 1])
    m = mod_ref[0]
    out = _layer_norm(ALPHA * x1_ref[...] + m[5:6] * f, lng_ref[...], lnb_ref[...])
    if split:
        prompt_ref, sample_ref = out_refs

        @pl.when(i < N_PROMPT // tm)
        def _():
            prompt_ref[...] = out

        @pl.when(i >= N_PROMPT // tm)
        def _():
            sample_ref[...] = out
    else:
        out_refs[0][...] = out


def _moe_combine(pos, gate_w, x1, mod, ln_g, ln_b, ys, split):
    tm = COMBINE_TM
    np_blk = N_PROMPT // tm
    if split:
        out_shape = (jax.ShapeDtypeStruct((N_PROMPT, D_MODEL), F32), jax.ShapeDtypeStruct((N_SAMPLE, D_MODEL), F32))
        out_specs = (pl.BlockSpec((tm, D_MODEL), lambda i, pos: (jnp.minimum(i, np_blk - 1), 0)),
                     pl.BlockSpec((tm, D_MODEL), lambda i, pos: (jnp.maximum(i - np_blk, 0), 0)))
    else:
        out_shape = jax.ShapeDtypeStruct((N_TOK, D_MODEL), F32)
        out_specs = pl.BlockSpec((tm, D_MODEL), lambda i, pos: (i, 0))
    return pl.pallas_call(
        functools.partial(_combine_kernel, split=split),
        out_shape=out_shape,
        grid_spec=pltpu.PrefetchScalarGridSpec(
            num_scalar_prefetch=1,
            grid=(N_TOK // tm,),
            in_specs=[
                pl.BlockSpec((tm, LANES), lambda i, pos: (i, 0)),
                pl.BlockSpec((tm, D_MODEL), lambda i, pos: (i, 0)),
                pl.BlockSpec((1, 6, D_MODEL), lambda i, pos: (_cond_row(i * tm), 0, 0)),
                pl.BlockSpec((1, D_MODEL), lambda i, pos: (0, 0)),
                pl.BlockSpec((1, D_MODEL), lambda i, pos: (0, 0)),
                pl.BlockSpec(memory_space=pl.ANY),
            ],
            out_specs=out_specs,
            scratch_shapes=[pltpu.VMEM((2, 2, tm, N_CHUNKS, LANES), F32), pltpu.SemaphoreType.DMA((2,))],
        ),
        compiler_params=_params("arbitrary"),
        name="moe_combine",
    )(pos, gate_w, x1, mod, ln_g.reshape(1, D_MODEL), ln_b.reshape(1, D_MODEL), ys)


def _moe_ffn(h2_tiles, route_i, gate_w, counts, wg, wu, wd, x1, mod, ln_g, ln_b, split=False):
    pos, tile_expert, n_tiles, fill_rows = _moe_plan(route_i, counts)
    xs = _moe_dispatch(pos, fill_rows, h2_tiles)
    ys = _moe_experts(tile_expert, n_tiles, xs, wg, wu, wd)
    return _moe_combine(pos, gate_w, x1, mod, ln_g, ln_b, ys, split)


DN_GATE_COL0 = DN_CONV_CH
DN_AB_COL0 = DN_CONV_CH + DN_WIDTH
DN_IN = DN_AB_COL0 + LANES
N_DIRS = 2


def _dn_prep_kernel(q_ref, k_ref, v_ref, wq_ref, wk_ref, wv_ref, qo_ref, ko_ref, vo_ref):
    length = q_ref.shape[0]
    row = lax.broadcasted_iota(jnp.int32, (length, DN_DK), 0)

    def conv(x, w):
        prev = jnp.where(row == 0, 0.0, pltpu.roll(x, 1, axis=0))
        nxt = jnp.where(row == length - 1, 0.0, pltpu.roll(x, length - 1, axis=0))
        return _silu(prev * w[0:1] + x * w[1:2] + nxt * w[2:3])

    def l2n(x):
        return x * lax.rsqrt(jnp.sum(x * x, axis=-1, keepdims=True) + 1e-6)

    qo_ref[0] = l2n(conv(q_ref[...], wq_ref[...])) * (DN_DK ** -0.5)
    ko_ref[0] = l2n(conv(k_ref[...], wk_ref[...]))
    vo_ref[0] = conv(v_ref[...], wv_ref[...])


def _dn_prep(z1, conv_w, nb, length, row_blk0):
    hq, hk, hv = 0, DN_HEADS, 2 * DN_HEADS
    shp = jax.ShapeDtypeStruct((DN_HEADS, nb * length, DN_DK), F32)
    blk = lambda off: pl.BlockSpec((length, DN_DK), lambda b, h: (row_blk0 + b, off + h))
    wblk = lambda off: pl.BlockSpec((3, DN_DK), lambda b, h: (0, off + h))
    oblk = pl.BlockSpec((1, length, DN_DK), lambda b, h: (h, b, 0))
    return pl.pallas_call(
        _dn_prep_kernel,
        out_shape=(shp, shp, shp),
        grid=(nb, DN_HEADS),
        in_specs=[blk(hq), blk(hk), blk(hv), wblk(hq), wblk(hk), wblk(hv)],
        out_specs=(oblk, oblk, oblk),
        compiler_params=_params("arbitrary", "arbitrary"),
        name="deltanet_prep",
    )(z1, z1, z1, conv_w, conv_w, conv_w)


def _dn_gates_kernel(ab_ref, alog_ref, dtb_ref, o_ref):
    length = ab_ref.shape[0]
    x = ab_ref[...]
    a = x + dtb_ref[...]
    softplus = jnp.maximum(a, 0.0) + jnp.log1p(jnp.exp(-jnp.abs(a)))
    g = -jnp.exp(alog_ref[...]) * softplus
    beta = jax.nn.sigmoid(x)
    ri = lax.broadcasted_iota(jnp.int32, (DN_CHUNK, DN_CHUNK), 0)
    ci = lax.broadcasted_iota(jnp.int32, (DN_CHUNK, DN_CHUNK), 1)
    tril = (ri >= ci).astype(F32)
    triu = (ri <= ci).astype(F32)
    lane = lax.broadcasted_iota(jnp.int32, (DN_CHUNK, LANES), 1)
    for c in range(length // DN_CHUNK):
        rows = slice(c * DN_CHUNK, (c + 1) * DN_CHUNK)
        gch = g[rows]
        pre = jnp.dot(tril, gch, precision=lax.Precision.HIGHEST, preferred_element_type=F32)
        suf = jnp.dot(triu, gch, precision=lax.Precision.HIGHEST, preferred_element_type=F32)
        gc = jnp.where(lane < DN_HEADS, pre, suf)
        o_ref[rows, :] = jnp.where(lane < N_DIRS * DN_HEADS, gc, beta[rows])


def _dn_gates(z1, a_log, dt_bias, nb, length, row_blk0):
    pad = LANES - N_DIRS * DN_HEADS
    alog = jnp.pad(a_log.reshape(1, -1).astype(F32), ((0, 0), (0, pad)))
    dtb = jnp.pad(dt_bias.reshape(1, -1).astype(F32), ((0, 0), (0, pad)))
    return pl.pallas_call(
        _dn_gates_kernel,
        out_shape=jax.ShapeDtypeStruct((nb * length, LANES), F32),
        grid=(nb,),
        in_specs=[
            pl.BlockSpec((length, LANES), lambda b: (row_blk0 + b, DN_AB_COL0 // LANES)),
            pl.BlockSpec((1, LANES), lambda b: (0, 0)),
            pl.BlockSpec((1, LANES), lambda b: (0, 0)),
        ],
        out_specs=pl.BlockSpec((length, LANES), lambda b: (b, 0)),
        compiler_params=_params("arbitrary"),
        name="deltanet_gates",
    )(z1, alog, dtb)


def _delta_chunk(q, k, v, beta, gcol, grow, s, lower):
    ri = lax.broadcasted_iota(jnp.int32, (DN_CHUNK, DN_CHUNK), 0)
    ci = lax.broadcasted_iota(jnp.int32, (DN_CHUNK, DN_CHUNK), 1)
    incl = (ri >= ci) if lower else (ri <= ci)
    strict = (ri > ci) if lower else (ri < ci)
    decay = jnp.exp(jnp.where(incl, gcol - grow, -jnp.inf))
    kb = k * beta
    k16 = k.astype(BF16)
    kk = _dot_nt(kb.astype(BF16), k16)
    a_qk = _dot_nt(q.astype(BF16), k16) * decay
    yield
    tri_l = jnp.where(strict, kk * decay, 0.0)

    def off_block(size):
        same = (ri // (2 * size)) == (ci // (2 * size))
        rpar, cpar = (ri // size) % 2, (ci // size) % 2
        return same & ((rpar == 1) & (cpar == 0) if lower else (rpar == 0) & (cpar == 1))

    p = jnp.where(ri == ci, 1.0, 0.0) - jnp.where(off_block(1), tri_l, 0.0)
    size = 2
    while size < DN_CHUNK:
        p16 = p.astype(BF16)
        pc = _dot(p16, jnp.where(off_block(size), tri_l, 0.0).astype(BF16))
        yield
        p = p - _dot(pc.astype(BF16), p16)
        yield
        size *= 2
    eg = jnp.exp(gcol)
    uw = _dot(p.astype(BF16), jnp.concatenate([v * beta, kb * eg], axis=1).astype(BF16))
    yield
    u, w = uw[:, :DN_DV], uw[:, DN_DV:]
    s16 = s.astype(BF16)
    v_new = u - _dot(w.astype(BF16), s16)
    o_inter = _dot((q * eg).astype(BF16), s16)
    yield
    v_new16 = v_new.astype(BF16)
    o = o_inter + _dot(a_qk.astype(BF16), v_new16)
    g_last = gcol[DN_CHUNK - 1:DN_CHUNK] if lower else gcol[0:1]
    kd = k * jnp.exp(g_last - gcol)
    s_new = s * jnp.exp(g_last) + _dot_tn(kd.astype(BF16), v_new16)
    return o, s_new


def _run_interleaved(chains):
    results = [None] * len(chains)
    active = list(enumerate(chains))
    while active:
        still = []
        for idx, gen in active:
            try:
                next(gen)
                still.append((idx, gen))
            except StopIteration as stop:
                results[idx] = stop.value
        active = still
    return results


def _dn_scan_kernel(*refs, has_init, want_final):
    (qf_ref, kf_ref, vf_ref, gf_ref, qb_ref, kb_ref, vb_ref, gb_ref), rest = refs[:8], refs[8:]
    if has_init:
        s0_ref, rest = rest[0], rest[1:]
    of_ref, ob_ref = rest[0], rest[1]
    rest = rest[2:]
    if want_final:
        sf_ref, rest = rest[0], rest[1:]
    s_ref = rest[0]
    c = pl.program_id(1)
    nc = pl.num_programs(1)

    @pl.when(c == 0)
    def _():
        for d in range(N_DIRS):
            for h in range(DN_HEADS):
                if has_init:
                    s_ref[d * DN_HEADS + h] = s0_ref[0, d, h]
                else:
                    s_ref[d * DN_HEADS + h] = jnp.zeros((DN_DK, DN_DV), F32)

    dirs = ((qf_ref, kf_ref, vf_ref, gf_ref, of_ref, True), (qb_ref, kb_ref, vb_ref, gb_ref, ob_ref, False))
    chains = []
    for d, (q_ref, k_ref, v_ref, g_ref, o_ref, lower) in enumerate(dirs):
        gates = g_ref[...]
        gates_t = gates.T
        for h in range(DN_HEADS):
            j = d * DN_HEADS + h
            jb = N_DIRS * DN_HEADS + j
            chains.append(_delta_chunk(q_ref[h], k_ref[h], v_ref[h], gates[:, jb:jb + 1],
                                       gates[:, j:j + 1], gates_t[j:j + 1, :], s_ref[j], lower))
    results = _run_interleaved(chains)
    for d, (_, _, _, _, o_ref, _) in enumerate(dirs):
        for h in range(DN_HEADS):
            j = d * DN_HEADS + h
            o, s_new = results[j]
            o_ref[:, h * DN_DV:(h + 1) * DN_DV] = o
            s_ref[j] = s_new

    if want_final:
        @pl.when(c == nc - 1)
        def _():
            for d in range(N_DIRS):
                for h in range(DN_HEADS):
                    sf_ref[0, d, h] = s_ref[d * DN_HEADS + h]


def _dn_scan(qn, kn, vn, gates, s0, nb, length, want_final):
    nc = length // DN_CHUNK
    has_init = s0 is not None
    fwd = lambda b, c: (0, b * nc + c, 0)
    bwd = lambda b, c: (0, b * nc + (nc - 1 - c), 0)
    hm = lambda im: pl.BlockSpec((DN_HEADS, DN_CHUNK, DN_DK), im)
    in_specs = [hm(fwd), hm(fwd), hm(fwd), pl.BlockSpec((DN_CHUNK, LANES), lambda b, c: (b * nc + c, 0)),
                hm(bwd), hm(bwd), hm(bwd), pl.BlockSpec((DN_CHUNK, LANES), lambda b, c: (b * nc + (nc - 1 - c), 0))]
    args = [qn, kn, vn, gates, qn, kn, vn, gates]
    state_blk = pl.BlockSpec((1, N_DIRS, DN_HEADS, DN_DK, DN_DV), lambda b, c: (b, 0, 0, 0, 0))
    if has_init:
        in_specs.append(state_blk)
        args.append(s0)
    o_shape = jax.ShapeDtypeStruct((nb * length, DN_WIDTH), F32)
    out_shape = [o_shape, o_shape]
    out_specs = [pl.BlockSpec((DN_CHUNK, DN_WIDTH), lambda b, c: (b * nc + c, 0)),
                 pl.BlockSpec((DN_CHUNK, DN_WIDTH), lambda b, c: (b * nc + (nc - 1 - c), 0))]
    if want_final:
        out_shape.append(jax.ShapeDtypeStruct((nb, N_DIRS, DN_HEADS, DN_DK, DN_DV), F32))
        out_specs.append(state_blk)
    return pl.pallas_call(
        functools.partial(_dn_scan_kernel, has_init=has_init, want_final=want_final),
        out_shape=tuple(out_shape),
        grid=(nb, nc),
        in_specs=in_specs,
        out_specs=tuple(out_specs),
        scratch_shapes=[pltpu.VMEM((N_DIRS * DN_HEADS, DN_DK, DN_DV), F32)],
        compiler_params=_params("arbitrary", "arbitrary"),
        name="deltanet_scan",
    )(*args)


DN_MERGE_TM = 256


def _dn_merge_kernel(of_ref, ob_ref, gate_ref, g_ref, o_ref):
    for h in range(DN_HEADS):
        sl = slice(h * DN_DV, (h + 1) * DN_DV)
        o = of_ref[:, sl] + ob_ref[:, sl]
        o = o * lax.rsqrt(jnp.mean(o * o, axis=-1, keepdims=True) + EPS) * g_ref[...]
        o_ref[:, sl] = (o * _silu(gate_ref[:, sl])).astype(o_ref.dtype)


def _dn_merge(o_f, o_b, z1, onorm_g, n_rows, row0):
    tm = DN_MERGE_TM
    blk = pl.BlockSpec((tm, DN_WIDTH), lambda i: (i, 0))
    return pl.pallas_call(
        _dn_merge_kernel,
        out_shape=jax.ShapeDtypeStruct((n_rows, DN_WIDTH), BF16),
        grid=(n_rows // tm,),
        in_specs=[blk, blk,
                  pl.BlockSpec((tm, DN_WIDTH), lambda i: (row0 // tm + i, DN_GATE_COL0 // DN_WIDTH)),
                  pl.BlockSpec((1, DN_DV), lambda i: (0, 0))],
        out_specs=blk,
        compiler_params=_params("arbitrary"),
        name="deltanet_merge",
    )(o_f, o_b, z1, onorm_g.reshape(1, DN_DV))


def _deltanet_stream(z1, conv_w, a_log, dt_bias, onorm_g, s0, nb, length, row0, want_final):
    row_blk0 = row0 // length
    qn, kn, vn = _dn_prep(z1, conv_w, nb, length, row_blk0)
    gates = _dn_gates(z1, a_log, dt_bias, nb, length, row_blk0)
    outs = _dn_scan(qn, kn, vn, gates, s0, nb, length, want_final)
    merged = _dn_merge(outs[0], outs[1], z1, onorm_g, nb * length, row0)
    return merged, (outs[2] if want_final else None)


def kernel(x_prompt, x_sample, cache_nat_k, cache_nat_v, cache_diff_k, cache_diff_v, state_delta, c, c_ctx, w_in_even, w_out_even, nat_bias, diff_lam, diff_subln, w_in_odd, conv_odd, a_log_odd, dt_bias_odd, onorm_odd, w_out_odd, ada_w, ada_b, ln_g, ln_b, router_w, router_b, moe_wg, moe_wu, moe_wd):
    x = jnp.concatenate([x_prompt.reshape(N_PROMPT, D_MODEL), x_sample.reshape(N_SAMPLE, D_MODEL)], axis=0)
    cond = jnp.concatenate([c_ctx[None], c, jnp.zeros((N_COND - 1 - DEC_BATCH, D_MODEL), F32)], axis=0)
    mod = _ada_modulation(cond, ada_w, ada_b).reshape(DEPTH, N_COND, 6, D_MODEL)
    router_w_pad = jnp.pad(router_w, ((0, 0), (0, LANES - N_EXPERTS)))

    lam_init = 0.8 - 0.6 * math.exp(-0.3 * 0)
    z = _input_projection(x, mod[0], w_in_even[0].astype(BF16))
    new_nat_k = z[:N_PROMPT, _NAT_K0:_NAT_V0].reshape(BATCH, 1, SEQ, NAT_HEADS, HEAD_DIM)
    new_nat_v = z[:N_PROMPT, _NAT_V0:_DIFF_Q0].reshape(BATCH, 1, SEQ, NAT_HEADS, HEAD_DIM)
    new_diff_k = z[:N_PROMPT, _DIFF_K0:_DIFF_V0].reshape(BATCH, 1, SEQ, DIFF_HEADS, 2, HEAD_DIM)
    new_diff_v = z[:N_PROMPT, _DIFF_V0:].reshape(BATCH, 1, SEQ, DIFF_HEADS, 2 * HEAD_DIM)
    o_ctx = _context_attention(z, diff_lam[0], diff_subln[0], lam_init)
    o_nat = _nat_latent_attention(z, cache_nat_k[:, 0].reshape(DEC_BATCH, PAST_LEN, NAT_WIDTH),
                                  cache_nat_v[:, 0].reshape(DEC_BATCH, PAST_LEN, NAT_WIDTH),
                                  _nat_bias_table(nat_bias[0]))
    o_diff = _diff_latent_attention(z, cache_diff_k[:, 0].reshape(DEC_BATCH, PAST_LEN, DIFF_WIDTH),
                                    cache_diff_v[:, 0].reshape(DEC_BATCH, PAST_LEN, DIFF_WIDTH),
                                    diff_lam[0], diff_subln[0], lam_init)
    o = jnp.concatenate([o_ctx, jnp.concatenate([o_nat, o_diff], axis=1)], axis=0)
    x1, *routed = _output_projection(o, x, mod[0], w_out_even[0].astype(BF16), ln_g[0, 0], ln_b[0, 0],
                                     router_w_pad, router_b)
    x = _moe_ffn(*routed, moe_wg[0], moe_wu[0], moe_wd[0], x1, mod[0], ln_g[0, 1], ln_b[0, 1])

    w1 = w_in_odd[0]
    w1 = jnp.concatenate([w1[:, :DN_CONV_CH], w1[:, DN_CONV_CH + 4 * DN_HEADS:],
                          w1[:, DN_CONV_CH:DN_CONV_CH + 4 * DN_HEADS],
                          jnp.zeros((D_MODEL, LANES - 4 * DN_HEADS), F32)], axis=1).astype(BF16)
    z1 = _input_projection(x, mod[1], w1, tm=256)
    o_p, new_state = _deltanet_stream(z1, conv_odd[0], a_log_odd[0], dt_bias_odd[0], onorm_odd[0],
                                      None, BATCH, SEQ, 0, True)
    o_s, _ = _deltanet_stream(z1, conv_odd[0], a_log_odd[0], dt_bias_odd[0], onorm_odd[0],
                              state_delta[:, 0], DEC_BATCH, DEC_SEQ, N_PROMPT, False)
    o = jnp.concatenate([o_p, o_s], axis=0)
    x1, *routed = _output_projection(o, x, mod[1], w_out_odd[0].astype(BF16), ln_g[1, 0], ln_b[1, 0],
                                     router_w_pad, router_b)
    y_prompt, y_sample = _moe_ffn(*routed, moe_wg[1], moe_wu[1], moe_wd[1], x1, mod[1], ln_g[1, 1], ln_b[1, 1],
                                  split=True)
    y_prompt = y_prompt.reshape(BATCH, SEQ, D_MODEL)
    y_sample = y_sample.reshape(DEC_BATCH, DEC_SEQ, D_MODEL)
    return (y_prompt, y_sample, new_nat_k, new_nat_v, new_diff_k, new_diff_v, new_state[:, None])
```

```python
import functools
import math

import jax
import jax.numpy as jnp
import numpy as np
from jax import lax
from jax.experimental import pallas as pl
from jax.experimental.pallas import tpu as pltpu

F32 = jnp.float32
BF16 = jnp.bfloat16

D_MODEL = 1024
BATCH = 16
SEQ = 256
DEC_BATCH = 8
DEC_SEQ = 1024
PAST_LEN = 256
GRID_W = 64
N_ROWS = DEC_SEQ // GRID_W

HEAD_DIM = 64
NAT_HEADS = 8
NAT_WIN_H = 8
NAT_WIN_W = 16
DIFF_HEADS = 4
NAT_WIDTH = NAT_HEADS * HEAD_DIM
DIFF_WIDTH = DIFF_HEADS * 2 * HEAD_DIM
EVEN_IN = 3 * NAT_WIDTH + 3 * DIFF_WIDTH
ROPE_THETA = 10000.0

DN_HEADS = 8
DN_DK = 128
DN_DV = 128
DN_QK_WIDTH = DN_HEADS * DN_DK
DN_WIDTH = DN_HEADS * DN_DV
DN_CONV_CH = 2 * DN_QK_WIDTH + DN_WIDTH
DN_CHUNK = 64

N_EXPERTS = 16
N_GROUPS = 4
EXPERTS_PER_GROUP = N_EXPERTS // N_GROUPS
EXPERT_FF = 512

DEPTH = 2
ALPHA = (2 * DEPTH) ** 0.25
EPS = 1e-5

N_PROMPT = BATCH * SEQ
N_SAMPLE = DEC_BATCH * DEC_SEQ
N_TOK = N_PROMPT + N_SAMPLE
N_COND = 16

VMEM_LIMIT = 56 * 1024 * 1024
LANES = 128
N_CHUNKS = D_MODEL // LANES


def _params(*sem):
    return pltpu.CompilerParams(dimension_semantics=sem, vmem_limit_bytes=VMEM_LIMIT)


def _cond_row(row0):
    return jnp.where(row0 < N_PROMPT, 0, 1 + (row0 - N_PROMPT) // DEC_SEQ)


def _silu(x):
    return x * jax.nn.sigmoid(x)


def _layer_norm(r, g, b):
    mu = jnp.mean(r, axis=-1, keepdims=True)
    xc = r - mu
    var = jnp.mean(xc * xc, axis=-1, keepdims=True)
    return xc * lax.rsqrt(var + EPS) * g + b


def _dot(a, b):
    return jnp.dot(a, b, preferred_element_type=F32)


def _dot_nt(a, b):
    return lax.dot_general(a, b, (((1,), (1,)), ((), ())), preferred_element_type=F32)


def _dot_tn(a, b):
    return lax.dot_general(a, b, (((0,), (0,)), ((), ())), preferred_element_type=F32)


ADA_TN = 1536


def _ada_kernel(c_ref, w_ref, b_ref, o_ref):
    s = _silu(c_ref[...])
    o_ref[0] = _dot(s.astype(BF16), w_ref[0].astype(BF16)) + b_ref[0]


def _ada_modulation(cond, ada_w, ada_b):
    n = 6 * D_MODEL
    return pl.pallas_call(
        _ada_kernel,
        out_shape=jax.ShapeDtypeStruct((DEPTH, N_COND, n), F32),
        grid=(DEPTH, n // ADA_TN),
        in_specs=[
            pl.BlockSpec((N_COND, D_MODEL), lambda l, j: (0, 0)),
            pl.BlockSpec((1, D_MODEL, ADA_TN), lambda l, j: (l, 0, j)),
            pl.BlockSpec((1, 1, ADA_TN), lambda l, j: (l, 0, j)),
        ],
        out_specs=pl.BlockSpec((1, N_COND, ADA_TN), lambda l, j: (l, 0, j)),
        compiler_params=_params("arbitrary", "arbitrary"),
        name="ada_modulation",
    )(cond, ada_w, ada_b.reshape(DEPTH, 1, n))


PROJ_TM = 512


def _stream_specs(widths, tm):
    if len(widths) == 1:
        return [pl.BlockSpec((tm, widths[0]), lambda i: (i, 0))]
    n_ctx = N_PROMPT // tm
    return ([pl.BlockSpec((tm, widths[0]), lambda i: (jnp.minimum(i, n_ctx - 1), 0))]
            + [pl.BlockSpec((tm, w), lambda i: (jnp.maximum(i - n_ctx, 0), 0)) for w in widths[1:]])


def _stream_tile(refs, tm):
    if len(refs) == 1:
        return refs[0][...]
    latent = [r[...] for r in refs[1:]]
    latent = latent[0] if len(latent) == 1 else jnp.concatenate(latent, axis=1)
    return jnp.where(pl.program_id(0) * tm < N_PROMPT, refs[0][...], latent)


def _inproj_kernel(*refs, n_x):
    x_refs, (mod_ref, w_ref, o_ref) = refs[:n_x], refs[n_x:]
    m = mod_ref[0]
    h = _stream_tile(x_refs, o_ref.shape[0]) * (1.0 + m[1:2]) + m[0:1]
    o_ref[...] = _dot(h.astype(BF16), w_ref[...])


def _input_projection(x_parts, mod, w_bf16, tm=PROJ_TM):
    n = w_bf16.shape[1]
    return pl.pallas_call(
        functools.partial(_inproj_kernel, n_x=len(x_parts)),
        out_shape=jax.ShapeDtypeStruct((N_TOK, n), F32),
        grid=(N_TOK // tm,),
        in_specs=_stream_specs([a.shape[1] for a in x_parts], tm) + [
            pl.BlockSpec((1, 6, D_MODEL), lambda i: (_cond_row(i * tm), 0, 0)),
            pl.BlockSpec((D_MODEL, n), lambda i: (0, 0)),
        ],
        out_specs=pl.BlockSpec((tm, n), lambda i: (i, 0)),
        compiler_params=_params("arbitrary"),
        name="input_projection",
    )(*x_parts, mod, w_bf16)


_NAT_Q0, _NAT_K0, _NAT_V0 = 0, NAT_WIDTH, 2 * NAT_WIDTH
_DIFF_Q0 = 3 * NAT_WIDTH
_DIFF_K0 = _DIFF_Q0 + DIFF_WIDTH
_DIFF_V0 = _DIFF_K0 + DIFF_WIDTH
ATTN_SCALE = HEAD_DIM ** -0.5


def _diff_lambda(lam_ref, lam_init):
    lp = lam_ref[...]
    return (jnp.exp(jnp.sum(lp[0:1] * lp[1:2], axis=-1, keepdims=True))
            - jnp.exp(jnp.sum(lp[2:3] * lp[3:4], axis=-1, keepdims=True)) + lam_init)


def _softmax_parts(s):
    m = jnp.max(s, axis=-1, keepdims=True)
    e = jnp.exp(s - m)
    return e, jnp.sum(e, axis=-1, keepdims=True)


def _sub_norm(o, g, lam_init):
    ms = jnp.mean(o * o, axis=-1, keepdims=True)
    return o * lax.rsqrt(ms + EPS) * g * (1.0 - lam_init)


def _ctx_attn_kernel(z_ref, lam_ref, subg_ref, o_ref, *, lam_init):
    lam = _diff_lambda(lam_ref, lam_init)

    def softmax_av(q0, k0, v0, v_width):
        q = z_ref[:, q0:q0 + HEAD_DIM].astype(BF16)
        k = z_ref[:, k0:k0 + HEAD_DIM].astype(BF16)
        s = _dot_nt(q, k) * ATTN_SCALE
        yield
        e, l = _softmax_parts(s)
        yield
        av = _dot(e.astype(BF16), z_ref[:, v0:v0 + v_width].astype(BF16))
        yield
        return av / l

    chains = [softmax_av(_NAT_Q0 + h * HEAD_DIM, _NAT_K0 + h * HEAD_DIM, _NAT_V0 + h * HEAD_DIM, HEAD_DIM)
              for h in range(NAT_HEADS)]
    chains += [softmax_av(_DIFF_Q0 + i * HEAD_DIM, _DIFF_K0 + i * HEAD_DIM,
                          _DIFF_V0 + (i // 2) * 2 * HEAD_DIM, 2 * HEAD_DIM)
               for i in range(2 * DIFF_HEADS)]
    outs = _run_interleaved(chains)
    for h in range(NAT_HEADS):
        o_ref[:, h * HEAD_DIM:(h + 1) * HEAD_DIM] = outs[h].astype(o_ref.dtype)
    for h in range(DIFF_HEADS):
        c = NAT_WIDTH + h * 2 * HEAD_DIM
        o = outs[NAT_HEADS + 2 * h] - lam * outs[NAT_HEADS + 2 * h + 1]
        o_ref[:, c:c + 2 * HEAD_DIM] = _sub_norm(o, subg_ref[...], lam_init).astype(o_ref.dtype)


def _context_attention(z, lam_p, subln_g, lam_init):
    return pl.pallas_call(
        functools.partial(_ctx_attn_kernel, lam_init=lam_init),
        out_shape=jax.ShapeDtypeStruct((N_PROMPT, D_MODEL), BF16),
        grid=(BATCH,),
        in_specs=[
            pl.BlockSpec((SEQ, EVEN_IN), lambda b: (b, 0)),
            pl.BlockSpec((4, HEAD_DIM), lambda b: (0, 0)),
            pl.BlockSpec((1, 2 * HEAD_DIM), lambda b: (0, 0)),
        ],
        out_specs=pl.BlockSpec((SEQ, D_MODEL), lambda b: (b, 0)),
        compiler_params=_params("arbitrary"),
        name="context_attention",
    )(z, lam_p, subln_g.reshape(1, 2 * HEAD_DIM))


NAT_TQ = 256
NAT_KEYS = 12 * GRID_W
_SAMPLE_BLK0 = N_PROMPT // DEC_SEQ


def _nat_bias_table(rel_bias):
    rows = N_ROWS
    kh = min(NAT_WIN_H, rows)
    kw = NAT_WIN_W
    col = np.arange(GRID_W)
    c_start = np.clip(col - kw // 2, 0, GRID_W - kw)
    col_ok = (col[None, :] >= c_start[:, None]) & (col[None, :] < c_start[:, None] + kw)
    dc = np.clip(col[None, :] - col[:, None], 1 - kw, kw - 1) + (NAT_WIN_W - 1)
    onehot = ((dc[None] == np.arange(2 * kw - 1)[:, None, None]) & col_ok[None]).astype(np.float32)
    tiles = jnp.einsum('hrd,dqk->hrqk', rel_bias.astype(F32), onehot, precision=lax.Precision.HIGHEST)
    tiles = jnp.where(col_ok, tiles, -jnp.inf)
    outside = jnp.full((NAT_HEADS, GRID_W, GRID_W), -jnp.inf, F32)
    out_rows = []
    for r in range(rows):
        start = min(max(r - kh // 2, 0), rows - kh)
        blocks = [tiles[:, a - r + NAT_WIN_H - 1] if start <= a < start + kh else outside
                  for a in range(rows)]
        out_rows.append(jnp.concatenate(blocks, axis=-1))
    full = jnp.stack(out_rows, axis=1).reshape(NAT_HEADS, DEC_SEQ, DEC_SEQ)
    return jnp.concatenate(
        [full[:, qt * NAT_TQ:(qt + 1) * NAT_TQ, _nat_key0(qt):_nat_key0(qt) + NAT_KEYS]
         for qt in range(DEC_SEQ // NAT_TQ)], axis=1)


def _nat_key0(qt):
    half = DEC_SEQ // NAT_TQ // 2
    return (qt >= half) * (DEC_SEQ - NAT_KEYS)


def _nat_latent_kernel(q_ref, k_ref, v_ref, ck_ref, cv_ref, bias_ref, o_ref):
    key0 = pl.multiple_of(_nat_key0(pl.program_id(1)).astype(jnp.int32), DEC_SEQ - NAT_KEYS)
    keys = pl.ds(key0, NAT_KEYS)

    def head(j):
        sl = slice(j * HEAD_DIM, (j + 1) * HEAD_DIM)
        q = q_ref[:, sl].astype(BF16)
        s_loc = _dot_nt(q, k_ref[keys, sl].astype(BF16)) * ATTN_SCALE + bias_ref[j]
        s_ctx = _dot_nt(q, ck_ref[0, :, sl].astype(BF16)) * ATTN_SCALE
        yield
        m = jnp.maximum(jnp.max(s_loc, axis=-1, keepdims=True), jnp.max(s_ctx, axis=-1, keepdims=True))
        yield
        e_loc = jnp.exp(s_loc - m)
        e_ctx = jnp.exp(s_ctx - m)
        l = jnp.sum(e_loc, axis=-1, keepdims=True) + jnp.sum(e_ctx, axis=-1, keepdims=True)
        yield
        o = (_dot(e_loc.astype(BF16), v_ref[keys, sl].astype(BF16))
             + _dot(e_ctx.astype(BF16), cv_ref[0, :, sl].astype(BF16)))
        yield
        return (o / l).astype(o_ref.dtype)

    outs = _run_interleaved([head(j) for j in range(2)])
    for j in range(2):
        o_ref[:, j * HEAD_DIM:(j + 1) * HEAD_DIM] = outs[j]


def _nat_latent_attention(z, ck, cv, bias):
    nq = DEC_SEQ // NAT_TQ
    qblk0 = N_PROMPT // NAT_TQ
    pair = 2 * HEAD_DIM
    return pl.pallas_call(
        _nat_latent_kernel,
        out_shape=jax.ShapeDtypeStruct((N_SAMPLE, NAT_WIDTH), BF16),
        grid=(NAT_HEADS // 2, nq, DEC_BATCH),
        in_specs=[
            pl.BlockSpec((NAT_TQ, pair), lambda hp, qt, b: (qblk0 + b * nq + qt, _NAT_Q0 // pair + hp)),
            pl.BlockSpec((DEC_SEQ, pair), lambda hp, qt, b: (_SAMPLE_BLK0 + b, _NAT_K0 // pair + hp)),
            pl.BlockSpec((DEC_SEQ, pair), lambda hp, qt, b: (_SAMPLE_BLK0 + b, _NAT_V0 // pair + hp)),
            pl.BlockSpec((1, PAST_LEN, pair), lambda hp, qt, b: (b, 0, hp)),
            pl.BlockSpec((1, PAST_LEN, pair), lambda hp, qt, b: (b, 0, hp)),
            pl.BlockSpec((2, NAT_TQ, NAT_KEYS), lambda hp, qt, b: (hp, qt, 0)),
        ],
        out_specs=pl.BlockSpec((NAT_TQ, pair), lambda hp, qt, b: (b * nq + qt, hp)),
        compiler_params=_params("arbitrary", "arbitrary", "arbitrary"),
        name="nat_latent_attention",
    )(z, z, z, ck, cv, bias)


DIFF_TQ = 256


def _rope_tables():
    nf = HEAD_DIM // 4
    t = jnp.arange(DEC_SEQ)
    inv = ROPE_THETA ** (-jnp.arange(nf, dtype=F32) / nf)
    ang_r = (t // GRID_W).astype(F32)[:, None] * inv
    ang_c = (t % GRID_W).astype(F32)[:, None] * inv
    cos = jnp.concatenate([jnp.cos(ang_r)] * 2 + [jnp.cos(ang_c)] * 2, axis=-1)
    sin = jnp.concatenate([-jnp.sin(ang_r), jnp.sin(ang_r), -jnp.sin(ang_c), jnp.sin(ang_c)], axis=-1)
    return jnp.tile(cos, (1, 2)), jnp.tile(sin, (1, 2))


def _rope(x, cos, sin):
    nf = HEAD_DIM // 4
    lane = lax.broadcasted_iota(jnp.int32, x.shape, 1)
    upper = (lane // nf) % 2 == 1
    partner = jnp.where(upper, pltpu.roll(x, nf, axis=1), pltpu.roll(x, x.shape[1] - nf, axis=1))
    return x * cos + partner * sin


def _diff_latent_kernel(q_ref, k_ref, v_ref, ck_ref, cv_ref, cos_ref, sin_ref, lam_ref, subg_ref,
                        o_ref, kr_ref, *, lam_init):
    qt = pl.program_id(2)

    @pl.when(qt == 0)
    def _():
        kr_ref[...] = _rope(k_ref[...], cos_ref[...], sin_ref[...]).astype(BF16)

    lam = _diff_lambda(lam_ref, lam_init)
    row0 = pl.multiple_of(qt * DIFF_TQ, DIFF_TQ)
    q = _rope(q_ref[...], cos_ref[pl.ds(row0, DIFF_TQ), :], sin_ref[pl.ds(row0, DIFF_TQ), :]).astype(BF16)
    v_loc = v_ref[...].astype(BF16)
    v_ctx = cv_ref[0].astype(BF16)
    def softmax_av(j):
        sl = slice(j * HEAD_DIM, (j + 1) * HEAD_DIM)
        s_ctx = _dot_nt(q[:, sl], ck_ref[0, :, sl].astype(BF16)) * ATTN_SCALE
        s_loc = _dot_nt(q[:, sl], kr_ref[:, sl]) * ATTN_SCALE
        yield
        m = jnp.maximum(jnp.max(s_loc, axis=-1, keepdims=True), jnp.max(s_ctx, axis=-1, keepdims=True))
        yield
        e_loc = jnp.exp(s_loc - m)
        e_ctx = jnp.exp(s_ctx - m)
        l = jnp.sum(e_loc, axis=-1, keepdims=True) + jnp.sum(e_ctx, axis=-1, keepdims=True)
        yield
        av = _dot(e_loc.astype(BF16), v_loc) + _dot(e_ctx.astype(BF16), v_ctx)
        yield
        return av / l

    parts = _run_interleaved([softmax_av(j) for j in range(2)])
    o = parts[0] - lam * parts[1]
    o_ref[...] = _sub_norm(o, subg_ref[...], lam_init).astype(o_ref.dtype)


def _diff_latent_attention(z, ck, cv, lam_p, subln_g, lam_init):
    nq = DEC_SEQ // DIFF_TQ
    qblk0 = N_PROMPT // DIFF_TQ
    w = 2 * HEAD_DIM
    cos, sin = _rope_tables()
    return pl.pallas_call(
        functools.partial(_diff_latent_kernel, lam_init=lam_init),
        out_shape=jax.ShapeDtypeStruct((N_SAMPLE, DIFF_WIDTH), BF16),
        grid=(DEC_BATCH, DIFF_HEADS, nq),
        in_specs=[
            pl.BlockSpec((DIFF_TQ, w), lambda b, h, qt: (qblk0 + b * nq + qt, _DIFF_Q0 // w + h)),
            pl.BlockSpec((DEC_SEQ, w), lambda b, h, qt: (_SAMPLE_BLK0 + b, _DIFF_K0 // w + h)),
            pl.BlockSpec((DEC_SEQ, w), lambda b, h, qt: (_SAMPLE_BLK0 + b, _DIFF_V0 // w + h)),
            pl.BlockSpec((1, PAST_LEN, w), lambda b, h, qt: (b, 0, h)),
            pl.BlockSpec((1, PAST_LEN, w), lambda b, h, qt: (b, 0, h)),
            pl.BlockSpec((DEC_SEQ, w), lambda b, h, qt: (0, 0)),
            pl.BlockSpec((DEC_SEQ, w), lambda b, h, qt: (0, 0)),
            pl.BlockSpec((4, HEAD_DIM), lambda b, h, qt: (0, 0)),
            pl.BlockSpec((1, w), lambda b, h, qt: (0, 0)),
        ],
        out_specs=pl.BlockSpec((DIFF_TQ, w), lambda b, h, qt: (b * nq + qt, h)),
        scratch_shapes=[pltpu.VMEM((DEC_SEQ, w), BF16)],
        compiler_params=_params("arbitrary", "arbitrary", "arbitrary"),
        name="diff_latent_attention",
    )(z, z, z, ck, cv, cos, sin, lam_p, subln_g.reshape(1, w))


OUT_TM = 256


def _route(logits_t, rb_col):
    sc = jax.nn.sigmoid(logits_t)
    bi = sc + rb_col
    srow = [sc[e:e + 1] for e in range(N_EXPERTS)]
    brow = [bi[e:e + 1] for e in range(N_EXPERTS)]
    n = EXPERTS_PER_GROUP
    gscore = []
    for g in range(N_GROUPS):
        v = brow[g * n:(g + 1) * n]
        best = None
        for i in range(n):
            for j in range(i + 1, n):
                s = v[i] + v[j]
                best = s if best is None else jnp.maximum(best, s)
        gscore.append(best)
    sel = jnp.zeros_like(gscore[0], dtype=jnp.int32)
    best = gscore[0]
    for g in range(1, N_GROUPS):
        better = gscore[g] > best
        sel = jnp.where(better, g, sel)
        best = jnp.where(better, gscore[g], best)

    def pick_group(rows, i):
        out = rows[(N_GROUPS - 1) * n + i]
        for g in range(N_GROUPS - 2, -1, -1):
            out = jnp.where(sel == g, rows[g * n + i], out)
        return out

    bv = [pick_group(brow, i) for i in range(n)]
    sv = [pick_group(srow, i) for i in range(n)]
    i1 = jnp.zeros_like(sel)
    m1, w1 = bv[0], sv[0]
    for i in range(1, n):
        better = bv[i] > m1
        i1 = jnp.where(better, i, i1)
        m1 = jnp.where(better, bv[i], m1)
        w1 = jnp.where(better, sv[i], w1)
    i2 = jnp.zeros_like(sel)
    m2 = jnp.full_like(m1, -jnp.inf)
    w2 = jnp.zeros_like(w1)
    for i in range(n):
        better = (i1 != i) & (bv[i] > m2)
        i2 = jnp.where(better, i, i2)
        m2 = jnp.where(better, bv[i], m2)
        w2 = jnp.where(better, sv[i], w2)
    tot = w1 + w2
    return sel * n + i1, sel * n + i2, w1 / tot, w2 / tot


def _store_token_tiles(ref, value):
    for c in range(N_CHUNKS):
        ref[:, c, :] = value[:, c * LANES:(c + 1) * LANES]


def _load_token_tiles(ref):
    return jnp.concatenate([ref[:, c, :] for c in range(N_CHUNKS)], axis=1)


def _outproj_kernel(*refs, n_o, n_x):
    o_refs, x_refs, refs = refs[:n_o], refs[n_o:n_o + n_x], refs[n_o + n_x:]
    mod_ref, w_ref, lng_ref, lnb_ref, rw_ref, rb_ref, x1_ref, h2_ref, ri_ref, rw_out_ref, cnt_ref = refs
    tm = x1_ref.shape[0]
    m = mod_ref[0]
    y = _dot(_stream_tile(o_refs, tm), w_ref[...])
    x1 = _layer_norm(ALPHA * _stream_tile(x_refs, tm) + m[2:3] * y, lng_ref[...], lnb_ref[...])
    x1_ref[...] = x1
    h2 = x1 * (1.0 + m[4:5]) + m[3:4]
    _store_token_tiles(h2_ref, h2)
    rw = rw_ref[...]
    h_hi, rw_hi = h2.astype(BF16), rw.astype(BF16)
    h_lo = (h2 - h_hi.astype(F32)).astype(BF16)
    rw_lo = (rw - rw_hi.astype(F32)).astype(BF16)
    logits = _dot(h_hi, rw_hi) + (_dot(h_lo, rw_hi) + _dot(h_hi, rw_lo))
    e1, e2, w1, w2 = _route(logits.T[:N_EXPERTS], rb_ref[...])
    eid = lax.broadcasted_iota(jnp.int32, (N_EXPERTS, tm), 0)
    onehot = (eid == e1) | (eid == e2)
    ti = lax.broadcasted_iota(jnp.int32, (tm, tm), 0)
    tj = lax.broadcasted_iota(jnp.int32, (tm, tm), 1)
    earlier = jnp.where(ti < tj, 1.0, 0.0).astype(BF16)
    rank = _dot(jnp.where(onehot, 1.0, 0.0).astype(BF16), earlier)
    r1 = jnp.sum(jnp.where(eid == e1, rank, 0.0), axis=0, keepdims=True).astype(jnp.int32)
    r2 = jnp.sum(jnp.where(eid == e2, rank, 0.0), axis=0, keepdims=True).astype(jnp.int32)
    ri_ref[...] = jnp.concatenate([e1, e2, r1, r2, jnp.zeros((4, tm), jnp.int32)], axis=0)
    wt = jnp.concatenate([w1, w2, jnp.zeros((LANES - 2, tm), F32)], axis=0)
    rw_out_ref[...] = wt.T
    cnt = jnp.sum(jnp.where(onehot, 1.0, 0.0), axis=1, keepdims=True)
    cnt_ref[0] = jnp.broadcast_to(cnt, (N_EXPERTS, LANES)).astype(jnp.int32)


def _output_projection(o_parts, x_parts, mod, w_bf16, ln_g, ln_b, router_w_pad, router_b, tm=OUT_TM):
    nt = N_TOK // tm
    return pl.pallas_call(
        functools.partial(_outproj_kernel, n_o=len(o_parts), n_x=len(x_parts)),
        out_shape=(jax.ShapeDtypeStruct((N_TOK, D_MODEL), F32),
                   jax.ShapeDtypeStruct((N_TOK, N_CHUNKS, LANES), F32),
                   jax.ShapeDtypeStruct((8, N_TOK), jnp.int32),
                   jax.ShapeDtypeStruct((N_TOK, LANES), F32),
                   jax.ShapeDtypeStruct((nt, N_EXPERTS, LANES), jnp.int32)),
        grid=(nt,),
        in_specs=_stream_specs([a.shape[1] for a in o_parts], tm) + _stream_specs([a.shape[1] for a in x_parts], tm) + [
            pl.BlockSpec((1, 6, D_MODEL), lambda i: (_cond_row(i * tm), 0, 0)),
            pl.BlockSpec((D_MODEL, D_MODEL), lambda i: (0, 0)),
            pl.BlockSpec((1, D_MODEL), lambda i: (0, 0)),
            pl.BlockSpec((1, D_MODEL), lambda i: (0, 0)),
            pl.BlockSpec((D_MODEL, LANES), lambda i: (0, 0)),
            pl.BlockSpec((N_EXPERTS, 1), lambda i: (0, 0)),
        ],
        out_specs=(pl.BlockSpec((tm, D_MODEL), lambda i: (i, 0)),
                   pl.BlockSpec((tm, N_CHUNKS, LANES), lambda i: (i, 0, 0)),
                   pl.BlockSpec((8, tm), lambda i: (0, i)),
                   pl.BlockSpec((tm, LANES), lambda i: (i, 0)),
                   pl.BlockSpec((1, N_EXPERTS, LANES), lambda i: (i, 0, 0))),
        compiler_params=_params("arbitrary"),
        name="output_projection",
    )(*o_parts, *x_parts, mod, w_bf16, ln_g.reshape(1, D_MODEL), ln_b.reshape(1, D_MODEL),
      router_w_pad, router_b.reshape(N_EXPERTS, 1))


N_ASSIGN = 2 * N_TOK
MOE_TM = 256
MOE_ROWS = N_ASSIGN + N_EXPERTS * MOE_TM
MOE_TILES = MOE_ROWS // MOE_TM
DISPATCH_TM = 512
COMBINE_TM = 256


def _moe_plan(route_i, counts):
    cnt = counts[:, :, 0]
    total = jnp.sum(cnt, axis=0)
    padded = (total + MOE_TM - 1) // MOE_TM * MOE_TM
    seg_end = jnp.cumsum(padded)
    seg_start = seg_end - padded
    tile_base = seg_start[None, :] + jnp.cumsum(cnt, axis=0) - cnt
    base_tok = jnp.repeat(tile_base, OUT_TM, axis=0)
    eids = jnp.arange(N_EXPERTS, dtype=jnp.int32)[None, :]
    pos = [jnp.sum(jnp.where(route_i[k][:, None] == eids, base_tok, 0), axis=1) + route_i[2 + k]
           for k in range(2)]
    pos = jnp.concatenate(pos).astype(jnp.int32)
    tile_row0 = jnp.arange(MOE_TILES, dtype=jnp.int32) * MOE_TM
    tile_expert = jnp.minimum(jnp.sum(seg_end[None, :] <= tile_row0[:, None], axis=1), N_EXPERTS - 1)
    n_tiles = (seg_end[-1] // MOE_TM).reshape(1)
    last = MOE_ROWS - MOE_TM
    fill_rows = jnp.concatenate([jnp.minimum(seg_start + total, last),
                                 last - jnp.arange(N_EXPERTS, dtype=jnp.int32) * MOE_TM])
    return pos, tile_expert.astype(jnp.int32), n_tiles.astype(jnp.int32), fill_rows.astype(jnp.int32)


def _row_copy_wait(src_hbm, dst, sem, n_rows):
    pltpu.make_async_copy(src_hbm.at[pl.ds(0, n_rows)], dst.at[pl.ds(0, n_rows)], sem).wait()


def _dispatch_kernel(pos_ref, fill_ref, h_ref, xs_ref, zero_ref, sem, fill_sem):
    tm = h_ref.shape[0]
    t0 = pl.program_id(0) * tm

    @pl.when(pl.program_id(0) == 0)
    def _():
        zero_ref[...] = jnp.zeros_like(zero_ref)
        fills = [pltpu.make_async_copy(zero_ref, xs_ref.at[pl.ds(fill_ref[j], MOE_TM)], fill_sem)
                 for j in range(2 * N_EXPERTS)]
        for cp in fills[N_EXPERTS:]:
            cp.start()
        for cp in fills[N_EXPERTS:]:
            cp.wait()
        for cp in fills[:N_EXPERTS]:
            cp.start()
            cp.wait()

    def body(r, carry):
        for k in range(2):
            pltpu.make_async_copy(h_ref.at[r], xs_ref.at[pos_ref[k * N_TOK + t0 + r]], sem).start()
        return carry

    lax.fori_loop(0, tm, body, 0, unroll=8)
    for _ in range(2):
        _row_copy_wait(h_ref, xs_ref, sem, tm)


def _moe_dispatch(pos, fill_rows, h2_tiles):
    tm = DISPATCH_TM
    return pl.pallas_call(
        _dispatch_kernel,
        out_shape=jax.ShapeDtypeStruct((MOE_ROWS, N_CHUNKS, LANES), F32),
        grid_spec=pltpu.PrefetchScalarGridSpec(
            num_scalar_prefetch=2,
            grid=(N_TOK // tm,),
            in_specs=[pl.BlockSpec((tm, N_CHUNKS, LANES), lambda i, pos, fill: (i, 0, 0))],
            out_specs=pl.BlockSpec(memory_space=pl.ANY),
            scratch_shapes=[pltpu.VMEM((MOE_TM, N_CHUNKS, LANES), F32), pltpu.SemaphoreType.DMA,
                            pltpu.SemaphoreType.DMA],
        ),
        compiler_params=_params("arbitrary"),
        name="moe_dispatch",
    )(pos, fill_rows, h2_tiles)


def _expert_kernel(te_ref, nt_ref, xs_ref, wg_ref, wu_ref, wd_ref, ys_ref, wg16_ref, wu16_ref, wd16_ref):
    i = pl.program_id(0)

    @pl.when(i < nt_ref[0])
    def _():
        @pl.when((i == 0) | (te_ref[i] != te_ref[jnp.maximum(i - 1, 0)]))
        def _():
            wg16_ref[...] = wg_ref[0, 0].astype(BF16)
            wu16_ref[...] = wu_ref[0, 0].astype(BF16)
            wd16_ref[...] = wd_ref[0, 0].astype(BF16)

        x = _load_token_tiles(xs_ref).astype(BF16)
        he = _silu(_dot(x, wg16_ref[...])) * _dot(x, wu16_ref[...])
        _store_token_tiles(ys_ref, _dot(he.astype(BF16), wd16_ref[...]))

    @pl.when(i >= nt_ref[0])
    def _():
        ys_ref[...] = jnp.zeros_like(ys_ref)


def _moe_experts(tile_expert, n_tiles, xs, layer, wg, wu, wd):
    row_blk = lambda i, te, nt: (jnp.minimum(i, nt[0] - 1), 0, 0)
    w_blk = lambda i, te, nt: (layer, te[i], 0, 0)
    return pl.pallas_call(
        _expert_kernel,
        out_shape=jax.ShapeDtypeStruct((MOE_ROWS, N_CHUNKS, LANES), F32),
        grid_spec=pltpu.PrefetchScalarGridSpec(
            num_scalar_prefetch=2,
            grid=(MOE_TILES,),
            in_specs=[
                pl.BlockSpec((MOE_TM, N_CHUNKS, LANES), row_blk),
                pl.BlockSpec((1, 1, D_MODEL, EXPERT_FF), w_blk),
                pl.BlockSpec((1, 1, D_MODEL, EXPERT_FF), w_blk),
                pl.BlockSpec((1, 1, EXPERT_FF, D_MODEL), w_blk),
            ],
            out_specs=pl.BlockSpec((MOE_TM, N_CHUNKS, LANES), lambda i, te, nt: (i, 0, 0)),
            scratch_shapes=[pltpu.VMEM((D_MODEL, EXPERT_FF), BF16), pltpu.VMEM((D_MODEL, EXPERT_FF), BF16),
                            pltpu.VMEM((EXPERT_FF, D_MODEL), BF16)],
        ),
        compiler_params=_params("arbitrary"),
        name="moe_experts",
    )(tile_expert, n_tiles, xs, wg, wu, wd)


def _combine_kernel(pos_ref, gw_ref, x1_ref, mod_ref, lng_ref, lnb_ref, ys_ref, *rest, split):
    out_refs, (buf_ref, sem) = rest[:-2], rest[-2:]
    i = pl.program_id(0)
    n = pl.num_programs(0)
    tm = COMBINE_TM

    def issue(tile, slot):
        def body(r, carry):
            for k in range(2):
                p = pos_ref[k * N_TOK + tile * tm + r]
                pltpu.make_async_copy(ys_ref.at[p], buf_ref.at[slot, k, r], sem.at[slot]).start()
            return carry
        lax.fori_loop(0, tm, body, 0, unroll=8)

    @pl.when(i == 0)
    def _():
        issue(0, 0)

    @pl.when(i + 1 < n)
    def _():
        issue(i + 1, (i + 1) % 2)

    slot = i % 2
    for k in range(2):
        _row_copy_wait(ys_ref, buf_ref.at[slot, k], sem.at[slot], tm)
    gw = gw_ref[...]
    f = gw[:, 0:1] * _load_token_tiles(buf_ref.at[slot, 0]) + gw[:, 1:2] * _load_token_tiles(buf_ref.at[slot, 1])
    m = mod_ref[0]
    out = _layer_norm(ALPHA * x1_ref[...] + m[5:6] * f, lng_ref[...], lnb_ref[...])
    if split:
        prompt_ref, sample_ref = out_refs

        @pl.when(i < N_PROMPT // tm)
        def _():
            prompt_ref[...] = out

        @pl.when(i >= N_PROMPT // tm)
        def _():
            sample_ref[...] = out
    else:
        out_refs[0][...] = out


def _moe_combine(pos, gate_w, x1, mod, ln_g, ln_b, ys, split):
    tm = COMBINE_TM
    np_blk = N_PROMPT // tm
    if split:
        out_shape = (jax.ShapeDtypeStruct((N_PROMPT, D_MODEL), F32), jax.ShapeDtypeStruct((N_SAMPLE, D_MODEL), F32))
        out_specs = (pl.BlockSpec((tm, D_MODEL), lambda i, pos: (jnp.minimum(i, np_blk - 1), 0)),
                     pl.BlockSpec((tm, D_MODEL), lambda i, pos: (jnp.maximum(i - np_blk, 0), 0)))
    else:
        out_shape = jax.ShapeDtypeStruct((N_TOK, D_MODEL), F32)
        out_specs = pl.BlockSpec((tm, D_MODEL), lambda i, pos: (i, 0))
    return pl.pallas_call(
        functools.partial(_combine_kernel, split=split),
        out_shape=out_shape,
        grid_spec=pltpu.PrefetchScalarGridSpec(
            num_scalar_prefetch=1,
            grid=(N_TOK // tm,),
            in_specs=[
                pl.BlockSpec((tm, LANES), lambda i, pos: (i, 0)),
                pl.BlockSpec((tm, D_MODEL), lambda i, pos: (i, 0)),
                pl.BlockSpec((1, 6, D_MODEL), lambda i, pos: (_cond_row(i * tm), 0, 0)),
                pl.BlockSpec((1, D_MODEL), lambda i, pos: (0, 0)),
                pl.BlockSpec((1, D_MODEL), lambda i, pos: (0, 0)),
                pl.BlockSpec(memory_space=pl.ANY),
            ],
            out_specs=out_specs,
            scratch_shapes=[pltpu.VMEM((2, 2, tm, N_CHUNKS, LANES), F32), pltpu.SemaphoreType.DMA((2,))],
        ),
        compiler_params=_params("arbitrary"),
        name="moe_combine",
    )(pos, gate_w, x1, mod, ln_g.reshape(1, D_MODEL), ln_b.reshape(1, D_MODEL), ys)


def _moe_ffn(h2_tiles, route_i, gate_w, counts, layer, wg, wu, wd, x1, mod, ln_g, ln_b, split=False):
    pos, tile_expert, n_tiles, fill_rows = _moe_plan(route_i, counts)
    xs = _moe_dispatch(pos, fill_rows, h2_tiles)
    ys = _moe_experts(tile_expert, n_tiles, xs, layer, wg, wu, wd)
    return _moe_combine(pos, gate_w, x1, mod, ln_g, ln_b, ys, split)


DN_GATE_COL0 = DN_CONV_CH
DN_AB_COL0 = DN_CONV_CH + DN_WIDTH
DN_IN = DN_AB_COL0 + LANES
N_DIRS = 2


def _dn_prep_kernel(q_ref, k_ref, v_ref, wq_ref, wk_ref, wv_ref, qo_ref, ko_ref, vo_ref):
    length = q_ref.shape[0]
    row = lax.broadcasted_iota(jnp.int32, (length, DN_DK), 0)

    def conv(x, w):
        prev = jnp.where(row == 0, 0.0, pltpu.roll(x, 1, axis=0))
        nxt = jnp.where(row == length - 1, 0.0, pltpu.roll(x, length - 1, axis=0))
        return _silu(prev * w[0:1] + x * w[1:2] + nxt * w[2:3])

    def l2n(x):
        return x * lax.rsqrt(jnp.sum(x * x, axis=-1, keepdims=True) + 1e-6)

    qo_ref[0] = (l2n(conv(q_ref[...], wq_ref[...])) * (DN_DK ** -0.5)).astype(qo_ref.dtype)
    ko_ref[0] = l2n(conv(k_ref[...], wk_ref[...])).astype(ko_ref.dtype)
    vo_ref[0] = conv(v_ref[...], wv_ref[...]).astype(vo_ref.dtype)


def _dn_prep(z1, conv_w, nb, length, row_blk0):
    hq, hk, hv = 0, DN_HEADS, 2 * DN_HEADS
    shp = jax.ShapeDtypeStruct((DN_HEADS, nb * length, DN_DK), BF16)
    blk = lambda off: pl.BlockSpec((length, DN_DK), lambda b, h: (row_blk0 + b, off + h))
    wblk = lambda off: pl.BlockSpec((3, DN_DK), lambda b, h: (0, off + h))
    oblk = pl.BlockSpec((1, length, DN_DK), lambda b, h: (h, b, 0))
    return pl.pallas_call(
        _dn_prep_kernel,
        out_shape=(shp, shp, shp),
        grid=(nb, DN_HEADS),
        in_specs=[blk(hq), blk(hk), blk(hv), wblk(hq), wblk(hk), wblk(hv)],
        out_specs=(oblk, oblk, oblk),
        compiler_params=_params("arbitrary", "arbitrary"),
        name="deltanet_prep",
    )(z1, z1, z1, conv_w, conv_w, conv_w)


def _dn_gates_kernel(ab_ref, alog_ref, dtb_ref, o_ref):
    length = ab_ref.shape[0]
    x = ab_ref[...]
    a = x + dtb_ref[...]
    softplus = jnp.maximum(a, 0.0) + jnp.log1p(jnp.exp(-jnp.abs(a)))
    g = -jnp.exp(alog_ref[...]) * softplus
    beta = jax.nn.sigmoid(x)
    ri = lax.broadcasted_iota(jnp.int32, (DN_CHUNK, DN_CHUNK), 0)
    ci = lax.broadcasted_iota(jnp.int32, (DN_CHUNK, DN_CHUNK), 1)
    tril = (ri >= ci).astype(F32)
    triu = (ri <= ci).astype(F32)
    lane = lax.broadcasted_iota(jnp.int32, (DN_CHUNK, LANES), 1)
    for c in range(length // DN_CHUNK):
        rows = slice(c * DN_CHUNK, (c + 1) * DN_CHUNK)
        gch = g[rows]
        pre = jnp.dot(tril, gch, precision=lax.Precision.HIGHEST, preferred_element_type=F32)
        suf = jnp.dot(triu, gch, precision=lax.Precision.HIGHEST, preferred_element_type=F32)
        gc = jnp.where(lane < DN_HEADS, pre, suf)
        o_ref[rows, :] = jnp.where(lane < N_DIRS * DN_HEADS, gc, beta[rows])


def _dn_gates(z1, a_log, dt_bias, nb, length, row_blk0):
    pad = LANES - N_DIRS * DN_HEADS
    alog = jnp.pad(a_log.reshape(1, -1).astype(F32), ((0, 0), (0, pad)))
    dtb = jnp.pad(dt_bias.reshape(1, -1).astype(F32), ((0, 0), (0, pad)))
    return pl.pallas_call(
        _dn_gates_kernel,
        out_shape=jax.ShapeDtypeStruct((nb * length, LANES), F32),
        grid=(nb,),
        in_specs=[
            pl.BlockSpec((length, LANES), lambda b: (row_blk0 + b, DN_AB_COL0 // LANES)),
            pl.BlockSpec((1, LANES), lambda b: (0, 0)),
            pl.BlockSpec((1, LANES), lambda b: (0, 0)),
        ],
        out_specs=pl.BlockSpec((length, LANES), lambda b: (b, 0)),
        compiler_params=_params("arbitrary"),
        name="deltanet_gates",
    )(z1, alog, dtb)


def _delta_chunk(q, k, v, beta, gcol, grow, s, lower):
    ri = lax.broadcasted_iota(jnp.int32, (DN_CHUNK, DN_CHUNK), 0)
    ci = lax.broadcasted_iota(jnp.int32, (DN_CHUNK, DN_CHUNK), 1)
    incl = (ri >= ci) if lower else (ri <= ci)
    strict = (ri > ci) if lower else (ri < ci)
    decay = jnp.exp(jnp.where(incl, gcol - grow, -jnp.inf))
    kb = k * beta
    kq = _dot_nt(jnp.concatenate([kb, q], axis=0).astype(BF16), k.astype(BF16))
    kk, a_qk = kq[:DN_CHUNK], kq[DN_CHUNK:] * decay
    yield
    tri_l = jnp.where(strict, kk * decay, 0.0)

    def off_block(size):
        same = (ri // (2 * size)) == (ci // (2 * size))
        rpar, cpar = (ri // size) % 2, (ci // size) % 2
        return same & ((rpar == 1) & (cpar == 0) if lower else (rpar == 0) & (cpar == 1))

    p = jnp.where(ri == ci, 1.0, 0.0) - jnp.where(off_block(1), tri_l, 0.0)
    size = 2
    while size < DN_CHUNK:
        p16 = p.astype(BF16)
        pc = _dot(p16, jnp.where(off_block(size), tri_l, 0.0).astype(BF16))
        yield
        p = p - _dot(pc.astype(BF16), p16)
        yield
        size *= 2
    eg = jnp.exp(gcol)
    uw = _dot(p.astype(BF16), jnp.concatenate([v * beta, kb * eg], axis=1).astype(BF16))
    yield
    u, w = uw[:, :DN_DV], uw[:, DN_DV:]
    ws = _dot(jnp.concatenate([w, q * eg], axis=0).astype(BF16), s.astype(BF16))
    v_new, o_inter = u - ws[:DN_CHUNK], ws[DN_CHUNK:]
    yield
    v_new16 = v_new.astype(BF16)
    o = o_inter + _dot(a_qk.astype(BF16), v_new16)
    g_last = gcol[DN_CHUNK - 1:DN_CHUNK] if lower else gcol[0:1]
    kd = k * jnp.exp(g_last - gcol)
    s_new = s * jnp.exp(g_last) + _dot_tn(kd.astype(BF16), v_new16)
    return o, s_new


def _run_interleaved(chains):
    results = [None] * len(chains)
    active = list(enumerate(chains))
    while active:
        still = []
        for idx, gen in active:
            try:
                next(gen)
                still.append((idx, gen))
            except StopIteration as stop:
                results[idx] = stop.value
        active = still
    return results


def _dn_scan_kernel(*refs, has_init, want_final):
    (qf_ref, kf_ref, vf_ref, gf_ref, qb_ref, kb_ref, vb_ref, gb_ref), rest = refs[:8], refs[8:]
    if has_init:
        s0_ref, rest = rest[0], rest[1:]
    of_ref, ob_ref = rest[0], rest[1]
    rest = rest[2:]
    if want_final:
        sf_ref, rest = rest[0], rest[1:]
    s_ref = rest[0]
    c = pl.program_id(1)
    nc = pl.num_programs(1)

    @pl.when(c == 0)
    def _():
        for d in range(N_DIRS):
            for h in range(DN_HEADS):
                if has_init:
                    s_ref[d * DN_HEADS + h] = s0_ref[0, d, h]
                else:
                    s_ref[d * DN_HEADS + h] = jnp.zeros((DN_DK, DN_DV), F32)

    dirs = ((qf_ref, kf_ref, vf_ref, gf_ref, of_ref, True), (qb_ref, kb_ref, vb_ref, gb_ref, ob_ref, False))
    chains = []
    for d, (q_ref, k_ref, v_ref, g_ref, o_ref, lower) in enumerate(dirs):
        gates = g_ref[...]
        gates_t = gates.T
        for h in range(DN_HEADS):
            j = d * DN_HEADS + h
            jb = N_DIRS * DN_HEADS + j
            chains.append(_delta_chunk(q_ref[h].astype(F32), k_ref[h].astype(F32), v_ref[h].astype(F32),
                                       gates[:, jb:jb + 1], gates[:, j:j + 1], gates_t[j:j + 1, :],
                                       s_ref[j], lower))
    results = _run_interleaved(chains)
    for d, (_, _, _, _, o_ref, _) in enumerate(dirs):
        for h in range(DN_HEADS):
            j = d * DN_HEADS + h
            o, s_new = results[j]
            o_ref[:, h * DN_DV:(h + 1) * DN_DV] = o.astype(o_ref.dtype)
            s_ref[j] = s_new

    if want_final:
        @pl.when(c == nc - 1)
        def _():
            for d in range(N_DIRS):
                for h in range(DN_HEADS):
                    sf_ref[0, d, h] = s_ref[d * DN_HEADS + h]


def _dn_scan(qn, kn, vn, gates, s0, nb, length, want_final):
    nc = length // DN_CHUNK
    has_init = s0 is not None
    fwd = lambda b, c: (0, b * nc + c, 0)
    bwd = lambda b, c: (0, b * nc + (nc - 1 - c), 0)
    hm = lambda im: pl.BlockSpec((DN_HEADS, DN_CHUNK, DN_DK), im)
    in_specs = [hm(fwd), hm(fwd), hm(fwd), pl.BlockSpec((DN_CHUNK, LANES), lambda b, c: (b * nc + c, 0)),
                hm(bwd), hm(bwd), hm(bwd), pl.BlockSpec((DN_CHUNK, LANES), lambda b, c: (b * nc + (nc - 1 - c), 0))]
    args = [qn, kn, vn, gates, qn, kn, vn, gates]
    state_blk = pl.BlockSpec((1, N_DIRS, DN_HEADS, DN_DK, DN_DV), lambda b, c: (b, 0, 0, 0, 0))
    if has_init:
        in_specs.append(state_blk)
        args.append(s0)
    o_shape = jax.ShapeDtypeStruct((nb * length, DN_WIDTH), BF16)
    out_shape = [o_shape, o_shape]
    out_specs = [pl.BlockSpec((DN_CHUNK, DN_WIDTH), lambda b, c: (b * nc + c, 0)),
                 pl.BlockSpec((DN_CHUNK, DN_WIDTH), lambda b, c: (b * nc + (nc - 1 - c), 0))]
    if want_final:
        out_shape.append(jax.ShapeDtypeStruct((nb, N_DIRS, DN_HEADS, DN_DK, DN_DV), F32))
        out_specs.append(state_blk)
    return pl.pallas_call(
        functools.partial(_dn_scan_kernel, has_init=has_init, want_final=want_final),
        out_shape=tuple(out_shape),
        grid=(nb, nc),
        in_specs=in_specs,
        out_specs=tuple(out_specs),
        scratch_shapes=[pltpu.VMEM((N_DIRS * DN_HEADS, DN_DK, DN_DV), F32)],
        compiler_params=_params("arbitrary", "arbitrary"),
        name="deltanet_scan",
    )(*args)


DN_MERGE_TM = 256


def _dn_merge_kernel(of_ref, ob_ref, gate_ref, g_ref, o_ref):
    for h in range(DN_HEADS):
        sl = slice(h * DN_DV, (h + 1) * DN_DV)
        o = of_ref[:, sl].astype(F32) + ob_ref[:, sl].astype(F32)
        o = o * lax.rsqrt(jnp.mean(o * o, axis=-1, keepdims=True) + EPS) * g_ref[...]
        o_ref[:, sl] = (o * _silu(gate_ref[:, sl])).astype(o_ref.dtype)


def _dn_merge(o_f, o_b, z1, onorm_g, n_rows, row0):
    tm = DN_MERGE_TM
    blk = pl.BlockSpec((tm, DN_WIDTH), lambda i: (i, 0))
    return pl.pallas_call(
        _dn_merge_kernel,
        out_shape=jax.ShapeDtypeStruct((n_rows, DN_WIDTH), BF16),
        grid=(n_rows // tm,),
        in_specs=[blk, blk,
                  pl.BlockSpec((tm, DN_WIDTH), lambda i: (row0 // tm + i, DN_GATE_COL0 // DN_WIDTH)),
                  pl.BlockSpec((1, DN_DV), lambda i: (0, 0))],
        out_specs=blk,
        compiler_params=_params("arbitrary"),
        name="deltanet_merge",
    )(o_f, o_b, z1, onorm_g.reshape(1, DN_DV))


def _deltanet_stream(z1, conv_w, a_log, dt_bias, onorm_g, s0, nb, length, row0, want_final):
    row_blk0 = row0 // length
    qn, kn, vn = _dn_prep(z1, conv_w, nb, length, row_blk0)
    gates = _dn_gates(z1, a_log, dt_bias, nb, length, row_blk0)
    outs = _dn_scan(qn, kn, vn, gates, s0, nb, length, want_final)
    merged = _dn_merge(outs[0], outs[1], z1, onorm_g, nb * length, row0)
    return merged, (outs[2] if want_final else None)


def kernel(x_prompt, x_sample, cache_nat_k, cache_nat_v, cache_diff_k, cache_diff_v, state_delta, c, c_ctx, w_in_even, w_out_even, nat_bias, diff_lam, diff_subln, w_in_odd, conv_odd, a_log_odd, dt_bias_odd, onorm_odd, w_out_odd, ada_w, ada_b, ln_g, ln_b, router_w, router_b, moe_wg, moe_wu, moe_wd):
    x_parts = (x_prompt.reshape(N_PROMPT, D_MODEL), x_sample.reshape(N_SAMPLE, D_MODEL))
    cond =jnp.concatenate([c_ctx[None], c, jnp.zeros((N_COND - 1 - DEC_BATCH, D_MODEL), F32)], axis=0)
    mod = _ada_modulation(cond, ada_w, ada_b).reshape(DEPTH, N_COND, 6, D_MODEL)
    router_w_pad = jnp.pad(router_w, ((0, 0), (0, LANES - N_EXPERTS)))

    lam_init = 0.8 - 0.6 * math.exp(-0.3 * 0)
    z = _input_projection(x_parts, mod[0], w_in_even[0].astype(BF16))
    new_nat_k = z[:N_PROMPT, _NAT_K0:_NAT_V0].reshape(BATCH, 1, SEQ, NAT_HEADS, HEAD_DIM)
    new_nat_v = z[:N_PROMPT, _NAT_V0:_DIFF_Q0].reshape(BATCH, 1, SEQ, NAT_HEADS, HEAD_DIM)
    new_diff_k = z[:N_PROMPT, _DIFF_K0:_DIFF_V0].reshape(BATCH, 1, SEQ, DIFF_HEADS, 2, HEAD_DIM)
    new_diff_v = z[:N_PROMPT, _DIFF_V0:].reshape(BATCH, 1, SEQ, DIFF_HEADS, 2 * HEAD_DIM)
    o_ctx = _context_attention(z, diff_lam[0], diff_subln[0], lam_init)
    o_nat = _nat_latent_attention(z, cache_nat_k[:, 0].reshape(DEC_BATCH, PAST_LEN, NAT_WIDTH),
                                  cache_nat_v[:, 0].reshape(DEC_BATCH, PAST_LEN, NAT_WIDTH),
                                  _nat_bias_table(nat_bias[0]))
    o_diff = _diff_latent_attention(z, cache_diff_k[:, 0].reshape(DEC_BATCH, PAST_LEN, DIFF_WIDTH),
                                    cache_diff_v[:, 0].reshape(DEC_BATCH, PAST_LEN, DIFF_WIDTH),
                                    diff_lam[0], diff_subln[0], lam_init)
    x1, *routed = _output_projection((o_ctx, o_nat, o_diff), x_parts, mod[0], w_out_even[0].astype(BF16),
                                     ln_g[0, 0], ln_b[0, 0], router_w_pad, router_b)
    x = _moe_ffn(*routed, 0, moe_wg, moe_wu, moe_wd, x1, mod[0], ln_g[0, 1], ln_b[0, 1])

    w1 = w_in_odd[0]
    w1 = jnp.concatenate([w1[:, :DN_CONV_CH], w1[:, DN_CONV_CH + 4 * DN_HEADS:],
                          w1[:, DN_CONV_CH:DN_CONV_CH + 4 * DN_HEADS],
                          jnp.zeros((D_MODEL, LANES - 4 * DN_HEADS), F32)], axis=1).astype(BF16)
    z1 = _input_projection((x,), mod[1], w1, tm=256)
    o_p, new_state = _deltanet_stream(z1, conv_odd[0], a_log_odd[0], dt_bias_odd[0], onorm_odd[0],
                                      None, BATCH, SEQ, 0, True)
    o_s, _ = _deltanet_stream(z1, conv_odd[0], a_log_odd[0], dt_bias_odd[0], onorm_odd[0],
                              state_delta[:, 0], DEC_BATCH, DEC_SEQ, N_PROMPT, False)
    x1, *routed = _output_projection((o_p, o_s), (x,), mod[1], w_out_odd[0].astype(BF16), ln_g[1, 0], ln_b[1, 0],
                                     router_w_pad, router_b)
    y_prompt, y_sample = _moe_ffn(*routed, 1, moe_wg, moe_wu, moe_wd, x1, mod[1], ln_g[1, 1], ln_b[1, 1],
                                  split=True)
    y_prompt = y_prompt.reshape(BATCH, SEQ, D_MODEL)
    y_sample = y_sample.reshape(DEC_BATCH, DEC_SEQ, D_MODEL)
    return (y_prompt, y_sample, new_nat_k, new_nat_v, new_diff_k, new_diff_v, new_state[:, None])
```

```python
import functools
import math

import jax
import jax.numpy as jnp
import numpy as np
from jax import lax
from jax.experimental import pallas as pl
from jax.experimental.pallas import tpu as pltpu

F32 = jnp.float32
BF16 = jnp.bfloat16

D_MODEL = 1024
BATCH = 16
SEQ = 256
DEC_BATCH = 8
DEC_SEQ = 1024
PAST_LEN = 256
GRID_W = 64
N_ROWS = DEC_SEQ // GRID_W

HEAD_DIM = 64
NAT_HEADS = 8
NAT_WIN_H = 8
NAT_WIN_W = 16
DIFF_HEADS = 4
NAT_WIDTH = NAT_HEADS * HEAD_DIM
DIFF_WIDTH = DIFF_HEADS * 2 * HEAD_DIM
EVEN_IN = 3 * NAT_WIDTH + 3 * DIFF_WIDTH
ROPE_THETA = 10000.0

DN_HEADS = 8
DN_DK = 128
DN_DV = 128
DN_QK_WIDTH = DN_HEADS * DN_DK
DN_WIDTH = DN_HEADS * DN_DV
DN_CONV_CH = 2 * DN_QK_WIDTH + DN_WIDTH
DN_CHUNK = 64

N_EXPERTS = 16
N_GROUPS = 4
EXPERTS_PER_GROUP = N_EXPERTS // N_GROUPS
EXPERT_FF = 512

DEPTH = 2
ALPHA = (2 * DEPTH) ** 0.25
EPS = 1e-5

N_PROMPT = BATCH * SEQ
N_SAMPLE = DEC_BATCH * DEC_SEQ
N_TOK = N_PROMPT + N_SAMPLE
N_COND = 16

VMEM_LIMIT = 56 * 1024 * 1024
LANES = 128
N_CHUNKS = D_MODEL // LANES


def _params(*sem):
    return pltpu.CompilerParams(dimension_semantics=sem, vmem_limit_bytes=VMEM_LIMIT)


def _cond_row(row0):
    return jnp.where(row0 < N_PROMPT, 0, 1 + (row0 - N_PROMPT) // DEC_SEQ)


def _silu(x):
    return x * jax.nn.sigmoid(x)


def _layer_norm(r, g, b):
    mu = jnp.mean(r, axis=-1, keepdims=True)
    xc = r - mu
    var = jnp.mean(xc * xc, axis=-1, keepdims=True)
    return xc * lax.rsqrt(var + EPS) * g + b


def _dot(a, b):
    return jnp.dot(a, b, preferred_element_type=F32)


def _dot_nt(a, b):
    return lax.dot_general(a, b, (((1,), (1,)), ((), ())), preferred_element_type=F32)


def _dot_tn(a, b):
    return lax.dot_general(a, b, (((0,), (0,)), ((), ())), preferred_element_type=F32)


ADA_TN = 1536


def _ada_kernel(c_ref, w_ref, b_ref, o_ref):
    s = _silu(c_ref[...])
    o_ref[0] = _dot(s.astype(BF16), w_ref[0].astype(BF16)) + b_ref[0]


def _ada_modulation(cond, ada_w, ada_b):
    n = 6 * D_MODEL
    return pl.pallas_call(
        _ada_kernel,
        out_shape=jax.ShapeDtypeStruct((DEPTH, N_COND, n), F32),
        grid=(DEPTH, n // ADA_TN),
        in_specs=[
            pl.BlockSpec((N_COND, D_MODEL), lambda l, j: (0, 0)),
            pl.BlockSpec((1, D_MODEL, ADA_TN), lambda l, j: (l, 0, j)),
            pl.BlockSpec((1, 1, ADA_TN), lambda l, j: (l, 0, j)),
        ],
        out_specs=pl.BlockSpec((1, N_COND, ADA_TN), lambda l, j: (l, 0, j)),
        compiler_params=_params("arbitrary", "arbitrary"),
        name="ada_modulation",
    )(cond, ada_w, ada_b.reshape(DEPTH, 1, n))


PROJ_TM = 512


def _stream_specs(widths, tm):
    if len(widths) == 1:
        return [pl.BlockSpec((tm, widths[0]), lambda i: (i, 0))]
    n_ctx = N_PROMPT // tm
    return ([pl.BlockSpec((tm, widths[0]), lambda i: (jnp.minimum(i, n_ctx - 1), 0))]
            + [pl.BlockSpec((tm, w), lambda i: (jnp.maximum(i - n_ctx, 0), 0)) for w in widths[1:]])


def _stream_tile(refs, tm):
    if len(refs) == 1:
        return refs[0][...]
    latent = [r[...] for r in refs[1:]]
    latent = latent[0] if len(latent) == 1 else jnp.concatenate(latent, axis=1)
    return jnp.where(pl.program_id(0) * tm < N_PROMPT, refs[0][...], latent)


def _inproj_kernel(*refs, n_x):
    x_refs, (mod_ref, w_ref, o_ref) = refs[:n_x], refs[n_x:]
    m = mod_ref[0]
    h = _stream_tile(x_refs, o_ref.shape[0]) * (1.0 + m[1:2]) + m[0:1]
    o_ref[...] = _dot(h.astype(BF16), w_ref[...])


def _input_projection(x_parts, mod, w_bf16, tm=PROJ_TM):
    n = w_bf16.shape[1]
    return pl.pallas_call(
        functools.partial(_inproj_kernel, n_x=len(x_parts)),
        out_shape=jax.ShapeDtypeStruct((N_TOK, n), F32),
        grid=(N_TOK // tm,),
        in_specs=_stream_specs([a.shape[1] for a in x_parts], tm) + [
            pl.BlockSpec((1, 6, D_MODEL), lambda i: (_cond_row(i * tm), 0, 0)),
            pl.BlockSpec((D_MODEL, n), lambda i: (0, 0)),
        ],
        out_specs=pl.BlockSpec((tm, n), lambda i: (i, 0)),
        compiler_params=_params("arbitrary"),
        name="input_projection",
    )(*x_parts, mod, w_bf16)


_NAT_Q0, _NAT_K0, _NAT_V0 = 0, NAT_WIDTH, 2 * NAT_WIDTH
_DIFF_Q0 = 3 * NAT_WIDTH
_DIFF_K0 = _DIFF_Q0 + DIFF_WIDTH
_DIFF_V0 = _DIFF_K0 + DIFF_WIDTH
ATTN_SCALE = HEAD_DIM ** -0.5


def _diff_lambda(lam_ref, lam_init):
    lp = lam_ref[...]
    return (jnp.exp(jnp.sum(lp[0:1] * lp[1:2], axis=-1, keepdims=True))
            - jnp.exp(jnp.sum(lp[2:3] * lp[3:4], axis=-1, keepdims=True)) + lam_init)


def _softmax_parts(s):
    m = jnp.max(s, axis=-1, keepdims=True)
    e = jnp.exp(s - m)
    return e, jnp.sum(e, axis=-1, keepdims=True)


def _sub_norm(o, g, lam_init):
    ms = jnp.mean(o * o, axis=-1, keepdims=True)
    return o * lax.rsqrt(ms + EPS) * g * (1.0 - lam_init)


def _ctx_attn_kernel(z_ref, lam_ref, subg_ref, o_ref, *, lam_init):
    lam = _diff_lambda(lam_ref, lam_init)

    def softmax_av(q0, k0, v0, v_width):
        q = z_ref[:, q0:q0 + HEAD_DIM].astype(BF16)
        k = z_ref[:, k0:k0 + HEAD_DIM].astype(BF16)
        s = _dot_nt(q, k) * ATTN_SCALE
        yield
        e, l = _softmax_parts(s)
        yield
        av = _dot(e.astype(BF16), z_ref[:, v0:v0 + v_width].astype(BF16))
        yield
        return av / l

    chains = [softmax_av(_NAT_Q0 + h * HEAD_DIM, _NAT_K0 + h * HEAD_DIM, _NAT_V0 + h * HEAD_DIM, HEAD_DIM)
              for h in range(NAT_HEADS)]
    chains += [softmax_av(_DIFF_Q0 + i * HEAD_DIM, _DIFF_K0 + i * HEAD_DIM,
                          _DIFF_V0 + (i // 2) * 2 * HEAD_DIM, 2 * HEAD_DIM)
               for i in range(2 * DIFF_HEADS)]
    outs = _run_interleaved(chains)
    for h in range(NAT_HEADS):
        o_ref[:, h * HEAD_DIM:(h + 1) * HEAD_DIM] = outs[h].astype(o_ref.dtype)
    for h in range(DIFF_HEADS):
        c = NAT_WIDTH + h * 2 * HEAD_DIM
        o = outs[NAT_HEADS + 2 * h] - lam * outs[NAT_HEADS + 2 * h + 1]
        o_ref[:, c:c + 2 * HEAD_DIM] = _sub_norm(o, subg_ref[...], lam_init).astype(o_ref.dtype)


def _context_attention(z, lam_p, subln_g, lam_init):
    return pl.pallas_call(
        functools.partial(_ctx_attn_kernel, lam_init=lam_init),
        out_shape=jax.ShapeDtypeStruct((N_PROMPT, D_MODEL), BF16),
        grid=(BATCH,),
        in_specs=[
            pl.BlockSpec((SEQ, EVEN_IN), lambda b: (b, 0)),
            pl.BlockSpec((4, HEAD_DIM), lambda b: (0, 0)),
            pl.BlockSpec((1, 2 * HEAD_DIM), lambda b: (0, 0)),
        ],
        out_specs=pl.BlockSpec((SEQ, D_MODEL), lambda b: (b, 0)),
        compiler_params=_params("arbitrary"),
        name="context_attention",
    )(z, lam_p, subln_g.reshape(1, 2 * HEAD_DIM))


NAT_TQ = 256
NAT_KEYS = 12 * GRID_W
_SAMPLE_BLK0 = N_PROMPT // DEC_SEQ


def _nat_bias_table(rel_bias):
    rows = N_ROWS
    kh = min(NAT_WIN_H, rows)
    kw = NAT_WIN_W
    col = np.arange(GRID_W)
    c_start = np.clip(col - kw // 2, 0, GRID_W - kw)
    col_ok = (col[None, :] >= c_start[:, None]) & (col[None, :] < c_start[:, None] + kw)
    dc = np.clip(col[None, :] - col[:, None], 1 - kw, kw - 1) + (NAT_WIN_W - 1)
    col_onehot = ((dc[None] == np.arange(2 * kw - 1)[:, None, None]) & col_ok[None]).astype(np.float32)
    span = NAT_KEYS // GRID_W
    row_onehot = np.zeros((2 * NAT_WIN_H - 1, rows, span), np.float32)
    for r in range(rows):
        start = min(max(r - kh // 2, 0), rows - kh)
        a0 = _nat_key0(r * GRID_W // NAT_TQ) // GRID_W
        assert a0 <= start and start + kh <= a0 + span
        for a in range(start, start + kh):
            row_onehot[a - r + NAT_WIN_H - 1, r, a - a0] = 1.0
    hi = lax.Precision.HIGHEST
    by_row = jnp.einsum('hed,era->hrad', rel_bias.astype(F32), row_onehot, precision=hi)
    table = jnp.einsum('hrad,dqk->hrqak', by_row, col_onehot, precision=hi)
    inside = (row_onehot.sum(0) > 0)[:, None, :, None] & col_ok[None, :, None, :]
    return jnp.where(inside, table, -jnp.inf).reshape(NAT_HEADS, DEC_SEQ, NAT_KEYS)


def _nat_key0(qt):
    half = DEC_SEQ // NAT_TQ // 2
    return (qt >= half) * (DEC_SEQ - NAT_KEYS)


def _nat_latent_kernel(q_ref, k_ref, v_ref, ck_ref, cv_ref, bias_ref, o_ref):
    key0 = pl.multiple_of(_nat_key0(pl.program_id(1)).astype(jnp.int32), DEC_SEQ - NAT_KEYS)
    keys = pl.ds(key0, NAT_KEYS)

    def head(j):
        sl = slice(j * HEAD_DIM, (j + 1) * HEAD_DIM)
        q = q_ref[:, sl].astype(BF16)
        s_loc = _dot_nt(q, k_ref[keys, sl].astype(BF16)) * ATTN_SCALE + bias_ref[j]
        s_ctx = _dot_nt(q, ck_ref[0, :, sl].astype(BF16)) * ATTN_SCALE
        yield
        m = jnp.maximum(jnp.max(s_loc, axis=-1, keepdims=True), jnp.max(s_ctx, axis=-1, keepdims=True))
        yield
        e_loc = jnp.exp(s_loc - m)
        e_ctx = jnp.exp(s_ctx - m)
        l = jnp.sum(e_loc, axis=-1, keepdims=True) + jnp.sum(e_ctx, axis=-1, keepdims=True)
        yield
        o = (_dot(e_loc.astype(BF16), v_ref[keys, sl].astype(BF16))
             + _dot(e_ctx.astype(BF16), cv_ref[0, :, sl].astype(BF16)))
        yield
        return (o / l).astype(o_ref.dtype)

    outs = _run_interleaved([head(j) for j in range(2)])
    for j in range(2):
        o_ref[:, j * HEAD_DIM:(j + 1) * HEAD_DIM] = outs[j]


def _nat_latent_attention(z, ck, cv, bias):
    nq = DEC_SEQ // NAT_TQ
    qblk0 = N_PROMPT // NAT_TQ
    pair = 2 * HEAD_DIM
    return pl.pallas_call(
        _nat_latent_kernel,
        out_shape=jax.ShapeDtypeStruct((N_SAMPLE, NAT_WIDTH), BF16),
        grid=(NAT_HEADS // 2, nq, DEC_BATCH),
        in_specs=[
            pl.BlockSpec((NAT_TQ, pair), lambda hp, qt, b: (qblk0 + b * nq + qt, _NAT_Q0 // pair + hp)),
            pl.BlockSpec((DEC_SEQ, pair), lambda hp, qt, b: (_SAMPLE_BLK0 + b, _NAT_K0 // pair + hp)),
            pl.BlockSpec((DEC_SEQ, pair), lambda hp, qt, b: (_SAMPLE_BLK0 + b, _NAT_V0 // pair + hp)),
            pl.BlockSpec((1, PAST_LEN, pair), lambda hp, qt, b: (b, 0, hp)),
            pl.BlockSpec((1, PAST_LEN, pair), lambda hp, qt, b: (b, 0, hp)),
            pl.BlockSpec((2, NAT_TQ, NAT_KEYS), lambda hp, qt, b: (hp, qt, 0)),
        ],
        out_specs=pl.BlockSpec((NAT_TQ, pair), lambda hp, qt, b: (b * nq + qt, hp)),
        compiler_params=_params("arbitrary", "arbitrary", "arbitrary"),
        name="nat_latent_attention",
    )(z, z, z, ck, cv, bias)


DIFF_TQ = 256


def _rope_tables():
    nf = HEAD_DIM // 4
    t = jnp.arange(DEC_SEQ)
    inv = ROPE_THETA ** (-jnp.arange(nf, dtype=F32) / nf)
    ang_r = (t // GRID_W).astype(F32)[:, None] * inv
    ang_c = (t % GRID_W).astype(F32)[:, None] * inv
    cos = jnp.concatenate([jnp.cos(ang_r)] * 2 + [jnp.cos(ang_c)] * 2, axis=-1)
    sin = jnp.concatenate([-jnp.sin(ang_r), jnp.sin(ang_r), -jnp.sin(ang_c), jnp.sin(ang_c)], axis=-1)
    return jnp.tile(cos, (1, 2)), jnp.tile(sin, (1, 2))


def _rope(x, cos, sin):
    nf = HEAD_DIM // 4
    lane = lax.broadcasted_iota(jnp.int32, x.shape, 1)
    upper = (lane // nf) % 2 == 1
    partner = jnp.where(upper, pltpu.roll(x, nf, axis=1), pltpu.roll(x, x.shape[1] - nf, axis=1))
    return x * cos + partner * sin


def _diff_latent_kernel(q_ref, k_ref, v_ref, ck_ref, cv_ref, cos_ref, sin_ref, lam_ref, subg_ref,
                        o_ref, kr_ref, *, lam_init):
    qt = pl.program_id(2)

    @pl.when(qt == 0)
    def _():
        kr_ref[...] = _rope(k_ref[...], cos_ref[...], sin_ref[...]).astype(BF16)

    lam = _diff_lambda(lam_ref, lam_init)
    row0 = pl.multiple_of(qt * DIFF_TQ, DIFF_TQ)
    q = _rope(q_ref[...], cos_ref[pl.ds(row0, DIFF_TQ), :], sin_ref[pl.ds(row0, DIFF_TQ), :]).astype(BF16)
    v_loc = v_ref[...].astype(BF16)
    v_ctx = cv_ref[0].astype(BF16)
    def softmax_av(j):
        sl = slice(j * HEAD_DIM, (j + 1) * HEAD_DIM)
        s_ctx = _dot_nt(q[:, sl], ck_ref[0, :, sl].astype(BF16)) * ATTN_SCALE
        s_loc = _dot_nt(q[:, sl], kr_ref[:, sl]) * ATTN_SCALE
        yield
        m = jnp.maximum(jnp.max(s_loc, axis=-1, keepdims=True), jnp.max(s_ctx, axis=-1, keepdims=True))
        yield
        e_loc = jnp.exp(s_loc - m)
        e_ctx = jnp.exp(s_ctx - m)
        l = jnp.sum(e_loc, axis=-1, keepdims=True) + jnp.sum(e_ctx, axis=-1, keepdims=True)
        yield
        av = _dot(e_loc.astype(BF16), v_loc) + _dot(e_ctx.astype(BF16), v_ctx)
        yield
        return av / l

    parts = _run_interleaved([softmax_av(j) for j in range(2)])
    o = parts[0] - lam * parts[1]
    o_ref[...] = _sub_norm(o, subg_ref[...], lam_init).astype(o_ref.dtype)


def _diff_latent_attention(z, ck, cv, lam_p, subln_g, lam_init):
    nq = DEC_SEQ // DIFF_TQ
    qblk0 = N_PROMPT // DIFF_TQ
    w = 2 * HEAD_DIM
    cos, sin = _rope_tables()
    return pl.pallas_call(
        functools.partial(_diff_latent_kernel, lam_init=lam_init),
        out_shape=jax.ShapeDtypeStruct((N_SAMPLE, DIFF_WIDTH), BF16),
        grid=(DEC_BATCH, DIFF_HEADS, nq),
        in_specs=[
            pl.BlockSpec((DIFF_TQ, w), lambda b, h, qt: (qblk0 + b * nq + qt, _DIFF_Q0 // w + h)),
            pl.BlockSpec((DEC_SEQ, w), lambda b, h, qt: (_SAMPLE_BLK0 + b, _DIFF_K0 // w + h)),
            pl.BlockSpec((DEC_SEQ, w), lambda b, h, qt: (_SAMPLE_BLK0 + b, _DIFF_V0 // w + h)),
            pl.BlockSpec((1, PAST_LEN, w), lambda b, h, qt: (b, 0, h)),
            pl.BlockSpec((1, PAST_LEN, w), lambda b, h, qt: (b, 0, h)),
            pl.BlockSpec((DEC_SEQ, w), lambda b, h, qt: (0, 0)),
            pl.BlockSpec((DEC_SEQ, w), lambda b, h, qt: (0, 0)),
            pl.BlockSpec((4, HEAD_DIM), lambda b, h, qt: (0, 0)),
            pl.BlockSpec((1, w), lambda b, h, qt: (0, 0)),
        ],
        out_specs=pl.BlockSpec((DIFF_TQ, w), lambda b, h, qt: (b * nq + qt, h)),
        scratch_shapes=[pltpu.VMEM((DEC_SEQ, w), BF16)],
        compiler_params=_params("arbitrary", "arbitrary", "arbitrary"),
        name="diff_latent_attention",
    )(z, z, z, ck, cv, cos, sin, lam_p, subln_g.reshape(1, w))


OUT_TM = 256


def _route(logits_t, rb_col):
    sc = jax.nn.sigmoid(logits_t)
    bi = sc + rb_col
    srow = [sc[e:e + 1] for e in range(N_EXPERTS)]
    brow = [bi[e:e + 1] for e in range(N_EXPERTS)]
    n = EXPERTS_PER_GROUP
    gscore = []
    for g in range(N_GROUPS):
        v = brow[g * n:(g + 1) * n]
        best = None
        for i in range(n):
            for j in range(i + 1, n):
                s = v[i] + v[j]
                best = s if best is None else jnp.maximum(best, s)
        gscore.append(best)
    sel = jnp.zeros_like(gscore[0], dtype=jnp.int32)
    best = gscore[0]
    for g in range(1, N_GROUPS):
        better = gscore[g] > best
        sel = jnp.where(better, g, sel)
        best = jnp.where(better, gscore[g], best)

    def pick_group(rows, i):
        out = rows[(N_GROUPS - 1) * n + i]
        for g in range(N_GROUPS - 2, -1, -1):
            out = jnp.where(sel == g, rows[g * n + i], out)
        return out

    bv = [pick_group(brow, i) for i in range(n)]
    sv = [pick_group(srow, i) for i in range(n)]
    i1 = jnp.zeros_like(sel)
    m1, w1 = bv[0], sv[0]
    for i in range(1, n):
        better = bv[i] > m1
        i1 = jnp.where(better, i, i1)
        m1 = jnp.where(better, bv[i], m1)
        w1 = jnp.where(better, sv[i], w1)
    i2 = jnp.zeros_like(sel)
    m2 = jnp.full_like(m1, -jnp.inf)
    w2 = jnp.zeros_like(w1)
    for i in range(n):
        better = (i1 != i) & (bv[i] > m2)
        i2 = jnp.where(better, i, i2)
        m2 = jnp.where(better, bv[i], m2)
        w2 = jnp.where(better, sv[i], w2)
    tot = w1 + w2
    return sel * n + i1, sel * n + i2, w1 / tot, w2 / tot


def _store_token_tiles(ref, value):
    for c in range(N_CHUNKS):
        ref[:, c, :] = value[:, c * LANES:(c + 1) * LANES]


def _load_token_tiles(ref):
    return jnp.concatenate([ref[:, c, :] for c in range(N_CHUNKS)], axis=1)


def _outproj_kernel(*refs, n_o, n_x):
    o_refs, x_refs, refs = refs[:n_o], refs[n_o:n_o + n_x], refs[n_o + n_x:]
    mod_ref, w_ref, lng_ref, lnb_ref, rw_ref, rb_ref, x1_ref, h2_ref, ri_ref, rw_out_ref, cnt_ref = refs
    tm = x1_ref.shape[0]
    m = mod_ref[0]
    y = _dot(_stream_tile(o_refs, tm), w_ref[...])
    x1 = _layer_norm(ALPHA * _stream_tile(x_refs, tm) + m[2:3] * y, lng_ref[...], lnb_ref[...])
    x1_ref[...] = x1
    h2 = x1 * (1.0 + m[4:5]) + m[3:4]
    _store_token_tiles(h2_ref, h2)
    rw = rw_ref[...]
    h_hi, rw_hi = h2.astype(BF16), rw.astype(BF16)
    h_lo = (h2 - h_hi.astype(F32)).astype(BF16)
    rw_lo = (rw - rw_hi.astype(F32)).astype(BF16)
    logits = _dot(h_hi, rw_hi) + (_dot(h_lo, rw_hi) + _dot(h_hi, rw_lo))
    e1, e2, w1, w2 = _route(logits.T[:N_EXPERTS], rb_ref[...])
    eid = lax.broadcasted_iota(jnp.int32, (N_EXPERTS, tm), 0)
    onehot = (eid == e1) | (eid == e2)
    ti = lax.broadcasted_iota(jnp.int32, (tm, tm), 0)
    tj = lax.broadcasted_iota(jnp.int32, (tm, tm), 1)
    earlier = jnp.where(ti < tj, 1.0, 0.0).astype(BF16)
    rank = _dot(jnp.where(onehot, 1.0, 0.0).astype(BF16), earlier)
    r1 = jnp.sum(jnp.where(eid == e1, rank, 0.0), axis=0, keepdims=True).astype(jnp.int32)
    r2 = jnp.sum(jnp.where(eid == e2, rank, 0.0), axis=0, keepdims=True).astype(jnp.int32)
    ri_ref[...] = jnp.concatenate([e1, e2, r1, r2, jnp.zeros((4, tm), jnp.int32)], axis=0)
    wt = jnp.concatenate([w1, w2, jnp.zeros((LANES - 2, tm), F32)], axis=0)
    rw_out_ref[...] = wt.T
    cnt = jnp.sum(jnp.where(onehot, 1.0, 0.0), axis=1, keepdims=True)
    cnt_ref[0] = jnp.broadcast_to(cnt, (N_EXPERTS, LANES)).astype(jnp.int32)


def _output_projection(o_parts, x_parts, mod, w_bf16, ln_g, ln_b, router_w_pad, router_b, tm=OUT_TM):
    nt = N_TOK // tm
    return pl.pallas_call(
        functools.partial(_outproj_kernel, n_o=len(o_parts), n_x=len(x_parts)),
        out_shape=(jax.ShapeDtypeStruct((N_TOK, D_MODEL), F32),
                   jax.ShapeDtypeStruct((N_TOK, N_CHUNKS, LANES), F32),
                   jax.ShapeDtypeStruct((8, N_TOK), jnp.int32),
                   jax.ShapeDtypeStruct((N_TOK, LANES), F32),
                   jax.ShapeDtypeStruct((nt, N_EXPERTS, LANES), jnp.int32)),
        grid=(nt,),
        in_specs=_stream_specs([a.shape[1] for a in o_parts], tm) + _stream_specs([a.shape[1] for a in x_parts], tm) + [
            pl.BlockSpec((1, 6, D_MODEL), lambda i: (_cond_row(i * tm), 0, 0)),
            pl.BlockSpec((D_MODEL, D_MODEL), lambda i: (0, 0)),
            pl.BlockSpec((1, D_MODEL), lambda i: (0, 0)),
            pl.BlockSpec((1, D_MODEL), lambda i: (0, 0)),
            pl.BlockSpec((D_MODEL, LANES), lambda i: (0, 0)),
            pl.BlockSpec((N_EXPERTS, 1), lambda i: (0, 0)),
        ],
        out_specs=(pl.BlockSpec((tm, D_MODEL), lambda i: (i, 0)),
                   pl.BlockSpec((tm, N_CHUNKS, LANES), lambda i: (i, 0, 0)),
                   pl.BlockSpec((8, tm), lambda i: (0, i)),
                   pl.BlockSpec((tm, LANES), lambda i: (i, 0)),
                   pl.BlockSpec((1, N_EXPERTS, LANES), lambda i: (i, 0, 0))),
        compiler_params=_params("arbitrary"),
        name="output_projection",
    )(*o_parts, *x_parts, mod, w_bf16, ln_g.reshape(1, D_MODEL), ln_b.reshape(1, D_MODEL),
      router_w_pad, router_b.reshape(N_EXPERTS, 1))


N_ASSIGN = 2 * N_TOK
MOE_TM = 256
MOE_ROWS = N_ASSIGN + N_EXPERTS * MOE_TM
MOE_TILES = MOE_ROWS // MOE_TM
DISPATCH_TM = 512
COMBINE_TM = 256
COMBINE_GROUPS = 8


def _moe_plan(route_i, counts):
    cnt = counts[:, :, 0]
    total = jnp.sum(cnt, axis=0)
    padded = (total + MOE_TM - 1) // MOE_TM * MOE_TM
    seg_end = jnp.cumsum(padded)
    seg_start = seg_end - padded
    tile_base = seg_start[None, :] + jnp.cumsum(cnt, axis=0) - cnt
    base_tok = jnp.repeat(tile_base, OUT_TM, axis=0)
    eids = jnp.arange(N_EXPERTS, dtype=jnp.int32)[None, :]
    pos = [jnp.sum(jnp.where(route_i[k][:, None] == eids, base_tok, 0), axis=1) + route_i[2 + k]
           for k in range(2)]
    pos = jnp.concatenate(pos).astype(jnp.int32)
    tile_row0 = jnp.arange(MOE_TILES, dtype=jnp.int32) * MOE_TM
    tile_expert = jnp.minimum(jnp.sum(seg_end[None, :] <= tile_row0[:, None], axis=1), N_EXPERTS - 1)
    n_tiles = (seg_end[-1] // MOE_TM).reshape(1)
    last = MOE_ROWS - MOE_TM
    fill_rows = jnp.concatenate([jnp.minimum(seg_start + total, last),
                                 last - jnp.arange(N_EXPERTS, dtype=jnp.int32) * MOE_TM])
    return pos, tile_expert.astype(jnp.int32), n_tiles.astype(jnp.int32), fill_rows.astype(jnp.int32)


def _row_copy_wait(src_hbm, dst, sem, n_rows):
    pltpu.make_async_copy(src_hbm.at[pl.ds(0, n_rows)], dst.at[pl.ds(0, n_rows)], sem).wait()


def _dispatch_kernel(pos_ref, fill_ref, h_ref, xs_ref, zero_ref, sem, fill_sem):
    tm = h_ref.shape[0]
    t0 = pl.program_id(0) * tm

    @pl.when(pl.program_id(0) == 0)
    def _():
        zero_ref[...] = jnp.zeros_like(zero_ref)
        fills = [pltpu.make_async_copy(zero_ref, xs_ref.at[pl.ds(fill_ref[j], MOE_TM)], fill_sem)
                 for j in range(2 * N_EXPERTS)]
        for cp in fills[N_EXPERTS:]:
            cp.start()
        for cp in fills[N_EXPERTS:]:
            cp.wait()
        for cp in fills[:N_EXPERTS]:
            cp.start()
            cp.wait()

    def body(r, carry):
        for k in range(2):
            pltpu.make_async_copy(h_ref.at[r], xs_ref.at[pos_ref[k * N_TOK + t0 + r]], sem).start()
        return carry

    lax.fori_loop(0, tm, body, 0, unroll=8)
    for _ in range(2):
        _row_copy_wait(h_ref, xs_ref, sem, tm)


def _moe_dispatch(pos, fill_rows, h2_tiles):
    tm = DISPATCH_TM
    return pl.pallas_call(
        _dispatch_kernel,
        out_shape=jax.ShapeDtypeStruct((MOE_ROWS, N_CHUNKS, LANES), F32),
        grid_spec=pltpu.PrefetchScalarGridSpec(
            num_scalar_prefetch=2,
            grid=(N_TOK // tm,),
            in_specs=[pl.BlockSpec((tm, N_CHUNKS, LANES), lambda i, pos, fill: (i, 0, 0))],
            out_specs=pl.BlockSpec(memory_space=pl.ANY),
            scratch_shapes=[pltpu.VMEM((MOE_TM, N_CHUNKS, LANES), F32), pltpu.SemaphoreType.DMA,
                            pltpu.SemaphoreType.DMA],
        ),
        compiler_params=_params("arbitrary"),
        name="moe_dispatch",
    )(pos, fill_rows, h2_tiles)


def _expert_kernel(te_ref, nt_ref, xs_ref, wg_ref, wu_ref, wd_ref, ys_ref, wg16_ref, wu16_ref, wd16_ref):
    i = pl.program_id(0)

    @pl.when(i < nt_ref[0])
    def _():
        @pl.when((i == 0) | (te_ref[i] != te_ref[jnp.maximum(i - 1, 0)]))
        def _():
            wg16_ref[...] = wg_ref[0, 0].astype(BF16)
            wu16_ref[...] = wu_ref[0, 0].astype(BF16)
            wd16_ref[...] = wd_ref[0, 0].astype(BF16)

        x = _load_token_tiles(xs_ref).astype(BF16)
        he = _silu(_dot(x, wg16_ref[...])) * _dot(x, wu16_ref[...])
        _store_token_tiles(ys_ref, _dot(he.astype(BF16), wd16_ref[...]))

    @pl.when(i >= nt_ref[0])
    def _():
        ys_ref[...] = jnp.zeros_like(ys_ref)


def _moe_experts(tile_expert, n_tiles, xs, layer, wg, wu, wd):
    row_blk = lambda i, te, nt: (jnp.minimum(i, nt[0] - 1), 0, 0)
    w_blk = lambda i, te, nt: (layer, te[i], 0, 0)
    return pl.pallas_call(
        _expert_kernel,
        out_shape=jax.ShapeDtypeStruct((MOE_ROWS, N_CHUNKS, LANES), F32),
        grid_spec=pltpu.PrefetchScalarGridSpec(
            num_scalar_prefetch=2,
            grid=(MOE_TILES,),
            in_specs=[
                pl.BlockSpec((MOE_TM, N_CHUNKS, LANES), row_blk),
                pl.BlockSpec((1, 1, D_MODEL, EXPERT_FF), w_blk),
                pl.BlockSpec((1, 1, D_MODEL, EXPERT_FF), w_blk),
                pl.BlockSpec((1, 1, EXPERT_FF, D_MODEL), w_blk),
            ],
            out_specs=pl.BlockSpec((MOE_TM, N_CHUNKS, LANES), lambda i, te, nt: (i, 0, 0)),
            scratch_shapes=[pltpu.VMEM((D_MODEL, EXPERT_FF), BF16), pltpu.VMEM((D_MODEL, EXPERT_FF), BF16),
                            pltpu.VMEM((EXPERT_FF, D_MODEL), BF16)],
        ),
        compiler_params=_params("arbitrary"),
        name="moe_experts",
    )(tile_expert, n_tiles, xs, wg, wu, wd)


def _combine_kernel(pos_ref, gw_ref, x1_ref, mod_ref, lng_ref, lnb_ref, ys_ref, *rest, split):
    out_refs, (buf_ref, sem) = rest[:-2], rest[-2:]
    i = pl.program_id(0)
    n = pl.num_programs(0)
    tm = COMBINE_TM

    def start_row(tile, to_slot, r):
        for k in range(2):
            p = pos_ref[k * N_TOK + tile * tm + r]
            pltpu.make_async_copy(ys_ref.at[p], buf_ref.at[to_slot, k, r], sem.at[to_slot]).start()

    def wait_slot(s):
        for k in range(2):
            _row_copy_wait(ys_ref, buf_ref.at[s, k], sem.at[s], tm)

    @pl.when(i == 0)
    def _():
        def body(r, carry):
            start_row(0, 0, r)
            return carry
        lax.fori_loop(0, tm, body, 0, unroll=8)

    slot = i % 2
    wait_slot(slot)
    nxt = jnp.minimum(i + 1, n - 1)
    m = mod_ref[0]
    group = tm // COMBINE_GROUPS
    y0 = _load_token_tiles(buf_ref.at[slot, 0])
    y1 = _load_token_tiles(buf_ref.at[slot, 1])
    pieces = []
    for g in range(COMBINE_GROUPS):
        for r in range(g * group, (g + 1) * group):
            start_row(nxt, 1 - slot, r)
        rows = slice(g * group, (g + 1) * group)
        gw = gw_ref[rows, :]
        f = gw[:, 0:1] * y0[rows] + gw[:, 1:2] * y1[rows]
        pieces.append(_layer_norm(ALPHA * x1_ref[rows, :] + m[5:6] * f, lng_ref[...], lnb_ref[...]))
    out = jnp.concatenate(pieces, axis=0)

    @pl.when(i == n - 1)
    def _():
        wait_slot(1 - slot)

    if split:
        prompt_ref, sample_ref = out_refs

        @pl.when(i < N_PROMPT // tm)
        def _():
            prompt_ref[...] = out

        @pl.when(i >= N_PROMPT // tm)
        def _():
            sample_ref[...] = out
    else:
        out_refs[0][...] = out


def _moe_combine(pos, gate_w, x1, mod, ln_g, ln_b, ys, split):
    tm = COMBINE_TM
    np_blk = N_PROMPT // tm
    if split:
        out_shape = (jax.ShapeDtypeStruct((N_PROMPT, D_MODEL), F32), jax.ShapeDtypeStruct((N_SAMPLE, D_MODEL), F32))
        out_specs = (pl.BlockSpec((tm, D_MODEL), lambda i, pos: (jnp.minimum(i, np_blk - 1), 0)),
                     pl.BlockSpec((tm, D_MODEL), lambda i, pos: (jnp.maximum(i - np_blk, 0), 0)))
    else:
        out_shape = jax.ShapeDtypeStruct((N_TOK, D_MODEL), F32)
        out_specs = pl.BlockSpec((tm, D_MODEL), lambda i, pos: (i, 0))
    return pl.pallas_call(
        functools.partial(_combine_kernel, split=split),
        out_shape=out_shape,
        grid_spec=pltpu.PrefetchScalarGridSpec(
            num_scalar_prefetch=1,
            grid=(N_TOK // tm,),
            in_specs=[
                pl.BlockSpec((tm, LANES), lambda i, pos: (i, 0)),
                pl.BlockSpec((tm, D_MODEL), lambda i, pos: (i, 0)),
                pl.BlockSpec((1, 6, D_MODEL), lambda i, pos: (_cond_row(i * tm), 0, 0)),
                pl.BlockSpec((1, D_MODEL), lambda i, pos: (0, 0)),
                pl.BlockSpec((1, D_MODEL), lambda i, pos: (0, 0)),
                pl.BlockSpec(memory_space=pl.ANY),
            ],
            out_specs=out_specs,
            scratch_shapes=[pltpu.VMEM((2, 2, tm, N_CHUNKS, LANES), F32), pltpu.SemaphoreType.DMA((2,))],
        ),
        compiler_params=_params("arbitrary"),
        name="moe_combine",
    )(pos, gate_w, x1, mod, ln_g.reshape(1, D_MODEL), ln_b.reshape(1, D_MODEL), ys)


def _moe_ffn(h2_tiles, route_i, gate_w, counts, layer, wg, wu, wd, x1, mod, ln_g, ln_b, split=False):
    pos, tile_expert, n_tiles, fill_rows = _moe_plan(route_i, counts)
    xs = _moe_dispatch(pos, fill_rows, h2_tiles)
    ys = _moe_experts(tile_expert, n_tiles, xs, layer, wg, wu, wd)
    return _moe_combine(pos, gate_w, x1, mod, ln_g, ln_b, ys, split)


DN_GATE_COL0 = DN_CONV_CH
DN_AB_COL0 = DN_CONV_CH + DN_WIDTH
DN_IN = DN_AB_COL0 + LANES
N_DIRS = 2


def _dn_prep_kernel(q_ref, k_ref, v_ref, wq_ref, wk_ref, wv_ref, qo_ref, ko_ref, vo_ref):
    length = q_ref.shape[0]
    row = lax.broadcasted_iota(jnp.int32, (length, DN_DK), 0)

    def conv(x, w):
        prev = jnp.where(row == 0, 0.0, pltpu.roll(x, 1, axis=0))
        nxt = jnp.where(row == length - 1, 0.0, pltpu.roll(x, length - 1, axis=0))
        return _silu(prev * w[0:1] + x * w[1:2] + nxt * w[2:3])

    def l2n(x):
        return x * lax.rsqrt(jnp.sum(x * x, axis=-1, keepdims=True) + 1e-6)

    for h in range(DN_HEADS):
        sl = slice(h * DN_DK, (h + 1) * DN_DK)
        qo_ref[h] = (l2n(conv(q_ref[:, sl], wq_ref[:, sl])) * (DN_DK ** -0.5)).astype(qo_ref.dtype)
        ko_ref[h] = l2n(conv(k_ref[:, sl], wk_ref[:, sl])).astype(ko_ref.dtype)
        vo_ref[h] = conv(v_ref[:, sl], wv_ref[:, sl]).astype(vo_ref.dtype)


def _dn_prep(z1, conv_w, nb, length, row_blk0):
    shp = jax.ShapeDtypeStruct((DN_HEADS, nb * length, DN_DK), BF16)
    blk = lambda part: pl.BlockSpec((length, DN_QK_WIDTH), lambda b: (row_blk0 + b, part))
    wblk = lambda part: pl.BlockSpec((3, DN_QK_WIDTH), lambda b: (0, part))
    oblk = pl.BlockSpec((DN_HEADS, length, DN_DK), lambda b: (0, b, 0))
    return pl.pallas_call(
        _dn_prep_kernel,
        out_shape=(shp, shp, shp),
        grid=(nb,),
        in_specs=[blk(0), blk(1), blk(2), wblk(0), wblk(1), wblk(2)],
        out_specs=(oblk, oblk, oblk),
        compiler_params=_params("arbitrary"),
        name="deltanet_prep",
    )(z1, z1, z1, conv_w, conv_w, conv_w)


def _dn_gates_kernel(ab_ref, alog_ref, dtb_ref, o_ref):
    length = ab_ref.shape[0]
    x = ab_ref[...]
    a = x + dtb_ref[...]
    softplus = jnp.maximum(a, 0.0) + jnp.log1p(jnp.exp(-jnp.abs(a)))
    g = -jnp.exp(alog_ref[...]) * softplus
    beta = jax.nn.sigmoid(x)
    ri = lax.broadcasted_iota(jnp.int32, (DN_CHUNK, DN_CHUNK), 0)
    ci = lax.broadcasted_iota(jnp.int32, (DN_CHUNK, DN_CHUNK), 1)
    tril = (ri >= ci).astype(F32)
    triu = (ri <= ci).astype(F32)
    lane = lax.broadcasted_iota(jnp.int32, (DN_CHUNK, LANES), 1)
    for c in range(length // DN_CHUNK):
        rows = slice(c * DN_CHUNK, (c + 1) * DN_CHUNK)
        gch = g[rows]
        pre = jnp.dot(tril, gch, precision=lax.Precision.HIGHEST, preferred_element_type=F32)
        suf = jnp.dot(triu, gch, precision=lax.Precision.HIGHEST, preferred_element_type=F32)
        gc = jnp.where(lane < DN_HEADS, pre, suf)
        o_ref[rows, :] = jnp.where(lane < N_DIRS * DN_HEADS, gc, beta[rows])


def _dn_gates(z1, a_log, dt_bias, nb, length, row_blk0):
    pad = LANES - N_DIRS * DN_HEADS
    alog = jnp.pad(a_log.reshape(1, -1).astype(F32), ((0, 0), (0, pad)))
    dtb = jnp.pad(dt_bias.reshape(1, -1).astype(F32), ((0, 0), (0, pad)))
    return pl.pallas_call(
        _dn_gates_kernel,
        out_shape=jax.ShapeDtypeStruct((nb * length, LANES), F32),
        grid=(nb,),
        in_specs=[
            pl.BlockSpec((length, LANES), lambda b: (row_blk0 + b, DN_AB_COL0 // LANES)),
            pl.BlockSpec((1, LANES), lambda b: (0, 0)),
            pl.BlockSpec((1, LANES), lambda b: (0, 0)),
        ],
        out_specs=pl.BlockSpec((length, LANES), lambda b: (b, 0)),
        compiler_params=_params("arbitrary"),
        name="deltanet_gates",
    )(z1, alog, dtb)


def _delta_chunk(q, k, v, beta, gcol, grow, s, lower):
    ri = lax.broadcasted_iota(jnp.int32, (DN_CHUNK, DN_CHUNK), 0)
    ci = lax.broadcasted_iota(jnp.int32, (DN_CHUNK, DN_CHUNK), 1)
    incl = (ri >= ci) if lower else (ri <= ci)
    strict = (ri > ci) if lower else (ri < ci)
    decay = jnp.exp(jnp.where(incl, gcol - grow, -jnp.inf))
    kb = k * beta
    kq = _dot_nt(jnp.concatenate([kb, q], axis=0).astype(BF16), k.astype(BF16))
    kk, a_qk = kq[:DN_CHUNK], kq[DN_CHUNK:] * decay
    yield
    tri_l = jnp.where(strict, kk * decay, 0.0)

    def off_block(size):
        same = (ri // (2 * size)) == (ci // (2 * size))
        rpar, cpar = (ri // size) % 2, (ci // size) % 2
        return same & ((rpar == 1) & (cpar == 0) if lower else (rpar == 0) & (cpar == 1))

    p = jnp.where(ri == ci, 1.0, 0.0) - jnp.where(off_block(1), tri_l, 0.0)
    size = 2
    while size < DN_CHUNK:
        p16 = p.astype(BF16)
        pc = _dot(p16, jnp.where(off_block(size), tri_l, 0.0).astype(BF16))
        yield
        p = p - _dot(pc.astype(BF16), p16)
        yield
        size *= 2
    eg = jnp.exp(gcol)
    uw = _dot(p.astype(BF16), jnp.concatenate([v * beta, kb * eg], axis=1).astype(BF16))
    yield
    u, w = uw[:, :DN_DV], uw[:, DN_DV:]
    ws = _dot(jnp.concatenate([w, q * eg], axis=0).astype(BF16), s.astype(BF16))
    v_new, o_inter = u - ws[:DN_CHUNK], ws[DN_CHUNK:]
    yield
    v_new16 = v_new.astype(BF16)
    o = o_inter + _dot(a_qk.astype(BF16), v_new16)
    g_last = gcol[DN_CHUNK - 1:DN_CHUNK] if lower else gcol[0:1]
    kd = k * jnp.exp(g_last - gcol)
    s_new = s * jnp.exp(g_last) + _dot_tn(kd.astype(BF16), v_new16)
    return o, s_new


def _run_interleaved(chains):
    results = [None] * len(chains)
    active = list(enumerate(chains))
    while active:
        still = []
        for idx, gen in active:
            try:
                next(gen)
                still.append((idx, gen))
            except StopIteration as stop:
                results[idx] = stop.value
        active = still
    return results


def _dn_scan_kernel(*refs, has_init, want_final):
    (qf_ref, kf_ref, vf_ref, gf_ref, qb_ref, kb_ref, vb_ref, gb_ref), rest = refs[:8], refs[8:]
    if has_init:
        s0_ref, rest = rest[0], rest[1:]
    of_ref, ob_ref = rest[0], rest[1]
    rest = rest[2:]
    if want_final:
        sf_ref, rest = rest[0], rest[1:]
    s_ref = rest[0]
    c = pl.program_id(1)
    nc = pl.num_programs(1)

    @pl.when(c == 0)
    def _():
        for d in range(N_DIRS):
            for h in range(DN_HEADS):
                if has_init:
                    s_ref[d * DN_HEADS + h] = s0_ref[0, d, h]
                else:
                    s_ref[d * DN_HEADS + h] = jnp.zeros((DN_DK, DN_DV), F32)

    dirs = ((qf_ref, kf_ref, vf_ref, gf_ref, of_ref, True), (qb_ref, kb_ref, vb_ref, gb_ref, ob_ref, False))
    chains = []
    for d, (q_ref, k_ref, v_ref, g_ref, o_ref, lower) in enumerate(dirs):
        gates = g_ref[...]
        gates_t = gates.T
        for h in range(DN_HEADS):
            j = d * DN_HEADS + h
            jb = N_DIRS * DN_HEADS + j
            chains.append(_delta_chunk(q_ref[h].astype(F32), k_ref[h].astype(F32), v_ref[h].astype(F32),
                                       gates[:, jb:jb + 1], gates[:, j:j + 1], gates_t[j:j + 1, :],
                                       s_ref[j], lower))
    results = _run_interleaved(chains)
    for d, (_, _, _, _, o_ref, _) in enumerate(dirs):
        for h in range(DN_HEADS):
            j = d * DN_HEADS + h
            o, s_new = results[j]
            o_ref[:, h * DN_DV:(h + 1) * DN_DV] = o.astype(o_ref.dtype)
            s_ref[j] = s_new

    if want_final:
        @pl.when(c == nc - 1)
        def _():
            for d in range(N_DIRS):
                for h in range(DN_HEADS):
                    sf_ref[0, d, h] = s_ref[d * DN_HEADS + h]


def _dn_scan(qn, kn, vn, gates, s0, nb, length, want_final):
    nc = length // DN_CHUNK
    has_init = s0 is not None
    fwd = lambda b, c: (0, b * nc + c, 0)
    bwd = lambda b, c: (0, b * nc + (nc - 1 - c), 0)
    hm = lambda im: pl.BlockSpec((DN_HEADS, DN_CHUNK, DN_DK), im)
    in_specs = [hm(fwd), hm(fwd), hm(fwd), pl.BlockSpec((DN_CHUNK, LANES), lambda b, c: (b * nc + c, 0)),
                hm(bwd), hm(bwd), hm(bwd), pl.BlockSpec((DN_CHUNK, LANES), lambda b, c: (b * nc + (nc - 1 - c), 0))]
    args = [qn, kn, vn, gates, qn, kn, vn, gates]
    state_blk = pl.BlockSpec((1, N_DIRS, DN_HEADS, DN_DK, DN_DV), lambda b, c: (b, 0, 0, 0, 0))
    if has_init:
        in_specs.append(state_blk)
        args.append(s0)
    o_shape = jax.ShapeDtypeStruct((nb * length, DN_WIDTH), BF16)
    out_shape = [o_shape, o_shape]
    out_specs = [pl.BlockSpec((DN_CHUNK, DN_WIDTH), lambda b, c: (b * nc + c, 0)),
                 pl.BlockSpec((DN_CHUNK, DN_WIDTH), lambda b, c: (b * nc + (nc - 1 - c), 0))]
    if want_final:
        out_shape.append(jax.ShapeDtypeStruct((nb, N_DIRS, DN_HEADS, DN_DK, DN_DV), F32))
        out_specs.append(state_blk)
    return pl.pallas_call(
        functools.partial(_dn_scan_kernel, has_init=has_init, want_final=want_final),
        out_shape=tuple(out_shape),
        grid=(nb, nc),
        in_specs=in_specs,
        out_specs=tuple(out_specs),
        scratch_shapes=[pltpu.VMEM((N_DIRS * DN_HEADS, DN_DK, DN_DV), F32)],
        compiler_params=_params("arbitrary", "arbitrary"),
        name="deltanet_scan",
    )(*args)


DN_MERGE_TM = 256


def _dn_merge_kernel(of_ref, ob_ref, gate_ref, g_ref, o_ref):
    for h in range(DN_HEADS):
        sl = slice(h * DN_DV, (h + 1) * DN_DV)
        o = of_ref[:, sl].astype(F32) + ob_ref[:, sl].astype(F32)
        o = o * lax.rsqrt(jnp.mean(o * o, axis=-1, keepdims=True) + EPS) * g_ref[...]
        o_ref[:, sl] = (o * _silu(gate_ref[:, sl])).astype(o_ref.dtype)


def _dn_merge(o_f, o_b, z1, onorm_g, n_rows, row0):
    tm = DN_MERGE_TM
    blk = pl.BlockSpec((tm, DN_WIDTH), lambda i: (i, 0))
    return pl.pallas_call(
        _dn_merge_kernel,
        out_shape=jax.ShapeDtypeStruct((n_rows, DN_WIDTH), BF16),
        grid=(n_rows // tm,),
        in_specs=[blk, blk,
                  pl.BlockSpec((tm, DN_WIDTH), lambda i: (row0 // tm + i, DN_GATE_COL0 // DN_WIDTH)),
                  pl.BlockSpec((1, DN_DV), lambda i: (0, 0))],
        out_specs=blk,
        compiler_params=_params("arbitrary"),
        name="deltanet_merge",
    )(o_f, o_b, z1, onorm_g.reshape(1, DN_DV))


def _deltanet_stream(z1, conv_w, a_log, dt_bias, onorm_g, s0, nb, length, row0, want_final):
    row_blk0 = row0 // length
    qn, kn, vn = _dn_prep(z1, conv_w, nb, length, row_blk0)
    gates = _dn_gates(z1, a_log, dt_bias, nb, length, row_blk0)
    outs = _dn_scan(qn, kn, vn, gates, s0, nb, length, want_final)
    merged = _dn_merge(outs[0], outs[1], z1, onorm_g, nb * length, row0)
    return merged, (outs[2] if want_final else None)


def kernel(x_prompt, x_sample, cache_nat_k, cache_nat_v, cache_diff_k, cache_diff_v, state_delta, c, c_ctx, w_in_even, w_out_even, nat_bias, diff_lam, diff_subln, w_in_odd, conv_odd, a_log_odd, dt_bias_odd, onorm_odd, w_out_odd, ada_w, ada_b, ln_g, ln_b, router_w, router_b, moe_wg, moe_wu, moe_wd):
    x_parts = (x_prompt.reshape(N_PROMPT, D_MODEL), x_sample.reshape(N_SAMPLE, D_MODEL))
    cond =jnp.concatenate([c_ctx[None], c, jnp.zeros((N_COND - 1 - DEC_BATCH, D_MODEL), F32)], axis=0)
    mod = _ada_modulation(cond, ada_w, ada_b).reshape(DEPTH, N_COND, 6, D_MODEL)
    router_w_pad = jnp.pad(router_w, ((0, 0), (0, LANES - N_EXPERTS)))

    lam_init = 0.8 - 0.6 * math.exp(-0.3 * 0)
    z = _input_projection(x_parts, mod[0], w_in_even[0].astype(BF16))
    new_nat_k = z[:N_PROMPT, _NAT_K0:_NAT_V0].reshape(BATCH, 1, SEQ, NAT_HEADS, HEAD_DIM)
    new_nat_v = z[:N_PROMPT, _NAT_V0:_DIFF_Q0].reshape(BATCH, 1, SEQ, NAT_HEADS, HEAD_DIM)
    new_diff_k = z[:N_PROMPT, _DIFF_K0:_DIFF_V0].reshape(BATCH, 1, SEQ, DIFF_HEADS, 2, HEAD_DIM)
    new_diff_v = z[:N_PROMPT, _DIFF_V0:].reshape(BATCH, 1, SEQ, DIFF_HEADS, 2 * HEAD_DIM)
    o_ctx = _context_attention(z, diff_lam[0], diff_subln[0], lam_init)
    o_nat = _nat_latent_attention(z, cache_nat_k[:, 0].reshape(DEC_BATCH, PAST_LEN, NAT_WIDTH),
                                  cache_nat_v[:, 0].reshape(DEC_BATCH, PAST_LEN, NAT_WIDTH),
                                  _nat_bias_table(nat_bias[0]))
    o_diff = _diff_latent_attention(z, cache_diff_k[:, 0].reshape(DEC_BATCH, PAST_LEN, DIFF_WIDTH),
                                    cache_diff_v[:, 0].reshape(DEC_BATCH, PAST_LEN, DIFF_WIDTH),
                                    diff_lam[0], diff_subln[0], lam_init)
    x1, *routed = _output_projection((o_ctx, o_nat, o_diff), x_parts, mod[0], w_out_even[0].astype(BF16),
                                     ln_g[0, 0], ln_b[0, 0], router_w_pad, router_b)
    x = _moe_ffn(*routed, 0, moe_wg, moe_wu, moe_wd, x1, mod[0], ln_g[0, 1], ln_b[0, 1])

    w1 = w_in_odd[0]
    w1 = jnp.concatenate([w1[:, :DN_CONV_CH], w1[:, DN_CONV_CH + 4 * DN_HEADS:],
                          w1[:, DN_CONV_CH:DN_CONV_CH + 4 * DN_HEADS],
                          jnp.zeros((D_MODEL, LANES - 4 * DN_HEADS), F32)], axis=1).astype(BF16)
    z1 = _input_projection((x,), mod[1], w1, tm=256)
    o_p, new_state = _deltanet_stream(z1, conv_odd[0], a_log_odd[0], dt_bias_odd[0], onorm_odd[0],
                                      None, BATCH, SEQ, 0, True)
    o_s, _ = _deltanet_stream(z1, conv_odd[0], a_log_odd[0], dt_bias_odd[0], onorm_odd[0],
                              state_delta[:, 0], DEC_BATCH, DEC_SEQ, N_PROMPT, False)
    x1, *routed = _output_projection((o_p, o_s), (x,), mod[1], w_out_odd[0].astype(BF16), ln_g[1, 0], ln_b[1, 0],
                                     router_w_pad, router_b)
    y_prompt, y_sample = _moe_ffn(*routed, 1, moe_wg, moe_wu, moe_wd, x1, mod[1], ln_g[1, 1], ln_b[1, 1],
                                  split=True)
    y_prompt = y_prompt.reshape(BATCH, SEQ, D_MODEL)
    y_sample = y_sample.reshape(DEC_BATCH, DEC_SEQ, D_MODEL)
    return (y_prompt, y_sample, new_nat_k, new_nat_v, new_diff_k, new_diff_v, new_state[:, None])
```

```python
import functools
import math

import jax
import jax.numpy as jnp
import numpy as np
from jax import lax
from jax.experimental import pallas as pl
from jax.experimental.pallas import tpu as pltpu

F32 = jnp.float32
BF16 = jnp.bfloat16

D_MODEL = 1024
BATCH = 16
SEQ = 256
DEC_BATCH = 8
DEC_SEQ = 1024
PAST_LEN = 256
GRID_W = 64
N_ROWS = DEC_SEQ // GRID_W

HEAD_DIM = 64
NAT_HEADS = 8
NAT_WIN_H = 8
NAT_WIN_W = 16
DIFF_HEADS = 4
NAT_WIDTH = NAT_HEADS * HEAD_DIM
DIFF_WIDTH = DIFF_HEADS * 2 * HEAD_DIM
EVEN_IN = 3 * NAT_WIDTH + 3 * DIFF_WIDTH
ROPE_THETA = 10000.0

DN_HEADS = 8
DN_DK = 128
DN_DV = 128
DN_QK_WIDTH = DN_HEADS * DN_DK
DN_WIDTH = DN_HEADS * DN_DV
DN_CONV_CH = 2 * DN_QK_WIDTH + DN_WIDTH
DN_CHUNK = 64

N_EXPERTS = 16
N_GROUPS = 4
EXPERTS_PER_GROUP = N_EXPERTS // N_GROUPS
EXPERT_FF = 512

DEPTH = 2
ALPHA = (2 * DEPTH) ** 0.25
EPS = 1e-5

N_PROMPT = BATCH * SEQ
N_SAMPLE = DEC_BATCH * DEC_SEQ
N_TOK = N_PROMPT + N_SAMPLE
N_COND = 16

VMEM_LIMIT = 56 * 1024 * 1024
LANES = 128
N_CHUNKS = D_MODEL // LANES


def _params(*sem):
    return pltpu.CompilerParams(dimension_semantics=sem, vmem_limit_bytes=VMEM_LIMIT)


def _cond_row(row0):
    return jnp.where(row0 < N_PROMPT, 0, 1 + (row0 - N_PROMPT) // DEC_SEQ)


def _silu(x):
    return x * jax.nn.sigmoid(x)


def _layer_norm(r, g, b):
    mu = jnp.mean(r, axis=-1, keepdims=True)
    xc = r - mu
    var = jnp.mean(xc * xc, axis=-1, keepdims=True)
    return xc * lax.rsqrt(var + EPS) * g + b


def _dot(a, b):
    return jnp.dot(a, b, preferred_element_type=F32)


def _dot_nt(a, b):
    return lax.dot_general(a, b, (((1,), (1,)), ((), ())), preferred_element_type=F32)


def _dot_tn(a, b):
    return lax.dot_general(a, b, (((0,), (0,)), ((), ())), preferred_element_type=F32)


ADA_TN = 1536


def _ada_kernel(c_ref, w_ref, b_ref, o_ref):
    s = _silu(c_ref[...])
    o_ref[0] = _dot(s.astype(BF16), w_ref[0].astype(BF16)) + b_ref[0]


def _ada_modulation(cond, ada_w, ada_b):
    n = 6 * D_MODEL
    return pl.pallas_call(
        _ada_kernel,
        out_shape=jax.ShapeDtypeStruct((DEPTH, N_COND, n), F32),
        grid=(DEPTH, n // ADA_TN),
        in_specs=[
            pl.BlockSpec((N_COND, D_MODEL), lambda l, j: (0, 0)),
            pl.BlockSpec((1, D_MODEL, ADA_TN), lambda l, j: (l, 0, j)),
            pl.BlockSpec((1, 1, ADA_TN), lambda l, j: (l, 0, j)),
        ],
        out_specs=pl.BlockSpec((1, N_COND, ADA_TN), lambda l, j: (l, 0, j)),
        compiler_params=_params("arbitrary", "arbitrary"),
        name="ada_modulation",
    )(cond, ada_w, ada_b.reshape(DEPTH, 1, n))


PROJ_TM = 512


def _stream_specs(widths, tm):
    if len(widths) == 1:
        return [pl.BlockSpec((tm, widths[0]), lambda i: (i, 0))]
    n_ctx = N_PROMPT // tm
    return ([pl.BlockSpec((tm, widths[0]), lambda i: (jnp.minimum(i, n_ctx - 1), 0))]
            + [pl.BlockSpec((tm, w), lambda i: (jnp.maximum(i - n_ctx, 0), 0)) for w in widths[1:]])


def _stream_tile(refs, tm):
    if len(refs) == 1:
        return refs[0][...]
    latent = [r[...] for r in refs[1:]]
    latent = latent[0] if len(latent) == 1 else jnp.concatenate(latent, axis=1)
    return jnp.where(pl.program_id(0) * tm < N_PROMPT, refs[0][...], latent)


def _inproj_kernel(*refs, n_x):
    x_refs, (mod_ref, w_ref, o_ref) = refs[:n_x], refs[n_x:]
    m = mod_ref[0]
    h = _stream_tile(x_refs, o_ref.shape[0]) * (1.0 + m[1:2]) + m[0:1]
    o_ref[...] = _dot(h.astype(BF16), w_ref[...])


def _input_projection(x_parts, mod, w_bf16, tm=PROJ_TM):
    n = w_bf16.shape[1]
    return pl.pallas_call(
        functools.partial(_inproj_kernel, n_x=len(x_parts)),
        out_shape=jax.ShapeDtypeStruct((N_TOK, n), F32),
        grid=(N_TOK // tm,),
        in_specs=_stream_specs([a.shape[1] for a in x_parts], tm) + [
            pl.BlockSpec((1, 6, D_MODEL), lambda i: (_cond_row(i * tm), 0, 0)),
            pl.BlockSpec((D_MODEL, n), lambda i: (0, 0)),
        ],
        out_specs=pl.BlockSpec((tm, n), lambda i: (i, 0)),
        compiler_params=_params("arbitrary"),
        name="input_projection",
    )(*x_parts, mod, w_bf16)


_NAT_Q0, _NAT_K0, _NAT_V0 = 0, NAT_WIDTH, 2 * NAT_WIDTH
_DIFF_Q0 = 3 * NAT_WIDTH
_DIFF_K0 = _DIFF_Q0 + DIFF_WIDTH
_DIFF_V0 = _DIFF_K0 + DIFF_WIDTH
ATTN_SCALE = HEAD_DIM ** -0.5


def _diff_lambda(lam_ref, lam_init):
    lp = lam_ref[...]
    return (jnp.exp(jnp.sum(lp[0:1] * lp[1:2], axis=-1, keepdims=True))
            - jnp.exp(jnp.sum(lp[2:3] * lp[3:4], axis=-1, keepdims=True)) + lam_init)


def _scaled_q(q):
    assert math.log2(HEAD_DIM) % 2 == 0
    return (q * ATTN_SCALE).astype(BF16)


def _with_ones(v):
    width = v.shape[1] if v.shape[1] % LANES == 0 else LANES - v.shape[1] % LANES
    return jnp.concatenate([v, jnp.ones((v.shape[0], width), v.dtype)], axis=1)


def _sub_norm(o, g, lam_init):
    ms = jnp.mean(o * o, axis=-1, keepdims=True)
    return o * lax.rsqrt(ms + EPS) * g * (1.0 - lam_init)


def _ctx_attn_kernel(z_ref, lam_ref, subg_ref, o_ref, *, lam_init):
    lam = _diff_lambda(lam_ref, lam_init)

    def softmax_av(q0, k0, v0, v_width):
        q = _scaled_q(z_ref[:, q0:q0 + HEAD_DIM])
        k = z_ref[:, k0:k0 + HEAD_DIM].astype(BF16)
        s = _dot_nt(q, k)
        yield
        e = jnp.exp(s - jnp.max(s, axis=-1, keepdims=True))
        yield
        av = _dot(e.astype(BF16), _with_ones(z_ref[:, v0:v0 + v_width].astype(BF16)))
        yield
        return av[:, :v_width] / av[:, v_width:v_width + 1]

    chains = [softmax_av(_NAT_Q0 + h * HEAD_DIM, _NAT_K0 + h * HEAD_DIM, _NAT_V0 + h * HEAD_DIM, HEAD_DIM)
              for h in range(NAT_HEADS)]
    chains += [softmax_av(_DIFF_Q0 + i * HEAD_DIM, _DIFF_K0 + i * HEAD_DIM,
                          _DIFF_V0 + (i // 2) * 2 * HEAD_DIM, 2 * HEAD_DIM)
               for i in range(2 * DIFF_HEADS)]
    outs = _run_interleaved(chains)
    for h in range(NAT_HEADS):
        o_ref[:, h * HEAD_DIM:(h + 1) * HEAD_DIM] = outs[h].astype(o_ref.dtype)
    for h in range(DIFF_HEADS):
        c = NAT_WIDTH + h * 2 * HEAD_DIM
        o = outs[NAT_HEADS + 2 * h] - lam * outs[NAT_HEADS + 2 * h + 1]
        o_ref[:, c:c + 2 * HEAD_DIM] = _sub_norm(o, subg_ref[...], lam_init).astype(o_ref.dtype)


def _context_attention(z, lam_p, subln_g, lam_init):
    return pl.pallas_call(
        functools.partial(_ctx_attn_kernel, lam_init=lam_init),
        out_shape=jax.ShapeDtypeStruct((N_PROMPT, D_MODEL), BF16),
        grid=(BATCH,),
        in_specs=[
            pl.BlockSpec((SEQ, EVEN_IN), lambda b: (b, 0)),
            pl.BlockSpec((4, HEAD_DIM), lambda b: (0, 0)),
            pl.BlockSpec((1, 2 * HEAD_DIM), lambda b: (0, 0)),
        ],
        out_specs=pl.BlockSpec((SEQ, D_MODEL), lambda b: (b, 0)),
        compiler_params=_params("arbitrary"),
        name="context_attention",
    )(z, lam_p, subln_g.reshape(1, 2 * HEAD_DIM))


NAT_TQ = 512
NAT_KEYS = 12 * GRID_W
_SAMPLE_BLK0 = N_PROMPT // DEC_SEQ


def _nat_bias_table(rel_bias):
    rows = N_ROWS
    kh = min(NAT_WIN_H, rows)
    kw = NAT_WIN_W
    col = np.arange(GRID_W)
    c_start = np.clip(col - kw // 2, 0, GRID_W - kw)
    col_ok = (col[None, :] >= c_start[:, None]) & (col[None, :] < c_start[:, None] + kw)
    dc = np.clip(col[None, :] - col[:, None], 1 - kw, kw - 1) + (NAT_WIN_W - 1)
    col_onehot = ((dc[None] == np.arange(2 * kw - 1)[:, None, None]) & col_ok[None]).astype(np.float32)
    span = NAT_KEYS // GRID_W
    row_onehot = np.zeros((2 * NAT_WIN_H - 1, rows, span), np.float32)
    for r in range(rows):
        start = min(max(r - kh // 2, 0), rows - kh)
        a0 = _nat_key0(r * GRID_W // NAT_TQ) // GRID_W
        assert a0 <= start and start + kh <= a0 + span
        for a in range(start, start + kh):
            row_onehot[a - r + NAT_WIN_H - 1, r, a - a0] = 1.0
    hi = lax.Precision.HIGHEST
    by_row = jnp.einsum('hed,era->hrad', rel_bias.astype(F32), row_onehot, precision=hi)
    table = jnp.einsum('hrad,dqk->hrqak', by_row, col_onehot, precision=hi)
    inside = (row_onehot.sum(0) > 0)[:, None, :, None] & col_ok[None, :, None, :]
    return jnp.where(inside, table, -jnp.inf).reshape(NAT_HEADS, DEC_SEQ, NAT_KEYS)


def _nat_key0(qt):
    half = DEC_SEQ // NAT_TQ // 2
    return (qt >= half) * (DEC_SEQ - NAT_KEYS)


def _nat_latent_kernel(q_ref, k_ref, v_ref, ck_ref, cv_ref, bias_ref, o_ref):
    key0 = pl.multiple_of(_nat_key0(pl.program_id(1)).astype(jnp.int32), DEC_SEQ - NAT_KEYS)
    keys = pl.ds(key0, NAT_KEYS)

    def head(j):
        sl = slice(j * HEAD_DIM, (j + 1) * HEAD_DIM)
        q = _scaled_q(q_ref[:, sl])
        s_loc = _dot_nt(q, k_ref[keys, sl].astype(BF16)) + bias_ref[j]
        s_ctx = _dot_nt(q, ck_ref[0, :, sl].astype(BF16))
        yield
        m = jnp.maximum(jnp.max(s_loc, axis=-1, keepdims=True), jnp.max(s_ctx, axis=-1, keepdims=True))
        yield
        e_loc = jnp.exp(s_loc - m)
        e_ctx = jnp.exp(s_ctx - m)
        yield
        o = (_dot(e_loc.astype(BF16), _with_ones(v_ref[keys, sl].astype(BF16)))
             + _dot(e_ctx.astype(BF16), _with_ones(cv_ref[0, :, sl].astype(BF16))))
        yield
        return (o[:, :HEAD_DIM] / o[:, HEAD_DIM:HEAD_DIM + 1]).astype(o_ref.dtype)

    outs = _run_interleaved([head(j) for j in range(2)])
    for j in range(2):
        o_ref[:, j * HEAD_DIM:(j + 1) * HEAD_DIM] = outs[j]


def _nat_latent_attention(z, ck, cv, bias):
    nq = DEC_SEQ // NAT_TQ
    qblk0 = N_PROMPT // NAT_TQ
    pair = 2 * HEAD_DIM
    return pl.pallas_call(
        _nat_latent_kernel,
        out_shape=jax.ShapeDtypeStruct((N_SAMPLE, NAT_WIDTH), BF16),
        grid=(NAT_HEADS // 2, nq, DEC_BATCH),
        in_specs=[
            pl.BlockSpec((NAT_TQ, pair), lambda hp, qt, b: (qblk0 + b * nq + qt, _NAT_Q0 // pair + hp)),
            pl.BlockSpec((DEC_SEQ, pair), lambda hp, qt, b: (_SAMPLE_BLK0 + b, _NAT_K0 // pair + hp)),
            pl.BlockSpec((DEC_SEQ, pair), lambda hp, qt, b: (_SAMPLE_BLK0 + b, _NAT_V0 // pair + hp)),
            pl.BlockSpec((1, PAST_LEN, pair), lambda hp, qt, b: (b, 0, hp)),
            pl.BlockSpec((1, PAST_LEN, pair), lambda hp, qt, b: (b, 0, hp)),
            pl.BlockSpec((2, NAT_TQ, NAT_KEYS), lambda hp, qt, b: (hp, qt, 0)),
        ],
        out_specs=pl.BlockSpec((NAT_TQ, pair), lambda hp, qt, b: (b * nq + qt, hp)),
        compiler_params=_params("arbitrary", "arbitrary", "arbitrary"),
        name="nat_latent_attention",
    )(z, z, z, ck, cv, bias)


DIFF_TQ = 512


def _rope_tables():
    nf = HEAD_DIM // 4
    t = jnp.arange(DEC_SEQ)
    inv = ROPE_THETA ** (-jnp.arange(nf, dtype=F32) / nf)
    ang_r = (t // GRID_W).astype(F32)[:, None] * inv
    ang_c = (t % GRID_W).astype(F32)[:, None] * inv
    cos = jnp.concatenate([jnp.cos(ang_r)] * 2 + [jnp.cos(ang_c)] * 2, axis=-1)
    sin = jnp.concatenate([-jnp.sin(ang_r), jnp.sin(ang_r), -jnp.sin(ang_c), jnp.sin(ang_c)], axis=-1)
    return jnp.tile(cos, (1, 2)), jnp.tile(sin, (1, 2))


def _rope(x, cos, sin):
    nf = HEAD_DIM // 4
    lane = lax.broadcasted_iota(jnp.int32, x.shape, 1)
    upper = (lane // nf) % 2 == 1
    partner = jnp.where(upper, pltpu.roll(x, nf, axis=1), pltpu.roll(x, x.shape[1] - nf, axis=1))
    return x * cos + partner * sin


def _diff_latent_kernel(q_ref, k_ref, v_ref, ck_ref, cv_ref, cos_ref, sin_ref, lam_ref, subg_ref,
                        o_ref, kr_ref, *, lam_init):
    qt = pl.program_id(2)

    @pl.when(qt == 0)
    def _():
        kr_ref[...] = _rope(k_ref[...], cos_ref[...], sin_ref[...]).astype(BF16)

    lam = _diff_lambda(lam_ref, lam_init)
    row0 = pl.multiple_of(qt * DIFF_TQ, DIFF_TQ)
    q = _scaled_q(_rope(q_ref[...], cos_ref[pl.ds(row0, DIFF_TQ), :], sin_ref[pl.ds(row0, DIFF_TQ), :]))
    v_loc = _with_ones(v_ref[...].astype(BF16))
    v_ctx = _with_ones(cv_ref[0].astype(BF16))
    width = 2 * HEAD_DIM

    def softmax_av(j):
        sl = slice(j * HEAD_DIM, (j + 1) * HEAD_DIM)
        s_ctx = _dot_nt(q[:, sl], ck_ref[0, :, sl].astype(BF16))
        s_loc = _dot_nt(q[:, sl], kr_ref[:, sl])
        yield
        m = jnp.maximum(jnp.max(s_loc, axis=-1, keepdims=True), jnp.max(s_ctx, axis=-1, keepdims=True))
        yield
        e_loc = jnp.exp(s_loc - m)
        e_ctx = jnp.exp(s_ctx - m)
        yield
        av = _dot(e_loc.astype(BF16), v_loc) + _dot(e_ctx.astype(BF16), v_ctx)
        yield
        return av[:, :width] / av[:, width:width + 1]

    parts = _run_interleaved([softmax_av(j) for j in range(2)])
    o = parts[0] - lam * parts[1]
    o_ref[...] = _sub_norm(o, subg_ref[...], lam_init).astype(o_ref.dtype)


def _diff_latent_attention(z, ck, cv, lam_p, subln_g, lam_init):
    nq = DEC_SEQ // DIFF_TQ
    qblk0 = N_PROMPT // DIFF_TQ
    w = 2 * HEAD_DIM
    cos, sin = _rope_tables()
    return pl.pallas_call(
        functools.partial(_diff_latent_kernel, lam_init=lam_init),
        out_shape=jax.ShapeDtypeStruct((N_SAMPLE, DIFF_WIDTH), BF16),
        grid=(DEC_BATCH, DIFF_HEADS, nq),
        in_specs=[
            pl.BlockSpec((DIFF_TQ, w), lambda b, h, qt: (qblk0 + b * nq + qt, _DIFF_Q0 // w + h)),
            pl.BlockSpec((DEC_SEQ, w), lambda b, h, qt: (_SAMPLE_BLK0 + b, _DIFF_K0 // w + h)),
            pl.BlockSpec((DEC_SEQ, w), lambda b, h, qt: (_SAMPLE_BLK0 + b, _DIFF_V0 // w + h)),
            pl.BlockSpec((1, PAST_LEN, w), lambda b, h, qt: (b, 0, h)),
            pl.BlockSpec((1, PAST_LEN, w), lambda b, h, qt: (b, 0, h)),
            pl.BlockSpec((DEC_SEQ, w), lambda b, h, qt: (0, 0)),
            pl.BlockSpec((DEC_SEQ, w), lambda b, h, qt: (0, 0)),
            pl.BlockSpec((4, HEAD_DIM), lambda b, h, qt: (0, 0)),
            pl.BlockSpec((1, w), lambda b, h, qt: (0, 0)),
        ],
        out_specs=pl.BlockSpec((DIFF_TQ, w), lambda b, h, qt: (b * nq + qt, h)),
        scratch_shapes=[pltpu.VMEM((DEC_SEQ, w), BF16)],
        compiler_params=_params("arbitrary", "arbitrary", "arbitrary"),
        name="diff_latent_attention",
    )(z, z, z, ck, cv, cos, sin, lam_p, subln_g.reshape(1, w))


OUT_TM = 256


def _route(logits_t, rb_col):
    sc = jax.nn.sigmoid(logits_t)
    bi = sc + rb_col
    srow = [sc[e:e + 1] for e in range(N_EXPERTS)]
    brow = [bi[e:e + 1] for e in range(N_EXPERTS)]
    n = EXPERTS_PER_GROUP
    gscore = []
    for g in range(N_GROUPS):
        v = brow[g * n:(g + 1) * n]
        best = None
        for i in range(n):
            for j in range(i + 1, n):
                s = v[i] + v[j]
                best = s if best is None else jnp.maximum(best, s)
        gscore.append(best)
    sel = jnp.zeros_like(gscore[0], dtype=jnp.int32)
    best = gscore[0]
    for g in range(1, N_GROUPS):
        better = gscore[g] > best
        sel = jnp.where(better, g, sel)
        best = jnp.where(better, gscore[g], best)

    def pick_group(rows, i):
        out = rows[(N_GROUPS - 1) * n + i]
        for g in range(N_GROUPS - 2, -1, -1):
            out = jnp.where(sel == g, rows[g * n + i], out)
        return out

    bv = [pick_group(brow, i) for i in range(n)]
    sv = [pick_group(srow, i) for i in range(n)]
    i1 = jnp.zeros_like(sel)
    m1, w1 = bv[0], sv[0]
    for i in range(1, n):
        better = bv[i] > m1
        i1 = jnp.where(better, i, i1)
        m1 = jnp.where(better, bv[i], m1)
        w1 = jnp.where(better, sv[i], w1)
    i2 = jnp.zeros_like(sel)
    m2 = jnp.full_like(m1, -jnp.inf)
    w2 = jnp.zeros_like(w1)
    for i in range(n):
        better = (i1 != i) & (bv[i] > m2)
        i2 = jnp.where(better, i, i2)
        m2 = jnp.where(better, bv[i], m2)
        w2 = jnp.where(better, sv[i], w2)
    tot = w1 + w2
    return sel * n + i1, sel * n + i2, w1 / tot, w2 / tot


def _store_token_tiles(ref, value):
    for c in range(N_CHUNKS):
        ref[:, c, :] = value[:, c * LANES:(c + 1) * LANES]


def _load_token_tiles(ref):
    return jnp.concatenate([ref[:, c, :] for c in range(N_CHUNKS)], axis=1)


def _outproj_kernel(*refs, n_o, n_x):
    o_refs, x_refs, refs = refs[:n_o], refs[n_o:n_o + n_x], refs[n_o + n_x:]
    mod_ref, w_ref, lng_ref, lnb_ref, rw_ref, rb_ref, x1_ref, h2_ref, ri_ref, rw_out_ref, cnt_ref = refs
    tm = x1_ref.shape[0]
    m = mod_ref[0]
    y = _dot(_stream_tile(o_refs, tm), w_ref[...])
    x1 = _layer_norm(ALPHA * _stream_tile(x_refs, tm) + m[2:3] * y, lng_ref[...], lnb_ref[...])
    x1_ref[...] = x1
    h2 = x1 * (1.0 + m[4:5]) + m[3:4]
    _store_token_tiles(h2_ref, h2)
    rw = rw_ref[...]
    h_hi, rw_hi = h2.astype(BF16), rw.astype(BF16)
    h_lo = (h2 - h_hi.astype(F32)).astype(BF16)
    rw_lo = (rw - rw_hi.astype(F32)).astype(BF16)
    logits = _dot(h_hi, rw_hi) + (_dot(h_lo, rw_hi) + _dot(h_hi, rw_lo))
    e1, e2, w1, w2 = _route(logits.T[:N_EXPERTS], rb_ref[...])
    eid = lax.broadcasted_iota(jnp.int32, (N_EXPERTS, tm), 0)
    onehot = (eid == e1) | (eid == e2)
    ti = lax.broadcasted_iota(jnp.int32, (tm, tm), 0)
    tj = lax.broadcasted_iota(jnp.int32, (tm, tm), 1)
    earlier = jnp.where(ti < tj, 1.0, 0.0).astype(BF16)
    rank = _dot(jnp.where(onehot, 1.0, 0.0).astype(BF16), earlier)
    r1 = jnp.sum(jnp.where(eid == e1, rank, 0.0), axis=0, keepdims=True).astype(jnp.int32)
    r2 = jnp.sum(jnp.where(eid == e2, rank, 0.0), axis=0, keepdims=True).astype(jnp.int32)
    ri_ref[...] = jnp.concatenate([e1, e2, r1, r2, jnp.zeros((4, tm), jnp.int32)], axis=0)
    wt = jnp.concatenate([w1, w2, jnp.zeros((LANES - 2, tm), F32)], axis=0)
    rw_out_ref[...] = wt.T
    cnt = jnp.sum(jnp.where(onehot, 1.0, 0.0), axis=1, keepdims=True)
    cnt_ref[0] = jnp.broadcast_to(cnt, (N_EXPERTS, LANES)).astype(jnp.int32)


def _output_projection(o_parts, x_parts, mod, w_bf16, ln_g, ln_b, router_w_pad, router_b, tm=OUT_TM):
    nt = N_TOK // tm
    return pl.pallas_call(
        functools.partial(_outproj_kernel, n_o=len(o_parts), n_x=len(x_parts)),
        out_shape=(jax.ShapeDtypeStruct((N_TOK, D_MODEL), F32),
                   jax.ShapeDtypeStruct((N_TOK, N_CHUNKS, LANES), F32),
                   jax.ShapeDtypeStruct((8, N_TOK), jnp.int32),
                   jax.ShapeDtypeStruct((N_TOK, LANES), F32),
                   jax.ShapeDtypeStruct((nt, N_EXPERTS, LANES), jnp.int32)),
        grid=(nt,),
        in_specs=_stream_specs([a.shape[1] for a in o_parts], tm) + _stream_specs([a.shape[1] for a in x_parts], tm) + [
            pl.BlockSpec((1, 6, D_MODEL), lambda i: (_cond_row(i * tm), 0, 0)),
            pl.BlockSpec((D_MODEL, D_MODEL), lambda i: (0, 0)),
            pl.BlockSpec((1, D_MODEL), lambda i: (0, 0)),
            pl.BlockSpec((1, D_MODEL), lambda i: (0, 0)),
            pl.BlockSpec((D_MODEL, LANES), lambda i: (0, 0)),
            pl.BlockSpec((N_EXPERTS, 1), lambda i: (0, 0)),
        ],
        out_specs=(pl.BlockSpec((tm, D_MODEL), lambda i: (i, 0)),
                   pl.BlockSpec((tm, N_CHUNKS, LANES), lambda i: (i, 0, 0)),
                   pl.BlockSpec((8, tm), lambda i: (0, i)),
                   pl.BlockSpec((tm, LANES), lambda i: (i, 0)),
                   pl.BlockSpec((1, N_EXPERTS, LANES), lambda i: (i, 0, 0))),
        compiler_params=_params("arbitrary"),
        name="output_projection",
    )(*o_parts, *x_parts, mod, w_bf16, ln_g.reshape(1, D_MODEL), ln_b.reshape(1, D_MODEL),
      router_w_pad, router_b.reshape(N_EXPERTS, 1))


N_ASSIGN = 2 * N_TOK
MOE_TM = 256
MOE_ROWS = N_ASSIGN + N_EXPERTS * MOE_TM
MOE_TILES = MOE_ROWS // MOE_TM
DISPATCH_TM = 512
COMBINE_TM = 256


def _moe_plan(route_i, counts):
    cnt = counts[:, :, 0]
    total = jnp.sum(cnt, axis=0)
    padded = (total + MOE_TM - 1) // MOE_TM * MOE_TM
    seg_end = jnp.cumsum(padded)
    seg_start = seg_end - padded
    tile_base = seg_start[None, :] + jnp.cumsum(cnt, axis=0) - cnt
    base_tok = jnp.repeat(tile_base, OUT_TM, axis=0)
    eids = jnp.arange(N_EXPERTS, dtype=jnp.int32)[None, :]
    pos = [jnp.sum(jnp.where(route_i[k][:, None] == eids, base_tok, 0), axis=1) + route_i[2 + k]
           for k in range(2)]
    pos = jnp.concatenate(pos).astype(jnp.int32)
    tile_row0 = jnp.arange(MOE_TILES, dtype=jnp.int32) * MOE_TM
    tile_expert = jnp.minimum(jnp.sum(seg_end[None, :] <= tile_row0[:, None], axis=1), N_EXPERTS - 1)
    n_tiles = (seg_end[-1] // MOE_TM).reshape(1)
    last = MOE_ROWS - MOE_TM
    fill_rows = jnp.concatenate([jnp.minimum(seg_start + total, last),
                                 last - jnp.arange(N_EXPERTS, dtype=jnp.int32) * MOE_TM])
    return pos, tile_expert.astype(jnp.int32), n_tiles.astype(jnp.int32), fill_rows.astype(jnp.int32)


def _row_copy_wait(src_hbm, dst, sem, n_rows):
    pltpu.make_async_copy(src_hbm.at[pl.ds(0, n_rows)], dst.at[pl.ds(0, n_rows)], sem).wait()


def _dispatch_kernel(pos_ref, fill_ref, h_ref, xs_ref, zero_ref, sem, fill_sem):
    tm = h_ref.shape[0]
    t0 = pl.program_id(0) * tm

    @pl.when(pl.program_id(0) == 0)
    def _():
        zero_ref[...] = jnp.zeros_like(zero_ref)
        fills = [pltpu.make_async_copy(zero_ref, xs_ref.at[pl.ds(fill_ref[j], MOE_TM)], fill_sem)
                 for j in range(2 * N_EXPERTS)]
        for cp in fills[N_EXPERTS:]:
            cp.start()
        for cp in fills[N_EXPERTS:]:
            cp.wait()
        for cp in fills[:N_EXPERTS]:
            cp.start()
            cp.wait()

    def body(r, carry):
        for k in range(2):
            pltpu.make_async_copy(h_ref.at[r], xs_ref.at[pos_ref[k * N_TOK + t0 + r]], sem).start()
        return carry

    lax.fori_loop(0, tm, body, 0, unroll=8)
    for _ in range(2):
        _row_copy_wait(h_ref, xs_ref, sem, tm)


def _moe_dispatch(pos, fill_rows, h2_tiles):
    tm = DISPATCH_TM
    return pl.pallas_call(
        _dispatch_kernel,
        out_shape=jax.ShapeDtypeStruct((MOE_ROWS, N_CHUNKS, LANES), F32),
        grid_spec=pltpu.PrefetchScalarGridSpec(
            num_scalar_prefetch=2,
            grid=(N_TOK // tm,),
            in_specs=[pl.BlockSpec((tm, N_CHUNKS, LANES), lambda i, pos, fill: (i, 0, 0))],
            out_specs=pl.BlockSpec(memory_space=pl.ANY),
            scratch_shapes=[pltpu.VMEM((MOE_TM, N_CHUNKS, LANES), F32), pltpu.SemaphoreType.DMA,
                            pltpu.SemaphoreType.DMA],
        ),
        compiler_params=_params("arbitrary"),
        name="moe_dispatch",
    )(pos, fill_rows, h2_tiles)


def _expert_kernel(te_ref, nt_ref, xs_ref, wg_ref, wu_ref, wd_ref, ys_ref, wg16_ref, wu16_ref, wd16_ref):
    i = pl.program_id(0)

    @pl.when(i < nt_ref[0])
    def _():
        @pl.when((i == 0) | (te_ref[i] != te_ref[jnp.maximum(i - 1, 0)]))
        def _():
            wg16_ref[...] = wg_ref[0, 0].astype(BF16)
            wu16_ref[...] = wu_ref[0, 0].astype(BF16)
            wd16_ref[...] = wd_ref[0, 0].astype(BF16)

        x = _load_token_tiles(xs_ref).astype(BF16)
        he = _silu(_dot(x, wg16_ref[...])) * _dot(x, wu16_ref[...])
        _store_token_tiles(ys_ref, _dot(he.astype(BF16), wd16_ref[...]))

    @pl.when(i >= nt_ref[0])
    def _():
        ys_ref[...] = jnp.zeros_like(ys_ref)


def _moe_experts(tile_expert, n_tiles, xs, layer, wg, wu, wd):
    row_blk = lambda i, te, nt: (jnp.minimum(i, nt[0] - 1), 0, 0)
    w_blk = lambda i, te, nt: (layer, te[i], 0, 0)
    return pl.pallas_call(
        _expert_kernel,
        out_shape=jax.ShapeDtypeStruct((MOE_ROWS, N_CHUNKS, LANES), F32),
        grid_spec=pltpu.PrefetchScalarGridSpec(
            num_scalar_prefetch=2,
            grid=(MOE_TILES,),
            in_specs=[
                pl.BlockSpec((MOE_TM, N_CHUNKS, LANES), row_blk),
                pl.BlockSpec((1, 1, D_MODEL, EXPERT_FF), w_blk),
                pl.BlockSpec((1, 1, D_MODEL, EXPERT_FF), w_blk),
                pl.BlockSpec((1, 1, EXPERT_FF, D_MODEL), w_blk),
            ],
            out_specs=pl.BlockSpec((MOE_TM, N_CHUNKS, LANES), lambda i, te, nt: (i, 0, 0)),
            scratch_shapes=[pltpu.VMEM((D_MODEL, EXPERT_FF), BF16), pltpu.VMEM((D_MODEL, EXPERT_FF), BF16),
                            pltpu.VMEM((EXPERT_FF, D_MODEL), BF16)],
        ),
        compiler_params=_params("arbitrary"),
        name="moe_experts",
    )(tile_expert, n_tiles, xs, wg, wu, wd)


def _combine_kernel(pos_ref, gw_ref, x1_ref, mod_ref, lng_ref, lnb_ref, ys_ref, *rest, split):
    out_refs, (buf_ref, sem) = rest[:-2], rest[-2:]
    i = pl.program_id(0)
    n = pl.num_programs(0)
    tm = COMBINE_TM

    def start_row(tile, to_slot, r):
        for k in range(2):
            p = pos_ref[k * N_TOK + tile * tm + r]
            pltpu.make_async_copy(ys_ref.at[p], buf_ref.at[to_slot, k, r], sem.at[to_slot]).start()

    def wait_slot(s):
        for k in range(2):
            _row_copy_wait(ys_ref, buf_ref.at[s, k], sem.at[s], tm)

    def issue(tile, to_slot):
        def body(r, carry):
            start_row(tile, to_slot, r)
            return carry
        lax.fori_loop(0, tm, body, 0, unroll=8)

    @pl.when(i == 0)
    def _():
        issue(0, 0)

    @pl.when(i + 1 < n)
    def _():
        issue(i + 1, (i + 1) % 2)

    slot = i % 2
    wait_slot(slot)
    gw = gw_ref[...]
    f = gw[:, 0:1] * _load_token_tiles(buf_ref.at[slot, 0]) + gw[:, 1:2] * _load_token_tiles(buf_ref.at[slot, 1])
    m = mod_ref[0]
    out = _layer_norm(ALPHA * x1_ref[...] + m[5:6] * f, lng_ref[...], lnb_ref[...])

    if split:
        prompt_ref, sample_ref = out_refs

        @pl.when(i < N_PROMPT // tm)
        def _():
            prompt_ref[...] = out

        @pl.when(i >= N_PROMPT // tm)
        def _():
            sample_ref[...] = out
    else:
        out_refs[0][...] = out


def _moe_combine(pos, gate_w, x1, mod, ln_g, ln_b, ys, split):
    tm = COMBINE_TM
    np_blk = N_PROMPT // tm
    if split:
        out_shape = (jax.ShapeDtypeStruct((N_PROMPT, D_MODEL), F32), jax.ShapeDtypeStruct((N_SAMPLE, D_MODEL), F32))
        out_specs = (pl.BlockSpec((tm, D_MODEL), lambda i, pos: (jnp.minimum(i, np_blk - 1), 0)),
                     pl.BlockSpec((tm, D_MODEL), lambda i, pos: (jnp.maximum(i - np_blk, 0), 0)))
    else:
        out_shape = jax.ShapeDtypeStruct((N_TOK, D_MODEL), F32)
        out_specs = pl.BlockSpec((tm, D_MODEL), lambda i, pos: (i, 0))
    return pl.pallas_call(
        functools.partial(_combine_kernel, split=split),
        out_shape=out_shape,
        grid_spec=pltpu.PrefetchScalarGridSpec(
            num_scalar_prefetch=1,
            grid=(N_TOK // tm,),
            in_specs=[
                pl.BlockSpec((tm, LANES), lambda i, pos: (i, 0)),
                pl.BlockSpec((tm, D_MODEL), lambda i, pos: (i, 0)),
                pl.BlockSpec((1, 6, D_MODEL), lambda i, pos: (_cond_row(i * tm), 0, 0)),
                pl.BlockSpec((1, D_MODEL), lambda i, pos: (0, 0)),
                pl.BlockSpec((1, D_MODEL), lambda i, pos: (0, 0)),
                pl.BlockSpec(memory_space=pl.ANY),
            ],
            out_specs=out_specs,
            scratch_shapes=[pltpu.VMEM((2, 2, tm, N_CHUNKS, LANES), F32), pltpu.SemaphoreType.DMA((2,))],
        ),
        compiler_params=_params("arbitrary"),
        name="moe_combine",
    )(pos, gate_w, x1, mod, ln_g.reshape(1, D_MODEL), ln_b.reshape(1, D_MODEL), ys)


def _moe_ffn(h2_tiles, route_i, gate_w, counts, layer, wg, wu, wd, x1, mod, ln_g, ln_b, split=False):
    pos, tile_expert, n_tiles, fill_rows = _moe_plan(route_i, counts)
    xs = _moe_dispatch(pos, fill_rows, h2_tiles)
    ys = _moe_experts(tile_expert, n_tiles, xs, layer, wg, wu, wd)
    return _moe_combine(pos, gate_w, x1, mod, ln_g, ln_b, ys, split)


DN_GATE_COL0 = DN_CONV_CH
DN_AB_COL0 = DN_CONV_CH + DN_WIDTH
DN_IN = DN_AB_COL0 + LANES
N_DIRS = 2


def _dn_prep_kernel(q_ref, k_ref, v_ref, wq_ref, wk_ref, wv_ref, qo_ref, ko_ref, vo_ref):
    length = q_ref.shape[0]
    row = lax.broadcasted_iota(jnp.int32, (length, DN_DK), 0)

    def conv(x, w):
        prev = jnp.where(row == 0, 0.0, pltpu.roll(x, 1, axis=0))
        nxt = jnp.where(row == length - 1, 0.0, pltpu.roll(x, length - 1, axis=0))
        return _silu(prev * w[0:1] + x * w[1:2] + nxt * w[2:3])

    def l2n(x):
        return x * lax.rsqrt(jnp.sum(x * x, axis=-1, keepdims=True) + 1e-6)

    for h in range(DN_HEADS):
        sl = slice(h * DN_DK, (h + 1) * DN_DK)
        qo_ref[h] = (l2n(conv(q_ref[:, sl], wq_ref[:, sl])) * (DN_DK ** -0.5)).astype(qo_ref.dtype)
        ko_ref[h] = l2n(conv(k_ref[:, sl], wk_ref[:, sl])).astype(ko_ref.dtype)
        vo_ref[h] = conv(v_ref[:, sl], wv_ref[:, sl]).astype(vo_ref.dtype)


def _dn_prep(z1, conv_w, nb, length, row_blk0):
    shp = jax.ShapeDtypeStruct((DN_HEADS, nb * length, DN_DK), BF16)
    blk = lambda part: pl.BlockSpec((length, DN_QK_WIDTH), lambda b: (row_blk0 + b, part))
    wblk = lambda part: pl.BlockSpec((3, DN_QK_WIDTH), lambda b: (0, part))
    oblk = pl.BlockSpec((DN_HEADS, length, DN_DK), lambda b: (0, b, 0))
    return pl.pallas_call(
        _dn_prep_kernel,
        out_shape=(shp, shp, shp),
        grid=(nb,),
        in_specs=[blk(0), blk(1), blk(2), wblk(0), wblk(1), wblk(2)],
        out_specs=(oblk, oblk, oblk),
        compiler_params=_params("arbitrary"),
        name="deltanet_prep",
    )(z1, z1, z1, conv_w, conv_w, conv_w)


def _dn_gates_kernel(ab_ref, alog_ref, dtb_ref, o_ref):
    length = ab_ref.shape[0]
    x = ab_ref[...]
    a = x + dtb_ref[...]
    softplus = jnp.maximum(a, 0.0) + jnp.log1p(jnp.exp(-jnp.abs(a)))
    g = -jnp.exp(alog_ref[...]) * softplus
    beta = jax.nn.sigmoid(x)
    ri = lax.broadcasted_iota(jnp.int32, (DN_CHUNK, DN_CHUNK), 0)
    ci = lax.broadcasted_iota(jnp.int32, (DN_CHUNK, DN_CHUNK), 1)
    tril = (ri >= ci).astype(F32)
    triu = (ri <= ci).astype(F32)
    lane = lax.broadcasted_iota(jnp.int32, (DN_CHUNK, LANES), 1)
    for c in range(length // DN_CHUNK):
        rows = slice(c * DN_CHUNK, (c + 1) * DN_CHUNK)
        gch = g[rows]
        pre = jnp.dot(tril, gch, precision=lax.Precision.HIGHEST, preferred_element_type=F32)
        suf = jnp.dot(triu, gch, precision=lax.Precision.HIGHEST, preferred_element_type=F32)
        gc = jnp.where(lane < DN_HEADS, pre, suf)
        o_ref[rows, :] = jnp.where(lane < N_DIRS * DN_HEADS, gc, beta[rows])


def _dn_gates(z1, a_log, dt_bias, nb, length, row_blk0):
    pad = LANES - N_DIRS * DN_HEADS
    alog = jnp.pad(a_log.reshape(1, -1).astype(F32), ((0, 0), (0, pad)))
    dtb = jnp.pad(dt_bias.reshape(1, -1).astype(F32), ((0, 0), (0, pad)))
    return pl.pallas_call(
        _dn_gates_kernel,
        out_shape=jax.ShapeDtypeStruct((nb * length, LANES), F32),
        grid=(nb,),
        in_specs=[
            pl.BlockSpec((length, LANES), lambda b: (row_blk0 + b, DN_AB_COL0 // LANES)),
            pl.BlockSpec((1, LANES), lambda b: (0, 0)),
            pl.BlockSpec((1, LANES), lambda b: (0, 0)),
        ],
        out_specs=pl.BlockSpec((length, LANES), lambda b: (b, 0)),
        compiler_params=_params("arbitrary"),
        name="deltanet_gates",
    )(z1, alog, dtb)


def _delta_chunk(q, k, v, beta, gcol, grow, s, lower):
    ri = lax.broadcasted_iota(jnp.int32, (DN_CHUNK, DN_CHUNK), 0)
    ci = lax.broadcasted_iota(jnp.int32, (DN_CHUNK, DN_CHUNK), 1)
    incl = (ri >= ci) if lower else (ri <= ci)
    strict = (ri > ci) if lower else (ri < ci)
    decay = jnp.exp(jnp.where(incl, gcol - grow, -jnp.inf))
    kb = k * beta
    kq = _dot_nt(jnp.concatenate([kb, q], axis=0).astype(BF16), k.astype(BF16))
    kk, a_qk = kq[:DN_CHUNK], kq[DN_CHUNK:] * decay
    yield
    tri_l = jnp.where(strict, kk * decay, 0.0)

    def off_block(size):
        same = (ri // (2 * size)) == (ci // (2 * size))
        rpar, cpar = (ri // size) % 2, (ci // size) % 2
        return same & ((rpar == 1) & (cpar == 0) if lower else (rpar == 0) & (cpar == 1))

    p = jnp.where(ri == ci, 1.0, 0.0) - jnp.where(off_block(1), tri_l, 0.0)
    size = 2
    while size < DN_CHUNK:
        p16 = p.astype(BF16)
        pc = _dot(p16, jnp.where(off_block(size), tri_l, 0.0).astype(BF16))
        yield
        p = p - _dot(pc.astype(BF16), p16)
        yield
        size *= 2
    eg = jnp.exp(gcol)
    uw = _dot(p.astype(BF16), jnp.concatenate([v * beta, kb * eg], axis=1).astype(BF16))
    yield
    u, w = uw[:, :DN_DV], uw[:, DN_DV:]
    ws = _dot(jnp.concatenate([w, q * eg], axis=0).astype(BF16), s.astype(BF16))
    v_new, o_inter = u - ws[:DN_CHUNK], ws[DN_CHUNK:]
    yield
    v_new16 = v_new.astype(BF16)
    o = o_inter + _dot(a_qk.astype(BF16), v_new16)
    g_last = gcol[DN_CHUNK - 1:DN_CHUNK] if lower else gcol[0:1]
    kd = k * jnp.exp(g_last - gcol)
    s_new = s * jnp.exp(g_last) + _dot_tn(kd.astype(BF16), v_new16)
    return o, s_new


def _run_interleaved(chains):
    results = [None] * len(chains)
    active = list(enumerate(chains))
    while active:
        still = []
        for idx, gen in active:
            try:
                next(gen)
                still.append((idx, gen))
            except StopIteration as stop:
                results[idx] = stop.value
        active = still
    return results


def _dn_scan_kernel(*refs, has_init, want_final):
    (qf_ref, kf_ref, vf_ref, gf_ref, qb_ref, kb_ref, vb_ref, gb_ref), rest = refs[:8], refs[8:]
    if has_init:
        s0_ref, rest = rest[0], rest[1:]
    of_ref, ob_ref = rest[0], rest[1]
    rest = rest[2:]
    if want_final:
        sf_ref, rest = rest[0], rest[1:]
    s_ref = rest[0]
    c = pl.program_id(1)
    nc = pl.num_programs(1)

    @pl.when(c == 0)
    def _():
        for d in range(N_DIRS):
            for h in range(DN_HEADS):
                if has_init:
                    s_ref[d * DN_HEADS + h] = s0_ref[0, d, h]
                else:
                    s_ref[d * DN_HEADS + h] = jnp.zeros((DN_DK, DN_DV), F32)

    dirs = ((qf_ref, kf_ref, vf_ref, gf_ref, of_ref, True), (qb_ref, kb_ref, vb_ref, gb_ref, ob_ref, False))
    chains = []
    for d, (q_ref, k_ref, v_ref, g_ref, o_ref, lower) in enumerate(dirs):
        gates = g_ref[...]
        gates_t = gates.T
        for h in range(DN_HEADS):
            j = d * DN_HEADS + h
            jb = N_DIRS * DN_HEADS + j
            chains.append(_delta_chunk(q_ref[h].astype(F32), k_ref[h].astype(F32), v_ref[h].astype(F32),
                                       gates[:, jb:jb + 1], gates[:, j:j + 1], gates_t[j:j + 1, :],
                                       s_ref[j], lower))
    results = _run_interleaved(chains)
    for d, (_, _, _, _, o_ref, _) in enumerate(dirs):
        for h in range(DN_HEADS):
            j = d * DN_HEADS + h
            o, s_new = results[j]
            o_ref[:, h * DN_DV:(h + 1) * DN_DV] = o.astype(o_ref.dtype)
            s_ref[j] = s_new

    if want_final:
        @pl.when(c == nc - 1)
        def _():
            for d in range(N_DIRS):
                for h in range(DN_HEADS):
                    sf_ref[0, d, h] = s_ref[d * DN_HEADS + h]


def _dn_scan(qn, kn, vn, gates, s0, nb, length, want_final):
    nc = length // DN_CHUNK
    has_init = s0 is not None
    fwd = lambda b, c: (0, b * nc + c, 0)
    bwd = lambda b, c: (0, b * nc + (nc - 1 - c), 0)
    hm = lambda im: pl.BlockSpec((DN_HEADS, DN_CHUNK, DN_DK), im)
    in_specs = [hm(fwd), hm(fwd), hm(fwd), pl.BlockSpec((DN_CHUNK, LANES), lambda b, c: (b * nc + c, 0)),
                hm(bwd), hm(bwd), hm(bwd), pl.BlockSpec((DN_CHUNK, LANES), lambda b, c: (b * nc + (nc - 1 - c), 0))]
    args = [qn, kn, vn, gates, qn, kn, vn, gates]
    state_blk = pl.BlockSpec((1, N_DIRS, DN_HEADS, DN_DK, DN_DV), lambda b, c: (b, 0, 0, 0, 0))
    if has_init:
        in_specs.append(state_blk)
        args.append(s0)
    o_shape = jax.ShapeDtypeStruct((nb * length, DN_WIDTH), BF16)
    out_shape = [o_shape, o_shape]
    out_specs = [pl.BlockSpec((DN_CHUNK, DN_WIDTH), lambda b, c: (b * nc + c, 0)),
                 pl.BlockSpec((DN_CHUNK, DN_WIDTH), lambda b, c: (b * nc + (nc - 1 - c), 0))]
    if want_final:
        out_shape.append(jax.ShapeDtypeStruct((nb, N_DIRS, DN_HEADS, DN_DK, DN_DV), F32))
        out_specs.append(state_blk)
    return pl.pallas_call(
        functools.partial(_dn_scan_kernel, has_init=has_init, want_final=want_final),
        out_shape=tuple(out_shape),
        grid=(nb, nc),
        in_specs=in_specs,
        out_specs=tuple(out_specs),
        scratch_shapes=[pltpu.VMEM((N_DIRS * DN_HEADS, DN_DK, DN_DV), F32)],
        compiler_params=_params("arbitrary", "arbitrary"),
        name="deltanet_scan",
    )(*args)


DN_MERGE_TM = 256


def _dn_merge_kernel(of_ref, ob_ref, gate_ref, g_ref, o_ref):
    for h in range(DN_HEADS):
        sl = slice(h * DN_DV, (h + 1) * DN_DV)
        o = of_ref[:, sl].astype(F32) + ob_ref[:, sl].astype(F32)
        o = o * lax.rsqrt(jnp.mean(o * o, axis=-1, keepdims=True) + EPS) * g_ref[...]
        o_ref[:, sl] = (o * _silu(gate_ref[:, sl])).astype(o_ref.dtype)


def _dn_merge(o_f, o_b, z1, onorm_g, n_rows, row0):
    tm = DN_MERGE_TM
    blk = pl.BlockSpec((tm, DN_WIDTH), lambda i: (i, 0))
    return pl.pallas_call(
        _dn_merge_kernel,
        out_shape=jax.ShapeDtypeStruct((n_rows, DN_WIDTH), BF16),
        grid=(n_rows // tm,),
        in_specs=[blk, blk,
                  pl.BlockSpec((tm, DN_WIDTH), lambda i: (row0 // tm + i, DN_GATE_COL0 // DN_WIDTH)),
                  pl.BlockSpec((1, DN_DV), lambda i: (0, 0))],
        out_specs=blk,
        compiler_params=_params("arbitrary"),
        name="deltanet_merge",
    )(o_f, o_b, z1, onorm_g.reshape(1, DN_DV))


def _deltanet_stream(z1, conv_w, a_log, dt_bias, onorm_g, s0, nb, length, row0, want_final):
    row_blk0 = row0 // length
    qn, kn, vn = _dn_prep(z1, conv_w, nb, length, row_blk0)
    gates = _dn_gates(z1, a_log, dt_bias, nb, length, row_blk0)
    outs = _dn_scan(qn, kn, vn, gates, s0, nb, length, want_final)
    merged = _dn_merge(outs[0], outs[1], z1, onorm_g, nb * length, row0)
    return merged, (outs[2] if want_final else None)


def kernel(x_prompt, x_sample, cache_nat_k, cache_nat_v, cache_diff_k, cache_diff_v, state_delta, c, c_ctx, w_in_even, w_out_even, nat_bias, diff_lam, diff_subln, w_in_odd, conv_odd, a_log_odd, dt_bias_odd, onorm_odd, w_out_odd, ada_w, ada_b, ln_g, ln_b, router_w, router_b, moe_wg, moe_wu, moe_wd):
    x_parts = (x_prompt.reshape(N_PROMPT, D_MODEL), x_sample.reshape(N_SAMPLE, D_MODEL))
    cond =jnp.concatenate([c_ctx[None], c, jnp.zeros((N_COND - 1 - DEC_BATCH, D_MODEL), F32)], axis=0)
    mod = _ada_modulation(cond, ada_w, ada_b).reshape(DEPTH, N_COND, 6, D_MODEL)
    router_w_pad = jnp.pad(router_w, ((0, 0), (0, LANES - N_EXPERTS)))

    lam_init = 0.8 - 0.6 * math.exp(-0.3 * 0)
    z = _input_projection(x_parts, mod[0], w_in_even[0].astype(BF16))
    new_nat_k = z[:N_PROMPT, _NAT_K0:_NAT_V0].reshape(BATCH, 1, SEQ, NAT_HEADS, HEAD_DIM)
    new_nat_v = z[:N_PROMPT, _NAT_V0:_DIFF_Q0].reshape(BATCH, 1, SEQ, NAT_HEADS, HEAD_DIM)
    new_diff_k = z[:N_PROMPT, _DIFF_K0:_DIFF_V0].reshape(BATCH, 1, SEQ, DIFF_HEADS, 2, HEAD_DIM)
    new_diff_v = z[:N_PROMPT, _DIFF_V0:].reshape(BATCH, 1, SEQ, DIFF_HEADS, 2 * HEAD_DIM)
    o_ctx = _context_attention(z, diff_lam[0], diff_subln[0], lam_init)
    o_nat = _nat_latent_attention(z, cache_nat_k[:, 0].reshape(DEC_BATCH, PAST_LEN, NAT_WIDTH),
                                  cache_nat_v[:, 0].reshape(DEC_BATCH, PAST_LEN, NAT_WIDTH),
                                  _nat_bias_table(nat_bias[0]))
    o_diff = _diff_latent_attention(z, cache_diff_k[:, 0].reshape(DEC_BATCH, PAST_LEN, DIFF_WIDTH),
                                    cache_diff_v[:, 0].reshape(DEC_BATCH, PAST_LEN, DIFF_WIDTH),
                                    diff_lam[0], diff_subln[0], lam_init)
    x1, *routed = _output_projection((o_ctx, o_nat, o_diff), x_parts, mod[0], w_out_even[0].astype(BF16),
                                     ln_g[0, 0], ln_b[0, 0], router_w_pad, router_b)
    x = _moe_ffn(*routed, 0, moe_wg, moe_wu, moe_wd, x1, mod[0], ln_g[0, 1], ln_b[0, 1])

    w1 = w_in_odd[0]
    w1 = jnp.concatenate([w1[:, :DN_CONV_CH], w1[:, DN_CONV_CH + 4 * DN_HEADS:],
                          w1[:, DN_CONV_CH:DN_CONV_CH + 4 * DN_HEADS],
                          jnp.zeros((D_MODEL, LANES - 4 * DN_HEADS), F32)], axis=1).astype(BF16)
    z1 = _input_projection((x,), mod[1], w1, tm=256)
    o_p, new_state = _deltanet_stream(z1, conv_odd[0], a_log_odd[0], dt_bias_odd[0], onorm_odd[0],
                                      None, BATCH, SEQ, 0, True)
    o_s, _ = _deltanet_stream(z1, conv_odd[0], a_log_odd[0], dt_bias_odd[0], onorm_odd[0],
                              state_delta[:, 0], DEC_BATCH, DEC_SEQ, N_PROMPT, False)
    x1, *routed = _output_projection((o_p, o_s), (x,), mod[1], w_out_odd[0].astype(BF16), ln_g[1, 0], ln_b[1, 0],
                                     router_w_pad, router_b)
    y_prompt, y_sample = _moe_ffn(*routed, 1, moe_wg, moe_wu, moe_wd, x1, mod[1], ln_g[1, 1], ln_b[1, 1],
                                  split=True)
    y_prompt = y_prompt.reshape(BATCH, SEQ, D_MODEL)
    y_sample = y_sample.reshape(DEC_BATCH, DEC_SEQ, D_MODEL)
    return (y_prompt, y_sample, new_nat_k, new_nat_v, new_diff_k, new_diff_v, new_state[:, None])
```

```python
import functools
import math

import jax
import jax.numpy as jnp
import numpy as np
from jax import lax
from jax.experimental import pallas as pl
from jax.experimental.pallas import tpu as pltpu

F32 = jnp.float32
BF16 = jnp.bfloat16

D_MODEL = 1024
BATCH = 16
SEQ = 256
DEC_BATCH = 8
DEC_SEQ = 1024
PAST_LEN = 256
GRID_W = 64
N_ROWS = DEC_SEQ // GRID_W

HEAD_DIM = 64
NAT_HEADS = 8
NAT_WIN_H = 8
NAT_WIN_W = 16
DIFF_HEADS = 4
NAT_WIDTH = NAT_HEADS * HEAD_DIM
DIFF_WIDTH = DIFF_HEADS * 2 * HEAD_DIM
EVEN_IN = 3 * NAT_WIDTH + 3 * DIFF_WIDTH
ROPE_THETA = 10000.0

DN_HEADS = 8
DN_DK = 128
DN_DV = 128
DN_QK_WIDTH = DN_HEADS * DN_DK
DN_WIDTH = DN_HEADS * DN_DV
DN_CONV_CH = 2 * DN_QK_WIDTH + DN_WIDTH
DN_CHUNK = 64

N_EXPERTS = 16
N_GROUPS = 4
EXPERTS_PER_GROUP = N_EXPERTS // N_GROUPS
EXPERT_FF = 512

DEPTH = 2
ALPHA = (2 * DEPTH) ** 0.25
EPS = 1e-5

N_PROMPT = BATCH * SEQ
N_SAMPLE = DEC_BATCH * DEC_SEQ
N_TOK = N_PROMPT + N_SAMPLE
N_COND = 16

VMEM_LIMIT = 56 * 1024 * 1024
LANES = 128
N_CHUNKS = D_MODEL // LANES


def _params(*sem):
    return pltpu.CompilerParams(dimension_semantics=sem, vmem_limit_bytes=VMEM_LIMIT)


def _cond_row(row0):
    return jnp.where(row0 < N_PROMPT, 0, 1 + (row0 - N_PROMPT) // DEC_SEQ)


def _silu(x):
    return x * jax.nn.sigmoid(x)


def _layer_norm(r, g, b):
    mu = jnp.mean(r, axis=-1, keepdims=True)
    xc = r - mu
    var = jnp.mean(xc * xc, axis=-1, keepdims=True)
    return xc * lax.rsqrt(var + EPS) * g + b


def _dot(a, b):
    return jnp.dot(a, b, preferred_element_type=F32)


def _dot_nt(a, b):
    return lax.dot_general(a, b, (((1,), (1,)), ((), ())), preferred_element_type=F32)


def _dot_tn(a, b):
    return lax.dot_general(a, b, (((0,), (0,)), ((), ())), preferred_element_type=F32)


ADA_TN = 1536


def _ada_kernel(c_ref, w_ref, b_ref, o_ref):
    s = _silu(c_ref[...])
    o_ref[0] = _dot(s.astype(BF16), w_ref[0].astype(BF16)) + b_ref[0]


def _ada_modulation(cond, ada_w, ada_b):
    n = 6 * D_MODEL
    return pl.pallas_call(
        _ada_kernel,
        out_shape=jax.ShapeDtypeStruct((DEPTH, N_COND, n), F32),
        grid=(DEPTH, n // ADA_TN),
        in_specs=[
            pl.BlockSpec((N_COND, D_MODEL), lambda l, j: (0, 0)),
            pl.BlockSpec((1, D_MODEL, ADA_TN), lambda l, j: (l, 0, j)),
            pl.BlockSpec((1, 1, ADA_TN), lambda l, j: (l, 0, j)),
        ],
        out_specs=pl.BlockSpec((1, N_COND, ADA_TN), lambda l, j: (l, 0, j)),
        compiler_params=_params("arbitrary", "arbitrary"),
        name="ada_modulation",
    )(cond, ada_w, ada_b.reshape(DEPTH, 1, n))


PROJ_TM = 512


def _stream_specs(widths, tm):
    if len(widths) == 1:
        return [pl.BlockSpec((tm, widths[0]), lambda i: (i, 0))]
    n_ctx = N_PROMPT // tm
    return ([pl.BlockSpec((tm, widths[0]), lambda i: (jnp.minimum(i, n_ctx - 1), 0))]
            + [pl.BlockSpec((tm, w), lambda i: (jnp.maximum(i - n_ctx, 0), 0)) for w in widths[1:]])


def _stream_tile(refs, tm):
    if len(refs) == 1:
        return refs[0][...]
    latent = [r[...] for r in refs[1:]]
    latent = latent[0] if len(latent) == 1 else jnp.concatenate(latent, axis=1)
    return jnp.where(pl.program_id(0) * tm < N_PROMPT, refs[0][...], latent)


def _inproj_kernel(*refs, n_x):
    x_refs, (mod_ref, w_ref, o_ref), tail_refs = refs[:n_x], refs[n_x:n_x + 3], refs[n_x + 3:]
    m = mod_ref[0]
    h = _stream_tile(x_refs, o_ref.shape[0]) * (1.0 + m[1:2]) + m[0:1]
    z = _dot(h.astype(BF16), w_ref[...])
    n_main = o_ref.shape[1]
    o_ref[...] = z[:, :n_main].astype(o_ref.dtype)
    if tail_refs:
        tail_refs[0][...] = z[:, n_main:]


def _input_projection(x_parts, mod, w_bf16, tm=PROJ_TM, f32_tail=0):
    n = w_bf16.shape[1] - f32_tail
    out_shape = [jax.ShapeDtypeStruct((N_TOK, n), BF16 if f32_tail else F32)]
    out_specs = [pl.BlockSpec((tm, n), lambda i: (i, 0))]
    if f32_tail:
        out_shape.append(jax.ShapeDtypeStruct((N_TOK, f32_tail), F32))
        out_specs.append(pl.BlockSpec((tm, f32_tail), lambda i: (i, 0)))
    outs = pl.pallas_call(
        functools.partial(_inproj_kernel, n_x=len(x_parts)),
        out_shape=tuple(out_shape),
        grid=(N_TOK // tm,),
        in_specs=_stream_specs([a.shape[1] for a in x_parts], tm) + [
            pl.BlockSpec((1, 6, D_MODEL), lambda i: (_cond_row(i * tm), 0, 0)),
            pl.BlockSpec((D_MODEL, n + f32_tail), lambda i: (0, 0)),
        ],
        out_specs=tuple(out_specs),
        compiler_params=_params("arbitrary"),
        name="input_projection",
    )(*x_parts, mod, w_bf16)
    return outs if f32_tail else outs[0]


_NAT_Q0, _NAT_K0, _NAT_V0 = 0, NAT_WIDTH, 2 * NAT_WIDTH
_DIFF_Q0 = 3 * NAT_WIDTH
_DIFF_K0 = _DIFF_Q0 + DIFF_WIDTH
_DIFF_V0 = _DIFF_K0 + DIFF_WIDTH
ATTN_SCALE = HEAD_DIM ** -0.5


def _diff_lambda(lam_ref, lam_init):
    lp = lam_ref[...]
    return (jnp.exp(jnp.sum(lp[0:1] * lp[1:2], axis=-1, keepdims=True))
            - jnp.exp(jnp.sum(lp[2:3] * lp[3:4], axis=-1, keepdims=True)) + lam_init)


def _scaled_q(q):
    assert math.log2(HEAD_DIM) % 2 == 0
    return (q * ATTN_SCALE).astype(BF16)


def _with_ones(v):
    width = v.shape[1] if v.shape[1] % LANES == 0 else LANES - v.shape[1] % LANES
    return jnp.concatenate([v, jnp.ones((v.shape[0], width), v.dtype)], axis=1)


def _sub_norm(o, g, lam_init):
    ms = jnp.mean(o * o, axis=-1, keepdims=True)
    return o * lax.rsqrt(ms + EPS) * g * (1.0 - lam_init)


def _ctx_attn_kernel(z_ref, lam_ref, subg_ref, o_ref, *, lam_init):
    lam = _diff_lambda(lam_ref, lam_init)

    def softmax_av(q0, k0, v0, v_width):
        q = _scaled_q(z_ref[:, q0:q0 + HEAD_DIM])
        k = z_ref[:, k0:k0 + HEAD_DIM].astype(BF16)
        s = _dot_nt(q, k)
        yield
        e = jnp.exp(s - jnp.max(s, axis=-1, keepdims=True))
        yield
        av = _dot(e.astype(BF16), _with_ones(z_ref[:, v0:v0 + v_width].astype(BF16)))
        yield
        return av[:, :v_width] / av[:, v_width:v_width + 1]

    chains = [softmax_av(_NAT_Q0 + h * HEAD_DIM, _NAT_K0 + h * HEAD_DIM, _NAT_V0 + h * HEAD_DIM, HEAD_DIM)
              for h in range(NAT_HEADS)]
    chains += [softmax_av(_DIFF_Q0 + i * HEAD_DIM, _DIFF_K0 + i * HEAD_DIM,
                          _DIFF_V0 + (i // 2) * 2 * HEAD_DIM, 2 * HEAD_DIM)
               for i in range(2 * DIFF_HEADS)]
    outs = _run_interleaved(chains)
    for h in range(NAT_HEADS):
        o_ref[:, h * HEAD_DIM:(h + 1) * HEAD_DIM] = outs[h].astype(o_ref.dtype)
    for h in range(DIFF_HEADS):
        c = NAT_WIDTH + h * 2 * HEAD_DIM
        o = outs[NAT_HEADS + 2 * h] - lam * outs[NAT_HEADS + 2 * h + 1]
        o_ref[:, c:c + 2 * HEAD_DIM] = _sub_norm(o, subg_ref[...], lam_init).astype(o_ref.dtype)


def _context_attention(z, lam_p, subln_g, lam_init):
    return pl.pallas_call(
        functools.partial(_ctx_attn_kernel, lam_init=lam_init),
        out_shape=jax.ShapeDtypeStruct((N_PROMPT, D_MODEL), BF16),
        grid=(BATCH,),
        in_specs=[
            pl.BlockSpec((SEQ, EVEN_IN), lambda b: (b, 0)),
            pl.BlockSpec((4, HEAD_DIM), lambda b: (0, 0)),
            pl.BlockSpec((1, 2 * HEAD_DIM), lambda b: (0, 0)),
        ],
        out_specs=pl.BlockSpec((SEQ, D_MODEL), lambda b: (b, 0)),
        compiler_params=_params("arbitrary"),
        name="context_attention",
    )(z, lam_p, subln_g.reshape(1, 2 * HEAD_DIM))


NAT_TQ = 512
NAT_KEYS = 12 * GRID_W
_SAMPLE_BLK0 = N_PROMPT // DEC_SEQ


def _nat_bias_table(rel_bias):
    rows = N_ROWS
    kh = min(NAT_WIN_H, rows)
    kw = NAT_WIN_W
    col = np.arange(GRID_W)
    c_start = np.clip(col - kw // 2, 0, GRID_W - kw)
    col_ok = (col[None, :] >= c_start[:, None]) & (col[None, :] < c_start[:, None] + kw)
    dc = np.clip(col[None, :] - col[:, None], 1 - kw, kw - 1) + (NAT_WIN_W - 1)
    col_onehot = ((dc[None] == np.arange(2 * kw - 1)[:, None, None]) & col_ok[None]).astype(np.float32)
    span = NAT_KEYS // GRID_W
    row_onehot = np.zeros((2 * NAT_WIN_H - 1, rows, span), np.float32)
    for r in range(rows):
        start = min(max(r - kh // 2, 0), rows - kh)
        a0 = _nat_key0(r * GRID_W // NAT_TQ) // GRID_W
        assert a0 <= start and start + kh <= a0 + span
        for a in range(start, start + kh):
            row_onehot[a - r + NAT_WIN_H - 1, r, a - a0] = 1.0
    hi = lax.Precision.HIGHEST
    by_row = jnp.einsum('hed,era->hrad', rel_bias.astype(F32), row_onehot, precision=hi)
    table = jnp.einsum('hrad,dqk->hrqak', by_row, col_onehot, precision=hi)
    inside = (row_onehot.sum(0) > 0)[:, None, :, None] & col_ok[None, :, None, :]
    return jnp.where(inside, table, -jnp.inf).reshape(NAT_HEADS, DEC_SEQ, NAT_KEYS)


def _nat_key0(qt):
    half = DEC_SEQ // NAT_TQ // 2
    return (qt >= half) * (DEC_SEQ - NAT_KEYS)


def _nat_latent_kernel(q_ref, k_ref, v_ref, ck_ref, cv_ref, bias_ref, o_ref):
    key0 = pl.multiple_of(_nat_key0(pl.program_id(1)).astype(jnp.int32), DEC_SEQ - NAT_KEYS)
    keys = pl.ds(key0, NAT_KEYS)

    def head(j):
        sl = slice(j * HEAD_DIM, (j + 1) * HEAD_DIM)
        q = _scaled_q(q_ref[:, sl])
        s_loc = _dot_nt(q, k_ref[keys, sl].astype(BF16)) + bias_ref[j]
        s_ctx = _dot_nt(q, ck_ref[0, :, sl].astype(BF16))
        yield
        m = jnp.maximum(jnp.max(s_loc, axis=-1, keepdims=True), jnp.max(s_ctx, axis=-1, keepdims=True))
        yield
        e_loc = jnp.exp(s_loc - m)
        e_ctx = jnp.exp(s_ctx - m)
        yield
        o = (_dot(e_loc.astype(BF16), _with_ones(v_ref[keys, sl].astype(BF16)))
             + _dot(e_ctx.astype(BF16), _with_ones(cv_ref[0, :, sl].astype(BF16))))
        yield
        return (o[:, :HEAD_DIM] / o[:, HEAD_DIM:HEAD_DIM + 1]).astype(o_ref.dtype)

    outs = _run_interleaved([head(j) for j in range(2)])
    for j in range(2):
        o_ref[:, j * HEAD_DIM:(j + 1) * HEAD_DIM] = outs[j]


def _nat_latent_attention(z, ck, cv, bias):
    nq = DEC_SEQ // NAT_TQ
    qblk0 = N_PROMPT // NAT_TQ
    pair = 2 * HEAD_DIM
    return pl.pallas_call(
        _nat_latent_kernel,
        out_shape=jax.ShapeDtypeStruct((N_SAMPLE, NAT_WIDTH), BF16),
        grid=(NAT_HEADS // 2, nq, DEC_BATCH),
        in_specs=[
            pl.BlockSpec((NAT_TQ, pair), lambda hp, qt, b: (qblk0 + b * nq + qt, _NAT_Q0 // pair + hp)),
            pl.BlockSpec((DEC_SEQ, pair), lambda hp, qt, b: (_SAMPLE_BLK0 + b, _NAT_K0 // pair + hp)),
            pl.BlockSpec((DEC_SEQ, pair), lambda hp, qt, b: (_SAMPLE_BLK0 + b, _NAT_V0 // pair + hp)),
            pl.BlockSpec((1, PAST_LEN, pair), lambda hp, qt, b: (b, 0, hp)),
            pl.BlockSpec((1, PAST_LEN, pair), lambda hp, qt, b: (b, 0, hp)),
            pl.BlockSpec((2, NAT_TQ, NAT_KEYS), lambda hp, qt, b: (hp, qt, 0)),
        ],
        out_specs=pl.BlockSpec((NAT_TQ, pair), lambda hp, qt, b: (b * nq + qt, hp)),
        compiler_params=_params("arbitrary", "arbitrary", "arbitrary"),
        name="nat_latent_attention",
    )(z, z, z, ck, cv, bias)


DIFF_TQ = 512


def _rope_tables():
    nf = HEAD_DIM // 4
    t = jnp.arange(DEC_SEQ)
    inv = ROPE_THETA ** (-jnp.arange(nf, dtype=F32) / nf)
    ang_r = (t // GRID_W).astype(F32)[:, None] * inv
    ang_c = (t % GRID_W).astype(F32)[:, None] * inv
    cos = jnp.concatenate([jnp.cos(ang_r)] * 2 + [jnp.cos(ang_c)] * 2, axis=-1)
    sin = jnp.concatenate([-jnp.sin(ang_r), jnp.sin(ang_r), -jnp.sin(ang_c), jnp.sin(ang_c)], axis=-1)
    return jnp.tile(cos, (1, 2)), jnp.tile(sin, (1, 2))


def _rope(x, cos, sin):
    nf = HEAD_DIM // 4
    lane = lax.broadcasted_iota(jnp.int32, x.shape, 1)
    upper = (lane // nf) % 2 == 1
    partner = jnp.where(upper, pltpu.roll(x, nf, axis=1), pltpu.roll(x, x.shape[1] - nf, axis=1))
    return x * cos + partner * sin


def _diff_latent_kernel(q_ref, k_ref, v_ref, ck_ref, cv_ref, cos_ref, sin_ref, lam_ref, subg_ref,
                        o_ref, kr_ref, *, lam_init):
    qt = pl.program_id(2)

    @pl.when(qt == 0)
    def _():
        kr_ref[...] = _rope(k_ref[...], cos_ref[...], sin_ref[...]).astype(BF16)

    lam = _diff_lambda(lam_ref, lam_init)
    row0 = pl.multiple_of(qt * DIFF_TQ, DIFF_TQ)
    q = _scaled_q(_rope(q_ref[...], cos_ref[pl.ds(row0, DIFF_TQ), :], sin_ref[pl.ds(row0, DIFF_TQ), :]))
    v_loc = _with_ones(v_ref[...].astype(BF16))
    v_ctx = _with_ones(cv_ref[0].astype(BF16))
    width = 2 * HEAD_DIM

    def softmax_av(j):
        sl = slice(j * HEAD_DIM, (j + 1) * HEAD_DIM)
        s_ctx = _dot_nt(q[:, sl], ck_ref[0, :, sl].astype(BF16))
        s_loc = _dot_nt(q[:, sl], kr_ref[:, sl])
        yield
        m = jnp.maximum(jnp.max(s_loc, axis=-1, keepdims=True), jnp.max(s_ctx, axis=-1, keepdims=True))
        yield
        e_loc = jnp.exp(s_loc - m)
        e_ctx = jnp.exp(s_ctx - m)
        yield
        av = _dot(e_loc.astype(BF16), v_loc) + _dot(e_ctx.astype(BF16), v_ctx)
        yield
        return av[:, :width] / av[:, width:width + 1]

    parts = _run_interleaved([softmax_av(j) for j in range(2)])
    o = parts[0] - lam * parts[1]
    o_ref[...] = _sub_norm(o, subg_ref[...], lam_init).astype(o_ref.dtype)


def _diff_latent_attention(z, ck, cv, lam_p, subln_g, lam_init):
    nq = DEC_SEQ // DIFF_TQ
    qblk0 = N_PROMPT // DIFF_TQ
    w = 2 * HEAD_DIM
    cos, sin = _rope_tables()
    return pl.pallas_call(
        functools.partial(_diff_latent_kernel, lam_init=lam_init),
        out_shape=jax.ShapeDtypeStruct((N_SAMPLE, DIFF_WIDTH), BF16),
        grid=(DEC_BATCH, DIFF_HEADS, nq),
        in_specs=[
            pl.BlockSpec((DIFF_TQ, w), lambda b, h, qt: (qblk0 + b * nq + qt, _DIFF_Q0 // w + h)),
            pl.BlockSpec((DEC_SEQ, w), lambda b, h, qt: (_SAMPLE_BLK0 + b, _DIFF_K0 // w + h)),
            pl.BlockSpec((DEC_SEQ, w), lambda b, h, qt: (_SAMPLE_BLK0 + b, _DIFF_V0 // w + h)),
            pl.BlockSpec((1, PAST_LEN, w), lambda b, h, qt: (b, 0, h)),
            pl.BlockSpec((1, PAST_LEN, w), lambda b, h, qt: (b, 0, h)),
            pl.BlockSpec((DEC_SEQ, w), lambda b, h, qt: (0, 0)),
            pl.BlockSpec((DEC_SEQ, w), lambda b, h, qt: (0, 0)),
            pl.BlockSpec((4, HEAD_DIM), lambda b, h, qt: (0, 0)),
            pl.BlockSpec((1, w), lambda b, h, qt: (0, 0)),
        ],
        out_specs=pl.BlockSpec((DIFF_TQ, w), lambda b, h, qt: (b * nq + qt, h)),
        scratch_shapes=[pltpu.VMEM((DEC_SEQ, w), BF16)],
        compiler_params=_params("arbitrary", "arbitrary", "arbitrary"),
        name="diff_latent_attention",
    )(z, z, z, ck, cv, cos, sin, lam_p, subln_g.reshape(1, w))


OUT_TM = 256


def _route(logits_t, rb_col):
    sc = jax.nn.sigmoid(logits_t)
    bi = sc + rb_col
    srow = [sc[e:e + 1] for e in range(N_EXPERTS)]
    brow = [bi[e:e + 1] for e in range(N_EXPERTS)]
    n = EXPERTS_PER_GROUP
    gscore = []
    for g in range(N_GROUPS):
        v = brow[g * n:(g + 1) * n]
        best = None
        for i in range(n):
            for j in range(i + 1, n):
                s = v[i] + v[j]
                best = s if best is None else jnp.maximum(best, s)
        gscore.append(best)
    sel = jnp.zeros_like(gscore[0], dtype=jnp.int32)
    best = gscore[0]
    for g in range(1, N_GROUPS):
        better = gscore[g] > best
        sel = jnp.where(better, g, sel)
        best = jnp.where(better, gscore[g], best)

    def pick_group(rows, i):
        out = rows[(N_GROUPS - 1) * n + i]
        for g in range(N_GROUPS - 2, -1, -1):
            out = jnp.where(sel == g, rows[g * n + i], out)
        return out

    bv = [pick_group(brow, i) for i in range(n)]
    sv = [pick_group(srow, i) for i in range(n)]
    i1 = jnp.zeros_like(sel)
    m1, w1 = bv[0], sv[0]
    for i in range(1, n):
        better = bv[i] > m1
        i1 = jnp.where(better, i, i1)
        m1 = jnp.where(better, bv[i], m1)
        w1 = jnp.where(better, sv[i], w1)
    i2 = jnp.zeros_like(sel)
    m2 = jnp.full_like(m1, -jnp.inf)
    w2 = jnp.zeros_like(w1)
    for i in range(n):
        better = (i1 != i) & (bv[i] > m2)
        i2 = jnp.where(better, i, i2)
        m2 = jnp.where(better, bv[i], m2)
        w2 = jnp.where(better, sv[i], w2)
    tot = w1 + w2
    return sel * n + i1, sel * n + i2, w1 / tot, w2 / tot


def _store_token_tiles(ref, value):
    for c in range(N_CHUNKS):
        ref[:, c, :] = value[:, c * LANES:(c + 1) * LANES]


def _load_token_tiles(ref):
    return jnp.concatenate([ref[:, c, :] for c in range(N_CHUNKS)], axis=1)


def _outproj_kernel(*refs, n_o, n_x):
    o_refs, x_refs, refs = refs[:n_o], refs[n_o:n_o + n_x], refs[n_o + n_x:]
    mod_ref, w_ref, lng_ref, lnb_ref, rw_ref, rb_ref, x1_ref, h2_ref, ri_ref, rw_out_ref, cnt_ref = refs
    tm = x1_ref.shape[0]
    m = mod_ref[0]
    y = _dot(_stream_tile(o_refs, tm), w_ref[...])
    x1 = _layer_norm(ALPHA * _stream_tile(x_refs, tm) + m[2:3] * y, lng_ref[...], lnb_ref[...])
    x1_ref[...] = x1
    h2 = x1 * (1.0 + m[4:5]) + m[3:4]
    _store_token_tiles(h2_ref, h2)
    rw = rw_ref[...]
    h_hi, rw_hi = h2.astype(BF16), rw.astype(BF16)
    h_lo = (h2 - h_hi.astype(F32)).astype(BF16)
    rw_lo = (rw - rw_hi.astype(F32)).astype(BF16)
    logits = _dot(h_hi, rw_hi) + (_dot(h_lo, rw_hi) + _dot(h_hi, rw_lo))
    e1, e2, w1, w2 = _route(logits.T[:N_EXPERTS], rb_ref[...])
    eid = lax.broadcasted_iota(jnp.int32, (N_EXPERTS, tm), 0)
    onehot = (eid == e1) | (eid == e2)
    ti = lax.broadcasted_iota(jnp.int32, (tm, tm), 0)
    tj = lax.broadcasted_iota(jnp.int32, (tm, tm), 1)
    earlier = jnp.where(ti < tj, 1.0, 0.0).astype(BF16)
    rank = _dot(jnp.where(onehot, 1.0, 0.0).astype(BF16), earlier)
    r1 = jnp.sum(jnp.where(eid == e1, rank, 0.0), axis=0, keepdims=True).astype(jnp.int32)
    r2 = jnp.sum(jnp.where(eid == e2, rank, 0.0), axis=0, keepdims=True).astype(jnp.int32)
    ri_ref[...] = jnp.concatenate([e1, e2, r1, r2, jnp.zeros((4, tm), jnp.int32)], axis=0)
    wt = jnp.concatenate([w1, w2, jnp.zeros((LANES - 2, tm), F32)], axis=0)
    rw_out_ref[...] = wt.T
    cnt = jnp.sum(jnp.where(onehot, 1.0, 0.0), axis=1, keepdims=True)
    cnt_ref[0] = jnp.broadcast_to(cnt, (N_EXPERTS, LANES)).astype(jnp.int32)


def _output_projection(o_parts, x_parts, mod, w_bf16, ln_g, ln_b, router_w_pad, router_b, tm=OUT_TM):
    nt = N_TOK // tm
    return pl.pallas_call(
        functools.partial(_outproj_kernel, n_o=len(o_parts), n_x=len(x_parts)),
        out_shape=(jax.ShapeDtypeStruct((N_TOK, D_MODEL), F32),
                   jax.ShapeDtypeStruct((N_TOK, N_CHUNKS, LANES), F32),
                   jax.ShapeDtypeStruct((8, N_TOK), jnp.int32),
                   jax.ShapeDtypeStruct((N_TOK, LANES), F32),
                   jax.ShapeDtypeStruct((nt, N_EXPERTS, LANES), jnp.int32)),
        grid=(nt,),
        in_specs=_stream_specs([a.shape[1] for a in o_parts], tm) + _stream_specs([a.shape[1] for a in x_parts], tm) + [
            pl.BlockSpec((1, 6, D_MODEL), lambda i: (_cond_row(i * tm), 0, 0)),
            pl.BlockSpec((D_MODEL, D_MODEL), lambda i: (0, 0)),
            pl.BlockSpec((1, D_MODEL), lambda i: (0, 0)),
            pl.BlockSpec((1, D_MODEL), lambda i: (0, 0)),
            pl.BlockSpec((D_MODEL, LANES), lambda i: (0, 0)),
            pl.BlockSpec((N_EXPERTS, 1), lambda i: (0, 0)),
        ],
        out_specs=(pl.BlockSpec((tm, D_MODEL), lambda i: (i, 0)),
                   pl.BlockSpec((tm, N_CHUNKS, LANES), lambda i: (i, 0, 0)),
                   pl.BlockSpec((8, tm), lambda i: (0, i)),
                   pl.BlockSpec((tm, LANES), lambda i: (i, 0)),
                   pl.BlockSpec((1, N_EXPERTS, LANES), lambda i: (i, 0, 0))),
        compiler_params=_params("arbitrary"),
        name="output_projection",
    )(*o_parts, *x_parts, mod, w_bf16, ln_g.reshape(1, D_MODEL), ln_b.reshape(1, D_MODEL),
      router_w_pad, router_b.reshape(N_EXPERTS, 1))


N_ASSIGN = 2 * N_TOK
MOE_TM = 256
MOE_ROWS = N_ASSIGN + N_EXPERTS * MOE_TM
MOE_TILES = MOE_ROWS // MOE_TM
DISPATCH_TM = 512
COMBINE_TM = 256


def _moe_plan(route_i, counts):
    cnt = counts[:, :, 0]
    total = jnp.sum(cnt, axis=0)
    padded = (total + MOE_TM - 1) // MOE_TM * MOE_TM
    seg_end = jnp.cumsum(padded)
    seg_start = seg_end - padded
    tile_base = seg_start[None, :] + jnp.cumsum(cnt, axis=0) - cnt
    base_tok = jnp.repeat(tile_base, OUT_TM, axis=0)
    eids = jnp.arange(N_EXPERTS, dtype=jnp.int32)[None, :]
    pos = [jnp.sum(jnp.where(route_i[k][:, None] == eids, base_tok, 0), axis=1) + route_i[2 + k]
           for k in range(2)]
    pos = jnp.concatenate(pos).astype(jnp.int32)
    tile_row0 = jnp.arange(MOE_TILES, dtype=jnp.int32) * MOE_TM
    tile_expert = jnp.minimum(jnp.sum(seg_end[None, :] <= tile_row0[:, None], axis=1), N_EXPERTS - 1)
    n_tiles = (seg_end[-1] // MOE_TM).reshape(1)
    last = MOE_ROWS - MOE_TM
    fill_rows = jnp.concatenate([jnp.minimum(seg_start + total, last),
                                 last - jnp.arange(N_EXPERTS, dtype=jnp.int32) * MOE_TM])
    return pos, tile_expert.astype(jnp.int32), n_tiles.astype(jnp.int32), fill_rows.astype(jnp.int32)


def _row_copy_wait(src_hbm, dst, sem, n_rows):
    pltpu.make_async_copy(src_hbm.at[pl.ds(0, n_rows)], dst.at[pl.ds(0, n_rows)], sem).wait()


def _dispatch_kernel(pos_ref, fill_ref, h_ref, xs_ref, zero_ref, sem, fill_sem):
    tm = h_ref.shape[0]
    t0 = pl.program_id(0) * tm

    @pl.when(pl.program_id(0) == 0)
    def _():
        zero_ref[...] = jnp.zeros_like(zero_ref)
        fills = [pltpu.make_async_copy(zero_ref, xs_ref.at[pl.ds(fill_ref[j], MOE_TM)], fill_sem)
                 for j in range(2 * N_EXPERTS)]
        for cp in fills[N_EXPERTS:]:
            cp.start()
        for cp in fills[N_EXPERTS:]:
            cp.wait()
        for cp in fills[:N_EXPERTS]:
            cp.start()
            cp.wait()

    def body(r, carry):
        for k in range(2):
            pltpu.make_async_copy(h_ref.at[r], xs_ref.at[pos_ref[k * N_TOK + t0 + r]], sem).start()
        return carry

    lax.fori_loop(0, tm, body, 0, unroll=8)
    for _ in range(2):
        _row_copy_wait(h_ref, xs_ref, sem, tm)


def _moe_dispatch(pos, fill_rows, h2_tiles):
    tm = DISPATCH_TM
    return pl.pallas_call(
        _dispatch_kernel,
        out_shape=jax.ShapeDtypeStruct((MOE_ROWS, N_CHUNKS, LANES), F32),
        grid_spec=pltpu.PrefetchScalarGridSpec(
            num_scalar_prefetch=2,
            grid=(N_TOK // tm,),
            in_specs=[pl.BlockSpec((tm, N_CHUNKS, LANES), lambda i, pos, fill: (i, 0, 0))],
            out_specs=pl.BlockSpec(memory_space=pl.ANY),
            scratch_shapes=[pltpu.VMEM((MOE_TM, N_CHUNKS, LANES), F32), pltpu.SemaphoreType.DMA,
                            pltpu.SemaphoreType.DMA],
        ),
        compiler_params=_params("arbitrary"),
        name="moe_dispatch",
    )(pos, fill_rows, h2_tiles)


def _expert_kernel(te_ref, nt_ref, xs_ref, wg_ref, wu_ref, wd_ref, ys_ref, wg16_ref, wu16_ref, wd16_ref):
    i = pl.program_id(0)

    @pl.when(i < nt_ref[0])
    def _():
        @pl.when((i == 0) | (te_ref[i] != te_ref[jnp.maximum(i - 1, 0)]))
        def _():
            wg16_ref[...] = wg_ref[0, 0].astype(BF16)
            wu16_ref[...] = wu_ref[0, 0].astype(BF16)
            wd16_ref[...] = wd_ref[0, 0].astype(BF16)

        x = _load_token_tiles(xs_ref).astype(BF16)
        he = _silu(_dot(x, wg16_ref[...])) * _dot(x, wu16_ref[...])
        _store_token_tiles(ys_ref, _dot(he.astype(BF16), wd16_ref[...]))

    @pl.when(i >= nt_ref[0])
    def _():
        ys_ref[...] = jnp.zeros_like(ys_ref)


def _moe_experts(tile_expert, n_tiles, xs, layer, wg, wu, wd):
    row_blk = lambda i, te, nt: (jnp.minimum(i, nt[0] - 1), 0, 0)
    w_blk = lambda i, te, nt: (layer, te[i], 0, 0)
    return pl.pallas_call(
        _expert_kernel,
        out_shape=jax.ShapeDtypeStruct((MOE_ROWS, N_CHUNKS, LANES), F32),
        grid_spec=pltpu.PrefetchScalarGridSpec(
            num_scalar_prefetch=2,
            grid=(MOE_TILES,),
            in_specs=[
                pl.BlockSpec((MOE_TM, N_CHUNKS, LANES), row_blk),
                pl.BlockSpec((1, 1, D_MODEL, EXPERT_FF), w_blk),
                pl.BlockSpec((1, 1, D_MODEL, EXPERT_FF), w_blk),
                pl.BlockSpec((1, 1, EXPERT_FF, D_MODEL), w_blk),
            ],
            out_specs=pl.BlockSpec((MOE_TM, N_CHUNKS, LANES), lambda i, te, nt: (i, 0, 0)),
            scratch_shapes=[pltpu.VMEM((D_MODEL, EXPERT_FF), BF16), pltpu.VMEM((D_MODEL, EXPERT_FF), BF16),
                            pltpu.VMEM((EXPERT_FF, D_MODEL), BF16)],
        ),
        compiler_params=_params("arbitrary"),
        name="moe_experts",
    )(tile_expert, n_tiles, xs, wg, wu, wd)


def _combine_kernel(pos_ref, gw_ref, x1_ref, mod_ref, lng_ref, lnb_ref, ys_ref, *rest, split):
    out_refs, (buf_ref, sem) = rest[:-2], rest[-2:]
    i = pl.program_id(0)
    n = pl.num_programs(0)
    tm = COMBINE_TM

    def start_row(tile, to_slot, r):
        for k in range(2):
            p = pos_ref[k * N_TOK + tile * tm + r]
            pltpu.make_async_copy(ys_ref.at[p], buf_ref.at[to_slot, k, r], sem.at[to_slot]).start()

    def wait_slot(s):
        for k in range(2):
            _row_copy_wait(ys_ref, buf_ref.at[s, k], sem.at[s], tm)

    def issue(tile, to_slot):
        def body(r, carry):
            start_row(tile, to_slot, r)
            return carry
        lax.fori_loop(0, tm, body, 0, unroll=8)

    @pl.when(i == 0)
    def _():
        issue(0, 0)

    @pl.when(i + 1 < n)
    def _():
        issue(i + 1, (i + 1) % 2)

    slot = i % 2
    wait_slot(slot)
    gw = gw_ref[...]
    f = gw[:, 0:1] * _load_token_tiles(buf_ref.at[slot, 0]) + gw[:, 1:2] * _load_token_tiles(buf_ref.at[slot, 1])
    m = mod_ref[0]
    out = _layer_norm(ALPHA * x1_ref[...] + m[5:6] * f, lng_ref[...], lnb_ref[...])

    if split:
        prompt_ref, sample_ref = out_refs

        @pl.when(i < N_PROMPT // tm)
        def _():
            prompt_ref[...] = out

        @pl.when(i >= N_PROMPT // tm)
        def _():
            sample_ref[...] = out
    else:
        out_refs[0][...] = out


def _moe_combine(pos, gate_w, x1, mod, ln_g, ln_b, ys, split):
    tm = COMBINE_TM
    np_blk = N_PROMPT // tm
    if split:
        out_shape = (jax.ShapeDtypeStruct((N_PROMPT, D_MODEL), F32), jax.ShapeDtypeStruct((N_SAMPLE, D_MODEL), F32))
        out_specs = (pl.BlockSpec((tm, D_MODEL), lambda i, pos: (jnp.minimum(i, np_blk - 1), 0)),
                     pl.BlockSpec((tm, D_MODEL), lambda i, pos: (jnp.maximum(i - np_blk, 0), 0)))
    else:
        out_shape = jax.ShapeDtypeStruct((N_TOK, D_MODEL), F32)
        out_specs = pl.BlockSpec((tm, D_MODEL), lambda i, pos: (i, 0))
    return pl.pallas_call(
        functools.partial(_combine_kernel, split=split),
        out_shape=out_shape,
        grid_spec=pltpu.PrefetchScalarGridSpec(
            num_scalar_prefetch=1,
            grid=(N_TOK // tm,),
            in_specs=[
                pl.BlockSpec((tm, LANES), lambda i, pos: (i, 0)),
                pl.BlockSpec((tm, D_MODEL), lambda i, pos: (i, 0)),
                pl.BlockSpec((1, 6, D_MODEL), lambda i, pos: (_cond_row(i * tm), 0, 0)),
                pl.BlockSpec((1, D_MODEL), lambda i, pos: (0, 0)),
                pl.BlockSpec((1, D_MODEL), lambda i, pos: (0, 0)),
                pl.BlockSpec(memory_space=pl.ANY),
            ],
            out_specs=out_specs,
            scratch_shapes=[pltpu.VMEM((2, 2, tm, N_CHUNKS, LANES), F32), pltpu.SemaphoreType.DMA((2,))],
        ),
        compiler_params=_params("arbitrary"),
        name="moe_combine",
    )(pos, gate_w, x1, mod, ln_g.reshape(1, D_MODEL), ln_b.reshape(1, D_MODEL), ys)


def _moe_ffn(h2_tiles, route_i, gate_w, counts, layer, wg, wu, wd, x1, mod, ln_g, ln_b, split=False):
    pos, tile_expert, n_tiles, fill_rows = _moe_plan(route_i, counts)
    xs = _moe_dispatch(pos, fill_rows, h2_tiles)
    ys = _moe_experts(tile_expert, n_tiles, xs, layer, wg, wu, wd)
    return _moe_combine(pos, gate_w, x1, mod, ln_g, ln_b, ys, split)


DN_GATE_COL0 = DN_CONV_CH
N_DIRS = 2


def _dn_prep_kernel(q_ref, k_ref, v_ref, wq_ref, wk_ref, wv_ref, qo_ref, ko_ref, vo_ref):
    length = q_ref.shape[0]
    row = lax.broadcasted_iota(jnp.int32, (length, DN_DK), 0)

    def conv(x, w):
        prev = jnp.where(row == 0, 0.0, pltpu.roll(x, 1, axis=0))
        nxt = jnp.where(row == length - 1, 0.0, pltpu.roll(x, length - 1, axis=0))
        return _silu(prev * w[0:1] + x * w[1:2] + nxt * w[2:3])

    def l2n(x):
        return x * lax.rsqrt(jnp.sum(x * x, axis=-1, keepdims=True) + 1e-6)

    for h in range(DN_HEADS):
        sl = slice(h * DN_DK, (h + 1) * DN_DK)
        q, k, v = (r[:, sl].astype(F32) for r in (q_ref, k_ref, v_ref))
        qo_ref[h] = (l2n(conv(q, wq_ref[:, sl])) * (DN_DK ** -0.5)).astype(qo_ref.dtype)
        ko_ref[h] = l2n(conv(k, wk_ref[:, sl])).astype(ko_ref.dtype)
        vo_ref[h] = conv(v, wv_ref[:, sl]).astype(vo_ref.dtype)


def _dn_prep(z1, conv_w, nb, length, row_blk0):
    shp = jax.ShapeDtypeStruct((DN_HEADS, nb * length, DN_DK), BF16)
    blk = lambda part: pl.BlockSpec((length, DN_QK_WIDTH), lambda b: (row_blk0 + b, part))
    wblk = lambda part: pl.BlockSpec((3, DN_QK_WIDTH), lambda b: (0, part))
    oblk = pl.BlockSpec((DN_HEADS, length, DN_DK), lambda b: (0, b, 0))
    return pl.pallas_call(
        _dn_prep_kernel,
        out_shape=(shp, shp, shp),
        grid=(nb,),
        in_specs=[blk(0), blk(1), blk(2), wblk(0), wblk(1), wblk(2)],
        out_specs=(oblk, oblk, oblk),
        compiler_params=_params("arbitrary"),
        name="deltanet_prep",
    )(z1, z1, z1, conv_w, conv_w, conv_w)


def _dn_gates_kernel(ab_ref, alog_ref, dtb_ref, o_ref):
    length = ab_ref.shape[0]
    x = ab_ref[...]
    a = x + dtb_ref[...]
    softplus = jnp.maximum(a, 0.0) + jnp.log1p(jnp.exp(-jnp.abs(a)))
    g = -jnp.exp(alog_ref[...]) * softplus
    beta = jax.nn.sigmoid(x)
    ri = lax.broadcasted_iota(jnp.int32, (DN_CHUNK, DN_CHUNK), 0)
    ci = lax.broadcasted_iota(jnp.int32, (DN_CHUNK, DN_CHUNK), 1)
    tril = (ri >= ci).astype(F32)
    triu = (ri <= ci).astype(F32)
    lane = lax.broadcasted_iota(jnp.int32, (DN_CHUNK, LANES), 1)
    for c in range(length // DN_CHUNK):
        rows = slice(c * DN_CHUNK, (c + 1) * DN_CHUNK)
        gch = g[rows]
        pre = jnp.dot(tril, gch, precision=lax.Precision.HIGHEST, preferred_element_type=F32)
        suf = jnp.dot(triu, gch, precision=lax.Precision.HIGHEST, preferred_element_type=F32)
        gc = jnp.where(lane < DN_HEADS, pre, suf)
        o_ref[rows, :] = jnp.where(lane < N_DIRS * DN_HEADS, gc, beta[rows])


def _dn_gates(z_ab, a_log, dt_bias, nb, length, row_blk0):
    pad = LANES - N_DIRS * DN_HEADS
    alog = jnp.pad(a_log.reshape(1, -1).astype(F32), ((0, 0), (0, pad)))
    dtb = jnp.pad(dt_bias.reshape(1, -1).astype(F32), ((0, 0), (0, pad)))
    return pl.pallas_call(
        _dn_gates_kernel,
        out_shape=jax.ShapeDtypeStruct((nb * length, LANES), F32),
        grid=(nb,),
        in_specs=[
            pl.BlockSpec((length, LANES), lambda b: (row_blk0 + b, 0)),
            pl.BlockSpec((1, LANES), lambda b: (0, 0)),
            pl.BlockSpec((1, LANES), lambda b: (0, 0)),
        ],
        out_specs=pl.BlockSpec((length, LANES), lambda b: (b, 0)),
        compiler_params=_params("arbitrary"),
        name="deltanet_gates",
    )(z_ab, alog, dtb)


def _delta_chunk(q, k, v, beta, gcol, grow, s, lower):
    ri = lax.broadcasted_iota(jnp.int32, (DN_CHUNK, DN_CHUNK), 0)
    ci = lax.broadcasted_iota(jnp.int32, (DN_CHUNK, DN_CHUNK), 1)
    incl = (ri >= ci) if lower else (ri <= ci)
    strict = (ri > ci) if lower else (ri < ci)
    decay = jnp.exp(jnp.where(incl, gcol - grow, -jnp.inf))
    kb = k * beta
    kq = _dot_nt(jnp.concatenate([kb, q], axis=0).astype(BF16), k.astype(BF16))
    kk, a_qk = kq[:DN_CHUNK], kq[DN_CHUNK:] * decay
    yield
    tri_l = jnp.where(strict, kk * decay, 0.0)

    def off_block(size):
        same = (ri // (2 * size)) == (ci // (2 * size))
        rpar, cpar = (ri // size) % 2, (ci // size) % 2
        return same & ((rpar == 1) & (cpar == 0) if lower else (rpar == 0) & (cpar == 1))

    p = jnp.where(ri == ci, 1.0, 0.0) - jnp.where(off_block(1), tri_l, 0.0)
    size = 2
    while size < DN_CHUNK:
        p16 = p.astype(BF16)
        pc = _dot(p16, jnp.where(off_block(size), tri_l, 0.0).astype(BF16))
        yield
        p = p - _dot(pc.astype(BF16), p16)
        yield
        size *= 2
    eg = jnp.exp(gcol)
    uw = _dot(p.astype(BF16), jnp.concatenate([v * beta, kb * eg], axis=1).astype(BF16))
    yield
    u, w = uw[:, :DN_DV], uw[:, DN_DV:]
    ws = _dot(jnp.concatenate([w, q * eg], axis=0).astype(BF16), s.astype(BF16))
    v_new, o_inter = u - ws[:DN_CHUNK], ws[DN_CHUNK:]
    yield
    v_new16 = v_new.astype(BF16)
    o = o_inter + _dot(a_qk.astype(BF16), v_new16)
    g_last = gcol[DN_CHUNK - 1:DN_CHUNK] if lower else gcol[0:1]
    kd = k * jnp.exp(g_last - gcol)
    s_new = s * jnp.exp(g_last) + _dot_tn(kd.astype(BF16), v_new16)
    return o, s_new


def _run_interleaved(chains):
    results = [None] * len(chains)
    active = list(enumerate(chains))
    while active:
        still = []
        for idx, gen in active:
            try:
                next(gen)
                still.append((idx, gen))
            except StopIteration as stop:
                results[idx] = stop.value
        active = still
    return results


def _dn_scan_kernel(*refs, has_init, want_final):
    (qf_ref, kf_ref, vf_ref, gf_ref, qb_ref, kb_ref, vb_ref, gb_ref), rest = refs[:8], refs[8:]
    if has_init:
        s0_ref, rest = rest[0], rest[1:]
    of_ref, ob_ref = rest[0], rest[1]
    rest = rest[2:]
    if want_final:
        sf_ref, rest = rest[0], rest[1:]
    s_ref = rest[0]
    c = pl.program_id(1)
    nc = pl.num_programs(1)

    @pl.when(c == 0)
    def _():
        for d in range(N_DIRS):
            for h in range(DN_HEADS):
                if has_init:
                    s_ref[d * DN_HEADS + h] = s0_ref[0, d, h]
                else:
                    s_ref[d * DN_HEADS + h] = jnp.zeros((DN_DK, DN_DV), F32)

    dirs = ((qf_ref, kf_ref, vf_ref, gf_ref, of_ref, True), (qb_ref, kb_ref, vb_ref, gb_ref, ob_ref, False))
    chains = []
    for d, (q_ref, k_ref, v_ref, g_ref, o_ref, lower) in enumerate(dirs):
        gates = g_ref[...]
        gates_t = gates.T
        for h in range(DN_HEADS):
            j = d * DN_HEADS + h
            jb = N_DIRS * DN_HEADS + j
            chains.append(_delta_chunk(q_ref[h].astype(F32), k_ref[h].astype(F32), v_ref[h].astype(F32),
                                       gates[:, jb:jb + 1], gates[:, j:j + 1], gates_t[j:j + 1, :],
                                       s_ref[j], lower))
    results = _run_interleaved(chains)
    for d, (_, _, _, _, o_ref, _) in enumerate(dirs):
        for h in range(DN_HEADS):
            j = d * DN_HEADS + h
            o, s_new = results[j]
            o_ref[:, h * DN_DV:(h + 1) * DN_DV] = o.astype(o_ref.dtype)
            s_ref[j] = s_new

    if want_final:
        @pl.when(c == nc - 1)
        def _():
            for d in range(N_DIRS):
                for h in range(DN_HEADS):
                    sf_ref[0, d, h] = s_ref[d * DN_HEADS + h]


def _dn_scan(qn, kn, vn, gates, s0, nb, length, want_final):
    nc = length // DN_CHUNK
    has_init = s0 is not None
    fwd = lambda b, c: (0, b * nc + c, 0)
    bwd = lambda b, c: (0, b * nc + (nc - 1 - c), 0)
    hm = lambda im: pl.BlockSpec((DN_HEADS, DN_CHUNK, DN_DK), im)
    in_specs = [hm(fwd), hm(fwd), hm(fwd), pl.BlockSpec((DN_CHUNK, LANES), lambda b, c: (b * nc + c, 0)),
                hm(bwd), hm(bwd), hm(bwd), pl.BlockSpec((DN_CHUNK, LANES), lambda b, c: (b * nc + (nc - 1 - c), 0))]
    args = [qn, kn, vn, gates, qn, kn, vn, gates]
    state_blk = pl.BlockSpec((1, N_DIRS, DN_HEADS, DN_DK, DN_DV), lambda b, c: (b, 0, 0, 0, 0))
    if has_init:
        in_specs.append(state_blk)
        args.append(s0)
    o_shape = jax.ShapeDtypeStruct((nb * length, DN_WIDTH), BF16)
    out_shape = [o_shape, o_shape]
    out_specs = [pl.BlockSpec((DN_CHUNK, DN_WIDTH), lambda b, c: (b * nc + c, 0)),
                 pl.BlockSpec((DN_CHUNK, DN_WIDTH), lambda b, c: (b * nc + (nc - 1 - c), 0))]
    if want_final:
        out_shape.append(jax.ShapeDtypeStruct((nb, N_DIRS, DN_HEADS, DN_DK, DN_DV), F32))
        out_specs.append(state_blk)
    return pl.pallas_call(
        functools.partial(_dn_scan_kernel, has_init=has_init, want_final=want_final),
        out_shape=tuple(out_shape),
        grid=(nb, nc),
        in_specs=in_specs,
        out_specs=tuple(out_specs),
        scratch_shapes=[pltpu.VMEM((N_DIRS * DN_HEADS, DN_DK, DN_DV), F32)],
        compiler_params=_params("arbitrary", "arbitrary"),
        name="deltanet_scan",
    )(*args)


DN_MERGE_TM = 256


def _dn_merge_kernel(of_ref, ob_ref, gate_ref, g_ref, o_ref):
    for h in range(DN_HEADS):
        sl = slice(h * DN_DV, (h + 1) * DN_DV)
        o = of_ref[:, sl].astype(F32) + ob_ref[:, sl].astype(F32)
        o = o * lax.rsqrt(jnp.mean(o * o, axis=-1, keepdims=True) + EPS) * g_ref[...]
        o_ref[:, sl] = (o * _silu(gate_ref[:, sl].astype(F32))).astype(o_ref.dtype)


def _dn_merge(o_f, o_b, z1, onorm_g, n_rows, row0):
    tm = DN_MERGE_TM
    blk = pl.BlockSpec((tm, DN_WIDTH), lambda i: (i, 0))
    return pl.pallas_call(
        _dn_merge_kernel,
        out_shape=jax.ShapeDtypeStruct((n_rows, DN_WIDTH), BF16),
        grid=(n_rows // tm,),
        in_specs=[blk, blk,
                  pl.BlockSpec((tm, DN_WIDTH), lambda i: (row0 // tm + i, DN_GATE_COL0 // DN_WIDTH)),
                  pl.BlockSpec((1, DN_DV), lambda i: (0, 0))],
        out_specs=blk,
        compiler_params=_params("arbitrary"),
        name="deltanet_merge",
    )(o_f, o_b, z1, onorm_g.reshape(1, DN_DV))


def _deltanet_stream(z1, z_ab, conv_w, a_log, dt_bias, onorm_g, s0, nb, length, row0, want_final):
    row_blk0 = row0 // length
    qn, kn, vn = _dn_prep(z1, conv_w, nb, length, row_blk0)
    gates = _dn_gates(z_ab, a_log, dt_bias, nb, length, row_blk0)
    outs = _dn_scan(qn, kn, vn, gates, s0, nb, length, want_final)
    merged = _dn_merge(outs[0], outs[1], z1, onorm_g, nb * length, row0)
    return merged, (outs[2] if want_final else None)


def kernel(x_prompt, x_sample, cache_nat_k, cache_nat_v, cache_diff_k, cache_diff_v, state_delta, c, c_ctx, w_in_even, w_out_even, nat_bias, diff_lam, diff_subln, w_in_odd, conv_odd, a_log_odd, dt_bias_odd, onorm_odd, w_out_odd, ada_w, ada_b, ln_g, ln_b, router_w, router_b, moe_wg, moe_wu, moe_wd):
    x_parts = (x_prompt.reshape(N_PROMPT, D_MODEL), x_sample.reshape(N_SAMPLE, D_MODEL))
    cond =jnp.concatenate([c_ctx[None], c, jnp.zeros((N_COND - 1 - DEC_BATCH, D_MODEL), F32)], axis=0)
    mod = _ada_modulation(cond, ada_w, ada_b).reshape(DEPTH, N_COND, 6, D_MODEL)
    router_w_pad = jnp.pad(router_w, ((0, 0), (0, LANES - N_EXPERTS)))

    lam_init = 0.8 - 0.6 * math.exp(-0.3 * 0)
    z = _input_projection(x_parts, mod[0], w_in_even[0].astype(BF16))
    new_nat_k = z[:N_PROMPT, _NAT_K0:_NAT_V0].reshape(BATCH, 1, SEQ, NAT_HEADS, HEAD_DIM)
    new_nat_v = z[:N_PROMPT, _NAT_V0:_DIFF_Q0].reshape(BATCH, 1, SEQ, NAT_HEADS, HEAD_DIM)
    new_diff_k = z[:N_PROMPT, _DIFF_K0:_DIFF_V0].reshape(BATCH, 1, SEQ, DIFF_HEADS, 2, HEAD_DIM)
    new_diff_v = z[:N_PROMPT, _DIFF_V0:].reshape(BATCH, 1, SEQ, DIFF_HEADS, 2 * HEAD_DIM)
    o_ctx = _context_attention(z, diff_lam[0], diff_subln[0], lam_init)
    o_nat = _nat_latent_attention(z, cache_nat_k[:, 0].reshape(DEC_BATCH, PAST_LEN, NAT_WIDTH),
                                  cache_nat_v[:, 0].reshape(DEC_BATCH, PAST_LEN, NAT_WIDTH),
                                  _nat_bias_table(nat_bias[0]))
    o_diff = _diff_latent_attention(z, cache_diff_k[:, 0].reshape(DEC_BATCH, PAST_LEN, DIFF_WIDTH),
                                    cache_diff_v[:, 0].reshape(DEC_BATCH, PAST_LEN, DIFF_WIDTH),
                                    diff_lam[0], diff_subln[0], lam_init)
    x1, *routed = _output_projection((o_ctx, o_nat, o_diff), x_parts, mod[0], w_out_even[0].astype(BF16),
                                     ln_g[0, 0], ln_b[0, 0], router_w_pad, router_b)
    x = _moe_ffn(*routed, 0, moe_wg, moe_wu, moe_wd, x1, mod[0], ln_g[0, 1], ln_b[0, 1])

    w1 = w_in_odd[0]
    w1 = jnp.concatenate([w1[:, :DN_CONV_CH], w1[:, DN_CONV_CH + 4 * DN_HEADS:],
                          w1[:, DN_CONV_CH:DN_CONV_CH + 4 * DN_HEADS],
                          jnp.zeros((D_MODEL, LANES - 4 * DN_HEADS), F32)], axis=1).astype(BF16)
    z1, z_ab = _input_projection((x,), mod[1], w1, f32_tail=LANES)
    o_p, new_state = _deltanet_stream(z1, z_ab, conv_odd[0], a_log_odd[0], dt_bias_odd[0], onorm_odd[0],
                                      None, BATCH, SEQ, 0, True)
    o_s, _ = _deltanet_stream(z1, z_ab, conv_odd[0], a_log_odd[0], dt_bias_odd[0], onorm_odd[0],
                              state_delta[:, 0], DEC_BATCH, DEC_SEQ, N_PROMPT, False)
    x1, *routed = _output_projection((o_p, o_s), (x,), mod[1], w_out_odd[0].astype(BF16), ln_g[1, 0], ln_b[1, 0],
                                     router_w_pad, router_b)
    y_prompt, y_sample = _moe_ffn(*routed, 1, moe_wg, moe_wu, moe_wd, x1, mod[1], ln_g[1, 1], ln_b[1, 1],
                                  split=True)
    y_prompt = y_prompt.reshape(BATCH, SEQ, D_MODEL)
    y_sample = y_sample.reshape(DEC_BATCH, DEC_SEQ, D_MODEL)
    return (y_prompt, y_sample, new_nat_k, new_nat_v, new_diff_k, new_diff_v, new_state[:, None])
```

```python
import functools
import math

import jax
import jax.numpy as jnp
import numpy as np
from jax import lax
from jax.experimental import pallas as pl
from jax.experimental.pallas import tpu as pltpu

F32 = jnp.float32
BF16 = jnp.bfloat16

D_MODEL = 1024
BATCH = 16
SEQ = 256
DEC_BATCH = 8
DEC_SEQ = 1024
PAST_LEN = 256
GRID_W = 64
N_ROWS = DEC_SEQ // GRID_W

HEAD_DIM = 64
NAT_HEADS = 8
NAT_WIN_H = 8
NAT_WIN_W = 16
DIFF_HEADS = 4
NAT_WIDTH = NAT_HEADS * HEAD_DIM
DIFF_WIDTH = DIFF_HEADS * 2 * HEAD_DIM
EVEN_IN = 3 * NAT_WIDTH + 3 * DIFF_WIDTH
ROPE_THETA = 10000.0

DN_HEADS = 8
DN_DK = 128
DN_DV = 128
DN_QK_WIDTH = DN_HEADS * DN_DK
DN_WIDTH = DN_HEADS * DN_DV
DN_CONV_CH = 2 * DN_QK_WIDTH + DN_WIDTH
DN_CHUNK = 64

N_EXPERTS = 16
N_GROUPS = 4
EXPERTS_PER_GROUP = N_EXPERTS // N_GROUPS
EXPERT_FF = 512

DEPTH = 2
ALPHA = (2 * DEPTH) ** 0.25
EPS = 1e-5

N_PROMPT = BATCH * SEQ
N_SAMPLE = DEC_BATCH * DEC_SEQ
N_TOK = N_PROMPT + N_SAMPLE
N_COND = 16

VMEM_LIMIT = 56 * 1024 * 1024
LANES = 128
N_CHUNKS = D_MODEL // LANES


def _params(*sem):
    return pltpu.CompilerParams(dimension_semantics=sem, vmem_limit_bytes=VMEM_LIMIT)


def _cond_row(row0):
    return jnp.where(row0 < N_PROMPT, 0, 1 + (row0 - N_PROMPT) // DEC_SEQ)


def _silu(x):
    return x * jax.nn.sigmoid(x)


def _layer_norm(r, g, b):
    mu = jnp.mean(r, axis=-1, keepdims=True)
    xc = r - mu
    var = jnp.mean(xc * xc, axis=-1, keepdims=True)
    return xc * lax.rsqrt(var + EPS) * g + b


def _dot(a, b):
    return jnp.dot(a, b, preferred_element_type=F32)


def _dot_nt(a, b):
    return lax.dot_general(a, b, (((1,), (1,)), ((), ())), preferred_element_type=F32)


def _dot_tn(a, b):
    return lax.dot_general(a, b, (((0,), (0,)), ((), ())), preferred_element_type=F32)


ADA_TN = 1536


def _ada_kernel(c_ref, w_ref, b_ref, o_ref):
    s = _silu(c_ref[...])
    o_ref[0] = _dot(s.astype(BF16), w_ref[0].astype(BF16)) + b_ref[0]


def _ada_modulation(cond, ada_w, ada_b):
    n = 6 * D_MODEL
    return pl.pallas_call(
        _ada_kernel,
        out_shape=jax.ShapeDtypeStruct((DEPTH, N_COND, n), F32),
        grid=(DEPTH, n // ADA_TN),
        in_specs=[
            pl.BlockSpec((N_COND, D_MODEL), lambda l, j: (0, 0)),
            pl.BlockSpec((1, D_MODEL, ADA_TN), lambda l, j: (l, 0, j)),
            pl.BlockSpec((1, 1, ADA_TN), lambda l, j: (l, 0, j)),
        ],
        out_specs=pl.BlockSpec((1, N_COND, ADA_TN), lambda l, j: (l, 0, j)),
        compiler_params=_params("arbitrary", "arbitrary"),
        name="ada_modulation",
    )(cond, ada_w, ada_b.reshape(DEPTH, 1, n))


PROJ_TM = 512


def _stream_specs(widths, tm):
    if len(widths) == 1:
        return [pl.BlockSpec((tm, widths[0]), lambda i: (i, 0))]
    n_ctx = N_PROMPT // tm
    return ([pl.BlockSpec((tm, widths[0]), lambda i: (jnp.minimum(i, n_ctx - 1), 0))]
            + [pl.BlockSpec((tm, w), lambda i: (jnp.maximum(i - n_ctx, 0), 0)) for w in widths[1:]])


def _stream_tile(refs, tm):
    if len(refs) == 1:
        return refs[0][...]
    latent = [r[...] for r in refs[1:]]
    latent = latent[0] if len(latent) == 1 else jnp.concatenate(latent, axis=1)
    return jnp.where(pl.program_id(0) * tm < N_PROMPT, refs[0][...], latent)


def _inproj_kernel(*refs, n_x):
    x_refs, (mod_ref, w_ref, o_ref), tail_refs = refs[:n_x], refs[n_x:n_x + 3], refs[n_x + 3:]
    m = mod_ref[0]
    h = _stream_tile(x_refs, o_ref.shape[0]) * (1.0 + m[1:2]) + m[0:1]
    z = _dot(h.astype(BF16), w_ref[...])
    n_main = o_ref.shape[1]
    o_ref[...] = z[:, :n_main].astype(o_ref.dtype)
    if tail_refs:
        tail_refs[0][...] = z[:, n_main:]


def _input_projection(x_parts, mod, w_bf16, tm=PROJ_TM, f32_tail=0):
    n = w_bf16.shape[1] - f32_tail
    out_shape = [jax.ShapeDtypeStruct((N_TOK, n), BF16 if f32_tail else F32)]
    out_specs = [pl.BlockSpec((tm, n), lambda i: (i, 0))]
    if f32_tail:
        out_shape.append(jax.ShapeDtypeStruct((N_TOK, f32_tail), F32))
        out_specs.append(pl.BlockSpec((tm, f32_tail), lambda i: (i, 0)))
    outs = pl.pallas_call(
        functools.partial(_inproj_kernel, n_x=len(x_parts)),
        out_shape=tuple(out_shape),
        grid=(N_TOK // tm,),
        in_specs=_stream_specs([a.shape[1] for a in x_parts], tm) + [
            pl.BlockSpec((1, 6, D_MODEL), lambda i: (_cond_row(i * tm), 0, 0)),
            pl.BlockSpec((D_MODEL, n + f32_tail), lambda i: (0, 0)),
        ],
        out_specs=tuple(out_specs),
        compiler_params=_params("arbitrary"),
        name="input_projection",
    )(*x_parts, mod, w_bf16)
    return outs if f32_tail else outs[0]


_NAT_Q0, _NAT_K0, _NAT_V0 = 0, NAT_WIDTH, 2 * NAT_WIDTH
_DIFF_Q0 = 3 * NAT_WIDTH
_DIFF_K0 = _DIFF_Q0 + DIFF_WIDTH
_DIFF_V0 = _DIFF_K0 + DIFF_WIDTH
ATTN_SCALE = HEAD_DIM ** -0.5


def _diff_lambda(lam_ref, lam_init):
    lp = lam_ref[...]
    return (jnp.exp(jnp.sum(lp[0:1] * lp[1:2], axis=-1, keepdims=True))
            - jnp.exp(jnp.sum(lp[2:3] * lp[3:4], axis=-1, keepdims=True)) + lam_init)


def _scaled_q(q):
    assert math.log2(HEAD_DIM) % 2 == 0
    return (q * ATTN_SCALE).astype(BF16)


def _with_ones(v):
    width = v.shape[1] if v.shape[1] % LANES == 0 else LANES - v.shape[1] % LANES
    return jnp.concatenate([v, jnp.ones((v.shape[0], width), v.dtype)], axis=1)


def _sub_norm(o, g, lam_init):
    ms = jnp.mean(o * o, axis=-1, keepdims=True)
    return o * lax.rsqrt(ms + EPS) * g * (1.0 - lam_init)


def _ctx_attn_kernel(z_ref, lam_ref, subg_ref, o_ref, *, lam_init):
    lam = _diff_lambda(lam_ref, lam_init)

    def softmax_av(q0, k0, v0, v_width):
        q = _scaled_q(z_ref[:, q0:q0 + HEAD_DIM])
        k = z_ref[:, k0:k0 + HEAD_DIM].astype(BF16)
        s = _dot_nt(q, k)
        yield
        e = jnp.exp(s - jnp.max(s, axis=-1, keepdims=True))
        yield
        av = _dot(e.astype(BF16), _with_ones(z_ref[:, v0:v0 + v_width].astype(BF16)))
        yield
        return av[:, :v_width] / av[:, v_width:v_width + 1]

    chains = [softmax_av(_NAT_Q0 + h * HEAD_DIM, _NAT_K0 + h * HEAD_DIM, _NAT_V0 + h * HEAD_DIM, HEAD_DIM)
              for h in range(NAT_HEADS)]
    chains += [softmax_av(_DIFF_Q0 + i * HEAD_DIM, _DIFF_K0 + i * HEAD_DIM,
                          _DIFF_V0 + (i // 2) * 2 * HEAD_DIM, 2 * HEAD_DIM)
               for i in range(2 * DIFF_HEADS)]
    outs = _run_interleaved(chains)
    for h in range(NAT_HEADS):
        o_ref[:, h * HEAD_DIM:(h + 1) * HEAD_DIM] = outs[h].astype(o_ref.dtype)
    for h in range(DIFF_HEADS):
        c = NAT_WIDTH + h * 2 * HEAD_DIM
        o = outs[NAT_HEADS + 2 * h] - lam * outs[NAT_HEADS + 2 * h + 1]
        o_ref[:, c:c + 2 * HEAD_DIM] = _sub_norm(o, subg_ref[...], lam_init).astype(o_ref.dtype)


def _context_attention(z, lam_p, subln_g, lam_init):
    return pl.pallas_call(
        functools.partial(_ctx_attn_kernel, lam_init=lam_init),
        out_shape=jax.ShapeDtypeStruct((N_PROMPT, D_MODEL), BF16),
        grid=(BATCH,),
        in_specs=[
            pl.BlockSpec((SEQ, EVEN_IN), lambda b: (b, 0)),
            pl.BlockSpec((4, HEAD_DIM), lambda b: (0, 0)),
            pl.BlockSpec((1, 2 * HEAD_DIM), lambda b: (0, 0)),
        ],
        out_specs=pl.BlockSpec((SEQ, D_MODEL), lambda b: (b, 0)),
        compiler_params=_params("arbitrary"),
        name="context_attention",
    )(z, lam_p, subln_g.reshape(1, 2 * HEAD_DIM))


NAT_TQ = 512
NAT_KEYS = 12 * GRID_W
_SAMPLE_BLK0 = N_PROMPT // DEC_SEQ


def _nat_bias_table(rel_bias):
    rows = N_ROWS
    kh = min(NAT_WIN_H, rows)
    kw = NAT_WIN_W
    col = np.arange(GRID_W)
    c_start = np.clip(col - kw // 2, 0, GRID_W - kw)
    col_ok = (col[None, :] >= c_start[:, None]) & (col[None, :] < c_start[:, None] + kw)
    dc = np.clip(col[None, :] - col[:, None], 1 - kw, kw - 1) + (NAT_WIN_W - 1)
    col_onehot = ((dc[None] == np.arange(2 * kw - 1)[:, None, None]) & col_ok[None]).astype(np.float32)
    span = NAT_KEYS // GRID_W
    row_onehot = np.zeros((2 * NAT_WIN_H - 1, rows, span), np.float32)
    for r in range(rows):
        start = min(max(r - kh // 2, 0), rows - kh)
        a0 = _nat_key0(r * GRID_W // NAT_TQ) // GRID_W
        assert a0 <= start and start + kh <= a0 + span
        for a in range(start, start + kh):
            row_onehot[a - r + NAT_WIN_H - 1, r, a - a0] = 1.0
    hi = lax.Precision.HIGHEST
    by_row = jnp.einsum('hed,era->hrad', rel_bias.astype(F32), row_onehot, precision=hi)
    table = jnp.einsum('hrad,dqk->hrqak', by_row, col_onehot, precision=hi)
    inside = (row_onehot.sum(0) > 0)[:, None, :, None] & col_ok[None, :, None, :]
    return jnp.where(inside, table, -jnp.inf).reshape(NAT_HEADS, DEC_SEQ, NAT_KEYS)


def _nat_key0(qt):
    half = DEC_SEQ // NAT_TQ // 2
    return (qt >= half) * (DEC_SEQ - NAT_KEYS)


def _nat_latent_kernel(q_ref, k_ref, v_ref, ck_ref, cv_ref, bias_ref, o_ref):
    key0 = pl.multiple_of(_nat_key0(pl.program_id(1)).astype(jnp.int32), DEC_SEQ - NAT_KEYS)
    keys = pl.ds(key0, NAT_KEYS)

    def head(j):
        sl = slice(j * HEAD_DIM, (j + 1) * HEAD_DIM)
        q = _scaled_q(q_ref[:, sl])
        s_loc = _dot_nt(q, k_ref[keys, sl].astype(BF16)) + bias_ref[j]
        s_ctx = _dot_nt(q, ck_ref[0, :, sl].astype(BF16))
        yield
        m = jnp.maximum(jnp.max(s_loc, axis=-1, keepdims=True), jnp.max(s_ctx, axis=-1, keepdims=True))
        yield
        e_loc = jnp.exp(s_loc - m)
        e_ctx = jnp.exp(s_ctx - m)
        yield
        o = (_dot(e_loc.astype(BF16), _with_ones(v_ref[keys, sl].astype(BF16)))
             + _dot(e_ctx.astype(BF16), _with_ones(cv_ref[0, :, sl].astype(BF16))))
        yield
        return (o[:, :HEAD_DIM] / o[:, HEAD_DIM:HEAD_DIM + 1]).astype(o_ref.dtype)

    outs = _run_interleaved([head(j) for j in range(2)])
    for j in range(2):
        o_ref[:, j * HEAD_DIM:(j + 1) * HEAD_DIM] = outs[j]


def _nat_latent_attention(z, ck, cv, bias):
    nq = DEC_SEQ // NAT_TQ
    qblk0 = N_PROMPT // NAT_TQ
    pair = 2 * HEAD_DIM
    return pl.pallas_call(
        _nat_latent_kernel,
        out_shape=jax.ShapeDtypeStruct((N_SAMPLE, NAT_WIDTH), BF16),
        grid=(NAT_HEADS // 2, nq, DEC_BATCH),
        in_specs=[
            pl.BlockSpec((NAT_TQ, pair), lambda hp, qt, b: (qblk0 + b * nq + qt, _NAT_Q0 // pair + hp)),
            pl.BlockSpec((DEC_SEQ, pair), lambda hp, qt, b: (_SAMPLE_BLK0 + b, _NAT_K0 // pair + hp)),
            pl.BlockSpec((DEC_SEQ, pair), lambda hp, qt, b: (_SAMPLE_BLK0 + b, _NAT_V0 // pair + hp)),
            pl.BlockSpec((1, PAST_LEN, pair), lambda hp, qt, b: (b, 0, hp)),
            pl.BlockSpec((1, PAST_LEN, pair), lambda hp, qt, b: (b, 0, hp)),
            pl.BlockSpec((2, NAT_TQ, NAT_KEYS), lambda hp, qt, b: (hp, qt, 0)),
        ],
        out_specs=pl.BlockSpec((NAT_TQ, pair), lambda hp, qt, b: (b * nq + qt, hp)),
        compiler_params=_params("arbitrary", "arbitrary", "arbitrary"),
        name="nat_latent_attention",
    )(z, z, z, ck, cv, bias)


DIFF_TQ = 512


def _rope_tables():
    nf = HEAD_DIM // 4
    t = jnp.arange(DEC_SEQ)
    inv = ROPE_THETA ** (-jnp.arange(nf, dtype=F32) / nf)
    ang_r = (t // GRID_W).astype(F32)[:, None] * inv
    ang_c = (t % GRID_W).astype(F32)[:, None] * inv
    cos = jnp.concatenate([jnp.cos(ang_r)] * 2 + [jnp.cos(ang_c)] * 2, axis=-1)
    sin = jnp.concatenate([-jnp.sin(ang_r), jnp.sin(ang_r), -jnp.sin(ang_c), jnp.sin(ang_c)], axis=-1)
    return jnp.tile(cos, (1, 2)), jnp.tile(sin, (1, 2))


def _rope(x, cos, sin):
    nf = HEAD_DIM // 4
    lane = lax.broadcasted_iota(jnp.int32, x.shape, 1)
    upper = (lane // nf) % 2 == 1
    partner = jnp.where(upper, pltpu.roll(x, nf, axis=1), pltpu.roll(x, x.shape[1] - nf, axis=1))
    return x * cos + partner * sin


def _diff_latent_kernel(q_ref, k_ref, v_ref, ck_ref, cv_ref, cos_ref, sin_ref, lam_ref, subg_ref,
                        o_ref, kr_ref, *, lam_init):
    qt = pl.program_id(2)

    @pl.when(qt == 0)
    def _():
        kr_ref[...] = _rope(k_ref[...], cos_ref[...], sin_ref[...]).astype(BF16)

    lam = _diff_lambda(lam_ref, lam_init)
    row0 = pl.multiple_of(qt * DIFF_TQ, DIFF_TQ)
    q = _scaled_q(_rope(q_ref[...], cos_ref[pl.ds(row0, DIFF_TQ), :], sin_ref[pl.ds(row0, DIFF_TQ), :]))
    v_loc = _with_ones(v_ref[...].astype(BF16))
    v_ctx = _with_ones(cv_ref[0].astype(BF16))
    width = 2 * HEAD_DIM

    def softmax_av(j):
        sl = slice(j * HEAD_DIM, (j + 1) * HEAD_DIM)
        s_ctx = _dot_nt(q[:, sl], ck_ref[0, :, sl].astype(BF16))
        s_loc = _dot_nt(q[:, sl], kr_ref[:, sl])
        yield
        m = jnp.maximum(jnp.max(s_loc, axis=-1, keepdims=True), jnp.max(s_ctx, axis=-1, keepdims=True))
        yield
        e_loc = jnp.exp(s_loc - m)
        e_ctx = jnp.exp(s_ctx - m)
        yield
        av = _dot(e_loc.astype(BF16), v_loc) + _dot(e_ctx.astype(BF16), v_ctx)
        yield
        return av[:, :width] / av[:, width:width + 1]

    parts = _run_interleaved([softmax_av(j) for j in range(2)])
    o = parts[0] - lam * parts[1]
    o_ref[...] = _sub_norm(o, subg_ref[...], lam_init).astype(o_ref.dtype)


def _diff_latent_attention(z, ck, cv, lam_p, subln_g, lam_init):
    nq = DEC_SEQ // DIFF_TQ
    qblk0 = N_PROMPT // DIFF_TQ
    w = 2 * HEAD_DIM
    cos, sin = _rope_tables()
    return pl.pallas_call(
        functools.partial(_diff_latent_kernel, lam_init=lam_init),
        out_shape=jax.ShapeDtypeStruct((N_SAMPLE, DIFF_WIDTH), BF16),
        grid=(DEC_BATCH, DIFF_HEADS, nq),
        in_specs=[
            pl.BlockSpec((DIFF_TQ, w), lambda b, h, qt: (qblk0 + b * nq + qt, _DIFF_Q0 // w + h)),
            pl.BlockSpec((DEC_SEQ, w), lambda b, h, qt: (_SAMPLE_BLK0 + b, _DIFF_K0 // w + h)),
            pl.BlockSpec((DEC_SEQ, w), lambda b, h, qt: (_SAMPLE_BLK0 + b, _DIFF_V0 // w + h)),
            pl.BlockSpec((1, PAST_LEN, w), lambda b, h, qt: (b, 0, h)),
            pl.BlockSpec((1, PAST_LEN, w), lambda b, h, qt: (b, 0, h)),
            pl.BlockSpec((DEC_SEQ, w), lambda b, h, qt: (0, 0)),
            pl.BlockSpec((DEC_SEQ, w), lambda b, h, qt: (0, 0)),
            pl.BlockSpec((4, HEAD_DIM), lambda b, h, qt: (0, 0)),
            pl.BlockSpec((1, w), lambda b, h, qt: (0, 0)),
        ],
        out_specs=pl.BlockSpec((DIFF_TQ, w), lambda b, h, qt: (b * nq + qt, h)),
        scratch_shapes=[pltpu.VMEM((DEC_SEQ, w), BF16)],
        compiler_params=_params("arbitrary", "arbitrary", "arbitrary"),
        name="diff_latent_attention",
    )(z, z, z, ck, cv, cos, sin, lam_p, subln_g.reshape(1, w))


OUT_TM = 256


def _route(logits_t, rb_col):
    sc = jax.nn.sigmoid(logits_t)
    bi = sc + rb_col
    srow = [sc[e:e + 1] for e in range(N_EXPERTS)]
    brow = [bi[e:e + 1] for e in range(N_EXPERTS)]
    n = EXPERTS_PER_GROUP
    gscore = []
    for g in range(N_GROUPS):
        v = brow[g * n:(g + 1) * n]
        best = None
        for i in range(n):
            for j in range(i + 1, n):
                s = v[i] + v[j]
                best = s if best is None else jnp.maximum(best, s)
        gscore.append(best)
    sel = jnp.zeros_like(gscore[0], dtype=jnp.int32)
    best = gscore[0]
    for g in range(1, N_GROUPS):
        better = gscore[g] > best
        sel = jnp.where(better, g, sel)
        best = jnp.where(better, gscore[g], best)

    def pick_group(rows, i):
        out = rows[(N_GROUPS - 1) * n + i]
        for g in range(N_GROUPS - 2, -1, -1):
            out = jnp.where(sel == g, rows[g * n + i], out)
        return out

    bv = [pick_group(brow, i) for i in range(n)]
    sv = [pick_group(srow, i) for i in range(n)]
    i1 = jnp.zeros_like(sel)
    m1, w1 = bv[0], sv[0]
    for i in range(1, n):
        better = bv[i] > m1
        i1 = jnp.where(better, i, i1)
        m1 = jnp.where(better, bv[i], m1)
        w1 = jnp.where(better, sv[i], w1)
    i2 = jnp.zeros_like(sel)
    m2 = jnp.full_like(m1, -jnp.inf)
    w2 = jnp.zeros_like(w1)
    for i in range(n):
        better = (i1 != i) & (bv[i] > m2)
        i2 = jnp.where(better, i, i2)
        m2 = jnp.where(better, bv[i], m2)
        w2 = jnp.where(better, sv[i], w2)
    tot = w1 + w2
    return sel * n + i1, sel * n + i2, w1 / tot, w2 / tot


def _store_token_tiles(ref, value):
    rows = value.shape[0]
    chunks = jnp.stack([value[:, c * LANES:(c + 1) * LANES] for c in range(N_CHUNKS)], axis=0)
    tiles = jnp.transpose(chunks.reshape(N_CHUNKS, rows // 8, 8, LANES), (1, 2, 0, 3))
    ref[...] = tiles.reshape(rows, N_CHUNKS, LANES)


def _load_token_tiles(ref):
    rows = ref.shape[0]
    tiles = ref[...].reshape(rows // 8, 8, N_CHUNKS, LANES)
    chunks = jnp.transpose(tiles, (2, 0, 1, 3)).reshape(N_CHUNKS, rows, LANES)
    return jnp.concatenate([chunks[c] for c in range(N_CHUNKS)], axis=1)


def _outproj_kernel(*refs, n_o, n_x):
    o_refs, x_refs, refs = refs[:n_o], refs[n_o:n_o + n_x], refs[n_o + n_x:]
    mod_ref, w_ref, lng_ref, lnb_ref, rw_ref, rb_ref, x1_ref, h2_ref, ri_ref, rw_out_ref, cnt_ref = refs
    tm = x1_ref.shape[0]
    m = mod_ref[0]
    y = _dot(_stream_tile(o_refs, tm), w_ref[...])
    x1 = _layer_norm(ALPHA * _stream_tile(x_refs, tm) + m[2:3] * y, lng_ref[...], lnb_ref[...])
    x1_ref[...] = x1
    h2 = x1 * (1.0 + m[4:5]) + m[3:4]
    _store_token_tiles(h2_ref, h2)
    rw = rw_ref[...]
    h_hi, rw_hi = h2.astype(BF16), rw.astype(BF16)
    h_lo = (h2 - h_hi.astype(F32)).astype(BF16)
    rw_lo = (rw - rw_hi.astype(F32)).astype(BF16)
    logits = _dot(h_hi, rw_hi) + (_dot(h_lo, rw_hi) + _dot(h_hi, rw_lo))
    e1, e2, w1, w2 = _route(logits.T[:N_EXPERTS], rb_ref[...])
    eid = lax.broadcasted_iota(jnp.int32, (N_EXPERTS, tm), 0)
    onehot = (eid == e1) | (eid == e2)
    ti = lax.broadcasted_iota(jnp.int32, (tm, tm), 0)
    tj = lax.broadcasted_iota(jnp.int32, (tm, tm), 1)
    earlier = jnp.where(ti < tj, 1.0, 0.0).astype(BF16)
    rank = _dot(jnp.where(onehot, 1.0, 0.0).astype(BF16), earlier)
    r1 = jnp.sum(jnp.where(eid == e1, rank, 0.0), axis=0, keepdims=True).astype(jnp.int32)
    r2 = jnp.sum(jnp.where(eid == e2, rank, 0.0), axis=0, keepdims=True).astype(jnp.int32)
    ri_ref[...] = jnp.concatenate([e1, e2, r1, r2, jnp.zeros((4, tm), jnp.int32)], axis=0)
    wt = jnp.concatenate([w1, w2, jnp.zeros((LANES - 2, tm), F32)], axis=0)
    rw_out_ref[...] = wt.T
    cnt = jnp.sum(jnp.where(onehot, 1.0, 0.0), axis=1, keepdims=True)
    cnt_ref[0] = jnp.broadcast_to(cnt, (N_EXPERTS, LANES)).astype(jnp.int32)


def _output_projection(o_parts, x_parts, mod, w_bf16, ln_g, ln_b, router_w_pad, router_b, tm=OUT_TM):
    nt = N_TOK // tm
    return pl.pallas_call(
        functools.partial(_outproj_kernel, n_o=len(o_parts), n_x=len(x_parts)),
        out_shape=(jax.ShapeDtypeStruct((N_TOK, D_MODEL), F32),
                   jax.ShapeDtypeStruct((N_TOK, N_CHUNKS, LANES), F32),
                   jax.ShapeDtypeStruct((8, N_TOK), jnp.int32),
                   jax.ShapeDtypeStruct((N_TOK, LANES), F32),
                   jax.ShapeDtypeStruct((nt, N_EXPERTS, LANES), jnp.int32)),
        grid=(nt,),
        in_specs=_stream_specs([a.shape[1] for a in o_parts], tm) + _stream_specs([a.shape[1] for a in x_parts], tm) + [
            pl.BlockSpec((1, 6, D_MODEL), lambda i: (_cond_row(i * tm), 0, 0)),
            pl.BlockSpec((D_MODEL, D_MODEL), lambda i: (0, 0)),
            pl.BlockSpec((1, D_MODEL), lambda i: (0, 0)),
            pl.BlockSpec((1, D_MODEL), lambda i: (0, 0)),
            pl.BlockSpec((D_MODEL, LANES), lambda i: (0, 0)),
            pl.BlockSpec((N_EXPERTS, 1), lambda i: (0, 0)),
        ],
        out_specs=(pl.BlockSpec((tm, D_MODEL), lambda i: (i, 0)),
                   pl.BlockSpec((tm, N_CHUNKS, LANES), lambda i: (i, 0, 0)),
                   pl.BlockSpec((8, tm), lambda i: (0, i)),
                   pl.BlockSpec((tm, LANES), lambda i: (i, 0)),
                   pl.BlockSpec((1, N_EXPERTS, LANES), lambda i: (i, 0, 0))),
        compiler_params=_params("arbitrary"),
        name="output_projection",
    )(*o_parts, *x_parts, mod, w_bf16, ln_g.reshape(1, D_MODEL), ln_b.reshape(1, D_MODEL),
      router_w_pad, router_b.reshape(N_EXPERTS, 1))


N_ASSIGN = 2 * N_TOK
MOE_TM = 256
MOE_ROWS = N_ASSIGN + N_EXPERTS * MOE_TM
MOE_TILES = MOE_ROWS // MOE_TM
DISPATCH_TM = 512
COMBINE_TM = 256


def _moe_plan(route_i, counts):
    cnt = counts[:, :, 0]
    total = jnp.sum(cnt, axis=0)
    padded = (total + MOE_TM - 1) // MOE_TM * MOE_TM
    seg_end = jnp.cumsum(padded)
    seg_start = seg_end - padded
    tile_base = seg_start[None, :] + jnp.cumsum(cnt, axis=0) - cnt
    base_tok = jnp.repeat(tile_base, OUT_TM, axis=0)
    eids = jnp.arange(N_EXPERTS, dtype=jnp.int32)[None, :]
    pos = [jnp.sum(jnp.where(route_i[k][:, None] == eids, base_tok, 0), axis=1) + route_i[2 + k]
           for k in range(2)]
    pos = jnp.concatenate(pos).astype(jnp.int32)
    tile_row0 = jnp.arange(MOE_TILES, dtype=jnp.int32) * MOE_TM
    tile_expert = jnp.minimum(jnp.sum(seg_end[None, :] <= tile_row0[:, None], axis=1), N_EXPERTS - 1)
    n_tiles = (seg_end[-1] // MOE_TM).reshape(1)
    last = MOE_ROWS - MOE_TM
    fill_rows = jnp.concatenate([jnp.minimum(seg_start + total, last),
                                 last - jnp.arange(N_EXPERTS, dtype=jnp.int32) * MOE_TM])
    return pos, tile_expert.astype(jnp.int32), n_tiles.astype(jnp.int32), fill_rows.astype(jnp.int32)


def _row_copy_wait(src_hbm, dst, sem, n_rows):
    pltpu.make_async_copy(src_hbm.at[pl.ds(0, n_rows)], dst.at[pl.ds(0, n_rows)], sem).wait()


def _dispatch_kernel(pos_ref, fill_ref, h_ref, xs_ref, zero_ref, sem, fill_sem):
    tm = h_ref.shape[0]
    t0 = pl.program_id(0) * tm

    @pl.when(pl.program_id(0) == 0)
    def _():
        zero_ref[...] = jnp.zeros_like(zero_ref)
        fills = [pltpu.make_async_copy(zero_ref, xs_ref.at[pl.ds(fill_ref[j], MOE_TM)], fill_sem)
                 for j in range(2 * N_EXPERTS)]
        for cp in fills[N_EXPERTS:]:
            cp.start()
        for cp in fills[N_EXPERTS:]:
            cp.wait()
        for cp in fills[:N_EXPERTS]:
            cp.start()
            cp.wait()

    def body(r, carry):
        for k in range(2):
            pltpu.make_async_copy(h_ref.at[r], xs_ref.at[pos_ref[k * N_TOK + t0 + r]], sem).start()
        return carry

    lax.fori_loop(0, tm, body, 0, unroll=8)
    for _ in range(2):
        _row_copy_wait(h_ref, xs_ref, sem, tm)


def _moe_dispatch(pos, fill_rows, h2_tiles):
    tm = DISPATCH_TM
    return pl.pallas_call(
        _dispatch_kernel,
        out_shape=jax.ShapeDtypeStruct((MOE_ROWS, N_CHUNKS, LANES), F32),
        grid_spec=pltpu.PrefetchScalarGridSpec(
            num_scalar_prefetch=2,
            grid=(N_TOK // tm,),
            in_specs=[pl.BlockSpec((tm, N_CHUNKS, LANES), lambda i, pos, fill: (i, 0, 0))],
            out_specs=pl.BlockSpec(memory_space=pl.ANY),
            scratch_shapes=[pltpu.VMEM((MOE_TM, N_CHUNKS, LANES), F32), pltpu.SemaphoreType.DMA,
                            pltpu.SemaphoreType.DMA],
        ),
        compiler_params=_params("arbitrary"),
        name="moe_dispatch",
    )(pos, fill_rows, h2_tiles)


def _expert_kernel(te_ref, nt_ref, xs_ref, wg_ref, wu_ref, wd_ref, ys_ref, wg16_ref, wu16_ref, wd16_ref):
    i = pl.program_id(0)

    @pl.when(i < nt_ref[0])
    def _():
        @pl.when((i == 0) | (te_ref[i] != te_ref[jnp.maximum(i - 1, 0)]))
        def _():
            wg16_ref[...] = wg_ref[0, 0].astype(BF16)
            wu16_ref[...] = wu_ref[0, 0].astype(BF16)
            wd16_ref[...] = wd_ref[0, 0].astype(BF16)

        x = _load_token_tiles(xs_ref).astype(BF16)
        he = _silu(_dot(x, wg16_ref[...])) * _dot(x, wu16_ref[...])
        _store_token_tiles(ys_ref, _dot(he.astype(BF16), wd16_ref[...]))

    @pl.when(i >= nt_ref[0])
    def _():
        ys_ref[...] = jnp.zeros_like(ys_ref)


def _moe_experts(tile_expert, n_tiles, xs, layer, wg, wu, wd):
    row_blk = lambda i, te, nt: (jnp.minimum(i, nt[0] - 1), 0, 0)
    w_blk = lambda i, te, nt: (layer, te[i], 0, 0)
    return pl.pallas_call(
        _expert_kernel,
        out_shape=jax.ShapeDtypeStruct((MOE_ROWS, N_CHUNKS, LANES), F32),
        grid_spec=pltpu.PrefetchScalarGridSpec(
            num_scalar_prefetch=2,
            grid=(MOE_TILES,),
            in_specs=[
                pl.BlockSpec((MOE_TM, N_CHUNKS, LANES), row_blk),
                pl.BlockSpec((1, 1, D_MODEL, EXPERT_FF), w_blk),
                pl.BlockSpec((1, 1, D_MODEL, EXPERT_FF), w_blk),
                pl.BlockSpec((1, 1, EXPERT_FF, D_MODEL), w_blk),
            ],
            out_specs=pl.BlockSpec((MOE_TM, N_CHUNKS, LANES), lambda i, te, nt: (i, 0, 0)),
            scratch_shapes=[pltpu.VMEM((D_MODEL, EXPERT_FF), BF16), pltpu.VMEM((D_MODEL, EXPERT_FF), BF16),
                            pltpu.VMEM((EXPERT_FF, D_MODEL), BF16)],
        ),
        compiler_params=_params("arbitrary"),
        name="moe_experts",
    )(tile_expert, n_tiles, xs, wg, wu, wd)


def _combine_kernel(pos_ref, gw_ref, x1_ref, mod_ref, lng_ref, lnb_ref, ys_ref, *rest, split):
    out_refs, (buf_ref, sem) = rest[:-2], rest[-2:]
    i = pl.program_id(0)
    n = pl.num_programs(0)
    tm = COMBINE_TM

    def start_row(tile, to_slot, r):
        for k in range(2):
            p = pos_ref[k * N_TOK + tile * tm + r]
            pltpu.make_async_copy(ys_ref.at[p], buf_ref.at[to_slot, k, r], sem.at[to_slot]).start()

    def wait_slot(s):
        for k in range(2):
            _row_copy_wait(ys_ref, buf_ref.at[s, k], sem.at[s], tm)

    def issue(tile, to_slot):
        def body(r, carry):
            start_row(tile, to_slot, r)
            return carry
        lax.fori_loop(0, tm, body, 0, unroll=8)

    @pl.when(i == 0)
    def _():
        issue(0, 0)

    @pl.when(i + 1 < n)
    def _():
        issue(i + 1, (i + 1) % 2)

    slot = i % 2
    wait_slot(slot)
    gw = gw_ref[...]
    f = gw[:, 0:1] * _load_token_tiles(buf_ref.at[slot, 0]) + gw[:, 1:2] * _load_token_tiles(buf_ref.at[slot, 1])
    m = mod_ref[0]
    out = _layer_norm(ALPHA * x1_ref[...] + m[5:6] * f, lng_ref[...], lnb_ref[...])

    if split:
        prompt_ref, sample_ref = out_refs

        @pl.when(i < N_PROMPT // tm)
        def _():
            prompt_ref[...] = out

        @pl.when(i >= N_PROMPT // tm)
        def _():
            sample_ref[...] = out
    else:
        out_refs[0][...] = out


def _moe_combine(pos, gate_w, x1, mod, ln_g, ln_b, ys, split):
    tm = COMBINE_TM
    np_blk = N_PROMPT // tm
    if split:
        out_shape = (jax.ShapeDtypeStruct((N_PROMPT, D_MODEL), F32), jax.ShapeDtypeStruct((N_SAMPLE, D_MODEL), F32))
        out_specs = (pl.BlockSpec((tm, D_MODEL), lambda i, pos: (jnp.minimum(i, np_blk - 1), 0)),
                     pl.BlockSpec((tm, D_MODEL), lambda i, pos: (jnp.maximum(i - np_blk, 0), 0)))
    else:
        out_shape = jax.ShapeDtypeStruct((N_TOK, D_MODEL), F32)
        out_specs = pl.BlockSpec((tm, D_MODEL), lambda i, pos: (i, 0))
    return pl.pallas_call(
        functools.partial(_combine_kernel, split=split),
        out_shape=out_shape,
        grid_spec=pltpu.PrefetchScalarGridSpec(
            num_scalar_prefetch=1,
            grid=(N_TOK // tm,),
            in_specs=[
                pl.BlockSpec((tm, LANES), lambda i, pos: (i, 0)),
                pl.BlockSpec((tm, D_MODEL), lambda i, pos: (i, 0)),
                pl.BlockSpec((1, 6, D_MODEL), lambda i, pos: (_cond_row(i * tm), 0, 0)),
                pl.BlockSpec((1, D_MODEL), lambda i, pos: (0, 0)),
                pl.BlockSpec((1, D_MODEL), lambda i, pos: (0, 0)),
                pl.BlockSpec(memory_space=pl.ANY),
            ],
            out_specs=out_specs,
            scratch_shapes=[pltpu.VMEM((2, 2, tm, N_CHUNKS, LANES), F32), pltpu.SemaphoreType.DMA((2,))],
        ),
        compiler_params=_params("arbitrary"),
        name="moe_combine",
    )(pos, gate_w, x1, mod, ln_g.reshape(1, D_MODEL), ln_b.reshape(1, D_MODEL), ys)


def _moe_ffn(h2_tiles, route_i, gate_w, counts, layer, wg, wu, wd, x1, mod, ln_g, ln_b, split=False):
    pos, tile_expert, n_tiles, fill_rows = _moe_plan(route_i, counts)
    xs = _moe_dispatch(pos, fill_rows, h2_tiles)
    ys = _moe_experts(tile_expert, n_tiles, xs, layer, wg, wu, wd)
    return _moe_combine(pos, gate_w, x1, mod, ln_g, ln_b, ys, split)


DN_GATE_COL0 = DN_CONV_CH
N_DIRS = 2


def _dn_prep_kernel(q_ref, k_ref, v_ref, wq_ref, wk_ref, wv_ref, qo_ref, ko_ref, vo_ref):
    length = q_ref.shape[0]
    row = lax.broadcasted_iota(jnp.int32, (length, DN_DK), 0)

    def conv(x, w):
        prev = jnp.where(row == 0, 0.0, pltpu.roll(x, 1, axis=0))
        nxt = jnp.where(row == length - 1, 0.0, pltpu.roll(x, length - 1, axis=0))
        return _silu(prev * w[0:1] + x * w[1:2] + nxt * w[2:3])

    def l2n(x):
        return x * lax.rsqrt(jnp.sum(x * x, axis=-1, keepdims=True) + 1e-6)

    for h in range(DN_HEADS):
        sl = slice(h * DN_DK, (h + 1) * DN_DK)
        q, k, v = (r[:, sl].astype(F32) for r in (q_ref, k_ref, v_ref))
        qo_ref[h] = (l2n(conv(q, wq_ref[:, sl])) * (DN_DK ** -0.5)).astype(qo_ref.dtype)
        ko_ref[h] = l2n(conv(k, wk_ref[:, sl])).astype(ko_ref.dtype)
        vo_ref[h] = conv(v, wv_ref[:, sl]).astype(vo_ref.dtype)


def _dn_prep(z1, conv_w, nb, length, row_blk0):
    shp = jax.ShapeDtypeStruct((DN_HEADS, nb * length, DN_DK), BF16)
    blk = lambda part: pl.BlockSpec((length, DN_QK_WIDTH), lambda b: (row_blk0 + b, part))
    wblk = lambda part: pl.BlockSpec((3, DN_QK_WIDTH), lambda b: (0, part))
    oblk = pl.BlockSpec((DN_HEADS, length, DN_DK), lambda b: (0, b, 0))
    return pl.pallas_call(
        _dn_prep_kernel,
        out_shape=(shp, shp, shp),
        grid=(nb,),
        in_specs=[blk(0), blk(1), blk(2), wblk(0), wblk(1), wblk(2)],
        out_specs=(oblk, oblk, oblk),
        compiler_params=_params("arbitrary"),
        name="deltanet_prep",
    )(z1, z1, z1, conv_w, conv_w, conv_w)


def _dn_gates_kernel(ab_ref, alog_ref, dtb_ref, o_ref):
    length = ab_ref.shape[0]
    x = ab_ref[...]
    a = x + dtb_ref[...]
    softplus = jnp.maximum(a, 0.0) + jnp.log1p(jnp.exp(-jnp.abs(a)))
    g = -jnp.exp(alog_ref[...]) * softplus
    beta = jax.nn.sigmoid(x)
    ri = lax.broadcasted_iota(jnp.int32, (DN_CHUNK, DN_CHUNK), 0)
    ci = lax.broadcasted_iota(jnp.int32, (DN_CHUNK, DN_CHUNK), 1)
    tril = (ri >= ci).astype(F32)
    triu = (ri <= ci).astype(F32)
    lane = lax.broadcasted_iota(jnp.int32, (DN_CHUNK, LANES), 1)
    for c in range(length // DN_CHUNK):
        rows = slice(c * DN_CHUNK, (c + 1) * DN_CHUNK)
        gch = g[rows]
        pre = jnp.dot(tril, gch, precision=lax.Precision.HIGHEST, preferred_element_type=F32)
        suf = jnp.dot(triu, gch, precision=lax.Precision.HIGHEST, preferred_element_type=F32)
        gc = jnp.where(lane < DN_HEADS, pre, suf)
        o_ref[rows, :] = jnp.where(lane < N_DIRS * DN_HEADS, gc, beta[rows])


def _dn_gates(z_ab, a_log, dt_bias, nb, length, row_blk0):
    pad = LANES - N_DIRS * DN_HEADS
    alog = jnp.pad(a_log.reshape(1, -1).astype(F32), ((0, 0), (0, pad)))
    dtb = jnp.pad(dt_bias.reshape(1, -1).astype(F32), ((0, 0), (0, pad)))
    return pl.pallas_call(
        _dn_gates_kernel,
        out_shape=jax.ShapeDtypeStruct((nb * length, LANES), F32),
        grid=(nb,),
        in_specs=[
            pl.BlockSpec((length, LANES), lambda b: (row_blk0 + b, 0)),
            pl.BlockSpec((1, LANES), lambda b: (0, 0)),
            pl.BlockSpec((1, LANES), lambda b: (0, 0)),
        ],
        out_specs=pl.BlockSpec((length, LANES), lambda b: (b, 0)),
        compiler_params=_params("arbitrary"),
        name="deltanet_gates",
    )(z_ab, alog, dtb)


def _delta_chunk(q, k, v, beta, gcol, grow, s, lower):
    ri = lax.broadcasted_iota(jnp.int32, (DN_CHUNK, DN_CHUNK), 0)
    ci = lax.broadcasted_iota(jnp.int32, (DN_CHUNK, DN_CHUNK), 1)
    incl = (ri >= ci) if lower else (ri <= ci)
    strict = (ri > ci) if lower else (ri < ci)
    decay = jnp.exp(jnp.where(incl, gcol - grow, -jnp.inf))
    kb = k * beta
    kq = _dot_nt(jnp.concatenate([kb, q], axis=0).astype(BF16), k.astype(BF16))
    kk, a_qk = kq[:DN_CHUNK], kq[DN_CHUNK:] * decay
    yield
    tri_l = jnp.where(strict, kk * decay, 0.0)

    def off_block(size):
        same = (ri // (2 * size)) == (ci // (2 * size))
        rpar, cpar = (ri // size) % 2, (ci // size) % 2
        return same & ((rpar == 1) & (cpar == 0) if lower else (rpar == 0) & (cpar == 1))

    p = jnp.where(ri == ci, 1.0, 0.0) - jnp.where(off_block(1), tri_l, 0.0)
    size = 2
    while size < DN_CHUNK:
        p16 = p.astype(BF16)
        pc = _dot(p16, jnp.where(off_block(size), tri_l, 0.0).astype(BF16))
        yield
        p = p - _dot(pc.astype(BF16), p16)
        yield
        size *= 2
    eg = jnp.exp(gcol)
    uw = _dot(p.astype(BF16), jnp.concatenate([v * beta, kb * eg], axis=1).astype(BF16))
    yield
    u, w = uw[:, :DN_DV], uw[:, DN_DV:]
    ws = _dot(jnp.concatenate([w, q * eg], axis=0).astype(BF16), s.astype(BF16))
    v_new, o_inter = u - ws[:DN_CHUNK], ws[DN_CHUNK:]
    yield
    v_new16 = v_new.astype(BF16)
    o = o_inter + _dot(a_qk.astype(BF16), v_new16)
    g_last = gcol[DN_CHUNK - 1:DN_CHUNK] if lower else gcol[0:1]
    kd = k * jnp.exp(g_last - gcol)
    s_new = s * jnp.exp(g_last) + _dot_tn(kd.astype(BF16), v_new16)
    return o, s_new


def _run_interleaved(chains):
    results = [None] * len(chains)
    active = list(enumerate(chains))
    while active:
        still = []
        for idx, gen in active:
            try:
                next(gen)
                still.append((idx, gen))
            except StopIteration as stop:
                results[idx] = stop.value
        active = still
    return results


def _dn_scan_kernel(*refs, has_init, want_final):
    (qf_ref, kf_ref, vf_ref, gf_ref, qb_ref, kb_ref, vb_ref, gb_ref), rest = refs[:8], refs[8:]
    if has_init:
        s0_ref, rest = rest[0], rest[1:]
    of_ref, ob_ref = rest[0], rest[1]
    rest = rest[2:]
    if want_final:
        sf_ref, rest = rest[0], rest[1:]
    s_ref = rest[0]
    c = pl.program_id(1)
    nc = pl.num_programs(1)

    @pl.when(c == 0)
    def _():
        for d in range(N_DIRS):
            for h in range(DN_HEADS):
                if has_init:
                    s_ref[d * DN_HEADS + h] = s0_ref[0, d, h]
                else:
                    s_ref[d * DN_HEADS + h] = jnp.zeros((DN_DK, DN_DV), F32)

    dirs = ((qf_ref, kf_ref, vf_ref, gf_ref, of_ref, True), (qb_ref, kb_ref, vb_ref, gb_ref, ob_ref, False))
    chains = []
    for d, (q_ref, k_ref, v_ref, g_ref, o_ref, lower) in enumerate(dirs):
        gates = g_ref[...]
        gates_t = gates.T
        for h in range(DN_HEADS):
            j = d * DN_HEADS + h
            jb = N_DIRS * DN_HEADS + j
            chains.append(_delta_chunk(q_ref[h].astype(F32), k_ref[h].astype(F32), v_ref[h].astype(F32),
                                       gates[:, jb:jb + 1], gates[:, j:j + 1], gates_t[j:j + 1, :],
                                       s_ref[j], lower))
    results = _run_interleaved(chains)
    for d, (_, _, _, _, o_ref, _) in enumerate(dirs):
        for h in range(DN_HEADS):
            j = d * DN_HEADS + h
            o, s_new = results[j]
            o_ref[:, h * DN_DV:(h + 1) * DN_DV] = o.astype(o_ref.dtype)
            s_ref[j] = s_new

    if want_final:
        @pl.when(c == nc - 1)
        def _():
            for d in range(N_DIRS):
                for h in range(DN_HEADS):
                    sf_ref[0, d, h] = s_ref[d * DN_HEADS + h]


def _dn_scan(qn, kn, vn, gates, s0, nb, length, want_final):
    nc = length // DN_CHUNK
    has_init = s0 is not None
    fwd = lambda b, c: (0, b * nc + c, 0)
    bwd = lambda b, c: (0, b * nc + (nc - 1 - c), 0)
    hm = lambda im: pl.BlockSpec((DN_HEADS, DN_CHUNK, DN_DK), im)
    in_specs = [hm(fwd), hm(fwd), hm(fwd), pl.BlockSpec((DN_CHUNK, LANES), lambda b, c: (b * nc + c, 0)),
                hm(bwd), hm(bwd), hm(bwd), pl.BlockSpec((DN_CHUNK, LANES), lambda b, c: (b * nc + (nc - 1 - c), 0))]
    args = [qn, kn, vn, gates, qn, kn, vn, gates]
    state_blk = pl.BlockSpec((1, N_DIRS, DN_HEADS, DN_DK, DN_DV), lambda b, c: (b, 0, 0, 0, 0))
    if has_init:
        in_specs.append(state_blk)
        args.append(s0)
    o_shape = jax.ShapeDtypeStruct((nb * length, DN_WIDTH), BF16)
    out_shape = [o_shape, o_shape]
    out_specs = [pl.BlockSpec((DN_CHUNK, DN_WIDTH), lambda b, c: (b * nc + c, 0)),
                 pl.BlockSpec((DN_CHUNK, DN_WIDTH), lambda b, c: (b * nc + (nc - 1 - c), 0))]
    if want_final:
        out_shape.append(jax.ShapeDtypeStruct((nb, N_DIRS, DN_HEADS, DN_DK, DN_DV), F32))
        out_specs.append(state_blk)
    return pl.pallas_call(
        functools.partial(_dn_scan_kernel, has_init=has_init, want_final=want_final),
        out_shape=tuple(out_shape),
        grid=(nb, nc),
        in_specs=in_specs,
        out_specs=tuple(out_specs),
        scratch_shapes=[pltpu.VMEM((N_DIRS * DN_HEADS, DN_DK, DN_DV), F32)],
        compiler_params=_params("arbitrary", "arbitrary"),
        name="deltanet_scan",
    )(*args)


DN_MERGE_TM = 256


def _dn_merge_kernel(of_ref, ob_ref, gate_ref, g_ref, o_ref):
    for h in range(DN_HEADS):
        sl = slice(h * DN_DV, (h + 1) * DN_DV)
        o = of_ref[:, sl].astype(F32) + ob_ref[:, sl].astype(F32)
        o = o * lax.rsqrt(jnp.mean(o * o, axis=-1, keepdims=True) + EPS) * g_ref[...]
        o_ref[:, sl] = (o * _silu(gate_ref[:, sl].astype(F32))).astype(o_ref.dtype)


def _dn_merge(o_f, o_b, z1, onorm_g, n_rows, row0):
    tm = DN_MERGE_TM
    blk = pl.BlockSpec((tm, DN_WIDTH), lambda i: (i, 0))
    return pl.pallas_call(
        _dn_merge_kernel,
        out_shape=jax.ShapeDtypeStruct((n_rows, DN_WIDTH), BF16),
        grid=(n_rows // tm,),
        in_specs=[blk, blk,
                  pl.BlockSpec((tm, DN_WIDTH), lambda i: (row0 // tm + i, DN_GATE_COL0 // DN_WIDTH)),
                  pl.BlockSpec((1, DN_DV), lambda i: (0, 0))],
        out_specs=blk,
        compiler_params=_params("arbitrary"),
        name="deltanet_merge",
    )(o_f, o_b, z1, onorm_g.reshape(1, DN_DV))


def _deltanet_stream(z1, z_ab, conv_w, a_log, dt_bias, onorm_g, s0, nb, length, row0, want_final):
    row_blk0 = row0 // length
    qn, kn, vn = _dn_prep(z1, conv_w, nb, length, row_blk0)
    gates = _dn_gates(z_ab, a_log, dt_bias, nb, length, row_blk0)
    outs = _dn_scan(qn, kn, vn, gates, s0, nb, length, want_final)
    merged = _dn_merge(outs[0], outs[1], z1, onorm_g, nb * length, row0)
    return merged, (outs[2] if want_final else None)


def kernel(x_prompt, x_sample, cache_nat_k, cache_nat_v, cache_diff_k, cache_diff_v, state_delta, c, c_ctx, w_in_even, w_out_even, nat_bias, diff_lam, diff_subln, w_in_odd, conv_odd, a_log_odd, dt_bias_odd, onorm_odd, w_out_odd, ada_w, ada_b, ln_g, ln_b, router_w, router_b, moe_wg, moe_wu, moe_wd):
    x_parts = (x_prompt.reshape(N_PROMPT, D_MODEL), x_sample.reshape(N_SAMPLE, D_MODEL))
    cond =jnp.concatenate([c_ctx[None], c, jnp.zeros((N_COND - 1 - DEC_BATCH, D_MODEL), F32)], axis=0)
    mod = _ada_modulation(cond, ada_w, ada_b).reshape(DEPTH, N_COND, 6, D_MODEL)
    router_w_pad = jnp.pad(router_w, ((0, 0), (0, LANES - N_EXPERTS)))

    lam_init = 0.8 - 0.6 * math.exp(-0.3 * 0)
    z = _input_projection(x_parts, mod[0], w_in_even[0].astype(BF16))
    new_nat_k = z[:N_PROMPT, _NAT_K0:_NAT_V0].reshape(BATCH, 1, SEQ, NAT_HEADS, HEAD_DIM)
    new_nat_v = z[:N_PROMPT, _NAT_V0:_DIFF_Q0].reshape(BATCH, 1, SEQ, NAT_HEADS, HEAD_DIM)
    new_diff_k = z[:N_PROMPT, _DIFF_K0:_DIFF_V0].reshape(BATCH, 1, SEQ, DIFF_HEADS, 2, HEAD_DIM)
    new_diff_v = z[:N_PROMPT, _DIFF_V0:].reshape(BATCH, 1, SEQ, DIFF_HEADS, 2 * HEAD_DIM)
    o_ctx = _context_attention(z, diff_lam[0], diff_subln[0], lam_init)
    o_nat = _nat_latent_attention(z, cache_nat_k[:, 0].reshape(DEC_BATCH, PAST_LEN, NAT_WIDTH),
                                  cache_nat_v[:, 0].reshape(DEC_BATCH, PAST_LEN, NAT_WIDTH),
                                  _nat_bias_table(nat_bias[0]))
    o_diff = _diff_latent_attention(z, cache_diff_k[:, 0].reshape(DEC_BATCH, PAST_LEN, DIFF_WIDTH),
                                    cache_diff_v[:, 0].reshape(DEC_BATCH, PAST_LEN, DIFF_WIDTH),
                                    diff_lam[0], diff_subln[0], lam_init)
    x1, *routed = _output_projection((o_ctx, o_nat, o_diff), x_parts, mod[0], w_out_even[0].astype(BF16),
                                     ln_g[0, 0], ln_b[0, 0], router_w_pad, router_b)
    x = _moe_ffn(*routed, 0, moe_wg, moe_wu, moe_wd, x1, mod[0], ln_g[0, 1], ln_b[0, 1])

    w1 = w_in_odd[0]
    w1 = jnp.concatenate([w1[:, :DN_CONV_CH], w1[:, DN_CONV_CH + 4 * DN_HEADS:],
                          w1[:, DN_CONV_CH:DN_CONV_CH + 4 * DN_HEADS],
                          jnp.zeros((D_MODEL, LANES - 4 * DN_HEADS), F32)], axis=1).astype(BF16)
    z1, z_ab = _input_projection((x,), mod[1], w1, f32_tail=LANES)
    o_p, new_state = _deltanet_stream(z1, z_ab, conv_odd[0], a_log_odd[0], dt_bias_odd[0], onorm_odd[0],
                                      None, BATCH, SEQ, 0, True)
    o_s, _ = _deltanet_stream(z1, z_ab, conv_odd[0], a_log_odd[0], dt_bias_odd[0], onorm_odd[0],
                              state_delta[:, 0], DEC_BATCH, DEC_SEQ, N_PROMPT, False)
    x1, *routed = _output_projection((o_p, o_s), (x,), mod[1], w_out_odd[0].astype(BF16), ln_g[1, 0], ln_b[1, 0],
                                     router_w_pad, router_b)
    y_prompt, y_sample = _moe_ffn(*routed, 1, moe_wg, moe_wu, moe_wd, x1, mod[1], ln_g[1, 1], ln_b[1, 1],
                                  split=True)
    y_prompt = y_prompt.reshape(BATCH, SEQ, D_MODEL)
    y_sample = y_sample.reshape(DEC_BATCH, DEC_SEQ, D_MODEL)
    return (y_prompt, y_sample, new_nat_k, new_nat_v, new_diff_k, new_diff_v, new_state[:, None])
```

```python
import functools
import math

import jax
import jax.numpy as jnp
import numpy as np
from jax import lax
from jax.experimental import pallas as pl
from jax.experimental.pallas import tpu as pltpu

F32 = jnp.float32
BF16 = jnp.bfloat16

D_MODEL = 1024
BATCH = 16
SEQ = 256
DEC_BATCH = 8
DEC_SEQ = 1024
PAST_LEN = 256
GRID_W = 64
N_ROWS = DEC_SEQ // GRID_W

HEAD_DIM = 64
NAT_HEADS = 8
NAT_WIN_H = 8
NAT_WIN_W = 16
DIFF_HEADS = 4
NAT_WIDTH = NAT_HEADS * HEAD_DIM
DIFF_WIDTH = DIFF_HEADS * 2 * HEAD_DIM
EVEN_IN = 3 * NAT_WIDTH + 3 * DIFF_WIDTH
ROPE_THETA = 10000.0

DN_HEADS = 8
DN_DK = 128
DN_DV = 128
DN_QK_WIDTH = DN_HEADS * DN_DK
DN_WIDTH = DN_HEADS * DN_DV
DN_CONV_CH = 2 * DN_QK_WIDTH + DN_WIDTH
DN_CHUNK = 64

N_EXPERTS = 16
N_GROUPS = 4
EXPERTS_PER_GROUP = N_EXPERTS // N_GROUPS
EXPERT_FF = 512

DEPTH = 2
ALPHA = (2 * DEPTH) ** 0.25
EPS = 1e-5

N_PROMPT = BATCH * SEQ
N_SAMPLE = DEC_BATCH * DEC_SEQ
N_TOK = N_PROMPT + N_SAMPLE
N_COND = 16

VMEM_LIMIT = 56 * 1024 * 1024
LANES = 128
N_CHUNKS = D_MODEL // LANES


def _params(*sem):
    return pltpu.CompilerParams(dimension_semantics=sem, vmem_limit_bytes=VMEM_LIMIT)


def _cond_row(row0):
    return jnp.where(row0 < N_PROMPT, 0, 1 + (row0 - N_PROMPT) // DEC_SEQ)


def _silu(x):
    return x * jax.nn.sigmoid(x)


def _layer_norm(r, g, b):
    mu = jnp.mean(r, axis=-1, keepdims=True)
    xc = r - mu
    var = jnp.mean(xc * xc, axis=-1, keepdims=True)
    return xc * lax.rsqrt(var + EPS) * g + b


def _dot(a, b):
    return jnp.dot(a, b, preferred_element_type=F32)


def _dot_nt(a, b):
    return lax.dot_general(a, b, (((1,), (1,)), ((), ())), preferred_element_type=F32)


def _dot_tn(a, b):
    return lax.dot_general(a, b, (((0,), (0,)), ((), ())), preferred_element_type=F32)


ADA_TN = 1536


def _ada_kernel(c_ref, w_ref, b_ref, o_ref):
    s = _silu(c_ref[...])
    o_ref[0] = _dot(s.astype(BF16), w_ref[0].astype(BF16)) + b_ref[0]


def _ada_modulation(cond, ada_w, ada_b):
    n = 6 * D_MODEL
    return pl.pallas_call(
        _ada_kernel,
        out_shape=jax.ShapeDtypeStruct((DEPTH, N_COND, n), F32),
        grid=(DEPTH, n // ADA_TN),
        in_specs=[
            pl.BlockSpec((N_COND, D_MODEL), lambda l, j: (0, 0)),
            pl.BlockSpec((1, D_MODEL, ADA_TN), lambda l, j: (l, 0, j)),
            pl.BlockSpec((1, 1, ADA_TN), lambda l, j: (l, 0, j)),
        ],
        out_specs=pl.BlockSpec((1, N_COND, ADA_TN), lambda l, j: (l, 0, j)),
        compiler_params=_params("arbitrary", "arbitrary"),
        name="ada_modulation",
    )(cond, ada_w, ada_b.reshape(DEPTH, 1, n))


PROJ_TM = 512


def _stream_specs(widths, tm):
    if len(widths) == 1:
        return [pl.BlockSpec((tm, widths[0]), lambda i: (i, 0))]
    n_ctx = N_PROMPT // tm
    return ([pl.BlockSpec((tm, widths[0]), lambda i: (jnp.minimum(i, n_ctx - 1), 0))]
            + [pl.BlockSpec((tm, w), lambda i: (jnp.maximum(i - n_ctx, 0), 0)) for w in widths[1:]])


def _stream_tile(refs, tm):
    if len(refs) == 1:
        return refs[0][...]
    latent = [r[...] for r in refs[1:]]
    latent = latent[0] if len(latent) == 1 else jnp.concatenate(latent, axis=1)
    return jnp.where(pl.program_id(0) * tm < N_PROMPT, refs[0][...], latent)


def _inproj_kernel(*refs, n_x):
    x_refs, (mod_ref, w_ref, o_ref), tail_refs = refs[:n_x], refs[n_x:n_x + 3], refs[n_x + 3:]
    m = mod_ref[0]
    h = _stream_tile(x_refs, o_ref.shape[0]) * (1.0 + m[1:2]) + m[0:1]
    z = _dot(h.astype(BF16), w_ref[...])
    n_main = o_ref.shape[1]
    o_ref[...] = z[:, :n_main].astype(o_ref.dtype)
    if tail_refs:
        tail_refs[0][...] = z[:, n_main:]


def _input_projection(x_parts, mod, w_bf16, tm=PROJ_TM, f32_tail=0):
    n = w_bf16.shape[1] - f32_tail
    out_shape = [jax.ShapeDtypeStruct((N_TOK, n), BF16 if f32_tail else F32)]
    out_specs = [pl.BlockSpec((tm, n), lambda i: (i, 0))]
    if f32_tail:
        out_shape.append(jax.ShapeDtypeStruct((N_TOK, f32_tail), F32))
        out_specs.append(pl.BlockSpec((tm, f32_tail), lambda i: (i, 0)))
    outs = pl.pallas_call(
        functools.partial(_inproj_kernel, n_x=len(x_parts)),
        out_shape=tuple(out_shape),
        grid=(N_TOK // tm,),
        in_specs=_stream_specs([a.shape[1] for a in x_parts], tm) + [
            pl.BlockSpec((1, 6, D_MODEL), lambda i: (_cond_row(i * tm), 0, 0)),
            pl.BlockSpec((D_MODEL, n + f32_tail), lambda i: (0, 0)),
        ],
        out_specs=tuple(out_specs),
        compiler_params=_params("arbitrary"),
        name="input_projection",
    )(*x_parts, mod, w_bf16)
    return outs if f32_tail else outs[0]


_NAT_Q0, _NAT_K0, _NAT_V0 = 0, NAT_WIDTH, 2 * NAT_WIDTH
_DIFF_Q0 = 3 * NAT_WIDTH
_DIFF_K0 = _DIFF_Q0 + DIFF_WIDTH
_DIFF_V0 = _DIFF_K0 + DIFF_WIDTH
ATTN_SCALE = HEAD_DIM ** -0.5


def _diff_lambda(lam_ref, lam_init):
    lp = lam_ref[...]
    return (jnp.exp(jnp.sum(lp[0:1] * lp[1:2], axis=-1, keepdims=True))
            - jnp.exp(jnp.sum(lp[2:3] * lp[3:4], axis=-1, keepdims=True)) + lam_init)


def _scaled_q(q):
    assert math.log2(HEAD_DIM) % 2 == 0
    return (q * ATTN_SCALE).astype(BF16)


def _with_ones(v):
    width = v.shape[1] if v.shape[1] % LANES == 0 else LANES - v.shape[1] % LANES
    return jnp.concatenate([v, jnp.ones((v.shape[0], width), v.dtype)], axis=1)


def _sub_norm(o, g, lam_init):
    ms = jnp.mean(o * o, axis=-1, keepdims=True)
    return o * lax.rsqrt(ms + EPS) * g * (1.0 - lam_init)


def _ctx_attn_kernel(z_ref, lam_ref, subg_ref, o_ref, *, lam_init):
    lam = _diff_lambda(lam_ref, lam_init)

    def softmax_av(q0, k0, v0, v_width):
        q = _scaled_q(z_ref[:, q0:q0 + HEAD_DIM])
        k = z_ref[:, k0:k0 + HEAD_DIM].astype(BF16)
        s = _dot_nt(q, k)
        yield
        e = jnp.exp(s - jnp.max(s, axis=-1, keepdims=True))
        yield
        av = _dot(e.astype(BF16), _with_ones(z_ref[:, v0:v0 + v_width].astype(BF16)))
        yield
        return av[:, :v_width] / av[:, v_width:v_width + 1]

    chains = [softmax_av(_NAT_Q0 + h * HEAD_DIM, _NAT_K0 + h * HEAD_DIM, _NAT_V0 + h * HEAD_DIM, HEAD_DIM)
              for h in range(NAT_HEADS)]
    chains += [softmax_av(_DIFF_Q0 + i * HEAD_DIM, _DIFF_K0 + i * HEAD_DIM,
                          _DIFF_V0 + (i // 2) * 2 * HEAD_DIM, 2 * HEAD_DIM)
               for i in range(2 * DIFF_HEADS)]
    outs = _run_interleaved(chains)
    for h in range(NAT_HEADS):
        o_ref[:, h * HEAD_DIM:(h + 1) * HEAD_DIM] = outs[h].astype(o_ref.dtype)
    for h in range(DIFF_HEADS):
        c = NAT_WIDTH + h * 2 * HEAD_DIM
        o = outs[NAT_HEADS + 2 * h] - lam * outs[NAT_HEADS + 2 * h + 1]
        o_ref[:, c:c + 2 * HEAD_DIM] = _sub_norm(o, subg_ref[...], lam_init).astype(o_ref.dtype)


def _context_attention(z, lam_p, subln_g, lam_init):
    return pl.pallas_call(
        functools.partial(_ctx_attn_kernel, lam_init=lam_init),
        out_shape=jax.ShapeDtypeStruct((N_PROMPT, D_MODEL), BF16),
        grid=(BATCH,),
        in_specs=[
            pl.BlockSpec((SEQ, EVEN_IN), lambda b: (b, 0)),
            pl.BlockSpec((4, HEAD_DIM), lambda b: (0, 0)),
            pl.BlockSpec((1, 2 * HEAD_DIM), lambda b: (0, 0)),
        ],
        out_specs=pl.BlockSpec((SEQ, D_MODEL), lambda b: (b, 0)),
        compiler_params=_params("arbitrary"),
        name="context_attention",
    )(z, lam_p, subln_g.reshape(1, 2 * HEAD_DIM))


NAT_TQ = 512
NAT_KEYS = 12 * GRID_W
_SAMPLE_BLK0 = N_PROMPT // DEC_SEQ


NAT_SPAN = NAT_KEYS // GRID_W
NAT_SPAN_PAD = 16


def _nat_bias_diagonals(rel_bias):
    rows = N_ROWS
    kh = min(NAT_WIN_H, rows)
    row_onehot = np.zeros((2 * NAT_WIN_H - 1, rows, NAT_SPAN), np.float32)
    for r in range(rows):
        start = min(max(r - kh // 2, 0), rows - kh)
        a0 = _nat_key0(r * GRID_W // NAT_TQ) // GRID_W
        assert a0 <= start and start + kh <= a0 + NAT_SPAN
        for a in range(start, start + kh):
            row_onehot[a - r + NAT_WIN_H - 1, r, a - a0] = 1.0
    lane_onehot = np.zeros((2 * NAT_WIN_W - 1, LANES), np.float32)
    for lane in range(2 * GRID_W - 1):
        lane_onehot[np.clip(lane - (GRID_W - 1), 1 - NAT_WIN_W, NAT_WIN_W - 1) + NAT_WIN_W - 1, lane] = 1.0
    hi = lax.Precision.HIGHEST
    by_row = jnp.einsum('hed,era->hrad', rel_bias.astype(F32), row_onehot, precision=hi)
    diag = jnp.einsum('hrad,dl->hral', by_row, lane_onehot, precision=hi)
    diag = jnp.where((row_onehot.sum(0) > 0)[None, :, :, None], diag, -jnp.inf)
    return jnp.pad(diag, ((0, 0), (0, 0), (0, NAT_SPAN_PAD - NAT_SPAN), (0, 0)))


def _nat_bias_tiles(diag_ref, bias_ref):
    shape = (GRID_W, LANES)
    q = lax.broadcasted_iota(jnp.int32, shape, 0)
    lane = lax.broadcasted_iota(jnp.int32, shape, 1)
    k = lane % GRID_W
    c0 = jnp.clip(q - NAT_WIN_W // 2, 0, GRID_W - NAT_WIN_W)
    in_cols = (k >= c0) & (k < c0 + NAT_WIN_W)
    for j in range(2):
        for r in range(NAT_TQ // GRID_W):
            for pair in range(NAT_SPAN // 2):
                even = jnp.broadcast_to(diag_ref[j, r, 2 * pair:2 * pair + 1, :], shape)
                odd = jnp.broadcast_to(diag_ref[j, r, 2 * pair + 1:2 * pair + 2, :], shape)
                even = pltpu.roll(even, GRID_W + 1, 1, stride=1, stride_axis=0)
                odd = pltpu.roll(odd, 1, 1, stride=1, stride_axis=0)
                tile = jnp.where(lane < GRID_W, even, odd)
                bias_ref[j, r * GRID_W:(r + 1) * GRID_W, pair * LANES:(pair + 1) * LANES] = (
                    jnp.where(in_cols, tile, -jnp.inf))


def _nat_key0(qt):
    half = DEC_SEQ // NAT_TQ // 2
    return (qt >= half) * (DEC_SEQ - NAT_KEYS)


def _nat_latent_kernel(q_ref, k_ref, v_ref, ck_ref, cv_ref, diag_ref, o_ref, bias_ref):
    key0 = pl.multiple_of(_nat_key0(pl.program_id(1)).astype(jnp.int32), DEC_SEQ - NAT_KEYS)
    keys = pl.ds(key0, NAT_KEYS)

    @pl.when(pl.program_id(2) == 0)
    def _():
        _nat_bias_tiles(diag_ref, bias_ref)

    def head(j):
        sl = slice(j * HEAD_DIM, (j + 1) * HEAD_DIM)
        q = _scaled_q(q_ref[:, sl])
        s_loc = _dot_nt(q, k_ref[keys, sl].astype(BF16)) + bias_ref[j]
        s_ctx = _dot_nt(q, ck_ref[0, :, sl].astype(BF16))
        yield
        m = jnp.maximum(jnp.max(s_loc, axis=-1, keepdims=True), jnp.max(s_ctx, axis=-1, keepdims=True))
        yield
        e_loc = jnp.exp(s_loc - m)
        e_ctx = jnp.exp(s_ctx - m)
        yield
        o = (_dot(e_loc.astype(BF16), _with_ones(v_ref[keys, sl].astype(BF16)))
             + _dot(e_ctx.astype(BF16), _with_ones(cv_ref[0, :, sl].astype(BF16))))
        yield
        return (o[:, :HEAD_DIM] / o[:, HEAD_DIM:HEAD_DIM + 1]).astype(o_ref.dtype)

    outs = _run_interleaved([head(j) for j in range(2)])
    for j in range(2):
        o_ref[:, j * HEAD_DIM:(j + 1) * HEAD_DIM] = outs[j]


def _nat_latent_attention(z, ck, cv, diag):
    nq = DEC_SEQ // NAT_TQ
    qblk0 = N_PROMPT // NAT_TQ
    pair = 2 * HEAD_DIM
    return pl.pallas_call(
        _nat_latent_kernel,
        out_shape=jax.ShapeDtypeStruct((N_SAMPLE, NAT_WIDTH), BF16),
        grid=(NAT_HEADS // 2, nq, DEC_BATCH),
        in_specs=[
            pl.BlockSpec((NAT_TQ, pair), lambda hp, qt, b: (qblk0 + b * nq + qt, _NAT_Q0 // pair + hp)),
            pl.BlockSpec((DEC_SEQ, pair), lambda hp, qt, b: (_SAMPLE_BLK0 + b, _NAT_K0 // pair + hp)),
            pl.BlockSpec((DEC_SEQ, pair), lambda hp, qt, b: (_SAMPLE_BLK0 + b, _NAT_V0 // pair + hp)),
            pl.BlockSpec((1, PAST_LEN, pair), lambda hp, qt, b: (b, 0, hp)),
            pl.BlockSpec((1, PAST_LEN, pair), lambda hp, qt, b: (b, 0, hp)),
            pl.BlockSpec((2, NAT_TQ // GRID_W, NAT_SPAN_PAD, LANES), lambda hp, qt, b: (hp, qt, 0, 0)),
        ],
        out_specs=pl.BlockSpec((NAT_TQ, pair), lambda hp, qt, b: (b * nq + qt, hp)),
        scratch_shapes=[pltpu.VMEM((2, NAT_TQ, NAT_KEYS), F32)],
        compiler_params=_params("arbitrary", "arbitrary", "arbitrary"),
        name="nat_latent_attention",
    )(z, z, z, ck, cv, diag)


DIFF_TQ = 512


def _rope_tables():
    nf = HEAD_DIM // 4
    t = jnp.arange(DEC_SEQ)
    inv = ROPE_THETA ** (-jnp.arange(nf, dtype=F32) / nf)
    ang_r = (t // GRID_W).astype(F32)[:, None] * inv
    ang_c = (t % GRID_W).astype(F32)[:, None] * inv
    cos = jnp.concatenate([jnp.cos(ang_r)] * 2 + [jnp.cos(ang_c)] * 2, axis=-1)
    sin = jnp.concatenate([-jnp.sin(ang_r), jnp.sin(ang_r), -jnp.sin(ang_c), jnp.sin(ang_c)], axis=-1)
    return jnp.tile(cos, (1, 2)), jnp.tile(sin, (1, 2))


def _rope(x, cos, sin):
    nf = HEAD_DIM // 4
    lane = lax.broadcasted_iota(jnp.int32, x.shape, 1)
    upper = (lane // nf) % 2 == 1
    partner = jnp.where(upper, pltpu.roll(x, nf, axis=1), pltpu.roll(x, x.shape[1] - nf, axis=1))
    return x * cos + partner * sin


def _diff_latent_kernel(q_ref, k_ref, v_ref, ck_ref, cv_ref, cos_ref, sin_ref, lam_ref, subg_ref,
                        o_ref, kr_ref, *, lam_init):
    qt = pl.program_id(2)

    @pl.when(qt == 0)
    def _():
        kr_ref[...] = _rope(k_ref[...], cos_ref[...], sin_ref[...]).astype(BF16)

    lam = _diff_lambda(lam_ref, lam_init)
    row0 = pl.multiple_of(qt * DIFF_TQ, DIFF_TQ)
    q = _scaled_q(_rope(q_ref[...], cos_ref[pl.ds(row0, DIFF_TQ), :], sin_ref[pl.ds(row0, DIFF_TQ), :]))
    v_loc = _with_ones(v_ref[...].astype(BF16))
    v_ctx = _with_ones(cv_ref[0].astype(BF16))
    width = 2 * HEAD_DIM

    def softmax_av(j):
        sl = slice(j * HEAD_DIM, (j + 1) * HEAD_DIM)
        s_ctx = _dot_nt(q[:, sl], ck_ref[0, :, sl].astype(BF16))
        s_loc = _dot_nt(q[:, sl], kr_ref[:, sl])
        yield
        m = jnp.maximum(jnp.max(s_loc, axis=-1, keepdims=True), jnp.max(s_ctx, axis=-1, keepdims=True))
        yield
        e_loc = jnp.exp(s_loc - m)
        e_ctx = jnp.exp(s_ctx - m)
        yield
        av = _dot(e_loc.astype(BF16), v_loc) + _dot(e_ctx.astype(BF16), v_ctx)
        yield
        return av[:, :width] / av[:, width:width + 1]

    parts = _run_interleaved([softmax_av(j) for j in range(2)])
    o = parts[0] - lam * parts[1]
    o_ref[...] = _sub_norm(o, subg_ref[...], lam_init).astype(o_ref.dtype)


def _diff_latent_attention(z, ck, cv, lam_p, subln_g, lam_init):
    nq = DEC_SEQ // DIFF_TQ
    qblk0 = N_PROMPT // DIFF_TQ
    w = 2 * HEAD_DIM
    cos, sin = _rope_tables()
    return pl.pallas_call(
        functools.partial(_diff_latent_kernel, lam_init=lam_init),
        out_shape=jax.ShapeDtypeStruct((N_SAMPLE, DIFF_WIDTH), BF16),
        grid=(DEC_BATCH, DIFF_HEADS, nq),
        in_specs=[
            pl.BlockSpec((DIFF_TQ, w), lambda b, h, qt: (qblk0 + b * nq + qt, _DIFF_Q0 // w + h)),
            pl.BlockSpec((DEC_SEQ, w), lambda b, h, qt: (_SAMPLE_BLK0 + b, _DIFF_K0 // w + h)),
            pl.BlockSpec((DEC_SEQ, w), lambda b, h, qt: (_SAMPLE_BLK0 + b, _DIFF_V0 // w + h)),
            pl.BlockSpec((1, PAST_LEN, w), lambda b, h, qt: (b, 0, h)),
            pl.BlockSpec((1, PAST_LEN, w), lambda b, h, qt: (b, 0, h)),
            pl.BlockSpec((DEC_SEQ, w), lambda b, h, qt: (0, 0)),
            pl.BlockSpec((DEC_SEQ, w), lambda b, h, qt: (0, 0)),
            pl.BlockSpec((4, HEAD_DIM), lambda b, h, qt: (0, 0)),
            pl.BlockSpec((1, w), lambda b, h, qt: (0, 0)),
        ],
        out_specs=pl.BlockSpec((DIFF_TQ, w), lambda b, h, qt: (b * nq + qt, h)),
        scratch_shapes=[pltpu.VMEM((DEC_SEQ, w), BF16)],
        compiler_params=_params("arbitrary", "arbitrary", "arbitrary"),
        name="diff_latent_attention",
    )(z, z, z, ck, cv, cos, sin, lam_p, subln_g.reshape(1, w))


OUT_TM = 256


def _route(logits_t, rb_col):
    sc = jax.nn.sigmoid(logits_t)
    bi = sc + rb_col
    srow = [sc[e:e + 1] for e in range(N_EXPERTS)]
    brow = [bi[e:e + 1] for e in range(N_EXPERTS)]
    n = EXPERTS_PER_GROUP
    gscore = []
    for g in range(N_GROUPS):
        v = brow[g * n:(g + 1) * n]
        best = None
        for i in range(n):
            for j in range(i + 1, n):
                s = v[i] + v[j]
                best = s if best is None else jnp.maximum(best, s)
        gscore.append(best)
    sel = jnp.zeros_like(gscore[0], dtype=jnp.int32)
    best = gscore[0]
    for g in range(1, N_GROUPS):
        better = gscore[g] > best
        sel = jnp.where(better, g, sel)
        best = jnp.where(better, gscore[g], best)

    def pick_group(rows, i):
        out = rows[(N_GROUPS - 1) * n + i]
        for g in range(N_GROUPS - 2, -1, -1):
            out = jnp.where(sel == g, rows[g * n + i], out)
        return out

    bv = [pick_group(brow, i) for i in range(n)]
    sv = [pick_group(srow, i) for i in range(n)]
    i1 = jnp.zeros_like(sel)
    m1, w1 = bv[0], sv[0]
    for i in range(1, n):
        better = bv[i] > m1
        i1 = jnp.where(better, i, i1)
        m1 = jnp.where(better, bv[i], m1)
        w1 = jnp.where(better, sv[i], w1)
    i2 = jnp.zeros_like(sel)
    m2 = jnp.full_like(m1, -jnp.inf)
    w2 = jnp.zeros_like(w1)
    for i in range(n):
        better = (i1 != i) & (bv[i] > m2)
        i2 = jnp.where(better, i, i2)
        m2 = jnp.where(better, bv[i], m2)
        w2 = jnp.where(better, sv[i], w2)
    tot = w1 + w2
    return sel * n + i1, sel * n + i2, w1 / tot, w2 / tot


def _store_token_tiles(ref, value):
    rows = value.shape[0]
    chunks = jnp.stack([value[:, c * LANES:(c + 1) * LANES] for c in range(N_CHUNKS)], axis=0)
    tiles = jnp.transpose(chunks.reshape(N_CHUNKS, rows // 8, 8, LANES), (1, 2, 0, 3))
    ref[...] = tiles.reshape(rows, N_CHUNKS, LANES)


def _load_token_tiles(ref):
    rows = ref.shape[0]
    tiles = ref[...].reshape(rows // 8, 8, N_CHUNKS, LANES)
    chunks = jnp.transpose(tiles, (2, 0, 1, 3)).reshape(N_CHUNKS, rows, LANES)
    return jnp.concatenate([chunks[c] for c in range(N_CHUNKS)], axis=1)


def _outproj_kernel(*refs, n_o, n_x):
    o_refs, x_refs, refs = refs[:n_o], refs[n_o:n_o + n_x], refs[n_o + n_x:]
    mod_ref, w_ref, lng_ref, lnb_ref, rw_ref, rb_ref, x1_ref, h2_ref, ri_ref, rw_out_ref, cnt_ref = refs
    tm = x1_ref.shape[0]
    m = mod_ref[0]
    y = _dot(_stream_tile(o_refs, tm), w_ref[...])
    x1 = _layer_norm(ALPHA * _stream_tile(x_refs, tm) + m[2:3] * y, lng_ref[...], lnb_ref[...])
    x1_ref[...] = x1
    h2 = x1 * (1.0 + m[4:5]) + m[3:4]
    _store_token_tiles(h2_ref, h2)
    rw = rw_ref[...]
    h_hi, rw_hi = h2.astype(BF16), rw.astype(BF16)
    h_lo = (h2 - h_hi.astype(F32)).astype(BF16)
    rw_lo = (rw - rw_hi.astype(F32)).astype(BF16)
    logits = _dot(h_hi, rw_hi) + (_dot(h_lo, rw_hi) + _dot(h_hi, rw_lo))
    e1, e2, w1, w2 = _route(logits.T[:N_EXPERTS], rb_ref[...])
    eid = lax.broadcasted_iota(jnp.int32, (N_EXPERTS, tm), 0)
    onehot = (eid == e1) | (eid == e2)
    ti = lax.broadcasted_iota(jnp.int32, (tm, tm), 0)
    tj = lax.broadcasted_iota(jnp.int32, (tm, tm), 1)
    earlier = jnp.where(ti < tj, 1.0, 0.0).astype(BF16)
    rank = _dot(jnp.where(onehot, 1.0, 0.0).astype(BF16), earlier)
    r1 = jnp.sum(jnp.where(eid == e1, rank, 0.0), axis=0, keepdims=True).astype(jnp.int32)
    r2 = jnp.sum(jnp.where(eid == e2, rank, 0.0), axis=0, keepdims=True).astype(jnp.int32)
    ri_ref[...] = jnp.concatenate([e1, e2, r1, r2, jnp.zeros((4, tm), jnp.int32)], axis=0)
    wt = jnp.concatenate([w1, w2, jnp.zeros((LANES - 2, tm), F32)], axis=0)
    rw_out_ref[...] = wt.T
    cnt = jnp.sum(jnp.where(onehot, 1.0, 0.0), axis=1, keepdims=True)
    cnt_ref[0] = jnp.broadcast_to(cnt, (N_EXPERTS, LANES)).astype(jnp.int32)


def _output_projection(o_parts, x_parts, mod, w_bf16, ln_g, ln_b, router_w_pad, router_b, tm=OUT_TM):
    nt = N_TOK // tm
    return pl.pallas_call(
        functools.partial(_outproj_kernel, n_o=len(o_parts), n_x=len(x_parts)),
        out_shape=(jax.ShapeDtypeStruct((N_TOK, D_MODEL), F32),
                   jax.ShapeDtypeStruct((N_TOK, N_CHUNKS, LANES), F32),
                   jax.ShapeDtypeStruct((8, N_TOK), jnp.int32),
                   jax.ShapeDtypeStruct((N_TOK, LANES), F32),
                   jax.ShapeDtypeStruct((nt, N_EXPERTS, LANES), jnp.int32)),
        grid=(nt,),
        in_specs=_stream_specs([a.shape[1] for a in o_parts], tm) + _stream_specs([a.shape[1] for a in x_parts], tm) + [
            pl.BlockSpec((1, 6, D_MODEL), lambda i: (_cond_row(i * tm), 0, 0)),
            pl.BlockSpec((D_MODEL, D_MODEL), lambda i: (0, 0)),
            pl.BlockSpec((1, D_MODEL), lambda i: (0, 0)),
            pl.BlockSpec((1, D_MODEL), lambda i: (0, 0)),
            pl.BlockSpec((D_MODEL, LANES), lambda i: (0, 0)),
            pl.BlockSpec((N_EXPERTS, 1), lambda i: (0, 0)),
        ],
        out_specs=(pl.BlockSpec((tm, D_MODEL), lambda i: (i, 0)),
                   pl.BlockSpec((tm, N_CHUNKS, LANES), lambda i: (i, 0, 0)),
                   pl.BlockSpec((8, tm), lambda i: (0, i)),
                   pl.BlockSpec((tm, LANES), lambda i: (i, 0)),
                   pl.BlockSpec((1, N_EXPERTS, LANES), lambda i: (i, 0, 0))),
        compiler_params=_params("arbitrary"),
        name="output_projection",
    )(*o_parts, *x_parts, mod, w_bf16, ln_g.reshape(1, D_MODEL), ln_b.reshape(1, D_MODEL),
      router_w_pad, router_b.reshape(N_EXPERTS, 1))


N_ASSIGN = 2 * N_TOK
MOE_TM = 256
MOE_ROWS = N_ASSIGN + N_EXPERTS * MOE_TM
MOE_TILES = MOE_ROWS // MOE_TM
DISPATCH_TM = 512
COMBINE_TM = 256


def _moe_plan(route_i, counts):
    cnt = counts[:, :, 0]
    total = jnp.sum(cnt, axis=0)
    padded = (total + MOE_TM - 1) // MOE_TM * MOE_TM
    seg_end = jnp.cumsum(padded)
    seg_start = seg_end - padded
    tile_base = seg_start[None, :] + jnp.cumsum(cnt, axis=0) - cnt
    base_tok = jnp.repeat(tile_base, OUT_TM, axis=0)
    eids = jnp.arange(N_EXPERTS, dtype=jnp.int32)[None, :]
    pos = [jnp.sum(jnp.where(route_i[k][:, None] == eids, base_tok, 0), axis=1) + route_i[2 + k]
           for k in range(2)]
    pos = jnp.concatenate(pos).astype(jnp.int32)
    tile_row0 = jnp.arange(MOE_TILES, dtype=jnp.int32) * MOE_TM
    tile_expert = jnp.minimum(jnp.sum(seg_end[None, :] <= tile_row0[:, None], axis=1), N_EXPERTS - 1)
    n_tiles = (seg_end[-1] // MOE_TM).reshape(1)
    last = MOE_ROWS - MOE_TM
    fill_rows = jnp.concatenate([jnp.minimum(seg_start + total, last),
                                 last - jnp.arange(N_EXPERTS, dtype=jnp.int32) * MOE_TM])
    return pos, tile_expert.astype(jnp.int32), n_tiles.astype(jnp.int32), fill_rows.astype(jnp.int32)


def _row_copy_wait(src_hbm, dst, sem, n_rows):
    pltpu.make_async_copy(src_hbm.at[pl.ds(0, n_rows)], dst.at[pl.ds(0, n_rows)], sem).wait()


def _dispatch_kernel(pos_ref, fill_ref, h_ref, xs_ref, zero_ref, sem, fill_sem):
    tm = h_ref.shape[0]
    t0 = pl.program_id(0) * tm

    @pl.when(pl.program_id(0) == 0)
    def _():
        zero_ref[...] = jnp.zeros_like(zero_ref)
        fills = [pltpu.make_async_copy(zero_ref, xs_ref.at[pl.ds(fill_ref[j], MOE_TM)], fill_sem)
                 for j in range(2 * N_EXPERTS)]
        for cp in fills[N_EXPERTS:]:
            cp.start()
        for cp in fills[N_EXPERTS:]:
            cp.wait()
        for cp in fills[:N_EXPERTS]:
            cp.start()
            cp.wait()

    def body(r, carry):
        for k in range(2):
            pltpu.make_async_copy(h_ref.at[r], xs_ref.at[pos_ref[k * N_TOK + t0 + r]], sem).start()
        return carry

    lax.fori_loop(0, tm, body, 0, unroll=8)
    for _ in range(2):
        _row_copy_wait(h_ref, xs_ref, sem, tm)


def _moe_dispatch(pos, fill_rows, h2_tiles):
    tm = DISPATCH_TM
    return pl.pallas_call(
        _dispatch_kernel,
        out_shape=jax.ShapeDtypeStruct((MOE_ROWS, N_CHUNKS, LANES), F32),
        grid_spec=pltpu.PrefetchScalarGridSpec(
            num_scalar_prefetch=2,
            grid=(N_TOK // tm,),
            in_specs=[pl.BlockSpec((tm, N_CHUNKS, LANES), lambda i, pos, fill: (i, 0, 0))],
            out_specs=pl.BlockSpec(memory_space=pl.ANY),
            scratch_shapes=[pltpu.VMEM((MOE_TM, N_CHUNKS, LANES), F32), pltpu.SemaphoreType.DMA,
                            pltpu.SemaphoreType.DMA],
        ),
        compiler_params=_params("arbitrary"),
        name="moe_dispatch",
    )(pos, fill_rows, h2_tiles)


def _expert_kernel(te_ref, nt_ref, xs_ref, wg_ref, wu_ref, wd_ref, ys_ref, wg16_ref, wu16_ref, wd16_ref):
    i = pl.program_id(0)

    @pl.when(i < nt_ref[0])
    def _():
        @pl.when((i == 0) | (te_ref[i] != te_ref[jnp.maximum(i - 1, 0)]))
        def _():
            wg16_ref[...] = wg_ref[0, 0].astype(BF16)
            wu16_ref[...] = wu_ref[0, 0].astype(BF16)
            wd16_ref[...] = wd_ref[0, 0].astype(BF16)

        x = _load_token_tiles(xs_ref).astype(BF16)
        he = _silu(_dot(x, wg16_ref[...])) * _dot(x, wu16_ref[...])
        _store_token_tiles(ys_ref, _dot(he.astype(BF16), wd16_ref[...]))

    @pl.when(i >= nt_ref[0])
    def _():
        ys_ref[...] = jnp.zeros_like(ys_ref)


def _moe_experts(tile_expert, n_tiles, xs, layer, wg, wu, wd):
    row_blk = lambda i, te, nt: (jnp.minimum(i, nt[0] - 1), 0, 0)
    w_blk = lambda i, te, nt: (layer, te[i], 0, 0)
    return pl.pallas_call(
        _expert_kernel,
        out_shape=jax.ShapeDtypeStruct((MOE_ROWS, N_CHUNKS, LANES), F32),
        grid_spec=pltpu.PrefetchScalarGridSpec(
            num_scalar_prefetch=2,
            grid=(MOE_TILES,),
            in_specs=[
                pl.BlockSpec((MOE_TM, N_CHUNKS, LANES), row_blk),
                pl.BlockSpec((1, 1, D_MODEL, EXPERT_FF), w_blk),
                pl.BlockSpec((1, 1, D_MODEL, EXPERT_FF), w_blk),
                pl.BlockSpec((1, 1, EXPERT_FF, D_MODEL), w_blk),
            ],
            out_specs=pl.BlockSpec((MOE_TM, N_CHUNKS, LANES), lambda i, te, nt: (i, 0, 0)),
            scratch_shapes=[pltpu.VMEM((D_MODEL, EXPERT_FF), BF16), pltpu.VMEM((D_MODEL, EXPERT_FF), BF16),
                            pltpu.VMEM((EXPERT_FF, D_MODEL), BF16)],
        ),
        compiler_params=_params("arbitrary"),
        name="moe_experts",
    )(tile_expert, n_tiles, xs, wg, wu, wd)


def _combine_kernel(pos_ref, gw_ref, x1_ref, mod_ref, lng_ref, lnb_ref, ys_ref, *rest, split):
    out_refs, (buf_ref, sem) = rest[:-2], rest[-2:]
    i = pl.program_id(0)
    n = pl.num_programs(0)
    tm = COMBINE_TM

    def start_row(tile, to_slot, r):
        for k in range(2):
            p = pos_ref[k * N_TOK + tile * tm + r]
            pltpu.make_async_copy(ys_ref.at[p], buf_ref.at[to_slot, k, r], sem.at[to_slot]).start()

    def wait_slot(s):
        for k in range(2):
            _row_copy_wait(ys_ref, buf_ref.at[s, k], sem.at[s], tm)

    def issue(tile, to_slot):
        def body(r, carry):
            start_row(tile, to_slot, r)
            return carry
        lax.fori_loop(0, tm, body, 0, unroll=8)

    @pl.when(i == 0)
    def _():
        issue(0, 0)

    @pl.when(i + 1 < n)
    def _():
        issue(i + 1, (i + 1) % 2)

    slot = i % 2
    wait_slot(slot)
    gw = gw_ref[...]
    f = gw[:, 0:1] * _load_token_tiles(buf_ref.at[slot, 0]) + gw[:, 1:2] * _load_token_tiles(buf_ref.at[slot, 1])
    m = mod_ref[0]
    out = _layer_norm(ALPHA * x1_ref[...] + m[5:6] * f, lng_ref[...], lnb_ref[...])

    if split:
        prompt_ref, sample_ref = out_refs

        @pl.when(i < N_PROMPT // tm)
        def _():
            prompt_ref[...] = out

        @pl.when(i >= N_PROMPT // tm)
        def _():
            sample_ref[...] = out
    else:
        out_refs[0][...] = out


def _moe_combine(pos, gate_w, x1, mod, ln_g, ln_b, ys, split):
    tm = COMBINE_TM
    np_blk = N_PROMPT // tm
    if split:
        out_shape = (jax.ShapeDtypeStruct((N_PROMPT, D_MODEL), F32), jax.ShapeDtypeStruct((N_SAMPLE, D_MODEL), F32))
        out_specs = (pl.BlockSpec((tm, D_MODEL), lambda i, pos: (jnp.minimum(i, np_blk - 1), 0)),
                     pl.BlockSpec((tm, D_MODEL), lambda i, pos: (jnp.maximum(i - np_blk, 0), 0)))
    else:
        out_shape = jax.ShapeDtypeStruct((N_TOK, D_MODEL), F32)
        out_specs = pl.BlockSpec((tm, D_MODEL), lambda i, pos: (i, 0))
    return pl.pallas_call(
        functools.partial(_combine_kernel, split=split),
        out_shape=out_shape,
        grid_spec=pltpu.PrefetchScalarGridSpec(
            num_scalar_prefetch=1,
            grid=(N_TOK // tm,),
            in_specs=[
                pl.BlockSpec((tm, LANES), lambda i, pos: (i, 0)),
                pl.BlockSpec((tm, D_MODEL), lambda i, pos: (i, 0)),
                pl.BlockSpec((1, 6, D_MODEL), lambda i, pos: (_cond_row(i * tm), 0, 0)),
                pl.BlockSpec((1, D_MODEL), lambda i, pos: (0, 0)),
                pl.BlockSpec((1, D_MODEL), lambda i, pos: (0, 0)),
                pl.BlockSpec(memory_space=pl.ANY),
            ],
            out_specs=out_specs,
            scratch_shapes=[pltpu.VMEM((2, 2, tm, N_CHUNKS, LANES), F32), pltpu.SemaphoreType.DMA((2,))],
        ),
        compiler_params=_params("arbitrary"),
        name="moe_combine",
    )(pos, gate_w, x1, mod, ln_g.reshape(1, D_MODEL), ln_b.reshape(1, D_MODEL), ys)


def _moe_ffn(h2_tiles, route_i, gate_w, counts, layer, wg, wu, wd, x1, mod, ln_g, ln_b, split=False):
    pos, tile_expert, n_tiles, fill_rows = _moe_plan(route_i, counts)
    xs = _moe_dispatch(pos, fill_rows, h2_tiles)
    ys = _moe_experts(tile_expert, n_tiles, xs, layer, wg, wu, wd)
    return _moe_combine(pos, gate_w, x1, mod, ln_g, ln_b, ys, split)


DN_GATE_COL0 = DN_CONV_CH
N_DIRS = 2


def _dn_prep_kernel(q_ref, k_ref, v_ref, wq_ref, wk_ref, wv_ref, qo_ref, ko_ref, vo_ref):
    length = q_ref.shape[0]
    row = lax.broadcasted_iota(jnp.int32, (length, DN_DK), 0)

    def conv(x, w):
        prev = jnp.where(row == 0, 0.0, pltpu.roll(x, 1, axis=0))
        nxt = jnp.where(row == length - 1, 0.0, pltpu.roll(x, length - 1, axis=0))
        return _silu(prev * w[0:1] + x * w[1:2] + nxt * w[2:3])

    def l2n(x):
        return x * lax.rsqrt(jnp.sum(x * x, axis=-1, keepdims=True) + 1e-6)

    for h in range(DN_HEADS):
        sl = slice(h * DN_DK, (h + 1) * DN_DK)
        q, k, v = (r[:, sl].astype(F32) for r in (q_ref, k_ref, v_ref))
        qo_ref[h] = (l2n(conv(q, wq_ref[:, sl])) * (DN_DK ** -0.5)).astype(qo_ref.dtype)
        ko_ref[h] = l2n(conv(k, wk_ref[:, sl])).astype(ko_ref.dtype)
        vo_ref[h] = conv(v, wv_ref[:, sl]).astype(vo_ref.dtype)


def _dn_prep(z1, conv_w, nb, length, row_blk0):
    shp = jax.ShapeDtypeStruct((DN_HEADS, nb * length, DN_DK), BF16)
    blk = lambda part: pl.BlockSpec((length, DN_QK_WIDTH), lambda b: (row_blk0 + b, part))
    wblk = lambda part: pl.BlockSpec((3, DN_QK_WIDTH), lambda b: (0, part))
    oblk = pl.BlockSpec((DN_HEADS, length, DN_DK), lambda b: (0, b, 0))
    return pl.pallas_call(
        _dn_prep_kernel,
        out_shape=(shp, shp, shp),
        grid=(nb,),
        in_specs=[blk(0), blk(1), blk(2), wblk(0), wblk(1), wblk(2)],
        out_specs=(oblk, oblk, oblk),
        compiler_params=_params("arbitrary"),
        name="deltanet_prep",
    )(z1, z1, z1, conv_w, conv_w, conv_w)


def _dn_gates_kernel(ab_ref, alog_ref, dtb_ref, o_ref):
    length = ab_ref.shape[0]
    x = ab_ref[...]
    a = x + dtb_ref[...]
    softplus = jnp.maximum(a, 0.0) + jnp.log1p(jnp.exp(-jnp.abs(a)))
    g = -jnp.exp(alog_ref[...]) * softplus
    beta = jax.nn.sigmoid(x)
    ri = lax.broadcasted_iota(jnp.int32, (DN_CHUNK, DN_CHUNK), 0)
    ci = lax.broadcasted_iota(jnp.int32, (DN_CHUNK, DN_CHUNK), 1)
    tril = (ri >= ci).astype(F32)
    triu = (ri <= ci).astype(F32)
    lane = lax.broadcasted_iota(jnp.int32, (DN_CHUNK, LANES), 1)
    for c in range(length // DN_CHUNK):
        rows = slice(c * DN_CHUNK, (c + 1) * DN_CHUNK)
        gch = g[rows]
        pre = jnp.dot(tril, gch, precision=lax.Precision.HIGHEST, preferred_element_type=F32)
        suf = jnp.dot(triu, gch, precision=lax.Precision.HIGHEST, preferred_element_type=F32)
        gc = jnp.where(lane < DN_HEADS, pre, suf)
        o_ref[rows, :] = jnp.where(lane < N_DIRS * DN_HEADS, gc, beta[rows])


def _dn_gates(z_ab, a_log, dt_bias, nb, length, row_blk0):
    pad = LANES - N_DIRS * DN_HEADS
    alog = jnp.pad(a_log.reshape(1, -1).astype(F32), ((0, 0), (0, pad)))
    dtb = jnp.pad(dt_bias.reshape(1, -1).astype(F32), ((0, 0), (0, pad)))
    return pl.pallas_call(
        _dn_gates_kernel,
        out_shape=jax.ShapeDtypeStruct((nb * length, LANES), F32),
        grid=(nb,),
        in_specs=[
            pl.BlockSpec((length, LANES), lambda b: (row_blk0 + b, 0)),
            pl.BlockSpec((1, LANES), lambda b: (0, 0)),
            pl.BlockSpec((1, LANES), lambda b: (0, 0)),
        ],
        out_specs=pl.BlockSpec((length, LANES), lambda b: (b, 0)),
        compiler_params=_params("arbitrary"),
        name="deltanet_gates",
    )(z_ab, alog, dtb)


def _delta_chunk(q, k, v, beta, gcol, grow, s, lower):
    ri = lax.broadcasted_iota(jnp.int32, (DN_CHUNK, DN_CHUNK), 0)
    ci = lax.broadcasted_iota(jnp.int32, (DN_CHUNK, DN_CHUNK), 1)
    incl = (ri >= ci) if lower else (ri <= ci)
    strict = (ri > ci) if lower else (ri < ci)
    decay = jnp.exp(jnp.where(incl, gcol - grow, -jnp.inf))
    kb = k * beta
    kq = _dot_nt(jnp.concatenate([kb, q], axis=0).astype(BF16), k.astype(BF16))
    kk, a_qk = kq[:DN_CHUNK], kq[DN_CHUNK:] * decay
    yield
    tri_l = jnp.where(strict, kk * decay, 0.0)

    def off_block(size):
        same = (ri // (2 * size)) == (ci // (2 * size))
        rpar, cpar = (ri // size) % 2, (ci // size) % 2
        return same & ((rpar == 1) & (cpar == 0) if lower else (rpar == 0) & (cpar == 1))

    p = jnp.where(ri == ci, 1.0, 0.0) - jnp.where(off_block(1), tri_l, 0.0)
    size = 2
    while size < DN_CHUNK:
        p16 = p.astype(BF16)
        pc = _dot(p16, jnp.where(off_block(size), tri_l, 0.0).astype(BF16))
        yield
        p = p - _dot(pc.astype(BF16), p16)
        yield
        size *= 2
    eg = jnp.exp(gcol)
    uw = _dot(p.astype(BF16), jnp.concatenate([v * beta, kb * eg], axis=1).astype(BF16))
    yield
    u, w = uw[:, :DN_DV], uw[:, DN_DV:]
    ws = _dot(jnp.concatenate([w, q * eg], axis=0).astype(BF16), s.astype(BF16))
    v_new, o_inter = u - ws[:DN_CHUNK], ws[DN_CHUNK:]
    yield
    v_new16 = v_new.astype(BF16)
    o = o_inter + _dot(a_qk.astype(BF16), v_new16)
    g_last = gcol[DN_CHUNK - 1:DN_CHUNK] if lower else gcol[0:1]
    kd = k * jnp.exp(g_last - gcol)
    s_new = s * jnp.exp(g_last) + _dot_tn(kd.astype(BF16), v_new16)
    return o, s_new


def _run_interleaved(chains):
    results = [None] * len(chains)
    active = list(enumerate(chains))
    while active:
        still = []
        for idx, gen in active:
            try:
                next(gen)
                still.append((idx, gen))
            except StopIteration as stop:
                results[idx] = stop.value
        active = still
    return results


def _dn_scan_kernel(*refs, has_init, want_final):
    (qf_ref, kf_ref, vf_ref, gf_ref, qb_ref, kb_ref, vb_ref, gb_ref), rest = refs[:8], refs[8:]
    if has_init:
        s0_ref, rest = rest[0], rest[1:]
    of_ref, ob_ref = rest[0], rest[1]
    rest = rest[2:]
    if want_final:
        sf_ref, rest = rest[0], rest[1:]
    s_ref = rest[0]
    c = pl.program_id(1)
    nc = pl.num_programs(1)

    @pl.when(c == 0)
    def _():
        for d in range(N_DIRS):
            for h in range(DN_HEADS):
                if has_init:
                    s_ref[d * DN_HEADS + h] = s0_ref[0, d, h]
                else:
                    s_ref[d * DN_HEADS + h] = jnp.zeros((DN_DK, DN_DV), F32)

    dirs = ((qf_ref, kf_ref, vf_ref, gf_ref, of_ref, True), (qb_ref, kb_ref, vb_ref, gb_ref, ob_ref, False))
    chains = []
    for d, (q_ref, k_ref, v_ref, g_ref, o_ref, lower) in enumerate(dirs):
        gates = g_ref[...]
        gates_t = gates.T
        for h in range(DN_HEADS):
            j = d * DN_HEADS + h
            jb = N_DIRS * DN_HEADS + j
            chains.append(_delta_chunk(q_ref[h].astype(F32), k_ref[h].astype(F32), v_ref[h].astype(F32),
                                       gates[:, jb:jb + 1], gates[:, j:j + 1], gates_t[j:j + 1, :],
                                       s_ref[j], lower))
    results = _run_interleaved(chains)
    for d, (_, _, _, _, o_ref, _) in enumerate(dirs):
        for h in range(DN_HEADS):
            j = d * DN_HEADS + h
            o, s_new = results[j]
            o_ref[:, h * DN_DV:(h + 1) * DN_DV] = o.astype(o_ref.dtype)
            s_ref[j] = s_new

    if want_final:
        @pl.when(c == nc - 1)
        def _():
            for d in range(N_DIRS):
                for h in range(DN_HEADS):
                    sf_ref[0, d, h] = s_ref[d * DN_HEADS + h]


def _dn_scan(qn, kn, vn, gates, s0, nb, length, want_final):
    nc = length // DN_CHUNK
    has_init = s0 is not None
    fwd = lambda b, c: (0, b * nc + c, 0)
    bwd = lambda b, c: (0, b * nc + (nc - 1 - c), 0)
    hm = lambda im: pl.BlockSpec((DN_HEADS, DN_CHUNK, DN_DK), im)
    in_specs = [hm(fwd), hm(fwd), hm(fwd), pl.BlockSpec((DN_CHUNK, LANES), lambda b, c: (b * nc + c, 0)),
                hm(bwd), hm(bwd), hm(bwd), pl.BlockSpec((DN_CHUNK, LANES), lambda b, c: (b * nc + (nc - 1 - c), 0))]
    args = [qn, kn, vn, gates, qn, kn, vn, gates]
    state_blk = pl.BlockSpec((1, N_DIRS, DN_HEADS, DN_DK, DN_DV), lambda b, c: (b, 0, 0, 0, 0))
    if has_init:
        in_specs.append(state_blk)
        args.append(s0)
    o_shape = jax.ShapeDtypeStruct((nb * length, DN_WIDTH), BF16)
    out_shape = [o_shape, o_shape]
    out_specs = [pl.BlockSpec((DN_CHUNK, DN_WIDTH), lambda b, c: (b * nc + c, 0)),
                 pl.BlockSpec((DN_CHUNK, DN_WIDTH), lambda b, c: (b * nc + (nc - 1 - c), 0))]
    if want_final:
        out_shape.append(jax.ShapeDtypeStruct((nb, N_DIRS, DN_HEADS, DN_DK, DN_DV), F32))
        out_specs.append(state_blk)
    return pl.pallas_call(
        functools.partial(_dn_scan_kernel, has_init=has_init, want_final=want_final),
        out_shape=tuple(out_shape),
        grid=(nb, nc),
        in_specs=in_specs,
        out_specs=tuple(out_specs),
        scratch_shapes=[pltpu.VMEM((N_DIRS * DN_HEADS, DN_DK, DN_DV), F32)],
        compiler_params=_params("arbitrary", "arbitrary"),
        name="deltanet_scan",
    )(*args)


DN_MERGE_TM = 256


def _dn_merge_kernel(of_ref, ob_ref, gate_ref, g_ref, o_ref):
    for h in range(DN_HEADS):
        sl = slice(h * DN_DV, (h + 1) * DN_DV)
        o = of_ref[:, sl].astype(F32) + ob_ref[:, sl].astype(F32)
        o = o * lax.rsqrt(jnp.mean(o * o, axis=-1, keepdims=True) + EPS) * g_ref[...]
        o_ref[:, sl] = (o * _silu(gate_ref[:, sl].astype(F32))).astype(o_ref.dtype)


def _dn_merge(o_f, o_b, z1, onorm_g, n_rows, row0):
    tm = DN_MERGE_TM
    blk = pl.BlockSpec((tm, DN_WIDTH), lambda i: (i, 0))
    return pl.pallas_call(
        _dn_merge_kernel,
        out_shape=jax.ShapeDtypeStruct((n_rows, DN_WIDTH), BF16),
        grid=(n_rows // tm,),
        in_specs=[blk, blk,
                  pl.BlockSpec((tm, DN_WIDTH), lambda i: (row0 // tm + i, DN_GATE_COL0 // DN_WIDTH)),
                  pl.BlockSpec((1, DN_DV), lambda i: (0, 0))],
        out_specs=blk,
        compiler_params=_params("arbitrary"),
        name="deltanet_merge",
    )(o_f, o_b, z1, onorm_g.reshape(1, DN_DV))


def _deltanet_stream(z1, z_ab, conv_w, a_log, dt_bias, onorm_g, s0, nb, length, row0, want_final):
    row_blk0 = row0 // length
    qn, kn, vn = _dn_prep(z1, conv_w, nb, length, row_blk0)
    gates = _dn_gates(z_ab, a_log, dt_bias, nb, length, row_blk0)
    outs = _dn_scan(qn, kn, vn, gates, s0, nb, length, want_final)
    merged = _dn_merge(outs[0], outs[1], z1, onorm_g, nb * length, row0)
    return merged, (outs[2] if want_final else None)


def kernel(x_prompt, x_sample, cache_nat_k, cache_nat_v, cache_diff_k, cache_diff_v, state_delta, c, c_ctx, w_in_even, w_out_even, nat_bias, diff_lam, diff_subln, w_in_odd, conv_odd, a_log_odd, dt_bias_odd, onorm_odd, w_out_odd, ada_w, ada_b, ln_g, ln_b, router_w, router_b, moe_wg, moe_wu, moe_wd):
    x_parts = (x_prompt.reshape(N_PROMPT, D_MODEL), x_sample.reshape(N_SAMPLE, D_MODEL))
    cond =jnp.concatenate([c_ctx[None], c, jnp.zeros((N_COND - 1 - DEC_BATCH, D_MODEL), F32)], axis=0)
    mod = _ada_modulation(cond, ada_w, ada_b).reshape(DEPTH, N_COND, 6, D_MODEL)
    router_w_pad = jnp.pad(router_w, ((0, 0), (0, LANES - N_EXPERTS)))

    lam_init = 0.8 - 0.6 * math.exp(-0.3 * 0)
    z = _input_projection(x_parts, mod[0], w_in_even[0].astype(BF16))
    new_nat_k = z[:N_PROMPT, _NAT_K0:_NAT_V0].reshape(BATCH, 1, SEQ, NAT_HEADS, HEAD_DIM)
    new_nat_v = z[:N_PROMPT, _NAT_V0:_DIFF_Q0].reshape(BATCH, 1, SEQ, NAT_HEADS, HEAD_DIM)
    new_diff_k = z[:N_PROMPT, _DIFF_K0:_DIFF_V0].reshape(BATCH, 1, SEQ, DIFF_HEADS, 2, HEAD_DIM)
    new_diff_v = z[:N_PROMPT, _DIFF_V0:].reshape(BATCH, 1, SEQ, DIFF_HEADS, 2 * HEAD_DIM)
    o_ctx = _context_attention(z, diff_lam[0], diff_subln[0], lam_init)
    o_nat = _nat_latent_attention(z, cache_nat_k[:, 0].reshape(DEC_BATCH, PAST_LEN, NAT_WIDTH),
                                  cache_nat_v[:, 0].reshape(DEC_BATCH, PAST_LEN, NAT_WIDTH),
                                  _nat_bias_diagonals(nat_bias[0]))
    o_diff = _diff_latent_attention(z, cache_diff_k[:, 0].reshape(DEC_BATCH, PAST_LEN, DIFF_WIDTH),
                                    cache_diff_v[:, 0].reshape(DEC_BATCH, PAST_LEN, DIFF_WIDTH),
                                    diff_lam[0], diff_subln[0], lam_init)
    x1, *routed = _output_projection((o_ctx, o_nat, o_diff), x_parts, mod[0], w_out_even[0].astype(BF16),
                                     ln_g[0, 0], ln_b[0, 0], router_w_pad, router_b)
    x = _moe_ffn(*routed, 0, moe_wg, moe_wu, moe_wd, x1, mod[0], ln_g[0, 1], ln_b[0, 1])

    w1 = w_in_odd[0]
    w1 = jnp.concatenate([w1[:, :DN_CONV_CH], w1[:, DN_CONV_CH + 4 * DN_HEADS:],
                          w1[:, DN_CONV_CH:DN_CONV_CH + 4 * DN_HEADS],
                          jnp.zeros((D_MODEL, LANES - 4 * DN_HEADS), F32)], axis=1).astype(BF16)
    z1, z_ab = _input_projection((x,), mod[1], w1, f32_tail=LANES)
    o_p, new_state = _deltanet_stream(z1, z_ab, conv_odd[0], a_log_odd[0], dt_bias_odd[0], onorm_odd[0],
                                      None, BATCH, SEQ, 0, True)
    o_s, _ = _deltanet_stream(z1, z_ab, conv_odd[0], a_log_odd[0], dt_bias_odd[0], onorm_odd[0],
                              state_delta[:, 0], DEC_BATCH, DEC_SEQ, N_PROMPT, False)
    x1, *routed = _output_projection((o_p, o_s), (x,), mod[1], w_out_odd[0].astype(BF16), ln_g[1, 0], ln_b[1, 0],
                                     router_w_pad, router_b)
    y_prompt, y_sample = _moe_ffn(*routed, 1, moe_wg, moe_wu, moe_wd, x1, mod[1], ln_g[1, 1], ln_b[1, 1],
                                  split=True)
    y_prompt = y_prompt.reshape(BATCH, SEQ, D_MODEL)
    y_sample = y_sample.reshape(DEC_BATCH, DEC_SEQ, D_MODEL)
    return (y_prompt, y_sample, new_nat_k, new_nat_v, new_diff_k, new_diff_v, new_state[:, None])
```

```python
import functools
import math

import jax
import jax.numpy as jnp
import numpy as np
from jax import lax
from jax.experimental import pallas as pl
from jax.experimental.pallas import tpu as pltpu

F32 = jnp.float32
BF16 = jnp.bfloat16

D_MODEL = 1024
BATCH = 16
SEQ = 256
DEC_BATCH = 8
DEC_SEQ = 1024
PAST_LEN = 256
GRID_W = 64
N_ROWS = DEC_SEQ // GRID_W

HEAD_DIM = 64
NAT_HEADS = 8
NAT_WIN_H = 8
NAT_WIN_W = 16
DIFF_HEADS = 4
NAT_WIDTH = NAT_HEADS * HEAD_DIM
DIFF_WIDTH = DIFF_HEADS * 2 * HEAD_DIM
EVEN_IN = 3 * NAT_WIDTH + 3 * DIFF_WIDTH
ROPE_THETA = 10000.0

DN_HEADS = 8
DN_DK = 128
DN_DV = 128
DN_QK_WIDTH = DN_HEADS * DN_DK
DN_WIDTH = DN_HEADS * DN_DV
DN_CONV_CH = 2 * DN_QK_WIDTH + DN_WIDTH
DN_CHUNK = 64

N_EXPERTS = 16
N_GROUPS = 4
EXPERTS_PER_GROUP = N_EXPERTS // N_GROUPS
EXPERT_FF = 512

DEPTH = 2
ALPHA = (2 * DEPTH) ** 0.25
EPS = 1e-5

N_PROMPT = BATCH * SEQ
N_SAMPLE = DEC_BATCH * DEC_SEQ
N_TOK = N_PROMPT + N_SAMPLE
N_COND = 16

VMEM_LIMIT = 56 * 1024 * 1024
LANES = 128
N_CHUNKS = D_MODEL // LANES


def _params(*sem):
    return pltpu.CompilerParams(dimension_semantics=sem, vmem_limit_bytes=VMEM_LIMIT)


def _cond_row(row0):
    return jnp.where(row0 < N_PROMPT, 0, 1 + (row0 - N_PROMPT) // DEC_SEQ)


def _silu(x):
    return x * jax.nn.sigmoid(x)


def _layer_norm(r, g, b):
    mu = jnp.mean(r, axis=-1, keepdims=True)
    xc = r - mu
    var = jnp.mean(xc * xc, axis=-1, keepdims=True)
    return xc * lax.rsqrt(var + EPS) * g + b


def _dot(a, b):
    return jnp.dot(a, b, preferred_element_type=F32)


def _dot_nt(a, b):
    return lax.dot_general(a, b, (((1,), (1,)), ((), ())), preferred_element_type=F32)


def _dot_tn(a, b):
    return lax.dot_general(a, b, (((0,), (0,)), ((), ())), preferred_element_type=F32)


ADA_TN = 1536


def _ada_kernel(c_ref, w_ref, b_ref, o_ref):
    s = _silu(c_ref[...])
    o_ref[0] = _dot(s.astype(BF16), w_ref[0].astype(BF16)) + b_ref[0]


def _ada_modulation(cond, ada_w, ada_b):
    n = 6 * D_MODEL
    return pl.pallas_call(
        _ada_kernel,
        out_shape=jax.ShapeDtypeStruct((DEPTH, N_COND, n), F32),
        grid=(DEPTH, n // ADA_TN),
        in_specs=[
            pl.BlockSpec((N_COND, D_MODEL), lambda l, j: (0, 0)),
            pl.BlockSpec((1, D_MODEL, ADA_TN), lambda l, j: (l, 0, j)),
            pl.BlockSpec((1, 1, ADA_TN), lambda l, j: (l, 0, j)),
        ],
        out_specs=pl.BlockSpec((1, N_COND, ADA_TN), lambda l, j: (l, 0, j)),
        compiler_params=_params("arbitrary", "arbitrary"),
        name="ada_modulation",
    )(cond, ada_w, ada_b.reshape(DEPTH, 1, n))


PROJ_TM = 512


def _stream_specs(widths, tm):
    if len(widths) == 1:
        return [pl.BlockSpec((tm, widths[0]), lambda i: (i, 0))]
    n_ctx = N_PROMPT // tm
    return ([pl.BlockSpec((tm, widths[0]), lambda i: (jnp.minimum(i, n_ctx - 1), 0))]
            + [pl.BlockSpec((tm, w), lambda i: (jnp.maximum(i - n_ctx, 0), 0)) for w in widths[1:]])


def _stream_tile(refs, tm):
    if len(refs) == 1:
        return refs[0][...]
    latent = [r[...] for r in refs[1:]]
    latent = latent[0] if len(latent) == 1 else jnp.concatenate(latent, axis=1)
    return jnp.where(pl.program_id(0) * tm < N_PROMPT, refs[0][...], latent)


def _inproj_kernel(*refs, n_x):
    x_refs, (mod_ref, w_ref, o_ref), tail_refs = refs[:n_x], refs[n_x:n_x + 3], refs[n_x + 3:]
    m = mod_ref[0]
    h = _stream_tile(x_refs, o_ref.shape[0]) * (1.0 + m[1:2]) + m[0:1]
    z = _dot(h.astype(BF16), w_ref[...])
    n_main = o_ref.shape[1]
    o_ref[...] = z[:, :n_main].astype(o_ref.dtype)
    if tail_refs:
        tail_refs[0][...] = z[:, n_main:]


def _input_projection(x_parts, mod, w_bf16, tm=PROJ_TM, f32_tail=0):
    n = w_bf16.shape[1] - f32_tail
    out_shape = [jax.ShapeDtypeStruct((N_TOK, n), BF16 if f32_tail else F32)]
    out_specs = [pl.BlockSpec((tm, n), lambda i: (i, 0))]
    if f32_tail:
        out_shape.append(jax.ShapeDtypeStruct((N_TOK, f32_tail), F32))
        out_specs.append(pl.BlockSpec((tm, f32_tail), lambda i: (i, 0)))
    outs = pl.pallas_call(
        functools.partial(_inproj_kernel, n_x=len(x_parts)),
        out_shape=tuple(out_shape),
        grid=(N_TOK // tm,),
        in_specs=_stream_specs([a.shape[1] for a in x_parts], tm) + [
            pl.BlockSpec((1, 6, D_MODEL), lambda i: (_cond_row(i * tm), 0, 0)),
            pl.BlockSpec((D_MODEL, n + f32_tail), lambda i: (0, 0)),
        ],
        out_specs=tuple(out_specs),
        compiler_params=_params("arbitrary"),
        name="input_projection",
    )(*x_parts, mod, w_bf16)
    return outs if f32_tail else outs[0]


_NAT_Q0, _NAT_K0, _NAT_V0 = 0, NAT_WIDTH, 2 * NAT_WIDTH
_DIFF_Q0 = 3 * NAT_WIDTH
_DIFF_K0 = _DIFF_Q0 + DIFF_WIDTH
_DIFF_V0 = _DIFF_K0 + DIFF_WIDTH
ATTN_SCALE = HEAD_DIM ** -0.5


def _diff_lambda(lam_ref, lam_init):
    lp = lam_ref[...]
    return (jnp.exp(jnp.sum(lp[0:1] * lp[1:2], axis=-1, keepdims=True))
            - jnp.exp(jnp.sum(lp[2:3] * lp[3:4], axis=-1, keepdims=True)) + lam_init)


def _scaled_q(q):
    assert math.log2(HEAD_DIM) % 2 == 0
    return (q * ATTN_SCALE).astype(BF16)


def _with_ones(v):
    width = v.shape[1] if v.shape[1] % LANES == 0 else LANES - v.shape[1] % LANES
    return jnp.concatenate([v, jnp.ones((v.shape[0], width), v.dtype)], axis=1)


def _sub_norm(o, g, lam_init):
    ms = jnp.mean(o * o, axis=-1, keepdims=True)
    return o * lax.rsqrt(ms + EPS) * g * (1.0 - lam_init)


def _ctx_attn_kernel(z_ref, lam_ref, subg_ref, o_ref, *, lam_init):
    lam = _diff_lambda(lam_ref, lam_init)

    def softmax_av(q0, k0, v0, v_width):
        q = _scaled_q(z_ref[:, q0:q0 + HEAD_DIM])
        k = z_ref[:, k0:k0 + HEAD_DIM].astype(BF16)
        s = _dot_nt(q, k)
        yield
        e = jnp.exp(s - jnp.max(s, axis=-1, keepdims=True))
        yield
        av = _dot(e.astype(BF16), _with_ones(z_ref[:, v0:v0 + v_width].astype(BF16)))
        yield
        return av[:, :v_width] / av[:, v_width:v_width + 1]

    chains = [softmax_av(_NAT_Q0 + h * HEAD_DIM, _NAT_K0 + h * HEAD_DIM, _NAT_V0 + h * HEAD_DIM, HEAD_DIM)
              for h in range(NAT_HEADS)]
    chains += [softmax_av(_DIFF_Q0 + i * HEAD_DIM, _DIFF_K0 + i * HEAD_DIM,
                          _DIFF_V0 + (i // 2) * 2 * HEAD_DIM, 2 * HEAD_DIM)
               for i in range(2 * DIFF_HEADS)]
    outs = _run_interleaved(chains)
    for h in range(NAT_HEADS):
        o_ref[:, h * HEAD_DIM:(h + 1) * HEAD_DIM] = outs[h].astype(o_ref.dtype)
    for h in range(DIFF_HEADS):
        c = NAT_WIDTH + h * 2 * HEAD_DIM
        o = outs[NAT_HEADS + 2 * h] - lam * outs[NAT_HEADS + 2 * h + 1]
        o_ref[:, c:c + 2 * HEAD_DIM] = _sub_norm(o, subg_ref[...], lam_init).astype(o_ref.dtype)


def _context_attention(z, lam_p, subln_g, lam_init):
    return pl.pallas_call(
        functools.partial(_ctx_attn_kernel, lam_init=lam_init),
        out_shape=jax.ShapeDtypeStruct((N_PROMPT, D_MODEL), BF16),
        grid=(BATCH,),
        in_specs=[
            pl.BlockSpec((SEQ, EVEN_IN), lambda b: (b, 0)),
            pl.BlockSpec((4, HEAD_DIM), lambda b: (0, 0)),
            pl.BlockSpec((1, 2 * HEAD_DIM), lambda b: (0, 0)),
        ],
        out_specs=pl.BlockSpec((SEQ, D_MODEL), lambda b: (b, 0)),
        compiler_params=_params("arbitrary"),
        name="context_attention",
    )(z, lam_p, subln_g.reshape(1, 2 * HEAD_DIM))


NAT_TQ = 512
NAT_KEYS = 12 * GRID_W
_SAMPLE_BLK0 = N_PROMPT // DEC_SEQ


NAT_SPAN = NAT_KEYS // GRID_W
NAT_SPAN_PAD = 16


def _nat_bias_diagonals(rel_bias):
    rows = N_ROWS
    kh = min(NAT_WIN_H, rows)
    row_onehot = np.zeros((2 * NAT_WIN_H - 1, rows, NAT_SPAN), np.float32)
    for r in range(rows):
        start = min(max(r - kh // 2, 0), rows - kh)
        a0 = _nat_key0(r * GRID_W // NAT_TQ) // GRID_W
        assert a0 <= start and start + kh <= a0 + NAT_SPAN
        for a in range(start, start + kh):
            row_onehot[a - r + NAT_WIN_H - 1, r, a - a0] = 1.0
    lane_onehot = np.zeros((2 * NAT_WIN_W - 1, LANES), np.float32)
    for lane in range(2 * GRID_W - 1):
        lane_onehot[np.clip(lane - (GRID_W - 1), 1 - NAT_WIN_W, NAT_WIN_W - 1) + NAT_WIN_W - 1, lane] = 1.0
    hi = lax.Precision.HIGHEST
    by_row = jnp.einsum('hed,era->hrad', rel_bias.astype(F32), row_onehot, precision=hi)
    diag = jnp.einsum('hrad,dl->hral', by_row, lane_onehot, precision=hi)
    diag = jnp.where((row_onehot.sum(0) > 0)[None, :, :, None], diag, -jnp.inf)
    return jnp.pad(diag, ((0, 0), (0, 0), (0, NAT_SPAN_PAD - NAT_SPAN), (0, 0)))


def _nat_bias_tiles(diag_ref, bias_ref):
    shape = (GRID_W, LANES)
    q = lax.broadcasted_iota(jnp.int32, shape, 0)
    lane = lax.broadcasted_iota(jnp.int32, shape, 1)
    k = lane % GRID_W
    c0 = jnp.clip(q - NAT_WIN_W // 2, 0, GRID_W - NAT_WIN_W)
    in_cols = (k >= c0) & (k < c0 + NAT_WIN_W)
    for j in range(2):
        for r in range(NAT_TQ // GRID_W):
            for pair in range(NAT_SPAN // 2):
                even = jnp.broadcast_to(diag_ref[j, r, 2 * pair:2 * pair + 1, :], shape)
                odd = jnp.broadcast_to(diag_ref[j, r, 2 * pair + 1:2 * pair + 2, :], shape)
                even = pltpu.roll(even, GRID_W + 1, 1, stride=1, stride_axis=0)
                odd = pltpu.roll(odd, 1, 1, stride=1, stride_axis=0)
                tile = jnp.where(lane < GRID_W, even, odd)
                bias_ref[j, r * GRID_W:(r + 1) * GRID_W, pair * LANES:(pair + 1) * LANES] = (
                    jnp.where(in_cols, tile, -jnp.inf))


def _nat_key0(qt):
    half = DEC_SEQ // NAT_TQ // 2
    return (qt >= half) * (DEC_SEQ - NAT_KEYS)


def _nat_latent_kernel(q_ref, k_ref, v_ref, ck_ref, cv_ref, diag_ref, o_ref, bias_ref):
    key0 = pl.multiple_of(_nat_key0(pl.program_id(1)).astype(jnp.int32), DEC_SEQ - NAT_KEYS)
    keys = pl.ds(key0, NAT_KEYS)

    @pl.when(pl.program_id(2) == 0)
    def _():
        _nat_bias_tiles(diag_ref, bias_ref)

    def head(j):
        sl = slice(j * HEAD_DIM, (j + 1) * HEAD_DIM)
        q = _scaled_q(q_ref[:, sl])
        s_loc = _dot_nt(q, k_ref[keys, sl].astype(BF16)) + bias_ref[j]
        s_ctx = _dot_nt(q, ck_ref[0, :, sl].astype(BF16))
        yield
        m = jnp.maximum(jnp.max(s_loc, axis=-1, keepdims=True), jnp.max(s_ctx, axis=-1, keepdims=True))
        yield
        e_loc = jnp.exp(s_loc - m)
        e_ctx = jnp.exp(s_ctx - m)
        yield
        o = (_dot(e_loc.astype(BF16), _with_ones(v_ref[keys, sl].astype(BF16)))
             + _dot(e_ctx.astype(BF16), _with_ones(cv_ref[0, :, sl].astype(BF16))))
        yield
        return (o[:, :HEAD_DIM] / o[:, HEAD_DIM:HEAD_DIM + 1]).astype(o_ref.dtype)

    outs = _run_interleaved([head(j) for j in range(2)])
    for j in range(2):
        o_ref[:, j * HEAD_DIM:(j + 1) * HEAD_DIM] = outs[j]


def _nat_latent_attention(z, ck, cv, diag):
    nq = DEC_SEQ // NAT_TQ
    qblk0 = N_PROMPT // NAT_TQ
    pair = 2 * HEAD_DIM
    return pl.pallas_call(
        _nat_latent_kernel,
        out_shape=jax.ShapeDtypeStruct((N_SAMPLE, NAT_WIDTH), BF16),
        grid=(NAT_HEADS // 2, nq, DEC_BATCH),
        in_specs=[
            pl.BlockSpec((NAT_TQ, pair), lambda hp, qt, b: (qblk0 + b * nq + qt, _NAT_Q0 // pair + hp)),
            pl.BlockSpec((DEC_SEQ, pair), lambda hp, qt, b: (_SAMPLE_BLK0 + b, _NAT_K0 // pair + hp)),
            pl.BlockSpec((DEC_SEQ, pair), lambda hp, qt, b: (_SAMPLE_BLK0 + b, _NAT_V0 // pair + hp)),
            pl.BlockSpec((1, PAST_LEN, pair), lambda hp, qt, b: (b, 0, hp)),
            pl.BlockSpec((1, PAST_LEN, pair), lambda hp, qt, b: (b, 0, hp)),
            pl.BlockSpec((2, NAT_TQ // GRID_W, NAT_SPAN_PAD, LANES), lambda hp, qt, b: (hp, qt, 0, 0)),
        ],
        out_specs=pl.BlockSpec((NAT_TQ, pair), lambda hp, qt, b: (b * nq + qt, hp)),
        scratch_shapes=[pltpu.VMEM((2, NAT_TQ, NAT_KEYS), F32)],
        compiler_params=_params("arbitrary", "arbitrary", "arbitrary"),
        name="nat_latent_attention",
    )(z, z, z, ck, cv, diag)


DIFF_TQ = 1024


def _rope_tables():
    nf = HEAD_DIM // 4
    t = jnp.arange(DEC_SEQ)
    inv = ROPE_THETA ** (-jnp.arange(nf, dtype=F32) / nf)
    ang_r = (t // GRID_W).astype(F32)[:, None] * inv
    ang_c = (t % GRID_W).astype(F32)[:, None] * inv
    cos = jnp.concatenate([jnp.cos(ang_r)] * 2 + [jnp.cos(ang_c)] * 2, axis=-1)
    sin = jnp.concatenate([-jnp.sin(ang_r), jnp.sin(ang_r), -jnp.sin(ang_c), jnp.sin(ang_c)], axis=-1)
    return jnp.tile(cos, (1, 2)), jnp.tile(sin, (1, 2))


def _rope(x, cos, sin):
    nf = HEAD_DIM // 4
    lane = lax.broadcasted_iota(jnp.int32, x.shape, 1)
    upper = (lane // nf) % 2 == 1
    partner = jnp.where(upper, pltpu.roll(x, nf, axis=1), pltpu.roll(x, x.shape[1] - nf, axis=1))
    return x * cos + partner * sin


def _diff_latent_kernel(q_ref, k_ref, v_ref, ck_ref, cv_ref, cos_ref, sin_ref, lam_ref, subg_ref,
                        o_ref, kr_ref, *, lam_init):
    qt = pl.program_id(2)

    @pl.when(qt == 0)
    def _():
        kr_ref[...] = _rope(k_ref[...], cos_ref[...], sin_ref[...]).astype(BF16)

    lam = _diff_lambda(lam_ref, lam_init)
    row0 = pl.multiple_of(qt * DIFF_TQ, DIFF_TQ)
    q = _scaled_q(_rope(q_ref[...], cos_ref[pl.ds(row0, DIFF_TQ), :], sin_ref[pl.ds(row0, DIFF_TQ), :]))
    v_loc = _with_ones(v_ref[...].astype(BF16))
    v_ctx = _with_ones(cv_ref[0].astype(BF16))
    width = 2 * HEAD_DIM

    def softmax_av(j):
        sl = slice(j * HEAD_DIM, (j + 1) * HEAD_DIM)
        s_ctx = _dot_nt(q[:, sl], ck_ref[0, :, sl].astype(BF16))
        s_loc = _dot_nt(q[:, sl], kr_ref[:, sl])
        yield
        m = jnp.maximum(jnp.max(s_loc, axis=-1, keepdims=True), jnp.max(s_ctx, axis=-1, keepdims=True))
        yield
        e_loc = jnp.exp(s_loc - m)
        e_ctx = jnp.exp(s_ctx - m)
        yield
        av = _dot(e_loc.astype(BF16), v_loc) + _dot(e_ctx.astype(BF16), v_ctx)
        yield
        return av[:, :width] / av[:, width:width + 1]

    parts = _run_interleaved([softmax_av(j) for j in range(2)])
    o = parts[0] - lam * parts[1]
    o_ref[...] = _sub_norm(o, subg_ref[...], lam_init).astype(o_ref.dtype)


def _diff_latent_attention(z, ck, cv, lam_p, subln_g, lam_init):
    nq = DEC_SEQ // DIFF_TQ
    qblk0 = N_PROMPT // DIFF_TQ
    w = 2 * HEAD_DIM
    cos, sin = _rope_tables()
    return pl.pallas_call(
        functools.partial(_diff_latent_kernel, lam_init=lam_init),
        out_shape=jax.ShapeDtypeStruct((N_SAMPLE, DIFF_WIDTH), BF16),
        grid=(DEC_BATCH, DIFF_HEADS, nq),
        in_specs=[
            pl.BlockSpec((DIFF_TQ, w), lambda b, h, qt: (qblk0 + b * nq + qt, _DIFF_Q0 // w + h)),
            pl.BlockSpec((DEC_SEQ, w), lambda b, h, qt: (_SAMPLE_BLK0 + b, _DIFF_K0 // w + h)),
            pl.BlockSpec((DEC_SEQ, w), lambda b, h, qt: (_SAMPLE_BLK0 + b, _DIFF_V0 // w + h)),
            pl.BlockSpec((1, PAST_LEN, w), lambda b, h, qt: (b, 0, h)),
            pl.BlockSpec((1, PAST_LEN, w), lambda b, h, qt: (b, 0, h)),
            pl.BlockSpec((DEC_SEQ, w), lambda b, h, qt: (0, 0)),
            pl.BlockSpec((DEC_SEQ, w), lambda b, h, qt: (0, 0)),
            pl.BlockSpec((4, HEAD_DIM), lambda b, h, qt: (0, 0)),
            pl.BlockSpec((1, w), lambda b, h, qt: (0, 0)),
        ],
        out_specs=pl.BlockSpec((DIFF_TQ, w), lambda b, h, qt: (b * nq + qt, h)),
        scratch_shapes=[pltpu.VMEM((DEC_SEQ, w), BF16)],
        compiler_params=_params("arbitrary", "arbitrary", "arbitrary"),
        name="diff_latent_attention",
    )(z, z, z, ck, cv, cos, sin, lam_p, subln_g.reshape(1, w))


OUT_TM = 256


def _route(logits_t, rb_col):
    sc = jax.nn.sigmoid(logits_t)
    bi = sc + rb_col
    srow = [sc[e:e + 1] for e in range(N_EXPERTS)]
    brow = [bi[e:e + 1] for e in range(N_EXPERTS)]
    n = EXPERTS_PER_GROUP
    gscore = []
    for g in range(N_GROUPS):
        v = brow[g * n:(g + 1) * n]
        best = None
        for i in range(n):
            for j in range(i + 1, n):
                s = v[i] + v[j]
                best = s if best is None else jnp.maximum(best, s)
        gscore.append(best)
    sel = jnp.zeros_like(gscore[0], dtype=jnp.int32)
    best = gscore[0]
    for g in range(1, N_GROUPS):
        better = gscore[g] > best
        sel = jnp.where(better, g, sel)
        best = jnp.where(better, gscore[g], best)

    def pick_group(rows, i):
        out = rows[(N_GROUPS - 1) * n + i]
        for g in range(N_GROUPS - 2, -1, -1):
            out = jnp.where(sel == g, rows[g * n + i], out)
        return out

    bv = [pick_group(brow, i) for i in range(n)]
    sv = [pick_group(srow, i) for i in range(n)]
    i1 = jnp.zeros_like(sel)
    m1, w1 = bv[0], sv[0]
    for i in range(1, n):
        better = bv[i] > m1
        i1 = jnp.where(better, i, i1)
        m1 = jnp.where(better, bv[i], m1)
        w1 = jnp.where(better, sv[i], w1)
    i2 = jnp.zeros_like(sel)
    m2 = jnp.full_like(m1, -jnp.inf)
    w2 = jnp.zeros_like(w1)
    for i in range(n):
        better = (i1 != i) & (bv[i] > m2)
        i2 = jnp.where(better, i, i2)
        m2 = jnp.where(better, bv[i], m2)
        w2 = jnp.where(better, sv[i], w2)
    tot = w1 + w2
    return sel * n + i1, sel * n + i2, w1 / tot, w2 / tot


def _store_token_tiles(ref, value):
    rows = value.shape[0]
    chunks = jnp.stack([value[:, c * LANES:(c + 1) * LANES] for c in range(N_CHUNKS)], axis=0)
    tiles = jnp.transpose(chunks.reshape(N_CHUNKS, rows // 8, 8, LANES), (1, 2, 0, 3))
    ref[...] = tiles.reshape(rows, N_CHUNKS, LANES)


def _load_token_tiles(ref):
    rows = ref.shape[0]
    tiles = ref[...].reshape(rows // 8, 8, N_CHUNKS, LANES)
    chunks = jnp.transpose(tiles, (2, 0, 1, 3)).reshape(N_CHUNKS, rows, LANES)
    return jnp.concatenate([chunks[c] for c in range(N_CHUNKS)], axis=1)


def _deltanet_mixer_out(o_refs, tm):
    of = _stream_tile(o_refs[0:2], tm).astype(F32)
    ob = _stream_tile(o_refs[2:4], tm).astype(F32)
    gate_ref, g_ref = o_refs[4], o_refs[5]
    pieces = []
    for h in range(DN_HEADS):
        sl = slice(h * DN_DV, (h + 1) * DN_DV)
        o = of[:, sl] + ob[:, sl]
        o = o * lax.rsqrt(jnp.mean(o * o, axis=-1, keepdims=True) + EPS) * g_ref[...]
        pieces.append((o * _silu(gate_ref[:, sl].astype(F32))).astype(BF16))
    return jnp.concatenate(pieces, axis=1)


def _outproj_kernel(*refs, n_o, n_x, deltanet):
    o_refs, x_refs, refs = refs[:n_o], refs[n_o:n_o + n_x], refs[n_o + n_x:]
    mod_ref, w_ref, lng_ref, lnb_ref, rw_ref, rb_ref, x1_ref, h2_ref, ri_ref, rw_out_ref, cnt_ref = refs
    tm = x1_ref.shape[0]
    m = mod_ref[0]
    y = _dot(_deltanet_mixer_out(o_refs, tm) if deltanet else _stream_tile(o_refs, tm), w_ref[...])
    x1 = _layer_norm(ALPHA * _stream_tile(x_refs, tm) + m[2:3] * y, lng_ref[...], lnb_ref[...])
    x1_ref[...] = x1
    h2 = x1 * (1.0 + m[4:5]) + m[3:4]
    _store_token_tiles(h2_ref, h2)
    rw = rw_ref[...]
    h_hi, rw_hi = h2.astype(BF16), rw.astype(BF16)
    h_lo = (h2 - h_hi.astype(F32)).astype(BF16)
    rw_lo = (rw - rw_hi.astype(F32)).astype(BF16)
    logits = _dot(h_hi, rw_hi) + (_dot(h_lo, rw_hi) + _dot(h_hi, rw_lo))
    e1, e2, w1, w2 = _route(logits.T[:N_EXPERTS], rb_ref[...])
    eid = lax.broadcasted_iota(jnp.int32, (N_EXPERTS, tm), 0)
    onehot = (eid == e1) | (eid == e2)
    ti = lax.broadcasted_iota(jnp.int32, (tm, tm), 0)
    tj = lax.broadcasted_iota(jnp.int32, (tm, tm), 1)
    earlier = jnp.where(ti < tj, 1.0, 0.0).astype(BF16)
    rank = _dot(jnp.where(onehot, 1.0, 0.0).astype(BF16), earlier)
    r1 = jnp.sum(jnp.where(eid == e1, rank, 0.0), axis=0, keepdims=True).astype(jnp.int32)
    r2 = jnp.sum(jnp.where(eid == e2, rank, 0.0), axis=0, keepdims=True).astype(jnp.int32)
    ri_ref[...] = jnp.concatenate([e1, e2, r1, r2, jnp.zeros((4, tm), jnp.int32)], axis=0)
    wt = jnp.concatenate([w1, w2, jnp.zeros((LANES - 2, tm), F32)], axis=0)
    rw_out_ref[...] = wt.T
    cnt = jnp.sum(jnp.where(onehot, 1.0, 0.0), axis=1, keepdims=True)
    cnt_ref[0] = jnp.broadcast_to(cnt, (N_EXPERTS, LANES)).astype(jnp.int32)


def _output_projection(o_parts, x_parts, mod, w_bf16, ln_g, ln_b, router_w_pad, router_b, tm=OUT_TM,
                       deltanet=None):
    nt = N_TOK // tm
    if deltanet is None:
        o_specs = _stream_specs([a.shape[1] for a in o_parts], tm)
    else:
        z1, onorm_g = deltanet
        o_specs = (_stream_specs([DN_WIDTH, DN_WIDTH], tm) + _stream_specs([DN_WIDTH, DN_WIDTH], tm)
                   + [pl.BlockSpec((tm, DN_WIDTH), lambda i: (i, DN_GATE_COL0 // DN_WIDTH)),
                      pl.BlockSpec((1, DN_DV), lambda i: (0, 0))])
        o_parts = tuple(o_parts) + (z1, onorm_g.reshape(1, DN_DV))
    return pl.pallas_call(
        functools.partial(_outproj_kernel, n_o=len(o_parts), n_x=len(x_parts), deltanet=deltanet is not None),
        out_shape=(jax.ShapeDtypeStruct((N_TOK, D_MODEL), F32),
                   jax.ShapeDtypeStruct((N_TOK, N_CHUNKS, LANES), F32),
                   jax.ShapeDtypeStruct((8, N_TOK), jnp.int32),
                   jax.ShapeDtypeStruct((N_TOK, LANES), F32),
                   jax.ShapeDtypeStruct((nt, N_EXPERTS, LANES), jnp.int32)),
        grid=(nt,),
        in_specs=o_specs + _stream_specs([a.shape[1] for a in x_parts], tm) + [
            pl.BlockSpec((1, 6, D_MODEL), lambda i: (_cond_row(i * tm), 0, 0)),
            pl.BlockSpec((D_MODEL, D_MODEL), lambda i: (0, 0)),
            pl.BlockSpec((1, D_MODEL), lambda i: (0, 0)),
            pl.BlockSpec((1, D_MODEL), lambda i: (0, 0)),
            pl.BlockSpec((D_MODEL, LANES), lambda i: (0, 0)),
            pl.BlockSpec((N_EXPERTS, 1), lambda i: (0, 0)),
        ],
        out_specs=(pl.BlockSpec((tm, D_MODEL), lambda i: (i, 0)),
                   pl.BlockSpec((tm, N_CHUNKS, LANES), lambda i: (i, 0, 0)),
                   pl.BlockSpec((8, tm), lambda i: (0, i)),
                   pl.BlockSpec((tm, LANES), lambda i: (i, 0)),
                   pl.BlockSpec((1, N_EXPERTS, LANES), lambda i: (i, 0, 0))),
        compiler_params=_params("arbitrary"),
        name="output_projection",
    )(*o_parts, *x_parts, mod, w_bf16, ln_g.reshape(1, D_MODEL), ln_b.reshape(1, D_MODEL),
      router_w_pad, router_b.reshape(N_EXPERTS, 1))


N_ASSIGN = 2 * N_TOK
MOE_TM = 256
MOE_ROWS = N_ASSIGN + N_EXPERTS * MOE_TM
MOE_TILES = MOE_ROWS // MOE_TM
DISPATCH_TM = 1024
COMBINE_TM = 256


def _moe_plan(route_i, counts):
    cnt = counts[:, :, 0]
    total = jnp.sum(cnt, axis=0)
    padded = (total + MOE_TM - 1) // MOE_TM * MOE_TM
    seg_end = jnp.cumsum(padded)
    seg_start = seg_end - padded
    tile_base = seg_start[None, :] + jnp.cumsum(cnt, axis=0) - cnt
    base_tok = jnp.repeat(tile_base, OUT_TM, axis=0)
    eids = jnp.arange(N_EXPERTS, dtype=jnp.int32)[None, :]
    pos = [jnp.sum(jnp.where(route_i[k][:, None] == eids, base_tok, 0), axis=1) + route_i[2 + k]
           for k in range(2)]
    pos = jnp.concatenate(pos).astype(jnp.int32)
    tile_row0 = jnp.arange(MOE_TILES, dtype=jnp.int32) * MOE_TM
    tile_expert = jnp.minimum(jnp.sum(seg_end[None, :] <= tile_row0[:, None], axis=1), N_EXPERTS - 1)
    n_tiles = (seg_end[-1] // MOE_TM).reshape(1)
    last = MOE_ROWS - MOE_TM
    fill_rows = jnp.concatenate([jnp.minimum(seg_start + total, last),
                                 last - jnp.arange(N_EXPERTS, dtype=jnp.int32) * MOE_TM])
    return pos, tile_expert.astype(jnp.int32), n_tiles.astype(jnp.int32), fill_rows.astype(jnp.int32)


def _row_copy_wait(src_hbm, dst, sem, n_rows):
    pltpu.make_async_copy(src_hbm.at[pl.ds(0, n_rows)], dst.at[pl.ds(0, n_rows)], sem).wait()


def _dispatch_kernel(pos_ref, fill_ref, h_ref, xs_ref, zero_ref, sem, fill_sem):
    tm = h_ref.shape[0]
    t0 = pl.program_id(0) * tm

    @pl.when(pl.program_id(0) == 0)
    def _():
        zero_ref[...] = jnp.zeros_like(zero_ref)
        fills = [pltpu.make_async_copy(zero_ref, xs_ref.at[pl.ds(fill_ref[j], MOE_TM)], fill_sem)
                 for j in range(2 * N_EXPERTS)]
        for cp in fills[N_EXPERTS:]:
            cp.start()
        for cp in fills[N_EXPERTS:]:
            cp.wait()
        for cp in fills[:N_EXPERTS]:
            cp.start()
            cp.wait()

    def body(r, carry):
        for k in range(2):
            pltpu.make_async_copy(h_ref.at[r], xs_ref.at[pos_ref[k * N_TOK + t0 + r]], sem).start()
        return carry

    lax.fori_loop(0, tm, body, 0, unroll=8)
    for _ in range(2):
        _row_copy_wait(h_ref, xs_ref, sem, tm)


def _moe_dispatch(pos, fill_rows, h2_tiles):
    tm = DISPATCH_TM
    return pl.pallas_call(
        _dispatch_kernel,
        out_shape=jax.ShapeDtypeStruct((MOE_ROWS, N_CHUNKS, LANES), F32),
        grid_spec=pltpu.PrefetchScalarGridSpec(
            num_scalar_prefetch=2,
            grid=(N_TOK // tm,),
            in_specs=[pl.BlockSpec((tm, N_CHUNKS, LANES), lambda i, pos, fill: (i, 0, 0))],
            out_specs=pl.BlockSpec(memory_space=pl.ANY),
            scratch_shapes=[pltpu.VMEM((MOE_TM, N_CHUNKS, LANES), F32), pltpu.SemaphoreType.DMA,
                            pltpu.SemaphoreType.DMA],
        ),
        compiler_params=_params("arbitrary"),
        name="moe_dispatch",
    )(pos, fill_rows, h2_tiles)


def _expert_kernel(te_ref, nt_ref, xs_ref, wg_ref, wu_ref, wd_ref, ys_ref, wg16_ref, wu16_ref, wd16_ref):
    i = pl.program_id(0)

    @pl.when(i < nt_ref[0])
    def _():
        @pl.when((i == 0) | (te_ref[i] != te_ref[jnp.maximum(i - 1, 0)]))
        def _():
            wg16_ref[...] = wg_ref[0, 0].astype(BF16)
            wu16_ref[...] = wu_ref[0, 0].astype(BF16)
            wd16_ref[...] = wd_ref[0, 0].astype(BF16)

        x = _load_token_tiles(xs_ref).astype(BF16)
        he = _silu(_dot(x, wg16_ref[...])) * _dot(x, wu16_ref[...])
        _store_token_tiles(ys_ref, _dot(he.astype(BF16), wd16_ref[...]))

    @pl.when(i >= nt_ref[0])
    def _():
        ys_ref[...] = jnp.zeros_like(ys_ref)


def _moe_experts(tile_expert, n_tiles, xs, layer, wg, wu, wd):
    row_blk = lambda i, te, nt: (jnp.minimum(i, nt[0] - 1), 0, 0)
    w_blk = lambda i, te, nt: (layer, te[i], 0, 0)
    return pl.pallas_call(
        _expert_kernel,
        out_shape=jax.ShapeDtypeStruct((MOE_ROWS, N_CHUNKS, LANES), F32),
        grid_spec=pltpu.PrefetchScalarGridSpec(
            num_scalar_prefetch=2,
            grid=(MOE_TILES,),
            in_specs=[
                pl.BlockSpec((MOE_TM, N_CHUNKS, LANES), row_blk),
                pl.BlockSpec((1, 1, D_MODEL, EXPERT_FF), w_blk),
                pl.BlockSpec((1, 1, D_MODEL, EXPERT_FF), w_blk),
                pl.BlockSpec((1, 1, EXPERT_FF, D_MODEL), w_blk),
            ],
            out_specs=pl.BlockSpec((MOE_TM, N_CHUNKS, LANES), lambda i, te, nt: (i, 0, 0)),
            scratch_shapes=[pltpu.VMEM((D_MODEL, EXPERT_FF), BF16), pltpu.VMEM((D_MODEL, EXPERT_FF), BF16),
                            pltpu.VMEM((EXPERT_FF, D_MODEL), BF16)],
        ),
        compiler_params=_params("arbitrary"),
        name="moe_experts",
    )(tile_expert, n_tiles, xs, wg, wu, wd)


def _combine_kernel(pos_ref, gw_ref, x1_ref, mod_ref, lng_ref, lnb_ref, ys_ref, *rest, split):
    out_refs, (buf_ref, sem) = rest[:-2], rest[-2:]
    i = pl.program_id(0)
    n = pl.num_programs(0)
    tm = COMBINE_TM

    def start_row(tile, to_slot, r):
        for k in range(2):
            p = pos_ref[k * N_TOK + tile * tm + r]
            pltpu.make_async_copy(ys_ref.at[p], buf_ref.at[to_slot, k, r], sem.at[to_slot]).start()

    def wait_slot(s):
        for k in range(2):
            _row_copy_wait(ys_ref, buf_ref.at[s, k], sem.at[s], tm)

    def issue(tile, to_slot):
        def body(r, carry):
            start_row(tile, to_slot, r)
            return carry
        lax.fori_loop(0, tm, body, 0, unroll=8)

    @pl.when(i == 0)
    def _():
        issue(0, 0)

    @pl.when(i + 1 < n)
    def _():
        issue(i + 1, (i + 1) % 2)

    slot = i % 2
    wait_slot(slot)
    gw = gw_ref[...]
    f = gw[:, 0:1] * _load_token_tiles(buf_ref.at[slot, 0]) + gw[:, 1:2] * _load_token_tiles(buf_ref.at[slot, 1])
    m = mod_ref[0]
    out = _layer_norm(ALPHA * x1_ref[...] + m[5:6] * f, lng_ref[...], lnb_ref[...])

    if split:
        prompt_ref, sample_ref = out_refs

        @pl.when(i < N_PROMPT // tm)
        def _():
            prompt_ref[...] = out

        @pl.when(i >= N_PROMPT // tm)
        def _():
            sample_ref[...] = out
    else:
        out_refs[0][...] = out


def _moe_combine(pos, gate_w, x1, mod, ln_g, ln_b, ys, split):
    tm = COMBINE_TM
    np_blk = N_PROMPT // tm
    if split:
        out_shape = (jax.ShapeDtypeStruct((N_PROMPT, D_MODEL), F32), jax.ShapeDtypeStruct((N_SAMPLE, D_MODEL), F32))
        out_specs = (pl.BlockSpec((tm, D_MODEL), lambda i, pos: (jnp.minimum(i, np_blk - 1), 0)),
                     pl.BlockSpec((tm, D_MODEL), lambda i, pos: (jnp.maximum(i - np_blk, 0), 0)))
    else:
        out_shape = jax.ShapeDtypeStruct((N_TOK, D_MODEL), F32)
        out_specs = pl.BlockSpec((tm, D_MODEL), lambda i, pos: (i, 0))
    return pl.pallas_call(
        functools.partial(_combine_kernel, split=split),
        out_shape=out_shape,
        grid_spec=pltpu.PrefetchScalarGridSpec(
            num_scalar_prefetch=1,
            grid=(N_TOK // tm,),
            in_specs=[
                pl.BlockSpec((tm, LANES), lambda i, pos: (i, 0)),
                pl.BlockSpec((tm, D_MODEL), lambda i, pos: (i, 0)),
                pl.BlockSpec((1, 6, D_MODEL), lambda i, pos: (_cond_row(i * tm), 0, 0)),
                pl.BlockSpec((1, D_MODEL), lambda i, pos: (0, 0)),
                pl.BlockSpec((1, D_MODEL), lambda i, pos: (0, 0)),
                pl.BlockSpec(memory_space=pl.ANY),
            ],
            out_specs=out_specs,
            scratch_shapes=[pltpu.VMEM((2, 2, tm, N_CHUNKS, LANES), F32), pltpu.SemaphoreType.DMA((2,))],
        ),
        compiler_params=_params("arbitrary"),
        name="moe_combine",
    )(pos, gate_w, x1, mod, ln_g.reshape(1, D_MODEL), ln_b.reshape(1, D_MODEL), ys)


def _moe_ffn(h2_tiles, route_i, gate_w, counts, layer, wg, wu, wd, x1, mod, ln_g, ln_b, split=False):
    pos, tile_expert, n_tiles, fill_rows = _moe_plan(route_i, counts)
    xs = _moe_dispatch(pos, fill_rows, h2_tiles)
    ys = _moe_experts(tile_expert, n_tiles, xs, layer, wg, wu, wd)
    return _moe_combine(pos, gate_w, x1, mod, ln_g, ln_b, ys, split)


DN_GATE_COL0 = DN_CONV_CH
N_DIRS = 2


def _dn_prep_kernel(q_ref, k_ref, v_ref, wq_ref, wk_ref, wv_ref, qo_ref, ko_ref, vo_ref):
    length = q_ref.shape[0]
    row = lax.broadcasted_iota(jnp.int32, (length, DN_DK), 0)

    def conv(x, w):
        prev = jnp.where(row == 0, 0.0, pltpu.roll(x, 1, axis=0))
        nxt = jnp.where(row == length - 1, 0.0, pltpu.roll(x, length - 1, axis=0))
        return _silu(prev * w[0:1] + x * w[1:2] + nxt * w[2:3])

    def l2n(x):
        return x * lax.rsqrt(jnp.sum(x * x, axis=-1, keepdims=True) + 1e-6)

    for h in range(DN_HEADS):
        sl = slice(h * DN_DK, (h + 1) * DN_DK)
        q, k, v = (r[:, sl].astype(F32) for r in (q_ref, k_ref, v_ref))
        qo_ref[h] = (l2n(conv(q, wq_ref[:, sl])) * (DN_DK ** -0.5)).astype(qo_ref.dtype)
        ko_ref[h] = l2n(conv(k, wk_ref[:, sl])).astype(ko_ref.dtype)
        vo_ref[h] = conv(v, wv_ref[:, sl]).astype(vo_ref.dtype)


def _dn_prep(z1, conv_w, nb, length, row_blk0):
    shp = jax.ShapeDtypeStruct((DN_HEADS, nb * length, DN_DK), BF16)
    blk = lambda part: pl.BlockSpec((length, DN_QK_WIDTH), lambda b: (row_blk0 + b, part))
    wblk = lambda part: pl.BlockSpec((3, DN_QK_WIDTH), lambda b: (0, part))
    oblk = pl.BlockSpec((DN_HEADS, length, DN_DK), lambda b: (0, b, 0))
    return pl.pallas_call(
        _dn_prep_kernel,
        out_shape=(shp, shp, shp),
        grid=(nb,),
        in_specs=[blk(0), blk(1), blk(2), wblk(0), wblk(1), wblk(2)],
        out_specs=(oblk, oblk, oblk),
        compiler_params=_params("arbitrary"),
        name="deltanet_prep",
    )(z1, z1, z1, conv_w, conv_w, conv_w)


def _dn_gates_kernel(ab_ref, alog_ref, dtb_ref, o_ref):
    length = ab_ref.shape[0]
    x = ab_ref[...]
    a = x + dtb_ref[...]
    softplus = jnp.maximum(a, 0.0) + jnp.log1p(jnp.exp(-jnp.abs(a)))
    g = -jnp.exp(alog_ref[...]) * softplus
    beta = jax.nn.sigmoid(x)
    ri = lax.broadcasted_iota(jnp.int32, (DN_CHUNK, DN_CHUNK), 0)
    ci = lax.broadcasted_iota(jnp.int32, (DN_CHUNK, DN_CHUNK), 1)
    tril = (ri >= ci).astype(F32)
    triu = (ri <= ci).astype(F32)
    lane = lax.broadcasted_iota(jnp.int32, (DN_CHUNK, LANES), 1)
    for c in range(length // DN_CHUNK):
        rows = slice(c * DN_CHUNK, (c + 1) * DN_CHUNK)
        gch = g[rows]
        pre = jnp.dot(tril, gch, precision=lax.Precision.HIGHEST, preferred_element_type=F32)
        suf = jnp.dot(triu, gch, precision=lax.Precision.HIGHEST, preferred_element_type=F32)
        gc = jnp.where(lane < DN_HEADS, pre, suf)
        o_ref[rows, :] = jnp.where(lane < N_DIRS * DN_HEADS, gc, beta[rows])


def _dn_gates(z_ab, a_log, dt_bias, nb, length, row_blk0):
    pad = LANES - N_DIRS * DN_HEADS
    alog = jnp.pad(a_log.reshape(1, -1).astype(F32), ((0, 0), (0, pad)))
    dtb = jnp.pad(dt_bias.reshape(1, -1).astype(F32), ((0, 0), (0, pad)))
    return pl.pallas_call(
        _dn_gates_kernel,
        out_shape=jax.ShapeDtypeStruct((nb * length, LANES), F32),
        grid=(nb,),
        in_specs=[
            pl.BlockSpec((length, LANES), lambda b: (row_blk0 + b, 0)),
            pl.BlockSpec((1, LANES), lambda b: (0, 0)),
            pl.BlockSpec((1, LANES), lambda b: (0, 0)),
        ],
        out_specs=pl.BlockSpec((length, LANES), lambda b: (b, 0)),
        compiler_params=_params("arbitrary"),
        name="deltanet_gates",
    )(z_ab, alog, dtb)


def _delta_chunk(q, k, v, beta, gcol, grow, s, lower):
    ri = lax.broadcasted_iota(jnp.int32, (DN_CHUNK, DN_CHUNK), 0)
    ci = lax.broadcasted_iota(jnp.int32, (DN_CHUNK, DN_CHUNK), 1)
    incl = (ri >= ci) if lower else (ri <= ci)
    strict = (ri > ci) if lower else (ri < ci)
    decay = jnp.exp(jnp.where(incl, gcol - grow, -jnp.inf))
    kb = k * beta
    kq = _dot_nt(jnp.concatenate([kb, q], axis=0).astype(BF16), k.astype(BF16))
    kk, a_qk = kq[:DN_CHUNK], kq[DN_CHUNK:] * decay
    yield
    tri_l = jnp.where(strict, kk * decay, 0.0)

    def off_block(size):
        same = (ri // (2 * size)) == (ci // (2 * size))
        rpar, cpar = (ri // size) % 2, (ci // size) % 2
        return same & ((rpar == 1) & (cpar == 0) if lower else (rpar == 0) & (cpar == 1))

    p = jnp.where(ri == ci, 1.0, 0.0) - jnp.where(off_block(1), tri_l, 0.0)
    size = 2
    while size < DN_CHUNK:
        p16 = p.astype(BF16)
        pc = _dot(p16, jnp.where(off_block(size), tri_l, 0.0).astype(BF16))
        yield
        p = p - _dot(pc.astype(BF16), p16)
        yield
        size *= 2
    eg = jnp.exp(gcol)
    uw = _dot(p.astype(BF16), jnp.concatenate([v * beta, kb * eg], axis=1).astype(BF16))
    yield
    u, w = uw[:, :DN_DV], uw[:, DN_DV:]
    ws = _dot(jnp.concatenate([w, q * eg], axis=0).astype(BF16), s.astype(BF16))
    v_new, o_inter = u - ws[:DN_CHUNK], ws[DN_CHUNK:]
    yield
    v_new16 = v_new.astype(BF16)
    o = o_inter + _dot(a_qk.astype(BF16), v_new16)
    g_last = gcol[DN_CHUNK - 1:DN_CHUNK] if lower else gcol[0:1]
    kd = k * jnp.exp(g_last - gcol)
    s_new = s * jnp.exp(g_last) + _dot_tn(kd.astype(BF16), v_new16)
    return o, s_new


def _run_interleaved(chains):
    results = [None] * len(chains)
    active = list(enumerate(chains))
    while active:
        still = []
        for idx, gen in active:
            try:
                next(gen)
                still.append((idx, gen))
            except StopIteration as stop:
                results[idx] = stop.value
        active = still
    return results


def _dn_scan_kernel(*refs, has_init, want_final):
    (qf_ref, kf_ref, vf_ref, gf_ref, qb_ref, kb_ref, vb_ref, gb_ref), rest = refs[:8], refs[8:]
    if has_init:
        s0_ref, rest = rest[0], rest[1:]
    of_ref, ob_ref = rest[0], rest[1]
    rest = rest[2:]
    if want_final:
        sf_ref, rest = rest[0], rest[1:]
    s_ref = rest[0]
    c = pl.program_id(1)
    nc = pl.num_programs(1)
    per_request = N_DIRS * DN_HEADS
    states = [(u, d, h) for u in range(DN_SCAN_BATCH) for d in range(N_DIRS) for h in range(DN_HEADS)]

    @pl.when(c == 0)
    def _():
        for j, (u, d, h) in enumerate(states):
            s_ref[j] = s0_ref[u, d, h] if has_init else jnp.zeros((DN_DK, DN_DV), F32)

    dirs = ((qf_ref, kf_ref, vf_ref, gf_ref, of_ref, True), (qb_ref, kb_ref, vb_ref, gb_ref, ob_ref, False))
    chains = []
    for u in range(DN_SCAN_BATCH):
        for d, (q_ref, k_ref, v_ref, g_ref, o_ref, lower) in enumerate(dirs):
            gates = g_ref[u]
            gates_t = gates.T
            for h in range(DN_HEADS):
                j = d * DN_HEADS + h
                jb = per_request + j
                chains.append(_delta_chunk(q_ref[h, u].astype(F32), k_ref[h, u].astype(F32),
                                           v_ref[h, u].astype(F32), gates[:, jb:jb + 1], gates[:, j:j + 1],
                                           gates_t[j:j + 1, :], s_ref[u * per_request + j], lower))
    results = _run_interleaved(chains)
    for j, (u, d, h) in enumerate(states):
        o, s_new = results[j]
        dirs[d][4][u, :, h * DN_DV:(h + 1) * DN_DV] = o.astype(of_ref.dtype)
        s_ref[j] = s_new

    if want_final:
        @pl.when(c == nc - 1)
        def _():
            for j, (u, d, h) in enumerate(states):
                sf_ref[u, d, h] = s_ref[j]


def _dn_scan(qn, kn, vn, gates, s0, nb, length, want_final):
    nc = length // DN_CHUNK
    bb = DN_SCAN_BATCH
    has_init = s0 is not None
    qn, kn, vn = (t.reshape(DN_HEADS, nb, length, DN_DK) for t in (qn, kn, vn))
    gates = gates.reshape(nb, length, LANES)
    fwd = lambda b, c: (b, c, 0)
    bwd = lambda b, c: (b, nc - 1 - c, 0)
    hm = lambda im: pl.BlockSpec((DN_HEADS, bb, DN_CHUNK, DN_DK), lambda b, c: (0,) + im(b, c))
    gm = lambda im: pl.BlockSpec((bb, DN_CHUNK, LANES), im)
    in_specs = [hm(fwd), hm(fwd), hm(fwd), gm(fwd), hm(bwd), hm(bwd), hm(bwd), gm(bwd)]
    args = [qn, kn, vn, gates, qn, kn, vn, gates]
    state_blk = pl.BlockSpec((bb, N_DIRS, DN_HEADS, DN_DK, DN_DV), lambda b, c: (b, 0, 0, 0, 0))
    if has_init:
        in_specs.append(state_blk)
        args.append(s0)
    o_shape = jax.ShapeDtypeStruct((nb, length, DN_WIDTH), BF16)
    out_shape = [o_shape, o_shape]
    out_specs = [pl.BlockSpec((bb, DN_CHUNK, DN_WIDTH), fwd), pl.BlockSpec((bb, DN_CHUNK, DN_WIDTH), bwd)]
    if want_final:
        out_shape.append(jax.ShapeDtypeStruct((nb, N_DIRS, DN_HEADS, DN_DK, DN_DV), F32))
        out_specs.append(state_blk)
    outs = pl.pallas_call(
        functools.partial(_dn_scan_kernel, has_init=has_init, want_final=want_final),
        out_shape=tuple(out_shape),
        grid=(nb // bb, nc),
        in_specs=in_specs,
        out_specs=tuple(out_specs),
        scratch_shapes=[pltpu.VMEM((bb * N_DIRS * DN_HEADS, DN_DK, DN_DV), F32)],
        compiler_params=_params("arbitrary", "arbitrary"),
        name="deltanet_scan",
    )(*args)
    return [outs[0].reshape(nb * length, DN_WIDTH), outs[1].reshape(nb * length, DN_WIDTH)] + list(outs[2:])


DN_SCAN_BATCH = 2


def _deltanet_stream(z1, z_ab, conv_w, a_log, dt_bias, s0, nb, length, row0, want_final):
    row_blk0 = row0 // length
    qn, kn, vn = _dn_prep(z1, conv_w, nb, length, row_blk0)
    gates = _dn_gates(z_ab, a_log, dt_bias, nb, length, row_blk0)
    outs = _dn_scan(qn, kn, vn, gates, s0, nb, length, want_final)
    return outs[0], outs[1], (outs[2] if want_final else None)


def kernel(x_prompt, x_sample, cache_nat_k, cache_nat_v, cache_diff_k, cache_diff_v, state_delta, c, c_ctx, w_in_even, w_out_even, nat_bias, diff_lam, diff_subln, w_in_odd, conv_odd, a_log_odd, dt_bias_odd, onorm_odd, w_out_odd, ada_w, ada_b, ln_g, ln_b, router_w, router_b, moe_wg, moe_wu, moe_wd):
    x_parts = (x_prompt.reshape(N_PROMPT, D_MODEL), x_sample.reshape(N_SAMPLE, D_MODEL))
    cond =jnp.concatenate([c_ctx[None], c, jnp.zeros((N_COND - 1 - DEC_BATCH, D_MODEL), F32)], axis=0)
    mod = _ada_modulation(cond, ada_w, ada_b).reshape(DEPTH, N_COND, 6, D_MODEL)
    router_w_pad = jnp.pad(router_w, ((0, 0), (0, LANES - N_EXPERTS)))

    lam_init = 0.8 - 0.6 * math.exp(-0.3 * 0)
    z = _input_projection(x_parts, mod[0], w_in_even[0].astype(BF16))
    new_nat_k = z[:N_PROMPT, _NAT_K0:_NAT_V0].reshape(BATCH, 1, SEQ, NAT_HEADS, HEAD_DIM)
    new_nat_v = z[:N_PROMPT, _NAT_V0:_DIFF_Q0].reshape(BATCH, 1, SEQ, NAT_HEADS, HEAD_DIM)
    new_diff_k = z[:N_PROMPT, _DIFF_K0:_DIFF_V0].reshape(BATCH, 1, SEQ, DIFF_HEADS, 2, HEAD_DIM)
    new_diff_v = z[:N_PROMPT, _DIFF_V0:].reshape(BATCH, 1, SEQ, DIFF_HEADS, 2 * HEAD_DIM)
    o_ctx = _context_attention(z, diff_lam[0], diff_subln[0], lam_init)
    o_nat = _nat_latent_attention(z, cache_nat_k[:, 0].reshape(DEC_BATCH, PAST_LEN, NAT_WIDTH),
                                  cache_nat_v[:, 0].reshape(DEC_BATCH, PAST_LEN, NAT_WIDTH),
                                  _nat_bias_diagonals(nat_bias[0]))
    o_diff = _diff_latent_attention(z, cache_diff_k[:, 0].reshape(DEC_BATCH, PAST_LEN, DIFF_WIDTH),
                                    cache_diff_v[:, 0].reshape(DEC_BATCH, PAST_LEN, DIFF_WIDTH),
                                    diff_lam[0], diff_subln[0], lam_init)
    x1, *routed = _output_projection((o_ctx, o_nat, o_diff), x_parts, mod[0], w_out_even[0].astype(BF16),
                                     ln_g[0, 0], ln_b[0, 0], router_w_pad, router_b)
    x = _moe_ffn(*routed, 0, moe_wg, moe_wu, moe_wd, x1, mod[0], ln_g[0, 1], ln_b[0, 1])

    w1 = w_in_odd[0]
    w1 = jnp.concatenate([w1[:, :DN_CONV_CH], w1[:, DN_CONV_CH + 4 * DN_HEADS:],
                          w1[:, DN_CONV_CH:DN_CONV_CH + 4 * DN_HEADS],
                          jnp.zeros((D_MODEL, LANES - 4 * DN_HEADS), F32)], axis=1).astype(BF16)
    z1, z_ab = _input_projection((x,), mod[1], w1, f32_tail=LANES)
    of_p, ob_p, new_state = _deltanet_stream(z1, z_ab, conv_odd[0], a_log_odd[0], dt_bias_odd[0],
                                             None, BATCH, SEQ, 0, True)
    of_s, ob_s, _ = _deltanet_stream(z1, z_ab, conv_odd[0], a_log_odd[0], dt_bias_odd[0],
                                     state_delta[:, 0], DEC_BATCH, DEC_SEQ, N_PROMPT, False)
    x1, *routed = _output_projection((of_p, of_s, ob_p, ob_s), (x,), mod[1], w_out_odd[0].astype(BF16),
                                     ln_g[1, 0], ln_b[1, 0], router_w_pad, router_b,
                                     deltanet=(z1, onorm_odd[0]))
    y_prompt, y_sample = _moe_ffn(*routed, 1, moe_wg, moe_wu, moe_wd, x1, mod[1], ln_g[1, 1], ln_b[1, 1],
                                  split=True)
    y_prompt = y_prompt.reshape(BATCH, SEQ, D_MODEL)
    y_sample = y_sample.reshape(DEC_BATCH, DEC_SEQ, D_MODEL)
    return (y_prompt, y_sample, new_nat_k, new_nat_v, new_diff_k, new_diff_v, new_state[:, None])
```

```python
import functools
import math

import jax
import jax.numpy as jnp
import numpy as np
from jax import lax
from jax.experimental import pallas as pl
from jax.experimental.pallas import tpu as pltpu

F32 = jnp.float32
BF16 = jnp.bfloat16

D_MODEL = 1024
BATCH = 16
SEQ = 256
DEC_BATCH = 8
DEC_SEQ = 1024
PAST_LEN = 256
GRID_W = 64
N_ROWS = DEC_SEQ // GRID_W

HEAD_DIM = 64
NAT_HEADS = 8
NAT_WIN_H = 8
NAT_WIN_W = 16
DIFF_HEADS = 4
NAT_WIDTH = NAT_HEADS * HEAD_DIM
DIFF_WIDTH = DIFF_HEADS * 2 * HEAD_DIM
EVEN_IN = 3 * NAT_WIDTH + 3 * DIFF_WIDTH
ROPE_THETA = 10000.0

DN_HEADS = 8
DN_DK = 128
DN_DV = 128
DN_QK_WIDTH = DN_HEADS * DN_DK
DN_WIDTH = DN_HEADS * DN_DV
DN_CONV_CH = 2 * DN_QK_WIDTH + DN_WIDTH
DN_CHUNK = 64

N_EXPERTS = 16
N_GROUPS = 4
EXPERTS_PER_GROUP = N_EXPERTS // N_GROUPS
EXPERT_FF = 512

DEPTH = 2
ALPHA = (2 * DEPTH) ** 0.25
EPS = 1e-5

N_PROMPT = BATCH * SEQ
N_SAMPLE = DEC_BATCH * DEC_SEQ
N_TOK = N_PROMPT + N_SAMPLE
N_COND = 16

VMEM_LIMIT = 56 * 1024 * 1024
LANES = 128
SUBLANES = 8
N_CHUNKS = D_MODEL // LANES
assert N_CHUNKS == SUBLANES


def _params(*sem):
    return pltpu.CompilerParams(dimension_semantics=sem, vmem_limit_bytes=VMEM_LIMIT)


def _cond_row(row0):
    return jnp.where(row0 < N_PROMPT, 0, 1 + (row0 - N_PROMPT) // DEC_SEQ)


def _silu(x):
    return x * jax.nn.sigmoid(x)


def _layer_norm(r, g, b):
    mu = jnp.mean(r, axis=-1, keepdims=True)
    xc = r - mu
    var = jnp.mean(xc * xc, axis=-1, keepdims=True)
    return xc * lax.rsqrt(var + EPS) * g + b


def _dot(a, b):
    return jnp.dot(a, b, preferred_element_type=F32)


def _dot_nt(a, b):
    return lax.dot_general(a, b, (((1,), (1,)), ((), ())), preferred_element_type=F32)


def _dot_tn(a, b):
    return lax.dot_general(a, b, (((0,), (0,)), ((), ())), preferred_element_type=F32)


ADA_TN = 1536


def _ada_kernel(c_ref, w_ref, b_ref, o_ref):
    s = _silu(c_ref[...])
    o_ref[0] = _dot(s.astype(BF16), w_ref[0].astype(BF16)) + b_ref[0]


def _ada_modulation(cond, ada_w, ada_b):
    n = 6 * D_MODEL
    return pl.pallas_call(
        _ada_kernel,
        out_shape=jax.ShapeDtypeStruct((DEPTH, N_COND, n), F32),
        grid=(DEPTH, n // ADA_TN),
        in_specs=[
            pl.BlockSpec((N_COND, D_MODEL), lambda l, j: (0, 0)),
            pl.BlockSpec((1, D_MODEL, ADA_TN), lambda l, j: (l, 0, j)),
            pl.BlockSpec((1, 1, ADA_TN), lambda l, j: (l, 0, j)),
        ],
        out_specs=pl.BlockSpec((1, N_COND, ADA_TN), lambda l, j: (l, 0, j)),
        compiler_params=_params("arbitrary", "arbitrary"),
        name="ada_modulation",
    )(cond, ada_w, ada_b.reshape(DEPTH, 1, n))


PROJ_TM = 512


def _stream_specs(widths, tm):
    if len(widths) == 1:
        return [pl.BlockSpec((tm, widths[0]), lambda i: (i, 0))]
    n_ctx = N_PROMPT // tm
    return ([pl.BlockSpec((tm, widths[0]), lambda i: (jnp.minimum(i, n_ctx - 1), 0))]
            + [pl.BlockSpec((tm, w), lambda i: (jnp.maximum(i - n_ctx, 0), 0)) for w in widths[1:]])


def _stream_tile(refs, tm):
    if len(refs) == 1:
        return refs[0][...]
    latent = [r[...] for r in refs[1:]]
    latent = latent[0] if len(latent) == 1 else jnp.concatenate(latent, axis=1)
    return jnp.where(pl.program_id(0) * tm < N_PROMPT, refs[0][...], latent)


def _inproj_kernel(*refs, n_x):
    x_refs, (mod_ref, w_ref, o_ref), tail_refs = refs[:n_x], refs[n_x:n_x + 3], refs[n_x + 3:]
    m = mod_ref[0]
    h = _stream_tile(x_refs, o_ref.shape[0]) * (1.0 + m[1:2]) + m[0:1]
    z = _dot(h.astype(BF16), w_ref[...])
    n_main = o_ref.shape[1]
    o_ref[...] = z[:, :n_main].astype(o_ref.dtype)
    if tail_refs:
        tail_refs[0][...] = z[:, n_main:]


def _input_projection(x_parts, mod, w_bf16, tm=PROJ_TM, f32_tail=0):
    n = w_bf16.shape[1] - f32_tail
    out_shape = [jax.ShapeDtypeStruct((N_TOK, n), BF16 if f32_tail else F32)]
    out_specs = [pl.BlockSpec((tm, n), lambda i: (i, 0))]
    if f32_tail:
        out_shape.append(jax.ShapeDtypeStruct((N_TOK, f32_tail), F32))
        out_specs.append(pl.BlockSpec((tm, f32_tail), lambda i: (i, 0)))
    outs = pl.pallas_call(
        functools.partial(_inproj_kernel, n_x=len(x_parts)),
        out_shape=tuple(out_shape),
        grid=(N_TOK // tm,),
        in_specs=_stream_specs([a.shape[1] for a in x_parts], tm) + [
            pl.BlockSpec((1, 6, D_MODEL), lambda i: (_cond_row(i * tm), 0, 0)),
            pl.BlockSpec((D_MODEL, n + f32_tail), lambda i: (0, 0)),
        ],
        out_specs=tuple(out_specs),
        compiler_params=_params("arbitrary"),
        name="input_projection",
    )(*x_parts, mod, w_bf16)
    return outs if f32_tail else outs[0]


_NAT_Q0, _NAT_K0, _NAT_V0 = 0, NAT_WIDTH, 2 * NAT_WIDTH
_DIFF_Q0 = 3 * NAT_WIDTH
_DIFF_K0 = _DIFF_Q0 + DIFF_WIDTH
_DIFF_V0 = _DIFF_K0 + DIFF_WIDTH
ATTN_SCALE = HEAD_DIM ** -0.5


def _diff_lambda(lam_ref, lam_init):
    lp = lam_ref[...]
    return (jnp.exp(jnp.sum(lp[0:1] * lp[1:2], axis=-1, keepdims=True))
            - jnp.exp(jnp.sum(lp[2:3] * lp[3:4], axis=-1, keepdims=True)) + lam_init)


def _scaled_q(q):
    assert math.log2(HEAD_DIM) % 2 == 0
    return (q * ATTN_SCALE).astype(BF16)


def _with_ones(v):
    width = v.shape[1] if v.shape[1] % LANES == 0 else LANES - v.shape[1] % LANES
    return jnp.concatenate([v, jnp.ones((v.shape[0], width), v.dtype)], axis=1)


def _sub_norm(o, g, lam_init):
    ms = jnp.mean(o * o, axis=-1, keepdims=True)
    return o * lax.rsqrt(ms + EPS) * g * (1.0 - lam_init)


def _ctx_attn_kernel(z_ref, lam_ref, subg_ref, o_ref, *, lam_init):
    lam = _diff_lambda(lam_ref, lam_init)

    def softmax_av(q0, k0, v0, v_width):
        q = _scaled_q(z_ref[:, q0:q0 + HEAD_DIM])
        k = z_ref[:, k0:k0 + HEAD_DIM].astype(BF16)
        s = _dot_nt(q, k)
        yield
        e = jnp.exp(s - jnp.max(s, axis=-1, keepdims=True))
        yield
        av = _dot(e.astype(BF16), _with_ones(z_ref[:, v0:v0 + v_width].astype(BF16)))
        yield
        return av[:, :v_width] / av[:, v_width:v_width + 1]

    chains = [softmax_av(_NAT_Q0 + h * HEAD_DIM, _NAT_K0 + h * HEAD_DIM, _NAT_V0 + h * HEAD_DIM, HEAD_DIM)
              for h in range(NAT_HEADS)]
    chains += [softmax_av(_DIFF_Q0 + i * HEAD_DIM, _DIFF_K0 + i * HEAD_DIM,
                          _DIFF_V0 + (i // 2) * 2 * HEAD_DIM, 2 * HEAD_DIM)
               for i in range(2 * DIFF_HEADS)]
    outs = _run_interleaved(chains)
    for h in range(NAT_HEADS):
        o_ref[:, h * HEAD_DIM:(h + 1) * HEAD_DIM] = outs[h].astype(o_ref.dtype)
    for h in range(DIFF_HEADS):
        c = NAT_WIDTH + h * 2 * HEAD_DIM
        o = outs[NAT_HEADS + 2 * h] - lam * outs[NAT_HEADS + 2 * h + 1]
        o_ref[:, c:c + 2 * HEAD_DIM] = _sub_norm(o, subg_ref[...], lam_init).astype(o_ref.dtype)


def _context_attention(z, lam_p, subln_g, lam_init):
    return pl.pallas_call(
        functools.partial(_ctx_attn_kernel, lam_init=lam_init),
        out_shape=jax.ShapeDtypeStruct((N_PROMPT, D_MODEL), BF16),
        grid=(BATCH,),
        in_specs=[
            pl.BlockSpec((SEQ, EVEN_IN), lambda b: (b, 0)),
            pl.BlockSpec((4, HEAD_DIM), lambda b: (0, 0)),
            pl.BlockSpec((1, 2 * HEAD_DIM), lambda b: (0, 0)),
        ],
        out_specs=pl.BlockSpec((SEQ, D_MODEL), lambda b: (b, 0)),
        compiler_params=_params("arbitrary"),
        name="context_attention",
    )(z, lam_p, subln_g.reshape(1, 2 * HEAD_DIM))


NAT_TQ = 512
NAT_KEYS = 12 * GRID_W
_SAMPLE_BLK0 = N_PROMPT // DEC_SEQ


NAT_SPAN = NAT_KEYS // GRID_W
NAT_SPAN_PAD = 16


def _nat_bias_diagonals(rel_bias):
    rows = N_ROWS
    kh = min(NAT_WIN_H, rows)
    row_onehot = np.zeros((2 * NAT_WIN_H - 1, rows, NAT_SPAN), np.float32)
    for r in range(rows):
        start = min(max(r - kh // 2, 0), rows - kh)
        a0 = _nat_key0(r * GRID_W // NAT_TQ) // GRID_W
        assert a0 <= start and start + kh <= a0 + NAT_SPAN
        for a in range(start, start + kh):
            row_onehot[a - r + NAT_WIN_H - 1, r, a - a0] = 1.0
    lane_onehot = np.zeros((2 * NAT_WIN_W - 1, LANES), np.float32)
    for lane in range(2 * GRID_W - 1):
        lane_onehot[np.clip(lane - (GRID_W - 1), 1 - NAT_WIN_W, NAT_WIN_W - 1) + NAT_WIN_W - 1, lane] = 1.0
    hi = lax.Precision.HIGHEST
    by_row = jnp.einsum('hed,era->hrad', rel_bias.astype(F32), row_onehot, precision=hi)
    diag = jnp.einsum('hrad,dl->hral', by_row, lane_onehot, precision=hi)
    diag = jnp.where((row_onehot.sum(0) > 0)[None, :, :, None], diag, -jnp.inf)
    return jnp.pad(diag, ((0, 0), (0, 0), (0, NAT_SPAN_PAD - NAT_SPAN), (0, 0)))


def _nat_bias_tiles(diag_ref, bias_ref):
    shape = (GRID_W, LANES)
    q = lax.broadcasted_iota(jnp.int32, shape, 0)
    lane = lax.broadcasted_iota(jnp.int32, shape, 1)
    k = lane % GRID_W
    c0 = jnp.clip(q - NAT_WIN_W // 2, 0, GRID_W - NAT_WIN_W)
    in_cols = (k >= c0) & (k < c0 + NAT_WIN_W)
    for j in range(2):
        for r in range(NAT_TQ // GRID_W):
            for pair in range(NAT_SPAN // 2):
                even = jnp.broadcast_to(diag_ref[j, r, 2 * pair:2 * pair + 1, :], shape)
                odd = jnp.broadcast_to(diag_ref[j, r, 2 * pair + 1:2 * pair + 2, :], shape)
                even = pltpu.roll(even, GRID_W + 1, 1, stride=1, stride_axis=0)
                odd = pltpu.roll(odd, 1, 1, stride=1, stride_axis=0)
                tile = jnp.where(lane < GRID_W, even, odd)
                bias_ref[j, r * GRID_W:(r + 1) * GRID_W, pair * LANES:(pair + 1) * LANES] = (
                    jnp.where(in_cols, tile, -jnp.inf))


def _nat_key0(qt):
    half = DEC_SEQ // NAT_TQ // 2
    return (qt >= half) * (DEC_SEQ - NAT_KEYS)


def _nat_latent_kernel(q_ref, k_ref, v_ref, ck_ref, cv_ref, diag_ref, o_ref, bias_ref):
    key0 = pl.multiple_of(_nat_key0(pl.program_id(1)).astype(jnp.int32), DEC_SEQ - NAT_KEYS)
    keys = pl.ds(key0, NAT_KEYS)

    @pl.when(pl.program_id(2) == 0)
    def _():
        _nat_bias_tiles(diag_ref, bias_ref)

    def head(j):
        sl = slice(j * HEAD_DIM, (j + 1) * HEAD_DIM)
        q = _scaled_q(q_ref[:, sl])
        s_loc = _dot_nt(q, k_ref[keys, sl].astype(BF16)) + bias_ref[j]
        s_ctx = _dot_nt(q, ck_ref[0, :, sl].astype(BF16))
        yield
        m = jnp.maximum(jnp.max(s_loc, axis=-1, keepdims=True), jnp.max(s_ctx, axis=-1, keepdims=True))
        yield
        e_loc = jnp.exp(s_loc - m)
        e_ctx = jnp.exp(s_ctx - m)
        yield
        o = (_dot(e_loc.astype(BF16), _with_ones(v_ref[keys, sl].astype(BF16)))
             + _dot(e_ctx.astype(BF16), _with_ones(cv_ref[0, :, sl].astype(BF16))))
        yield
        return (o[:, :HEAD_DIM] / o[:, HEAD_DIM:HEAD_DIM + 1]).astype(o_ref.dtype)

    outs = _run_interleaved([head(j) for j in range(2)])
    for j in range(2):
        o_ref[:, j * HEAD_DIM:(j + 1) * HEAD_DIM] = outs[j]


def _nat_latent_attention(z, ck, cv, diag):
    nq = DEC_SEQ // NAT_TQ
    qblk0 = N_PROMPT // NAT_TQ
    pair = 2 * HEAD_DIM
    return pl.pallas_call(
        _nat_latent_kernel,
        out_shape=jax.ShapeDtypeStruct((N_SAMPLE, NAT_WIDTH), BF16),
        grid=(NAT_HEADS // 2, nq, DEC_BATCH),
        in_specs=[
            pl.BlockSpec((NAT_TQ, pair), lambda hp, qt, b: (qblk0 + b * nq + qt, _NAT_Q0 // pair + hp)),
            pl.BlockSpec((DEC_SEQ, pair), lambda hp, qt, b: (_SAMPLE_BLK0 + b, _NAT_K0 // pair + hp)),
            pl.BlockSpec((DEC_SEQ, pair), lambda hp, qt, b: (_SAMPLE_BLK0 + b, _NAT_V0 // pair + hp)),
            pl.BlockSpec((1, PAST_LEN, pair), lambda hp, qt, b: (b, 0, hp)),
            pl.BlockSpec((1, PAST_LEN, pair), lambda hp, qt, b: (b, 0, hp)),
            pl.BlockSpec((2, NAT_TQ // GRID_W, NAT_SPAN_PAD, LANES), lambda hp, qt, b: (hp, qt, 0, 0)),
        ],
        out_specs=pl.BlockSpec((NAT_TQ, pair), lambda hp, qt, b: (b * nq + qt, hp)),
        scratch_shapes=[pltpu.VMEM((2, NAT_TQ, NAT_KEYS), F32)],
        compiler_params=_params("arbitrary", "arbitrary", "arbitrary"),
        name="nat_latent_attention",
    )(z, z, z, ck, cv, diag)


DIFF_TQ = 1024


def _rope_tables():
    nf = HEAD_DIM // 4
    t = jnp.arange(DEC_SEQ)
    inv = ROPE_THETA ** (-jnp.arange(nf, dtype=F32) / nf)
    ang_r = (t // GRID_W).astype(F32)[:, None] * inv
    ang_c = (t % GRID_W).astype(F32)[:, None] * inv
    cos = jnp.concatenate([jnp.cos(ang_r)] * 2 + [jnp.cos(ang_c)] * 2, axis=-1)
    sin = jnp.concatenate([-jnp.sin(ang_r), jnp.sin(ang_r), -jnp.sin(ang_c), jnp.sin(ang_c)], axis=-1)
    return jnp.tile(cos, (1, 2)), jnp.tile(sin, (1, 2))


def _rope(x, cos, sin):
    nf = HEAD_DIM // 4
    lane = lax.broadcasted_iota(jnp.int32, x.shape, 1)
    upper = (lane // nf) % 2 == 1
    partner = jnp.where(upper, pltpu.roll(x, nf, axis=1), pltpu.roll(x, x.shape[1] - nf, axis=1))
    return x * cos + partner * sin


def _diff_latent_kernel(q_ref, k_ref, v_ref, ck_ref, cv_ref, cos_ref, sin_ref, lam_ref, subg_ref,
                        o_ref, kr_ref, *, lam_init):
    qt = pl.program_id(2)

    @pl.when(qt == 0)
    def _():
        kr_ref[...] = _rope(k_ref[...], cos_ref[...], sin_ref[...]).astype(BF16)

    lam = _diff_lambda(lam_ref, lam_init)
    row0 = pl.multiple_of(qt * DIFF_TQ, DIFF_TQ)
    q = _scaled_q(_rope(q_ref[...], cos_ref[pl.ds(row0, DIFF_TQ), :], sin_ref[pl.ds(row0, DIFF_TQ), :]))
    v_loc = _with_ones(v_ref[...].astype(BF16))
    v_ctx = _with_ones(cv_ref[0].astype(BF16))
    width = 2 * HEAD_DIM

    def softmax_av(j):
        sl = slice(j * HEAD_DIM, (j + 1) * HEAD_DIM)
        s_ctx = _dot_nt(q[:, sl], ck_ref[0, :, sl].astype(BF16))
        s_loc = _dot_nt(q[:, sl], kr_ref[:, sl])
        yield
        m = jnp.maximum(jnp.max(s_loc, axis=-1, keepdims=True), jnp.max(s_ctx, axis=-1, keepdims=True))
        yield
        e_loc = jnp.exp(s_loc - m)
        e_ctx = jnp.exp(s_ctx - m)
        yield
        av = _dot(e_loc.astype(BF16), v_loc) + _dot(e_ctx.astype(BF16), v_ctx)
        yield
        return av[:, :width] / av[:, width:width + 1]

    parts = _run_interleaved([softmax_av(j) for j in range(2)])
    o = parts[0] - lam * parts[1]
    o_ref[...] = _sub_norm(o, subg_ref[...], lam_init).astype(o_ref.dtype)


def _diff_latent_attention(z, ck, cv, lam_p, subln_g, lam_init):
    nq = DEC_SEQ // DIFF_TQ
    qblk0 = N_PROMPT // DIFF_TQ
    w = 2 * HEAD_DIM
    cos, sin = _rope_tables()
    return pl.pallas_call(
        functools.partial(_diff_latent_kernel, lam_init=lam_init),
        out_shape=jax.ShapeDtypeStruct((N_SAMPLE, DIFF_WIDTH), BF16),
        grid=(DEC_BATCH, DIFF_HEADS, nq),
        in_specs=[
            pl.BlockSpec((DIFF_TQ, w), lambda b, h, qt: (qblk0 + b * nq + qt, _DIFF_Q0 // w + h)),
            pl.BlockSpec((DEC_SEQ, w), lambda b, h, qt: (_SAMPLE_BLK0 + b, _DIFF_K0 // w + h)),
            pl.BlockSpec((DEC_SEQ, w), lambda b, h, qt: (_SAMPLE_BLK0 + b, _DIFF_V0 // w + h)),
            pl.BlockSpec((1, PAST_LEN, w), lambda b, h, qt: (b, 0, h)),
            pl.BlockSpec((1, PAST_LEN, w), lambda b, h, qt: (b, 0, h)),
            pl.BlockSpec((DEC_SEQ, w), lambda b, h, qt: (0, 0)),
            pl.BlockSpec((DEC_SEQ, w), lambda b, h, qt: (0, 0)),
            pl.BlockSpec((4, HEAD_DIM), lambda b, h, qt: (0, 0)),
            pl.BlockSpec((1, w), lambda b, h, qt: (0, 0)),
        ],
        out_specs=pl.BlockSpec((DIFF_TQ, w), lambda b, h, qt: (b * nq + qt, h)),
        scratch_shapes=[pltpu.VMEM((DEC_SEQ, w), BF16)],
        compiler_params=_params("arbitrary", "arbitrary", "arbitrary"),
        name="diff_latent_attention",
    )(z, z, z, ck, cv, cos, sin, lam_p, subln_g.reshape(1, w))


OUT_TM = 256


def _route(logits_t, rb_col):
    sc = jax.nn.sigmoid(logits_t)
    bi = sc + rb_col
    srow = [sc[e:e + 1] for e in range(N_EXPERTS)]
    brow = [bi[e:e + 1] for e in range(N_EXPERTS)]
    n = EXPERTS_PER_GROUP
    gscore = []
    for g in range(N_GROUPS):
        v = brow[g * n:(g + 1) * n]
        best = None
        for i in range(n):
            for j in range(i + 1, n):
                s = v[i] + v[j]
                best = s if best is None else jnp.maximum(best, s)
        gscore.append(best)
    sel = jnp.zeros_like(gscore[0], dtype=jnp.int32)
    best = gscore[0]
    for g in range(1, N_GROUPS):
        better = gscore[g] > best
        sel = jnp.where(better, g, sel)
        best = jnp.where(better, gscore[g], best)

    def pick_group(rows, i):
        out = rows[(N_GROUPS - 1) * n + i]
        for g in range(N_GROUPS - 2, -1, -1):
            out = jnp.where(sel == g, rows[g * n + i], out)
        return out

    bv = [pick_group(brow, i) for i in range(n)]
    sv = [pick_group(srow, i) for i in range(n)]
    i1 = jnp.zeros_like(sel)
    m1, w1 = bv[0], sv[0]
    for i in range(1, n):
        better = bv[i] > m1
        i1 = jnp.where(better, i, i1)
        m1 = jnp.where(better, bv[i], m1)
        w1 = jnp.where(better, sv[i], w1)
    i2 = jnp.zeros_like(sel)
    m2 = jnp.full_like(m1, -jnp.inf)
    w2 = jnp.zeros_like(w1)
    for i in range(n):
        better = (i1 != i) & (bv[i] > m2)
        i2 = jnp.where(better, i, i2)
        m2 = jnp.where(better, bv[i], m2)
        w2 = jnp.where(better, sv[i], w2)
    tot = w1 + w2
    return sel * n + i1, sel * n + i2, w1 / tot, w2 / tot


def _store_token_tiles(ref, value):
    rows = value.shape[0]
    chunks = jnp.stack([value[:, c * LANES:(c + 1) * LANES] for c in range(N_CHUNKS)], axis=0)
    tiles = jnp.transpose(chunks.reshape(N_CHUNKS, rows // SUBLANES, SUBLANES, LANES), (1, 2, 0, 3))
    ref[...] = tiles.reshape(rows, N_CHUNKS, LANES)


def _load_token_tiles(ref):
    rows = ref.shape[0]
    tiles = ref[...].reshape(rows // SUBLANES, SUBLANES, N_CHUNKS, LANES)
    chunks = jnp.transpose(tiles, (2, 0, 1, 3)).reshape(N_CHUNKS, rows, LANES)
    return jnp.concatenate([chunks[c] for c in range(N_CHUNKS)], axis=1)


def _deltanet_mixer_out(o_refs, tm):
    of = _stream_tile(o_refs[0:2], tm).astype(F32)
    ob = _stream_tile(o_refs[2:4], tm).astype(F32)
    gate_ref, g_ref = o_refs[4], o_refs[5]
    pieces = []
    for h in range(DN_HEADS):
        sl = slice(h * DN_DV, (h + 1) * DN_DV)
        o = of[:, sl] + ob[:, sl]
        o = o * lax.rsqrt(jnp.mean(o * o, axis=-1, keepdims=True) + EPS) * g_ref[...]
        pieces.append((o * _silu(gate_ref[:, sl].astype(F32))).astype(BF16))
    return jnp.concatenate(pieces, axis=1)


def _outproj_kernel(*refs, n_o, n_x, deltanet):
    o_refs, x_refs, refs = refs[:n_o], refs[n_o:n_o + n_x], refs[n_o + n_x:]
    mod_ref, w_ref, lng_ref, lnb_ref, rw_ref, rb_ref, x1_ref, h2_ref, ri_ref, rw_out_ref, cnt_ref = refs
    tm = x1_ref.shape[0]
    m = mod_ref[0]
    y = _dot(_deltanet_mixer_out(o_refs, tm) if deltanet else _stream_tile(o_refs, tm), w_ref[...])
    x1 = _layer_norm(ALPHA * _stream_tile(x_refs, tm) + m[2:3] * y, lng_ref[...], lnb_ref[...])
    x1_ref[...] = x1
    h2 = x1 * (1.0 + m[4:5]) + m[3:4]
    _store_token_tiles(h2_ref, h2)
    rw = rw_ref[...]
    h_hi, rw_hi = h2.astype(BF16), rw.astype(BF16)
    h_lo = (h2 - h_hi.astype(F32)).astype(BF16)
    rw_lo = (rw - rw_hi.astype(F32)).astype(BF16)
    logits = _dot(h_hi, rw_hi) + (_dot(h_lo, rw_hi) + _dot(h_hi, rw_lo))
    e1, e2, w1, w2 = _route(logits.T[:N_EXPERTS], rb_ref[...])
    eid = lax.broadcasted_iota(jnp.int32, (N_EXPERTS, tm), 0)
    onehot = (eid == e1) | (eid == e2)
    ti = lax.broadcasted_iota(jnp.int32, (tm, tm), 0)
    tj = lax.broadcasted_iota(jnp.int32, (tm, tm), 1)
    earlier = jnp.where(ti < tj, 1.0, 0.0).astype(BF16)
    rank = _dot(jnp.where(onehot, 1.0, 0.0).astype(BF16), earlier)
    r1 = jnp.sum(jnp.where(eid == e1, rank, 0.0), axis=0, keepdims=True).astype(jnp.int32)
    r2 = jnp.sum(jnp.where(eid == e2, rank, 0.0), axis=0, keepdims=True).astype(jnp.int32)
    ri_ref[...] = jnp.concatenate([e1, e2, r1, r2, jnp.zeros((4, tm), jnp.int32)], axis=0)
    wt = jnp.concatenate([w1, w2, jnp.zeros((LANES - 2, tm), F32)], axis=0)
    rw_out_ref[...] = wt.T
    cnt = jnp.sum(jnp.where(onehot, 1.0, 0.0), axis=1, keepdims=True)
    cnt_ref[0] = jnp.broadcast_to(cnt, (N_EXPERTS, LANES)).astype(jnp.int32)


def _output_projection(o_parts, x_parts, mod, w_bf16, ln_g, ln_b, router_w_pad, router_b, tm=OUT_TM,
                       deltanet=None):
    nt = N_TOK // tm
    if deltanet is None:
        o_specs = _stream_specs([a.shape[1] for a in o_parts], tm)
    else:
        z1, onorm_g = deltanet
        o_specs = (_stream_specs([DN_WIDTH, DN_WIDTH], tm) + _stream_specs([DN_WIDTH, DN_WIDTH], tm)
                   + [pl.BlockSpec((tm, DN_WIDTH), lambda i: (i, DN_GATE_COL0 // DN_WIDTH)),
                      pl.BlockSpec((1, DN_DV), lambda i: (0, 0))])
        o_parts = tuple(o_parts) + (z1, onorm_g.reshape(1, DN_DV))
    return pl.pallas_call(
        functools.partial(_outproj_kernel, n_o=len(o_parts), n_x=len(x_parts), deltanet=deltanet is not None),
        out_shape=(jax.ShapeDtypeStruct((N_TOK, D_MODEL), F32),
                   jax.ShapeDtypeStruct((N_TOK, N_CHUNKS, LANES), F32),
                   jax.ShapeDtypeStruct((8, N_TOK), jnp.int32),
                   jax.ShapeDtypeStruct((N_TOK, LANES), F32),
                   jax.ShapeDtypeStruct((nt, N_EXPERTS, LANES), jnp.int32)),
        grid=(nt,),
        in_specs=o_specs + _stream_specs([a.shape[1] for a in x_parts], tm) + [
            pl.BlockSpec((1, 6, D_MODEL), lambda i: (_cond_row(i * tm), 0, 0)),
            pl.BlockSpec((D_MODEL, D_MODEL), lambda i: (0, 0)),
            pl.BlockSpec((1, D_MODEL), lambda i: (0, 0)),
            pl.BlockSpec((1, D_MODEL), lambda i: (0, 0)),
            pl.BlockSpec((D_MODEL, LANES), lambda i: (0, 0)),
            pl.BlockSpec((N_EXPERTS, 1), lambda i: (0, 0)),
        ],
        out_specs=(pl.BlockSpec((tm, D_MODEL), lambda i: (i, 0)),
                   pl.BlockSpec((tm, N_CHUNKS, LANES), lambda i: (i, 0, 0)),
                   pl.BlockSpec((8, tm), lambda i: (0, i)),
                   pl.BlockSpec((tm, LANES), lambda i: (i, 0)),
                   pl.BlockSpec((1, N_EXPERTS, LANES), lambda i: (i, 0, 0))),
        compiler_params=_params("arbitrary"),
        name="output_projection",
    )(*o_parts, *x_parts, mod, w_bf16, ln_g.reshape(1, D_MODEL), ln_b.reshape(1, D_MODEL),
      router_w_pad, router_b.reshape(N_EXPERTS, 1))


N_ASSIGN = 2 * N_TOK
MOE_TM = 256
MOE_ROWS = N_ASSIGN + N_EXPERTS * MOE_TM
MOE_TILES = MOE_ROWS // MOE_TM
DISPATCH_TM = 1024
COMBINE_TM = 256


def _moe_plan(route_i, counts):
    cnt = counts[:, :, 0]
    total = jnp.sum(cnt, axis=0)
    padded = (total + MOE_TM - 1) // MOE_TM * MOE_TM
    seg_end = jnp.cumsum(padded)
    seg_start = seg_end - padded
    tile_base = seg_start[None, :] + jnp.cumsum(cnt, axis=0) - cnt
    base_tok = jnp.repeat(tile_base, OUT_TM, axis=0)
    eids = jnp.arange(N_EXPERTS, dtype=jnp.int32)[None, :]
    pos = [jnp.sum(jnp.where(route_i[k][:, None] == eids, base_tok, 0), axis=1) + route_i[2 + k]
           for k in range(2)]
    pos = jnp.concatenate(pos).astype(jnp.int32)
    tile_row0 = jnp.arange(MOE_TILES, dtype=jnp.int32) * MOE_TM
    tile_expert = jnp.minimum(jnp.sum(seg_end[None, :] <= tile_row0[:, None], axis=1), N_EXPERTS - 1)
    n_tiles = (seg_end[-1] // MOE_TM).reshape(1)
    last = MOE_ROWS - MOE_TM
    fill_rows = jnp.concatenate([jnp.minimum(seg_start + total, last),
                                 last - jnp.arange(N_EXPERTS, dtype=jnp.int32) * MOE_TM])
    return pos, tile_expert.astype(jnp.int32), n_tiles.astype(jnp.int32), fill_rows.astype(jnp.int32)


def _row_copy_wait(src_hbm, dst, sem, n_rows):
    pltpu.make_async_copy(src_hbm.at[pl.ds(0, n_rows)], dst.at[pl.ds(0, n_rows)], sem).wait()


def _dispatch_kernel(pos_ref, fill_ref, h_ref, xs_ref, zero_ref, sem, fill_sem):
    tm = h_ref.shape[0]
    t0 = pl.program_id(0) * tm

    @pl.when(pl.program_id(0) == 0)
    def _():
        zero_ref[...] = jnp.zeros_like(zero_ref)
        fills = [pltpu.make_async_copy(zero_ref, xs_ref.at[pl.ds(fill_ref[j], MOE_TM)], fill_sem)
                 for j in range(2 * N_EXPERTS)]
        for cp in fills[N_EXPERTS:]:
            cp.start()
        for cp in fills[N_EXPERTS:]:
            cp.wait()
        for cp in fills[:N_EXPERTS]:
            cp.start()
            cp.wait()

    def body(r, carry):
        for k in range(2):
            pltpu.make_async_copy(h_ref.at[r], xs_ref.at[pos_ref[k * N_TOK + t0 + r]], sem).start()
        return carry

    lax.fori_loop(0, tm, body, 0, unroll=8)
    for _ in range(2):
        _row_copy_wait(h_ref, xs_ref, sem, tm)


def _moe_dispatch(pos, fill_rows, h2_tiles):
    tm = DISPATCH_TM
    return pl.pallas_call(
        _dispatch_kernel,
        out_shape=jax.ShapeDtypeStruct((MOE_ROWS, N_CHUNKS, LANES), F32),
        grid_spec=pltpu.PrefetchScalarGridSpec(
            num_scalar_prefetch=2,
            grid=(N_TOK // tm,),
            in_specs=[pl.BlockSpec((tm, N_CHUNKS, LANES), lambda i, pos, fill: (i, 0, 0))],
            out_specs=pl.BlockSpec(memory_space=pl.ANY),
            scratch_shapes=[pltpu.VMEM((MOE_TM, N_CHUNKS, LANES), F32), pltpu.SemaphoreType.DMA,
                            pltpu.SemaphoreType.DMA],
        ),
        compiler_params=_params("arbitrary"),
        name="moe_dispatch",
    )(pos, fill_rows, h2_tiles)


def _expert_kernel(te_ref, nt_ref, xs_ref, wg_ref, wu_ref, wd_ref, ys_ref, wg16_ref, wu16_ref, wd16_ref):
    i = pl.program_id(0)

    @pl.when(i < nt_ref[0])
    def _():
        @pl.when((i == 0) | (te_ref[i] != te_ref[jnp.maximum(i - 1, 0)]))
        def _():
            wg16_ref[...] = wg_ref[0, 0].astype(BF16)
            wu16_ref[...] = wu_ref[0, 0].astype(BF16)
            wd16_ref[...] = wd_ref[0, 0].astype(BF16)

        x = _load_token_tiles(xs_ref).astype(BF16)
        he = _silu(_dot(x, wg16_ref[...])) * _dot(x, wu16_ref[...])
        _store_token_tiles(ys_ref, _dot(he.astype(BF16), wd16_ref[...]))

    @pl.when(i >= nt_ref[0])
    def _():
        ys_ref[...] = jnp.zeros_like(ys_ref)


def _moe_experts(tile_expert, n_tiles, xs, layer, wg, wu, wd):
    row_blk = lambda i, te, nt: (jnp.minimum(i, nt[0] - 1), 0, 0)
    w_blk = lambda i, te, nt: (layer, te[i], 0, 0)
    return pl.pallas_call(
        _expert_kernel,
        out_shape=jax.ShapeDtypeStruct((MOE_ROWS, N_CHUNKS, LANES), F32),
        grid_spec=pltpu.PrefetchScalarGridSpec(
            num_scalar_prefetch=2,
            grid=(MOE_TILES,),
            in_specs=[
                pl.BlockSpec((MOE_TM, N_CHUNKS, LANES), row_blk),
                pl.BlockSpec((1, 1, D_MODEL, EXPERT_FF), w_blk),
                pl.BlockSpec((1, 1, D_MODEL, EXPERT_FF), w_blk),
                pl.BlockSpec((1, 1, EXPERT_FF, D_MODEL), w_blk),
            ],
            out_specs=pl.BlockSpec((MOE_TM, N_CHUNKS, LANES), lambda i, te, nt: (i, 0, 0)),
            scratch_shapes=[pltpu.VMEM((D_MODEL, EXPERT_FF), BF16), pltpu.VMEM((D_MODEL, EXPERT_FF), BF16),
                            pltpu.VMEM((EXPERT_FF, D_MODEL), BF16)],
        ),
        compiler_params=_params("arbitrary"),
        name="moe_experts",
    )(tile_expert, n_tiles, xs, wg, wu, wd)


def _combine_kernel(pos_ref, gw_ref, x1_ref, mod_ref, lng_ref, lnb_ref, ys_ref, *rest, split):
    out_refs, (buf_ref, sem) = rest[:-2], rest[-2:]
    i = pl.program_id(0)
    n = pl.num_programs(0)
    tm = COMBINE_TM

    def start_row(tile, to_slot, r):
        for k in range(2):
            p = pos_ref[k * N_TOK + tile * tm + r]
            pltpu.make_async_copy(ys_ref.at[p], buf_ref.at[to_slot, k, r], sem.at[to_slot]).start()

    def wait_slot(s):
        for k in range(2):
            _row_copy_wait(ys_ref, buf_ref.at[s, k], sem.at[s], tm)

    def issue(tile, to_slot):
        def body(r, carry):
            start_row(tile, to_slot, r)
            return carry
        lax.fori_loop(0, tm, body, 0, unroll=8)

    @pl.when(i == 0)
    def _():
        issue(0, 0)

    @pl.when(i + 1 < n)
    def _():
        issue(i + 1, (i + 1) % 2)

    slot = i % 2
    wait_slot(slot)
    gw = gw_ref[...]
    f = gw[:, 0:1] * _load_token_tiles(buf_ref.at[slot, 0]) + gw[:, 1:2] * _load_token_tiles(buf_ref.at[slot, 1])
    m = mod_ref[0]
    out = _layer_norm(ALPHA * x1_ref[...] + m[5:6] * f, lng_ref[...], lnb_ref[...])

    if split:
        prompt_ref, sample_ref = out_refs

        @pl.when(i < N_PROMPT // tm)
        def _():
            prompt_ref[...] = out

        @pl.when(i >= N_PROMPT // tm)
        def _():
            sample_ref[...] = out
    else:
        out_refs[0][...] = out


def _moe_combine(pos, gate_w, x1, mod, ln_g, ln_b, ys, split):
    tm = COMBINE_TM
    np_blk = N_PROMPT // tm
    if split:
        out_shape = (jax.ShapeDtypeStruct((N_PROMPT, D_MODEL), F32), jax.ShapeDtypeStruct((N_SAMPLE, D_MODEL), F32))
        out_specs = (pl.BlockSpec((tm, D_MODEL), lambda i, pos: (jnp.minimum(i, np_blk - 1), 0)),
                     pl.BlockSpec((tm, D_MODEL), lambda i, pos: (jnp.maximum(i - np_blk, 0), 0)))
    else:
        out_shape = jax.ShapeDtypeStruct((N_TOK, D_MODEL), F32)
        out_specs = pl.BlockSpec((tm, D_MODEL), lambda i, pos: (i, 0))
    return pl.pallas_call(
        functools.partial(_combine_kernel, split=split),
        out_shape=out_shape,
        grid_spec=pltpu.PrefetchScalarGridSpec(
            num_scalar_prefetch=1,
            grid=(N_TOK // tm,),
            in_specs=[
                pl.BlockSpec((tm, LANES), lambda i, pos: (i, 0)),
                pl.BlockSpec((tm, D_MODEL), lambda i, pos: (i, 0)),
                pl.BlockSpec((1, 6, D_MODEL), lambda i, pos: (_cond_row(i * tm), 0, 0)),
                pl.BlockSpec((1, D_MODEL), lambda i, pos: (0, 0)),
                pl.BlockSpec((1, D_MODEL), lambda i, pos: (0, 0)),
                pl.BlockSpec(memory_space=pl.ANY),
            ],
            out_specs=out_specs,
            scratch_shapes=[pltpu.VMEM((2, 2, tm, N_CHUNKS, LANES), F32), pltpu.SemaphoreType.DMA((2,))],
        ),
        compiler_params=_params("arbitrary"),
        name="moe_combine",
    )(pos, gate_w, x1, mod, ln_g.reshape(1, D_MODEL), ln_b.reshape(1, D_MODEL), ys)


def _moe_ffn(h2_tiles, route_i, gate_w, counts, layer, wg, wu, wd, x1, mod, ln_g, ln_b, split=False):
    pos, tile_expert, n_tiles, fill_rows = _moe_plan(route_i, counts)
    xs = _moe_dispatch(pos, fill_rows, h2_tiles)
    ys = _moe_experts(tile_expert, n_tiles, xs, layer, wg, wu, wd)
    return _moe_combine(pos, gate_w, x1, mod, ln_g, ln_b, ys, split)


DN_GATE_COL0 = DN_CONV_CH
N_DIRS = 2


def _dn_prep_kernel(q_ref, k_ref, v_ref, wq_ref, wk_ref, wv_ref, qo_ref, ko_ref, vo_ref):
    length = q_ref.shape[0]
    row = lax.broadcasted_iota(jnp.int32, (length, DN_DK), 0)

    def conv(x, w):
        prev = jnp.where(row == 0, 0.0, pltpu.roll(x, 1, axis=0))
        nxt = jnp.where(row == length - 1, 0.0, pltpu.roll(x, length - 1, axis=0))
        return _silu(prev * w[0:1] + x * w[1:2] + nxt * w[2:3])

    def l2n(x):
        return x * lax.rsqrt(jnp.sum(x * x, axis=-1, keepdims=True) + 1e-6)

    for h in range(DN_HEADS):
        sl = slice(h * DN_DK, (h + 1) * DN_DK)
        q, k, v = (r[:, sl].astype(F32) for r in (q_ref, k_ref, v_ref))
        qo_ref[h] = (l2n(conv(q, wq_ref[:, sl])) * (DN_DK ** -0.5)).astype(qo_ref.dtype)
        ko_ref[h] = l2n(conv(k, wk_ref[:, sl])).astype(ko_ref.dtype)
        vo_ref[h] = conv(v, wv_ref[:, sl]).astype(vo_ref.dtype)


def _dn_prep(z1, conv_w, nb, length, row_blk0):
    shp = jax.ShapeDtypeStruct((DN_HEADS, nb * length, DN_DK), BF16)
    blk = lambda part: pl.BlockSpec((length, DN_QK_WIDTH), lambda b: (row_blk0 + b, part))
    wblk = lambda part: pl.BlockSpec((3, DN_QK_WIDTH), lambda b: (0, part))
    oblk = pl.BlockSpec((DN_HEADS, length, DN_DK), lambda b: (0, b, 0))
    return pl.pallas_call(
        _dn_prep_kernel,
        out_shape=(shp, shp, shp),
        grid=(nb,),
        in_specs=[blk(0), blk(1), blk(2), wblk(0), wblk(1), wblk(2)],
        out_specs=(oblk, oblk, oblk),
        compiler_params=_params("arbitrary"),
        name="deltanet_prep",
    )(z1, z1, z1, conv_w, conv_w, conv_w)


def _dn_gates_kernel(ab_ref, alog_ref, dtb_ref, o_ref):
    length = ab_ref.shape[0]
    x = ab_ref[...]
    a = x + dtb_ref[...]
    softplus = jnp.maximum(a, 0.0) + jnp.log1p(jnp.exp(-jnp.abs(a)))
    g = -jnp.exp(alog_ref[...]) * softplus
    beta = jax.nn.sigmoid(x)
    ri = lax.broadcasted_iota(jnp.int32, (DN_CHUNK, DN_CHUNK), 0)
    ci = lax.broadcasted_iota(jnp.int32, (DN_CHUNK, DN_CHUNK), 1)
    tril = (ri >= ci).astype(F32)
    triu = (ri <= ci).astype(F32)
    lane = lax.broadcasted_iota(jnp.int32, (DN_CHUNK, LANES), 1)
    for c in range(length // DN_CHUNK):
        rows = slice(c * DN_CHUNK, (c + 1) * DN_CHUNK)
        gch = g[rows]
        pre = jnp.dot(tril, gch, precision=lax.Precision.HIGHEST, preferred_element_type=F32)
        suf = jnp.dot(triu, gch, precision=lax.Precision.HIGHEST, preferred_element_type=F32)
        gc = jnp.where(lane < DN_HEADS, pre, suf)
        o_ref[rows, :] = jnp.where(lane < N_DIRS * DN_HEADS, gc, beta[rows])


def _dn_gates(z_ab, a_log, dt_bias, nb, length, row_blk0):
    pad = LANES - N_DIRS * DN_HEADS
    alog = jnp.pad(a_log.reshape(1, -1).astype(F32), ((0, 0), (0, pad)))
    dtb = jnp.pad(dt_bias.reshape(1, -1).astype(F32), ((0, 0), (0, pad)))
    return pl.pallas_call(
        _dn_gates_kernel,
        out_shape=jax.ShapeDtypeStruct((nb * length, LANES), F32),
        grid=(nb,),
        in_specs=[
            pl.BlockSpec((length, LANES), lambda b: (row_blk0 + b, 0)),
            pl.BlockSpec((1, LANES), lambda b: (0, 0)),
            pl.BlockSpec((1, LANES), lambda b: (0, 0)),
        ],
        out_specs=pl.BlockSpec((length, LANES), lambda b: (b, 0)),
        compiler_params=_params("arbitrary"),
        name="deltanet_gates",
    )(z_ab, alog, dtb)


def _delta_chunk(q, k, v, beta, gcol, grow, s, lower):
    ri = lax.broadcasted_iota(jnp.int32, (DN_CHUNK, DN_CHUNK), 0)
    ci = lax.broadcasted_iota(jnp.int32, (DN_CHUNK, DN_CHUNK), 1)
    incl = (ri >= ci) if lower else (ri <= ci)
    strict = (ri > ci) if lower else (ri < ci)
    decay = jnp.exp(jnp.where(incl, gcol - grow, -jnp.inf))
    kb = k * beta
    kq = _dot_nt(jnp.concatenate([kb, q], axis=0).astype(BF16), k.astype(BF16))
    kk, a_qk = kq[:DN_CHUNK], kq[DN_CHUNK:] * decay
    yield
    tri_l = jnp.where(strict, kk * decay, 0.0)

    def off_block(size):
        same = (ri // (2 * size)) == (ci // (2 * size))
        rpar, cpar = (ri // size) % 2, (ci // size) % 2
        return same & ((rpar == 1) & (cpar == 0) if lower else (rpar == 0) & (cpar == 1))

    p = jnp.where(ri == ci, 1.0, 0.0) - jnp.where(off_block(1), tri_l, 0.0)
    size = 2
    while size < DN_CHUNK:
        p16 = p.astype(BF16)
        pc = _dot(p16, jnp.where(off_block(size), tri_l, 0.0).astype(BF16))
        yield
        p = p - _dot(pc.astype(BF16), p16)
        yield
        size *= 2
    eg = jnp.exp(gcol)
    uw = _dot(p.astype(BF16), jnp.concatenate([v * beta, kb * eg], axis=1).astype(BF16))
    yield
    u, w = uw[:, :DN_DV], uw[:, DN_DV:]
    ws = _dot(jnp.concatenate([w, q * eg], axis=0).astype(BF16), s.astype(BF16))
    v_new, o_inter = u - ws[:DN_CHUNK], ws[DN_CHUNK:]
    yield
    v_new16 = v_new.astype(BF16)
    o = o_inter + _dot(a_qk.astype(BF16), v_new16)
    g_last = gcol[DN_CHUNK - 1:DN_CHUNK] if lower else gcol[0:1]
    kd = k * jnp.exp(g_last - gcol)
    s_new = s * jnp.exp(g_last) + _dot_tn(kd.astype(BF16), v_new16)
    return o, s_new


def _run_interleaved(chains):
    results = [None] * len(chains)
    active = list(enumerate(chains))
    while active:
        still = []
        for idx, gen in active:
            try:
                next(gen)
                still.append((idx, gen))
            except StopIteration as stop:
                results[idx] = stop.value
        active = still
    return results


def _dn_scan_kernel(*refs, has_init, want_final):
    (qf_ref, kf_ref, vf_ref, gf_ref, qb_ref, kb_ref, vb_ref, gb_ref), rest = refs[:8], refs[8:]
    if has_init:
        s0_ref, rest = rest[0], rest[1:]
    of_ref, ob_ref = rest[0], rest[1]
    rest = rest[2:]
    if want_final:
        sf_ref, rest = rest[0], rest[1:]
    s_ref = rest[0]
    c = pl.program_id(1)
    nc = pl.num_programs(1)
    per_request = N_DIRS * DN_HEADS
    states = [(u, d, h) for u in range(DN_SCAN_BATCH) for d in range(N_DIRS) for h in range(DN_HEADS)]

    @pl.when(c == 0)
    def _():
        for j, (u, d, h) in enumerate(states):
            s_ref[j] = s0_ref[u, d, h] if has_init else jnp.zeros((DN_DK, DN_DV), F32)

    dirs = ((qf_ref, kf_ref, vf_ref, gf_ref, of_ref, True), (qb_ref, kb_ref, vb_ref, gb_ref, ob_ref, False))
    chains = []
    for u in range(DN_SCAN_BATCH):
        for d, (q_ref, k_ref, v_ref, g_ref, o_ref, lower) in enumerate(dirs):
            gates = g_ref[u]
            gates_t = gates.T
            for h in range(DN_HEADS):
                j = d * DN_HEADS + h
                jb = per_request + j
                chains.append(_delta_chunk(q_ref[h, u].astype(F32), k_ref[h, u].astype(F32),
                                           v_ref[h, u].astype(F32), gates[:, jb:jb + 1], gates[:, j:j + 1],
                                           gates_t[j:j + 1, :], s_ref[u * per_request + j], lower))
    results = _run_interleaved(chains)
    for j, (u, d, h) in enumerate(states):
        o, s_new = results[j]
        dirs[d][4][u, :, h * DN_DV:(h + 1) * DN_DV] = o.astype(of_ref.dtype)
        s_ref[j] = s_new

    if want_final:
        @pl.when(c == nc - 1)
        def _():
            for j, (u, d, h) in enumerate(states):
                sf_ref[u, d, h] = s_ref[j]


def _dn_scan(qn, kn, vn, gates, s0, nb, length, want_final):
    nc = length // DN_CHUNK
    bb = DN_SCAN_BATCH
    has_init = s0 is not None
    qn, kn, vn = (t.reshape(DN_HEADS, nb, length, DN_DK) for t in (qn, kn, vn))
    gates = gates.reshape(nb, length, LANES)
    fwd = lambda b, c: (b, c, 0)
    bwd = lambda b, c: (b, nc - 1 - c, 0)
    hm = lambda im: pl.BlockSpec((DN_HEADS, bb, DN_CHUNK, DN_DK), lambda b, c: (0,) + im(b, c))
    gm = lambda im: pl.BlockSpec((bb, DN_CHUNK, LANES), im)
    in_specs = [hm(fwd), hm(fwd), hm(fwd), gm(fwd), hm(bwd), hm(bwd), hm(bwd), gm(bwd)]
    args = [qn, kn, vn, gates, qn, kn, vn, gates]
    state_blk = pl.BlockSpec((bb, N_DIRS, DN_HEADS, DN_DK, DN_DV), lambda b, c: (b, 0, 0, 0, 0))
    if has_init:
        in_specs.append(state_blk)
        args.append(s0)
    o_shape = jax.ShapeDtypeStruct((nb, length, DN_WIDTH), BF16)
    out_shape = [o_shape, o_shape]
    out_specs = [pl.BlockSpec((bb, DN_CHUNK, DN_WIDTH), fwd), pl.BlockSpec((bb, DN_CHUNK, DN_WIDTH), bwd)]
    if want_final:
        out_shape.append(jax.ShapeDtypeStruct((nb, N_DIRS, DN_HEADS, DN_DK, DN_DV), F32))
        out_specs.append(state_blk)
    outs = pl.pallas_call(
        functools.partial(_dn_scan_kernel, has_init=has_init, want_final=want_final),
        out_shape=tuple(out_shape),
        grid=(nb // bb, nc),
        in_specs=in_specs,
        out_specs=tuple(out_specs),
        scratch_shapes=[pltpu.VMEM((bb * N_DIRS * DN_HEADS, DN_DK, DN_DV), F32)],
        compiler_params=_params("arbitrary", "arbitrary"),
        name="deltanet_scan",
    )(*args)
    return [outs[0].reshape(nb * length, DN_WIDTH), outs[1].reshape(nb * length, DN_WIDTH)] + list(outs[2:])


DN_SCAN_BATCH = 2


def _deltanet_stream(z1, z_ab, conv_w, a_log, dt_bias, s0, nb, length, row0, want_final):
    row_blk0 = row0 // length
    qn, kn, vn = _dn_prep(z1, conv_w, nb, length, row_blk0)
    gates = _dn_gates(z_ab, a_log, dt_bias, nb, length, row_blk0)
    outs = _dn_scan(qn, kn, vn, gates, s0, nb, length, want_final)
    return outs[0], outs[1], (outs[2] if want_final else None)


def kernel(x_prompt, x_sample, cache_nat_k, cache_nat_v, cache_diff_k, cache_diff_v, state_delta, c, c_ctx, w_in_even, w_out_even, nat_bias, diff_lam, diff_subln, w_in_odd, conv_odd, a_log_odd, dt_bias_odd, onorm_odd, w_out_odd, ada_w, ada_b, ln_g, ln_b, router_w, router_b, moe_wg, moe_wu, moe_wd):
    x_parts = (x_prompt.reshape(N_PROMPT, D_MODEL), x_sample.reshape(N_SAMPLE, D_MODEL))
    cond =jnp.concatenate([c_ctx[None], c, jnp.zeros((N_COND - 1 - DEC_BATCH, D_MODEL), F32)], axis=0)
    mod = _ada_modulation(cond, ada_w, ada_b).reshape(DEPTH, N_COND, 6, D_MODEL)
    router_w_pad = jnp.pad(router_w, ((0, 0), (0, LANES - N_EXPERTS)))

    lam_init = 0.8 - 0.6 * math.exp(-0.3 * 0)
    z = _input_projection(x_parts, mod[0], w_in_even[0].astype(BF16))
    new_nat_k = z[:N_PROMPT, _NAT_K0:_NAT_V0].reshape(BATCH, 1, SEQ, NAT_HEADS, HEAD_DIM)
    new_nat_v = z[:N_PROMPT, _NAT_V0:_DIFF_Q0].reshape(BATCH, 1, SEQ, NAT_HEADS, HEAD_DIM)
    new_diff_k = z[:N_PROMPT, _DIFF_K0:_DIFF_V0].reshape(BATCH, 1, SEQ, DIFF_HEADS, 2, HEAD_DIM)
    new_diff_v = z[:N_PROMPT, _DIFF_V0:].reshape(BATCH, 1, SEQ, DIFF_HEADS, 2 * HEAD_DIM)
    o_ctx = _context_attention(z, diff_lam[0], diff_subln[0], lam_init)
    o_nat = _nat_latent_attention(z, cache_nat_k[:, 0].reshape(DEC_BATCH, PAST_LEN, NAT_WIDTH),
                                  cache_nat_v[:, 0].reshape(DEC_BATCH, PAST_LEN, NAT_WIDTH),
                                  _nat_bias_diagonals(nat_bias[0]))
    o_diff = _diff_latent_attention(z, cache_diff_k[:, 0].reshape(DEC_BATCH, PAST_LEN, DIFF_WIDTH),
                                    cache_diff_v[:, 0].reshape(DEC_BATCH, PAST_LEN, DIFF_WIDTH),
                                    diff_lam[0], diff_subln[0], lam_init)
    x1, *routed = _output_projection((o_ctx, o_nat, o_diff), x_parts, mod[0], w_out_even[0].astype(BF16),
                                     ln_g[0, 0], ln_b[0, 0], router_w_pad, router_b)
    x = _moe_ffn(*routed, 0, moe_wg, moe_wu, moe_wd, x1, mod[0], ln_g[0, 1], ln_b[0, 1])

    w1 = w_in_odd[0]
    w1 = jnp.concatenate([w1[:, :DN_CONV_CH], w1[:, DN_CONV_CH + 4 * DN_HEADS:],
                          w1[:, DN_CONV_CH:DN_CONV_CH + 4 * DN_HEADS],
                          jnp.zeros((D_MODEL, LANES - 4 * DN_HEADS), F32)], axis=1).astype(BF16)
    z1, z_ab = _input_projection((x,), mod[1], w1, f32_tail=LANES)
    of_p, ob_p, new_state = _deltanet_stream(z1, z_ab, conv_odd[0], a_log_odd[0], dt_bias_odd[0],
                                             None, BATCH, SEQ, 0, True)
    of_s, ob_s, _ = _deltanet_stream(z1, z_ab, conv_odd[0], a_log_odd[0], dt_bias_odd[0],
                                     state_delta[:, 0], DEC_BATCH, DEC_SEQ, N_PROMPT, False)
    x1, *routed = _output_projection((of_p, of_s, ob_p, ob_s), (x,), mod[1], w_out_odd[0].astype(BF16),
                                     ln_g[1, 0], ln_b[1, 0], router_w_pad, router_b,
                                     deltanet=(z1, onorm_odd[0]))
    y_prompt, y_sample = _moe_ffn(*routed, 1, moe_wg, moe_wu, moe_wd, x1, mod[1], ln_g[1, 1], ln_b[1, 1],
                                  split=True)
    y_prompt = y_prompt.reshape(BATCH, SEQ, D_MODEL)
    y_sample = y_sample.reshape(DEC_BATCH, DEC_SEQ, D_MODEL)
    return (y_prompt, y_sample, new_nat_k, new_nat_v, new_diff_k, new_diff_v, new_state[:, None])
```

```python
import functools
import math

import jax
import jax.numpy as jnp
import numpy as np
from jax import lax
from jax.experimental import pallas as pl
from jax.experimental.pallas import tpu as pltpu

F32 = jnp.float32
BF16 = jnp.bfloat16

D_MODEL = 1024
BATCH = 16
SEQ = 256
DEC_BATCH = 8
DEC_SEQ = 1024
PAST_LEN = 256
GRID_W = 64
N_ROWS = DEC_SEQ // GRID_W

HEAD_DIM = 64
NAT_HEADS = 8
NAT_WIN_H = 8
NAT_WIN_W = 16
DIFF_HEADS = 4
NAT_WIDTH = NAT_HEADS * HEAD_DIM
DIFF_WIDTH = DIFF_HEADS * 2 * HEAD_DIM
EVEN_IN = 3 * NAT_WIDTH + 3 * DIFF_WIDTH
ROPE_THETA = 10000.0

DN_HEADS = 8
DN_DK = 128
DN_DV = 128
DN_QK_WIDTH = DN_HEADS * DN_DK
DN_WIDTH = DN_HEADS * DN_DV
DN_CONV_CH = 2 * DN_QK_WIDTH + DN_WIDTH
DN_CHUNK = 64

N_EXPERTS = 16
N_GROUPS = 4
EXPERTS_PER_GROUP = N_EXPERTS // N_GROUPS
EXPERT_FF = 512

DEPTH = 2
ALPHA = (2 * DEPTH) ** 0.25
EPS = 1e-5

N_PROMPT = BATCH * SEQ
N_SAMPLE = DEC_BATCH * DEC_SEQ
N_TOK = N_PROMPT + N_SAMPLE
N_COND = 16

VMEM_LIMIT = 56 * 1024 * 1024
LANES = 128
SUBLANES = 8
N_CHUNKS = D_MODEL // LANES
assert N_CHUNKS == SUBLANES


def _params(*sem):
    return pltpu.CompilerParams(dimension_semantics=sem, vmem_limit_bytes=VMEM_LIMIT)


def _cond_row(row0):
    return jnp.where(row0 < N_PROMPT, 0, 1 + (row0 - N_PROMPT) // DEC_SEQ)


def _silu(x):
    return x * jax.nn.sigmoid(x)


def _layer_norm(r, g, b):
    mu = jnp.mean(r, axis=-1, keepdims=True)
    xc = r - mu
    var = jnp.mean(xc * xc, axis=-1, keepdims=True)
    return xc * lax.rsqrt(var + EPS) * g + b


def _dot(a, b):
    return jnp.dot(a, b, preferred_element_type=F32)


def _dot_nt(a, b):
    return lax.dot_general(a, b, (((1,), (1,)), ((), ())), preferred_element_type=F32)


def _dot_tn(a, b):
    return lax.dot_general(a, b, (((0,), (0,)), ((), ())), preferred_element_type=F32)


ADA_TN = 1536


def _ada_kernel(c_ref, w_ref, b_ref, o_ref):
    s = _silu(c_ref[...])
    o_ref[0] = _dot(s.astype(BF16), w_ref[0].astype(BF16)) + b_ref[0]


def _ada_modulation(cond, ada_w, ada_b):
    n = 6 * D_MODEL
    return pl.pallas_call(
        _ada_kernel,
        out_shape=jax.ShapeDtypeStruct((DEPTH, N_COND, n), F32),
        grid=(DEPTH, n // ADA_TN),
        in_specs=[
            pl.BlockSpec((N_COND, D_MODEL), lambda l, j: (0, 0)),
            pl.BlockSpec((1, D_MODEL, ADA_TN), lambda l, j: (l, 0, j)),
            pl.BlockSpec((1, 1, ADA_TN), lambda l, j: (l, 0, j)),
        ],
        out_specs=pl.BlockSpec((1, N_COND, ADA_TN), lambda l, j: (l, 0, j)),
        compiler_params=_params("arbitrary", "arbitrary"),
        name="ada_modulation",
    )(cond, ada_w, ada_b.reshape(DEPTH, 1, n))


PROJ_TM = 512


def _stream_specs(widths, tm):
    if len(widths) == 1:
        return [pl.BlockSpec((tm, widths[0]), lambda i: (i, 0))]
    n_ctx = N_PROMPT // tm
    return ([pl.BlockSpec((tm, widths[0]), lambda i: (jnp.minimum(i, n_ctx - 1), 0))]
            + [pl.BlockSpec((tm, w), lambda i: (jnp.maximum(i - n_ctx, 0), 0)) for w in widths[1:]])


def _stream_tile(refs, tm):
    if len(refs) == 1:
        return refs[0][...]
    latent = [r[...] for r in refs[1:]]
    latent = latent[0] if len(latent) == 1 else jnp.concatenate(latent, axis=1)
    return jnp.where(pl.program_id(0) * tm < N_PROMPT, refs[0][...], latent)


def _inproj_kernel(*refs, n_x):
    x_refs, (mod_ref, w_ref, o_ref), tail_refs = refs[:n_x], refs[n_x:n_x + 3], refs[n_x + 3:]
    m = mod_ref[0]
    h = _stream_tile(x_refs, o_ref.shape[0]) * (1.0 + m[1:2]) + m[0:1]
    z = _dot(h.astype(BF16), w_ref[...])
    n_main = o_ref.shape[1]
    o_ref[...] = z[:, :n_main].astype(o_ref.dtype)
    if tail_refs:
        tail_refs[0][...] = z[:, n_main:]


def _input_projection(x_parts, mod, w_bf16, tm=PROJ_TM, f32_tail=0):
    n = w_bf16.shape[1] - f32_tail
    out_shape = [jax.ShapeDtypeStruct((N_TOK, n), BF16 if f32_tail else F32)]
    out_specs = [pl.BlockSpec((tm, n), lambda i: (i, 0))]
    if f32_tail:
        out_shape.append(jax.ShapeDtypeStruct((N_TOK, f32_tail), F32))
        out_specs.append(pl.BlockSpec((tm, f32_tail), lambda i: (i, 0)))
    outs = pl.pallas_call(
        functools.partial(_inproj_kernel, n_x=len(x_parts)),
        out_shape=tuple(out_shape),
        grid=(N_TOK // tm,),
        in_specs=_stream_specs([a.shape[1] for a in x_parts], tm) + [
            pl.BlockSpec((1, 6, D_MODEL), lambda i: (_cond_row(i * tm), 0, 0)),
            pl.BlockSpec((D_MODEL, n + f32_tail), lambda i: (0, 0)),
        ],
        out_specs=tuple(out_specs),
        compiler_params=_params("arbitrary"),
        name="input_projection",
    )(*x_parts, mod, w_bf16)
    return outs if f32_tail else outs[0]


_NAT_Q0, _NAT_K0, _NAT_V0 = 0, NAT_WIDTH, 2 * NAT_WIDTH
_DIFF_Q0 = 3 * NAT_WIDTH
_DIFF_K0 = _DIFF_Q0 + DIFF_WIDTH
_DIFF_V0 = _DIFF_K0 + DIFF_WIDTH
ATTN_SCALE = HEAD_DIM ** -0.5


def _diff_lambda(lam_ref, lam_init):
    lp = lam_ref[...]
    return (jnp.exp(jnp.sum(lp[0:1] * lp[1:2], axis=-1, keepdims=True))
            - jnp.exp(jnp.sum(lp[2:3] * lp[3:4], axis=-1, keepdims=True)) + lam_init)


def _scaled_q(q):
    assert math.log2(HEAD_DIM) % 2 == 0
    return (q * ATTN_SCALE).astype(BF16)


def _with_ones(v):
    width = v.shape[1] if v.shape[1] % LANES == 0 else LANES - v.shape[1] % LANES
    return jnp.concatenate([v, jnp.ones((v.shape[0], width), v.dtype)], axis=1)


def _sub_norm(o, g, lam_init):
    ms = jnp.mean(o * o, axis=-1, keepdims=True)
    return o * lax.rsqrt(ms + EPS) * g * (1.0 - lam_init)


def _ctx_attn_kernel(z_ref, lam_ref, subg_ref, o_ref, *, lam_init):
    lam = _diff_lambda(lam_ref, lam_init)

    def softmax_av(q0, k0, v0, v_width):
        q = _scaled_q(z_ref[:, q0:q0 + HEAD_DIM])
        k = z_ref[:, k0:k0 + HEAD_DIM].astype(BF16)
        s = _dot_nt(q, k)
        yield
        e = jnp.exp(s - jnp.max(s, axis=-1, keepdims=True))
        yield
        av = _dot(e.astype(BF16), _with_ones(z_ref[:, v0:v0 + v_width].astype(BF16)))
        yield
        return av[:, :v_width] / av[:, v_width:v_width + 1]

    chains = [softmax_av(_NAT_Q0 + h * HEAD_DIM, _NAT_K0 + h * HEAD_DIM, _NAT_V0 + h * HEAD_DIM, HEAD_DIM)
              for h in range(NAT_HEADS)]
    chains += [softmax_av(_DIFF_Q0 + i * HEAD_DIM, _DIFF_K0 + i * HEAD_DIM,
                          _DIFF_V0 + (i // 2) * 2 * HEAD_DIM, 2 * HEAD_DIM)
               for i in range(2 * DIFF_HEADS)]
    outs = _run_interleaved(chains)
    for h in range(NAT_HEADS):
        o_ref[:, h * HEAD_DIM:(h + 1) * HEAD_DIM] = outs[h].astype(o_ref.dtype)
    for h in range(DIFF_HEADS):
        c = NAT_WIDTH + h * 2 * HEAD_DIM
        o = outs[NAT_HEADS + 2 * h] - lam * outs[NAT_HEADS + 2 * h + 1]
        o_ref[:, c:c + 2 * HEAD_DIM] = _sub_norm(o, subg_ref[...], lam_init).astype(o_ref.dtype)


def _context_attention(z, lam_p, subln_g, lam_init):
    return pl.pallas_call(
        functools.partial(_ctx_attn_kernel, lam_init=lam_init),
        out_shape=jax.ShapeDtypeStruct((N_PROMPT, D_MODEL), BF16),
        grid=(BATCH,),
        in_specs=[
            pl.BlockSpec((SEQ, EVEN_IN), lambda b: (b, 0)),
            pl.BlockSpec((4, HEAD_DIM), lambda b: (0, 0)),
            pl.BlockSpec((1, 2 * HEAD_DIM), lambda b: (0, 0)),
        ],
        out_specs=pl.BlockSpec((SEQ, D_MODEL), lambda b: (b, 0)),
        compiler_params=_params("arbitrary"),
        name="context_attention",
    )(z, lam_p, subln_g.reshape(1, 2 * HEAD_DIM))


NAT_TQ = 512
NAT_GROUP = 4
NAT_KEYS = 12 * GRID_W
_SAMPLE_BLK0 = N_PROMPT // DEC_SEQ


NAT_SPAN = NAT_KEYS // GRID_W
NAT_SPAN_PAD = 16


def _nat_bias_diagonals(rel_bias):
    rows = N_ROWS
    kh = min(NAT_WIN_H, rows)
    row_onehot = np.zeros((2 * NAT_WIN_H - 1, rows, NAT_SPAN), np.float32)
    for r in range(rows):
        start = min(max(r - kh // 2, 0), rows - kh)
        a0 = _nat_key0(r * GRID_W // NAT_TQ) // GRID_W
        assert a0 <= start and start + kh <= a0 + NAT_SPAN
        for a in range(start, start + kh):
            row_onehot[a - r + NAT_WIN_H - 1, r, a - a0] = 1.0
    lane_onehot = np.zeros((2 * NAT_WIN_W - 1, LANES), np.float32)
    for lane in range(2 * GRID_W - 1):
        lane_onehot[np.clip(lane - (GRID_W - 1), 1 - NAT_WIN_W, NAT_WIN_W - 1) + NAT_WIN_W - 1, lane] = 1.0
    hi = lax.Precision.HIGHEST
    by_row = jnp.einsum('hed,era->hrad', rel_bias.astype(F32), row_onehot, precision=hi)
    diag = jnp.einsum('hrad,dl->hral', by_row, lane_onehot, precision=hi)
    diag = jnp.where((row_onehot.sum(0) > 0)[None, :, :, None], diag, -jnp.inf)
    return jnp.pad(diag, ((0, 0), (0, 0), (0, NAT_SPAN_PAD - NAT_SPAN), (0, 0)))


def _nat_bias_tiles(diag_ref, bias_ref):
    shape = (GRID_W, LANES)
    q = lax.broadcasted_iota(jnp.int32, shape, 0)
    lane = lax.broadcasted_iota(jnp.int32, shape, 1)
    k = lane % GRID_W
    c0 = jnp.clip(q - NAT_WIN_W // 2, 0, GRID_W - NAT_WIN_W)
    in_cols = (k >= c0) & (k < c0 + NAT_WIN_W)
    for j in range(NAT_GROUP):
        for r in range(NAT_TQ // GRID_W):
            for pair in range(NAT_SPAN // 2):
                even = jnp.broadcast_to(diag_ref[j, r, 2 * pair:2 * pair + 1, :], shape)
                odd = jnp.broadcast_to(diag_ref[j, r, 2 * pair + 1:2 * pair + 2, :], shape)
                even = pltpu.roll(even, GRID_W + 1, 1, stride=1, stride_axis=0)
                odd = pltpu.roll(odd, 1, 1, stride=1, stride_axis=0)
                tile = jnp.where(lane < GRID_W, even, odd)
                bias_ref[j, r * GRID_W:(r + 1) * GRID_W, pair * LANES:(pair + 1) * LANES] = (
                    jnp.where(in_cols, tile, -jnp.inf))


def _nat_key0(qt):
    half = DEC_SEQ // NAT_TQ // 2
    return (qt >= half) * (DEC_SEQ - NAT_KEYS)


def _nat_latent_kernel(q_ref, k_ref, v_ref, ck_ref, cv_ref, diag_ref, o_ref, bias_ref):
    key0 = pl.multiple_of(_nat_key0(pl.program_id(1)).astype(jnp.int32), DEC_SEQ - NAT_KEYS)
    keys = pl.ds(key0, NAT_KEYS)

    @pl.when(pl.program_id(2) == 0)
    def _():
        _nat_bias_tiles(diag_ref, bias_ref)

    def head(j):
        sl = slice(j * HEAD_DIM, (j + 1) * HEAD_DIM)
        q = _scaled_q(q_ref[:, sl])
        s_loc = _dot_nt(q, k_ref[keys, sl].astype(BF16)) + bias_ref[j]
        s_ctx = _dot_nt(q, ck_ref[0, :, sl].astype(BF16))
        yield
        m = jnp.maximum(jnp.max(s_loc, axis=-1, keepdims=True), jnp.max(s_ctx, axis=-1, keepdims=True))
        yield
        e_loc = jnp.exp(s_loc - m)
        e_ctx = jnp.exp(s_ctx - m)
        yield
        o = (_dot(e_loc.astype(BF16), _with_ones(v_ref[keys, sl].astype(BF16)))
             + _dot(e_ctx.astype(BF16), _with_ones(cv_ref[0, :, sl].astype(BF16))))
        yield
        return (o[:, :HEAD_DIM] / o[:, HEAD_DIM:HEAD_DIM + 1]).astype(o_ref.dtype)

    outs = _run_interleaved([head(j) for j in range(NAT_GROUP)])
    for j in range(NAT_GROUP):
        o_ref[:, j * HEAD_DIM:(j + 1) * HEAD_DIM] = outs[j]


def _nat_latent_attention(z, ck, cv, diag):
    nq = DEC_SEQ // NAT_TQ
    qblk0 = N_PROMPT // NAT_TQ
    pair = NAT_GROUP * HEAD_DIM
    return pl.pallas_call(
        _nat_latent_kernel,
        out_shape=jax.ShapeDtypeStruct((N_SAMPLE, NAT_WIDTH), BF16),
        grid=(NAT_HEADS // NAT_GROUP, nq, DEC_BATCH),
        in_specs=[
            pl.BlockSpec((NAT_TQ, pair), lambda hp, qt, b: (qblk0 + b * nq + qt, _NAT_Q0 // pair + hp)),
            pl.BlockSpec((DEC_SEQ, pair), lambda hp, qt, b: (_SAMPLE_BLK0 + b, _NAT_K0 // pair + hp)),
            pl.BlockSpec((DEC_SEQ, pair), lambda hp, qt, b: (_SAMPLE_BLK0 + b, _NAT_V0 // pair + hp)),
            pl.BlockSpec((1, PAST_LEN, pair), lambda hp, qt, b: (b, 0, hp)),
            pl.BlockSpec((1, PAST_LEN, pair), lambda hp, qt, b: (b, 0, hp)),
            pl.BlockSpec((NAT_GROUP, NAT_TQ // GRID_W, NAT_SPAN_PAD, LANES), lambda hp, qt, b: (hp, qt, 0, 0)),
        ],
        out_specs=pl.BlockSpec((NAT_TQ, pair), lambda hp, qt, b: (b * nq + qt, hp)),
        scratch_shapes=[pltpu.VMEM((NAT_GROUP, NAT_TQ, NAT_KEYS), F32)],
        compiler_params=_params("arbitrary", "arbitrary", "arbitrary"),
        name="nat_latent_attention",
    )(z, z, z, ck, cv, diag)


DIFF_TQ = 1024


def _rope_tables():
    nf = HEAD_DIM // 4
    t = jnp.arange(DEC_SEQ)
    inv = ROPE_THETA ** (-jnp.arange(nf, dtype=F32) / nf)
    ang_r = (t // GRID_W).astype(F32)[:, None] * inv
    ang_c = (t % GRID_W).astype(F32)[:, None] * inv
    cos = jnp.concatenate([jnp.cos(ang_r)] * 2 + [jnp.cos(ang_c)] * 2, axis=-1)
    sin = jnp.concatenate([-jnp.sin(ang_r), jnp.sin(ang_r), -jnp.sin(ang_c), jnp.sin(ang_c)], axis=-1)
    return jnp.tile(cos, (1, 2)), jnp.tile(sin, (1, 2))


def _rope(x, cos, sin):
    nf = HEAD_DIM // 4
    lane = lax.broadcasted_iota(jnp.int32, x.shape, 1)
    upper = (lane // nf) % 2 == 1
    partner = jnp.where(upper, pltpu.roll(x, nf, axis=1), pltpu.roll(x, x.shape[1] - nf, axis=1))
    return x * cos + partner * sin


def _diff_latent_kernel(q_ref, k_ref, v_ref, ck_ref, cv_ref, cos_ref, sin_ref, lam_ref, subg_ref,
                        o_ref, kr_ref, *, lam_init):
    qt = pl.program_id(2)

    @pl.when(qt == 0)
    def _():
        kr_ref[...] = _rope(k_ref[...], cos_ref[...], sin_ref[...]).astype(BF16)

    lam = _diff_lambda(lam_ref, lam_init)
    row0 = pl.multiple_of(qt * DIFF_TQ, DIFF_TQ)
    q = _scaled_q(_rope(q_ref[...], cos_ref[pl.ds(row0, DIFF_TQ), :], sin_ref[pl.ds(row0, DIFF_TQ), :]))
    v_loc = _with_ones(v_ref[...].astype(BF16))
    v_ctx = _with_ones(cv_ref[0].astype(BF16))
    width = 2 * HEAD_DIM

    def softmax_av(j):
        sl = slice(j * HEAD_DIM, (j + 1) * HEAD_DIM)
        s_ctx = _dot_nt(q[:, sl], ck_ref[0, :, sl].astype(BF16))
        s_loc = _dot_nt(q[:, sl], kr_ref[:, sl])
        yield
        m = jnp.maximum(jnp.max(s_loc, axis=-1, keepdims=True), jnp.max(s_ctx, axis=-1, keepdims=True))
        yield
        e_loc = jnp.exp(s_loc - m)
        e_ctx = jnp.exp(s_ctx - m)
        yield
        av = _dot(e_loc.astype(BF16), v_loc) + _dot(e_ctx.astype(BF16), v_ctx)
        yield
        return av[:, :width] / av[:, width:width + 1]

    parts = _run_interleaved([softmax_av(j) for j in range(2)])
    o = parts[0] - lam * parts[1]
    o_ref[...] = _sub_norm(o, subg_ref[...], lam_init).astype(o_ref.dtype)


def _diff_latent_attention(z, ck, cv, lam_p, subln_g, lam_init):
    nq = DEC_SEQ // DIFF_TQ
    qblk0 = N_PROMPT // DIFF_TQ
    w = 2 * HEAD_DIM
    cos, sin = _rope_tables()
    return pl.pallas_call(
        functools.partial(_diff_latent_kernel, lam_init=lam_init),
        out_shape=jax.ShapeDtypeStruct((N_SAMPLE, DIFF_WIDTH), BF16),
        grid=(DEC_BATCH, DIFF_HEADS, nq),
        in_specs=[
            pl.BlockSpec((DIFF_TQ, w), lambda b, h, qt: (qblk0 + b * nq + qt, _DIFF_Q0 // w + h)),
            pl.BlockSpec((DEC_SEQ, w), lambda b, h, qt: (_SAMPLE_BLK0 + b, _DIFF_K0 // w + h)),
            pl.BlockSpec((DEC_SEQ, w), lambda b, h, qt: (_SAMPLE_BLK0 + b, _DIFF_V0 // w + h)),
            pl.BlockSpec((1, PAST_LEN, w), lambda b, h, qt: (b, 0, h)),
            pl.BlockSpec((1, PAST_LEN, w), lambda b, h, qt: (b, 0, h)),
            pl.BlockSpec((DEC_SEQ, w), lambda b, h, qt: (0, 0)),
            pl.BlockSpec((DEC_SEQ, w), lambda b, h, qt: (0, 0)),
            pl.BlockSpec((4, HEAD_DIM), lambda b, h, qt: (0, 0)),
            pl.BlockSpec((1, w), lambda b, h, qt: (0, 0)),
        ],
        out_specs=pl.BlockSpec((DIFF_TQ, w), lambda b, h, qt: (b * nq + qt, h)),
        scratch_shapes=[pltpu.VMEM((DEC_SEQ, w), BF16)],
        compiler_params=_params("arbitrary", "arbitrary", "arbitrary"),
        name="diff_latent_attention",
    )(z, z, z, ck, cv, cos, sin, lam_p, subln_g.reshape(1, w))


OUT_TM = 256


def _route(logits_t, rb_col):
    sc = jax.nn.sigmoid(logits_t)
    bi = sc + rb_col
    srow = [sc[e:e + 1] for e in range(N_EXPERTS)]
    brow = [bi[e:e + 1] for e in range(N_EXPERTS)]
    n = EXPERTS_PER_GROUP
    gscore = []
    for g in range(N_GROUPS):
        v = brow[g * n:(g + 1) * n]
        best = None
        for i in range(n):
            for j in range(i + 1, n):
                s = v[i] + v[j]
                best = s if best is None else jnp.maximum(best, s)
        gscore.append(best)
    sel = jnp.zeros_like(gscore[0], dtype=jnp.int32)
    best = gscore[0]
    for g in range(1, N_GROUPS):
        better = gscore[g] > best
        sel = jnp.where(better, g, sel)
        best = jnp.where(better, gscore[g], best)

    def pick_group(rows, i):
        out = rows[(N_GROUPS - 1) * n + i]
        for g in range(N_GROUPS - 2, -1, -1):
            out = jnp.where(sel == g, rows[g * n + i], out)
        return out

    bv = [pick_group(brow, i) for i in range(n)]
    sv = [pick_group(srow, i) for i in range(n)]
    i1 = jnp.zeros_like(sel)
    m1, w1 = bv[0], sv[0]
    for i in range(1, n):
        better = bv[i] > m1
        i1 = jnp.where(better, i, i1)
        m1 = jnp.where(better, bv[i], m1)
        w1 = jnp.where(better, sv[i], w1)
    i2 = jnp.zeros_like(sel)
    m2 = jnp.full_like(m1, -jnp.inf)
    w2 = jnp.zeros_like(w1)
    for i in range(n):
        better = (i1 != i) & (bv[i] > m2)
        i2 = jnp.where(better, i, i2)
        m2 = jnp.where(better, bv[i], m2)
        w2 = jnp.where(better, sv[i], w2)
    tot = w1 + w2
    return sel * n + i1, sel * n + i2, w1 / tot, w2 / tot


def _store_token_tiles(ref, value):
    rows = value.shape[0]
    chunks = jnp.stack([value[:, c * LANES:(c + 1) * LANES] for c in range(N_CHUNKS)], axis=0)
    tiles = jnp.transpose(chunks.reshape(N_CHUNKS, rows // SUBLANES, SUBLANES, LANES), (1, 2, 0, 3))
    ref[...] = tiles.reshape(rows, N_CHUNKS, LANES)


def _load_token_tiles(ref):
    rows = ref.shape[0]
    tiles = ref[...].reshape(rows // SUBLANES, SUBLANES, N_CHUNKS, LANES)
    chunks = jnp.transpose(tiles, (2, 0, 1, 3)).reshape(N_CHUNKS, rows, LANES)
    return jnp.concatenate([chunks[c] for c in range(N_CHUNKS)], axis=1)


def _deltanet_mixer_out(o_refs, tm):
    of = _stream_tile(o_refs[0:2], tm).astype(F32)
    ob = _stream_tile(o_refs[2:4], tm).astype(F32)
    gate_ref, g_ref = o_refs[4], o_refs[5]
    pieces = []
    for h in range(DN_HEADS):
        sl = slice(h * DN_DV, (h + 1) * DN_DV)
        o = of[:, sl] + ob[:, sl]
        o = o * lax.rsqrt(jnp.mean(o * o, axis=-1, keepdims=True) + EPS) * g_ref[...]
        pieces.append((o * _silu(gate_ref[:, sl].astype(F32))).astype(BF16))
    return jnp.concatenate(pieces, axis=1)


def _outproj_kernel(*refs, n_o, n_x, deltanet):
    o_refs, x_refs, refs = refs[:n_o], refs[n_o:n_o + n_x], refs[n_o + n_x:]
    mod_ref, w_ref, lng_ref, lnb_ref, rw_ref, rb_ref, x1_ref, h2_ref, ri_ref, rw_out_ref, cnt_ref = refs
    tm = x1_ref.shape[0]
    m = mod_ref[0]
    y = _dot(_deltanet_mixer_out(o_refs, tm) if deltanet else _stream_tile(o_refs, tm), w_ref[...])
    x1 = _layer_norm(ALPHA * _stream_tile(x_refs, tm) + m[2:3] * y, lng_ref[...], lnb_ref[...])
    x1_ref[...] = x1
    h2 = x1 * (1.0 + m[4:5]) + m[3:4]
    _store_token_tiles(h2_ref, h2)
    rw = rw_ref[...]
    h_hi, rw_hi = h2.astype(BF16), rw.astype(BF16)
    h_lo = (h2 - h_hi.astype(F32)).astype(BF16)
    rw_lo = (rw - rw_hi.astype(F32)).astype(BF16)
    logits = _dot(h_hi, rw_hi) + (_dot(h_lo, rw_hi) + _dot(h_hi, rw_lo))
    e1, e2, w1, w2 = _route(logits.T[:N_EXPERTS], rb_ref[...])
    eid = lax.broadcasted_iota(jnp.int32, (N_EXPERTS, tm), 0)
    onehot = (eid == e1) | (eid == e2)
    ti = lax.broadcasted_iota(jnp.int32, (tm, tm), 0)
    tj = lax.broadcasted_iota(jnp.int32, (tm, tm), 1)
    earlier = jnp.where(ti < tj, 1.0, 0.0).astype(BF16)
    rank = _dot(jnp.where(onehot, 1.0, 0.0).astype(BF16), earlier)
    r1 = jnp.sum(jnp.where(eid == e1, rank, 0.0), axis=0, keepdims=True).astype(jnp.int32)
    r2 = jnp.sum(jnp.where(eid == e2, rank, 0.0), axis=0, keepdims=True).astype(jnp.int32)
    ri_ref[...] = jnp.concatenate([e1, e2, r1, r2, jnp.zeros((4, tm), jnp.int32)], axis=0)
    wt = jnp.concatenate([w1, w2, jnp.zeros((LANES - 2, tm), F32)], axis=0)
    rw_out_ref[...] = wt.T
    cnt = jnp.sum(jnp.where(onehot, 1.0, 0.0), axis=1, keepdims=True)
    cnt_ref[0] = jnp.broadcast_to(cnt, (N_EXPERTS, LANES)).astype(jnp.int32)


def _output_projection(o_parts, x_parts, mod, w_bf16, ln_g, ln_b, router_w_pad, router_b, tm=OUT_TM,
                       deltanet=None):
    nt = N_TOK // tm
    if deltanet is None:
        o_specs = _stream_specs([a.shape[1] for a in o_parts], tm)
    else:
        z1, onorm_g = deltanet
        o_specs = (_stream_specs([DN_WIDTH, DN_WIDTH], tm) + _stream_specs([DN_WIDTH, DN_WIDTH], tm)
                   + [pl.BlockSpec((tm, DN_WIDTH), lambda i: (i, DN_GATE_COL0 // DN_WIDTH)),
                      pl.BlockSpec((1, DN_DV), lambda i: (0, 0))])
        o_parts = tuple(o_parts) + (z1, onorm_g.reshape(1, DN_DV))
    return pl.pallas_call(
        functools.partial(_outproj_kernel, n_o=len(o_parts), n_x=len(x_parts), deltanet=deltanet is not None),
        out_shape=(jax.ShapeDtypeStruct((N_TOK, D_MODEL), F32),
                   jax.ShapeDtypeStruct((N_TOK, N_CHUNKS, LANES), F32),
                   jax.ShapeDtypeStruct((8, N_TOK), jnp.int32),
                   jax.ShapeDtypeStruct((N_TOK, LANES), F32),
                   jax.ShapeDtypeStruct((nt, N_EXPERTS, LANES), jnp.int32)),
        grid=(nt,),
        in_specs=o_specs + _stream_specs([a.shape[1] for a in x_parts], tm) + [
            pl.BlockSpec((1, 6, D_MODEL), lambda i: (_cond_row(i * tm), 0, 0)),
            pl.BlockSpec((D_MODEL, D_MODEL), lambda i: (0, 0)),
            pl.BlockSpec((1, D_MODEL), lambda i: (0, 0)),
            pl.BlockSpec((1, D_MODEL), lambda i: (0, 0)),
            pl.BlockSpec((D_MODEL, LANES), lambda i: (0, 0)),
            pl.BlockSpec((N_EXPERTS, 1), lambda i: (0, 0)),
        ],
        out_specs=(pl.BlockSpec((tm, D_MODEL), lambda i: (i, 0)),
                   pl.BlockSpec((tm, N_CHUNKS, LANES), lambda i: (i, 0, 0)),
                   pl.BlockSpec((8, tm), lambda i: (0, i)),
                   pl.BlockSpec((tm, LANES), lambda i: (i, 0)),
                   pl.BlockSpec((1, N_EXPERTS, LANES), lambda i: (i, 0, 0))),
        compiler_params=_params("arbitrary"),
        name="output_projection",
    )(*o_parts, *x_parts, mod, w_bf16, ln_g.reshape(1, D_MODEL), ln_b.reshape(1, D_MODEL),
      router_w_pad, router_b.reshape(N_EXPERTS, 1))


N_ASSIGN = 2 * N_TOK
MOE_TM = 256
MOE_ROWS = N_ASSIGN + N_EXPERTS * MOE_TM
MOE_TILES = MOE_ROWS // MOE_TM
DISPATCH_TM = 1024
COMBINE_TM = 256


def _moe_plan(route_i, counts):
    cnt = counts[:, :, 0]
    total = jnp.sum(cnt, axis=0)
    padded = (total + MOE_TM - 1) // MOE_TM * MOE_TM
    seg_end = jnp.cumsum(padded)
    seg_start = seg_end - padded
    tile_base = seg_start[None, :] + jnp.cumsum(cnt, axis=0) - cnt
    base_tok = jnp.repeat(tile_base, OUT_TM, axis=0)
    eids = jnp.arange(N_EXPERTS, dtype=jnp.int32)[None, :]
    pos = [jnp.sum(jnp.where(route_i[k][:, None] == eids, base_tok, 0), axis=1) + route_i[2 + k]
           for k in range(2)]
    pos = jnp.concatenate(pos).astype(jnp.int32)
    tile_row0 = jnp.arange(MOE_TILES, dtype=jnp.int32) * MOE_TM
    tile_expert = jnp.minimum(jnp.sum(seg_end[None, :] <= tile_row0[:, None], axis=1), N_EXPERTS - 1)
    n_tiles = (seg_end[-1] // MOE_TM).reshape(1)
    last = MOE_ROWS - MOE_TM
    fill_rows = jnp.concatenate([jnp.minimum(seg_start + total, last),
                                 last - jnp.arange(N_EXPERTS, dtype=jnp.int32) * MOE_TM])
    return pos, tile_expert.astype(jnp.int32), n_tiles.astype(jnp.int32), fill_rows.astype(jnp.int32)


def _row_copy_wait(src_hbm, dst, sem, n_rows):
    pltpu.make_async_copy(src_hbm.at[pl.ds(0, n_rows)], dst.at[pl.ds(0, n_rows)], sem).wait()


def _dispatch_kernel(pos_ref, fill_ref, h_ref, xs_ref, zero_ref, sem, fill_sem):
    tm = h_ref.shape[0]
    t0 = pl.program_id(0) * tm

    @pl.when(pl.program_id(0) == 0)
    def _():
        zero_ref[...] = jnp.zeros_like(zero_ref)
        fills = [pltpu.make_async_copy(zero_ref, xs_ref.at[pl.ds(fill_ref[j], MOE_TM)], fill_sem)
                 for j in range(2 * N_EXPERTS)]
        for cp in fills[N_EXPERTS:]:
            cp.start()
        for cp in fills[N_EXPERTS:]:
            cp.wait()
        for cp in fills[:N_EXPERTS]:
            cp.start()
            cp.wait()

    def body(r, carry):
        for k in range(2):
            pltpu.make_async_copy(h_ref.at[r], xs_ref.at[pos_ref[k * N_TOK + t0 + r]], sem).start()
        return carry

    lax.fori_loop(0, tm, body, 0, unroll=8)
    for _ in range(2):
        _row_copy_wait(h_ref, xs_ref, sem, tm)


def _moe_dispatch(pos, fill_rows, h2_tiles):
    tm = DISPATCH_TM
    return pl.pallas_call(
        _dispatch_kernel,
        out_shape=jax.ShapeDtypeStruct((MOE_ROWS, N_CHUNKS, LANES), F32),
        grid_spec=pltpu.PrefetchScalarGridSpec(
            num_scalar_prefetch=2,
            grid=(N_TOK // tm,),
            in_specs=[pl.BlockSpec((tm, N_CHUNKS, LANES), lambda i, pos, fill: (i, 0, 0))],
            out_specs=pl.BlockSpec(memory_space=pl.ANY),
            scratch_shapes=[pltpu.VMEM((MOE_TM, N_CHUNKS, LANES), F32), pltpu.SemaphoreType.DMA,
                            pltpu.SemaphoreType.DMA],
        ),
        compiler_params=_params("arbitrary"),
        name="moe_dispatch",
    )(pos, fill_rows, h2_tiles)


def _expert_kernel(te_ref, nt_ref, xs_ref, wg_ref, wu_ref, wd_ref, ys_ref, wg16_ref, wu16_ref, wd16_ref):
    i = pl.program_id(0)

    @pl.when(i < nt_ref[0])
    def _():
        @pl.when((i == 0) | (te_ref[i] != te_ref[jnp.maximum(i - 1, 0)]))
        def _():
            wg16_ref[...] = wg_ref[0, 0].astype(BF16)
            wu16_ref[...] = wu_ref[0, 0].astype(BF16)
            wd16_ref[...] = wd_ref[0, 0].astype(BF16)

        x = _load_token_tiles(xs_ref).astype(BF16)
        he = _silu(_dot(x, wg16_ref[...])) * _dot(x, wu16_ref[...])
        _store_token_tiles(ys_ref, _dot(he.astype(BF16), wd16_ref[...]))

    @pl.when(i >= nt_ref[0])
    def _():
        ys_ref[...] = jnp.zeros_like(ys_ref)


def _moe_experts(tile_expert, n_tiles, xs, layer, wg, wu, wd):
    row_blk = lambda i, te, nt: (jnp.minimum(i, nt[0] - 1), 0, 0)
    w_blk = lambda i, te, nt: (layer, te[i], 0, 0)
    return pl.pallas_call(
        _expert_kernel,
        out_shape=jax.ShapeDtypeStruct((MOE_ROWS, N_CHUNKS, LANES), F32),
        grid_spec=pltpu.PrefetchScalarGridSpec(
            num_scalar_prefetch=2,
            grid=(MOE_TILES,),
            in_specs=[
                pl.BlockSpec((MOE_TM, N_CHUNKS, LANES), row_blk),
                pl.BlockSpec((1, 1, D_MODEL, EXPERT_FF), w_blk),
                pl.BlockSpec((1, 1, D_MODEL, EXPERT_FF), w_blk),
                pl.BlockSpec((1, 1, EXPERT_FF, D_MODEL), w_blk),
            ],
            out_specs=pl.BlockSpec((MOE_TM, N_CHUNKS, LANES), lambda i, te, nt: (i, 0, 0)),
            scratch_shapes=[pltpu.VMEM((D_MODEL, EXPERT_FF), BF16), pltpu.VMEM((D_MODEL, EXPERT_FF), BF16),
                            pltpu.VMEM((EXPERT_FF, D_MODEL), BF16)],
        ),
        compiler_params=_params("arbitrary"),
        name="moe_experts",
    )(tile_expert, n_tiles, xs, wg, wu, wd)


def _combine_kernel(pos_ref, gw_ref, x1_ref, mod_ref, lng_ref, lnb_ref, ys_ref, *rest, split):
    out_refs, (buf_ref, sem) = rest[:-2], rest[-2:]
    i = pl.program_id(0)
    n = pl.num_programs(0)
    tm = COMBINE_TM

    def start_row(tile, to_slot, r):
        for k in range(2):
            p = pos_ref[k * N_TOK + tile * tm + r]
            pltpu.make_async_copy(ys_ref.at[p], buf_ref.at[to_slot, k, r], sem.at[to_slot]).start()

    def wait_slot(s):
        for k in range(2):
            _row_copy_wait(ys_ref, buf_ref.at[s, k], sem.at[s], tm)

    def issue(tile, to_slot):
        def body(r, carry):
            start_row(tile, to_slot, r)
            return carry
        lax.fori_loop(0, tm, body, 0, unroll=8)

    @pl.when(i == 0)
    def _():
        issue(0, 0)

    @pl.when(i + 1 < n)
    def _():
        issue(i + 1, (i + 1) % 2)

    slot = i % 2
    wait_slot(slot)
    gw = gw_ref[...]
    f = gw[:, 0:1] * _load_token_tiles(buf_ref.at[slot, 0]) + gw[:, 1:2] * _load_token_tiles(buf_ref.at[slot, 1])
    m = mod_ref[0]
    out = _layer_norm(ALPHA * x1_ref[...] + m[5:6] * f, lng_ref[...], lnb_ref[...])

    if split:
        prompt_ref, sample_ref = out_refs

        @pl.when(i < N_PROMPT // tm)
        def _():
            prompt_ref[...] = out

        @pl.when(i >= N_PROMPT // tm)
        def _():
            sample_ref[...] = out
    else:
        out_refs[0][...] = out


def _moe_combine(pos, gate_w, x1, mod, ln_g, ln_b, ys, split):
    tm = COMBINE_TM
    np_blk = N_PROMPT // tm
    if split:
        out_shape = (jax.ShapeDtypeStruct((N_PROMPT, D_MODEL), F32), jax.ShapeDtypeStruct((N_SAMPLE, D_MODEL), F32))
        out_specs = (pl.BlockSpec((tm, D_MODEL), lambda i, pos: (jnp.minimum(i, np_blk - 1), 0)),
                     pl.BlockSpec((tm, D_MODEL), lambda i, pos: (jnp.maximum(i - np_blk, 0), 0)))
    else:
        out_shape = jax.ShapeDtypeStruct((N_TOK, D_MODEL), F32)
        out_specs = pl.BlockSpec((tm, D_MODEL), lambda i, pos: (i, 0))
    return pl.pallas_call(
        functools.partial(_combine_kernel, split=split),
        out_shape=out_shape,
        grid_spec=pltpu.PrefetchScalarGridSpec(
            num_scalar_prefetch=1,
            grid=(N_TOK // tm,),
            in_specs=[
                pl.BlockSpec((tm, LANES), lambda i, pos: (i, 0)),
                pl.BlockSpec((tm, D_MODEL), lambda i, pos: (i, 0)),
                pl.BlockSpec((1, 6, D_MODEL), lambda i, pos: (_cond_row(i * tm), 0, 0)),
                pl.BlockSpec((1, D_MODEL), lambda i, pos: (0, 0)),
                pl.BlockSpec((1, D_MODEL), lambda i, pos: (0, 0)),
                pl.BlockSpec(memory_space=pl.ANY),
            ],
            out_specs=out_specs,
            scratch_shapes=[pltpu.VMEM((2, 2, tm, N_CHUNKS, LANES), F32), pltpu.SemaphoreType.DMA((2,))],
        ),
        compiler_params=_params("arbitrary"),
        name="moe_combine",
    )(pos, gate_w, x1, mod, ln_g.reshape(1, D_MODEL), ln_b.reshape(1, D_MODEL), ys)


def _moe_ffn(h2_tiles, route_i, gate_w, counts, layer, wg, wu, wd, x1, mod, ln_g, ln_b, split=False):
    pos, tile_expert, n_tiles, fill_rows = _moe_plan(route_i, counts)
    xs = _moe_dispatch(pos, fill_rows, h2_tiles)
    ys = _moe_experts(tile_expert, n_tiles, xs, layer, wg, wu, wd)
    return _moe_combine(pos, gate_w, x1, mod, ln_g, ln_b, ys, split)


DN_GATE_COL0 = DN_CONV_CH
N_DIRS = 2


def _dn_prep_kernel(q_ref, k_ref, v_ref, wq_ref, wk_ref, wv_ref, qo_ref, ko_ref, vo_ref):
    length = q_ref.shape[0]
    row = lax.broadcasted_iota(jnp.int32, (length, DN_DK), 0)

    def conv(x, w):
        prev = jnp.where(row == 0, 0.0, pltpu.roll(x, 1, axis=0))
        nxt = jnp.where(row == length - 1, 0.0, pltpu.roll(x, length - 1, axis=0))
        return _silu(prev * w[0:1] + x * w[1:2] + nxt * w[2:3])

    def l2n(x):
        return x * lax.rsqrt(jnp.sum(x * x, axis=-1, keepdims=True) + 1e-6)

    for h in range(DN_HEADS):
        sl = slice(h * DN_DK, (h + 1) * DN_DK)
        q, k, v = (r[:, sl].astype(F32) for r in (q_ref, k_ref, v_ref))
        qo_ref[h] = (l2n(conv(q, wq_ref[:, sl])) * (DN_DK ** -0.5)).astype(qo_ref.dtype)
        ko_ref[h] = l2n(conv(k, wk_ref[:, sl])).astype(ko_ref.dtype)
        vo_ref[h] = conv(v, wv_ref[:, sl]).astype(vo_ref.dtype)


def _dn_prep(z1, conv_w, nb, length, row_blk0):
    shp = jax.ShapeDtypeStruct((DN_HEADS, nb * length, DN_DK), BF16)
    blk = lambda part: pl.BlockSpec((length, DN_QK_WIDTH), lambda b: (row_blk0 + b, part))
    wblk = lambda part: pl.BlockSpec((3, DN_QK_WIDTH), lambda b: (0, part))
    oblk = pl.BlockSpec((DN_HEADS, length, DN_DK), lambda b: (0, b, 0))
    return pl.pallas_call(
        _dn_prep_kernel,
        out_shape=(shp, shp, shp),
        grid=(nb,),
        in_specs=[blk(0), blk(1), blk(2), wblk(0), wblk(1), wblk(2)],
        out_specs=(oblk, oblk, oblk),
        compiler_params=_params("arbitrary"),
        name="deltanet_prep",
    )(z1, z1, z1, conv_w, conv_w, conv_w)


def _dn_gates_kernel(ab_ref, alog_ref, dtb_ref, o_ref):
    length = ab_ref.shape[0]
    x = ab_ref[...]
    a = x + dtb_ref[...]
    softplus = jnp.maximum(a, 0.0) + jnp.log1p(jnp.exp(-jnp.abs(a)))
    g = -jnp.exp(alog_ref[...]) * softplus
    beta = jax.nn.sigmoid(x)
    ri = lax.broadcasted_iota(jnp.int32, (DN_CHUNK, DN_CHUNK), 0)
    ci = lax.broadcasted_iota(jnp.int32, (DN_CHUNK, DN_CHUNK), 1)
    tril = (ri >= ci).astype(F32)
    triu = (ri <= ci).astype(F32)
    lane = lax.broadcasted_iota(jnp.int32, (DN_CHUNK, LANES), 1)
    for c in range(length // DN_CHUNK):
        rows = slice(c * DN_CHUNK, (c + 1) * DN_CHUNK)
        gch = g[rows]
        pre = jnp.dot(tril, gch, precision=lax.Precision.HIGHEST, preferred_element_type=F32)
        suf = jnp.dot(triu, gch, precision=lax.Precision.HIGHEST, preferred_element_type=F32)
        gc = jnp.where(lane < DN_HEADS, pre, suf)
        o_ref[rows, :] = jnp.where(lane < N_DIRS * DN_HEADS, gc, beta[rows])


def _dn_gates(z_ab, a_log, dt_bias, nb, length, row_blk0):
    pad = LANES - N_DIRS * DN_HEADS
    alog = jnp.pad(a_log.reshape(1, -1).astype(F32), ((0, 0), (0, pad)))
    dtb = jnp.pad(dt_bias.reshape(1, -1).astype(F32), ((0, 0), (0, pad)))
    return pl.pallas_call(
        _dn_gates_kernel,
        out_shape=jax.ShapeDtypeStruct((nb * length, LANES), F32),
        grid=(nb,),
        in_specs=[
            pl.BlockSpec((length, LANES), lambda b: (row_blk0 + b, 0)),
            pl.BlockSpec((1, LANES), lambda b: (0, 0)),
            pl.BlockSpec((1, LANES), lambda b: (0, 0)),
        ],
        out_specs=pl.BlockSpec((length, LANES), lambda b: (b, 0)),
        compiler_params=_params("arbitrary"),
        name="deltanet_gates",
    )(z_ab, alog, dtb)


def _delta_chunk(q, k, v, beta, gcol, grow, s, lower):
    ri = lax.broadcasted_iota(jnp.int32, (DN_CHUNK, DN_CHUNK), 0)
    ci = lax.broadcasted_iota(jnp.int32, (DN_CHUNK, DN_CHUNK), 1)
    incl = (ri >= ci) if lower else (ri <= ci)
    strict = (ri > ci) if lower else (ri < ci)
    decay = jnp.exp(jnp.where(incl, gcol - grow, -jnp.inf))
    kb = k * beta
    kq = _dot_nt(jnp.concatenate([kb, q], axis=0).astype(BF16), k.astype(BF16))
    kk, a_qk = kq[:DN_CHUNK], kq[DN_CHUNK:] * decay
    yield
    tri_l = jnp.where(strict, kk * decay, 0.0)

    def off_block(size):
        same = (ri // (2 * size)) == (ci // (2 * size))
        rpar, cpar = (ri // size) % 2, (ci // size) % 2
        return same & ((rpar == 1) & (cpar == 0) if lower else (rpar == 0) & (cpar == 1))

    p = jnp.where(ri == ci, 1.0, 0.0) - jnp.where(off_block(1), tri_l, 0.0)
    size = 2
    while size < DN_CHUNK:
        p16 = p.astype(BF16)
        pc = _dot(p16, jnp.where(off_block(size), tri_l, 0.0).astype(BF16))
        yield
        p = p - _dot(pc.astype(BF16), p16)
        yield
        size *= 2
    eg = jnp.exp(gcol)
    uw = _dot(p.astype(BF16), jnp.concatenate([v * beta, kb * eg], axis=1).astype(BF16))
    yield
    u, w = uw[:, :DN_DV], uw[:, DN_DV:]
    ws = _dot(jnp.concatenate([w, q * eg], axis=0).astype(BF16), s.astype(BF16))
    v_new, o_inter = u - ws[:DN_CHUNK], ws[DN_CHUNK:]
    yield
    v_new16 = v_new.astype(BF16)
    o = o_inter + _dot(a_qk.astype(BF16), v_new16)
    g_last = gcol[DN_CHUNK - 1:DN_CHUNK] if lower else gcol[0:1]
    kd = k * jnp.exp(g_last - gcol)
    s_new = s * jnp.exp(g_last) + _dot_tn(kd.astype(BF16), v_new16)
    return o, s_new


def _run_interleaved(chains):
    results = [None] * len(chains)
    active = list(enumerate(chains))
    while active:
        still = []
        for idx, gen in active:
            try:
                next(gen)
                still.append((idx, gen))
            except StopIteration as stop:
                results[idx] = stop.value
        active = still
    return results


def _dn_scan_kernel(*refs, has_init, want_final):
    (qf_ref, kf_ref, vf_ref, gf_ref, qb_ref, kb_ref, vb_ref, gb_ref), rest = refs[:8], refs[8:]
    if has_init:
        s0_ref, rest = rest[0], rest[1:]
    of_ref, ob_ref = rest[0], rest[1]
    rest = rest[2:]
    if want_final:
        sf_ref, rest = rest[0], rest[1:]
    s_ref = rest[0]
    c = pl.program_id(1)
    nc = pl.num_programs(1)
    per_request = N_DIRS * DN_HEADS
    states = [(u, d, h) for u in range(DN_SCAN_BATCH) for d in range(N_DIRS) for h in range(DN_HEADS)]

    @pl.when(c == 0)
    def _():
        for j, (u, d, h) in enumerate(states):
            s_ref[j] = s0_ref[u, d, h] if has_init else jnp.zeros((DN_DK, DN_DV), F32)

    dirs = ((qf_ref, kf_ref, vf_ref, gf_ref, of_ref, True), (qb_ref, kb_ref, vb_ref, gb_ref, ob_ref, False))
    chains = []
    for u in range(DN_SCAN_BATCH):
        for d, (q_ref, k_ref, v_ref, g_ref, o_ref, lower) in enumerate(dirs):
            gates = g_ref[u]
            gates_t = gates.T
            for h in range(DN_HEADS):
                j = d * DN_HEADS + h
                jb = per_request + j
                chains.append(_delta_chunk(q_ref[h, u].astype(F32), k_ref[h, u].astype(F32),
                                           v_ref[h, u].astype(F32), gates[:, jb:jb + 1], gates[:, j:j + 1],
                                           gates_t[j:j + 1, :], s_ref[u * per_request + j], lower))
    results = _run_interleaved(chains)
    for j, (u, d, h) in enumerate(states):
        o, s_new = results[j]
        dirs[d][4][u, :, h * DN_DV:(h + 1) * DN_DV] = o.astype(of_ref.dtype)
        s_ref[j] = s_new

    if want_final:
        @pl.when(c == nc - 1)
        def _():
            for j, (u, d, h) in enumerate(states):
                sf_ref[u, d, h] = s_ref[j]


def _dn_scan(qn, kn, vn, gates, s0, nb, length, want_final):
    nc = length // DN_CHUNK
    bb = DN_SCAN_BATCH
    has_init = s0 is not None
    qn, kn, vn = (t.reshape(DN_HEADS, nb, length, DN_DK) for t in (qn, kn, vn))
    gates = gates.reshape(nb, length, LANES)
    fwd = lambda b, c: (b, c, 0)
    bwd = lambda b, c: (b, nc - 1 - c, 0)
    hm = lambda im: pl.BlockSpec((DN_HEADS, bb, DN_CHUNK, DN_DK), lambda b, c: (0,) + im(b, c))
    gm = lambda im: pl.BlockSpec((bb, DN_CHUNK, LANES), im)
    in_specs = [hm(fwd), hm(fwd), hm(fwd), gm(fwd), hm(bwd), hm(bwd), hm(bwd), gm(bwd)]
    args = [qn, kn, vn, gates, qn, kn, vn, gates]
    state_blk = pl.BlockSpec((bb, N_DIRS, DN_HEADS, DN_DK, DN_DV), lambda b, c: (b, 0, 0, 0, 0))
    if has_init:
        in_specs.append(state_blk)
        args.append(s0)
    o_shape = jax.ShapeDtypeStruct((nb, length, DN_WIDTH), BF16)
    out_shape = [o_shape, o_shape]
    out_specs = [pl.BlockSpec((bb, DN_CHUNK, DN_WIDTH), fwd), pl.BlockSpec((bb, DN_CHUNK, DN_WIDTH), bwd)]
    if want_final:
        out_shape.append(jax.ShapeDtypeStruct((nb, N_DIRS, DN_HEADS, DN_DK, DN_DV), F32))
        out_specs.append(state_blk)
    outs = pl.pallas_call(
        functools.partial(_dn_scan_kernel, has_init=has_init, want_final=want_final),
        out_shape=tuple(out_shape),
        grid=(nb // bb, nc),
        in_specs=in_specs,
        out_specs=tuple(out_specs),
        scratch_shapes=[pltpu.VMEM((bb * N_DIRS * DN_HEADS, DN_DK, DN_DV), F32)],
        compiler_params=_params("arbitrary", "arbitrary"),
        name="deltanet_scan",
    )(*args)
    return [outs[0].reshape(nb * length, DN_WIDTH), outs[1].reshape(nb * length, DN_WIDTH)] + list(outs[2:])


DN_SCAN_BATCH = 2


def _deltanet_stream(z1, z_ab, conv_w, a_log, dt_bias, s0, nb, length, row0, want_final):
    row_blk0 = row0 // length
    qn, kn, vn = _dn_prep(z1, conv_w, nb, length, row_blk0)
    gates = _dn_gates(z_ab, a_log, dt_bias, nb, length, row_blk0)
    outs = _dn_scan(qn, kn, vn, gates, s0, nb, length, want_final)
    return outs[0], outs[1], (outs[2] if want_final else None)


def kernel(x_prompt, x_sample, cache_nat_k, cache_nat_v, cache_diff_k, cache_diff_v, state_delta, c, c_ctx, w_in_even, w_out_even, nat_bias, diff_lam, diff_subln, w_in_odd, conv_odd, a_log_odd, dt_bias_odd, onorm_odd, w_out_odd, ada_w, ada_b, ln_g, ln_b, router_w, router_b, moe_wg, moe_wu, moe_wd):
    x_parts = (x_prompt.reshape(N_PROMPT, D_MODEL), x_sample.reshape(N_SAMPLE, D_MODEL))
    cond =jnp.concatenate([c_ctx[None], c, jnp.zeros((N_COND - 1 - DEC_BATCH, D_MODEL), F32)], axis=0)
    mod = _ada_modulation(cond, ada_w, ada_b).reshape(DEPTH, N_COND, 6, D_MODEL)
    router_w_pad = jnp.pad(router_w, ((0, 0), (0, LANES - N_EXPERTS)))

    lam_init = 0.8 - 0.6 * math.exp(-0.3 * 0)
    z = _input_projection(x_parts, mod[0], w_in_even[0].astype(BF16))
    new_nat_k = z[:N_PROMPT, _NAT_K0:_NAT_V0].reshape(BATCH, 1, SEQ, NAT_HEADS, HEAD_DIM)
    new_nat_v = z[:N_PROMPT, _NAT_V0:_DIFF_Q0].reshape(BATCH, 1, SEQ, NAT_HEADS, HEAD_DIM)
    new_diff_k = z[:N_PROMPT, _DIFF_K0:_DIFF_V0].reshape(BATCH, 1, SEQ, DIFF_HEADS, 2, HEAD_DIM)
    new_diff_v = z[:N_PROMPT, _DIFF_V0:].reshape(BATCH, 1, SEQ, DIFF_HEADS, 2 * HEAD_DIM)
    o_ctx = _context_attention(z, diff_lam[0], diff_subln[0], lam_init)
    o_nat = _nat_latent_attention(z, cache_nat_k[:, 0].reshape(DEC_BATCH, PAST_LEN, NAT_WIDTH),
                                  cache_nat_v[:, 0].reshape(DEC_BATCH, PAST_LEN, NAT_WIDTH),
                                  _nat_bias_diagonals(nat_bias[0]))
    o_diff = _diff_latent_attention(z, cache_diff_k[:, 0].reshape(DEC_BATCH, PAST_LEN, DIFF_WIDTH),
                                    cache_diff_v[:, 0].reshape(DEC_BATCH, PAST_LEN, DIFF_WIDTH),
                                    diff_lam[0], diff_subln[0], lam_init)
    x1, *routed = _output_projection((o_ctx, o_nat, o_diff), x_parts, mod[0], w_out_even[0].astype(BF16),
                                     ln_g[0, 0], ln_b[0, 0], router_w_pad, router_b)
    x = _moe_ffn(*routed, 0, moe_wg, moe_wu, moe_wd, x1, mod[0], ln_g[0, 1], ln_b[0, 1])

    w1 = w_in_odd[0]
    w1 = jnp.concatenate([w1[:, :DN_CONV_CH], w1[:, DN_CONV_CH + 4 * DN_HEADS:],
                          w1[:, DN_CONV_CH:DN_CONV_CH + 4 * DN_HEADS],
                          jnp.zeros((D_MODEL, LANES - 4 * DN_HEADS), F32)], axis=1).astype(BF16)
    z1, z_ab = _input_projection((x,), mod[1], w1, f32_tail=LANES)
    of_p, ob_p, new_state = _deltanet_stream(z1, z_ab, conv_odd[0], a_log_odd[0], dt_bias_odd[0],
                                             None, BATCH, SEQ, 0, True)
    of_s, ob_s, _ = _deltanet_stream(z1, z_ab, conv_odd[0], a_log_odd[0], dt_bias_odd[0],
                                     state_delta[:, 0], DEC_BATCH, DEC_SEQ, N_PROMPT, False)
    x1, *routed = _output_projection((of_p, of_s, ob_p, ob_s), (x,), mod[1], w_out_odd[0].astype(BF16),
                                     ln_g[1, 0], ln_b[1, 0], router_w_pad, router_b,
                                     deltanet=(z1, onorm_odd[0]))
    y_prompt, y_sample = _moe_ffn(*routed, 1, moe_wg, moe_wu, moe_wd, x1, mod[1], ln_g[1, 1], ln_b[1, 1],
                                  split=True)
    y_prompt = y_prompt.reshape(BATCH, SEQ, D_MODEL)
    y_sample = y_sample.reshape(DEC_BATCH, DEC_SEQ, D_MODEL)
    return (y_prompt, y_sample, new_nat_k, new_nat_v, new_diff_k, new_diff_v, new_state[:, None])
```

```python
import functools
import math

import jax
import jax.numpy as jnp
import numpy as np
from jax import lax
from jax.experimental import pallas as pl
from jax.experimental.pallas import tpu as pltpu

F32 = jnp.float32
BF16 = jnp.bfloat16

D_MODEL = 1024
BATCH = 16
SEQ = 256
DEC_BATCH = 8
DEC_SEQ = 1024
PAST_LEN = 256
GRID_W = 64
N_ROWS = DEC_SEQ // GRID_W

HEAD_DIM = 64
NAT_HEADS = 8
NAT_WIN_H = 8
NAT_WIN_W = 16
DIFF_HEADS = 4
NAT_WIDTH = NAT_HEADS * HEAD_DIM
DIFF_WIDTH = DIFF_HEADS * 2 * HEAD_DIM
EVEN_IN = 3 * NAT_WIDTH + 3 * DIFF_WIDTH
ROPE_THETA = 10000.0

DN_HEADS = 8
DN_DK = 128
DN_DV = 128
DN_QK_WIDTH = DN_HEADS * DN_DK
DN_WIDTH = DN_HEADS * DN_DV
DN_CONV_CH = 2 * DN_QK_WIDTH + DN_WIDTH
DN_CHUNK = 64

N_EXPERTS = 16
N_GROUPS = 4
EXPERTS_PER_GROUP = N_EXPERTS // N_GROUPS
EXPERT_FF = 512

DEPTH = 2
ALPHA = (2 * DEPTH) ** 0.25
EPS = 1e-5

N_PROMPT = BATCH * SEQ
N_SAMPLE = DEC_BATCH * DEC_SEQ
N_TOK = N_PROMPT + N_SAMPLE
N_COND = 16

VMEM_LIMIT = 56 * 1024 * 1024
LANES = 128
SUBLANES = 8
N_CHUNKS = D_MODEL // LANES
assert N_CHUNKS == SUBLANES


def _params(*sem):
    return pltpu.CompilerParams(dimension_semantics=sem, vmem_limit_bytes=VMEM_LIMIT)


def _cond_row(row0):
    return jnp.where(row0 < N_PROMPT, 0, 1 + (row0 - N_PROMPT) // DEC_SEQ)


def _silu(x):
    return x * jax.nn.sigmoid(x)


def _layer_norm(r, g, b):
    mu = jnp.mean(r, axis=-1, keepdims=True)
    xc = r - mu
    var = jnp.mean(xc * xc, axis=-1, keepdims=True)
    return xc * lax.rsqrt(var + EPS) * g + b


def _dot(a, b):
    return jnp.dot(a, b, preferred_element_type=F32)


def _dot_nt(a, b):
    return lax.dot_general(a, b, (((1,), (1,)), ((), ())), preferred_element_type=F32)


def _dot_tn(a, b):
    return lax.dot_general(a, b, (((0,), (0,)), ((), ())), preferred_element_type=F32)


ADA_TN = 1536


def _ada_kernel(c_ref, w_ref, b_ref, o_ref):
    s = _silu(c_ref[...])
    o_ref[0] = _dot(s.astype(BF16), w_ref[0].astype(BF16)) + b_ref[0]


def _ada_modulation(cond, ada_w, ada_b):
    n = 6 * D_MODEL
    return pl.pallas_call(
        _ada_kernel,
        out_shape=jax.ShapeDtypeStruct((DEPTH, N_COND, n), F32),
        grid=(DEPTH, n // ADA_TN),
        in_specs=[
            pl.BlockSpec((N_COND, D_MODEL), lambda l, j: (0, 0)),
            pl.BlockSpec((1, D_MODEL, ADA_TN), lambda l, j: (l, 0, j)),
            pl.BlockSpec((1, 1, ADA_TN), lambda l, j: (l, 0, j)),
        ],
        out_specs=pl.BlockSpec((1, N_COND, ADA_TN), lambda l, j: (l, 0, j)),
        compiler_params=_params("arbitrary", "arbitrary"),
        name="ada_modulation",
    )(cond, ada_w, ada_b.reshape(DEPTH, 1, n))


PROJ_TM = 512


def _stream_specs(widths, tm):
    if len(widths) == 1:
        return [pl.BlockSpec((tm, widths[0]), lambda i: (i, 0))]
    n_ctx = N_PROMPT // tm
    return ([pl.BlockSpec((tm, widths[0]), lambda i: (jnp.minimum(i, n_ctx - 1), 0))]
            + [pl.BlockSpec((tm, w), lambda i: (jnp.maximum(i - n_ctx, 0), 0)) for w in widths[1:]])


def _stream_tile(refs, tm):
    if len(refs) == 1:
        return refs[0][...]
    latent = [r[...] for r in refs[1:]]
    latent = latent[0] if len(latent) == 1 else jnp.concatenate(latent, axis=1)
    return jnp.where(pl.program_id(0) * tm < N_PROMPT, refs[0][...], latent)


def _inproj_kernel(*refs, n_x):
    x_refs, (mod_ref, w_ref, o_ref), tail_refs = refs[:n_x], refs[n_x:n_x + 3], refs[n_x + 3:]
    m = mod_ref[0]
    h = _stream_tile(x_refs, o_ref.shape[0]) * (1.0 + m[1:2]) + m[0:1]
    z = _dot(h.astype(BF16), w_ref[...])
    n_main = o_ref.shape[1]
    o_ref[...] = z[:, :n_main].astype(o_ref.dtype)
    if tail_refs:
        tail_refs[0][...] = z[:, n_main:]


def _input_projection(x_parts, mod, w_bf16, tm=PROJ_TM, f32_tail=0):
    n = w_bf16.shape[1] - f32_tail
    out_shape = [jax.ShapeDtypeStruct((N_TOK, n), BF16 if f32_tail else F32)]
    out_specs = [pl.BlockSpec((tm, n), lambda i: (i, 0))]
    if f32_tail:
        out_shape.append(jax.ShapeDtypeStruct((N_TOK, f32_tail), F32))
        out_specs.append(pl.BlockSpec((tm, f32_tail), lambda i: (i, 0)))
    outs = pl.pallas_call(
        functools.partial(_inproj_kernel, n_x=len(x_parts)),
        out_shape=tuple(out_shape),
        grid=(N_TOK // tm,),
        in_specs=_stream_specs([a.shape[1] for a in x_parts], tm) + [
            pl.BlockSpec((1, 6, D_MODEL), lambda i: (_cond_row(i * tm), 0, 0)),
            pl.BlockSpec((D_MODEL, n + f32_tail), lambda i: (0, 0)),
        ],
        out_specs=tuple(out_specs),
        compiler_params=_params("arbitrary"),
        name="input_projection",
    )(*x_parts, mod, w_bf16)
    return outs if f32_tail else outs[0]


_NAT_Q0, _NAT_K0, _NAT_V0 = 0, NAT_WIDTH, 2 * NAT_WIDTH
_DIFF_Q0 = 3 * NAT_WIDTH
_DIFF_K0 = _DIFF_Q0 + DIFF_WIDTH
_DIFF_V0 = _DIFF_K0 + DIFF_WIDTH
ATTN_SCALE = HEAD_DIM ** -0.5


def _diff_lambda(lam_ref, lam_init):
    lp = lam_ref[...]
    return (jnp.exp(jnp.sum(lp[0:1] * lp[1:2], axis=-1, keepdims=True))
            - jnp.exp(jnp.sum(lp[2:3] * lp[3:4], axis=-1, keepdims=True)) + lam_init)


def _scaled_q(q):
    assert math.log2(HEAD_DIM) % 2 == 0
    return (q * ATTN_SCALE).astype(BF16)


def _with_ones(v):
    width = v.shape[1] if v.shape[1] % LANES == 0 else LANES - v.shape[1] % LANES
    return jnp.concatenate([v, jnp.ones((v.shape[0], width), v.dtype)], axis=1)


def _sub_norm(o, g, lam_init):
    ms = jnp.mean(o * o, axis=-1, keepdims=True)
    return o * lax.rsqrt(ms + EPS) * g * (1.0 - lam_init)


def _ctx_attn_kernel(z_ref, lam_ref, subg_ref, o_ref, *, lam_init):
    lam = _diff_lambda(lam_ref, lam_init)

    def softmax_av(q0, k0, v0, v_width):
        q = _scaled_q(z_ref[:, q0:q0 + HEAD_DIM])
        k = z_ref[:, k0:k0 + HEAD_DIM].astype(BF16)
        s = _dot_nt(q, k)
        yield
        e = jnp.exp(s - jnp.max(s, axis=-1, keepdims=True))
        yield
        av = _dot(e.astype(BF16), _with_ones(z_ref[:, v0:v0 + v_width].astype(BF16)))
        yield
        return av[:, :v_width] / av[:, v_width:v_width + 1]

    chains = [softmax_av(_NAT_Q0 + h * HEAD_DIM, _NAT_K0 + h * HEAD_DIM, _NAT_V0 + h * HEAD_DIM, HEAD_DIM)
              for h in range(NAT_HEADS)]
    chains += [softmax_av(_DIFF_Q0 + i * HEAD_DIM, _DIFF_K0 + i * HEAD_DIM,
                          _DIFF_V0 + (i // 2) * 2 * HEAD_DIM, 2 * HEAD_DIM)
               for i in range(2 * DIFF_HEADS)]
    outs = _run_interleaved(chains)
    for h in range(NAT_HEADS):
        o_ref[:, h * HEAD_DIM:(h + 1) * HEAD_DIM] = outs[h].astype(o_ref.dtype)
    for h in range(DIFF_HEADS):
        c = NAT_WIDTH + h * 2 * HEAD_DIM
        o = outs[NAT_HEADS + 2 * h] - lam * outs[NAT_HEADS + 2 * h + 1]
        o_ref[:, c:c + 2 * HEAD_DIM] = _sub_norm(o, subg_ref[...], lam_init).astype(o_ref.dtype)


def _context_attention(z, lam_p, subln_g, lam_init):
    return pl.pallas_call(
        functools.partial(_ctx_attn_kernel, lam_init=lam_init),
        out_shape=jax.ShapeDtypeStruct((N_PROMPT, D_MODEL), BF16),
        grid=(BATCH,),
        in_specs=[
            pl.BlockSpec((SEQ, EVEN_IN), lambda b: (b, 0)),
            pl.BlockSpec((4, HEAD_DIM), lambda b: (0, 0)),
            pl.BlockSpec((1, 2 * HEAD_DIM), lambda b: (0, 0)),
        ],
        out_specs=pl.BlockSpec((SEQ, D_MODEL), lambda b: (b, 0)),
        compiler_params=_params("arbitrary"),
        name="context_attention",
    )(z, lam_p, subln_g.reshape(1, 2 * HEAD_DIM))


NAT_TQ = 512
NAT_GROUP = 4
NAT_KEYS = 12 * GRID_W
_SAMPLE_BLK0 = N_PROMPT // DEC_SEQ


NAT_SPAN = NAT_KEYS // GRID_W
NAT_SPAN_PAD = 16


def _nat_bias_diagonals(rel_bias):
    rows = N_ROWS
    kh = min(NAT_WIN_H, rows)
    row_onehot = np.zeros((2 * NAT_WIN_H - 1, rows, NAT_SPAN), np.float32)
    for r in range(rows):
        start = min(max(r - kh // 2, 0), rows - kh)
        a0 = _nat_key0(r * GRID_W // NAT_TQ) // GRID_W
        assert a0 <= start and start + kh <= a0 + NAT_SPAN
        for a in range(start, start + kh):
            row_onehot[a - r + NAT_WIN_H - 1, r, a - a0] = 1.0
    lane_onehot = np.zeros((2 * NAT_WIN_W - 1, LANES), np.float32)
    for lane in range(2 * GRID_W - 1):
        lane_onehot[np.clip(lane - (GRID_W - 1), 1 - NAT_WIN_W, NAT_WIN_W - 1) + NAT_WIN_W - 1, lane] = 1.0
    hi = lax.Precision.HIGHEST
    by_row = jnp.einsum('hed,era->hrad', rel_bias.astype(F32), row_onehot, precision=hi)
    diag = jnp.einsum('hrad,dl->hral', by_row, lane_onehot, precision=hi)
    diag = jnp.where((row_onehot.sum(0) > 0)[None, :, :, None], diag, -jnp.inf)
    return jnp.pad(diag, ((0, 0), (0, 0), (0, NAT_SPAN_PAD - NAT_SPAN), (0, 0)))


def _nat_bias_tiles(diag_ref, bias_ref):
    shape = (GRID_W, LANES)
    q = lax.broadcasted_iota(jnp.int32, shape, 0)
    lane = lax.broadcasted_iota(jnp.int32, shape, 1)
    k = lane % GRID_W
    c0 = jnp.clip(q - NAT_WIN_W // 2, 0, GRID_W - NAT_WIN_W)
    in_cols = (k >= c0) & (k < c0 + NAT_WIN_W)
    for j in range(NAT_GROUP):
        for r in range(NAT_TQ // GRID_W):
            for pair in range(NAT_SPAN // 2):
                even = jnp.broadcast_to(diag_ref[j, r, 2 * pair:2 * pair + 1, :], shape)
                odd = jnp.broadcast_to(diag_ref[j, r, 2 * pair + 1:2 * pair + 2, :], shape)
                even = pltpu.roll(even, GRID_W + 1, 1, stride=1, stride_axis=0)
                odd = pltpu.roll(odd, 1, 1, stride=1, stride_axis=0)
                tile = jnp.where(lane < GRID_W, even, odd)
                bias_ref[j, r * GRID_W:(r + 1) * GRID_W, pair * LANES:(pair + 1) * LANES] = (
                    jnp.where(in_cols, tile, -jnp.inf))


def _nat_key0(qt):
    half = DEC_SEQ // NAT_TQ // 2
    return (qt >= half) * (DEC_SEQ - NAT_KEYS)


def _nat_latent_kernel(q_ref, k_ref, v_ref, ck_ref, cv_ref, diag_ref, o_ref, bias_ref):
    key0 = pl.multiple_of(_nat_key0(pl.program_id(1)).astype(jnp.int32), DEC_SEQ - NAT_KEYS)
    keys = pl.ds(key0, NAT_KEYS)

    @pl.when(pl.program_id(2) == 0)
    def _():
        _nat_bias_tiles(diag_ref, bias_ref)

    def head(j):
        sl = slice(j * HEAD_DIM, (j + 1) * HEAD_DIM)
        q = _scaled_q(q_ref[:, sl])
        s_loc = _dot_nt(q, k_ref[keys, sl].astype(BF16)) + bias_ref[j]
        s_ctx = _dot_nt(q, ck_ref[0, :, sl].astype(BF16))
        yield
        m = jnp.maximum(jnp.max(s_loc, axis=-1, keepdims=True), jnp.max(s_ctx, axis=-1, keepdims=True))
        yield
        e_loc = jnp.exp(s_loc - m)
        e_ctx = jnp.exp(s_ctx - m)
        yield
        o = (_dot(e_loc.astype(BF16), _with_ones(v_ref[keys, sl].astype(BF16)))
             + _dot(e_ctx.astype(BF16), _with_ones(cv_ref[0, :, sl].astype(BF16))))
        yield
        return (o[:, :HEAD_DIM] / o[:, HEAD_DIM:HEAD_DIM + 1]).astype(o_ref.dtype)

    outs = _run_interleaved([head(j) for j in range(NAT_GROUP)])
    for j in range(NAT_GROUP):
        o_ref[:, j * HEAD_DIM:(j + 1) * HEAD_DIM] = outs[j]


def _nat_latent_attention(z, ck, cv, diag):
    nq = DEC_SEQ // NAT_TQ
    qblk0 = N_PROMPT // NAT_TQ
    pair = NAT_GROUP * HEAD_DIM
    return pl.pallas_call(
        _nat_latent_kernel,
        out_shape=jax.ShapeDtypeStruct((N_SAMPLE, NAT_WIDTH), BF16),
        grid=(NAT_HEADS // NAT_GROUP, nq, DEC_BATCH),
        in_specs=[
            pl.BlockSpec((NAT_TQ, pair), lambda hp, qt, b: (qblk0 + b * nq + qt, _NAT_Q0 // pair + hp)),
            pl.BlockSpec((DEC_SEQ, pair), lambda hp, qt, b: (_SAMPLE_BLK0 + b, _NAT_K0 // pair + hp)),
            pl.BlockSpec((DEC_SEQ, pair), lambda hp, qt, b: (_SAMPLE_BLK0 + b, _NAT_V0 // pair + hp)),
            pl.BlockSpec((1, PAST_LEN, pair), lambda hp, qt, b: (b, 0, hp)),
            pl.BlockSpec((1, PAST_LEN, pair), lambda hp, qt, b: (b, 0, hp)),
            pl.BlockSpec((NAT_GROUP, NAT_TQ // GRID_W, NAT_SPAN_PAD, LANES), lambda hp, qt, b: (hp, qt, 0, 0)),
        ],
        out_specs=pl.BlockSpec((NAT_TQ, pair), lambda hp, qt, b: (b * nq + qt, hp)),
        scratch_shapes=[pltpu.VMEM((NAT_GROUP, NAT_TQ, NAT_KEYS), F32)],
        compiler_params=_params("arbitrary", "arbitrary", "arbitrary"),
        name="nat_latent_attention",
    )(z, z, z, ck, cv, diag)


DIFF_TQ = 1024


def _rope_tables():
    nf = HEAD_DIM // 4
    t = jnp.arange(DEC_SEQ)
    inv = ROPE_THETA ** (-jnp.arange(nf, dtype=F32) / nf)
    ang_r = (t // GRID_W).astype(F32)[:, None] * inv
    ang_c = (t % GRID_W).astype(F32)[:, None] * inv
    cos = jnp.concatenate([jnp.cos(ang_r)] * 2 + [jnp.cos(ang_c)] * 2, axis=-1)
    sin = jnp.concatenate([-jnp.sin(ang_r), jnp.sin(ang_r), -jnp.sin(ang_c), jnp.sin(ang_c)], axis=-1)
    return jnp.tile(cos, (1, 2)), jnp.tile(sin, (1, 2))


def _rope(x, cos, sin):
    nf = HEAD_DIM // 4
    lane = lax.broadcasted_iota(jnp.int32, x.shape, 1)
    upper = (lane // nf) % 2 == 1
    partner = jnp.where(upper, pltpu.roll(x, nf, axis=1), pltpu.roll(x, x.shape[1] - nf, axis=1))
    return x * cos + partner * sin


def _diff_latent_kernel(q_ref, k_ref, v_ref, ck_ref, cv_ref, cos_ref, sin_ref, lam_ref, subg_ref,
                        o_ref, kr_ref, *, lam_init):
    qt = pl.program_id(2)

    @pl.when(qt == 0)
    def _():
        kr_ref[...] = _rope(k_ref[...], cos_ref[...], sin_ref[...]).astype(BF16)

    lam = _diff_lambda(lam_ref, lam_init)
    row0 = pl.multiple_of(qt * DIFF_TQ, DIFF_TQ)
    q = _scaled_q(_rope(q_ref[...], cos_ref[pl.ds(row0, DIFF_TQ), :], sin_ref[pl.ds(row0, DIFF_TQ), :]))
    v_loc = _with_ones(v_ref[...].astype(BF16))
    v_ctx = _with_ones(cv_ref[0].astype(BF16))
    width = 2 * HEAD_DIM

    def softmax_av(j):
        sl = slice(j * HEAD_DIM, (j + 1) * HEAD_DIM)
        s_ctx = _dot_nt(q[:, sl], ck_ref[0, :, sl].astype(BF16))
        s_loc = _dot_nt(q[:, sl], kr_ref[:, sl])
        yield
        m = jnp.maximum(jnp.max(s_loc, axis=-1, keepdims=True), jnp.max(s_ctx, axis=-1, keepdims=True))
        yield
        e_loc = jnp.exp(s_loc - m)
        e_ctx = jnp.exp(s_ctx - m)
        yield
        av = _dot(e_loc.astype(BF16), v_loc) + _dot(e_ctx.astype(BF16), v_ctx)
        yield
        return av[:, :width] / av[:, width:width + 1]

    parts = _run_interleaved([softmax_av(j) for j in range(2)])
    o = parts[0] - lam * parts[1]
    o_ref[...] = _sub_norm(o, subg_ref[...], lam_init).astype(o_ref.dtype)


def _diff_latent_attention(z, ck, cv, lam_p, subln_g, lam_init):
    nq = DEC_SEQ // DIFF_TQ
    qblk0 = N_PROMPT // DIFF_TQ
    w = 2 * HEAD_DIM
    cos, sin = _rope_tables()
    return pl.pallas_call(
        functools.partial(_diff_latent_kernel, lam_init=lam_init),
        out_shape=jax.ShapeDtypeStruct((N_SAMPLE, DIFF_WIDTH), BF16),
        grid=(DEC_BATCH, DIFF_HEADS, nq),
        in_specs=[
            pl.BlockSpec((DIFF_TQ, w), lambda b, h, qt: (qblk0 + b * nq + qt, _DIFF_Q0 // w + h)),
            pl.BlockSpec((DEC_SEQ, w), lambda b, h, qt: (_SAMPLE_BLK0 + b, _DIFF_K0 // w + h)),
            pl.BlockSpec((DEC_SEQ, w), lambda b, h, qt: (_SAMPLE_BLK0 + b, _DIFF_V0 // w + h)),
            pl.BlockSpec((1, PAST_LEN, w), lambda b, h, qt: (b, 0, h)),
            pl.BlockSpec((1, PAST_LEN, w), lambda b, h, qt: (b, 0, h)),
            pl.BlockSpec((DEC_SEQ, w), lambda b, h, qt: (0, 0)),
            pl.BlockSpec((DEC_SEQ, w), lambda b, h, qt: (0, 0)),
            pl.BlockSpec((4, HEAD_DIM), lambda b, h, qt: (0, 0)),
            pl.BlockSpec((1, w), lambda b, h, qt: (0, 0)),
        ],
        out_specs=pl.BlockSpec((DIFF_TQ, w), lambda b, h, qt: (b * nq + qt, h)),
        scratch_shapes=[pltpu.VMEM((DEC_SEQ, w), BF16)],
        compiler_params=_params("arbitrary", "arbitrary", "arbitrary"),
        name="diff_latent_attention",
    )(z, z, z, ck, cv, cos, sin, lam_p, subln_g.reshape(1, w))


OUT_TM = 256


def _route(logits_t, rb_col):
    sc = jax.nn.sigmoid(logits_t)
    bi = sc + rb_col
    srow = [sc[e:e + 1] for e in range(N_EXPERTS)]
    brow = [bi[e:e + 1] for e in range(N_EXPERTS)]
    n = EXPERTS_PER_GROUP
    gscore = []
    for g in range(N_GROUPS):
        v = brow[g * n:(g + 1) * n]
        best = None
        for i in range(n):
            for j in range(i + 1, n):
                s = v[i] + v[j]
                best = s if best is None else jnp.maximum(best, s)
        gscore.append(best)
    sel = jnp.zeros_like(gscore[0], dtype=jnp.int32)
    best = gscore[0]
    for g in range(1, N_GROUPS):
        better = gscore[g] > best
        sel = jnp.where(better, g, sel)
        best = jnp.where(better, gscore[g], best)

    def pick_group(rows, i):
        out = rows[(N_GROUPS - 1) * n + i]
        for g in range(N_GROUPS - 2, -1, -1):
            out = jnp.where(sel == g, rows[g * n + i], out)
        return out

    bv = [pick_group(brow, i) for i in range(n)]
    sv = [pick_group(srow, i) for i in range(n)]
    i1 = jnp.zeros_like(sel)
    m1, w1 = bv[0], sv[0]
    for i in range(1, n):
        better = bv[i] > m1
        i1 = jnp.where(better, i, i1)
        m1 = jnp.where(better, bv[i], m1)
        w1 = jnp.where(better, sv[i], w1)
    i2 = jnp.zeros_like(sel)
    m2 = jnp.full_like(m1, -jnp.inf)
    w2 = jnp.zeros_like(w1)
    for i in range(n):
        better = (i1 != i) & (bv[i] > m2)
        i2 = jnp.where(better, i, i2)
        m2 = jnp.where(better, bv[i], m2)
        w2 = jnp.where(better, sv[i], w2)
    tot = w1 + w2
    return sel * n + i1, sel * n + i2, w1 / tot, w2 / tot


def _store_token_tiles(ref, value):
    rows = value.shape[0]
    chunks = jnp.stack([value[:, c * LANES:(c + 1) * LANES] for c in range(N_CHUNKS)], axis=0)
    tiles = jnp.transpose(chunks.reshape(N_CHUNKS, rows // SUBLANES, SUBLANES, LANES), (1, 2, 0, 3))
    ref[...] = tiles.reshape(rows, N_CHUNKS, LANES)


def _load_token_tiles(ref):
    rows = ref.shape[0]
    tiles = ref[...].reshape(rows // SUBLANES, SUBLANES, N_CHUNKS, LANES)
    chunks = jnp.transpose(tiles, (2, 0, 1, 3)).reshape(N_CHUNKS, rows, LANES)
    return jnp.concatenate([chunks[c] for c in range(N_CHUNKS)], axis=1)


def _deltanet_mixer_out(o_refs, tm):
    of = _stream_tile(o_refs[0:2], tm).astype(F32)
    ob = _stream_tile(o_refs[2:4], tm).astype(F32)
    gate_ref, g_ref = o_refs[4], o_refs[5]
    pieces = []
    for h in range(DN_HEADS):
        sl = slice(h * DN_DV, (h + 1) * DN_DV)
        o = of[:, sl] + ob[:, sl]
        o = o * lax.rsqrt(jnp.mean(o * o, axis=-1, keepdims=True) + EPS) * g_ref[...]
        pieces.append((o * _silu(gate_ref[:, sl].astype(F32))).astype(BF16))
    return jnp.concatenate(pieces, axis=1)


def _outproj_kernel(*refs, n_o, n_x, deltanet):
    o_refs, x_refs, refs = refs[:n_o], refs[n_o:n_o + n_x], refs[n_o + n_x:]
    mod_ref, w_ref, lng_ref, lnb_ref, rw_ref, rb_ref, x1_ref, h2_ref, ri_ref, rw_out_ref, cnt_ref = refs
    tm = x1_ref.shape[0]
    m = mod_ref[0]
    y = _dot(_deltanet_mixer_out(o_refs, tm) if deltanet else _stream_tile(o_refs, tm), w_ref[...])
    x1 = _layer_norm(ALPHA * _stream_tile(x_refs, tm) + m[2:3] * y, lng_ref[...], lnb_ref[...])
    x1_ref[...] = x1
    h2 = x1 * (1.0 + m[4:5]) + m[3:4]
    _store_token_tiles(h2_ref, h2)
    rw = rw_ref[...]
    h_hi, rw_hi = h2.astype(BF16), rw.astype(BF16)
    h_lo = (h2 - h_hi.astype(F32)).astype(BF16)
    rw_lo = (rw - rw_hi.astype(F32)).astype(BF16)
    logits = _dot(h_hi, rw_hi) + (_dot(h_lo, rw_hi) + _dot(h_hi, rw_lo))
    e1, e2, w1, w2 = _route(logits.T[:N_EXPERTS], rb_ref[...])
    eid = lax.broadcasted_iota(jnp.int32, (N_EXPERTS, tm), 0)
    onehot = (eid == e1) | (eid == e2)
    ti = lax.broadcasted_iota(jnp.int32, (tm, tm), 0)
    tj = lax.broadcasted_iota(jnp.int32, (tm, tm), 1)
    earlier = jnp.where(ti < tj, 1.0, 0.0).astype(BF16)
    rank = _dot(jnp.where(onehot, 1.0, 0.0).astype(BF16), earlier)
    r1 = jnp.sum(jnp.where(eid == e1, rank, 0.0), axis=0, keepdims=True).astype(jnp.int32)
    r2 = jnp.sum(jnp.where(eid == e2, rank, 0.0), axis=0, keepdims=True).astype(jnp.int32)
    ri_ref[...] = jnp.concatenate([e1, e2, r1, r2, jnp.zeros((4, tm), jnp.int32)], axis=0)
    wt = jnp.concatenate([w1, w2, jnp.zeros((LANES - 2, tm), F32)], axis=0)
    rw_out_ref[...] = wt.T
    cnt = jnp.sum(jnp.where(onehot, 1.0, 0.0), axis=1, keepdims=True)
    cnt_ref[0] = jnp.broadcast_to(cnt, (N_EXPERTS, LANES)).astype(jnp.int32)


def _output_projection(o_parts, x_parts, mod, w_bf16, ln_g, ln_b, router_w_pad, router_b, tm=OUT_TM,
                       deltanet=None):
    nt = N_TOK // tm
    if deltanet is None:
        o_specs = _stream_specs([a.shape[1] for a in o_parts], tm)
    else:
        z1, onorm_g = deltanet
        o_specs = (_stream_specs([DN_WIDTH, DN_WIDTH], tm) + _stream_specs([DN_WIDTH, DN_WIDTH], tm)
                   + [pl.BlockSpec((tm, DN_WIDTH), lambda i: (i, DN_GATE_COL0 // DN_WIDTH)),
                      pl.BlockSpec((1, DN_DV), lambda i: (0, 0))])
        o_parts = tuple(o_parts) + (z1, onorm_g.reshape(1, DN_DV))
    return pl.pallas_call(
        functools.partial(_outproj_kernel, n_o=len(o_parts), n_x=len(x_parts), deltanet=deltanet is not None),
        out_shape=(jax.ShapeDtypeStruct((N_TOK, D_MODEL), F32),
                   jax.ShapeDtypeStruct((N_TOK, N_CHUNKS, LANES), F32),
                   jax.ShapeDtypeStruct((8, N_TOK), jnp.int32),
                   jax.ShapeDtypeStruct((N_TOK, LANES), F32),
                   jax.ShapeDtypeStruct((nt, N_EXPERTS, LANES), jnp.int32)),
        grid=(nt,),
        in_specs=o_specs + _stream_specs([a.shape[1] for a in x_parts], tm) + [
            pl.BlockSpec((1, 6, D_MODEL), lambda i: (_cond_row(i * tm), 0, 0)),
            pl.BlockSpec((D_MODEL, D_MODEL), lambda i: (0, 0)),
            pl.BlockSpec((1, D_MODEL), lambda i: (0, 0)),
            pl.BlockSpec((1, D_MODEL), lambda i: (0, 0)),
            pl.BlockSpec((D_MODEL, LANES), lambda i: (0, 0)),
            pl.BlockSpec((N_EXPERTS, 1), lambda i: (0, 0)),
        ],
        out_specs=(pl.BlockSpec((tm, D_MODEL), lambda i: (i, 0)),
                   pl.BlockSpec((tm, N_CHUNKS, LANES), lambda i: (i, 0, 0)),
                   pl.BlockSpec((8, tm), lambda i: (0, i)),
                   pl.BlockSpec((tm, LANES), lambda i: (i, 0)),
                   pl.BlockSpec((1, N_EXPERTS, LANES), lambda i: (i, 0, 0))),
        compiler_params=_params("arbitrary"),
        name="output_projection",
    )(*o_parts, *x_parts, mod, w_bf16, ln_g.reshape(1, D_MODEL), ln_b.reshape(1, D_MODEL),
      router_w_pad, router_b.reshape(N_EXPERTS, 1))


N_ASSIGN = 2 * N_TOK
MOE_TM = 256
MOE_ROWS = N_ASSIGN + N_EXPERTS * MOE_TM
MOE_TILES = MOE_ROWS // MOE_TM
DISPATCH_TM = 1024
COMBINE_TM = 256


def _moe_plan(route_i, counts):
    cnt = counts[:, :, 0]
    total = jnp.sum(cnt, axis=0)
    padded = (total + MOE_TM - 1) // MOE_TM * MOE_TM
    seg_end = jnp.cumsum(padded)
    seg_start = seg_end - padded
    tile_base = seg_start[None, :] + jnp.cumsum(cnt, axis=0) - cnt
    base_tok = jnp.repeat(tile_base, OUT_TM, axis=0)
    eids = jnp.arange(N_EXPERTS, dtype=jnp.int32)[None, :]
    pos = [jnp.sum(jnp.where(route_i[k][:, None] == eids, base_tok, 0), axis=1) + route_i[2 + k]
           for k in range(2)]
    pos = jnp.concatenate(pos).astype(jnp.int32)
    tile_row0 = jnp.arange(MOE_TILES, dtype=jnp.int32) * MOE_TM
    tile_expert = jnp.minimum(jnp.sum(seg_end[None, :] <= tile_row0[:, None], axis=1), N_EXPERTS - 1)
    n_tiles = (seg_end[-1] // MOE_TM).reshape(1)
    last = MOE_ROWS - MOE_TM
    fill_rows = jnp.concatenate([jnp.minimum(seg_start + total, last),
                                 last - jnp.arange(N_EXPERTS, dtype=jnp.int32) * MOE_TM])
    return pos, tile_expert.astype(jnp.int32), n_tiles.astype(jnp.int32), fill_rows.astype(jnp.int32)


def _row_copy_wait(src_hbm, dst, sem, n_rows):
    pltpu.make_async_copy(src_hbm.at[pl.ds(0, n_rows)], dst.at[pl.ds(0, n_rows)], sem).wait()


def _dispatch_kernel(pos_ref, fill_ref, h_ref, xs_ref, zero_ref, sem, fill_sem):
    tm = h_ref.shape[0]
    t0 = pl.program_id(0) * tm

    @pl.when(pl.program_id(0) == 0)
    def _():
        zero_ref[...] = jnp.zeros_like(zero_ref)
        fills = [pltpu.make_async_copy(zero_ref, xs_ref.at[pl.ds(fill_ref[j], MOE_TM)], fill_sem)
                 for j in range(2 * N_EXPERTS)]
        for cp in fills[N_EXPERTS:]:
            cp.start()
        for cp in fills[N_EXPERTS:]:
            cp.wait()
        for cp in fills[:N_EXPERTS]:
            cp.start()
            cp.wait()

    def body(r, carry):
        for k in range(2):
            pltpu.make_async_copy(h_ref.at[r], xs_ref.at[pos_ref[k * N_TOK + t0 + r]], sem).start(priority=k)
        return carry

    lax.fori_loop(0, tm, body, 0, unroll=8)
    for _ in range(2):
        _row_copy_wait(h_ref, xs_ref, sem, tm)


def _moe_dispatch(pos, fill_rows, h2_tiles):
    tm = DISPATCH_TM
    return pl.pallas_call(
        _dispatch_kernel,
        out_shape=jax.ShapeDtypeStruct((MOE_ROWS, N_CHUNKS, LANES), F32),
        grid_spec=pltpu.PrefetchScalarGridSpec(
            num_scalar_prefetch=2,
            grid=(N_TOK // tm,),
            in_specs=[pl.BlockSpec((tm, N_CHUNKS, LANES), lambda i, pos, fill: (i, 0, 0))],
            out_specs=pl.BlockSpec(memory_space=pl.ANY),
            scratch_shapes=[pltpu.VMEM((MOE_TM, N_CHUNKS, LANES), F32), pltpu.SemaphoreType.DMA,
                            pltpu.SemaphoreType.DMA],
        ),
        compiler_params=_params("arbitrary"),
        name="moe_dispatch",
    )(pos, fill_rows, h2_tiles)


def _expert_kernel(te_ref, nt_ref, xs_ref, wg_ref, wu_ref, wd_ref, ys_ref, wg16_ref, wu16_ref, wd16_ref):
    i = pl.program_id(0)

    @pl.when(i < nt_ref[0])
    def _():
        @pl.when((i == 0) | (te_ref[i] != te_ref[jnp.maximum(i - 1, 0)]))
        def _():
            wg16_ref[...] = wg_ref[0, 0].astype(BF16)
            wu16_ref[...] = wu_ref[0, 0].astype(BF16)
            wd16_ref[...] = wd_ref[0, 0].astype(BF16)

        x = _load_token_tiles(xs_ref).astype(BF16)
        he = _silu(_dot(x, wg16_ref[...])) * _dot(x, wu16_ref[...])
        _store_token_tiles(ys_ref, _dot(he.astype(BF16), wd16_ref[...]))

    @pl.when(i >= nt_ref[0])
    def _():
        ys_ref[...] = jnp.zeros_like(ys_ref)


def _moe_experts(tile_expert, n_tiles, xs, layer, wg, wu, wd):
    row_blk = lambda i, te, nt: (jnp.minimum(i, nt[0] - 1), 0, 0)
    w_blk = lambda i, te, nt: (layer, te[i], 0, 0)
    return pl.pallas_call(
        _expert_kernel,
        out_shape=jax.ShapeDtypeStruct((MOE_ROWS, N_CHUNKS, LANES), F32),
        grid_spec=pltpu.PrefetchScalarGridSpec(
            num_scalar_prefetch=2,
            grid=(MOE_TILES,),
            in_specs=[
                pl.BlockSpec((MOE_TM, N_CHUNKS, LANES), row_blk),
                pl.BlockSpec((1, 1, D_MODEL, EXPERT_FF), w_blk),
                pl.BlockSpec((1, 1, D_MODEL, EXPERT_FF), w_blk),
                pl.BlockSpec((1, 1, EXPERT_FF, D_MODEL), w_blk),
            ],
            out_specs=pl.BlockSpec((MOE_TM, N_CHUNKS, LANES), lambda i, te, nt: (i, 0, 0)),
            scratch_shapes=[pltpu.VMEM((D_MODEL, EXPERT_FF), BF16), pltpu.VMEM((D_MODEL, EXPERT_FF), BF16),
                            pltpu.VMEM((EXPERT_FF, D_MODEL), BF16)],
        ),
        compiler_params=_params("arbitrary"),
        name="moe_experts",
    )(tile_expert, n_tiles, xs, wg, wu, wd)


def _combine_kernel(pos_ref, gw_ref, x1_ref, mod_ref, lng_ref, lnb_ref, ys_ref, *rest, split):
    out_refs, (buf_ref, sem) = rest[:-2], rest[-2:]
    i = pl.program_id(0)
    n = pl.num_programs(0)
    tm = COMBINE_TM

    def start_row(tile, to_slot, r):
        for k in range(2):
            p = pos_ref[k * N_TOK + tile * tm + r]
            pltpu.make_async_copy(ys_ref.at[p], buf_ref.at[to_slot, k, r], sem.at[to_slot]).start(priority=k)

    def wait_slot(s):
        for k in range(2):
            _row_copy_wait(ys_ref, buf_ref.at[s, k], sem.at[s], tm)

    def issue(tile, to_slot):
        def body(r, carry):
            start_row(tile, to_slot, r)
            return carry
        lax.fori_loop(0, tm, body, 0, unroll=8)

    @pl.when(i == 0)
    def _():
        issue(0, 0)

    @pl.when(i + 1 < n)
    def _():
        issue(i + 1, (i + 1) % 2)

    slot = i % 2
    wait_slot(slot)
    gw = gw_ref[...]
    f = gw[:, 0:1] * _load_token_tiles(buf_ref.at[slot, 0]) + gw[:, 1:2] * _load_token_tiles(buf_ref.at[slot, 1])
    m = mod_ref[0]
    out = _layer_norm(ALPHA * x1_ref[...] + m[5:6] * f, lng_ref[...], lnb_ref[...])

    if split:
        prompt_ref, sample_ref = out_refs

        @pl.when(i < N_PROMPT // tm)
        def _():
            prompt_ref[...] = out

        @pl.when(i >= N_PROMPT // tm)
        def _():
            sample_ref[...] = out
    else:
        out_refs[0][...] = out


def _moe_combine(pos, gate_w, x1, mod, ln_g, ln_b, ys, split):
    tm = COMBINE_TM
    np_blk = N_PROMPT // tm
    if split:
        out_shape = (jax.ShapeDtypeStruct((N_PROMPT, D_MODEL), F32), jax.ShapeDtypeStruct((N_SAMPLE, D_MODEL), F32))
        out_specs = (pl.BlockSpec((tm, D_MODEL), lambda i, pos: (jnp.minimum(i, np_blk - 1), 0)),
                     pl.BlockSpec((tm, D_MODEL), lambda i, pos: (jnp.maximum(i - np_blk, 0), 0)))
    else:
        out_shape = jax.ShapeDtypeStruct((N_TOK, D_MODEL), F32)
        out_specs = pl.BlockSpec((tm, D_MODEL), lambda i, pos: (i, 0))
    return pl.pallas_call(
        functools.partial(_combine_kernel, split=split),
        out_shape=out_shape,
        grid_spec=pltpu.PrefetchScalarGridSpec(
            num_scalar_prefetch=1,
            grid=(N_TOK // tm,),
            in_specs=[
                pl.BlockSpec((tm, LANES), lambda i, pos: (i, 0)),
                pl.BlockSpec((tm, D_MODEL), lambda i, pos: (i, 0)),
                pl.BlockSpec((1, 6, D_MODEL), lambda i, pos: (_cond_row(i * tm), 0, 0)),
                pl.BlockSpec((1, D_MODEL), lambda i, pos: (0, 0)),
                pl.BlockSpec((1, D_MODEL), lambda i, pos: (0, 0)),
                pl.BlockSpec(memory_space=pl.ANY),
            ],
            out_specs=out_specs,
            scratch_shapes=[pltpu.VMEM((2, 2, tm, N_CHUNKS, LANES), F32), pltpu.SemaphoreType.DMA((2,))],
        ),
        compiler_params=_params("arbitrary"),
        name="moe_combine",
    )(pos, gate_w, x1, mod, ln_g.reshape(1, D_MODEL), ln_b.reshape(1, D_MODEL), ys)


def _moe_ffn(h2_tiles, route_i, gate_w, counts, layer, wg, wu, wd, x1, mod, ln_g, ln_b, split=False):
    pos, tile_expert, n_tiles, fill_rows = _moe_plan(route_i, counts)
    xs = _moe_dispatch(pos, fill_rows, h2_tiles)
    ys = _moe_experts(tile_expert, n_tiles, xs, layer, wg, wu, wd)
    return _moe_combine(pos, gate_w, x1, mod, ln_g, ln_b, ys, split)


DN_GATE_COL0 = DN_CONV_CH
N_DIRS = 2


def _dn_prep_kernel(q_ref, k_ref, v_ref, wq_ref, wk_ref, wv_ref, qo_ref, ko_ref, vo_ref):
    length = q_ref.shape[0]
    row = lax.broadcasted_iota(jnp.int32, (length, DN_DK), 0)

    def conv(x, w):
        prev = jnp.where(row == 0, 0.0, pltpu.roll(x, 1, axis=0))
        nxt = jnp.where(row == length - 1, 0.0, pltpu.roll(x, length - 1, axis=0))
        return _silu(prev * w[0:1] + x * w[1:2] + nxt * w[2:3])

    def l2n(x):
        return x * lax.rsqrt(jnp.sum(x * x, axis=-1, keepdims=True) + 1e-6)

    for h in range(DN_HEADS):
        sl = slice(h * DN_DK, (h + 1) * DN_DK)
        q, k, v = (r[:, sl].astype(F32) for r in (q_ref, k_ref, v_ref))
        qo_ref[h] = (l2n(conv(q, wq_ref[:, sl])) * (DN_DK ** -0.5)).astype(qo_ref.dtype)
        ko_ref[h] = l2n(conv(k, wk_ref[:, sl])).astype(ko_ref.dtype)
        vo_ref[h] = conv(v, wv_ref[:, sl]).astype(vo_ref.dtype)


def _dn_prep(z1, conv_w, nb, length, row_blk0):
    shp = jax.ShapeDtypeStruct((DN_HEADS, nb * length, DN_DK), BF16)
    blk = lambda part: pl.BlockSpec((length, DN_QK_WIDTH), lambda b: (row_blk0 + b, part))
    wblk = lambda part: pl.BlockSpec((3, DN_QK_WIDTH), lambda b: (0, part))
    oblk = pl.BlockSpec((DN_HEADS, length, DN_DK), lambda b: (0, b, 0))
    return pl.pallas_call(
        _dn_prep_kernel,
        out_shape=(shp, shp, shp),
        grid=(nb,),
        in_specs=[blk(0), blk(1), blk(2), wblk(0), wblk(1), wblk(2)],
        out_specs=(oblk, oblk, oblk),
        compiler_params=_params("arbitrary"),
        name="deltanet_prep",
    )(z1, z1, z1, conv_w, conv_w, conv_w)


def _dn_gates_kernel(ab_ref, alog_ref, dtb_ref, o_ref):
    length = ab_ref.shape[0]
    x = ab_ref[...]
    a = x + dtb_ref[...]
    softplus = jnp.maximum(a, 0.0) + jnp.log1p(jnp.exp(-jnp.abs(a)))
    g = -jnp.exp(alog_ref[...]) * softplus
    beta = jax.nn.sigmoid(x)
    ri = lax.broadcasted_iota(jnp.int32, (DN_CHUNK, DN_CHUNK), 0)
    ci = lax.broadcasted_iota(jnp.int32, (DN_CHUNK, DN_CHUNK), 1)
    tril = (ri >= ci).astype(F32)
    triu = (ri <= ci).astype(F32)
    lane = lax.broadcasted_iota(jnp.int32, (DN_CHUNK, LANES), 1)
    for c in range(length // DN_CHUNK):
        rows = slice(c * DN_CHUNK, (c + 1) * DN_CHUNK)
        gch = g[rows]
        pre = jnp.dot(tril, gch, precision=lax.Precision.HIGHEST, preferred_element_type=F32)
        suf = jnp.dot(triu, gch, precision=lax.Precision.HIGHEST, preferred_element_type=F32)
        gc = jnp.where(lane < DN_HEADS, pre, suf)
        o_ref[rows, :] = jnp.where(lane < N_DIRS * DN_HEADS, gc, beta[rows])


def _dn_gates(z_ab, a_log, dt_bias, nb, length, row_blk0):
    pad = LANES - N_DIRS * DN_HEADS
    alog = jnp.pad(a_log.reshape(1, -1).astype(F32), ((0, 0), (0, pad)))
    dtb = jnp.pad(dt_bias.reshape(1, -1).astype(F32), ((0, 0), (0, pad)))
    return pl.pallas_call(
        _dn_gates_kernel,
        out_shape=jax.ShapeDtypeStruct((nb * length, LANES), F32),
        grid=(nb,),
        in_specs=[
            pl.BlockSpec((length, LANES), lambda b: (row_blk0 + b, 0)),
            pl.BlockSpec((1, LANES), lambda b: (0, 0)),
            pl.BlockSpec((1, LANES), lambda b: (0, 0)),
        ],
        out_specs=pl.BlockSpec((length, LANES), lambda b: (b, 0)),
        compiler_params=_params("arbitrary"),
        name="deltanet_gates",
    )(z_ab, alog, dtb)


def _delta_chunk(q, k, v, beta, gcol, grow, s, lower):
    ri = lax.broadcasted_iota(jnp.int32, (DN_CHUNK, DN_CHUNK), 0)
    ci = lax.broadcasted_iota(jnp.int32, (DN_CHUNK, DN_CHUNK), 1)
    incl = (ri >= ci) if lower else (ri <= ci)
    strict = (ri > ci) if lower else (ri < ci)
    decay = jnp.exp(jnp.where(incl, gcol - grow, -jnp.inf))
    kb = k * beta
    kq = _dot_nt(jnp.concatenate([kb, q], axis=0).astype(BF16), k.astype(BF16))
    kk, a_qk = kq[:DN_CHUNK], kq[DN_CHUNK:] * decay
    yield
    tri_l = jnp.where(strict, kk * decay, 0.0)

    def off_block(size):
        same = (ri // (2 * size)) == (ci // (2 * size))
        rpar, cpar = (ri // size) % 2, (ci // size) % 2
        return same & ((rpar == 1) & (cpar == 0) if lower else (rpar == 0) & (cpar == 1))

    p = jnp.where(ri == ci, 1.0, 0.0) - jnp.where(off_block(1), tri_l, 0.0)
    size = 2
    while size < DN_CHUNK:
        p16 = p.astype(BF16)
        pc = _dot(p16, jnp.where(off_block(size), tri_l, 0.0).astype(BF16))
        yield
        p = p - _dot(pc.astype(BF16), p16)
        yield
        size *= 2
    eg = jnp.exp(gcol)
    uw = _dot(p.astype(BF16), jnp.concatenate([v * beta, kb * eg], axis=1).astype(BF16))
    yield
    u, w = uw[:, :DN_DV], uw[:, DN_DV:]
    ws = _dot(jnp.concatenate([w, q * eg], axis=0).astype(BF16), s.astype(BF16))
    v_new, o_inter = u - ws[:DN_CHUNK], ws[DN_CHUNK:]
    yield
    v_new16 = v_new.astype(BF16)
    o = o_inter + _dot(a_qk.astype(BF16), v_new16)
    g_last = gcol[DN_CHUNK - 1:DN_CHUNK] if lower else gcol[0:1]
    kd = k * jnp.exp(g_last - gcol)
    s_new = s * jnp.exp(g_last) + _dot_tn(kd.astype(BF16), v_new16)
    return o, s_new


def _run_interleaved(chains):
    results = [None] * len(chains)
    active = list(enumerate(chains))
    while active:
        still = []
        for idx, gen in active:
            try:
                next(gen)
                still.append((idx, gen))
            except StopIteration as stop:
                results[idx] = stop.value
        active = still
    return results


def _dn_scan_kernel(*refs, has_init, want_final):
    (qf_ref, kf_ref, vf_ref, gf_ref, qb_ref, kb_ref, vb_ref, gb_ref), rest = refs[:8], refs[8:]
    if has_init:
        s0_ref, rest = rest[0], rest[1:]
    of_ref, ob_ref = rest[0], rest[1]
    rest = rest[2:]
    if want_final:
        sf_ref, rest = rest[0], rest[1:]
    s_ref = rest[0]
    c = pl.program_id(1)
    nc = pl.num_programs(1)
    per_request = N_DIRS * DN_HEADS
    states = [(u, d, h) for u in range(DN_SCAN_BATCH) for d in range(N_DIRS) for h in range(DN_HEADS)]

    @pl.when(c == 0)
    def _():
        for j, (u, d, h) in enumerate(states):
            s_ref[j] = s0_ref[u, d, h] if has_init else jnp.zeros((DN_DK, DN_DV), F32)

    dirs = ((qf_ref, kf_ref, vf_ref, gf_ref, of_ref, True), (qb_ref, kb_ref, vb_ref, gb_ref, ob_ref, False))
    chains = []
    for u in range(DN_SCAN_BATCH):
        for d, (q_ref, k_ref, v_ref, g_ref, o_ref, lower) in enumerate(dirs):
            gates = g_ref[u]
            gates_t = gates.T
            for h in range(DN_HEADS):
                j = d * DN_HEADS + h
                jb = per_request + j
                chains.append(_delta_chunk(q_ref[h, u].astype(F32), k_ref[h, u].astype(F32),
                                           v_ref[h, u].astype(F32), gates[:, jb:jb + 1], gates[:, j:j + 1],
                                           gates_t[j:j + 1, :], s_ref[u * per_request + j], lower))
    results = _run_interleaved(chains)
    for j, (u, d, h) in enumerate(states):
        o, s_new = results[j]
        dirs[d][4][u, :, h * DN_DV:(h + 1) * DN_DV] = o.astype(of_ref.dtype)
        s_ref[j] = s_new

    if want_final:
        @pl.when(c == nc - 1)
        def _():
            for j, (u, d, h) in enumerate(states):
                sf_ref[u, d, h] = s_ref[j]


def _dn_scan(qn, kn, vn, gates, s0, nb, length, want_final):
    nc = length // DN_CHUNK
    bb = DN_SCAN_BATCH
    has_init = s0 is not None
    qn, kn, vn = (t.reshape(DN_HEADS, nb, length, DN_DK) for t in (qn, kn, vn))
    gates = gates.reshape(nb, length, LANES)
    fwd = lambda b, c: (b, c, 0)
    bwd = lambda b, c: (b, nc - 1 - c, 0)
    hm = lambda im: pl.BlockSpec((DN_HEADS, bb, DN_CHUNK, DN_DK), lambda b, c: (0,) + im(b, c))
    gm = lambda im: pl.BlockSpec((bb, DN_CHUNK, LANES), im)
    in_specs = [hm(fwd), hm(fwd), hm(fwd), gm(fwd), hm(bwd), hm(bwd), hm(bwd), gm(bwd)]
    args = [qn, kn, vn, gates, qn, kn, vn, gates]
    state_blk = pl.BlockSpec((bb, N_DIRS, DN_HEADS, DN_DK, DN_DV), lambda b, c: (b, 0, 0, 0, 0))
    if has_init:
        in_specs.append(state_blk)
        args.append(s0)
    o_shape = jax.ShapeDtypeStruct((nb, length, DN_WIDTH), BF16)
    out_shape = [o_shape, o_shape]
    out_specs = [pl.BlockSpec((bb, DN_CHUNK, DN_WIDTH), fwd), pl.BlockSpec((bb, DN_CHUNK, DN_WIDTH), bwd)]
    if want_final:
        out_shape.append(jax.ShapeDtypeStruct((nb, N_DIRS, DN_HEADS, DN_DK, DN_DV), F32))
        out_specs.append(state_blk)
    outs = pl.pallas_call(
        functools.partial(_dn_scan_kernel, has_init=has_init, want_final=want_final),
        out_shape=tuple(out_shape),
        grid=(nb // bb, nc),
        in_specs=in_specs,
        out_specs=tuple(out_specs),
        scratch_shapes=[pltpu.VMEM((bb * N_DIRS * DN_HEADS, DN_DK, DN_DV), F32)],
        compiler_params=_params("arbitrary", "arbitrary"),
        name="deltanet_scan",
    )(*args)
    return [outs[0].reshape(nb * length, DN_WIDTH), outs[1].reshape(nb * length, DN_WIDTH)] + list(outs[2:])


DN_SCAN_BATCH = 2


def _deltanet_stream(z1, z_ab, conv_w, a_log, dt_bias, s0, nb, length, row0, want_final):
    row_blk0 = row0 // length
    qn, kn, vn = _dn_prep(z1, conv_w, nb, length, row_blk0)
    gates = _dn_gates(z_ab, a_log, dt_bias, nb, length, row_blk0)
    outs = _dn_scan(qn, kn, vn, gates, s0, nb, length, want_final)
    return outs[0], outs[1], (outs[2] if want_final else None)


def kernel(x_prompt, x_sample, cache_nat_k, cache_nat_v, cache_diff_k, cache_diff_v, state_delta, c, c_ctx, w_in_even, w_out_even, nat_bias, diff_lam, diff_subln, w_in_odd, conv_odd, a_log_odd, dt_bias_odd, onorm_odd, w_out_odd, ada_w, ada_b, ln_g, ln_b, router_w, router_b, moe_wg, moe_wu, moe_wd):
    x_parts = (x_prompt.reshape(N_PROMPT, D_MODEL), x_sample.reshape(N_SAMPLE, D_MODEL))
    cond =jnp.concatenate([c_ctx[None], c, jnp.zeros((N_COND - 1 - DEC_BATCH, D_MODEL), F32)], axis=0)
    mod = _ada_modulation(cond, ada_w, ada_b).reshape(DEPTH, N_COND, 6, D_MODEL)
    router_w_pad = jnp.pad(router_w, ((0, 0), (0, LANES - N_EXPERTS)))

    lam_init = 0.8 - 0.6 * math.exp(-0.3 * 0)
    z = _input_projection(x_parts, mod[0], w_in_even[0].astype(BF16))
    new_nat_k = z[:N_PROMPT, _NAT_K0:_NAT_V0].reshape(BATCH, 1, SEQ, NAT_HEADS, HEAD_DIM)
    new_nat_v = z[:N_PROMPT, _NAT_V0:_DIFF_Q0].reshape(BATCH, 1, SEQ, NAT_HEADS, HEAD_DIM)
    new_diff_k = z[:N_PROMPT, _DIFF_K0:_DIFF_V0].reshape(BATCH, 1, SEQ, DIFF_HEADS, 2, HEAD_DIM)
    new_diff_v = z[:N_PROMPT, _DIFF_V0:].reshape(BATCH, 1, SEQ, DIFF_HEADS, 2 * HEAD_DIM)
    o_ctx = _context_attention(z, diff_lam[0], diff_subln[0], lam_init)
    o_nat = _nat_latent_attention(z, cache_nat_k[:, 0].reshape(DEC_BATCH, PAST_LEN, NAT_WIDTH),
                                  cache_nat_v[:, 0].reshape(DEC_BATCH, PAST_LEN, NAT_WIDTH),
                                  _nat_bias_diagonals(nat_bias[0]))
    o_diff = _diff_latent_attention(z, cache_diff_k[:, 0].reshape(DEC_BATCH, PAST_LEN, DIFF_WIDTH),
                                    cache_diff_v[:, 0].reshape(DEC_BATCH, PAST_LEN, DIFF_WIDTH),
                                    diff_lam[0], diff_subln[0], lam_init)
    x1, *routed = _output_projection((o_ctx, o_nat, o_diff), x_parts, mod[0], w_out_even[0].astype(BF16),
                                     ln_g[0, 0], ln_b[0, 0], router_w_pad, router_b)
    x = _moe_ffn(*routed, 0, moe_wg, moe_wu, moe_wd, x1, mod[0], ln_g[0, 1], ln_b[0, 1])

    w1 = w_in_odd[0]
    w1 = jnp.concatenate([w1[:, :DN_CONV_CH], w1[:, DN_CONV_CH + 4 * DN_HEADS:],
                          w1[:, DN_CONV_CH:DN_CONV_CH + 4 * DN_HEADS],
                          jnp.zeros((D_MODEL, LANES - 4 * DN_HEADS), F32)], axis=1).astype(BF16)
    z1, z_ab = _input_projection((x,), mod[1], w1, f32_tail=LANES)
    of_p, ob_p, new_state = _deltanet_stream(z1, z_ab, conv_odd[0], a_log_odd[0], dt_bias_odd[0],
                                             None, BATCH, SEQ, 0, True)
    of_s, ob_s, _ = _deltanet_stream(z1, z_ab, conv_odd[0], a_log_odd[0], dt_bias_odd[0],
                                     state_delta[:, 0], DEC_BATCH, DEC_SEQ, N_PROMPT, False)
    x1, *routed = _output_projection((of_p, of_s, ob_p, ob_s), (x,), mod[1], w_out_odd[0].astype(BF16),
                                     ln_g[1, 0], ln_b[1, 0], router_w_pad, router_b,
                                     deltanet=(z1, onorm_odd[0]))
    y_prompt, y_sample = _moe_ffn(*routed, 1, moe_wg, moe_wu, moe_wd, x1, mod[1], ln_g[1, 1], ln_b[1, 1],
                                  split=True)
    y_prompt = y_prompt.reshape(BATCH, SEQ, D_MODEL)
    y_sample = y_sample.reshape(DEC_BATCH, DEC_SEQ, D_MODEL)
    return (y_prompt, y_sample, new_nat_k, new_nat_v, new_diff_k, new_diff_v, new_state[:, None])
```

```python
import functools
import math

import jax
import jax.numpy as jnp
import numpy as np
from jax import lax
from jax.experimental import pallas as pl
from jax.experimental.pallas import tpu as pltpu

F32 = jnp.float32
BF16 = jnp.bfloat16

D_MODEL = 1024
BATCH = 16
SEQ = 256
DEC_BATCH = 8
DEC_SEQ = 1024
PAST_LEN = 256
GRID_W = 64
N_ROWS = DEC_SEQ // GRID_W

HEAD_DIM = 64
NAT_HEADS = 8
NAT_WIN_H = 8
NAT_WIN_W = 16
DIFF_HEADS = 4
NAT_WIDTH = NAT_HEADS * HEAD_DIM
DIFF_WIDTH = DIFF_HEADS * 2 * HEAD_DIM
EVEN_IN = 3 * NAT_WIDTH + 3 * DIFF_WIDTH
ROPE_THETA = 10000.0

DN_HEADS = 8
DN_DK = 128
DN_DV = 128
DN_QK_WIDTH = DN_HEADS * DN_DK
DN_WIDTH = DN_HEADS * DN_DV
DN_CONV_CH = 2 * DN_QK_WIDTH + DN_WIDTH
DN_CHUNK = 64

N_EXPERTS = 16
N_GROUPS = 4
EXPERTS_PER_GROUP = N_EXPERTS // N_GROUPS
EXPERT_FF = 512

DEPTH = 2
ALPHA = (2 * DEPTH) ** 0.25
EPS = 1e-5

N_PROMPT = BATCH * SEQ
N_SAMPLE = DEC_BATCH * DEC_SEQ
N_TOK = N_PROMPT + N_SAMPLE
N_COND = 16

VMEM_LIMIT = 56 * 1024 * 1024
LANES = 128
SUBLANES = 8
N_CHUNKS = D_MODEL // LANES
assert N_CHUNKS == SUBLANES


def _params(*sem):
    return pltpu.CompilerParams(dimension_semantics=sem, vmem_limit_bytes=VMEM_LIMIT)


def _cond_row(row0):
    return jnp.where(row0 < N_PROMPT, 0, 1 + (row0 - N_PROMPT) // DEC_SEQ)


def _silu(x):
    return x * jax.nn.sigmoid(x)


def _layer_norm(r, g, b):
    mu = jnp.mean(r, axis=-1, keepdims=True)
    xc = r - mu
    var = jnp.mean(xc * xc, axis=-1, keepdims=True)
    return xc * lax.rsqrt(var + EPS) * g + b


def _dot(a, b):
    return jnp.dot(a, b, preferred_element_type=F32)


def _dot_nt(a, b):
    return lax.dot_general(a, b, (((1,), (1,)), ((), ())), preferred_element_type=F32)


def _dot_tn(a, b):
    return lax.dot_general(a, b, (((0,), (0,)), ((), ())), preferred_element_type=F32)


ADA_TN = 1536


def _ada_kernel(c_ref, w_ref, b_ref, o_ref):
    s = _silu(c_ref[...])
    o_ref[0] = _dot(s.astype(BF16), w_ref[0].astype(BF16)) + b_ref[0]


def _ada_modulation(cond, ada_w, ada_b):
    n = 6 * D_MODEL
    return pl.pallas_call(
        _ada_kernel,
        out_shape=jax.ShapeDtypeStruct((DEPTH, N_COND, n), F32),
        grid=(DEPTH, n // ADA_TN),
        in_specs=[
            pl.BlockSpec((N_COND, D_MODEL), lambda l, j: (0, 0)),
            pl.BlockSpec((1, D_MODEL, ADA_TN), lambda l, j: (l, 0, j)),
            pl.BlockSpec((1, 1, ADA_TN), lambda l, j: (l, 0, j)),
        ],
        out_specs=pl.BlockSpec((1, N_COND, ADA_TN), lambda l, j: (l, 0, j)),
        compiler_params=_params("arbitrary", "arbitrary"),
        name="ada_modulation",
    )(cond, ada_w, ada_b.reshape(DEPTH, 1, n))


PROJ_TM = 512


def _stream_specs(widths, tm):
    if len(widths) == 1:
        return [pl.BlockSpec((tm, widths[0]), lambda i: (i, 0))]
    n_ctx = N_PROMPT // tm
    return ([pl.BlockSpec((tm, widths[0]), lambda i: (jnp.minimum(i, n_ctx - 1), 0))]
            + [pl.BlockSpec((tm, w), lambda i: (jnp.maximum(i - n_ctx, 0), 0)) for w in widths[1:]])


def _stream_tile(refs, tm):
    if len(refs) == 1:
        return refs[0][...]
    latent = [r[...] for r in refs[1:]]
    latent = latent[0] if len(latent) == 1 else jnp.concatenate(latent, axis=1)
    return jnp.where(pl.program_id(0) * tm < N_PROMPT, refs[0][...], latent)


def _inproj_kernel(*refs, n_x):
    x_refs, (mod_ref, w_ref, o_ref), tail_refs = refs[:n_x], refs[n_x:n_x + 3], refs[n_x + 3:]
    m = mod_ref[0]
    h = _stream_tile(x_refs, o_ref.shape[0]) * (1.0 + m[1:2]) + m[0:1]
    z = _dot(h.astype(BF16), w_ref[...])
    n_main = o_ref.shape[1]
    o_ref[...] = z[:, :n_main].astype(o_ref.dtype)
    if tail_refs:
        tail_refs[0][...] = z[:, n_main:]


def _input_projection(x_parts, mod, w_bf16, tm=PROJ_TM, f32_tail=0):
    n = w_bf16.shape[1] - f32_tail
    out_shape = [jax.ShapeDtypeStruct((N_TOK, n), BF16 if f32_tail else F32)]
    out_specs = [pl.BlockSpec((tm, n), lambda i: (i, 0))]
    if f32_tail:
        out_shape.append(jax.ShapeDtypeStruct((N_TOK, f32_tail), F32))
        out_specs.append(pl.BlockSpec((tm, f32_tail), lambda i: (i, 0)))
    outs = pl.pallas_call(
        functools.partial(_inproj_kernel, n_x=len(x_parts)),
        out_shape=tuple(out_shape),
        grid=(N_TOK // tm,),
        in_specs=_stream_specs([a.shape[1] for a in x_parts], tm) + [
            pl.BlockSpec((1, 6, D_MODEL), lambda i: (_cond_row(i * tm), 0, 0)),
            pl.BlockSpec((D_MODEL, n + f32_tail), lambda i: (0, 0)),
        ],
        out_specs=tuple(out_specs),
        compiler_params=_params("arbitrary"),
        name="input_projection",
    )(*x_parts, mod, w_bf16)
    return outs if f32_tail else outs[0]


_NAT_Q0, _NAT_K0, _NAT_V0 = 0, NAT_WIDTH, 2 * NAT_WIDTH
_DIFF_Q0 = 3 * NAT_WIDTH
_DIFF_K0 = _DIFF_Q0 + DIFF_WIDTH
_DIFF_V0 = _DIFF_K0 + DIFF_WIDTH
ATTN_SCALE = HEAD_DIM ** -0.5


def _diff_lambda(lam_ref, lam_init):
    lp = lam_ref[...]
    return (jnp.exp(jnp.sum(lp[0:1] * lp[1:2], axis=-1, keepdims=True))
            - jnp.exp(jnp.sum(lp[2:3] * lp[3:4], axis=-1, keepdims=True)) + lam_init)


def _scaled_q(q):
    assert math.log2(HEAD_DIM) % 2 == 0
    return (q * ATTN_SCALE).astype(BF16)


def _with_ones(v):
    width = v.shape[1] if v.shape[1] % LANES == 0 else LANES - v.shape[1] % LANES
    return jnp.concatenate([v, jnp.ones((v.shape[0], width), v.dtype)], axis=1)


def _sub_norm(o, g, lam_init):
    ms = jnp.mean(o * o, axis=-1, keepdims=True)
    return o * lax.rsqrt(ms + EPS) * g * (1.0 - lam_init)


def _ctx_attn_kernel(z_ref, lam_ref, subg_ref, o_ref, *, lam_init):
    lam = _diff_lambda(lam_ref, lam_init)

    def softmax_av(q0, k0, v0, v_width):
        q = _scaled_q(z_ref[:, q0:q0 + HEAD_DIM])
        k = z_ref[:, k0:k0 + HEAD_DIM].astype(BF16)
        s = _dot_nt(q, k)
        yield
        e = jnp.exp(s - jnp.max(s, axis=-1, keepdims=True))
        yield
        av = _dot(e.astype(BF16), _with_ones(z_ref[:, v0:v0 + v_width].astype(BF16)))
        yield
        return av[:, :v_width] / av[:, v_width:v_width + 1]

    chains = [softmax_av(_NAT_Q0 + h * HEAD_DIM, _NAT_K0 + h * HEAD_DIM, _NAT_V0 + h * HEAD_DIM, HEAD_DIM)
              for h in range(NAT_HEADS)]
    chains += [softmax_av(_DIFF_Q0 + i * HEAD_DIM, _DIFF_K0 + i * HEAD_DIM,
                          _DIFF_V0 + (i // 2) * 2 * HEAD_DIM, 2 * HEAD_DIM)
               for i in range(2 * DIFF_HEADS)]
    outs = _run_interleaved(chains)
    for h in range(NAT_HEADS):
        o_ref[:, h * HEAD_DIM:(h + 1) * HEAD_DIM] = outs[h].astype(o_ref.dtype)
    for h in range(DIFF_HEADS):
        c = NAT_WIDTH + h * 2 * HEAD_DIM
        o = outs[NAT_HEADS + 2 * h] - lam * outs[NAT_HEADS + 2 * h + 1]
        o_ref[:, c:c + 2 * HEAD_DIM] = _sub_norm(o, subg_ref[...], lam_init).astype(o_ref.dtype)


def _context_attention(z, lam_p, subln_g, lam_init):
    return pl.pallas_call(
        functools.partial(_ctx_attn_kernel, lam_init=lam_init),
        out_shape=jax.ShapeDtypeStruct((N_PROMPT, D_MODEL), BF16),
        grid=(BATCH,),
        in_specs=[
            pl.BlockSpec((SEQ, EVEN_IN), lambda b: (b, 0)),
            pl.BlockSpec((4, HEAD_DIM), lambda b: (0, 0)),
            pl.BlockSpec((1, 2 * HEAD_DIM), lambda b: (0, 0)),
        ],
        out_specs=pl.BlockSpec((SEQ, D_MODEL), lambda b: (b, 0)),
        compiler_params=_params("arbitrary"),
        name="context_attention",
    )(z, lam_p, subln_g.reshape(1, 2 * HEAD_DIM))


NAT_TQ = 512
NAT_GROUP = 4
NAT_KEYS = 12 * GRID_W
_SAMPLE_BLK0 = N_PROMPT // DEC_SEQ


NAT_SPAN = NAT_KEYS // GRID_W
NAT_SPAN_PAD = 16


def _nat_bias_diagonals(rel_bias):
    rows = N_ROWS
    kh = min(NAT_WIN_H, rows)
    row_onehot = np.zeros((2 * NAT_WIN_H - 1, rows, NAT_SPAN), np.float32)
    for r in range(rows):
        start = min(max(r - kh // 2, 0), rows - kh)
        a0 = _nat_key0(r * GRID_W // NAT_TQ) // GRID_W
        assert a0 <= start and start + kh <= a0 + NAT_SPAN
        for a in range(start, start + kh):
            row_onehot[a - r + NAT_WIN_H - 1, r, a - a0] = 1.0
    lane_onehot = np.zeros((2 * NAT_WIN_W - 1, LANES), np.float32)
    for lane in range(2 * GRID_W - 1):
        lane_onehot[np.clip(lane - (GRID_W - 1), 1 - NAT_WIN_W, NAT_WIN_W - 1) + NAT_WIN_W - 1, lane] = 1.0
    hi = lax.Precision.HIGHEST
    by_row = jnp.einsum('hed,era->hrad', rel_bias.astype(F32), row_onehot, precision=hi)
    diag = jnp.einsum('hrad,dl->hral', by_row, lane_onehot, precision=hi)
    diag = jnp.where((row_onehot.sum(0) > 0)[None, :, :, None], diag, -jnp.inf)
    return jnp.pad(diag, ((0, 0), (0, 0), (0, NAT_SPAN_PAD - NAT_SPAN), (0, 0)))


def _nat_bias_tiles(diag_ref, bias_ref):
    shape = (GRID_W, LANES)
    q = lax.broadcasted_iota(jnp.int32, shape, 0)
    lane = lax.broadcasted_iota(jnp.int32, shape, 1)
    k = lane % GRID_W
    c0 = jnp.clip(q - NAT_WIN_W // 2, 0, GRID_W - NAT_WIN_W)
    in_cols = (k >= c0) & (k < c0 + NAT_WIN_W)
    for j in range(NAT_GROUP):
        for r in range(NAT_TQ // GRID_W):
            for pair in range(NAT_SPAN // 2):
                even = jnp.broadcast_to(diag_ref[j, r, 2 * pair:2 * pair + 1, :], shape)
                odd = jnp.broadcast_to(diag_ref[j, r, 2 * pair + 1:2 * pair + 2, :], shape)
                even = pltpu.roll(even, GRID_W + 1, 1, stride=1, stride_axis=0)
                odd = pltpu.roll(odd, 1, 1, stride=1, stride_axis=0)
                tile = jnp.where(lane < GRID_W, even, odd)
                bias_ref[j, r * GRID_W:(r + 1) * GRID_W, pair * LANES:(pair + 1) * LANES] = (
                    jnp.where(in_cols, tile, -jnp.inf))


def _nat_key0(qt):
    half = DEC_SEQ // NAT_TQ // 2
    return (qt >= half) * (DEC_SEQ - NAT_KEYS)


def _nat_latent_kernel(q_ref, k_ref, v_ref, ck_ref, cv_ref, diag_ref, o_ref, bias_ref):
    key0 = pl.multiple_of(_nat_key0(pl.program_id(1)).astype(jnp.int32), DEC_SEQ - NAT_KEYS)
    keys = pl.ds(key0, NAT_KEYS)

    @pl.when(pl.program_id(2) == 0)
    def _():
        _nat_bias_tiles(diag_ref, bias_ref)

    def head(j):
        sl = slice(j * HEAD_DIM, (j + 1) * HEAD_DIM)
        q = _scaled_q(q_ref[:, sl])
        s_loc = _dot_nt(q, k_ref[keys, sl].astype(BF16)) + bias_ref[j]
        s_ctx = _dot_nt(q, ck_ref[0, :, sl].astype(BF16))
        yield
        m = jnp.maximum(jnp.max(s_loc, axis=-1, keepdims=True), jnp.max(s_ctx, axis=-1, keepdims=True))
        yield
        e_loc = jnp.exp(s_loc - m)
        e_ctx = jnp.exp(s_ctx - m)
        yield
        o = (_dot(e_loc.astype(BF16), _with_ones(v_ref[keys, sl].astype(BF16)))
             + _dot(e_ctx.astype(BF16), _with_ones(cv_ref[0, :, sl].astype(BF16))))
        yield
        return (o[:, :HEAD_DIM] / o[:, HEAD_DIM:HEAD_DIM + 1]).astype(o_ref.dtype)

    outs = _run_interleaved([head(j) for j in range(NAT_GROUP)])
    for j in range(NAT_GROUP):
        o_ref[:, j * HEAD_DIM:(j + 1) * HEAD_DIM] = outs[j]


def _nat_latent_attention(z, ck, cv, diag):
    nq = DEC_SEQ // NAT_TQ
    qblk0 = N_PROMPT // NAT_TQ
    pair = NAT_GROUP * HEAD_DIM
    return pl.pallas_call(
        _nat_latent_kernel,
        out_shape=jax.ShapeDtypeStruct((N_SAMPLE, NAT_WIDTH), BF16),
        grid=(NAT_HEADS // NAT_GROUP, nq, DEC_BATCH),
        in_specs=[
            pl.BlockSpec((NAT_TQ, pair), lambda hp, qt, b: (qblk0 + b * nq + qt, _NAT_Q0 // pair + hp)),
            pl.BlockSpec((DEC_SEQ, pair), lambda hp, qt, b: (_SAMPLE_BLK0 + b, _NAT_K0 // pair + hp)),
            pl.BlockSpec((DEC_SEQ, pair), lambda hp, qt, b: (_SAMPLE_BLK0 + b, _NAT_V0 // pair + hp)),
            pl.BlockSpec((1, PAST_LEN, pair), lambda hp, qt, b: (b, 0, hp)),
            pl.BlockSpec((1, PAST_LEN, pair), lambda hp, qt, b: (b, 0, hp)),
            pl.BlockSpec((NAT_GROUP, NAT_TQ // GRID_W, NAT_SPAN_PAD, LANES), lambda hp, qt, b: (hp, qt, 0, 0)),
        ],
        out_specs=pl.BlockSpec((NAT_TQ, pair), lambda hp, qt, b: (b * nq + qt, hp)),
        scratch_shapes=[pltpu.VMEM((NAT_GROUP, NAT_TQ, NAT_KEYS), F32)],
        compiler_params=_params("arbitrary", "arbitrary", "arbitrary"),
        name="nat_latent_attention",
    )(z, z, z, ck, cv, diag)


DIFF_TQ = 1024


def _rope_tables():
    nf = HEAD_DIM // 4
    t = jnp.arange(DEC_SEQ)
    inv = ROPE_THETA ** (-jnp.arange(nf, dtype=F32) / nf)
    ang_r = (t // GRID_W).astype(F32)[:, None] * inv
    ang_c = (t % GRID_W).astype(F32)[:, None] * inv
    cos = jnp.concatenate([jnp.cos(ang_r)] * 2 + [jnp.cos(ang_c)] * 2, axis=-1)
    sin = jnp.concatenate([-jnp.sin(ang_r), jnp.sin(ang_r), -jnp.sin(ang_c), jnp.sin(ang_c)], axis=-1)
    return jnp.tile(cos, (1, 2)), jnp.tile(sin, (1, 2))


def _rope(x, cos, sin):
    nf = HEAD_DIM // 4
    lane = lax.broadcasted_iota(jnp.int32, x.shape, 1)
    upper = (lane // nf) % 2 == 1
    partner = jnp.where(upper, pltpu.roll(x, nf, axis=1), pltpu.roll(x, x.shape[1] - nf, axis=1))
    return x * cos + partner * sin


def _diff_latent_kernel(q_ref, k_ref, v_ref, ck_ref, cv_ref, cos_ref, sin_ref, lam_ref, subg_ref,
                        o_ref, kr_ref, *, lam_init):
    qt = pl.program_id(2)

    @pl.when(qt == 0)
    def _():
        kr_ref[...] = _rope(k_ref[...], cos_ref[...], sin_ref[...]).astype(BF16)

    lam = _diff_lambda(lam_ref, lam_init)
    row0 = pl.multiple_of(qt * DIFF_TQ, DIFF_TQ)
    q = _scaled_q(_rope(q_ref[...], cos_ref[pl.ds(row0, DIFF_TQ), :], sin_ref[pl.ds(row0, DIFF_TQ), :]))
    v_loc = _with_ones(v_ref[...].astype(BF16))
    v_ctx = _with_ones(cv_ref[0].astype(BF16))
    width = 2 * HEAD_DIM

    def softmax_av(j):
        sl = slice(j * HEAD_DIM, (j + 1) * HEAD_DIM)
        s_ctx = _dot_nt(q[:, sl], ck_ref[0, :, sl].astype(BF16))
        s_loc = _dot_nt(q[:, sl], kr_ref[:, sl])
        yield
        m = jnp.maximum(jnp.max(s_loc, axis=-1, keepdims=True), jnp.max(s_ctx, axis=-1, keepdims=True))
        yield
        e_loc = jnp.exp(s_loc - m)
        e_ctx = jnp.exp(s_ctx - m)
        yield
        av = _dot(e_loc.astype(BF16), v_loc) + _dot(e_ctx.astype(BF16), v_ctx)
        yield
        return av[:, :width] / av[:, width:width + 1]

    parts = _run_interleaved([softmax_av(j) for j in range(2)])
    o = parts[0] - lam * parts[1]
    o_ref[...] = _sub_norm(o, subg_ref[...], lam_init).astype(o_ref.dtype)


def _diff_latent_attention(z, ck, cv, lam_p, subln_g, lam_init):
    nq = DEC_SEQ // DIFF_TQ
    qblk0 = N_PROMPT // DIFF_TQ
    w = 2 * HEAD_DIM
    cos, sin = _rope_tables()
    return pl.pallas_call(
        functools.partial(_diff_latent_kernel, lam_init=lam_init),
        out_shape=jax.ShapeDtypeStruct((N_SAMPLE, DIFF_WIDTH), BF16),
        grid=(DEC_BATCH, DIFF_HEADS, nq),
        in_specs=[
            pl.BlockSpec((DIFF_TQ, w), lambda b, h, qt: (qblk0 + b * nq + qt, _DIFF_Q0 // w + h)),
            pl.BlockSpec((DEC_SEQ, w), lambda b, h, qt: (_SAMPLE_BLK0 + b, _DIFF_K0 // w + h)),
            pl.BlockSpec((DEC_SEQ, w), lambda b, h, qt: (_SAMPLE_BLK0 + b, _DIFF_V0 // w + h)),
            pl.BlockSpec((1, PAST_LEN, w), lambda b, h, qt: (b, 0, h)),
            pl.BlockSpec((1, PAST_LEN, w), lambda b, h, qt: (b, 0, h)),
            pl.BlockSpec((DEC_SEQ, w), lambda b, h, qt: (0, 0)),
            pl.BlockSpec((DEC_SEQ, w), lambda b, h, qt: (0, 0)),
            pl.BlockSpec((4, HEAD_DIM), lambda b, h, qt: (0, 0)),
            pl.BlockSpec((1, w), lambda b, h, qt: (0, 0)),
        ],
        out_specs=pl.BlockSpec((DIFF_TQ, w), lambda b, h, qt: (b * nq + qt, h)),
        scratch_shapes=[pltpu.VMEM((DEC_SEQ, w), BF16)],
        compiler_params=_params("arbitrary", "arbitrary", "arbitrary"),
        name="diff_latent_attention",
    )(z, z, z, ck, cv, cos, sin, lam_p, subln_g.reshape(1, w))


OUT_TM = 256


def _route(logits_t, rb_col):
    sc = jax.nn.sigmoid(logits_t)
    bi = sc + rb_col
    srow = [sc[e:e + 1] for e in range(N_EXPERTS)]
    brow = [bi[e:e + 1] for e in range(N_EXPERTS)]
    n = EXPERTS_PER_GROUP
    gscore = []
    for g in range(N_GROUPS):
        v = brow[g * n:(g + 1) * n]
        best = None
        for i in range(n):
            for j in range(i + 1, n):
                s = v[i] + v[j]
                best = s if best is None else jnp.maximum(best, s)
        gscore.append(best)
    sel = jnp.zeros_like(gscore[0], dtype=jnp.int32)
    best = gscore[0]
    for g in range(1, N_GROUPS):
        better = gscore[g] > best
        sel = jnp.where(better, g, sel)
        best = jnp.where(better, gscore[g], best)

    def pick_group(rows, i):
        out = rows[(N_GROUPS - 1) * n + i]
        for g in range(N_GROUPS - 2, -1, -1):
            out = jnp.where(sel == g, rows[g * n + i], out)
        return out

    bv = [pick_group(brow, i) for i in range(n)]
    sv = [pick_group(srow, i) for i in range(n)]
    i1 = jnp.zeros_like(sel)
    m1, w1 = bv[0], sv[0]
    for i in range(1, n):
        better = bv[i] > m1
        i1 = jnp.where(better, i, i1)
        m1 = jnp.where(better, bv[i], m1)
        w1 = jnp.where(better, sv[i], w1)
    i2 = jnp.zeros_like(sel)
    m2 = jnp.full_like(m1, -jnp.inf)
    w2 = jnp.zeros_like(w1)
    for i in range(n):
        better = (i1 != i) & (bv[i] > m2)
        i2 = jnp.where(better, i, i2)
        m2 = jnp.where(better, bv[i], m2)
        w2 = jnp.where(better, sv[i], w2)
    tot = w1 + w2
    return sel * n + i1, sel * n + i2, w1 / tot, w2 / tot


def _store_token_tiles(ref, value):
    rows = value.shape[0]
    chunks = jnp.stack([value[:, c * LANES:(c + 1) * LANES] for c in range(N_CHUNKS)], axis=0)
    tiles = jnp.transpose(chunks.reshape(N_CHUNKS, rows // SUBLANES, SUBLANES, LANES), (1, 2, 0, 3))
    ref[...] = tiles.reshape(rows, N_CHUNKS, LANES)


def _load_token_tiles(ref):
    rows = ref.shape[0]
    tiles = ref[...].reshape(rows // SUBLANES, SUBLANES, N_CHUNKS, LANES)
    chunks = jnp.transpose(tiles, (2, 0, 1, 3)).reshape(N_CHUNKS, rows, LANES)
    return jnp.concatenate([chunks[c] for c in range(N_CHUNKS)], axis=1)


def _deltanet_mixer_out(o_refs, tm):
    of = _stream_tile(o_refs[0:2], tm).astype(F32)
    ob = _stream_tile(o_refs[2:4], tm).astype(F32)
    gate_ref, g_ref = o_refs[4], o_refs[5]
    pieces = []
    for h in range(DN_HEADS):
        sl = slice(h * DN_DV, (h + 1) * DN_DV)
        o = of[:, sl] + ob[:, sl]
        o = o * lax.rsqrt(jnp.mean(o * o, axis=-1, keepdims=True) + EPS) * g_ref[...]
        pieces.append((o * _silu(gate_ref[:, sl].astype(F32))).astype(BF16))
    return jnp.concatenate(pieces, axis=1)


def _outproj_kernel(*refs, n_o, n_x, deltanet):
    o_refs, x_refs, refs = refs[:n_o], refs[n_o:n_o + n_x], refs[n_o + n_x:]
    mod_ref, w_ref, lng_ref, lnb_ref, rw_ref, rb_ref, x1_ref, h2_ref, ri_ref, rw_out_ref, cnt_ref = refs
    tm = x1_ref.shape[0]
    m = mod_ref[0]
    y = _dot(_deltanet_mixer_out(o_refs, tm) if deltanet else _stream_tile(o_refs, tm), w_ref[...])
    x1 = _layer_norm(ALPHA * _stream_tile(x_refs, tm) + m[2:3] * y, lng_ref[...], lnb_ref[...])
    x1_ref[...] = x1
    h2 = x1 * (1.0 + m[4:5]) + m[3:4]
    _store_token_tiles(h2_ref, h2)
    rw = rw_ref[...]
    h_hi, rw_hi = h2.astype(BF16), rw.astype(BF16)
    h_lo = (h2 - h_hi.astype(F32)).astype(BF16)
    rw_lo = (rw - rw_hi.astype(F32)).astype(BF16)
    logits = _dot(h_hi, rw_hi) + (_dot(h_lo, rw_hi) + _dot(h_hi, rw_lo))
    e1, e2, w1, w2 = _route(logits.T[:N_EXPERTS], rb_ref[...])
    eid = lax.broadcasted_iota(jnp.int32, (N_EXPERTS, tm), 0)
    onehot = (eid == e1) | (eid == e2)
    ti = lax.broadcasted_iota(jnp.int32, (tm, tm), 0)
    tj = lax.broadcasted_iota(jnp.int32, (tm, tm), 1)
    earlier = jnp.where(ti < tj, 1.0, 0.0).astype(BF16)
    rank = _dot(jnp.where(onehot, 1.0, 0.0).astype(BF16), earlier)
    r1 = jnp.sum(jnp.where(eid == e1, rank, 0.0), axis=0, keepdims=True).astype(jnp.int32)
    r2 = jnp.sum(jnp.where(eid == e2, rank, 0.0), axis=0, keepdims=True).astype(jnp.int32)
    ri_ref[...] = jnp.concatenate([e1, e2, r1, r2, jnp.zeros((4, tm), jnp.int32)], axis=0)
    wt = jnp.concatenate([w1, w2, jnp.zeros((LANES - 2, tm), F32)], axis=0)
    rw_out_ref[...] = wt.T
    cnt = jnp.sum(jnp.where(onehot, 1.0, 0.0), axis=1, keepdims=True)
    cnt_ref[0] = jnp.broadcast_to(cnt, (N_EXPERTS, LANES)).astype(jnp.int32)


def _output_projection(o_parts, x_parts, mod, w_bf16, ln_g, ln_b, router_w_pad, router_b, tm=OUT_TM,
                       deltanet=None):
    nt = N_TOK // tm
    if deltanet is None:
        o_specs = _stream_specs([a.shape[1] for a in o_parts], tm)
    else:
        z1, onorm_g = deltanet
        o_specs = (_stream_specs([DN_WIDTH, DN_WIDTH], tm) + _stream_specs([DN_WIDTH, DN_WIDTH], tm)
                   + [pl.BlockSpec((tm, DN_WIDTH), lambda i: (i, DN_GATE_COL0 // DN_WIDTH)),
                      pl.BlockSpec((1, DN_DV), lambda i: (0, 0))])
        o_parts = tuple(o_parts) + (z1, onorm_g.reshape(1, DN_DV))
    return pl.pallas_call(
        functools.partial(_outproj_kernel, n_o=len(o_parts), n_x=len(x_parts), deltanet=deltanet is not None),
        out_shape=(jax.ShapeDtypeStruct((N_TOK, D_MODEL), F32),
                   jax.ShapeDtypeStruct((N_TOK, N_CHUNKS, LANES), F32),
                   jax.ShapeDtypeStruct((8, N_TOK), jnp.int32),
                   jax.ShapeDtypeStruct((N_TOK, LANES), F32),
                   jax.ShapeDtypeStruct((nt, N_EXPERTS, LANES), jnp.int32)),
        grid=(nt,),
        in_specs=o_specs + _stream_specs([a.shape[1] for a in x_parts], tm) + [
            pl.BlockSpec((1, 6, D_MODEL), lambda i: (_cond_row(i * tm), 0, 0)),
            pl.BlockSpec((D_MODEL, D_MODEL), lambda i: (0, 0)),
            pl.BlockSpec((1, D_MODEL), lambda i: (0, 0)),
            pl.BlockSpec((1, D_MODEL), lambda i: (0, 0)),
            pl.BlockSpec((D_MODEL, LANES), lambda i: (0, 0)),
            pl.BlockSpec((N_EXPERTS, 1), lambda i: (0, 0)),
        ],
        out_specs=(pl.BlockSpec((tm, D_MODEL), lambda i: (i, 0)),
                   pl.BlockSpec((tm, N_CHUNKS, LANES), lambda i: (i, 0, 0)),
                   pl.BlockSpec((8, tm), lambda i: (0, i)),
                   pl.BlockSpec((tm, LANES), lambda i: (i, 0)),
                   pl.BlockSpec((1, N_EXPERTS, LANES), lambda i: (i, 0, 0))),
        compiler_params=_params("arbitrary"),
        name="output_projection",
    )(*o_parts, *x_parts, mod, w_bf16, ln_g.reshape(1, D_MODEL), ln_b.reshape(1, D_MODEL),
      router_w_pad, router_b.reshape(N_EXPERTS, 1))


N_ASSIGN = 2 * N_TOK
MOE_TM = 256
MOE_ROWS = N_ASSIGN + N_EXPERTS * MOE_TM
MOE_TILES = MOE_ROWS // MOE_TM
DISPATCH_TM = 1024
COMBINE_TM = 256


def _moe_plan(route_i, counts):
    cnt = counts[:, :, 0]
    total = jnp.sum(cnt, axis=0)
    padded = (total + MOE_TM - 1) // MOE_TM * MOE_TM
    seg_end = jnp.cumsum(padded)
    seg_start = seg_end - padded
    tile_base = seg_start[None, :] + jnp.cumsum(cnt, axis=0) - cnt
    base_tok = jnp.repeat(tile_base, OUT_TM, axis=0)
    eids = jnp.arange(N_EXPERTS, dtype=jnp.int32)[None, :]
    pos = [jnp.sum(jnp.where(route_i[k][:, None] == eids, base_tok, 0), axis=1) + route_i[2 + k]
           for k in range(2)]
    pos = jnp.concatenate(pos).astype(jnp.int32)
    tile_row0 = jnp.arange(MOE_TILES, dtype=jnp.int32) * MOE_TM
    tile_expert = jnp.minimum(jnp.sum(seg_end[None, :] <= tile_row0[:, None], axis=1), N_EXPERTS - 1)
    n_tiles = (seg_end[-1] // MOE_TM).reshape(1)
    last = MOE_ROWS - MOE_TM
    fill_rows = jnp.concatenate([jnp.minimum(seg_start + total, last),
                                 last - jnp.arange(N_EXPERTS, dtype=jnp.int32) * MOE_TM])
    return pos, tile_expert.astype(jnp.int32), n_tiles.astype(jnp.int32), fill_rows.astype(jnp.int32)


def _row_copy_wait(src_hbm, dst, sem, n_rows):
    pltpu.make_async_copy(src_hbm.at[pl.ds(0, n_rows)], dst.at[pl.ds(0, n_rows)], sem).wait()


def _dispatch_kernel(pos_ref, fill_ref, h_ref, xs_ref, zero_ref, sem, fill_sem):
    tm = h_ref.shape[0]
    t0 = pl.program_id(0) * tm

    @pl.when(pl.program_id(0) == 0)
    def _():
        zero_ref[...] = jnp.zeros_like(zero_ref)
        fills = [pltpu.make_async_copy(zero_ref, xs_ref.at[pl.ds(fill_ref[j], MOE_TM)], fill_sem)
                 for j in range(2 * N_EXPERTS)]
        for cp in fills[N_EXPERTS:]:
            cp.start()
        for cp in fills[N_EXPERTS:]:
            cp.wait()
        for cp in fills[:N_EXPERTS]:
            cp.start()
            cp.wait()

    def body(r, carry):
        for k in range(2):
            pltpu.make_async_copy(h_ref.at[r], xs_ref.at[pos_ref[k * N_TOK + t0 + r]], sem).start(priority=k)
        return carry

    lax.fori_loop(0, tm, body, 0, unroll=8)
    for _ in range(2):
        _row_copy_wait(h_ref, xs_ref, sem, tm)


def _moe_dispatch(pos, fill_rows, h2_tiles):
    tm = DISPATCH_TM
    return pl.pallas_call(
        _dispatch_kernel,
        out_shape=jax.ShapeDtypeStruct((MOE_ROWS, N_CHUNKS, LANES), F32),
        grid_spec=pltpu.PrefetchScalarGridSpec(
            num_scalar_prefetch=2,
            grid=(N_TOK // tm,),
            in_specs=[pl.BlockSpec((tm, N_CHUNKS, LANES), lambda i, pos, fill: (i, 0, 0))],
            out_specs=pl.BlockSpec(memory_space=pl.ANY),
            scratch_shapes=[pltpu.VMEM((MOE_TM, N_CHUNKS, LANES), F32), pltpu.SemaphoreType.DMA,
                            pltpu.SemaphoreType.DMA],
        ),
        compiler_params=_params("arbitrary"),
        name="moe_dispatch",
    )(pos, fill_rows, h2_tiles)


def _expert_kernel(te_ref, nt_ref, xs_ref, wg_ref, wu_ref, wd_ref, ys_ref, wg16_ref, wu16_ref, wd16_ref):
    i = pl.program_id(0)

    @pl.when(i < nt_ref[0])
    def _():
        @pl.when((i == 0) | (te_ref[i] != te_ref[jnp.maximum(i - 1, 0)]))
        def _():
            wg16_ref[...] = wg_ref[0, 0].astype(BF16)
            wu16_ref[...] = wu_ref[0, 0].astype(BF16)
            wd16_ref[...] = wd_ref[0, 0].astype(BF16)

        x = _load_token_tiles(xs_ref).astype(BF16)
        he = _silu(_dot(x, wg16_ref[...])) * _dot(x, wu16_ref[...])
        _store_token_tiles(ys_ref, _dot(he.astype(BF16), wd16_ref[...]))

    @pl.when(i >= nt_ref[0])
    def _():
        ys_ref[...] = jnp.zeros_like(ys_ref)


def _moe_experts(tile_expert, n_tiles, xs, layer, wg, wu, wd):
    row_blk = lambda i, te, nt: (jnp.minimum(i, nt[0] - 1), 0, 0)
    w_blk = lambda i, te, nt: (layer, te[i], 0, 0)
    return pl.pallas_call(
        _expert_kernel,
        out_shape=jax.ShapeDtypeStruct((MOE_ROWS, N_CHUNKS, LANES), F32),
        grid_spec=pltpu.PrefetchScalarGridSpec(
            num_scalar_prefetch=2,
            grid=(MOE_TILES,),
            in_specs=[
                pl.BlockSpec((MOE_TM, N_CHUNKS, LANES), row_blk),
                pl.BlockSpec((1, 1, D_MODEL, EXPERT_FF), w_blk),
                pl.BlockSpec((1, 1, D_MODEL, EXPERT_FF), w_blk),
                pl.BlockSpec((1, 1, EXPERT_FF, D_MODEL), w_blk),
            ],
            out_specs=pl.BlockSpec((MOE_TM, N_CHUNKS, LANES), lambda i, te, nt: (i, 0, 0)),
            scratch_shapes=[pltpu.VMEM((D_MODEL, EXPERT_FF), BF16), pltpu.VMEM((D_MODEL, EXPERT_FF), BF16),
                            pltpu.VMEM((EXPERT_FF, D_MODEL), BF16)],
        ),
        compiler_params=_params("arbitrary"),
        name="moe_experts",
    )(tile_expert, n_tiles, xs, wg, wu, wd)


def _combine_kernel(pos_ref, gw_ref, x1_ref, mod_ref, lng_ref, lnb_ref, ys_ref, *rest, split):
    out_refs, (buf_ref, sem) = rest[:-2], rest[-2:]
    i = pl.program_id(0)
    n = pl.num_programs(0)
    tm = COMBINE_TM

    def start_row(tile, to_slot, r):
        for k in range(2):
            p = pos_ref[k * N_TOK + tile * tm + r]
            pltpu.make_async_copy(ys_ref.at[p], buf_ref.at[to_slot, k, r], sem.at[to_slot]).start(priority=1)

    def wait_slot(s):
        for k in range(2):
            _row_copy_wait(ys_ref, buf_ref.at[s, k], sem.at[s], tm)

    def issue(tile, to_slot):
        def body(r, carry):
            start_row(tile, to_slot, r)
            return carry
        lax.fori_loop(0, tm, body, 0, unroll=8)

    @pl.when(i == 0)
    def _():
        issue(0, 0)

    @pl.when(i + 1 < n)
    def _():
        issue(i + 1, (i + 1) % 2)

    slot = i % 2
    wait_slot(slot)
    gw = gw_ref[...]
    f = gw[:, 0:1] * _load_token_tiles(buf_ref.at[slot, 0]) + gw[:, 1:2] * _load_token_tiles(buf_ref.at[slot, 1])
    m = mod_ref[0]
    out = _layer_norm(ALPHA * x1_ref[...] + m[5:6] * f, lng_ref[...], lnb_ref[...])

    if split:
        prompt_ref, sample_ref = out_refs

        @pl.when(i < N_PROMPT // tm)
        def _():
            prompt_ref[...] = out

        @pl.when(i >= N_PROMPT // tm)
        def _():
            sample_ref[...] = out
    else:
        out_refs[0][...] = out


def _moe_combine(pos, gate_w, x1, mod, ln_g, ln_b, ys, split):
    tm = COMBINE_TM
    np_blk = N_PROMPT // tm
    if split:
        out_shape = (jax.ShapeDtypeStruct((N_PROMPT, D_MODEL), F32), jax.ShapeDtypeStruct((N_SAMPLE, D_MODEL), F32))
        out_specs = (pl.BlockSpec((tm, D_MODEL), lambda i, pos: (jnp.minimum(i, np_blk - 1), 0)),
                     pl.BlockSpec((tm, D_MODEL), lambda i, pos: (jnp.maximum(i - np_blk, 0), 0)))
    else:
        out_shape = jax.ShapeDtypeStruct((N_TOK, D_MODEL), F32)
        out_specs = pl.BlockSpec((tm, D_MODEL), lambda i, pos: (i, 0))
    return pl.pallas_call(
        functools.partial(_combine_kernel, split=split),
        out_shape=out_shape,
        grid_spec=pltpu.PrefetchScalarGridSpec(
            num_scalar_prefetch=1,
            grid=(N_TOK // tm,),
            in_specs=[
                pl.BlockSpec((tm, LANES), lambda i, pos: (i, 0)),
                pl.BlockSpec((tm, D_MODEL), lambda i, pos: (i, 0)),
                pl.BlockSpec((1, 6, D_MODEL), lambda i, pos: (_cond_row(i * tm), 0, 0)),
                pl.BlockSpec((1, D_MODEL), lambda i, pos: (0, 0)),
                pl.BlockSpec((1, D_MODEL), lambda i, pos: (0, 0)),
                pl.BlockSpec(memory_space=pl.ANY),
            ],
            out_specs=out_specs,
            scratch_shapes=[pltpu.VMEM((2, 2, tm, N_CHUNKS, LANES), F32), pltpu.SemaphoreType.DMA((2,))],
        ),
        compiler_params=_params("arbitrary"),
        name="moe_combine",
    )(pos, gate_w, x1, mod, ln_g.reshape(1, D_MODEL), ln_b.reshape(1, D_MODEL), ys)


def _moe_ffn(h2_tiles, route_i, gate_w, counts, layer, wg, wu, wd, x1, mod, ln_g, ln_b, split=False):
    pos, tile_expert, n_tiles, fill_rows = _moe_plan(route_i, counts)
    xs = _moe_dispatch(pos, fill_rows, h2_tiles)
    ys = _moe_experts(tile_expert, n_tiles, xs, layer, wg, wu, wd)
    return _moe_combine(pos, gate_w, x1, mod, ln_g, ln_b, ys, split)


DN_GATE_COL0 = DN_CONV_CH
N_DIRS = 2


def _dn_prep_kernel(q_ref, k_ref, v_ref, wq_ref, wk_ref, wv_ref, qo_ref, ko_ref, vo_ref):
    length = q_ref.shape[0]
    row = lax.broadcasted_iota(jnp.int32, (length, DN_DK), 0)

    def conv(x, w):
        prev = jnp.where(row == 0, 0.0, pltpu.roll(x, 1, axis=0))
        nxt = jnp.where(row == length - 1, 0.0, pltpu.roll(x, length - 1, axis=0))
        return _silu(prev * w[0:1] + x * w[1:2] + nxt * w[2:3])

    def l2n(x):
        return x * lax.rsqrt(jnp.sum(x * x, axis=-1, keepdims=True) + 1e-6)

    for h in range(DN_HEADS):
        sl = slice(h * DN_DK, (h + 1) * DN_DK)
        q, k, v = (r[:, sl].astype(F32) for r in (q_ref, k_ref, v_ref))
        qo_ref[h] = (l2n(conv(q, wq_ref[:, sl])) * (DN_DK ** -0.5)).astype(qo_ref.dtype)
        ko_ref[h] = l2n(conv(k, wk_ref[:, sl])).astype(ko_ref.dtype)
        vo_ref[h] = conv(v, wv_ref[:, sl]).astype(vo_ref.dtype)


def _dn_prep(z1, conv_w, nb, length, row_blk0):
    shp = jax.ShapeDtypeStruct((DN_HEADS, nb * length, DN_DK), BF16)
    blk = lambda part: pl.BlockSpec((length, DN_QK_WIDTH), lambda b: (row_blk0 + b, part))
    wblk = lambda part: pl.BlockSpec((3, DN_QK_WIDTH), lambda b: (0, part))
    oblk = pl.BlockSpec((DN_HEADS, length, DN_DK), lambda b: (0, b, 0))
    return pl.pallas_call(
        _dn_prep_kernel,
        out_shape=(shp, shp, shp),
        grid=(nb,),
        in_specs=[blk(0), blk(1), blk(2), wblk(0), wblk(1), wblk(2)],
        out_specs=(oblk, oblk, oblk),
        compiler_params=_params("arbitrary"),
        name="deltanet_prep",
    )(z1, z1, z1, conv_w, conv_w, conv_w)


def _dn_gates_kernel(ab_ref, alog_ref, dtb_ref, o_ref):
    length = ab_ref.shape[0]
    x = ab_ref[...]
    a = x + dtb_ref[...]
    softplus = jnp.maximum(a, 0.0) + jnp.log1p(jnp.exp(-jnp.abs(a)))
    g = -jnp.exp(alog_ref[...]) * softplus
    beta = jax.nn.sigmoid(x)
    ri = lax.broadcasted_iota(jnp.int32, (DN_CHUNK, DN_CHUNK), 0)
    ci = lax.broadcasted_iota(jnp.int32, (DN_CHUNK, DN_CHUNK), 1)
    tril = (ri >= ci).astype(F32)
    triu = (ri <= ci).astype(F32)
    lane = lax.broadcasted_iota(jnp.int32, (DN_CHUNK, LANES), 1)
    for c in range(length // DN_CHUNK):
        rows = slice(c * DN_CHUNK, (c + 1) * DN_CHUNK)
        gch = g[rows]
        pre = jnp.dot(tril, gch, precision=lax.Precision.HIGHEST, preferred_element_type=F32)
        suf = jnp.dot(triu, gch, precision=lax.Precision.HIGHEST, preferred_element_type=F32)
        gc = jnp.where(lane < DN_HEADS, pre, suf)
        o_ref[rows, :] = jnp.where(lane < N_DIRS * DN_HEADS, gc, beta[rows])


def _dn_gates(z_ab, a_log, dt_bias, nb, length, row_blk0):
    pad = LANES - N_DIRS * DN_HEADS
    alog = jnp.pad(a_log.reshape(1, -1).astype(F32), ((0, 0), (0, pad)))
    dtb = jnp.pad(dt_bias.reshape(1, -1).astype(F32), ((0, 0), (0, pad)))
    return pl.pallas_call(
        _dn_gates_kernel,
        out_shape=jax.ShapeDtypeStruct((nb * length, LANES), F32),
        grid=(nb,),
        in_specs=[
            pl.BlockSpec((length, LANES), lambda b: (row_blk0 + b, 0)),
            pl.BlockSpec((1, LANES), lambda b: (0, 0)),
            pl.BlockSpec((1, LANES), lambda b: (0, 0)),
        ],
        out_specs=pl.BlockSpec((length, LANES), lambda b: (b, 0)),
        compiler_params=_params("arbitrary"),
        name="deltanet_gates",
    )(z_ab, alog, dtb)


def _delta_chunk(q, k, v, beta, gcol, grow, s, lower):
    ri = lax.broadcasted_iota(jnp.int32, (DN_CHUNK, DN_CHUNK), 0)
    ci = lax.broadcasted_iota(jnp.int32, (DN_CHUNK, DN_CHUNK), 1)
    incl = (ri >= ci) if lower else (ri <= ci)
    strict = (ri > ci) if lower else (ri < ci)
    decay = jnp.exp(jnp.where(incl, gcol - grow, -jnp.inf))
    kb = k * beta
    kq = _dot_nt(jnp.concatenate([kb, q], axis=0).astype(BF16), k.astype(BF16))
    kk, a_qk = kq[:DN_CHUNK], kq[DN_CHUNK:] * decay
    yield
    tri_l = jnp.where(strict, kk * decay, 0.0)

    def off_block(size):
        same = (ri // (2 * size)) == (ci // (2 * size))
        rpar, cpar = (ri // size) % 2, (ci // size) % 2
        return same & ((rpar == 1) & (cpar == 0) if lower else (rpar == 0) & (cpar == 1))

    p = jnp.where(ri == ci, 1.0, 0.0) - jnp.where(off_block(1), tri_l, 0.0)
    size = 2
    while size < DN_CHUNK:
        p16 = p.astype(BF16)
        pc = _dot(p16, jnp.where(off_block(size), tri_l, 0.0).astype(BF16))
        yield
        p = p - _dot(pc.astype(BF16), p16)
        yield
        size *= 2
    eg = jnp.exp(gcol)
    uw = _dot(p.astype(BF16), jnp.concatenate([v * beta, kb * eg], axis=1).astype(BF16))
    yield
    u, w = uw[:, :DN_DV], uw[:, DN_DV:]
    ws = _dot(jnp.concatenate([w, q * eg], axis=0).astype(BF16), s.astype(BF16))
    v_new, o_inter = u - ws[:DN_CHUNK], ws[DN_CHUNK:]
    yield
    v_new16 = v_new.astype(BF16)
    o = o_inter + _dot(a_qk.astype(BF16), v_new16)
    g_last = gcol[DN_CHUNK - 1:DN_CHUNK] if lower else gcol[0:1]
    kd = k * jnp.exp(g_last - gcol)
    s_new = s * jnp.exp(g_last) + _dot_tn(kd.astype(BF16), v_new16)
    return o, s_new


def _run_interleaved(chains):
    results = [None] * len(chains)
    active = list(enumerate(chains))
    while active:
        still = []
        for idx, gen in active:
            try:
                next(gen)
                still.append((idx, gen))
            except StopIteration as stop:
                results[idx] = stop.value
        active = still
    return results


def _dn_scan_kernel(*refs, has_init, want_final):
    (qf_ref, kf_ref, vf_ref, gf_ref, qb_ref, kb_ref, vb_ref, gb_ref), rest = refs[:8], refs[8:]
    if has_init:
        s0_ref, rest = rest[0], rest[1:]
    of_ref, ob_ref = rest[0], rest[1]
    rest = rest[2:]
    if want_final:
        sf_ref, rest = rest[0], rest[1:]
    s_ref = rest[0]
    c = pl.program_id(1)
    nc = pl.num_programs(1)
    per_request = N_DIRS * DN_HEADS
    states = [(u, d, h) for u in range(DN_SCAN_BATCH) for d in range(N_DIRS) for h in range(DN_HEADS)]

    @pl.when(c == 0)
    def _():
        for j, (u, d, h) in enumerate(states):
            s_ref[j] = s0_ref[u, d, h] if has_init else jnp.zeros((DN_DK, DN_DV), F32)

    dirs = ((qf_ref, kf_ref, vf_ref, gf_ref, of_ref, True), (qb_ref, kb_ref, vb_ref, gb_ref, ob_ref, False))
    chains = []
    for u in range(DN_SCAN_BATCH):
        for d, (q_ref, k_ref, v_ref, g_ref, o_ref, lower) in enumerate(dirs):
            gates = g_ref[u]
            gates_t = gates.T
            for h in range(DN_HEADS):
                j = d * DN_HEADS + h
                jb = per_request + j
                chains.append(_delta_chunk(q_ref[h, u].astype(F32), k_ref[h, u].astype(F32),
                                           v_ref[h, u].astype(F32), gates[:, jb:jb + 1], gates[:, j:j + 1],
                                           gates_t[j:j + 1, :], s_ref[u * per_request + j], lower))
    results = _run_interleaved(chains)
    for j, (u, d, h) in enumerate(states):
        o, s_new = results[j]
        dirs[d][4][u, :, h * DN_DV:(h + 1) * DN_DV] = o.astype(of_ref.dtype)
        s_ref[j] = s_new

    if want_final:
        @pl.when(c == nc - 1)
        def _():
            for j, (u, d, h) in enumerate(states):
                sf_ref[u, d, h] = s_ref[j]


def _dn_scan(qn, kn, vn, gates, s0, nb, length, want_final):
    nc = length // DN_CHUNK
    bb = DN_SCAN_BATCH
    has_init = s0 is not None
    qn, kn, vn = (t.reshape(DN_HEADS, nb, length, DN_DK) for t in (qn, kn, vn))
    gates = gates.reshape(nb, length, LANES)
    fwd = lambda b, c: (b, c, 0)
    bwd = lambda b, c: (b, nc - 1 - c, 0)
    hm = lambda im: pl.BlockSpec((DN_HEADS, bb, DN_CHUNK, DN_DK), lambda b, c: (0,) + im(b, c))
    gm = lambda im: pl.BlockSpec((bb, DN_CHUNK, LANES), im)
    in_specs = [hm(fwd), hm(fwd), hm(fwd), gm(fwd), hm(bwd), hm(bwd), hm(bwd), gm(bwd)]
    args = [qn, kn, vn, gates, qn, kn, vn, gates]
    state_blk = pl.BlockSpec((bb, N_DIRS, DN_HEADS, DN_DK, DN_DV), lambda b, c: (b, 0, 0, 0, 0))
    if has_init:
        in_specs.append(state_blk)
        args.append(s0)
    o_shape = jax.ShapeDtypeStruct((nb, length, DN_WIDTH), BF16)
    out_shape = [o_shape, o_shape]
    out_specs = [pl.BlockSpec((bb, DN_CHUNK, DN_WIDTH), fwd), pl.BlockSpec((bb, DN_CHUNK, DN_WIDTH), bwd)]
    if want_final:
        out_shape.append(jax.ShapeDtypeStruct((nb, N_DIRS, DN_HEADS, DN_DK, DN_DV), F32))
        out_specs.append(state_blk)
    outs = pl.pallas_call(
        functools.partial(_dn_scan_kernel, has_init=has_init, want_final=want_final),
        out_shape=tuple(out_shape),
        grid=(nb // bb, nc),
        in_specs=in_specs,
        out_specs=tuple(out_specs),
        scratch_shapes=[pltpu.VMEM((bb * N_DIRS * DN_HEADS, DN_DK, DN_DV), F32)],
        compiler_params=_params("arbitrary", "arbitrary"),
        name="deltanet_scan",
    )(*args)
    return [outs[0].reshape(nb * length, DN_WIDTH), outs[1].reshape(nb * length, DN_WIDTH)] + list(outs[2:])


DN_SCAN_BATCH = 2


def _deltanet_stream(z1, z_ab, conv_w, a_log, dt_bias, s0, nb, length, row0, want_final):
    row_blk0 = row0 // length
    qn, kn, vn = _dn_prep(z1, conv_w, nb, length, row_blk0)
    gates = _dn_gates(z_ab, a_log, dt_bias, nb, length, row_blk0)
    outs = _dn_scan(qn, kn, vn, gates, s0, nb, length, want_final)
    return outs[0], outs[1], (outs[2] if want_final else None)


def kernel(x_prompt, x_sample, cache_nat_k, cache_nat_v, cache_diff_k, cache_diff_v, state_delta, c, c_ctx, w_in_even, w_out_even, nat_bias, diff_lam, diff_subln, w_in_odd, conv_odd, a_log_odd, dt_bias_odd, onorm_odd, w_out_odd, ada_w, ada_b, ln_g, ln_b, router_w, router_b, moe_wg, moe_wu, moe_wd):
    x_parts = (x_prompt.reshape(N_PROMPT, D_MODEL), x_sample.reshape(N_SAMPLE, D_MODEL))
    cond =jnp.concatenate([c_ctx[None], c, jnp.zeros((N_COND - 1 - DEC_BATCH, D_MODEL), F32)], axis=0)
    mod = _ada_modulation(cond, ada_w, ada_b).reshape(DEPTH, N_COND, 6, D_MODEL)
    router_w_pad = jnp.pad(router_w, ((0, 0), (0, LANES - N_EXPERTS)))

    lam_init = 0.8 - 0.6 * math.exp(-0.3 * 0)
    z = _input_projection(x_parts, mod[0], w_in_even[0].astype(BF16))
    new_nat_k = z[:N_PROMPT, _NAT_K0:_NAT_V0].reshape(BATCH, 1, SEQ, NAT_HEADS, HEAD_DIM)
    new_nat_v = z[:N_PROMPT, _NAT_V0:_DIFF_Q0].reshape(BATCH, 1, SEQ, NAT_HEADS, HEAD_DIM)
    new_diff_k = z[:N_PROMPT, _DIFF_K0:_DIFF_V0].reshape(BATCH, 1, SEQ, DIFF_HEADS, 2, HEAD_DIM)
    new_diff_v = z[:N_PROMPT, _DIFF_V0:].reshape(BATCH, 1, SEQ, DIFF_HEADS, 2 * HEAD_DIM)
    o_ctx = _context_attention(z, diff_lam[0], diff_subln[0], lam_init)
    o_nat = _nat_latent_attention(z, cache_nat_k[:, 0].reshape(DEC_BATCH, PAST_LEN, NAT_WIDTH),
                                  cache_nat_v[:, 0].reshape(DEC_BATCH, PAST_LEN, NAT_WIDTH),
                                  _nat_bias_diagonals(nat_bias[0]))
    o_diff = _diff_latent_attention(z, cache_diff_k[:, 0].reshape(DEC_BATCH, PAST_LEN, DIFF_WIDTH),
                                    cache_diff_v[:, 0].reshape(DEC_BATCH, PAST_LEN, DIFF_WIDTH),
                                    diff_lam[0], diff_subln[0], lam_init)
    x1, *routed = _output_projection((o_ctx, o_nat, o_diff), x_parts, mod[0], w_out_even[0].astype(BF16),
                                     ln_g[0, 0], ln_b[0, 0], router_w_pad, router_b)
    x = _moe_ffn(*routed, 0, moe_wg, moe_wu, moe_wd, x1, mod[0], ln_g[0, 1], ln_b[0, 1])

    w1 = w_in_odd[0]
    w1 = jnp.concatenate([w1[:, :DN_CONV_CH], w1[:, DN_CONV_CH + 4 * DN_HEADS:],
                          w1[:, DN_CONV_CH:DN_CONV_CH + 4 * DN_HEADS],
                          jnp.zeros((D_MODEL, LANES - 4 * DN_HEADS), F32)], axis=1).astype(BF16)
    z1, z_ab = _input_projection((x,), mod[1], w1, f32_tail=LANES)
    of_p, ob_p, new_state = _deltanet_stream(z1, z_ab, conv_odd[0], a_log_odd[0], dt_bias_odd[0],
                                             None, BATCH, SEQ, 0, True)
    of_s, ob_s, _ = _deltanet_stream(z1, z_ab, conv_odd[0], a_log_odd[0], dt_bias_odd[0],
                                     state_delta[:, 0], DEC_BATCH, DEC_SEQ, N_PROMPT, False)
    x1, *routed = _output_projection((of_p, of_s, ob_p, ob_s), (x,), mod[1], w_out_odd[0].astype(BF16),
                                     ln_g[1, 0], ln_b[1, 0], router_w_pad, router_b,
                                     deltanet=(z1, onorm_odd[0]))
    y_prompt, y_sample = _moe_ffn(*routed, 1, moe_wg, moe_wu, moe_wd, x1, mod[1], ln_g[1, 1], ln_b[1, 1],
                                  split=True)
    y_prompt = y_prompt.reshape(BATCH, SEQ, D_MODEL)
    y_sample = y_sample.reshape(DEC_BATCH, DEC_SEQ, D_MODEL)
    return (y_prompt, y_sample, new_nat_k, new_nat_v, new_diff_k, new_diff_v, new_state[:, None])
```
